```python
import math
import jax
import jax.numpy as jnp
from jax import lax
import numpy as np

D_MODEL = 2048
BATCH = 8
SEQ = 4096
DEPTH = 4

CHUNK = 64
Q_BLOCK = 128
NORM_EPS = 1e-6
ROPE_THETA = 10000.0

FOX_HEADS = 6
FOX_DH = 128
FOX_W = FOX_HEADS * FOX_DH
FORGET_BIAS_CENTER = 3.0

MLA_HEADS = 6
MLA_NOPE = 128
MLA_ROPE = 64
MLA_V = 128
MLA_Q_LORA = 512
MLA_KV_LORA = 256
MLA_W = MLA_HEADS * MLA_V

RET_HEADS = 4
RET_DK = 128
RET_DV = 256
RET_QK_W = RET_HEADS * RET_DK
RET_V_W = RET_HEADS * RET_DV

N_BRANCH = 3

D_FF = 5632
CONV_W = 3

IN_SPLITS = (FOX_W, FOX_W, FOX_W, FOX_HEADS,
             MLA_Q_LORA, MLA_KV_LORA, MLA_ROPE,
             RET_QK_W, RET_QK_W, RET_V_W, RET_V_W,
             N_BRANCH * D_MODEL)
IN_WIDTH = sum(IN_SPLITS)

kernel_name = 'hybrid_fox_mla_retention_convffn'


def rms_norm(x, g):
    xf = x.astype(jnp.float32)
    y = xf * lax.rsqrt(jnp.mean(xf * xf, axis=-1, keepdims=True) + NORM_EPS)
    return (y * g.astype(jnp.float32)).astype(x.dtype)


def apply_rope(x):
    s, d = x.shape[1], x.shape[-1]
    pos = jnp.arange(s, dtype=jnp.float32)
    inv_freq = ROPE_THETA ** (-jnp.arange(0, d, 2, dtype=jnp.float32) / d)
    ang = pos[:, None] * inv_freq[None, :]
    cos = jnp.cos(ang)[None, :, None, :]
    sin = jnp.sin(ang)[None, :, None, :]
    xf = x.astype(jnp.float32)
    x1, x2 = xf[..., : d // 2], xf[..., d // 2:]
    return jnp.concatenate([x1 * cos - x2 * sin, x2 * cos + x1 * sin], axis=-1).astype(x.dtype)


def block_attention(q, k, v, scale, frame_causal, log_decay_cum):
    s_len = q.shape[2]
    outs = []
    for s0 in range(0, s_len, Q_BLOCK):
        s1 = s0 + Q_BLOCK
        logits = jnp.einsum('bhqd,bhkd->bhqk', q[:, :, s0:s1], k[:, :, :s1]).astype(jnp.float32) * scale
        if log_decay_cum is not None:
            logits = logits + log_decay_cum[:, :, s0:s1, None] - log_decay_cum[:, :, None, :s1]
        q_pos = jnp.arange(s0, s1)
        k_pos = jnp.arange(s1)
        if frame_causal:
            mask = k_pos[None, :] <= q_pos[:, None]
        else:
            mask = (k_pos // CHUNK)[None, :] <= (q_pos // CHUNK)[:, None]
        p = jax.nn.softmax(jnp.where(mask, logits, -jnp.inf), axis=-1).astype(v.dtype)
        outs.append(jnp.einsum('bhqk,bhkd->bhqd', p, v[:, :, :s1]))
    return jnp.concatenate(outs, axis=2)


def fox_mixer(q, k, v, f_logit, b_f):
    b, s = q.shape[:2]
    def heads(t):
        return t.reshape(b, s, FOX_HEADS, FOX_DH).transpose(0, 2, 1, 3)
    log_f = jax.nn.log_sigmoid(f_logit.astype(jnp.float32) + b_f.astype(jnp.float32))
    c = jnp.cumsum(log_f, axis=1).transpose(0, 2, 1)
    o = block_attention(heads(q), heads(k), heads(v), FOX_DH ** -0.5, True, c)
    return o.transpose(0, 2, 1, 3).reshape(b, s, FOX_W)


def mla_mixer(c_q, c_kv, k_rope, q_norm_g, kv_norm_g, w_uq, w_ukv):
    b, s = c_q.shape[:2]
    q = (rms_norm(c_q, q_norm_g) @ w_uq).reshape(b, s, MLA_HEADS, MLA_NOPE + MLA_ROPE)
    q = jnp.concatenate([q[..., :MLA_NOPE], apply_rope(q[..., MLA_NOPE:])], axis=-1)
    kv = (rms_norm(c_kv, kv_norm_g) @ w_ukv).reshape(b, s, MLA_HEADS, MLA_NOPE + MLA_V)
    k_nope, v = kv[..., :MLA_NOPE], kv[..., MLA_NOPE:]
    k_r = apply_rope(k_rope[:, :, None, :])
    k = jnp.concatenate([k_nope, jnp.broadcast_to(k_r, (b, s, MLA_HEADS, MLA_ROPE))], axis=-1)
    o = block_attention(q.transpose(0, 2, 1, 3), k.transpose(0, 2, 1, 3), v.transpose(0, 2, 1, 3),
                        (MLA_NOPE + MLA_ROPE) ** -0.5, False, None)
    return o.transpose(0, 2, 1, 3).reshape(b, s, MLA_W)


def retention_mixer(q, k, v, g):
    b, s = q.shape[:2]
    n = s // CHUNK
    dt = v.dtype
    q = apply_rope(q.reshape(b, s, RET_HEADS, RET_DK))
    k = apply_rope(k.reshape(b, s, RET_HEADS, RET_DK)) * (RET_DK ** -0.5)
    qc = q.reshape(b, n, CHUNK, RET_HEADS, RET_DK)
    kc = k.reshape(b, n, CHUNK, RET_HEADS, RET_DK)
    vc = v.reshape(b, n, CHUNK, RET_HEADS, RET_DV)
    log_gamma = jnp.log(1.0 - 2.0 ** (-5.0 - jnp.arange(RET_HEADS, dtype=jnp.float32)))
    idx = jnp.arange(CHUNK, dtype=jnp.float32)
    intra_decay = jnp.exp(log_gamma[:, None, None] * jnp.abs(idx[:, None] - idx[None, :]))
    state_in = jnp.exp(log_gamma[:, None] * (CHUNK - 1 - idx)[None, :])
    cross_decay = jnp.exp(log_gamma[:, None] * (idx + 1.0)[None, :])
    chunk_decay = jnp.exp(log_gamma * CHUNK)
    scores = jnp.einsum('bnjhd,bnlhd->bnhjl', qc, kc) * intra_decay.astype(dt)
    intra = jnp.einsum('bnhjl,bnlhe->bnjhe', scores, vc)
    kv_chunk = jnp.einsum('bnlhd,hl,bnlhe->nbhde', kc, state_in.astype(dt), vc).astype(jnp.float32)
    def step(state, kv_n):
        return chunk_decay[None, :, None, None] * state + kv_n, state
    _, prev = lax.scan(step, jnp.zeros((b, RET_HEADS, RET_DK, RET_DV), jnp.float32), kv_chunk)
    cross = jnp.einsum('bnjhd,nbhde->bnjhe', qc, prev.astype(dt)) * cross_decay.T[None, None, :, :, None].astype(dt)
    o = (intra + cross).reshape(b, s, RET_HEADS, RET_DV).astype(jnp.float32)
    o = o * lax.rsqrt(jnp.mean(o * o, axis=-1, keepdims=True) + NORM_EPS)
    return o.reshape(b, s, RET_V_W).astype(dt) * jax.nn.silu(g)


def conv_ffn(h, w_up, w_gate, conv_w, conv_b, w_down):
    s = h.shape[1]
    u = h @ w_up
    u_pad = jnp.pad(u, ((0, 0), (CONV_W - 1, 0), (0, 0)))
    u_conv = conv_b + sum(conv_w[i] * u_pad[:, i:i + s] for i in range(CONV_W))
    return (jax.nn.gelu(u_conv) * (h @ w_gate)) @ w_down


def _normal(key, shape, fan_in):
    return jax.random.normal(key, shape, jnp.float32) * (fan_in ** -0.5)


def _gain(key, shape):
    return 1.0 + 0.02 * jax.random.normal(key, shape, jnp.float32)


def _fwd_setup_inputs(seed: int = 0) -> dict:
    key = jax.random.key(seed)
    ks = jax.random.split(key, 20)
    L, D = DEPTH, D_MODEL
    return {
        'x': jax.random.normal(ks[0], (BATCH, SEQ, D), jnp.float32),
        'norm1_g': _gain(ks[1], (L, D)),
        'w_in': _normal(ks[2], (L, D, IN_WIDTH), D),
        'mla_q_norm_g': _gain(ks[3], (L, MLA_Q_LORA)),
        'mla_kv_norm_g': _gain(ks[4], (L, MLA_KV_LORA)),
        'mla_w_uq': _normal(ks[5], (L, MLA_Q_LORA, MLA_HEADS * (MLA_NOPE + MLA_ROPE)), MLA_Q_LORA),
        'mla_w_ukv': _normal(ks[6], (L, MLA_KV_LORA, MLA_HEADS * (MLA_NOPE + MLA_V)), MLA_KV_LORA),
        'fox_b_f': FORGET_BIAS_CENTER + 0.1 * jax.random.normal(ks[7], (L, FOX_HEADS), jnp.float32),
        'w_br_fox': _normal(ks[8], (L, FOX_W, D), FOX_W),
        'w_br_mla': _normal(ks[9], (L, MLA_W, D), MLA_W),
        'w_br_ret': _normal(ks[10], (L, RET_V_W, D), RET_V_W),
        'w_out': _normal(ks[11], (L, D, D), D),
        'norm2_g': _gain(ks[12], (L, D)),
        'ffn_w_up': _normal(ks[13], (L, D, D_FF), D),
        'ffn_w_gate': _normal(ks[14], (L, D, D_FF), D),
        'ffn_conv_w': _normal(ks[15], (L, CONV_W, D_FF), CONV_W),
        'ffn_conv_b': 0.02 * jax.random.normal(ks[16], (L, D_FF), jnp.float32),
        'ffn_w_down': _normal(ks[17], (L, D_FF, D), D_FF),
        'final_norm_g': _gain(ks[18], (D,)),
    }


def _fwd_reference(x, norm1_g, w_in, mla_q_norm_g, mla_kv_norm_g, mla_w_uq, mla_w_ukv, fox_b_f,
              w_br_fox, w_br_mla, w_br_ret, w_out, norm2_g, ffn_w_up, ffn_w_gate,
              ffn_conv_w, ffn_conv_b, ffn_w_down, final_norm_g):
    b, s, d = x.shape
    split_points = [int(p) for p in np.cumsum(IN_SPLITS)[:-1]]
    for i in range(DEPTH):
        h = rms_norm(x, norm1_g[i])
        (fq, fk, fv, ff, mq, mkv, mkr, rq, rk, rv, rg, gates) = jnp.split(h @ w_in[i], split_points, axis=-1)
        a = fox_mixer(fq, fk, fv, ff, fox_b_f[i])
        bm = mla_mixer(mq, mkv, mkr, mla_q_norm_g[i], mla_kv_norm_g[i], mla_w_uq[i], mla_w_ukv[i])
        c = retention_mixer(rq, rk, rv, rg)
        g = jax.nn.sigmoid(gates.astype(jnp.float32)).astype(x.dtype).reshape(b, s, N_BRANCH, d)
        merged = (g[:, :, 0] * (a @ w_br_fox[i])
                  + g[:, :, 1] * (bm @ w_br_mla[i])
                  + g[:, :, 2] * (c @ w_br_ret[i]))
        x = x + merged @ w_out[i]
        x = x + conv_ffn(rms_norm(x, norm2_g[i]), ffn_w_up[i], ffn_w_gate[i],
                         ffn_conv_w[i], ffn_conv_b[i], ffn_w_down[i])
    return rms_norm(x, final_norm_g)


import jax as _jax
import jax.numpy as _jnp

TWIN_FORMAT = 'train_step'
FWD_PARAMS = ['x', 'norm1_g', 'w_in', 'mla_q_norm_g', 'mla_kv_norm_g', 'mla_w_uq', 'mla_w_ukv', 'fox_b_f', 'w_br_fox', 'w_br_mla', 'w_br_ret', 'w_out', 'norm2_g', 'ffn_w_up', 'ffn_w_gate', 'ffn_conv_w', 'ffn_conv_b', 'ffn_w_down', 'final_norm_g']
TWIN_WEIGHTS = ['norm1_g', 'w_in', 'mla_q_norm_g', 'mla_kv_norm_g', 'mla_w_uq', 'mla_w_ukv', 'fox_b_f', 'w_br_fox', 'w_br_mla', 'w_br_ret', 'w_out', 'norm2_g', 'ffn_w_up', 'ffn_w_gate', 'ffn_conv_w', 'ffn_conv_b', 'ffn_w_down', 'final_norm_g']
TWIN_DIFF_INPUT = 'x'
TWIN_INPUTS = ['x', 'norm1_g', 'w_in', 'mla_q_norm_g', 'mla_kv_norm_g', 'mla_w_uq', 'mla_w_ukv', 'fox_b_f', 'w_br_fox', 'w_br_mla', 'w_br_ret', 'w_out', 'norm2_g', 'ffn_w_up', 'ffn_w_gate', 'ffn_conv_w', 'ffn_conv_b', 'ffn_w_down', 'final_norm_g', 'loss_target', 'm_norm1_g', 'm_w_in', 'm_mla_q_norm_g', 'm_mla_kv_norm_g', 'm_mla_w_uq', 'm_mla_w_ukv', 'm_fox_b_f', 'm_w_br_fox', 'm_w_br_mla', 'm_w_br_ret', 'm_w_out', 'm_norm2_g', 'm_ffn_w_up', 'm_ffn_w_gate', 'm_ffn_conv_w', 'm_ffn_conv_b', 'm_ffn_w_down', 'm_final_norm_g', 'v_norm1_g', 'v_w_in', 'v_mla_q_norm_g', 'v_mla_kv_norm_g', 'v_mla_w_uq', 'v_mla_w_ukv', 'v_fox_b_f', 'v_w_br_fox', 'v_w_br_mla', 'v_w_br_ret', 'v_w_out', 'v_norm2_g', 'v_ffn_w_up', 'v_ffn_w_gate', 'v_ffn_conv_w', 'v_ffn_conv_b', 'v_ffn_w_down', 'v_final_norm_g']
TWIN_OUTPUTS = ['loss', 'grad_x', 'grad_norm1_g', 'grad_w_in', 'grad_mla_q_norm_g', 'grad_mla_kv_norm_g', 'grad_mla_w_uq', 'grad_mla_w_ukv', 'grad_fox_b_f', 'grad_w_br_fox', 'grad_w_br_mla', 'grad_w_br_ret', 'grad_w_out', 'grad_norm2_g', 'grad_ffn_w_up', 'grad_ffn_w_gate', 'grad_ffn_conv_w', 'grad_ffn_conv_b', 'grad_ffn_w_down', 'grad_final_norm_g', 'delta_norm1_g', 'delta_w_in', 'delta_mla_q_norm_g', 'delta_mla_kv_norm_g', 'delta_mla_w_uq', 'delta_mla_w_ukv', 'delta_fox_b_f', 'delta_w_br_fox', 'delta_w_br_mla', 'delta_w_br_ret', 'delta_w_out', 'delta_norm2_g', 'delta_ffn_w_up', 'delta_ffn_w_gate', 'delta_ffn_conv_w', 'delta_ffn_conv_b', 'delta_ffn_w_down', 'delta_final_norm_g', 'new_m_norm1_g', 'new_m_w_in', 'new_m_mla_q_norm_g', 'new_m_mla_kv_norm_g', 'new_m_mla_w_uq', 'new_m_mla_w_ukv', 'new_m_fox_b_f', 'new_m_w_br_fox', 'new_m_w_br_mla', 'new_m_w_br_ret', 'new_m_w_out', 'new_m_norm2_g', 'new_m_ffn_w_up', 'new_m_ffn_w_gate', 'new_m_ffn_conv_w', 'new_m_ffn_conv_b', 'new_m_ffn_w_down', 'new_m_final_norm_g', 'new_v_norm1_g', 'new_v_w_in', 'new_v_mla_q_norm_g', 'new_v_mla_kv_norm_g', 'new_v_mla_w_uq', 'new_v_mla_w_ukv', 'new_v_fox_b_f', 'new_v_w_br_fox', 'new_v_w_br_mla', 'new_v_w_br_ret', 'new_v_w_out', 'new_v_norm2_g', 'new_v_ffn_w_up', 'new_v_ffn_w_gate', 'new_v_ffn_conv_w', 'new_v_ffn_conv_b', 'new_v_ffn_w_down', 'new_v_final_norm_g']
TWIN_LEAF_KINDS = {'loss': 'loss', 'grad_x': 'grad_x', 'grad_norm1_g': 'grad_w', 'grad_w_in': 'grad_w', 'grad_mla_q_norm_g': 'grad_w', 'grad_mla_kv_norm_g': 'grad_w', 'grad_mla_w_uq': 'grad_w', 'grad_mla_w_ukv': 'grad_w', 'grad_fox_b_f': 'grad_w', 'grad_w_br_fox': 'grad_w', 'grad_w_br_mla': 'grad_w', 'grad_w_br_ret': 'grad_w', 'grad_w_out': 'grad_w', 'grad_norm2_g': 'grad_w', 'grad_ffn_w_up': 'grad_w', 'grad_ffn_w_gate': 'grad_w', 'grad_ffn_conv_w': 'grad_w', 'grad_ffn_conv_b': 'grad_w', 'grad_ffn_w_down': 'grad_w', 'grad_final_norm_g': 'grad_w', 'delta_norm1_g': 'delta_w', 'delta_w_in': 'delta_w', 'delta_mla_q_norm_g': 'delta_w', 'delta_mla_kv_norm_g': 'delta_w', 'delta_mla_w_uq': 'delta_w', 'delta_mla_w_ukv': 'delta_w', 'delta_fox_b_f': 'delta_w', 'delta_w_br_fox': 'delta_w', 'delta_w_br_mla': 'delta_w', 'delta_w_br_ret': 'delta_w', 'delta_w_out': 'delta_w', 'delta_norm2_g': 'delta_w', 'delta_ffn_w_up': 'delta_w', 'delta_ffn_w_gate': 'delta_w', 'delta_ffn_conv_w': 'delta_w', 'delta_ffn_conv_b': 'delta_w', 'delta_ffn_w_down': 'delta_w', 'delta_final_norm_g': 'delta_w', 'new_m_norm1_g': 'new_m', 'new_m_w_in': 'new_m', 'new_m_mla_q_norm_g': 'new_m', 'new_m_mla_kv_norm_g': 'new_m', 'new_m_mla_w_uq': 'new_m', 'new_m_mla_w_ukv': 'new_m', 'new_m_fox_b_f': 'new_m', 'new_m_w_br_fox': 'new_m', 'new_m_w_br_mla': 'new_m', 'new_m_w_br_ret': 'new_m', 'new_m_w_out': 'new_m', 'new_m_norm2_g': 'new_m', 'new_m_ffn_w_up': 'new_m', 'new_m_ffn_w_gate': 'new_m', 'new_m_ffn_conv_w': 'new_m', 'new_m_ffn_conv_b': 'new_m', 'new_m_ffn_w_down': 'new_m', 'new_m_final_norm_g': 'new_m', 'new_v_norm1_g': 'new_v', 'new_v_w_in': 'new_v', 'new_v_mla_q_norm_g': 'new_v', 'new_v_mla_kv_norm_g': 'new_v', 'new_v_mla_w_uq': 'new_v', 'new_v_mla_w_ukv': 'new_v', 'new_v_fox_b_f': 'new_v', 'new_v_w_br_fox': 'new_v', 'new_v_w_br_mla': 'new_v', 'new_v_w_br_ret': 'new_v', 'new_v_w_out': 'new_v', 'new_v_norm2_g': 'new_v', 'new_v_ffn_w_up': 'new_v', 'new_v_ffn_w_gate': 'new_v', 'new_v_ffn_conv_w': 'new_v', 'new_v_ffn_conv_b': 'new_v', 'new_v_ffn_w_down': 'new_v', 'new_v_final_norm_g': 'new_v'}


def _forward(args):
    return _fwd_reference(*[args[k] for k in FWD_PARAMS])


def _output_shape():
    out = _jax.eval_shape(lambda: _forward(_fwd_setup_inputs(0)))
    return out.shape, out.dtype

N_MICROBATCH = 1
ADAM_LR = 0.001
ADAM_B1 = 0.9
ADAM_B2 = 0.999
ADAM_EPS = 1e-08
ADAM_WD = 0.01
ADAM_STEP = 10
PER_EXAMPLE_BATCH_AXIS = {'x': 0, 'loss_target': 0}
SHARED_INPUTS = []
_WEIGHT_DTYPES = {'norm1_g': _jnp.float32, 'w_in': _jnp.float32, 'mla_q_norm_g': _jnp.float32, 'mla_kv_norm_g': _jnp.float32, 'mla_w_uq': _jnp.float32, 'mla_w_ukv': _jnp.float32, 'fox_b_f': _jnp.float32, 'w_br_fox': _jnp.float32, 'w_br_mla': _jnp.float32, 'w_br_ret': _jnp.float32, 'w_out': _jnp.float32, 'norm2_g': _jnp.float32, 'ffn_w_up': _jnp.float32, 'ffn_w_gate': _jnp.float32, 'ffn_conv_w': _jnp.float32, 'ffn_conv_b': _jnp.float32, 'ffn_w_down': _jnp.float32, 'final_norm_g': _jnp.float32}
MOMENT_SCALE = {'norm1_g': 7.575798e-02, 'w_in': 3.044798e-02, 'mla_q_norm_g': 1.639279e-02, 'mla_kv_norm_g': 3.403204e-02, 'mla_w_uq': 1.082635e-02, 'mla_w_ukv': 1.334547e-02, 'fox_b_f': 1.381467e-01, 'w_br_fox': 1.802061e-02, 'w_br_mla': 9.333897e-03, 'w_br_ret': 3.301108e-02, 'w_out': 3.860352e-02, 'norm2_g': 7.637844e-02, 'ffn_w_up': 3.306519e-02, 'ffn_w_gate': 3.204095e-02, 'ffn_conv_w': 3.338023e-02, 'ffn_conv_b': 3.157884e-02, 'ffn_w_down': 5.312227e-02, 'final_norm_g': 1.598052e+01}


def _to_microbatches(a, axis):
    t = _jnp.moveaxis(a, axis, 0)
    t = t.reshape((N_MICROBATCH, t.shape[0] // N_MICROBATCH) + t.shape[1:])
    return _jnp.moveaxis(t, 1, axis + 1)


def setup_inputs(seed: int = 0) -> dict:
    inp = _fwd_setup_inputs(seed)
    key = _jax.random.fold_in(_jax.random.key(seed), 7919)
    shape, _ = _output_shape()
    out = dict(inp)
    out["loss_target"] = _jax.random.normal(_jax.random.fold_in(key, 0), shape, _jnp.float32)
    for i, name in enumerate(TWIN_WEIGHTS):
        w = inp[name].astype(_jnp.float32)
        if MOMENT_SCALE is None:
            s = _jnp.sqrt(_jnp.mean(_jnp.square(w)) + 1e-30)
        else:
            s = MOMENT_SCALE[name]
        km, kv = _jax.random.split(_jax.random.fold_in(key, i + 1))
        out[name] = w
        out["m_" + name] = s * _jax.random.normal(km, w.shape, _jnp.float32)
        out["v_" + name] = (s * s) * _jax.random.uniform(kv, w.shape, _jnp.float32, 0.5, 1.5)
    if N_MICROBATCH > 1:
        for name, axis in PER_EXAMPLE_BATCH_AXIS.items():
            out[name] = _to_microbatches(out[name], axis)
    return {'x': out['x'], 'norm1_g': out['norm1_g'], 'w_in': out['w_in'], 'mla_q_norm_g': out['mla_q_norm_g'], 'mla_kv_norm_g': out['mla_kv_norm_g'], 'mla_w_uq': out['mla_w_uq'], 'mla_w_ukv': out['mla_w_ukv'], 'fox_b_f': out['fox_b_f'], 'w_br_fox': out['w_br_fox'], 'w_br_mla': out['w_br_mla'], 'w_br_ret': out['w_br_ret'], 'w_out': out['w_out'], 'norm2_g': out['norm2_g'], 'ffn_w_up': out['ffn_w_up'], 'ffn_w_gate': out['ffn_w_gate'], 'ffn_conv_w': out['ffn_conv_w'], 'ffn_conv_b': out['ffn_conv_b'], 'ffn_w_down': out['ffn_w_down'], 'final_norm_g': out['final_norm_g'], 'loss_target': out['loss_target'], 'm_norm1_g': out['m_norm1_g'], 'm_w_in': out['m_w_in'], 'm_mla_q_norm_g': out['m_mla_q_norm_g'], 'm_mla_kv_norm_g': out['m_mla_kv_norm_g'], 'm_mla_w_uq': out['m_mla_w_uq'], 'm_mla_w_ukv': out['m_mla_w_ukv'], 'm_fox_b_f': out['m_fox_b_f'], 'm_w_br_fox': out['m_w_br_fox'], 'm_w_br_mla': out['m_w_br_mla'], 'm_w_br_ret': out['m_w_br_ret'], 'm_w_out': out['m_w_out'], 'm_norm2_g': out['m_norm2_g'], 'm_ffn_w_up': out['m_ffn_w_up'], 'm_ffn_w_gate': out['m_ffn_w_gate'], 'm_ffn_conv_w': out['m_ffn_conv_w'], 'm_ffn_conv_b': out['m_ffn_conv_b'], 'm_ffn_w_down': out['m_ffn_w_down'], 'm_final_norm_g': out['m_final_norm_g'], 'v_norm1_g': out['v_norm1_g'], 'v_w_in': out['v_w_in'], 'v_mla_q_norm_g': out['v_mla_q_norm_g'], 'v_mla_kv_norm_g': out['v_mla_kv_norm_g'], 'v_mla_w_uq': out['v_mla_w_uq'], 'v_mla_w_ukv': out['v_mla_w_ukv'], 'v_fox_b_f': out['v_fox_b_f'], 'v_w_br_fox': out['v_w_br_fox'], 'v_w_br_mla': out['v_w_br_mla'], 'v_w_br_ret': out['v_w_br_ret'], 'v_w_out': out['v_w_out'], 'v_norm2_g': out['v_norm2_g'], 'v_ffn_w_up': out['v_ffn_w_up'], 'v_ffn_w_gate': out['v_ffn_w_gate'], 'v_ffn_conv_w': out['v_ffn_conv_w'], 'v_ffn_conv_b': out['v_ffn_conv_b'], 'v_ffn_w_down': out['v_ffn_w_down'], 'v_final_norm_g': out['v_final_norm_g']}


def _loss(weights, diff, rest, loss_target):
    with _jax.named_scope("forward"):
        args = {**rest, TWIN_DIFF_INPUT: diff, **{k: w.astype(_WEIGHT_DTYPES[k]) for k, w in weights.items()}}
        y = _forward(args)
    with _jax.named_scope("loss_head"):
        err = _jnp.square(y.astype(_jnp.float32) - loss_target)
        return 0.5 * _jnp.sum(_jnp.mean(err, axis=-1)) if err.ndim else 0.5 * err


def _adamw(w, g, m, v):
    m = ADAM_B1 * m + (1.0 - ADAM_B1) * g
    v = ADAM_B2 * v + (1.0 - ADAM_B2) * _jnp.square(g)
    m_hat = m / (1.0 - ADAM_B1 ** ADAM_STEP)
    v_hat = v / (1.0 - ADAM_B2 ** ADAM_STEP)
    delta = -ADAM_LR * (m_hat / (_jnp.sqrt(v_hat) + ADAM_EPS) + ADAM_WD * w)
    return delta, m, v


def reference(x, norm1_g, w_in, mla_q_norm_g, mla_kv_norm_g, mla_w_uq, mla_w_ukv, fox_b_f, w_br_fox, w_br_mla, w_br_ret, w_out, norm2_g, ffn_w_up, ffn_w_gate, ffn_conv_w, ffn_conv_b, ffn_w_down, final_norm_g, loss_target, m_norm1_g, m_w_in, m_mla_q_norm_g, m_mla_kv_norm_g, m_mla_w_uq, m_mla_w_ukv, m_fox_b_f, m_w_br_fox, m_w_br_mla, m_w_br_ret, m_w_out, m_norm2_g, m_ffn_w_up, m_ffn_w_gate, m_ffn_conv_w, m_ffn_conv_b, m_ffn_w_down, m_final_norm_g, v_norm1_g, v_w_in, v_mla_q_norm_g, v_mla_kv_norm_g, v_mla_w_uq, v_mla_w_ukv, v_fox_b_f, v_w_br_fox, v_w_br_mla, v_w_br_ret, v_w_out, v_norm2_g, v_ffn_w_up, v_ffn_w_gate, v_ffn_conv_w, v_ffn_conv_b, v_ffn_w_down, v_final_norm_g):
    given = dict(x=x, norm1_g=norm1_g, w_in=w_in, mla_q_norm_g=mla_q_norm_g, mla_kv_norm_g=mla_kv_norm_g, mla_w_uq=mla_w_uq, mla_w_ukv=mla_w_ukv, fox_b_f=fox_b_f, w_br_fox=w_br_fox, w_br_mla=w_br_mla, w_br_ret=w_br_ret, w_out=w_out, norm2_g=norm2_g, ffn_w_up=ffn_w_up, ffn_w_gate=ffn_w_gate, ffn_conv_w=ffn_conv_w, ffn_conv_b=ffn_conv_b, ffn_w_down=ffn_w_down, final_norm_g=final_norm_g, loss_target=loss_target, m_norm1_g=m_norm1_g, m_w_in=m_w_in, m_mla_q_norm_g=m_mla_q_norm_g, m_mla_kv_norm_g=m_mla_kv_norm_g, m_mla_w_uq=m_mla_w_uq, m_mla_w_ukv=m_mla_w_ukv, m_fox_b_f=m_fox_b_f, m_w_br_fox=m_w_br_fox, m_w_br_mla=m_w_br_mla, m_w_br_ret=m_w_br_ret, m_w_out=m_w_out, m_norm2_g=m_norm2_g, m_ffn_w_up=m_ffn_w_up, m_ffn_w_gate=m_ffn_w_gate, m_ffn_conv_w=m_ffn_conv_w, m_ffn_conv_b=m_ffn_conv_b, m_ffn_w_down=m_ffn_w_down, m_final_norm_g=m_final_norm_g, v_norm1_g=v_norm1_g, v_w_in=v_w_in, v_mla_q_norm_g=v_mla_q_norm_g, v_mla_kv_norm_g=v_mla_kv_norm_g, v_mla_w_uq=v_mla_w_uq, v_mla_w_ukv=v_mla_w_ukv, v_fox_b_f=v_fox_b_f, v_w_br_fox=v_w_br_fox, v_w_br_mla=v_w_br_mla, v_w_br_ret=v_w_br_ret, v_w_out=v_w_out, v_norm2_g=v_norm2_g, v_ffn_w_up=v_ffn_w_up, v_ffn_w_gate=v_ffn_w_gate, v_ffn_conv_w=v_ffn_conv_w, v_ffn_conv_b=v_ffn_conv_b, v_ffn_w_down=v_ffn_w_down, v_final_norm_g=v_final_norm_g)
    weights = {n: given[n] for n in TWIN_WEIGHTS}
    shared = {n: given[n] for n in SHARED_INPUTS}
    per_example = {n: given[n] for n in ['x']}
    grad_fn = _jax.value_and_grad(_loss, argnums=(0, 1))

    def one_microbatch(ex, loss_target):
        ex = dict(ex)
        diff = ex.pop(TWIN_DIFF_INPUT)
        return grad_fn(weights, diff, {**shared, **ex}, loss_target)

    if N_MICROBATCH == 1:
        loss, (grad_w, grad_x) = one_microbatch(per_example, given["loss_target"])
    else:
        def body(carry, xs):
            loss_sum, grad_sum = carry
            l_k, (gw_k, gx_k) = one_microbatch(xs[0], xs[1])
            with _jax.named_scope("update"):
                return (loss_sum + l_k, _jax.tree.map(_jnp.add, grad_sum, gw_k)), gx_k

        init = (_jnp.zeros((), _jnp.float32), _jax.tree.map(_jnp.zeros_like, weights))
        (loss, grad_w), grad_x = _jax.lax.scan(body, init, (per_example, given["loss_target"]))
    with _jax.named_scope("update"):
        delta_w, new_m, new_v = {}, {}, {}
        for n in TWIN_WEIGHTS:
            delta_w[n], new_m[n], new_v[n] = _adamw(weights[n], grad_w[n], given["m_" + n], given["v_" + n])
    return (loss, grad_x, *[grad_w[n] for n in TWIN_WEIGHTS], *[delta_w[n] for n in TWIN_WEIGHTS],
            *[new_m[n] for n in TWIN_WEIGHTS], *[new_v[n] for n in TWIN_WEIGHTS])
```

```python
import functools
import math

import numpy as np
import jax
import jax.numpy as jnp
from jax import lax
from jax.experimental import pallas as pl
from jax.experimental.pallas import tpu as pltpu

F32 = jnp.float32
BF16 = jnp.bfloat16

CHUNK = 64
NORM_EPS = 1e-6
ROPE_THETA = 10000.0
FOX_HEADS, FOX_DH = 6, 128
FOX_W = FOX_HEADS * FOX_DH
MLA_HEADS, MLA_NOPE, MLA_ROPE, MLA_V = 6, 128, 64, 128
MLA_Q_LORA, MLA_KV_LORA = 512, 256
MLA_W = MLA_HEADS * MLA_V
RET_HEADS, RET_DK, RET_DV = 4, 128, 256
RET_QK_W, RET_V_W = RET_HEADS * RET_DK, RET_HEADS * RET_DV
ADAM_LR, ADAM_B1, ADAM_B2, ADAM_EPS, ADAM_WD, ADAM_STEP = 0.001, 0.9, 0.999, 1e-08, 0.01, 10

N_DEV = 8
LANES = 128
V7X_VMEM_LIMIT_BYTES = 52 * 1024 * 1024
NEG_BIG = -1e30
HIGHEST = lax.Precision.HIGHEST

NT_DIMS = (((1,), (1,)), ((), ()))
TN_DIMS = (((0,), (0,)), ((), ()))
NN_DIMS = (((1,), (0,)), ((), ()))


def _pick(n, cap, mult=LANES):
    best = None
    for t in range(mult, min(n, cap) + 1, mult):
        if n % t == 0:
            best = t
    return n if best is None else best


def _cparams(*sem):
    return pltpu.CompilerParams(dimension_semantics=sem, vmem_limit_bytes=V7X_VMEM_LIMIT_BYTES)


class InLayout:
    def __init__(self, d_model):
        d = d_model
        self.d = d
        orig = dict(fq=(0, FOX_W), fk=(FOX_W, FOX_W), fv=(2 * FOX_W, FOX_W), ff=(3 * FOX_W, FOX_HEADS))
        o = 3 * FOX_W + FOX_HEADS
        for name, w in (("mq", MLA_Q_LORA), ("mkv", MLA_KV_LORA), ("mkr", MLA_ROPE), ("rq", RET_QK_W),
                        ("rk", RET_QK_W), ("rv", RET_V_W), ("rg", RET_V_W), ("gates", 3 * d)):
            orig[name] = (o, w)
            o += w
        self.orig = orig
        self.orig_width = o
        order = ["gates", "rv", "rg", "mq", "rq", "rk", "mkv", "fq", "fk", "fv", "mkr", "ff"]
        self.order = order
        self.off, self.width = {}, {}
        p = 0
        for name in order:
            w = orig[name][1]
            wp = -(-w // LANES) * LANES
            self.off[name], self.width[name] = p, wp
            p += wp
        self.total = p

    def cb(self, name, block):
        assert self.off[name] % block == 0, (name, block)
        return self.off[name] // block

    def permute(self, w):
        parts = []
        for name in self.order:
            o, n = self.orig[name]
            seg = w[..., o:o + n]
            pad = self.width[name] - n
            if pad:
                seg = jnp.pad(seg, [(0, 0)] * (w.ndim - 1) + [(0, pad)])
            parts.append(seg)
        return jnp.concatenate(parts, axis=-1)

    def unpermute(self, w):
        names = sorted(self.orig, key=lambda n: self.orig[n][0])
        return jnp.concatenate([w[..., self.off[n]:self.off[n] + self.orig[n][1]] for n in names], axis=-1)


def _mm(name, a, b, out_shape, grid, a_spec, b_spec, o_spec, dims, acc_shape, res=None):
    nk = grid[-1]
    has_res = res is not None

    def body(*refs):
        if has_res:
            a_ref, b_ref, r_ref, o_ref = refs[:4]
        else:
            a_ref, b_ref, o_ref = refs[:3]
            r_ref = None
        prod = lax.dot_general(a_ref[...].astype(BF16), b_ref[...].astype(BF16), dims,
                               preferred_element_type=F32)
        if nk == 1:
            if has_res:
                prod = prod + r_ref[...].astype(F32)
            o_ref[...] = prod.astype(o_ref.dtype)
        else:
            acc_ref = refs[-1]
            k = pl.program_id(len(grid) - 1)

            @pl.when(k == 0)
            def _():
                acc_ref[...] = prod

            @pl.when(k > 0)
            def _():
                acc_ref[...] += prod

            @pl.when(k == nk - 1)
            def _():
                r = acc_ref[...]
                if has_res:
                    r = r + r_ref[...].astype(F32)
                o_ref[...] = r.astype(o_ref.dtype)

    in_specs = [a_spec, b_spec] + ([o_spec] if has_res else [])
    args = (a, b) + ((res,) if has_res else ())
    scratch = [pltpu.VMEM(acc_shape, F32)] if nk > 1 else []
    sem = ("parallel",) * (len(grid) - 1) + ("arbitrary",)
    return pl.pallas_call(body, out_shape=out_shape, grid=grid, in_specs=in_specs, out_specs=o_spec,
                          scratch_shapes=scratch, name=name, compiler_params=_cparams(*sem))(*args)


def mm_nn(name, a, b, out_dtype, b_lead=None, res=None):
    M, K = a.shape
    N = b.shape[-1]
    tm, tn, tk = _pick(M, 1024, 8), _pick(N, 1024), _pick(K, 2048)
    grid = (M // tm, N // tn, K // tk)
    a_spec = pl.BlockSpec((tm, tk), lambda i, j, k: (i, k))
    if b_lead is None:
        b_spec = pl.BlockSpec((tk, tn), lambda i, j, k: (k, j))
    else:
        b_spec = pl.BlockSpec((None, tk, tn), lambda i, j, k: (b_lead, k, j))
    o_spec = pl.BlockSpec((tm, tn), lambda i, j, k: (i, j))
    return _mm(name, a, b, jax.ShapeDtypeStruct((M, N), out_dtype), grid, a_spec, b_spec, o_spec,
               NN_DIMS, (tm, tn), res)


def mm_nt(name, a, b, out_dtype, b_lead=None, res=None):
    M, N = a.shape
    K = b.shape[-2]
    tm, tko, tk = _pick(M, 1024, 8), _pick(K, 1024), _pick(N, 2048)
    grid = (M // tm, K // tko, N // tk)
    a_spec = pl.BlockSpec((tm, tk), lambda i, j, k: (i, k))
    if b_lead is None:
        b_spec = pl.BlockSpec((tko, tk), lambda i, j, k: (j, k))
    else:
        b_spec = pl.BlockSpec((None, tko, tk), lambda i, j, k: (b_lead, j, k))
    o_spec = pl.BlockSpec((tm, tko), lambda i, j, k: (i, j))
    return _mm(name, a, b, jax.ShapeDtypeStruct((M, K), out_dtype), grid, a_spec, b_spec, o_spec,
               NT_DIMS, (tm, tko), res)


def mm_tn(name, a, b, out_dtype):
    M, K = a.shape
    N = b.shape[-1]
    cap = 4096 if (a.dtype == BF16 and b.dtype == BF16) else 2048
    tko, tn, tk = _pick(K, 1024), _pick(N, 1024), _pick(M, cap, 8)
    grid = (K // tko, N // tn, M // tk)
    a_spec = pl.BlockSpec((tk, tko), lambda i, j, k: (k, i))
    b_spec = pl.BlockSpec((tk, tn), lambda i, j, k: (k, j))
    o_spec = pl.BlockSpec((tko, tn), lambda i, j, k: (i, j))
    return _mm(name, a, b, jax.ShapeDtypeStruct((K, N), out_dtype), grid, a_spec, b_spec, o_spec,
               TN_DIMS, (tko, tn))


def ffn_up(name, h, w, l, out_dtype):
    M, D = h.shape
    f = w.shape[-1]
    tm = _pick(M, 1024, 8)
    grid = (M // tm, N_DEV, 1)
    return _mm(name, h, w, jax.ShapeDtypeStruct((N_DEV, M, f), out_dtype), grid,
               pl.BlockSpec((tm, D), lambda i, j, k: (i, 0)),
               pl.BlockSpec((None, None, D, f), lambda i, j, k: (l, j, 0, 0)),
               pl.BlockSpec((None, tm, f), lambda i, j, k: (j, i, 0)), NN_DIMS, (tm, f))


def ffn_down(name, act, w, l, res, out_dtype):
    _, M, f = act.shape
    D = w.shape[-1]
    tm, tn = _pick(M, 1024, 8), _pick(D, 1024)
    grid = (M // tm, D // tn, N_DEV)
    return _mm(name, act, w, jax.ShapeDtypeStruct((M, D), out_dtype), grid,
               pl.BlockSpec((None, tm, f), lambda i, j, k: (k, i, 0)),
               pl.BlockSpec((None, None, f, tn), lambda i, j, k: (l, k, 0, j)),
               pl.BlockSpec((tm, tn), lambda i, j, k: (i, j)), NN_DIMS, (tm, tn), res)


def ffn_down_bwd_act(name, dy, w, l, out_dtype):
    M, D = dy.shape
    f = w.shape[-2]
    tm = _pick(M, 1024, 8)
    grid = (M // tm, N_DEV, 1)
    return _mm(name, dy, w, jax.ShapeDtypeStruct((N_DEV, M, f), out_dtype), grid,
               pl.BlockSpec((tm, D), lambda i, j, k: (i, 0)),
               pl.BlockSpec((None, None, f, D), lambda i, j, k: (l, j, 0, 0)),
               pl.BlockSpec((None, tm, f), lambda i, j, k: (j, i, 0)), NT_DIMS, (tm, f))


def ffn_down_bwd_w(name, act, dy, out_dtype):
    _, M, f = act.shape
    D = dy.shape[-1]
    tn, tk = _pick(D, 1024), _pick(M, 4096 if dy.dtype == BF16 else 2048, 8)
    grid = (N_DEV, D // tn, M // tk)
    return _mm(name, act, dy, jax.ShapeDtypeStruct((N_DEV, f, D), out_dtype), grid,
               pl.BlockSpec((None, tk, f), lambda j, n, k: (j, k, 0)),
               pl.BlockSpec((tk, tn), lambda j, n, k: (k, n)),
               pl.BlockSpec((None, f, tn), lambda j, n, k: (j, 0, n)), TN_DIMS, (f, tn))


def ffn_up_bwd_h(name, du, w, l, res, out_dtype):
    _, M, f = du.shape
    D = w.shape[-2]
    tm, tn = _pick(M, 1024, 8), _pick(D, 1024)
    grid = (M // tm, D // tn, N_DEV)
    return _mm(name, du, w, jax.ShapeDtypeStruct((M, D), out_dtype), grid,
               pl.BlockSpec((None, tm, f), lambda i, j, k: (k, i, 0)),
               pl.BlockSpec((None, None, tn, f), lambda i, j, k: (l, k, j, 0)),
               pl.BlockSpec((tm, tn), lambda i, j, k: (i, j)), NT_DIMS, (tm, tn), res)


def ffn_up_bwd_w(name, h, du, out_dtype):
    M, D = h.shape
    f = du.shape[-1]
    tko, tk = _pick(D, 1024), _pick(M, 4096, 8)
    grid = (N_DEV, D // tko, M // tk)
    return _mm(name, h, du, jax.ShapeDtypeStruct((N_DEV, D, f), out_dtype), grid,
               pl.BlockSpec((tk, tko), lambda j, n, k: (k, n)),
               pl.BlockSpec((None, tk, f), lambda j, n, k: (j, k, 0)),
               pl.BlockSpec((None, tko, f), lambda j, n, k: (j, n, 0)), TN_DIMS, (tko, f))


def _rms(xf, g):
    return xf * lax.rsqrt(jnp.mean(xf * xf, axis=-1, keepdims=True) + NORM_EPS) * g


def rms_fwd(name, x, cb, W, g, out_dtype):
    S = x.shape[0]
    tr = _pick(S, 256, 8)

    def body(x_ref, g_ref, o_ref):
        o_ref[...] = _rms(x_ref[...].astype(F32), g_ref[...]).astype(o_ref.dtype)

    return pl.pallas_call(
        body, out_shape=jax.ShapeDtypeStruct((S, W), out_dtype), grid=(S // tr,),
        in_specs=[pl.BlockSpec((tr, W), lambda i: (i, cb)), pl.BlockSpec((1, W), lambda i: (0, 0))],
        out_specs=pl.BlockSpec((tr, W), lambda i: (i, 0)), name=name, compiler_params=_cparams("parallel"))(x, g)


def rms_bwd(name, x, cb, W, g, dy, out_dtype, res=None):
    S = x.shape[0]
    tr = _pick(S, 256, 8)
    has_res = res is not None

    def body(*refs):
        if has_res:
            x_ref, g_ref, dy_ref, r_ref, dx_ref, dg_ref = refs
        else:
            x_ref, g_ref, dy_ref, dx_ref, dg_ref = refs
        _, vjp = jax.vjp(_rms, x_ref[...].astype(F32), g_ref[...])
        dx, dg = vjp(dy_ref[...].astype(F32))
        if has_res:
            dx = dx + r_ref[...]
        dx_ref[...] = dx.astype(dx_ref.dtype)

        @pl.when(pl.program_id(0) == 0)
        def _():
            dg_ref[...] = jnp.zeros_like(dg_ref)

        dg_ref[...] += dg

    row = pl.BlockSpec((tr, W), lambda i: (i, 0))
    vec = pl.BlockSpec((1, W), lambda i: (0, 0))
    in_specs = [pl.BlockSpec((tr, W), lambda i: (i, cb)), vec, row] + ([row] if has_res else [])
    args = (x, g, dy) + ((res,) if has_res else ())
    return pl.pallas_call(
        body, out_shape=(jax.ShapeDtypeStruct((S, W), out_dtype), jax.ShapeDtypeStruct((1, W), F32)),
        grid=(S // tr,), in_specs=in_specs, out_specs=(row, vec), name=name,
        compiler_params=_cparams("arbitrary"))(*args)


def rope_tables(S, d):
    pos = jnp.arange(S, dtype=F32)
    inv_freq = ROPE_THETA ** (-jnp.arange(0, d, 2, dtype=F32) / d)
    ang = pos[:, None] * inv_freq[None, :]
    cos, sin = jnp.cos(ang), jnp.sin(ang)
    half = d // 2
    z = jnp.zeros((S, LANES - d), F32)
    zh = jnp.zeros((S, half), F32)
    c = jnp.concatenate([cos, cos, z], axis=1)
    sa = jnp.concatenate([-sin, zh, z], axis=1)
    sb = jnp.concatenate([zh, sin, z], axis=1)
    return c, sa, sb, half


def rope_apply(name, x, cb, H, tabs, scale, out_dtype, transpose=False):
    c, sa, sb, half = tabs
    S = x.shape[0]
    tr = _pick(S, 512, 8)
    up, down = LANES - half, half

    def body(x_ref, c_ref, sa_ref, sb_ref, o_ref):
        xv = x_ref[...].astype(F32)
        if not transpose:
            y = xv * c_ref[...] + pltpu.roll(xv, up, 1) * sa_ref[...] + pltpu.roll(xv, down, 1) * sb_ref[...]
            y = y * scale
        else:
            xv = xv * scale
            y = (xv * c_ref[...] + pltpu.roll(xv * sa_ref[...], down, 1)
                 + pltpu.roll(xv * sb_ref[...], up, 1))
        o_ref[...] = y.astype(o_ref.dtype)

    tab = pl.BlockSpec((tr, LANES), lambda h, i: (i, 0))
    return pl.pallas_call(
        body, out_shape=jax.ShapeDtypeStruct((S, H * LANES), out_dtype), grid=(H, S // tr),
        in_specs=[pl.BlockSpec((tr, LANES), lambda h, i: (i, cb + h)), tab, tab, tab],
        out_specs=pl.BlockSpec((tr, LANES), lambda h, i: (i, h)), name=name,
        compiler_params=_cparams("parallel", "parallel"))(x, c, sa, sb)


def _ret_out(o, g):
    y = o * lax.rsqrt(jnp.mean(o * o, axis=-1, keepdims=True) + NORM_EPS)
    return y * jax.nn.silu(g)


def ret_out_fwd(name, o, gsrc, g_cb, out_dtype):
    S = o.shape[0]
    tr = _pick(S, 512, 8)
    W = RET_DV

    def body(o_ref, g_ref, y_ref):
        y_ref[...] = _ret_out(o_ref[...], g_ref[...].astype(F32)).astype(y_ref.dtype)

    blk = pl.BlockSpec((tr, W), lambda h, i: (i, h))
    return pl.pallas_call(
        body, out_shape=jax.ShapeDtypeStruct((S, RET_HEADS * W), out_dtype), grid=(RET_HEADS, S // tr),
        in_specs=[blk, pl.BlockSpec((tr, W), lambda h, i: (i, g_cb + h))], out_specs=blk, name=name,
        compiler_params=_cparams("parallel", "parallel"))(o, gsrc)


def ret_out_bwd(name, o, gsrc, g_cb, dy):
    S = o.shape[0]
    tr = _pick(S, 512, 8)
    W = RET_DV

    def body(o_ref, g_ref, dy_ref, do_ref, dg_ref):
        _, vjp = jax.vjp(_ret_out, o_ref[...], g_ref[...].astype(F32))
        do, dg = vjp(dy_ref[...].astype(F32))
        do_ref[...] = do.astype(do_ref.dtype)
        dg_ref[...] = dg.astype(dg_ref.dtype)

    blk = pl.BlockSpec((tr, W), lambda h, i: (i, h))
    return pl.pallas_call(
        body, out_shape=(jax.ShapeDtypeStruct((S, RET_HEADS * W), F32),
                         jax.ShapeDtypeStruct((S, RET_HEADS * W), BF16)),
        grid=(RET_HEADS, S // tr),
        in_specs=[blk, pl.BlockSpec((tr, W), lambda h, i: (i, g_cb + h)), blk], out_specs=(blk, blk),
        name=name, compiler_params=_cparams("parallel", "parallel"))(o, gsrc, dy)


def _merge(g0, g1, g2, a, b, c):
    return jax.nn.sigmoid(g0) * a + jax.nn.sigmoid(g1) * b + jax.nn.sigmoid(g2) * c


def merge_fwd(name, P, gates_cb, a, b, c, out_dtype):
    S, D = a.shape
    tr = _pick(S, 128, 8)

    def body(g0, g1, g2, a_ref, b_ref, c_ref, o_ref):
        o_ref[...] = _merge(g0[...], g1[...], g2[...], a_ref[...], b_ref[...], c_ref[...]).astype(o_ref.dtype)

    row = pl.BlockSpec((tr, D), lambda i: (i, 0))
    gs = [pl.BlockSpec((tr, D), lambda i, k=k: (i, gates_cb + k)) for k in range(3)]
    return pl.pallas_call(
        body, out_shape=jax.ShapeDtypeStruct((S, D), out_dtype), grid=(S // tr,),
        in_specs=gs + [row, row, row], out_specs=row, name=name,
        compiler_params=_cparams("parallel"))(P, P, P, a, b, c)


def merge_bwd(name, P, gates_cb, a, b, c, dm):
    S, D = a.shape
    tr = _pick(S, 128, 8)

    def body(g0, g1, g2, a_ref, b_ref, c_ref, dm_ref, dg_ref, da_ref, db_ref, dc_ref):
        _, vjp = jax.vjp(_merge, g0[...], g1[...], g2[...], a_ref[...], b_ref[...], c_ref[...])
        d0, d1, d2, da, db, dc = vjp(dm_ref[...].astype(F32))
        dg_ref[:, 0:D] = d0.astype(dg_ref.dtype)
        dg_ref[:, D:2 * D] = d1.astype(dg_ref.dtype)
        dg_ref[:, 2 * D:3 * D] = d2.astype(dg_ref.dtype)
        da_ref[...] = da.astype(da_ref.dtype)
        db_ref[...] = db.astype(db_ref.dtype)
        dc_ref[...] = dc.astype(dc_ref.dtype)

    row = pl.BlockSpec((tr, D), lambda i: (i, 0))
    gs = [pl.BlockSpec((tr, D), lambda i, k=k: (i, gates_cb + k)) for k in range(3)]
    bf = jax.ShapeDtypeStruct((S, D), BF16)
    return pl.pallas_call(
        body, out_shape=(jax.ShapeDtypeStruct((S, 3 * D), BF16), bf, bf, bf), grid=(S // tr,),
        in_specs=gs + [row, row, row, row],
        out_specs=(pl.BlockSpec((tr, 3 * D), lambda i: (i, 0)), row, row, row), name=name,
        compiler_params=_cparams("parallel"))(P, P, P, a, b, c, dm)


HALO = 8


def _conv_pre(u_ref, uh_ref, ext_ref, cw_ref, cb_ref, tr):
    i = pl.program_id(1)
    u = u_ref[...]
    ext_ref[0:HALO, :] = jnp.where(i > 0, uh_ref[...], 0.0)
    ext_ref[HALO:HALO + tr, :] = u
    u1 = ext_ref[HALO - 1:HALO - 1 + tr, :]
    u2 = ext_ref[HALO - 2:HALO - 2 + tr, :]
    cw = cw_ref[...]
    uc = cb_ref[...] + ((cw[0:1, :] * u2 + cw[1:2, :] * u1) + cw[2:3, :] * u)
    return u, u1, u2, uc


def _ffn_specs(S, f, tr, l):
    nb = tr // HALO
    row = pl.BlockSpec((None, tr, f), lambda j, i: (j, i, 0))
    prev = pl.BlockSpec((None, HALO, f), lambda j, i: (j, jnp.maximum(i * nb - 1, 0), 0))
    nxt = pl.BlockSpec((None, HALO, f), lambda j, i: (j, jnp.minimum((i + 1) * nb, S // HALO - 1), 0))
    cw = pl.BlockSpec((None, None, 3, f), lambda j, i: (l, j, 0, 0))
    cb = pl.BlockSpec((None, None, 1, f), lambda j, i: (l, j, 0, 0))
    return row, prev, nxt, cw, cb


def ffn_act_fwd(name, u, gt, cw, cb, l, out_dtype):
    _, S, f = u.shape
    tr = _pick(S, 512, 8)
    row, prev, _, cws, cbs = _ffn_specs(S, f, tr, l)

    def body(u_ref, uh_ref, gt_ref, cw_ref, cb_ref, o_ref, ext_ref):
        _, _, _, uc = _conv_pre(u_ref, uh_ref, ext_ref, cw_ref, cb_ref, tr)
        o_ref[...] = (jax.nn.gelu(uc) * gt_ref[...]).astype(o_ref.dtype)

    return pl.pallas_call(
        body, out_shape=jax.ShapeDtypeStruct((N_DEV, S, f), out_dtype), grid=(N_DEV, S // tr),
        in_specs=[row, prev, row, cws, cbs], out_specs=row,
        scratch_shapes=[pltpu.VMEM((tr + HALO, f), F32)], name=name,
        compiler_params=_cparams("parallel", "parallel"))(u, u, gt, cw, cb)


def ffn_act_bwd_point(name, u, gt, cw, cb, l, dact):
    _, S, f = u.shape
    tr = _pick(S, 512, 8)
    row, prev, _, cws, cbs = _ffn_specs(S, f, tr, l)

    def body(u_ref, uh_ref, gt_ref, cw_ref, cb_ref, da_ref, g_ref, dgt_ref, ext_ref):
        _, _, _, uc = _conv_pre(u_ref, uh_ref, ext_ref, cw_ref, cb_ref, tr)
        _, vjp = jax.vjp(lambda c, t: jax.nn.gelu(c) * t, uc, gt_ref[...])
        g, dgt = vjp(da_ref[...].astype(F32))
        g_ref[...] = g
        dgt_ref[...] = dgt.astype(dgt_ref.dtype)

    return pl.pallas_call(
        body, out_shape=(jax.ShapeDtypeStruct((N_DEV, S, f), F32), jax.ShapeDtypeStruct((N_DEV, S, f), BF16)),
        grid=(N_DEV, S // tr), in_specs=[row, prev, row, cws, cbs, row], out_specs=(row, row),
        scratch_shapes=[pltpu.VMEM((tr + HALO, f), F32)], name=name,
        compiler_params=_cparams("parallel", "parallel"))(u, u, gt, cw, cb, dact)


def ffn_act_bwd_conv(name, u, g, cw, l):
    _, S, f = u.shape
    tr = _pick(S, 512, 8)
    nt = S // tr
    row, prev, nxt, cws, _ = _ffn_specs(S, f, tr, l)

    def body(u_ref, uh_ref, g_ref, gn_ref, cw_ref, du_ref, dcw_ref, dcb_ref, ext_ref, gext_ref):
        i = pl.program_id(1)
        u = u_ref[...]
        ext_ref[0:HALO, :] = jnp.where(i > 0, uh_ref[...], 0.0)
        ext_ref[HALO:HALO + tr, :] = u
        u1 = ext_ref[HALO - 1:HALO - 1 + tr, :]
        u2 = ext_ref[HALO - 2:HALO - 2 + tr, :]
        g = g_ref[...]
        gext_ref[0:tr, :] = g
        gext_ref[tr:tr + HALO, :] = jnp.where(i < nt - 1, gn_ref[...], 0.0)
        g1 = gext_ref[1:1 + tr, :]
        g2 = gext_ref[2:2 + tr, :]
        cw = cw_ref[...]
        du_ref[...] = (cw[2:3, :] * g + cw[1:2, :] * g1 + cw[0:1, :] * g2).astype(du_ref.dtype)

        @pl.when(i == 0)
        def _():
            dcw_ref[...] = jnp.zeros_like(dcw_ref)
            dcb_ref[...] = jnp.zeros_like(dcb_ref)

        dcw_ref[0:1, :] += jnp.sum(g * u2, axis=0, keepdims=True)
        dcw_ref[1:2, :] += jnp.sum(g * u1, axis=0, keepdims=True)
        dcw_ref[2:3, :] += jnp.sum(g * u, axis=0, keepdims=True)
        dcb_ref[...] += jnp.sum(g, axis=0, keepdims=True)

    return pl.pallas_call(
        body, out_shape=(jax.ShapeDtypeStruct((N_DEV, S, f), BF16), jax.ShapeDtypeStruct((N_DEV, 3, f), F32),
                         jax.ShapeDtypeStruct((N_DEV, 1, f), F32)),
        grid=(N_DEV, nt), in_specs=[row, prev, row, nxt, cws],
        out_specs=(row, pl.BlockSpec((None, 3, f), lambda j, i: (j, 0, 0)),
                   pl.BlockSpec((None, 1, f), lambda j, i: (j, 0, 0))),
        scratch_shapes=[pltpu.VMEM((tr + HALO, f), F32), pltpu.VMEM((tr + HALO, f), F32)], name=name,
        compiler_params=_cparams("parallel", "arbitrary"))(u, u, g, g, cw)


def loss_head(name, x, g, tgt):
    S, D = x.shape
    tr = _pick(S, 256, 8)

    def body(x_ref, g_ref, t_ref, l_ref, dx_ref, dg_ref):
        tg = t_ref[...]

        def f(xv, gv):
            err = jnp.square(_rms(xv, gv) - tg)
            return 0.5 * jnp.sum(jnp.mean(err, axis=-1))

        val, vjp = jax.vjp(f, x_ref[...], g_ref[...])
        dx, dg = vjp(jnp.ones((), F32))
        dx_ref[...] = dx

        @pl.when(pl.program_id(0) == 0)
        def _():
            l_ref[...] = jnp.zeros_like(l_ref)
            dg_ref[...] = jnp.zeros_like(dg_ref)

        l_ref[...] += val
        dg_ref[...] += dg

    row = pl.BlockSpec((tr, D), lambda i: (i, 0))
    vec = pl.BlockSpec((1, D), lambda i: (0, 0))
    lt = pl.BlockSpec((8, LANES), lambda i: (0, 0))
    return pl.pallas_call(
        body, out_shape=(jax.ShapeDtypeStruct((8, LANES), F32), jax.ShapeDtypeStruct((S, D), F32),
                         jax.ShapeDtypeStruct((1, D), F32)),
        grid=(S // tr,), in_specs=[row, vec, row], out_specs=(lt, row, vec), name=name,
        compiler_params=_cparams("arbitrary"))(x, g, tgt)


def _tri(n, fn):
    r = lax.broadcasted_iota(jnp.int32, (n, n), 0)
    c = lax.broadcasted_iota(jnp.int32, (n, n), 1)
    return jnp.where(fn(r, c), 1.0, 0.0).astype(F32)


def _log_sigmoid(z):
    return jnp.minimum(z, 0.0) - jnp.log1p(jnp.exp(-jnp.abs(z)))


def fox_gate_fwd(name, ft, b):
    H, R, _ = ft.shape

    def body(f_ref, b_ref, o_ref):
        ls = _log_sigmoid(f_ref[...] + b_ref[...])
        cum = jnp.dot(ls, _tri(LANES, lambda r, c: r <= c), precision=HIGHEST, preferred_element_type=F32)
        tot = jnp.broadcast_to(cum[:, LANES - 1:LANES], (R, LANES))
        off = jnp.dot(_tri(R, lambda r, c: r > c), tot, precision=HIGHEST, preferred_element_type=F32)
        o_ref[...] = -(cum + off)

    blk = pl.BlockSpec((None, R, LANES), lambda h: (h, 0, 0))
    return pl.pallas_call(
        body, out_shape=jax.ShapeDtypeStruct((H, R, LANES), F32), grid=(H,),
        in_specs=[blk, pl.BlockSpec((None, 1, LANES), lambda h: (h, 0, 0))], out_specs=blk, name=name,
        compiler_params=_cparams("parallel"))(ft, b)


def fox_gate_bwd(name, ft, b, dkb):
    H, R, _ = ft.shape

    def body(f_ref, b_ref, d_ref, df_ref, db_ref):
        z = f_ref[...] + b_ref[...]
        d = d_ref[...]
        rev = jnp.dot(d, _tri(LANES, lambda r, c: r >= c), precision=HIGHEST, preferred_element_type=F32)
        tot = jnp.broadcast_to(rev[:, 0:1], (R, LANES))
        off = jnp.dot(_tri(R, lambda r, c: r < c), tot, precision=HIGHEST, preferred_element_type=F32)
        dls = -(rev + off)
        dz = dls * jax.nn.sigmoid(-z)
        df_ref[...] = dz
        s = jnp.sum(jnp.sum(dz, axis=1, keepdims=True), axis=0, keepdims=True)
        db_ref[...] = jnp.broadcast_to(s, (1, LANES))

    blk = pl.BlockSpec((None, R, LANES), lambda h: (h, 0, 0))
    vec = pl.BlockSpec((None, 1, LANES), lambda h: (h, 0, 0))
    return pl.pallas_call(
        body, out_shape=(jax.ShapeDtypeStruct((H, R, LANES), F32), jax.ShapeDtypeStruct((H, 1, LANES), F32)),
        grid=(H,), in_specs=[blk, vec, blk], out_specs=(blk, vec), name=name,
        compiler_params=_cparams("parallel"))(ft, b, dkb)


def _ret_log_gamma(h):
    lg = [float(np.log(np.float32(1.0) - np.float32(2.0) ** np.float32(-5.0 - i))) for i in range(RET_HEADS)]
    out = jnp.float32(lg[RET_HEADS - 1])
    for i in range(RET_HEADS - 2, -1, -1):
        out = jnp.where(h == i, jnp.float32(lg[i]), out)
    return out


def _visible(mode, B):
    r = lax.broadcasted_iota(jnp.int32, (B, B), 0)
    c = lax.broadcasted_iota(jnp.int32, (B, B), 1)
    if mode == "fox":
        return c <= r
    return (c // CHUNK) <= (r // CHUNK)


def _decay(lg, B, blocks_apart):
    r = lax.broadcasted_iota(jnp.int32, (B, B), 0)
    c = lax.broadcasted_iota(jnp.int32, (B, B), 1)
    dist = jnp.abs(r - c + blocks_apart * B).astype(F32)
    return jnp.exp(lg * dist)


def _attn_block(S):
    return 256 if S >= 2048 else 128


def attn_fwd(name, mode, q1, q1_cb, k1, k1_cb, v, v_cb, H, dv, scale, q2=None, q2_cb=0, k2=None, kbias=None):
    S = q1.shape[0]
    B = _attn_block(S)
    nq = S // B
    softmax = mode != "ret"
    two = mode == "mla"
    has_bias = mode == "fox"

    def body(*refs):
        it = iter(refs)
        q1_ref, k1_ref, v_ref = next(it), next(it), next(it)
        q2_ref = next(it) if two else None
        k2_ref = next(it) if two else None
        kb_ref = next(it) if has_bias else None
        o_ref = next(it)
        lse_ref = next(it) if softmax else None
        kbuf, vbuf = next(it), next(it)
        k2buf = next(it) if two else None
        acc = next(it)
        m_ref = next(it) if softmax else None
        l_ref = next(it) if softmax else None
        h = pl.program_id(0)
        i = pl.program_id(1)

        @pl.when(i == 0)
        def _():
            kbuf[...] = k1_ref[...].astype(BF16)
            vbuf[...] = v_ref[...].astype(BF16)
            if two:
                k2buf[...] = k2_ref[...].astype(BF16)

        qb = q1_ref[...].astype(BF16)
        q2b = q2_ref[...].astype(BF16) if two else None
        lg = _ret_log_gamma(h) if mode == "ret" else None
        acc[...] = jnp.zeros_like(acc)
        if softmax:
            m_ref[...] = jnp.full_like(m_ref, NEG_BIG)
            l_ref[...] = jnp.zeros_like(l_ref)

        def step(j, diag):
            rows = pl.ds(pl.multiple_of(j * B, B), B)
            s = lax.dot_general(qb, kbuf[rows, :], NT_DIMS, preferred_element_type=F32)
            if two:
                s = s + lax.dot_general(q2b, k2buf[rows, :], NT_DIMS, preferred_element_type=F32)
            vj = vbuf[rows, :]
            if softmax:
                s = s * scale
                if has_bias:
                    s = s + kb_ref[j]
                if diag:
                    s = jnp.where(_visible(mode, B), s, NEG_BIG)
                m_old = m_ref[...]
                m_new = jnp.maximum(m_old, jnp.max(s, axis=1, keepdims=True))
                alpha = jnp.exp(m_old - m_new)
                p = jnp.exp(s - m_new)
                l_ref[...] = alpha * l_ref[...] + jnp.sum(p, axis=1, keepdims=True)
                acc[...] = alpha * acc[...] + jnp.dot(p.astype(BF16), vj, preferred_element_type=F32)
                m_ref[...] = m_new
            else:
                if diag:
                    p = jnp.where(_visible(mode, B), s * _decay(lg, B, 0), 0.0)
                else:
                    p = s * _decay(lg, B, i - j)
                acc[...] += jnp.dot(p.astype(BF16), vj, preferred_element_type=F32)

        def loop_body(j, carry):
            step(j, False)
            return carry

        lax.fori_loop(0, i, loop_body, 0)
        step(i, True)
        if softmax:
            o_ref[...] = acc[...] / l_ref[...]
            lse_ref[...] = jnp.broadcast_to(m_ref[...] + jnp.log(l_ref[...]), (B, LANES))
        else:
            o_ref[...] = acc[...]

    in_specs = [pl.BlockSpec((B, LANES), lambda h, i: (i, q1_cb + h)),
                pl.BlockSpec((S, LANES), lambda h, i: (0, k1_cb + h)),
                pl.BlockSpec((S, dv), lambda h, i: (0, v_cb + h))]
    args = [q1, k1, v]
    if two:
        in_specs += [pl.BlockSpec((B, LANES), lambda h, i: (i, q2_cb + h)),
                     pl.BlockSpec((S, LANES), lambda h, i: (0, 0))]
        args += [q2, k2]
    if has_bias:
        in_specs.append(pl.BlockSpec((None, nq, 1, B), lambda h, i: (h, 0, 0, 0)))
        args.append(kbias)
    out_shape = [jax.ShapeDtypeStruct((S, H * dv), F32)]
    out_specs = [pl.BlockSpec((B, dv), lambda h, i: (i, h))]
    if softmax:
        out_shape.append(jax.ShapeDtypeStruct((S, H * LANES), F32))
        out_specs.append(pl.BlockSpec((B, LANES), lambda h, i: (i, h)))
    scratch = [pltpu.VMEM((S, LANES), BF16), pltpu.VMEM((S, dv), BF16)]
    if two:
        scratch.append(pltpu.VMEM((S, LANES), BF16))
    scratch.append(pltpu.VMEM((B, dv), F32))
    if softmax:
        scratch += [pltpu.VMEM((B, 1), F32), pltpu.VMEM((B, 1), F32)]
    res = pl.pallas_call(body, out_shape=tuple(out_shape), grid=(H, nq), in_specs=in_specs,
                         out_specs=tuple(out_specs), scratch_shapes=scratch, name=name,
                         compiler_params=_cparams("parallel", "arbitrary"))(*args)
    return res if softmax else (res[0], None)


def attn_bwd(name, mode, q1, q1_cb, k1, k1_cb, v, v_cb, H, dv, scale, do, o=None, lse=None,
             q2=None, q2_cb=0, k2=None, kbias=None):
    S = q1.shape[0]
    B = _attn_block(S)
    nb = S // B
    softmax = mode != "ret"
    two = mode == "mla"
    has_bias = mode == "fox"

    def body(*refs):
        it = iter(refs)
        q1_ref, k1_ref, v_ref, do_ref = next(it), next(it), next(it), next(it)
        o_ref = next(it) if softmax else None
        lse_ref = next(it) if softmax else None
        q2_ref = next(it) if two else None
        k2_ref = next(it) if two else None
        kb_ref = next(it) if has_bias else None
        dq1_ref, dk1_ref, dv_ref = next(it), next(it), next(it)
        dq2_ref = next(it) if two else None
        dk2_ref = next(it) if two else None
        dkb_ref = next(it) if has_bias else None
        drow_ref = next(it) if has_bias else None
        qbuf, dobuf = next(it), next(it)
        q2buf = next(it) if two else None
        delta = next(it) if softmax else None
        dk_acc, dv_acc = next(it), next(it)
        dk2_acc = next(it) if two else None
        dkb_acc = next(it) if has_bias else None
        h = pl.program_id(0)
        j = pl.program_id(1)

        @pl.when(j == 0)
        def _():
            qbuf[...] = q1_ref[...].astype(BF16)
            dobuf[...] = do_ref[...].astype(BF16)
            dq1_ref[...] = jnp.zeros_like(dq1_ref)
            if has_bias:
                drow_ref[...] = jnp.zeros_like(drow_ref)
            if two:
                q2buf[...] = q2_ref[...].astype(BF16)
                dq2_ref[...] = jnp.zeros_like(dq2_ref)
            if softmax:
                def drow(t, carry):
                    rows = pl.ds(pl.multiple_of(t * B, B), B)
                    delta[rows, :] = jnp.sum(do_ref[rows, :].astype(F32) * o_ref[rows, :], axis=1, keepdims=True)
                    return carry
                lax.fori_loop(0, nb, drow, 0)

        kj = k1_ref[...].astype(BF16)
        vj = v_ref[...].astype(BF16)
        k2j = k2_ref[...].astype(BF16) if two else None
        kbj = kb_ref[...] if has_bias else None
        lg = _ret_log_gamma(h) if mode == "ret" else None
        dk_acc[...] = jnp.zeros_like(dk_acc)
        dv_acc[...] = jnp.zeros_like(dv_acc)
        if two:
            dk2_acc[...] = jnp.zeros_like(dk2_acc)
        if has_bias:
            dkb_acc[...] = jnp.zeros_like(dkb_acc)

        def step(i, diag):
            rows = pl.ds(pl.multiple_of(i * B, B), B)
            qi = qbuf[rows, :]
            doi = dobuf[rows, :]
            s = lax.dot_general(qi, kj, NT_DIMS, preferred_element_type=F32)
            if two:
                q2i = q2buf[rows, :]
                s = s + lax.dot_general(q2i, k2j, NT_DIMS, preferred_element_type=F32)
            dp = lax.dot_general(doi, vj, NT_DIMS, preferred_element_type=F32)
            if softmax:
                s = s * scale
                if has_bias:
                    s = s + kbj
                if diag:
                    s = jnp.where(_visible(mode, B), s, NEG_BIG)
                p = jnp.exp(s - lse_ref[rows, 0:1])
                ds = p * (dp - delta[rows, :])
                if has_bias:
                    dkb_acc[...] += jnp.sum(ds, axis=0, keepdims=True)
                    drow_ref[rows, :] += jnp.broadcast_to(jnp.sum(ds, axis=1, keepdims=True), (B, LANES))
                dsb = (ds * scale).astype(BF16)
            else:
                if diag:
                    dec = jnp.where(_visible(mode, B), _decay(lg, B, 0), 0.0)
                else:
                    dec = _decay(lg, B, i - j)
                p = s * dec
                dsb = (dp * dec).astype(BF16)
            dv_acc[...] += lax.dot_general(p.astype(BF16), doi, TN_DIMS, preferred_element_type=F32)
            dk_acc[...] += lax.dot_general(dsb, qi, TN_DIMS, preferred_element_type=F32)
            dq1_ref[rows, :] += jnp.dot(dsb, kj, preferred_element_type=F32)
            if two:
                dk2_acc[...] += lax.dot_general(dsb, q2i, TN_DIMS, preferred_element_type=F32)
                dq2_ref[rows, :] += jnp.dot(dsb, k2j, preferred_element_type=F32)

        step(j, True)

        def loop_body(i, carry):
            step(i, False)
            return carry

        lax.fori_loop(j + 1, nb, loop_body, 0)
        dk1_ref[...] = dk_acc[...]
        dv_ref[...] = dv_acc[...]
        if two:
            dk2_ref[...] = dk2_acc[...]
        if has_bias:
            dkb_ref[...] = dkb_acc[...]

    full = lambda w, cb: pl.BlockSpec((S, w), lambda h, j: (0, cb + h))
    blk = lambda w, cb: pl.BlockSpec((B, w), lambda h, j: (j, cb + h))
    in_specs = [full(LANES, q1_cb), blk(LANES, k1_cb), blk(dv, v_cb), full(dv, 0)]
    args = [q1, k1, v, do]
    if softmax:
        in_specs += [full(dv, 0), full(LANES, 0)]
        args += [o, lse]
    if two:
        in_specs += [full(LANES, q2_cb), pl.BlockSpec((B, LANES), lambda h, j: (j, 0))]
        args += [q2, k2]
    if has_bias:
        in_specs.append(pl.BlockSpec((None, None, 1, B), lambda h, j: (h, j, 0, 0)))
        args.append(kbias)
    names = ["dq1", "dk1", "dv"]
    out_shape = [jax.ShapeDtypeStruct((S, H * LANES), F32), jax.ShapeDtypeStruct((S, H * LANES), F32),
                 jax.ShapeDtypeStruct((S, H * dv), F32)]
    out_specs = [full(LANES, 0), blk(LANES, 0), blk(dv, 0)]
    if two:
        names += ["dq2", "dk2h"]
        out_shape += [jax.ShapeDtypeStruct((S, H * LANES), F32)] * 2
        out_specs += [full(LANES, 0), blk(LANES, 0)]
    if has_bias:
        names.append("dkb")
        out_shape.append(jax.ShapeDtypeStruct((H, nb, 1, B), F32))
        out_specs.append(pl.BlockSpec((None, None, 1, B), lambda h, j: (h, j, 0, 0)))
        names.append("drow")
        out_shape.append(jax.ShapeDtypeStruct((S, H * LANES), F32))
        out_specs.append(full(LANES, 0))
    scratch = [pltpu.VMEM((S, LANES), BF16), pltpu.VMEM((S, dv), BF16)]
    if two:
        scratch.append(pltpu.VMEM((S, LANES), BF16))
    if softmax:
        scratch.append(pltpu.VMEM((S, 1), F32))
    scratch += [pltpu.VMEM((B, LANES), F32), pltpu.VMEM((B, dv), F32)]
    if two:
        scratch.append(pltpu.VMEM((B, LANES), F32))
    if has_bias:
        scratch.append(pltpu.VMEM((1, B), F32))
    res = pl.pallas_call(body, out_shape=tuple(out_shape), grid=(H, nb), in_specs=in_specs,
                         out_specs=tuple(out_specs), scratch_shapes=scratch, name=name,
                         compiler_params=_cparams("parallel", "arbitrary"))(*args)
    return dict(zip(names, res))


def head_sum(name, x, H, out_dtype):
    S = x.shape[0]
    tr = _pick(S, 512, 8)

    def body(x_ref, o_ref):
        acc = x_ref[:, 0:LANES]
        for h in range(1, H):
            acc = acc + x_ref[:, h * LANES:(h + 1) * LANES]
        o_ref[...] = acc.astype(o_ref.dtype)

    return pl.pallas_call(
        body, out_shape=jax.ShapeDtypeStruct((S, LANES), out_dtype), grid=(S // tr,),
        in_specs=[pl.BlockSpec((tr, H * LANES), lambda i: (i, 0))],
        out_specs=pl.BlockSpec((tr, LANES), lambda i: (i, 0)), name=name,
        compiler_params=_cparams("parallel"))(x)


def _mesh_pos():
    return lax.axis_index("x"), lax.axis_index("y"), lax.axis_index("c")


def _peer(pos, k):
    x, y, c = pos
    px = 1 - x if k & 4 else x
    py = 1 - y if k & 2 else y
    pc = 1 - c if k & 1 else c
    return (px, py, pc), 4 * px + 2 * py + pc


def exchange(name, tensors):
    nt = len(tensors)
    flat_in, counts = [], []
    out_shape = []
    for mode, srcs in tensors:
        counts.append(len(srcs))
        flat_in += list(srcs)
        rc = srcs[0].shape[-2:]
        out_shape.append(jax.ShapeDtypeStruct((len(srcs), N_DEV) + tuple(rc), srcs[0].dtype))
    n_in = len(flat_in)

    def body(*refs):
        ins = refs[:n_in]
        outs = refs[n_in:n_in + nt]
        send_sems, recv_sems, local_sems = refs[n_in + nt:]
        pos = _mesh_pos()
        me = 4 * pos[0] + 2 * pos[1] + pos[2]
        srcs_of, base = [], 0
        for t in range(nt):
            srcs_of.append(ins[base:base + counts[t]])
            base += counts[t]

        def src_view(t, l, slot):
            ref = srcs_of[t][l]
            return ref if tensors[t][0] == "gather" else ref.at[slot]

        def all_layers(t, slot):
            return outs[t].at[pl.ds(0, counts[t]), slot]

        for t in range(nt):
            for l in range(counts[t]):
                pltpu.make_async_copy(src_view(t, l, me), outs[t].at[l, me], local_sems.at[t]).start()
        for t in range(nt):
            for k in range(1, N_DEV):
                peer, pid = _peer(pos, k)
                for l in range(counts[t]):
                    pltpu.make_async_remote_copy(
                        src_ref=src_view(t, l, pid), dst_ref=outs[t].at[l, me],
                        send_sem=send_sems.at[t, k - 1], recv_sem=recv_sems.at[t, k - 1],
                        device_id=peer, device_id_type=pl.DeviceIdType.MESH).start()
        for t in range(nt):
            for k in range(1, N_DEV):
                peer, pid = _peer(pos, k)
                pltpu.make_async_remote_copy(
                    src_ref=all_layers(t, pid), dst_ref=all_layers(t, pid),
                    send_sem=send_sems.at[t, k - 1], recv_sem=recv_sems.at[t, k - 1],
                    device_id=peer, device_id_type=pl.DeviceIdType.MESH).wait()
        for t in range(nt):
            pltpu.make_async_copy(all_layers(t, me), all_layers(t, me), local_sems.at[t]).wait()

    any_spec = pl.BlockSpec(memory_space=pl.ANY)
    return pl.pallas_call(
        body, out_shape=tuple(out_shape), in_specs=[any_spec] * n_in, out_specs=tuple([any_spec] * nt),
        scratch_shapes=[pltpu.SemaphoreType.DMA((nt, N_DEV - 1)), pltpu.SemaphoreType.DMA((nt, N_DEV - 1)),
                        pltpu.SemaphoreType.DMA((nt,))],
        name=name)(*flat_in)


def reduce_parts(name, parts):
    L, n, R, C = parts.shape
    tr = _pick(R, max(8, (1 << 20) // (C * 4) // 8 * 8), 8)

    def body(p_ref, o_ref):
        acc = p_ref[0].astype(F32)
        for s in range(1, n):
            acc = acc + p_ref[s].astype(F32)
        o_ref[...] = acc

    return pl.pallas_call(
        body, out_shape=jax.ShapeDtypeStruct((L, R, C), F32), grid=(L, R // tr),
        in_specs=[pl.BlockSpec((None, n, tr, C), lambda l, i: (l, 0, i, 0))],
        out_specs=pl.BlockSpec((None, tr, C), lambda l, i: (l, i, 0)), name=name,
        compiler_params=_cparams("parallel", "parallel"))(parts)


def adamw(name, w, m, v, parts):
    L, R, C = w.shape
    n = parts.shape[1]
    tr = _pick(R, max(8, (1 << 19) // (C * 4) // 8 * 8), 8)

    def body(w_ref, m_ref, v_ref, p_ref, g_ref, d_ref, nm_ref, nv_ref):
        g = p_ref[0].astype(F32)
        for s in range(1, n):
            g = g + p_ref[s].astype(F32)
        wv = w_ref[...]
        mn = ADAM_B1 * m_ref[...] + (1.0 - ADAM_B1) * g
        vn = ADAM_B2 * v_ref[...] + (1.0 - ADAM_B2) * jnp.square(g)
        m_hat = mn / (1.0 - ADAM_B1 ** ADAM_STEP)
        v_hat = vn / (1.0 - ADAM_B2 ** ADAM_STEP)
        g_ref[...] = g
        d_ref[...] = -ADAM_LR * (m_hat / (jnp.sqrt(v_hat) + ADAM_EPS) + ADAM_WD * wv)
        nm_ref[...] = mn
        nv_ref[...] = vn

    blk = pl.BlockSpec((None, tr, C), lambda l, i: (l, i, 0))
    sh = jax.ShapeDtypeStruct((L, R, C), F32)
    return pl.pallas_call(
        body, out_shape=(sh, sh, sh, sh), grid=(L, R // tr),
        in_specs=[blk, blk, blk, pl.BlockSpec((None, n, tr, C), lambda l, i: (l, 0, i, 0))],
        out_specs=(blk, blk, blk, blk), name=name,
        compiler_params=_cparams("parallel", "parallel"))(w, m, v, parts)


def _cols_from_blocks(g):
    L, n, R, c = g.shape
    return g.transpose(0, 2, 1, 3).reshape(L, R, n * c)


def _cols_to_blocks(w):
    R, C = w.shape
    return w.reshape(R, N_DEV, C // N_DEV).transpose(1, 0, 2)


def _uq_permute(w):
    lead = w.shape[:-1]
    w4 = w.reshape(lead + (MLA_HEADS, MLA_NOPE + MLA_ROPE))
    nope = w4[..., :MLA_NOPE].reshape(lead + (MLA_HEADS * MLA_NOPE,))
    rope = jnp.pad(w4[..., MLA_NOPE:], [(0, 0)] * (w4.ndim - 1) + [(0, LANES - MLA_ROPE)])
    return jnp.concatenate([nope, rope.reshape(lead + (MLA_HEADS * LANES,))], axis=-1)


def _uq_unpermute(w):
    lead = w.shape[:-1]
    n = MLA_HEADS * MLA_NOPE
    nope = w[..., :n].reshape(lead + (MLA_HEADS, MLA_NOPE))
    rope = w[..., n:].reshape(lead + (MLA_HEADS, LANES))[..., :MLA_ROPE]
    return jnp.concatenate([nope, rope], axis=-1).reshape(lead + (MLA_HEADS * (MLA_NOPE + MLA_ROPE),))


def _ukv_permute(w):
    lead = w.shape[:-1]
    w4 = w.reshape(lead + (MLA_HEADS, 2, MLA_NOPE))
    return jnp.swapaxes(w4, -3, -2).reshape(lead + (2 * MLA_HEADS * MLA_NOPE,))


def _ukv_unpermute(w):
    lead = w.shape[:-1]
    w4 = w.reshape(lead + (2, MLA_HEADS, MLA_NOPE))
    return jnp.swapaxes(w4, -3, -2).reshape(lead + (2 * MLA_HEADS * MLA_NOPE,))


SMALL = ["norm1_g", "mla_q_norm_g", "mla_kv_norm_g", "fox_b_f", "norm2_g", "ffn_conv_b", "final_norm_g"]
SMALL_TILE = 8 * LANES


def _pack_small(d):
    flat = jnp.concatenate([d[n].reshape(-1).astype(F32) for n in SMALL])
    pad = -flat.shape[0] % SMALL_TILE
    return jnp.pad(flat, (0, pad)).reshape(-1, LANES)


def _unpack_small(packed, like):
    flat = packed.reshape(-1)
    out, o = {}, 0
    for n in SMALL:
        sz = int(np.prod(like[n].shape))
        out[n] = flat[o:o + sz].reshape(like[n].shape)
        o += sz
    return out


WEIGHTS = ["norm1_g", "w_in", "mla_q_norm_g", "mla_kv_norm_g", "mla_w_uq", "mla_w_ukv", "fox_b_f", "w_br_fox",
           "w_br_mla", "w_br_ret", "w_out", "norm2_g", "ffn_w_up", "ffn_w_gate", "ffn_conv_w", "ffn_conv_b",
           "ffn_w_down", "final_norm_g"]
BIG = ["w_in", "mla_w_uq", "mla_w_ukv", "w_br_fox", "w_br_mla", "w_br_ret", "w_out", "ffn_w_up", "ffn_w_gate",
       "ffn_conv_w", "ffn_w_down"]


def kernel(x, norm1_g, w_in, mla_q_norm_g, mla_kv_norm_g, mla_w_uq, mla_w_ukv, fox_b_f, w_br_fox, w_br_mla, w_br_ret, w_out, norm2_g, ffn_w_up, ffn_w_gate, ffn_conv_w, ffn_conv_b, ffn_w_down, final_norm_g, loss_target, m_norm1_g, m_w_in, m_mla_q_norm_g, m_mla_kv_norm_g, m_mla_w_uq, m_mla_w_ukv, m_fox_b_f, m_w_br_fox, m_w_br_mla, m_w_br_ret, m_w_out, m_norm2_g, m_ffn_w_up, m_ffn_w_gate, m_ffn_conv_w, m_ffn_conv_b, m_ffn_w_down, m_final_norm_g, v_norm1_g, v_w_in, v_mla_q_norm_g, v_mla_kv_norm_g, v_mla_w_uq, v_mla_w_ukv, v_fox_b_f, v_w_br_fox, v_w_br_mla, v_w_br_ret, v_w_out, v_norm2_g, v_ffn_w_up, v_ffn_w_gate, v_ffn_conv_w, v_ffn_conv_b, v_ffn_w_down, v_final_norm_g):
    env = dict(locals())
    W = {n: env[n] for n in WEIGHTS}
    Mo = {n: env["m_" + n] for n in WEIGHTS}
    Vo = {n: env["v_" + n] for n in WEIGHTS}
    S, D = x.shape[1], x.shape[2]
    L = w_in.shape[0]
    lay = InLayout(D)
    NP = lay.total
    f = ffn_w_up.shape[-1]
    xs = x.reshape(S, D)
    tgt = loss_target.reshape(S, D)

    local = {n: W[n].astype(BF16) for n in BIG}
    local["w_in"] = lay.permute(W["w_in"]).astype(BF16)
    gathered = exchange("gather_weights", [("gather", [local[n][l] for l in range(L)]) for n in BIG])
    G = dict(zip(BIG, gathered))
    Win = G["w_in"].reshape(L, D, NP)
    Wuq = _uq_permute(_cols_from_blocks(G["mla_w_uq"]))
    Wukv = _ukv_permute(_cols_from_blocks(G["mla_w_ukv"]))
    Wbf = _cols_from_blocks(G["w_br_fox"])
    Wbm = _cols_from_blocks(G["w_br_mla"])
    Wbr = _cols_from_blocks(G["w_br_ret"])
    Wout = G["w_out"].reshape(L, D, D)
    Wup, Wgate, Wdown, Wconv = G["ffn_w_up"], G["ffn_w_gate"], G["ffn_w_down"], G["ffn_conv_w"]
    Wconv = Wconv.astype(F32)
    cbias = ffn_conv_b.reshape(L, N_DEV, 1, f)

    tab64 = rope_tables(S, MLA_ROPE)
    tab128 = rope_tables(S, RET_DK)
    fox_scale = FOX_DH ** -0.5
    mla_scale = (MLA_NOPE + MLA_ROPE) ** -0.5
    ret_kscale = RET_DK ** -0.5
    R = S // LANES
    AB = _attn_block(S)
    NOPE_W = MLA_HEADS * MLA_NOPE

    def vec(a):
        return a.reshape(1, -1)

    saved = []
    xc = xs
    for l in range(L):
        s = {"x": xc}
        h1 = rms_fwd("norm1", xc, 0, D, vec(norm1_g[l]), BF16)
        P = mm_nn("in_proj", h1, Win, F32, b_lead=l)
        s.update(h1=h1, P=P)
        ff_off = lay.off["ff"]
        ft = P[:, ff_off:ff_off + FOX_HEADS].T.reshape(FOX_HEADS, R, LANES)
        bfl = jnp.broadcast_to(fox_b_f[l].reshape(FOX_HEADS, 1, 1), (FOX_HEADS, 1, LANES))
        kbias = fox_gate_fwd("fox_gate", ft, bfl).reshape(FOX_HEADS, S // AB, 1, AB)
        o_fox, lse_fox = attn_fwd("fox_attn", "fox", P, lay.cb("fq", LANES), P, lay.cb("fk", LANES),
                                  P, lay.cb("fv", LANES), FOX_HEADS, FOX_DH, fox_scale, kbias=kbias)
        s.update(ft=ft, bfl=bfl, kbias=kbias, o_fox=o_fox, lse_fox=lse_fox)
        cqn = rms_fwd("mla_q_norm", P, lay.cb("mq", MLA_Q_LORA), MLA_Q_LORA, vec(mla_q_norm_g[l]), BF16)
        qall = mm_nn("mla_uq", cqn, Wuq[l], F32)
        ckvn = rms_fwd("mla_kv_norm", P, lay.cb("mkv", MLA_KV_LORA), MLA_KV_LORA, vec(mla_kv_norm_g[l]), BF16)
        kvall = mm_nn("mla_ukv", ckvn, Wukv[l], F32)
        qrope = rope_apply("mla_q_rope", qall, NOPE_W // LANES, MLA_HEADS, tab64, 1.0, F32)
        krope = rope_apply("mla_k_rope", P, lay.cb("mkr", LANES), 1, tab64, 1.0, F32)
        o_mla, lse_mla = attn_fwd("mla_attn", "mla", qall, 0, kvall, 0, kvall, NOPE_W // MLA_V, MLA_HEADS, MLA_V,
                                  mla_scale, q2=qrope, q2_cb=0, k2=krope)
        s.update(cqn=cqn, qall=qall, ckvn=ckvn, kvall=kvall, qrope=qrope, krope=krope, o_mla=o_mla,
                 lse_mla=lse_mla)
        rq = rope_apply("ret_q_rope", P, lay.cb("rq", LANES), RET_HEADS, tab128, 1.0, F32)
        rk = rope_apply("ret_k_rope", P, lay.cb("rk", LANES), RET_HEADS, tab128, ret_kscale, F32)
        o_ret, _ = attn_fwd("ret_attn", "ret", rq, 0, rk, 0, P, lay.cb("rv", RET_DV), RET_HEADS, RET_DV, 1.0)
        c_ret = ret_out_fwd("ret_out", o_ret, P, lay.cb("rg", RET_DV), BF16)
        s.update(rq=rq, rk=rk, o_ret=o_ret, c_ret=c_ret)
        A = mm_nn("br_fox", o_fox, Wbf[l], F32)
        Bm = mm_nn("br_mla", o_mla, Wbm[l], F32)
        C = mm_nn("br_ret", c_ret, Wbr[l], F32)
        merged = merge_fwd("merge", P, lay.cb("gates", D), A, Bm, C, BF16)
        x2 = mm_nn("out_proj", merged, Wout, F32, b_lead=l, res=xc)
        s.update(A=A, Bm=Bm, C=C, merged=merged, x2=x2)
        h2 = rms_fwd("norm2", x2, 0, D, vec(norm2_g[l]), BF16)
        u = ffn_up("ffn_up", h2, Wup, l, F32)
        gt = ffn_up("ffn_gate", h2, Wgate, l, F32)
        act = ffn_act_fwd("ffn_act", u, gt, Wconv, cbias, l, BF16)
        xc = ffn_down("ffn_down", act, Wdown, l, x2, F32)
        s.update(h2=h2, u=u, gt=gt, act=act)
        saved.append(s)

    loss_tile, dx, dgf = loss_head("loss_head", xc, vec(final_norm_g), tgt)
    loss = lax.psum(loss_tile[0, 0], ("x", "y", "c"))

    gbig = {n: [None] * L for n in BIG}
    gsmall = {n: [None] * L for n in SMALL if n != "final_norm_g"}
    for l in reversed(range(L)):
        s = saved[l]
        P = s["P"]
        dxb = dx.astype(BF16)
        dact = ffn_down_bwd_act("ffn_down_da", dxb, Wdown, l, BF16)
        gbig["ffn_w_down"][l] = ffn_down_bwd_w("ffn_down_dw", s["act"], dxb, BF16)
        g, dgt = ffn_act_bwd_point("ffn_act_bwd", s["u"], s["gt"], Wconv, cbias, l, dact)
        du, dcw, dcb = ffn_act_bwd_conv("ffn_conv_bwd", s["u"], g, Wconv, l)
        gbig["ffn_conv_w"][l] = dcw.astype(BF16)
        gsmall["ffn_conv_b"][l] = dcb.reshape(-1)
        gbig["ffn_w_up"][l] = ffn_up_bwd_w("ffn_up_dw", s["h2"], du, BF16)
        gbig["ffn_w_gate"][l] = ffn_up_bwd_w("ffn_gate_dw", s["h2"], dgt, BF16)
        dh2 = ffn_up_bwd_h("ffn_up_dh", du, Wup, l, None, F32)
        dh2 = ffn_up_bwd_h("ffn_gate_dh", dgt, Wgate, l, dh2, BF16)
        dx2, dg2 = rms_bwd("norm2_bwd", s["x2"], 0, D, vec(norm2_g[l]), dh2, F32, res=dx)
        gsmall["norm2_g"][l] = dg2.reshape(-1)
        dx2b = dx2.astype(BF16)
        dmerged = mm_nt("out_proj_dm", dx2b, Wout, BF16, b_lead=l)
        gbig["w_out"][l] = mm_tn("out_proj_dw", s["merged"], dx2b, BF16).reshape(N_DEV, D // N_DEV, D)
        dgates, dA, dB, dC = merge_bwd("merge_bwd", P, lay.cb("gates", D), s["A"], s["Bm"], s["C"], dmerged)
        do_fox = mm_nt("br_fox_do", dA, Wbf[l], F32)
        do_mla = mm_nt("br_mla_do", dB, Wbm[l], F32)
        dc_ret = mm_nt("br_ret_do", dC, Wbr[l], BF16)
        gbig["w_br_fox"][l] = _cols_to_blocks(mm_tn("br_fox_dw", s["o_fox"], dA, BF16))
        gbig["w_br_mla"][l] = _cols_to_blocks(mm_tn("br_mla_dw", s["o_mla"], dB, BF16))
        gbig["w_br_ret"][l] = _cols_to_blocks(mm_tn("br_ret_dw", s["c_ret"], dC, BF16))
        do_ret, drg = ret_out_bwd("ret_out_bwd", s["o_ret"], P, lay.cb("rg", RET_DV), dc_ret)
        rb = attn_bwd("ret_attn_bwd", "ret", s["rq"], 0, s["rk"], 0, P, lay.cb("rv", RET_DV), RET_HEADS, RET_DV,
                      1.0, do_ret)
        drq = rope_apply("ret_q_rope_bwd", rb["dq1"], 0, RET_HEADS, tab128, 1.0, BF16, transpose=True)
        drk = rope_apply("ret_k_rope_bwd", rb["dk1"], 0, RET_HEADS, tab128, ret_kscale, BF16, transpose=True)
        drv = rb["dv"].astype(BF16)
        mb = attn_bwd("mla_attn_bwd", "mla", s["qall"], 0, s["kvall"], 0, s["kvall"], NOPE_W // MLA_V, MLA_HEADS,
                      MLA_V, mla_scale, do_mla, o=s["o_mla"], lse=s["lse_mla"], q2=s["qrope"], q2_cb=0,
                      k2=s["krope"])
        dqrope = rope_apply("mla_q_rope_bwd", mb["dq2"], 0, MLA_HEADS, tab64, 1.0, BF16, transpose=True)
        dkr_sum = head_sum("mla_k_rope_sum", mb["dk2h"], MLA_HEADS, F32)
        dmkr = rope_apply("mla_k_rope_bwd", dkr_sum, 0, 1, tab64, 1.0, BF16, transpose=True)
        dqall = jnp.concatenate([mb["dq1"].astype(BF16), dqrope], axis=1)
        dkvall = jnp.concatenate([mb["dk1"].astype(BF16), mb["dv"].astype(BF16)], axis=1)
        dcqn = mm_nt("mla_uq_dx", dqall, Wuq[l], F32)
        dckvn = mm_nt("mla_ukv_dx", dkvall, Wukv[l], F32)
        guq = _uq_unpermute(mm_tn("mla_uq_dw", s["cqn"], dqall, BF16))
        gukv = _ukv_unpermute(mm_tn("mla_ukv_dw", s["ckvn"], dkvall, BF16))
        gbig["mla_w_uq"][l] = _cols_to_blocks(guq)
        gbig["mla_w_ukv"][l] = _cols_to_blocks(gukv)
        dmq, dgq = rms_bwd("mla_q_norm_bwd", P, lay.cb("mq", MLA_Q_LORA), MLA_Q_LORA, vec(mla_q_norm_g[l]),
                           dcqn, BF16)
        dmkv, dgkv = rms_bwd("mla_kv_norm_bwd", P, lay.cb("mkv", MLA_KV_LORA), MLA_KV_LORA,
                             vec(mla_kv_norm_g[l]), dckvn, BF16)
        gsmall["mla_q_norm_g"][l] = dgq.reshape(-1)
        gsmall["mla_kv_norm_g"][l] = dgkv.reshape(-1)
        fb = attn_bwd("fox_attn_bwd", "fox", P, lay.cb("fq", LANES), P, lay.cb("fk", LANES), P,
                      lay.cb("fv", LANES), FOX_HEADS, FOX_DH, fox_scale, do_fox, o=s["o_fox"], lse=s["lse_fox"],
                      kbias=s["kbias"])
        drow = fb["drow"].reshape(S, FOX_HEADS, LANES)[:, :, 0].T.reshape(FOX_HEADS, R, LANES)
        dft, dbf = fox_gate_bwd("fox_gate_bwd", s["ft"], s["bfl"], fb["dkb"].reshape(FOX_HEADS, R, LANES) - drow)
        gsmall["fox_b_f"][l] = dbf[:, 0, 0]
        dff = jnp.pad(dft.reshape(FOX_HEADS, S).T, ((0, 0), (0, LANES - FOX_HEADS))).astype(BF16)
        segs = dict(gates=dgates, rv=drv, rg=drg, mq=dmq, rq=drq, rk=drk, mkv=dmkv, fq=fb["dq1"].astype(BF16),
                    fk=fb["dk1"].astype(BF16), fv=fb["dv"].astype(BF16), mkr=dmkr, ff=dff)
        dP = jnp.concatenate([segs[n] for n in lay.order], axis=1)
        dh1 = mm_nt("in_proj_dh", dP, Win, BF16, b_lead=l)
        gbig["w_in"][l] = mm_tn("in_proj_dw", s["h1"], dP, BF16).reshape(N_DEV, D // N_DEV, NP)
        dx, dg1 = rms_bwd("norm1_bwd", s["x"], 0, D, vec(norm1_g[l]), dh1, F32, res=dx2)
        gsmall["norm1_g"][l] = dg1.reshape(-1)

    small_like = {n: W[n] for n in SMALL}
    small_part = {n: jnp.stack(gsmall[n]) for n in gsmall}
    small_part["final_norm_g"] = dgf.reshape(-1)
    packed_part = _pack_small(small_part)
    received = exchange("reduce_grads", [("scatter", gbig[n]) for n in BIG] + [("gather", [packed_part])])
    recv = dict(zip(BIG, received[:-1]))
    small_recv = received[-1]

    out = {}
    for n in BIG:
        parts = recv[n]
        if n == "w_in":
            parts = lay.unpermute(reduce_parts("w_in_grad_sum", parts))[:, None]
        out[n] = adamw("adamw_" + n, W[n], Mo[n], Vo[n], parts)
    ps = adamw("adamw_small", _pack_small(small_like)[None], _pack_small({n: Mo[n] for n in SMALL})[None],
               _pack_small({n: Vo[n] for n in SMALL})[None], small_recv)
    small_out = [_unpack_small(a[0], small_like) for a in ps]
    for n in SMALL:
        out[n] = tuple(so[n] for so in small_out)

    grads = [out[n][0] for n in WEIGHTS]
    deltas = [out[n][1] for n in WEIGHTS]
    new_m = [out[n][2] for n in WEIGHTS]
    new_v = [out[n][3] for n in WEIGHTS]
    return (loss, dx.reshape(1, S, D), *grads, *deltas, *new_m, *new_v)
```

```python
import functools
import math

import numpy as np
import jax
import jax.numpy as jnp
from jax import lax
from jax.experimental import pallas as pl
from jax.experimental.pallas import tpu as pltpu

F32 = jnp.float32
BF16 = jnp.bfloat16

CHUNK = 64
NORM_EPS = 1e-6
ROPE_THETA = 10000.0
FOX_HEADS, FOX_DH = 6, 128
FOX_W = FOX_HEADS * FOX_DH
MLA_HEADS, MLA_NOPE, MLA_ROPE, MLA_V = 6, 128, 64, 128
MLA_Q_LORA, MLA_KV_LORA = 512, 256
MLA_W = MLA_HEADS * MLA_V
RET_HEADS, RET_DK, RET_DV = 4, 128, 256
RET_QK_W, RET_V_W = RET_HEADS * RET_DK, RET_HEADS * RET_DV
ADAM_LR, ADAM_B1, ADAM_B2, ADAM_EPS, ADAM_WD, ADAM_STEP = 0.001, 0.9, 0.999, 1e-08, 0.01, 10

N_DEV = 8
LANES = 128
V7X_VMEM_LIMIT_BYTES = 52 * 1024 * 1024
NEG_BIG = -1e30
HIGHEST = lax.Precision.HIGHEST

NT_DIMS = (((1,), (1,)), ((), ()))
TN_DIMS = (((0,), (0,)), ((), ()))
NN_DIMS = (((1,), (0,)), ((), ()))


def _pick(n, cap, mult=LANES):
    best = None
    for t in range(mult, min(n, cap) + 1, mult):
        if n % t == 0:
            best = t
    return n if best is None else best


def _cparams(*sem):
    return pltpu.CompilerParams(dimension_semantics=sem, vmem_limit_bytes=V7X_VMEM_LIMIT_BYTES)


class InLayout:
    def __init__(self, d_model):
        d = d_model
        self.d = d
        orig = dict(fq=(0, FOX_W), fk=(FOX_W, FOX_W), fv=(2 * FOX_W, FOX_W), ff=(3 * FOX_W, FOX_HEADS))
        o = 3 * FOX_W + FOX_HEADS
        for name, w in (("mq", MLA_Q_LORA), ("mkv", MLA_KV_LORA), ("mkr", MLA_ROPE), ("rq", RET_QK_W),
                        ("rk", RET_QK_W), ("rv", RET_V_W), ("rg", RET_V_W), ("gates", 3 * d)):
            orig[name] = (o, w)
            o += w
        self.orig = orig
        self.orig_width = o
        order = ["gates", "rv", "rg", "mq", "rq", "rk", "mkv", "fq", "fk", "fv", "mkr", "ff"]
        self.order = order
        self.off, self.width = {}, {}
        p = 0
        for name in order:
            w = orig[name][1]
            wp = -(-w // LANES) * LANES
            self.off[name], self.width[name] = p, wp
            p += wp
        self.total = p

    def cb(self, name, block):
        assert self.off[name] % block == 0, (name, block)
        return self.off[name] // block

    def permute(self, w):
        parts = []
        for name in self.order:
            o, n = self.orig[name]
            seg = w[..., o:o + n]
            pad = self.width[name] - n
            if pad:
                seg = jnp.pad(seg, [(0, 0)] * (w.ndim - 1) + [(0, pad)])
            parts.append(seg)
        return jnp.concatenate(parts, axis=-1)

    def unpermute(self, w):
        names = sorted(self.orig, key=lambda n: self.orig[n][0])
        return jnp.concatenate([w[..., self.off[n]:self.off[n] + self.orig[n][1]] for n in names], axis=-1)


def _mm(name, a, b, out_shape, grid, a_spec, b_spec, o_spec, dims, acc_shape, res=None):
    nk = grid[-1]
    has_res = res is not None

    def body(*refs):
        if has_res:
            a_ref, b_ref, r_ref, o_ref = refs[:4]
        else:
            a_ref, b_ref, o_ref = refs[:3]
            r_ref = None
        prod = lax.dot_general(a_ref[...].astype(BF16), b_ref[...].astype(BF16), dims,
                               preferred_element_type=F32)
        if nk == 1:
            if has_res:
                prod = prod + r_ref[...].astype(F32)
            o_ref[...] = prod.astype(o_ref.dtype)
        else:
            acc_ref = refs[-1]
            k = pl.program_id(len(grid) - 1)

            @pl.when(k == 0)
            def _():
                acc_ref[...] = prod

            @pl.when(k > 0)
            def _():
                acc_ref[...] += prod

            @pl.when(k == nk - 1)
            def _():
                r = acc_ref[...]
                if has_res:
                    r = r + r_ref[...].astype(F32)
                o_ref[...] = r.astype(o_ref.dtype)

    in_specs = [a_spec, b_spec] + ([o_spec] if has_res else [])
    args = (a, b) + ((res,) if has_res else ())
    scratch = [pltpu.VMEM(acc_shape, F32)] if nk > 1 else []
    sem = ("parallel",) * (len(grid) - 1) + ("arbitrary",)
    return pl.pallas_call(body, out_shape=out_shape, grid=grid, in_specs=in_specs, out_specs=o_spec,
                          scratch_shapes=scratch, name=name, compiler_params=_cparams(*sem))(*args)


def mm_nn(name, a, b, out_dtype, b_lead=None, res=None):
    M, K = a.shape
    N = b.shape[-1]
    tm, tn, tk = _pick(M, 1024, 8), _pick(N, 1024), _pick(K, 2048)
    grid = (M // tm, N // tn, K // tk)
    a_spec = pl.BlockSpec((tm, tk), lambda i, j, k: (i, k))
    if b_lead is None:
        b_spec = pl.BlockSpec((tk, tn), lambda i, j, k: (k, j))
    else:
        b_spec = pl.BlockSpec((None, tk, tn), lambda i, j, k: (b_lead, k, j))
    o_spec = pl.BlockSpec((tm, tn), lambda i, j, k: (i, j))
    return _mm(name, a, b, jax.ShapeDtypeStruct((M, N), out_dtype), grid, a_spec, b_spec, o_spec,
               NN_DIMS, (tm, tn), res)


def mm_nt(name, a, b, out_dtype, b_lead=None, res=None):
    M, N = a.shape
    K = b.shape[-2]
    tm, tko, tk = _pick(M, 1024, 8), _pick(K, 1024), _pick(N, 2048)
    grid = (M // tm, K // tko, N // tk)
    a_spec = pl.BlockSpec((tm, tk), lambda i, j, k: (i, k))
    if b_lead is None:
        b_spec = pl.BlockSpec((tko, tk), lambda i, j, k: (j, k))
    else:
        b_spec = pl.BlockSpec((None, tko, tk), lambda i, j, k: (b_lead, j, k))
    o_spec = pl.BlockSpec((tm, tko), lambda i, j, k: (i, j))
    return _mm(name, a, b, jax.ShapeDtypeStruct((M, K), out_dtype), grid, a_spec, b_spec, o_spec,
               NT_DIMS, (tm, tko), res)


def mm_tn(name, a, b, out_dtype):
    M, K = a.shape
    N = b.shape[-1]
    cap = 4096 if (a.dtype == BF16 and b.dtype == BF16) else 2048
    tko, tn, tk = _pick(K, 1024), _pick(N, 1024), _pick(M, cap, 8)
    grid = (K // tko, N // tn, M // tk)
    a_spec = pl.BlockSpec((tk, tko), lambda i, j, k: (k, i))
    b_spec = pl.BlockSpec((tk, tn), lambda i, j, k: (k, j))
    o_spec = pl.BlockSpec((tko, tn), lambda i, j, k: (i, j))
    return _mm(name, a, b, jax.ShapeDtypeStruct((K, N), out_dtype), grid, a_spec, b_spec, o_spec,
               TN_DIMS, (tko, tn))


def ffn_up(name, h, w, l, out_dtype):
    M, D = h.shape
    f = w.shape[-1]
    tm = _pick(M, 1024, 8)
    grid = (M // tm, N_DEV, 1)
    return _mm(name, h, w, jax.ShapeDtypeStruct((N_DEV, M, f), out_dtype), grid,
               pl.BlockSpec((tm, D), lambda i, j, k: (i, 0)),
               pl.BlockSpec((None, None, D, f), lambda i, j, k: (l, j, 0, 0)),
               pl.BlockSpec((None, tm, f), lambda i, j, k: (j, i, 0)), NN_DIMS, (tm, f))


def ffn_down(name, act, w, l, res, out_dtype):
    _, M, f = act.shape
    D = w.shape[-1]
    tm, tn = _pick(M, 1024, 8), _pick(D, 1024)
    grid = (M // tm, D // tn, N_DEV)
    return _mm(name, act, w, jax.ShapeDtypeStruct((M, D), out_dtype), grid,
               pl.BlockSpec((None, tm, f), lambda i, j, k: (k, i, 0)),
               pl.BlockSpec((None, None, f, tn), lambda i, j, k: (l, k, 0, j)),
               pl.BlockSpec((tm, tn), lambda i, j, k: (i, j)), NN_DIMS, (tm, tn), res)


def ffn_down_bwd_act(name, dy, w, l, out_dtype):
    M, D = dy.shape
    f = w.shape[-2]
    tm = _pick(M, 1024, 8)
    grid = (M // tm, N_DEV, 1)
    return _mm(name, dy, w, jax.ShapeDtypeStruct((N_DEV, M, f), out_dtype), grid,
               pl.BlockSpec((tm, D), lambda i, j, k: (i, 0)),
               pl.BlockSpec((None, None, f, D), lambda i, j, k: (l, j, 0, 0)),
               pl.BlockSpec((None, tm, f), lambda i, j, k: (j, i, 0)), NT_DIMS, (tm, f))


def ffn_down_bwd_w(name, act, dy, out_dtype):
    _, M, f = act.shape
    D = dy.shape[-1]
    tn, tk = _pick(D, 1024), _pick(M, 4096 if dy.dtype == BF16 else 2048, 8)
    grid = (N_DEV, D // tn, M // tk)
    return _mm(name, act, dy, jax.ShapeDtypeStruct((N_DEV, f, D), out_dtype), grid,
               pl.BlockSpec((None, tk, f), lambda j, n, k: (j, k, 0)),
               pl.BlockSpec((tk, tn), lambda j, n, k: (k, n)),
               pl.BlockSpec((None, f, tn), lambda j, n, k: (j, 0, n)), TN_DIMS, (f, tn))


def ffn_up_bwd_h(name, du, w, l, res, out_dtype):
    _, M, f = du.shape
    D = w.shape[-2]
    tm, tn = _pick(M, 1024, 8), _pick(D, 1024)
    grid = (M // tm, D // tn, N_DEV)
    return _mm(name, du, w, jax.ShapeDtypeStruct((M, D), out_dtype), grid,
               pl.BlockSpec((None, tm, f), lambda i, j, k: (k, i, 0)),
               pl.BlockSpec((None, None, tn, f), lambda i, j, k: (l, k, j, 0)),
               pl.BlockSpec((tm, tn), lambda i, j, k: (i, j)), NT_DIMS, (tm, tn), res)


def ffn_up_bwd_w(name, h, du, out_dtype):
    M, D = h.shape
    f = du.shape[-1]
    tko, tk = _pick(D, 1024), _pick(M, 4096, 8)
    grid = (N_DEV, D // tko, M // tk)
    return _mm(name, h, du, jax.ShapeDtypeStruct((N_DEV, D, f), out_dtype), grid,
               pl.BlockSpec((tk, tko), lambda j, n, k: (k, n)),
               pl.BlockSpec((None, tk, f), lambda j, n, k: (j, k, 0)),
               pl.BlockSpec((None, tko, f), lambda j, n, k: (j, n, 0)), TN_DIMS, (tko, f))


def _rms(xf, g):
    return xf * lax.rsqrt(jnp.mean(xf * xf, axis=-1, keepdims=True) + NORM_EPS) * g


def rms_fwd(name, x, cb, W, g, out_dtype):
    S = x.shape[0]
    tr = _pick(S, 256, 8)

    def body(x_ref, g_ref, o_ref):
        o_ref[...] = _rms(x_ref[...].astype(F32), g_ref[...]).astype(o_ref.dtype)

    return pl.pallas_call(
        body, out_shape=jax.ShapeDtypeStruct((S, W), out_dtype), grid=(S // tr,),
        in_specs=[pl.BlockSpec((tr, W), lambda i: (i, cb)), pl.BlockSpec((1, W), lambda i: (0, 0))],
        out_specs=pl.BlockSpec((tr, W), lambda i: (i, 0)), name=name, compiler_params=_cparams("parallel"))(x, g)


def rms_bwd(name, x, cb, W, g, dy, out_dtype, res=None):
    S = x.shape[0]
    tr = _pick(S, 256, 8)
    has_res = res is not None

    def body(*refs):
        if has_res:
            x_ref, g_ref, dy_ref, r_ref, dx_ref, dg_ref = refs
        else:
            x_ref, g_ref, dy_ref, dx_ref, dg_ref = refs
        _, vjp = jax.vjp(_rms, x_ref[...].astype(F32), g_ref[...])
        dx, dg = vjp(dy_ref[...].astype(F32))
        if has_res:
            dx = dx + r_ref[...]
        dx_ref[...] = dx.astype(dx_ref.dtype)

        @pl.when(pl.program_id(0) == 0)
        def _():
            dg_ref[...] = jnp.zeros_like(dg_ref)

        dg_ref[...] += dg

    row = pl.BlockSpec((tr, W), lambda i: (i, 0))
    vec = pl.BlockSpec((1, W), lambda i: (0, 0))
    in_specs = [pl.BlockSpec((tr, W), lambda i: (i, cb)), vec, row] + ([row] if has_res else [])
    args = (x, g, dy) + ((res,) if has_res else ())
    return pl.pallas_call(
        body, out_shape=(jax.ShapeDtypeStruct((S, W), out_dtype), jax.ShapeDtypeStruct((1, W), F32)),
        grid=(S // tr,), in_specs=in_specs, out_specs=(row, vec), name=name,
        compiler_params=_cparams("arbitrary"))(*args)


def rope_tables(S, d):
    pos = jnp.arange(S, dtype=F32)
    inv_freq = ROPE_THETA ** (-jnp.arange(0, d, 2, dtype=F32) / d)
    ang = pos[:, None] * inv_freq[None, :]
    cos, sin = jnp.cos(ang), jnp.sin(ang)
    half = d // 2
    z = jnp.zeros((S, LANES - d), F32)
    zh = jnp.zeros((S, half), F32)
    c = jnp.concatenate([cos, cos, z], axis=1)
    sa = jnp.concatenate([-sin, zh, z], axis=1)
    sb = jnp.concatenate([zh, sin, z], axis=1)
    return c, sa, sb, half


def rope_apply(name, x, cb, H, tabs, scale, out_dtype, transpose=False):
    c, sa, sb, half = tabs
    S = x.shape[0]
    tr = _pick(S, 512, 8)
    up, down = LANES - half, half

    def body(x_ref, c_ref, sa_ref, sb_ref, o_ref):
        xv = x_ref[...].astype(F32)
        if not transpose:
            y = xv * c_ref[...] + pltpu.roll(xv, up, 1) * sa_ref[...] + pltpu.roll(xv, down, 1) * sb_ref[...]
            y = y * scale
        else:
            xv = xv * scale
            y = (xv * c_ref[...] + pltpu.roll(xv * sa_ref[...], down, 1)
                 + pltpu.roll(xv * sb_ref[...], up, 1))
        o_ref[...] = y.astype(o_ref.dtype)

    tab = pl.BlockSpec((tr, LANES), lambda h, i: (i, 0))
    return pl.pallas_call(
        body, out_shape=jax.ShapeDtypeStruct((S, H * LANES), out_dtype), grid=(H, S // tr),
        in_specs=[pl.BlockSpec((tr, LANES), lambda h, i: (i, cb + h)), tab, tab, tab],
        out_specs=pl.BlockSpec((tr, LANES), lambda h, i: (i, h)), name=name,
        compiler_params=_cparams("parallel", "parallel"))(x, c, sa, sb)


def _ret_out(o, g):
    y = o * lax.rsqrt(jnp.mean(o * o, axis=-1, keepdims=True) + NORM_EPS)
    return y * jax.nn.silu(g)


def ret_out_fwd(name, o, gsrc, g_cb, out_dtype):
    S = o.shape[0]
    tr = _pick(S, 512, 8)
    W = RET_DV

    def body(o_ref, g_ref, y_ref):
        y_ref[...] = _ret_out(o_ref[...], g_ref[...].astype(F32)).astype(y_ref.dtype)

    blk = pl.BlockSpec((tr, W), lambda h, i: (i, h))
    return pl.pallas_call(
        body, out_shape=jax.ShapeDtypeStruct((S, RET_HEADS * W), out_dtype), grid=(RET_HEADS, S // tr),
        in_specs=[blk, pl.BlockSpec((tr, W), lambda h, i: (i, g_cb + h))], out_specs=blk, name=name,
        compiler_params=_cparams("parallel", "parallel"))(o, gsrc)


def ret_out_bwd(name, o, gsrc, g_cb, dy):
    S = o.shape[0]
    tr = _pick(S, 512, 8)
    W = RET_DV

    def body(o_ref, g_ref, dy_ref, do_ref, dg_ref):
        _, vjp = jax.vjp(_ret_out, o_ref[...], g_ref[...].astype(F32))
        do, dg = vjp(dy_ref[...].astype(F32))
        do_ref[...] = do.astype(do_ref.dtype)
        dg_ref[...] = dg.astype(dg_ref.dtype)

    blk = pl.BlockSpec((tr, W), lambda h, i: (i, h))
    return pl.pallas_call(
        body, out_shape=(jax.ShapeDtypeStruct((S, RET_HEADS * W), F32),
                         jax.ShapeDtypeStruct((S, RET_HEADS * W), BF16)),
        grid=(RET_HEADS, S // tr),
        in_specs=[blk, pl.BlockSpec((tr, W), lambda h, i: (i, g_cb + h)), blk], out_specs=(blk, blk),
        name=name, compiler_params=_cparams("parallel", "parallel"))(o, gsrc, dy)


def _merge(g0, g1, g2, a, b, c):
    return jax.nn.sigmoid(g0) * a + jax.nn.sigmoid(g1) * b + jax.nn.sigmoid(g2) * c


def merge_fwd(name, P, gates_cb, a, b, c, out_dtype):
    S, D = a.shape
    tr = _pick(S, 128, 8)

    def body(g0, g1, g2, a_ref, b_ref, c_ref, o_ref):
        o_ref[...] = _merge(g0[...], g1[...], g2[...], a_ref[...], b_ref[...], c_ref[...]).astype(o_ref.dtype)

    row = pl.BlockSpec((tr, D), lambda i: (i, 0))
    gs = [pl.BlockSpec((tr, D), lambda i, k=k: (i, gates_cb + k)) for k in range(3)]
    return pl.pallas_call(
        body, out_shape=jax.ShapeDtypeStruct((S, D), out_dtype), grid=(S // tr,),
        in_specs=gs + [row, row, row], out_specs=row, name=name,
        compiler_params=_cparams("parallel"))(P, P, P, a, b, c)


def merge_bwd(name, P, gates_cb, a, b, c, dm):
    S, D = a.shape
    tr = _pick(S, 128, 8)

    def body(g0, g1, g2, a_ref, b_ref, c_ref, dm_ref, dg_ref, da_ref, db_ref, dc_ref):
        _, vjp = jax.vjp(_merge, g0[...], g1[...], g2[...], a_ref[...], b_ref[...], c_ref[...])
        d0, d1, d2, da, db, dc = vjp(dm_ref[...].astype(F32))
        dg_ref[:, 0:D] = d0.astype(dg_ref.dtype)
        dg_ref[:, D:2 * D] = d1.astype(dg_ref.dtype)
        dg_ref[:, 2 * D:3 * D] = d2.astype(dg_ref.dtype)
        da_ref[...] = da.astype(da_ref.dtype)
        db_ref[...] = db.astype(db_ref.dtype)
        dc_ref[...] = dc.astype(dc_ref.dtype)

    row = pl.BlockSpec((tr, D), lambda i: (i, 0))
    gs = [pl.BlockSpec((tr, D), lambda i, k=k: (i, gates_cb + k)) for k in range(3)]
    bf = jax.ShapeDtypeStruct((S, D), BF16)
    return pl.pallas_call(
        body, out_shape=(jax.ShapeDtypeStruct((S, 3 * D), BF16), bf, bf, bf), grid=(S // tr,),
        in_specs=gs + [row, row, row, row],
        out_specs=(pl.BlockSpec((tr, 3 * D), lambda i: (i, 0)), row, row, row), name=name,
        compiler_params=_cparams("parallel"))(P, P, P, a, b, c, dm)


HALO = 8


def _conv_pre(u_ref, uh_ref, ext_ref, cw_ref, cb_ref, tr):
    i = pl.program_id(1)
    u = u_ref[...]
    ext_ref[0:HALO, :] = jnp.where(i > 0, uh_ref[...], 0.0)
    ext_ref[HALO:HALO + tr, :] = u
    u1 = ext_ref[HALO - 1:HALO - 1 + tr, :]
    u2 = ext_ref[HALO - 2:HALO - 2 + tr, :]
    cw = cw_ref[...]
    uc = cb_ref[...] + ((cw[0:1, :] * u2 + cw[1:2, :] * u1) + cw[2:3, :] * u)
    return u, u1, u2, uc


def _ffn_specs(S, f, tr, l):
    nb = tr // HALO
    row = pl.BlockSpec((None, tr, f), lambda j, i: (j, i, 0))
    prev = pl.BlockSpec((None, HALO, f), lambda j, i: (j, jnp.maximum(i * nb - 1, 0), 0))
    nxt = pl.BlockSpec((None, HALO, f), lambda j, i: (j, jnp.minimum((i + 1) * nb, S // HALO - 1), 0))
    cw = pl.BlockSpec((None, None, 3, f), lambda j, i: (l, j, 0, 0))
    cb = pl.BlockSpec((None, None, 1, f), lambda j, i: (l, j, 0, 0))
    return row, prev, nxt, cw, cb


def ffn_act_fwd(name, u, gt, cw, cb, l, out_dtype):
    _, S, f = u.shape
    tr = _pick(S, 512, 8)
    row, prev, _, cws, cbs = _ffn_specs(S, f, tr, l)

    def body(u_ref, uh_ref, gt_ref, cw_ref, cb_ref, o_ref, ext_ref):
        _, _, _, uc = _conv_pre(u_ref, uh_ref, ext_ref, cw_ref, cb_ref, tr)
        o_ref[...] = (jax.nn.gelu(uc) * gt_ref[...]).astype(o_ref.dtype)

    return pl.pallas_call(
        body, out_shape=jax.ShapeDtypeStruct((N_DEV, S, f), out_dtype), grid=(N_DEV, S // tr),
        in_specs=[row, prev, row, cws, cbs], out_specs=row,
        scratch_shapes=[pltpu.VMEM((tr + HALO, f), F32)], name=name,
        compiler_params=_cparams("parallel", "parallel"))(u, u, gt, cw, cb)


def ffn_act_bwd_point(name, u, gt, cw, cb, l, dact):
    _, S, f = u.shape
    tr = _pick(S, 512, 8)
    row, prev, _, cws, cbs = _ffn_specs(S, f, tr, l)

    def body(u_ref, uh_ref, gt_ref, cw_ref, cb_ref, da_ref, g_ref, dgt_ref, ext_ref):
        _, _, _, uc = _conv_pre(u_ref, uh_ref, ext_ref, cw_ref, cb_ref, tr)
        _, vjp = jax.vjp(lambda c, t: jax.nn.gelu(c) * t, uc, gt_ref[...])
        g, dgt = vjp(da_ref[...].astype(F32))
        g_ref[...] = g
        dgt_ref[...] = dgt.astype(dgt_ref.dtype)

    return pl.pallas_call(
        body, out_shape=(jax.ShapeDtypeStruct((N_DEV, S, f), F32), jax.ShapeDtypeStruct((N_DEV, S, f), BF16)),
        grid=(N_DEV, S // tr), in_specs=[row, prev, row, cws, cbs, row], out_specs=(row, row),
        scratch_shapes=[pltpu.VMEM((tr + HALO, f), F32)], name=name,
        compiler_params=_cparams("parallel", "parallel"))(u, u, gt, cw, cb, dact)


def ffn_act_bwd_conv(name, u, g, cw, l):
    _, S, f = u.shape
    tr = _pick(S, 512, 8)
    nt = S // tr
    row, prev, nxt, cws, _ = _ffn_specs(S, f, tr, l)

    def body(u_ref, uh_ref, g_ref, gn_ref, cw_ref, du_ref, dcw_ref, dcb_ref, ext_ref, gext_ref):
        i = pl.program_id(1)
        u = u_ref[...]
        ext_ref[0:HALO, :] = jnp.where(i > 0, uh_ref[...], 0.0)
        ext_ref[HALO:HALO + tr, :] = u
        u1 = ext_ref[HALO - 1:HALO - 1 + tr, :]
        u2 = ext_ref[HALO - 2:HALO - 2 + tr, :]
        g = g_ref[...]
        gext_ref[0:tr, :] = g
        gext_ref[tr:tr + HALO, :] = jnp.where(i < nt - 1, gn_ref[...], 0.0)
        g1 = gext_ref[1:1 + tr, :]
        g2 = gext_ref[2:2 + tr, :]
        cw = cw_ref[...]
        du_ref[...] = (cw[2:3, :] * g + cw[1:2, :] * g1 + cw[0:1, :] * g2).astype(du_ref.dtype)

        @pl.when(i == 0)
        def _():
            dcw_ref[...] = jnp.zeros_like(dcw_ref)
            dcb_ref[...] = jnp.zeros_like(dcb_ref)

        dcw_ref[0:1, :] += jnp.sum(g * u2, axis=0, keepdims=True)
        dcw_ref[1:2, :] += jnp.sum(g * u1, axis=0, keepdims=True)
        dcw_ref[2:3, :] += jnp.sum(g * u, axis=0, keepdims=True)
        dcb_ref[...] += jnp.sum(g, axis=0, keepdims=True)

    return pl.pallas_call(
        body, out_shape=(jax.ShapeDtypeStruct((N_DEV, S, f), BF16), jax.ShapeDtypeStruct((N_DEV, 3, f), F32),
                         jax.ShapeDtypeStruct((N_DEV, 1, f), F32)),
        grid=(N_DEV, nt), in_specs=[row, prev, row, nxt, cws],
        out_specs=(row, pl.BlockSpec((None, 3, f), lambda j, i: (j, 0, 0)),
                   pl.BlockSpec((None, 1, f), lambda j, i: (j, 0, 0))),
        scratch_shapes=[pltpu.VMEM((tr + HALO, f), F32), pltpu.VMEM((tr + HALO, f), F32)], name=name,
        compiler_params=_cparams("parallel", "arbitrary"))(u, u, g, g, cw)


def loss_head(name, x, g, tgt):
    S, D = x.shape
    tr = _pick(S, 256, 8)

    def body(x_ref, g_ref, t_ref, l_ref, dx_ref, dg_ref):
        tg = t_ref[...]

        def f(xv, gv):
            err = jnp.square(_rms(xv, gv) - tg)
            return 0.5 * jnp.sum(jnp.mean(err, axis=-1))

        val, vjp = jax.vjp(f, x_ref[...], g_ref[...])
        dx, dg = vjp(jnp.ones((), F32))
        dx_ref[...] = dx

        @pl.when(pl.program_id(0) == 0)
        def _():
            l_ref[...] = jnp.zeros_like(l_ref)
            dg_ref[...] = jnp.zeros_like(dg_ref)

        l_ref[...] += val
        dg_ref[...] += dg

    row = pl.BlockSpec((tr, D), lambda i: (i, 0))
    vec = pl.BlockSpec((1, D), lambda i: (0, 0))
    lt = pl.BlockSpec((8, LANES), lambda i: (0, 0))
    return pl.pallas_call(
        body, out_shape=(jax.ShapeDtypeStruct((8, LANES), F32), jax.ShapeDtypeStruct((S, D), F32),
                         jax.ShapeDtypeStruct((1, D), F32)),
        grid=(S // tr,), in_specs=[row, vec, row], out_specs=(lt, row, vec), name=name,
        compiler_params=_cparams("arbitrary"))(x, g, tgt)


def _tri(n, fn):
    r = lax.broadcasted_iota(jnp.int32, (n, n), 0)
    c = lax.broadcasted_iota(jnp.int32, (n, n), 1)
    return jnp.where(fn(r, c), 1.0, 0.0).astype(F32)


def _log_sigmoid(z):
    return jnp.minimum(z, 0.0) - jnp.log1p(jnp.exp(-jnp.abs(z)))


def fox_gate_fwd(name, ft, b):
    H, R, _ = ft.shape

    def body(f_ref, b_ref, o_ref):
        ls = _log_sigmoid(f_ref[...] + b_ref[...])
        cum = jnp.dot(ls, _tri(LANES, lambda r, c: r <= c), precision=HIGHEST, preferred_element_type=F32)
        tot = jnp.broadcast_to(cum[:, LANES - 1:LANES], (R, LANES))
        off = jnp.dot(_tri(R, lambda r, c: r > c), tot, precision=HIGHEST, preferred_element_type=F32)
        o_ref[...] = -(cum + off)

    blk = pl.BlockSpec((None, R, LANES), lambda h: (h, 0, 0))
    return pl.pallas_call(
        body, out_shape=jax.ShapeDtypeStruct((H, R, LANES), F32), grid=(H,),
        in_specs=[blk, pl.BlockSpec((None, 1, LANES), lambda h: (h, 0, 0))], out_specs=blk, name=name,
        compiler_params=_cparams("parallel"))(ft, b)


def fox_gate_bwd(name, ft, b, dkb):
    H, R, _ = ft.shape

    def body(f_ref, b_ref, d_ref, df_ref, db_ref):
        z = f_ref[...] + b_ref[...]
        d = d_ref[...]
        rev = jnp.dot(d, _tri(LANES, lambda r, c: r >= c), precision=HIGHEST, preferred_element_type=F32)
        tot = jnp.broadcast_to(rev[:, 0:1], (R, LANES))
        off = jnp.dot(_tri(R, lambda r, c: r < c), tot, precision=HIGHEST, preferred_element_type=F32)
        dls = -(rev + off)
        dz = dls * jax.nn.sigmoid(-z)
        df_ref[...] = dz
        s = jnp.sum(jnp.sum(dz, axis=1, keepdims=True), axis=0, keepdims=True)
        db_ref[...] = jnp.broadcast_to(s, (1, LANES))

    blk = pl.BlockSpec((None, R, LANES), lambda h: (h, 0, 0))
    vec = pl.BlockSpec((None, 1, LANES), lambda h: (h, 0, 0))
    return pl.pallas_call(
        body, out_shape=(jax.ShapeDtypeStruct((H, R, LANES), F32), jax.ShapeDtypeStruct((H, 1, LANES), F32)),
        grid=(H,), in_specs=[blk, vec, blk], out_specs=(blk, vec), name=name,
        compiler_params=_cparams("parallel"))(ft, b, dkb)


def _ret_log_gamma(h):
    lg = [float(np.log(np.float32(1.0) - np.float32(2.0) ** np.float32(-5.0 - i))) for i in range(RET_HEADS)]
    out = jnp.float32(lg[RET_HEADS - 1])
    for i in range(RET_HEADS - 2, -1, -1):
        out = jnp.where(h == i, jnp.float32(lg[i]), out)
    return out


def _visible(mode, B):
    r = lax.broadcasted_iota(jnp.int32, (B, B), 0)
    c = lax.broadcasted_iota(jnp.int32, (B, B), 1)
    if mode == "fox":
        return c <= r
    return (c // CHUNK) <= (r // CHUNK)


def _decay(lg, B, blocks_apart):
    r = lax.broadcasted_iota(jnp.int32, (B, B), 0)
    c = lax.broadcasted_iota(jnp.int32, (B, B), 1)
    dist = jnp.abs(r - c + blocks_apart * B).astype(F32)
    return jnp.exp(lg * dist)


def _attn_block(S):
    return 512 if S >= 2048 else 128


def attn_fwd(name, mode, q1, q1_cb, k1, k1_cb, v, v_cb, H, dv, scale, q2=None, q2_cb=0, k2=None, kbias=None):
    S = q1.shape[0]
    B = _attn_block(S)
    nq = S // B
    softmax = mode != "ret"
    two = mode == "mla"
    has_bias = mode == "fox"

    def body(*refs):
        it = iter(refs)
        q1_ref, k1_ref, v_ref = next(it), next(it), next(it)
        q2_ref = next(it) if two else None
        k2_ref = next(it) if two else None
        kb_ref = next(it) if has_bias else None
        o_ref = next(it)
        lse_ref = next(it) if softmax else None
        kbuf, vbuf = next(it), next(it)
        k2buf = next(it) if two else None
        acc = next(it)
        m_ref = next(it) if softmax else None
        l_ref = next(it) if softmax else None
        s_all = next(it) if softmax else None
        h = pl.program_id(0)
        i = pl.program_id(1)

        @pl.when(i == 0)
        def _():
            kbuf[...] = k1_ref[...].astype(BF16)
            vbuf[...] = v_ref[...].astype(BF16)
            if two:
                k2buf[...] = k2_ref[...].astype(BF16)

        qb = q1_ref[...].astype(BF16)
        q2b = q2_ref[...].astype(BF16) if two else None
        lg = _ret_log_gamma(h) if mode == "ret" else None
        acc[...] = jnp.zeros_like(acc)
        if softmax:
            m_ref[...] = jnp.full_like(m_ref, NEG_BIG)
            l_ref[...] = jnp.zeros_like(l_ref)

        def scores(g, diag):
            rows = slice(g * B, (g + 1) * B)
            s = lax.dot_general(qb, kbuf[rows, :], NT_DIMS, preferred_element_type=F32)
            if two:
                s = s + lax.dot_general(q2b, k2buf[rows, :], NT_DIMS, preferred_element_type=F32)
            if softmax:
                s = s * scale
                if has_bias:
                    s = s + kb_ref[g]
                if diag:
                    s = jnp.where(_visible(mode, B), s, NEG_BIG)
                s_all[:, rows] = s
                m_ref[...] = jnp.maximum(m_ref[...], jnp.max(s, axis=1, keepdims=True))
            else:
                if diag:
                    p = jnp.where(_visible(mode, B), s * _decay(lg, B, 0), 0.0)
                else:
                    p = s * _decay(lg, B, i - g)
                acc[...] += jnp.dot(p.astype(BF16), vbuf[rows, :], preferred_element_type=F32)

        def weighted(g):
            rows = slice(g * B, (g + 1) * B)
            p = jnp.exp(s_all[:, rows] - m_ref[...])
            l_ref[...] += jnp.sum(p, axis=1, keepdims=True)
            acc[...] += jnp.dot(p.astype(BF16), vbuf[rows, :], preferred_element_type=F32)

        for g in range(nq):
            pl.when(g < i)(functools.partial(scores, g, False))
            pl.when(g == i)(functools.partial(scores, g, True))
        if softmax:
            for g in range(nq):
                pl.when(g <= i)(functools.partial(weighted, g))
            o_ref[...] = acc[...] / l_ref[...]
            lse_ref[...] = jnp.broadcast_to(m_ref[...] + jnp.log(l_ref[...]), (B, LANES))
        else:
            o_ref[...] = acc[...]

    in_specs = [pl.BlockSpec((B, LANES), lambda h, i: (i, q1_cb + h)),
                pl.BlockSpec((S, LANES), lambda h, i: (0, k1_cb + h)),
                pl.BlockSpec((S, dv), lambda h, i: (0, v_cb + h))]
    args = [q1, k1, v]
    if two:
        in_specs += [pl.BlockSpec((B, LANES), lambda h, i: (i, q2_cb + h)),
                     pl.BlockSpec((S, LANES), lambda h, i: (0, 0))]
        args += [q2, k2]
    if has_bias:
        in_specs.append(pl.BlockSpec((None, nq, 1, B), lambda h, i: (h, 0, 0, 0)))
        args.append(kbias)
    out_shape = [jax.ShapeDtypeStruct((S, H * dv), F32)]
    out_specs = [pl.BlockSpec((B, dv), lambda h, i: (i, h))]
    if softmax:
        out_shape.append(jax.ShapeDtypeStruct((S, H * LANES), F32))
        out_specs.append(pl.BlockSpec((B, LANES), lambda h, i: (i, h)))
    scratch = [pltpu.VMEM((S, LANES), BF16), pltpu.VMEM((S, dv), BF16)]
    if two:
        scratch.append(pltpu.VMEM((S, LANES), BF16))
    scratch.append(pltpu.VMEM((B, dv), F32))
    if softmax:
        scratch += [pltpu.VMEM((B, 1), F32), pltpu.VMEM((B, 1), F32), pltpu.VMEM((B, S), F32)]
    res = pl.pallas_call(body, out_shape=tuple(out_shape), grid=(H, nq), in_specs=in_specs,
                         out_specs=tuple(out_specs), scratch_shapes=scratch, name=name,
                         compiler_params=_cparams("parallel", "arbitrary"))(*args)
    return res if softmax else (res[0], None)


def attn_bwd(name, mode, q1, q1_cb, k1, k1_cb, v, v_cb, H, dv, scale, do, o=None, lse=None,
             q2=None, q2_cb=0, k2=None, kbias=None):
    S = q1.shape[0]
    B = _attn_block(S)
    nb = S // B
    softmax = mode != "ret"
    two = mode == "mla"
    has_bias = mode == "fox"

    def body(*refs):
        it = iter(refs)
        q1_ref, k1_ref, v_ref, do_ref = next(it), next(it), next(it), next(it)
        o_ref = next(it) if softmax else None
        lse_ref = next(it) if softmax else None
        q2_ref = next(it) if two else None
        k2_ref = next(it) if two else None
        kb_ref = next(it) if has_bias else None
        dq1_ref, dk1_ref, dv_ref = next(it), next(it), next(it)
        dq2_ref = next(it) if two else None
        dk2_ref = next(it) if two else None
        dkb_ref = next(it) if has_bias else None
        drow_ref = next(it) if has_bias else None
        qbuf, dobuf = next(it), next(it)
        q2buf = next(it) if two else None
        delta = next(it) if softmax else None
        dk_acc, dv_acc = next(it), next(it)
        dk2_acc = next(it) if two else None
        dkb_acc = next(it) if has_bias else None
        h = pl.program_id(0)
        j = pl.program_id(1)

        @pl.when(j == 0)
        def _():
            qbuf[...] = q1_ref[...].astype(BF16)
            dobuf[...] = do_ref[...].astype(BF16)
            dq1_ref[...] = jnp.zeros_like(dq1_ref)
            if has_bias:
                drow_ref[...] = jnp.zeros_like(drow_ref)
            if two:
                q2buf[...] = q2_ref[...].astype(BF16)
                dq2_ref[...] = jnp.zeros_like(dq2_ref)
            if softmax:
                def drow(t, carry):
                    rows = pl.ds(pl.multiple_of(t * B, B), B)
                    delta[rows, :] = jnp.sum(do_ref[rows, :].astype(F32) * o_ref[rows, :], axis=1, keepdims=True)
                    return carry
                lax.fori_loop(0, nb, drow, 0)

        kj = k1_ref[...].astype(BF16)
        vj = v_ref[...].astype(BF16)
        k2j = k2_ref[...].astype(BF16) if two else None
        kbj = kb_ref[...] if has_bias else None
        lg = _ret_log_gamma(h) if mode == "ret" else None
        dk_acc[...] = jnp.zeros_like(dk_acc)
        dv_acc[...] = jnp.zeros_like(dv_acc)
        if two:
            dk2_acc[...] = jnp.zeros_like(dk2_acc)
        if has_bias:
            dkb_acc[...] = jnp.zeros_like(dkb_acc)

        def step(i, diag):
            rows = slice(i * B, (i + 1) * B)
            qi = qbuf[rows, :]
            doi = dobuf[rows, :]
            s = lax.dot_general(qi, kj, NT_DIMS, preferred_element_type=F32)
            if two:
                q2i = q2buf[rows, :]
                s = s + lax.dot_general(q2i, k2j, NT_DIMS, preferred_element_type=F32)
            dp = lax.dot_general(doi, vj, NT_DIMS, preferred_element_type=F32)
            if softmax:
                s = s * scale
                if has_bias:
                    s = s + kbj
                if diag:
                    s = jnp.where(_visible(mode, B), s, NEG_BIG)
                p = jnp.exp(s - lse_ref[rows, 0:1])
                ds = p * (dp - delta[rows, :])
                if has_bias:
                    dkb_acc[...] += jnp.sum(ds, axis=0, keepdims=True)
                    drow_ref[rows, :] += jnp.broadcast_to(jnp.sum(ds, axis=1, keepdims=True), (B, LANES))
                dsb = (ds * scale).astype(BF16)
            else:
                if diag:
                    dec = jnp.where(_visible(mode, B), _decay(lg, B, 0), 0.0)
                else:
                    dec = _decay(lg, B, i - j)
                p = s * dec
                dsb = (dp * dec).astype(BF16)
            dv_acc[...] += lax.dot_general(p.astype(BF16), doi, TN_DIMS, preferred_element_type=F32)
            dk_acc[...] += lax.dot_general(dsb, qi, TN_DIMS, preferred_element_type=F32)
            dq1_ref[rows, :] += jnp.dot(dsb, kj, preferred_element_type=F32)
            if two:
                dk2_acc[...] += lax.dot_general(dsb, q2i, TN_DIMS, preferred_element_type=F32)
                dq2_ref[rows, :] += jnp.dot(dsb, k2j, preferred_element_type=F32)

        for i in range(nb):
            pl.when(i == j)(functools.partial(step, i, True))
            pl.when(i > j)(functools.partial(step, i, False))
        dk1_ref[...] = dk_acc[...]
        dv_ref[...] = dv_acc[...]
        if two:
            dk2_ref[...] = dk2_acc[...]
        if has_bias:
            dkb_ref[...] = dkb_acc[...]

    full = lambda w, cb: pl.BlockSpec((S, w), lambda h, j: (0, cb + h))
    blk = lambda w, cb: pl.BlockSpec((B, w), lambda h, j: (j, cb + h))
    in_specs = [full(LANES, q1_cb), blk(LANES, k1_cb), blk(dv, v_cb), full(dv, 0)]
    args = [q1, k1, v, do]
    if softmax:
        in_specs += [full(dv, 0), full(LANES, 0)]
        args += [o, lse]
    if two:
        in_specs += [full(LANES, q2_cb), pl.BlockSpec((B, LANES), lambda h, j: (j, 0))]
        args += [q2, k2]
    if has_bias:
        in_specs.append(pl.BlockSpec((None, None, 1, B), lambda h, j: (h, j, 0, 0)))
        args.append(kbias)
    names = ["dq1", "dk1", "dv"]
    out_shape = [jax.ShapeDtypeStruct((S, H * LANES), F32), jax.ShapeDtypeStruct((S, H * LANES), F32),
                 jax.ShapeDtypeStruct((S, H * dv), F32)]
    out_specs = [full(LANES, 0), blk(LANES, 0), blk(dv, 0)]
    if two:
        names += ["dq2", "dk2h"]
        out_shape += [jax.ShapeDtypeStruct((S, H * LANES), F32)] * 2
        out_specs += [full(LANES, 0), blk(LANES, 0)]
    if has_bias:
        names.append("dkb")
        out_shape.append(jax.ShapeDtypeStruct((H, nb, 1, B), F32))
        out_specs.append(pl.BlockSpec((None, None, 1, B), lambda h, j: (h, j, 0, 0)))
        names.append("drow")
        out_shape.append(jax.ShapeDtypeStruct((S, H * LANES), F32))
        out_specs.append(full(LANES, 0))
    scratch = [pltpu.VMEM((S, LANES), BF16), pltpu.VMEM((S, dv), BF16)]
    if two:
        scratch.append(pltpu.VMEM((S, LANES), BF16))
    if softmax:
        scratch.append(pltpu.VMEM((S, 1), F32))
    scratch += [pltpu.VMEM((B, LANES), F32), pltpu.VMEM((B, dv), F32)]
    if two:
        scratch.append(pltpu.VMEM((B, LANES), F32))
    if has_bias:
        scratch.append(pltpu.VMEM((1, B), F32))
    res = pl.pallas_call(body, out_shape=tuple(out_shape), grid=(H, nb), in_specs=in_specs,
                         out_specs=tuple(out_specs), scratch_shapes=scratch, name=name,
                         compiler_params=_cparams("parallel", "arbitrary"))(*args)
    return dict(zip(names, res))


def head_sum(name, x, H, out_dtype):
    S = x.shape[0]
    tr = _pick(S, 512, 8)

    def body(x_ref, o_ref):
        acc = x_ref[:, 0:LANES]
        for h in range(1, H):
            acc = acc + x_ref[:, h * LANES:(h + 1) * LANES]
        o_ref[...] = acc.astype(o_ref.dtype)

    return pl.pallas_call(
        body, out_shape=jax.ShapeDtypeStruct((S, LANES), out_dtype), grid=(S // tr,),
        in_specs=[pl.BlockSpec((tr, H * LANES), lambda i: (i, 0))],
        out_specs=pl.BlockSpec((tr, LANES), lambda i: (i, 0)), name=name,
        compiler_params=_cparams("parallel"))(x)


def _mesh_pos():
    return lax.axis_index("x"), lax.axis_index("y"), lax.axis_index("c")


def _peer(pos, k):
    x, y, c = pos
    px = 1 - x if k & 4 else x
    py = 1 - y if k & 2 else y
    pc = 1 - c if k & 1 else c
    return (px, py, pc), 4 * px + 2 * py + pc


def exchange(name, tensors):
    nt = len(tensors)
    flat_in, counts = [], []
    out_shape = []
    for mode, srcs in tensors:
        counts.append(len(srcs))
        flat_in += list(srcs)
        rc = srcs[0].shape[-2:]
        out_shape.append(jax.ShapeDtypeStruct((len(srcs), N_DEV) + tuple(rc), srcs[0].dtype))
    n_in = len(flat_in)

    def body(*refs):
        ins = refs[:n_in]
        outs = refs[n_in:n_in + nt]
        send_sems, recv_sems, local_sems = refs[n_in + nt:]
        pos = _mesh_pos()
        me = 4 * pos[0] + 2 * pos[1] + pos[2]
        srcs_of, base = [], 0
        for t in range(nt):
            srcs_of.append(ins[base:base + counts[t]])
            base += counts[t]

        def src_view(t, l, slot):
            ref = srcs_of[t][l]
            return ref if tensors[t][0] == "gather" else ref.at[slot]

        def all_layers(t, slot):
            return outs[t].at[pl.ds(0, counts[t]), slot]

        for t in range(nt):
            for l in range(counts[t]):
                pltpu.make_async_copy(src_view(t, l, me), outs[t].at[l, me], local_sems.at[t]).start()
        for t in range(nt):
            for k in range(1, N_DEV):
                peer, pid = _peer(pos, k)
                for l in range(counts[t]):
                    pltpu.make_async_remote_copy(
                        src_ref=src_view(t, l, pid), dst_ref=outs[t].at[l, me],
                        send_sem=send_sems.at[t, k - 1], recv_sem=recv_sems.at[t, k - 1],
                        device_id=peer, device_id_type=pl.DeviceIdType.MESH).start()
        for t in range(nt):
            for k in range(1, N_DEV):
                peer, pid = _peer(pos, k)
                pltpu.make_async_remote_copy(
                    src_ref=all_layers(t, pid), dst_ref=all_layers(t, pid),
                    send_sem=send_sems.at[t, k - 1], recv_sem=recv_sems.at[t, k - 1],
                    device_id=peer, device_id_type=pl.DeviceIdType.MESH).wait()
        for t in range(nt):
            pltpu.make_async_copy(all_layers(t, me), all_layers(t, me), local_sems.at[t]).wait()

    any_spec = pl.BlockSpec(memory_space=pl.ANY)
    return pl.pallas_call(
        body, out_shape=tuple(out_shape), in_specs=[any_spec] * n_in, out_specs=tuple([any_spec] * nt),
        scratch_shapes=[pltpu.SemaphoreType.DMA((nt, N_DEV - 1)), pltpu.SemaphoreType.DMA((nt, N_DEV - 1)),
                        pltpu.SemaphoreType.DMA((nt,))],
        name=name)(*flat_in)


def reduce_parts(name, parts):
    L, n, R, C = parts.shape
    tr = _pick(R, max(8, (1 << 20) // (C * 4) // 8 * 8), 8)

    def body(p_ref, o_ref):
        acc = p_ref[0].astype(F32)
        for s in range(1, n):
            acc = acc + p_ref[s].astype(F32)
        o_ref[...] = acc

    return pl.pallas_call(
        body, out_shape=jax.ShapeDtypeStruct((L, R, C), F32), grid=(L, R // tr),
        in_specs=[pl.BlockSpec((None, n, tr, C), lambda l, i: (l, 0, i, 0))],
        out_specs=pl.BlockSpec((None, tr, C), lambda l, i: (l, i, 0)), name=name,
        compiler_params=_cparams("parallel", "parallel"))(parts)


def adamw(name, w, m, v, parts):
    L, R, C = w.shape
    n = parts.shape[1]
    tr = _pick(R, max(8, (1 << 19) // (C * 4) // 8 * 8), 8)

    def body(w_ref, m_ref, v_ref, p_ref, g_ref, d_ref, nm_ref, nv_ref):
        g = p_ref[0].astype(F32)
        for s in range(1, n):
            g = g + p_ref[s].astype(F32)
        wv = w_ref[...]
        mn = ADAM_B1 * m_ref[...] + (1.0 - ADAM_B1) * g
        vn = ADAM_B2 * v_ref[...] + (1.0 - ADAM_B2) * jnp.square(g)
        m_hat = mn / (1.0 - ADAM_B1 ** ADAM_STEP)
        v_hat = vn / (1.0 - ADAM_B2 ** ADAM_STEP)
        g_ref[...] = g
        d_ref[...] = -ADAM_LR * (m_hat / (jnp.sqrt(v_hat) + ADAM_EPS) + ADAM_WD * wv)
        nm_ref[...] = mn
        nv_ref[...] = vn

    blk = pl.BlockSpec((None, tr, C), lambda l, i: (l, i, 0))
    sh = jax.ShapeDtypeStruct((L, R, C), F32)
    return pl.pallas_call(
        body, out_shape=(sh, sh, sh, sh), grid=(L, R // tr),
        in_specs=[blk, blk, blk, pl.BlockSpec((None, n, tr, C), lambda l, i: (l, 0, i, 0))],
        out_specs=(blk, blk, blk, blk), name=name,
        compiler_params=_cparams("parallel", "parallel"))(w, m, v, parts)


def _cols_from_blocks(g):
    L, n, R, c = g.shape
    return g.transpose(0, 2, 1, 3).reshape(L, R, n * c)


def _cols_to_blocks(w):
    R, C = w.shape
    return w.reshape(R, N_DEV, C // N_DEV).transpose(1, 0, 2)


def _uq_permute(w):
    lead = w.shape[:-1]
    w4 = w.reshape(lead + (MLA_HEADS, MLA_NOPE + MLA_ROPE))
    nope = w4[..., :MLA_NOPE].reshape(lead + (MLA_HEADS * MLA_NOPE,))
    rope = jnp.pad(w4[..., MLA_NOPE:], [(0, 0)] * (w4.ndim - 1) + [(0, LANES - MLA_ROPE)])
    return jnp.concatenate([nope, rope.reshape(lead + (MLA_HEADS * LANES,))], axis=-1)


def _uq_unpermute(w):
    lead = w.shape[:-1]
    n = MLA_HEADS * MLA_NOPE
    nope = w[..., :n].reshape(lead + (MLA_HEADS, MLA_NOPE))
    rope = w[..., n:].reshape(lead + (MLA_HEADS, LANES))[..., :MLA_ROPE]
    return jnp.concatenate([nope, rope], axis=-1).reshape(lead + (MLA_HEADS * (MLA_NOPE + MLA_ROPE),))


def _ukv_permute(w):
    lead = w.shape[:-1]
    w4 = w.reshape(lead + (MLA_HEADS, 2, MLA_NOPE))
    return jnp.swapaxes(w4, -3, -2).reshape(lead + (2 * MLA_HEADS * MLA_NOPE,))


def _ukv_unpermute(w):
    lead = w.shape[:-1]
    w4 = w.reshape(lead + (2, MLA_HEADS, MLA_NOPE))
    return jnp.swapaxes(w4, -3, -2).reshape(lead + (2 * MLA_HEADS * MLA_NOPE,))


SMALL = ["norm1_g", "mla_q_norm_g", "mla_kv_norm_g", "fox_b_f", "norm2_g", "ffn_conv_b", "final_norm_g"]
SMALL_TILE = 8 * LANES


def _pack_small(d):
    flat = jnp.concatenate([d[n].reshape(-1).astype(F32) for n in SMALL])
    pad = -flat.shape[0] % SMALL_TILE
    return jnp.pad(flat, (0, pad)).reshape(-1, LANES)


def _unpack_small(packed, like):
    flat = packed.reshape(-1)
    out, o = {}, 0
    for n in SMALL:
        sz = int(np.prod(like[n].shape))
        out[n] = flat[o:o + sz].reshape(like[n].shape)
        o += sz
    return out


WEIGHTS = ["norm1_g", "w_in", "mla_q_norm_g", "mla_kv_norm_g", "mla_w_uq", "mla_w_ukv", "fox_b_f", "w_br_fox",
           "w_br_mla", "w_br_ret", "w_out", "norm2_g", "ffn_w_up", "ffn_w_gate", "ffn_conv_w", "ffn_conv_b",
           "ffn_w_down", "final_norm_g"]
BIG = ["w_in", "mla_w_uq", "mla_w_ukv", "w_br_fox", "w_br_mla", "w_br_ret", "w_out", "ffn_w_up", "ffn_w_gate",
       "ffn_conv_w", "ffn_w_down"]


def kernel(x, norm1_g, w_in, mla_q_norm_g, mla_kv_norm_g, mla_w_uq, mla_w_ukv, fox_b_f, w_br_fox, w_br_mla, w_br_ret, w_out, norm2_g, ffn_w_up, ffn_w_gate, ffn_conv_w, ffn_conv_b, ffn_w_down, final_norm_g, loss_target, m_norm1_g, m_w_in, m_mla_q_norm_g, m_mla_kv_norm_g, m_mla_w_uq, m_mla_w_ukv, m_fox_b_f, m_w_br_fox, m_w_br_mla, m_w_br_ret, m_w_out, m_norm2_g, m_ffn_w_up, m_ffn_w_gate, m_ffn_conv_w, m_ffn_conv_b, m_ffn_w_down, m_final_norm_g, v_norm1_g, v_w_in, v_mla_q_norm_g, v_mla_kv_norm_g, v_mla_w_uq, v_mla_w_ukv, v_fox_b_f, v_w_br_fox, v_w_br_mla, v_w_br_ret, v_w_out, v_norm2_g, v_ffn_w_up, v_ffn_w_gate, v_ffn_conv_w, v_ffn_conv_b, v_ffn_w_down, v_final_norm_g):
    env = dict(locals())
    W = {n: env[n] for n in WEIGHTS}
    Mo = {n: env["m_" + n] for n in WEIGHTS}
    Vo = {n: env["v_" + n] for n in WEIGHTS}
    S, D = x.shape[1], x.shape[2]
    L = w_in.shape[0]
    lay = InLayout(D)
    NP = lay.total
    f = ffn_w_up.shape[-1]
    xs = x.reshape(S, D)
    tgt = loss_target.reshape(S, D)

    local = {n: W[n].astype(BF16) for n in BIG}
    local["w_in"] = lay.permute(W["w_in"]).astype(BF16)
    gathered = exchange("gather_weights", [("gather", [local[n][l] for l in range(L)]) for n in BIG])
    G = dict(zip(BIG, gathered))
    Win = G["w_in"].reshape(L, D, NP)
    Wuq = _uq_permute(_cols_from_blocks(G["mla_w_uq"]))
    Wukv = _ukv_permute(_cols_from_blocks(G["mla_w_ukv"]))
    Wbf = _cols_from_blocks(G["w_br_fox"])
    Wbm = _cols_from_blocks(G["w_br_mla"])
    Wbr = _cols_from_blocks(G["w_br_ret"])
    Wout = G["w_out"].reshape(L, D, D)
    Wup, Wgate, Wdown, Wconv = G["ffn_w_up"], G["ffn_w_gate"], G["ffn_w_down"], G["ffn_conv_w"]
    Wconv = Wconv.astype(F32)
    cbias = ffn_conv_b.reshape(L, N_DEV, 1, f)

    tab64 = rope_tables(S, MLA_ROPE)
    tab128 = rope_tables(S, RET_DK)
    fox_scale = FOX_DH ** -0.5
    mla_scale = (MLA_NOPE + MLA_ROPE) ** -0.5
    ret_kscale = RET_DK ** -0.5
    R = S // LANES
    AB = _attn_block(S)
    NOPE_W = MLA_HEADS * MLA_NOPE

    def vec(a):
        return a.reshape(1, -1)

    saved = []
    xc = xs
    for l in range(L):
        s = {"x": xc}
        h1 = rms_fwd("norm1", xc, 0, D, vec(norm1_g[l]), BF16)
        P = mm_nn("in_proj", h1, Win, F32, b_lead=l)
        s.update(h1=h1, P=P)
        ff_off = lay.off["ff"]
        ft = P[:, ff_off:ff_off + FOX_HEADS].T.reshape(FOX_HEADS, R, LANES)
        bfl = jnp.broadcast_to(fox_b_f[l].reshape(FOX_HEADS, 1, 1), (FOX_HEADS, 1, LANES))
        kbias = fox_gate_fwd("fox_gate", ft, bfl).reshape(FOX_HEADS, S // AB, 1, AB)
        o_fox, lse_fox = attn_fwd("fox_attn", "fox", P, lay.cb("fq", LANES), P, lay.cb("fk", LANES),
                                  P, lay.cb("fv", LANES), FOX_HEADS, FOX_DH, fox_scale, kbias=kbias)
        s.update(ft=ft, bfl=bfl, kbias=kbias, o_fox=o_fox, lse_fox=lse_fox)
        cqn = rms_fwd("mla_q_norm", P, lay.cb("mq", MLA_Q_LORA), MLA_Q_LORA, vec(mla_q_norm_g[l]), BF16)
        qall = mm_nn("mla_uq", cqn, Wuq[l], F32)
        ckvn = rms_fwd("mla_kv_norm", P, lay.cb("mkv", MLA_KV_LORA), MLA_KV_LORA, vec(mla_kv_norm_g[l]), BF16)
        kvall = mm_nn("mla_ukv", ckvn, Wukv[l], F32)
        qrope = rope_apply("mla_q_rope", qall, NOPE_W // LANES, MLA_HEADS, tab64, 1.0, F32)
        krope = rope_apply("mla_k_rope", P, lay.cb("mkr", LANES), 1, tab64, 1.0, F32)
        o_mla, lse_mla = attn_fwd("mla_attn", "mla", qall, 0, kvall, 0, kvall, NOPE_W // MLA_V, MLA_HEADS, MLA_V,
                                  mla_scale, q2=qrope, q2_cb=0, k2=krope)
        s.update(cqn=cqn, qall=qall, ckvn=ckvn, kvall=kvall, qrope=qrope, krope=krope, o_mla=o_mla,
                 lse_mla=lse_mla)
        rq = rope_apply("ret_q_rope", P, lay.cb("rq", LANES), RET_HEADS, tab128, 1.0, F32)
        rk = rope_apply("ret_k_rope", P, lay.cb("rk", LANES), RET_HEADS, tab128, ret_kscale, F32)
        o_ret, _ = attn_fwd("ret_attn", "ret", rq, 0, rk, 0, P, lay.cb("rv", RET_DV), RET_HEADS, RET_DV, 1.0)
        c_ret = ret_out_fwd("ret_out", o_ret, P, lay.cb("rg", RET_DV), BF16)
        s.update(rq=rq, rk=rk, o_ret=o_ret, c_ret=c_ret)
        A = mm_nn("br_fox", o_fox, Wbf[l], F32)
        Bm = mm_nn("br_mla", o_mla, Wbm[l], F32)
        C = mm_nn("br_ret", c_ret, Wbr[l], F32)
        merged = merge_fwd("merge", P, lay.cb("gates", D), A, Bm, C, BF16)
        x2 = mm_nn("out_proj", merged, Wout, F32, b_lead=l, res=xc)
        s.update(A=A, Bm=Bm, C=C, merged=merged, x2=x2)
        h2 = rms_fwd("norm2", x2, 0, D, vec(norm2_g[l]), BF16)
        u = ffn_up("ffn_up", h2, Wup, l, F32)
        gt = ffn_up("ffn_gate", h2, Wgate, l, F32)
        act = ffn_act_fwd("ffn_act", u, gt, Wconv, cbias, l, BF16)
        xc = ffn_down("ffn_down", act, Wdown, l, x2, F32)
        s.update(h2=h2, u=u, gt=gt, act=act)
        saved.append(s)

    loss_tile, dx, dgf = loss_head("loss_head", xc, vec(final_norm_g), tgt)
    loss = lax.psum(loss_tile[0, 0], ("x", "y", "c"))

    gbig = {n: [None] * L for n in BIG}
    gsmall = {n: [None] * L for n in SMALL if n != "final_norm_g"}
    for l in reversed(range(L)):
        s = saved[l]
        P = s["P"]
        dxb = dx.astype(BF16)
        dact = ffn_down_bwd_act("ffn_down_da", dxb, Wdown, l, BF16)
        gbig["ffn_w_down"][l] = ffn_down_bwd_w("ffn_down_dw", s["act"], dxb, BF16)
        g, dgt = ffn_act_bwd_point("ffn_act_bwd", s["u"], s["gt"], Wconv, cbias, l, dact)
        du, dcw, dcb = ffn_act_bwd_conv("ffn_conv_bwd", s["u"], g, Wconv, l)
        gbig["ffn_conv_w"][l] = dcw.astype(BF16)
        gsmall["ffn_conv_b"][l] = dcb.reshape(-1)
        gbig["ffn_w_up"][l] = ffn_up_bwd_w("ffn_up_dw", s["h2"], du, BF16)
        gbig["ffn_w_gate"][l] = ffn_up_bwd_w("ffn_gate_dw", s["h2"], dgt, BF16)
        dh2 = ffn_up_bwd_h("ffn_up_dh", du, Wup, l, None, F32)
        dh2 = ffn_up_bwd_h("ffn_gate_dh", dgt, Wgate, l, dh2, BF16)
        dx2, dg2 = rms_bwd("norm2_bwd", s["x2"], 0, D, vec(norm2_g[l]), dh2, F32, res=dx)
        gsmall["norm2_g"][l] = dg2.reshape(-1)
        dx2b = dx2.astype(BF16)
        dmerged = mm_nt("out_proj_dm", dx2b, Wout, BF16, b_lead=l)
        gbig["w_out"][l] = mm_tn("out_proj_dw", s["merged"], dx2b, BF16).reshape(N_DEV, D // N_DEV, D)
        dgates, dA, dB, dC = merge_bwd("merge_bwd", P, lay.cb("gates", D), s["A"], s["Bm"], s["C"], dmerged)
        do_fox = mm_nt("br_fox_do", dA, Wbf[l], F32)
        do_mla = mm_nt("br_mla_do", dB, Wbm[l], F32)
        dc_ret = mm_nt("br_ret_do", dC, Wbr[l], BF16)
        gbig["w_br_fox"][l] = _cols_to_blocks(mm_tn("br_fox_dw", s["o_fox"], dA, BF16))
        gbig["w_br_mla"][l] = _cols_to_blocks(mm_tn("br_mla_dw", s["o_mla"], dB, BF16))
        gbig["w_br_ret"][l] = _cols_to_blocks(mm_tn("br_ret_dw", s["c_ret"], dC, BF16))
        do_ret, drg = ret_out_bwd("ret_out_bwd", s["o_ret"], P, lay.cb("rg", RET_DV), dc_ret)
        rb = attn_bwd("ret_attn_bwd", "ret", s["rq"], 0, s["rk"], 0, P, lay.cb("rv", RET_DV), RET_HEADS, RET_DV,
                      1.0, do_ret)
        drq = rope_apply("ret_q_rope_bwd", rb["dq1"], 0, RET_HEADS, tab128, 1.0, BF16, transpose=True)
        drk = rope_apply("ret_k_rope_bwd", rb["dk1"], 0, RET_HEADS, tab128, ret_kscale, BF16, transpose=True)
        drv = rb["dv"].astype(BF16)
        mb = attn_bwd("mla_attn_bwd", "mla", s["qall"], 0, s["kvall"], 0, s["kvall"], NOPE_W // MLA_V, MLA_HEADS,
                      MLA_V, mla_scale, do_mla, o=s["o_mla"], lse=s["lse_mla"], q2=s["qrope"], q2_cb=0,
                      k2=s["krope"])
        dqrope = rope_apply("mla_q_rope_bwd", mb["dq2"], 0, MLA_HEADS, tab64, 1.0, BF16, transpose=True)
        dkr_sum = head_sum("mla_k_rope_sum", mb["dk2h"], MLA_HEADS, F32)
        dmkr = rope_apply("mla_k_rope_bwd", dkr_sum, 0, 1, tab64, 1.0, BF16, transpose=True)
        dqall = jnp.concatenate([mb["dq1"].astype(BF16), dqrope], axis=1)
        dkvall = jnp.concatenate([mb["dk1"].astype(BF16), mb["dv"].astype(BF16)], axis=1)
        dcqn = mm_nt("mla_uq_dx", dqall, Wuq[l], F32)
        dckvn = mm_nt("mla_ukv_dx", dkvall, Wukv[l], F32)
        guq = _uq_unpermute(mm_tn("mla_uq_dw", s["cqn"], dqall, BF16))
        gukv = _ukv_unpermute(mm_tn("mla_ukv_dw", s["ckvn"], dkvall, BF16))
        gbig["mla_w_uq"][l] = _cols_to_blocks(guq)
        gbig["mla_w_ukv"][l] = _cols_to_blocks(gukv)
        dmq, dgq = rms_bwd("mla_q_norm_bwd", P, lay.cb("mq", MLA_Q_LORA), MLA_Q_LORA, vec(mla_q_norm_g[l]),
                           dcqn, BF16)
        dmkv, dgkv = rms_bwd("mla_kv_norm_bwd", P, lay.cb("mkv", MLA_KV_LORA), MLA_KV_LORA,
                             vec(mla_kv_norm_g[l]), dckvn, BF16)
        gsmall["mla_q_norm_g"][l] = dgq.reshape(-1)
        gsmall["mla_kv_norm_g"][l] = dgkv.reshape(-1)
        fb = attn_bwd("fox_attn_bwd", "fox", P, lay.cb("fq", LANES), P, lay.cb("fk", LANES), P,
                      lay.cb("fv", LANES), FOX_HEADS, FOX_DH, fox_scale, do_fox, o=s["o_fox"], lse=s["lse_fox"],
                      kbias=s["kbias"])
        drow = fb["drow"].reshape(S, FOX_HEADS, LANES)[:, :, 0].T.reshape(FOX_HEADS, R, LANES)
        dft, dbf = fox_gate_bwd("fox_gate_bwd", s["ft"], s["bfl"], fb["dkb"].reshape(FOX_HEADS, R, LANES) - drow)
        gsmall["fox_b_f"][l] = dbf[:, 0, 0]
        dff = jnp.pad(dft.reshape(FOX_HEADS, S).T, ((0, 0), (0, LANES - FOX_HEADS))).astype(BF16)
        segs = dict(gates=dgates, rv=drv, rg=drg, mq=dmq, rq=drq, rk=drk, mkv=dmkv, fq=fb["dq1"].astype(BF16),
                    fk=fb["dk1"].astype(BF16), fv=fb["dv"].astype(BF16), mkr=dmkr, ff=dff)
        dP = jnp.concatenate([segs[n] for n in lay.order], axis=1)
        dh1 = mm_nt("in_proj_dh", dP, Win, BF16, b_lead=l)
        gbig["w_in"][l] = mm_tn("in_proj_dw", s["h1"], dP, BF16).reshape(N_DEV, D // N_DEV, NP)
        dx, dg1 = rms_bwd("norm1_bwd", s["x"], 0, D, vec(norm1_g[l]), dh1, F32, res=dx2)
        gsmall["norm1_g"][l] = dg1.reshape(-1)

    small_like = {n: W[n] for n in SMALL}
    small_part = {n: jnp.stack(gsmall[n]) for n in gsmall}
    small_part["final_norm_g"] = dgf.reshape(-1)
    packed_part = _pack_small(small_part)
    received = exchange("reduce_grads", [("scatter", gbig[n]) for n in BIG] + [("gather", [packed_part])])
    recv = dict(zip(BIG, received[:-1]))
    small_recv = received[-1]

    out = {}
    for n in BIG:
        parts = recv[n]
        if n == "w_in":
            parts = lay.unpermute(reduce_parts("w_in_grad_sum", parts))[:, None]
        out[n] = adamw("adamw_" + n, W[n], Mo[n], Vo[n], parts)
    ps = adamw("adamw_small", _pack_small(small_like)[None], _pack_small({n: Mo[n] for n in SMALL})[None],
               _pack_small({n: Vo[n] for n in SMALL})[None], small_recv)
    small_out = [_unpack_small(a[0], small_like) for a in ps]
    for n in SMALL:
        out[n] = tuple(so[n] for so in small_out)

    grads = [out[n][0] for n in WEIGHTS]
    deltas = [out[n][1] for n in WEIGHTS]
    new_m = [out[n][2] for n in WEIGHTS]
    new_v = [out[n][3] for n in WEIGHTS]
    return (loss, dx.reshape(1, S, D), *grads, *deltas, *new_m, *new_v)
```

```python
import functools
import math

import numpy as np
import jax
import jax.numpy as jnp
from jax import lax
from jax.experimental import pallas as pl
from jax.experimental.pallas import tpu as pltpu

F32 = jnp.float32
BF16 = jnp.bfloat16

CHUNK = 64
NORM_EPS = 1e-6
ROPE_THETA = 10000.0
FOX_HEADS, FOX_DH = 6, 128
FOX_W = FOX_HEADS * FOX_DH
MLA_HEADS, MLA_NOPE, MLA_ROPE, MLA_V = 6, 128, 64, 128
MLA_Q_LORA, MLA_KV_LORA = 512, 256
MLA_W = MLA_HEADS * MLA_V
RET_HEADS, RET_DK, RET_DV = 4, 128, 256
RET_QK_W, RET_V_W = RET_HEADS * RET_DK, RET_HEADS * RET_DV
ADAM_LR, ADAM_B1, ADAM_B2, ADAM_EPS, ADAM_WD, ADAM_STEP = 0.001, 0.9, 0.999, 1e-08, 0.01, 10

N_DEV = 8
LANES = 128
V7X_VMEM_LIMIT_BYTES = 52 * 1024 * 1024
NEG_BIG = -1e30
HIGHEST = lax.Precision.HIGHEST

NT_DIMS = (((1,), (1,)), ((), ()))
TN_DIMS = (((0,), (0,)), ((), ()))
NN_DIMS = (((1,), (0,)), ((), ()))


def _pick(n, cap, mult=LANES):
    best = None
    for t in range(mult, min(n, cap) + 1, mult):
        if n % t == 0:
            best = t
    return n if best is None else best


def _cparams(*sem):
    return pltpu.CompilerParams(dimension_semantics=sem, vmem_limit_bytes=V7X_VMEM_LIMIT_BYTES)


class InLayout:
    def __init__(self, d_model):
        d = d_model
        self.d = d
        orig = dict(fq=(0, FOX_W), fk=(FOX_W, FOX_W), fv=(2 * FOX_W, FOX_W), ff=(3 * FOX_W, FOX_HEADS))
        o = 3 * FOX_W + FOX_HEADS
        for name, w in (("mq", MLA_Q_LORA), ("mkv", MLA_KV_LORA), ("mkr", MLA_ROPE), ("rq", RET_QK_W),
                        ("rk", RET_QK_W), ("rv", RET_V_W), ("rg", RET_V_W), ("gates", 3 * d)):
            orig[name] = (o, w)
            o += w
        self.orig = orig
        self.orig_width = o
        order = ["gates", "rv", "rg", "mq", "rq", "rk", "mkv", "fq", "fk", "fv", "mkr", "ff"]
        self.order = order
        self.off, self.width = {}, {}
        p = 0
        for name in order:
            w = orig[name][1]
            wp = -(-w // LANES) * LANES
            self.off[name], self.width[name] = p, wp
            p += wp
        self.total = p

    def cb(self, name, block):
        assert self.off[name] % block == 0, (name, block)
        return self.off[name] // block

    def permute(self, w):
        parts = []
        for name in self.order:
            o, n = self.orig[name]
            seg = w[..., o:o + n]
            pad = self.width[name] - n
            if pad:
                seg = jnp.pad(seg, [(0, 0)] * (w.ndim - 1) + [(0, pad)])
            parts.append(seg)
        return jnp.concatenate(parts, axis=-1)

    def unpermute(self, w):
        names = sorted(self.orig, key=lambda n: self.orig[n][0])
        return jnp.concatenate([w[..., self.off[n]:self.off[n] + self.orig[n][1]] for n in names], axis=-1)


def _mm(name, a, b, out_shape, grid, a_spec, b_spec, o_spec, dims, acc_shape, res=None):
    nk = grid[-1]
    has_res = res is not None

    def body(*refs):
        if has_res:
            a_ref, b_ref, r_ref, o_ref = refs[:4]
        else:
            a_ref, b_ref, o_ref = refs[:3]
            r_ref = None
        prod = lax.dot_general(a_ref[...].astype(BF16), b_ref[...].astype(BF16), dims,
                               preferred_element_type=F32)
        if nk == 1:
            if has_res:
                prod = prod + r_ref[...].astype(F32)
            o_ref[...] = prod.astype(o_ref.dtype)
        else:
            acc_ref = refs[-1]
            k = pl.program_id(len(grid) - 1)

            @pl.when(k == 0)
            def _():
                acc_ref[...] = prod

            @pl.when(k > 0)
            def _():
                acc_ref[...] += prod

            @pl.when(k == nk - 1)
            def _():
                r = acc_ref[...]
                if has_res:
                    r = r + r_ref[...].astype(F32)
                o_ref[...] = r.astype(o_ref.dtype)

    in_specs = [a_spec, b_spec] + ([o_spec] if has_res else [])
    args = (a, b) + ((res,) if has_res else ())
    scratch = [pltpu.VMEM(acc_shape, F32)] if nk > 1 else []
    sem = ("parallel",) * (len(grid) - 1) + ("arbitrary",)
    return pl.pallas_call(body, out_shape=out_shape, grid=grid, in_specs=in_specs, out_specs=o_spec,
                          scratch_shapes=scratch, name=name, compiler_params=_cparams(*sem))(*args)


def mm_nn(name, a, b, out_dtype, b_lead=None, res=None):
    M, K = a.shape
    N = b.shape[-1]
    tm, tn, tk = _pick(M, 1024, 8), _pick(N, 1024), _pick(K, 2048)
    grid = (M // tm, N // tn, K // tk)
    a_spec = pl.BlockSpec((tm, tk), lambda i, j, k: (i, k))
    if b_lead is None:
        b_spec = pl.BlockSpec((tk, tn), lambda i, j, k: (k, j))
    else:
        b_spec = pl.BlockSpec((None, tk, tn), lambda i, j, k: (b_lead, k, j))
    o_spec = pl.BlockSpec((tm, tn), lambda i, j, k: (i, j))
    return _mm(name, a, b, jax.ShapeDtypeStruct((M, N), out_dtype), grid, a_spec, b_spec, o_spec,
               NN_DIMS, (tm, tn), res)


def mm_nt(name, a, b, out_dtype, b_lead=None, res=None):
    M, N = a.shape
    K = b.shape[-2]
    tm, tko, tk = _pick(M, 1024, 8), _pick(K, 1024), _pick(N, 2048)
    grid = (M // tm, K // tko, N // tk)
    a_spec = pl.BlockSpec((tm, tk), lambda i, j, k: (i, k))
    if b_lead is None:
        b_spec = pl.BlockSpec((tko, tk), lambda i, j, k: (j, k))
    else:
        b_spec = pl.BlockSpec((None, tko, tk), lambda i, j, k: (b_lead, j, k))
    o_spec = pl.BlockSpec((tm, tko), lambda i, j, k: (i, j))
    return _mm(name, a, b, jax.ShapeDtypeStruct((M, K), out_dtype), grid, a_spec, b_spec, o_spec,
               NT_DIMS, (tm, tko), res)


def mm_tn(name, a, b, out_dtype):
    M, K = a.shape
    N = b.shape[-1]
    cap = 4096 if (a.dtype == BF16 and b.dtype == BF16) else 2048
    tko, tn, tk = _pick(K, 1024), _pick(N, 1024), _pick(M, cap, 8)
    grid = (K // tko, N // tn, M // tk)
    a_spec = pl.BlockSpec((tk, tko), lambda i, j, k: (k, i))
    b_spec = pl.BlockSpec((tk, tn), lambda i, j, k: (k, j))
    o_spec = pl.BlockSpec((tko, tn), lambda i, j, k: (i, j))
    return _mm(name, a, b, jax.ShapeDtypeStruct((K, N), out_dtype), grid, a_spec, b_spec, o_spec,
               TN_DIMS, (tko, tn))


def ffn_up(name, h, w, l, out_dtype):
    M, D = h.shape
    f = w.shape[-1]
    tm = _pick(M, 1024, 8)
    grid = (M // tm, N_DEV, 1)
    return _mm(name, h, w, jax.ShapeDtypeStruct((N_DEV, M, f), out_dtype), grid,
               pl.BlockSpec((tm, D), lambda i, j, k: (i, 0)),
               pl.BlockSpec((None, None, D, f), lambda i, j, k: (l, j, 0, 0)),
               pl.BlockSpec((None, tm, f), lambda i, j, k: (j, i, 0)), NN_DIMS, (tm, f))


def ffn_down(name, act, w, l, res, out_dtype):
    _, M, f = act.shape
    D = w.shape[-1]
    tm, tn = _pick(M, 1024, 8), _pick(D, 1024)
    grid = (M // tm, D // tn, N_DEV)
    return _mm(name, act, w, jax.ShapeDtypeStruct((M, D), out_dtype), grid,
               pl.BlockSpec((None, tm, f), lambda i, j, k: (k, i, 0)),
               pl.BlockSpec((None, None, f, tn), lambda i, j, k: (l, k, 0, j)),
               pl.BlockSpec((tm, tn), lambda i, j, k: (i, j)), NN_DIMS, (tm, tn), res)


def ffn_down_bwd_act(name, dy, w, l, out_dtype):
    M, D = dy.shape
    f = w.shape[-2]
    tm = _pick(M, 1024, 8)
    grid = (M // tm, N_DEV, 1)
    return _mm(name, dy, w, jax.ShapeDtypeStruct((N_DEV, M, f), out_dtype), grid,
               pl.BlockSpec((tm, D), lambda i, j, k: (i, 0)),
               pl.BlockSpec((None, None, f, D), lambda i, j, k: (l, j, 0, 0)),
               pl.BlockSpec((None, tm, f), lambda i, j, k: (j, i, 0)), NT_DIMS, (tm, f))


def ffn_down_bwd_w(name, act, dy, out_dtype):
    _, M, f = act.shape
    D = dy.shape[-1]
    tn, tk = _pick(D, 1024), _pick(M, 4096 if dy.dtype == BF16 else 2048, 8)
    grid = (N_DEV, D // tn, M // tk)
    return _mm(name, act, dy, jax.ShapeDtypeStruct((N_DEV, f, D), out_dtype), grid,
               pl.BlockSpec((None, tk, f), lambda j, n, k: (j, k, 0)),
               pl.BlockSpec((tk, tn), lambda j, n, k: (k, n)),
               pl.BlockSpec((None, f, tn), lambda j, n, k: (j, 0, n)), TN_DIMS, (f, tn))


def ffn_up_bwd_h(name, du, w, l, res, out_dtype):
    _, M, f = du.shape
    D = w.shape[-2]
    tm, tn = _pick(M, 1024, 8), _pick(D, 1024)
    grid = (M // tm, D // tn, N_DEV)
    return _mm(name, du, w, jax.ShapeDtypeStruct((M, D), out_dtype), grid,
               pl.BlockSpec((None, tm, f), lambda i, j, k: (k, i, 0)),
               pl.BlockSpec((None, None, tn, f), lambda i, j, k: (l, k, j, 0)),
               pl.BlockSpec((tm, tn), lambda i, j, k: (i, j)), NT_DIMS, (tm, tn), res)


def ffn_up_bwd_w(name, h, du, out_dtype):
    M, D = h.shape
    f = du.shape[-1]
    tko, tk = _pick(D, 1024), _pick(M, 4096, 8)
    grid = (N_DEV, D // tko, M // tk)
    return _mm(name, h, du, jax.ShapeDtypeStruct((N_DEV, D, f), out_dtype), grid,
               pl.BlockSpec((tk, tko), lambda j, n, k: (k, n)),
               pl.BlockSpec((None, tk, f), lambda j, n, k: (j, k, 0)),
               pl.BlockSpec((None, tko, f), lambda j, n, k: (j, n, 0)), TN_DIMS, (tko, f))


def _rms(xf, g):
    return xf * lax.rsqrt(jnp.mean(xf * xf, axis=-1, keepdims=True) + NORM_EPS) * g


def rms_fwd(name, x, cb, W, g, out_dtype):
    S = x.shape[0]
    tr = _pick(S, 256, 8)

    def body(x_ref, g_ref, o_ref):
        o_ref[...] = _rms(x_ref[...].astype(F32), g_ref[...]).astype(o_ref.dtype)

    return pl.pallas_call(
        body, out_shape=jax.ShapeDtypeStruct((S, W), out_dtype), grid=(S // tr,),
        in_specs=[pl.BlockSpec((tr, W), lambda i: (i, cb)), pl.BlockSpec((1, W), lambda i: (0, 0))],
        out_specs=pl.BlockSpec((tr, W), lambda i: (i, 0)), name=name, compiler_params=_cparams("parallel"))(x, g)


def rms_bwd(name, x, cb, W, g, dy, out_dtype, res=None):
    S = x.shape[0]
    tr = _pick(S, 256, 8)
    has_res = res is not None

    def body(*refs):
        if has_res:
            x_ref, g_ref, dy_ref, r_ref, dx_ref, dg_ref = refs
        else:
            x_ref, g_ref, dy_ref, dx_ref, dg_ref = refs
        _, vjp = jax.vjp(_rms, x_ref[...].astype(F32), g_ref[...])
        dx, dg = vjp(dy_ref[...].astype(F32))
        if has_res:
            dx = dx + r_ref[...]
        dx_ref[...] = dx.astype(dx_ref.dtype)

        @pl.when(pl.program_id(0) == 0)
        def _():
            dg_ref[...] = jnp.zeros_like(dg_ref)

        dg_ref[...] += dg

    row = pl.BlockSpec((tr, W), lambda i: (i, 0))
    vec = pl.BlockSpec((1, W), lambda i: (0, 0))
    in_specs = [pl.BlockSpec((tr, W), lambda i: (i, cb)), vec, row] + ([row] if has_res else [])
    args = (x, g, dy) + ((res,) if has_res else ())
    return pl.pallas_call(
        body, out_shape=(jax.ShapeDtypeStruct((S, W), out_dtype), jax.ShapeDtypeStruct((1, W), F32)),
        grid=(S // tr,), in_specs=in_specs, out_specs=(row, vec), name=name,
        compiler_params=_cparams("arbitrary"))(*args)


def rope_tables(S, d):
    pos = jnp.arange(S, dtype=F32)
    inv_freq = ROPE_THETA ** (-jnp.arange(0, d, 2, dtype=F32) / d)
    ang = pos[:, None] * inv_freq[None, :]
    cos, sin = jnp.cos(ang), jnp.sin(ang)
    half = d // 2
    z = jnp.zeros((S, LANES - d), F32)
    zh = jnp.zeros((S, half), F32)
    c = jnp.concatenate([cos, cos, z], axis=1)
    sa = jnp.concatenate([-sin, zh, z], axis=1)
    sb = jnp.concatenate([zh, sin, z], axis=1)
    return c, sa, sb, half


def rope_apply(name, x, cb, H, tabs, scale, out_dtype, transpose=False):
    c, sa, sb, half = tabs
    S = x.shape[0]
    tr = _pick(S, 512, 8)
    up, down = LANES - half, half

    def body(x_ref, c_ref, sa_ref, sb_ref, o_ref):
        xv = x_ref[...].astype(F32)
        if not transpose:
            y = xv * c_ref[...] + pltpu.roll(xv, up, 1) * sa_ref[...] + pltpu.roll(xv, down, 1) * sb_ref[...]
            y = y * scale
        else:
            xv = xv * scale
            y = (xv * c_ref[...] + pltpu.roll(xv * sa_ref[...], down, 1)
                 + pltpu.roll(xv * sb_ref[...], up, 1))
        o_ref[...] = y.astype(o_ref.dtype)

    tab = pl.BlockSpec((tr, LANES), lambda h, i: (i, 0))
    return pl.pallas_call(
        body, out_shape=jax.ShapeDtypeStruct((S, H * LANES), out_dtype), grid=(H, S // tr),
        in_specs=[pl.BlockSpec((tr, LANES), lambda h, i: (i, cb + h)), tab, tab, tab],
        out_specs=pl.BlockSpec((tr, LANES), lambda h, i: (i, h)), name=name,
        compiler_params=_cparams("parallel", "parallel"))(x, c, sa, sb)


def _ret_out(o, g):
    y = o * lax.rsqrt(jnp.mean(o * o, axis=-1, keepdims=True) + NORM_EPS)
    return y * jax.nn.silu(g)


def ret_out_fwd(name, o, gsrc, g_cb, out_dtype):
    S = o.shape[0]
    tr = _pick(S, 512, 8)
    W = RET_DV

    def body(o_ref, g_ref, y_ref):
        y_ref[...] = _ret_out(o_ref[...], g_ref[...].astype(F32)).astype(y_ref.dtype)

    blk = pl.BlockSpec((tr, W), lambda h, i: (i, h))
    return pl.pallas_call(
        body, out_shape=jax.ShapeDtypeStruct((S, RET_HEADS * W), out_dtype), grid=(RET_HEADS, S // tr),
        in_specs=[blk, pl.BlockSpec((tr, W), lambda h, i: (i, g_cb + h))], out_specs=blk, name=name,
        compiler_params=_cparams("parallel", "parallel"))(o, gsrc)


def ret_out_bwd(name, o, gsrc, g_cb, dy):
    S = o.shape[0]
    tr = _pick(S, 512, 8)
    W = RET_DV

    def body(o_ref, g_ref, dy_ref, do_ref, dg_ref):
        _, vjp = jax.vjp(_ret_out, o_ref[...], g_ref[...].astype(F32))
        do, dg = vjp(dy_ref[...].astype(F32))
        do_ref[...] = do.astype(do_ref.dtype)
        dg_ref[...] = dg.astype(dg_ref.dtype)

    blk = pl.BlockSpec((tr, W), lambda h, i: (i, h))
    return pl.pallas_call(
        body, out_shape=(jax.ShapeDtypeStruct((S, RET_HEADS * W), F32),
                         jax.ShapeDtypeStruct((S, RET_HEADS * W), BF16)),
        grid=(RET_HEADS, S // tr),
        in_specs=[blk, pl.BlockSpec((tr, W), lambda h, i: (i, g_cb + h)), blk], out_specs=(blk, blk),
        name=name, compiler_params=_cparams("parallel", "parallel"))(o, gsrc, dy)


def _merge(g0, g1, g2, a, b, c):
    return jax.nn.sigmoid(g0) * a + jax.nn.sigmoid(g1) * b + jax.nn.sigmoid(g2) * c


def merge_fwd(name, P, gates_cb, a, b, c, out_dtype):
    S, D = a.shape
    tr = _pick(S, 128, 8)

    def body(g0, g1, g2, a_ref, b_ref, c_ref, o_ref):
        o_ref[...] = _merge(g0[...], g1[...], g2[...], a_ref[...], b_ref[...], c_ref[...]).astype(o_ref.dtype)

    row = pl.BlockSpec((tr, D), lambda i: (i, 0))
    gs = [pl.BlockSpec((tr, D), lambda i, k=k: (i, gates_cb + k)) for k in range(3)]
    return pl.pallas_call(
        body, out_shape=jax.ShapeDtypeStruct((S, D), out_dtype), grid=(S // tr,),
        in_specs=gs + [row, row, row], out_specs=row, name=name,
        compiler_params=_cparams("parallel"))(P, P, P, a, b, c)


def merge_bwd(name, P, gates_cb, a, b, c, dm):
    S, D = a.shape
    tr = _pick(S, 128, 8)

    def body(g0, g1, g2, a_ref, b_ref, c_ref, dm_ref, dg_ref, da_ref, db_ref, dc_ref):
        _, vjp = jax.vjp(_merge, g0[...], g1[...], g2[...], a_ref[...], b_ref[...], c_ref[...])
        d0, d1, d2, da, db, dc = vjp(dm_ref[...].astype(F32))
        dg_ref[:, 0:D] = d0.astype(dg_ref.dtype)
        dg_ref[:, D:2 * D] = d1.astype(dg_ref.dtype)
        dg_ref[:, 2 * D:3 * D] = d2.astype(dg_ref.dtype)
        da_ref[...] = da.astype(da_ref.dtype)
        db_ref[...] = db.astype(db_ref.dtype)
        dc_ref[...] = dc.astype(dc_ref.dtype)

    row = pl.BlockSpec((tr, D), lambda i: (i, 0))
    gs = [pl.BlockSpec((tr, D), lambda i, k=k: (i, gates_cb + k)) for k in range(3)]
    bf = jax.ShapeDtypeStruct((S, D), BF16)
    return pl.pallas_call(
        body, out_shape=(jax.ShapeDtypeStruct((S, 3 * D), BF16), bf, bf, bf), grid=(S // tr,),
        in_specs=gs + [row, row, row, row],
        out_specs=(pl.BlockSpec((tr, 3 * D), lambda i: (i, 0)), row, row, row), name=name,
        compiler_params=_cparams("parallel"))(P, P, P, a, b, c, dm)


HALO = 8


def _conv_pre(u_ref, uh_ref, ext_ref, cw_ref, cb_ref, tr):
    i = pl.program_id(1)
    u = u_ref[...]
    ext_ref[0:HALO, :] = jnp.where(i > 0, uh_ref[...], 0.0)
    ext_ref[HALO:HALO + tr, :] = u
    u1 = ext_ref[HALO - 1:HALO - 1 + tr, :]
    u2 = ext_ref[HALO - 2:HALO - 2 + tr, :]
    cw = cw_ref[...]
    uc = cb_ref[...] + ((cw[0:1, :] * u2 + cw[1:2, :] * u1) + cw[2:3, :] * u)
    return u, u1, u2, uc


def _ffn_specs(S, f, tr, l):
    nb = tr // HALO
    row = pl.BlockSpec((None, tr, f), lambda j, i: (j, i, 0))
    prev = pl.BlockSpec((None, HALO, f), lambda j, i: (j, jnp.maximum(i * nb - 1, 0), 0))
    nxt = pl.BlockSpec((None, HALO, f), lambda j, i: (j, jnp.minimum((i + 1) * nb, S // HALO - 1), 0))
    cw = pl.BlockSpec((None, None, 3, f), lambda j, i: (l, j, 0, 0))
    cb = pl.BlockSpec((None, None, 1, f), lambda j, i: (l, j, 0, 0))
    return row, prev, nxt, cw, cb


def ffn_act_fwd(name, u, gt, cw, cb, l, out_dtype):
    _, S, f = u.shape
    tr = _pick(S, 512, 8)
    row, prev, _, cws, cbs = _ffn_specs(S, f, tr, l)

    def body(u_ref, uh_ref, gt_ref, cw_ref, cb_ref, o_ref, ext_ref):
        _, _, _, uc = _conv_pre(u_ref, uh_ref, ext_ref, cw_ref, cb_ref, tr)
        o_ref[...] = (jax.nn.gelu(uc) * gt_ref[...]).astype(o_ref.dtype)

    return pl.pallas_call(
        body, out_shape=jax.ShapeDtypeStruct((N_DEV, S, f), out_dtype), grid=(N_DEV, S // tr),
        in_specs=[row, prev, row, cws, cbs], out_specs=row,
        scratch_shapes=[pltpu.VMEM((tr + HALO, f), F32)], name=name,
        compiler_params=_cparams("parallel", "parallel"))(u, u, gt, cw, cb)


def ffn_act_bwd_point(name, u, gt, cw, cb, l, dact):
    _, S, f = u.shape
    tr = _pick(S, 512, 8)
    row, prev, _, cws, cbs = _ffn_specs(S, f, tr, l)

    def body(u_ref, uh_ref, gt_ref, cw_ref, cb_ref, da_ref, g_ref, dgt_ref, ext_ref):
        _, _, _, uc = _conv_pre(u_ref, uh_ref, ext_ref, cw_ref, cb_ref, tr)
        _, vjp = jax.vjp(lambda c, t: jax.nn.gelu(c) * t, uc, gt_ref[...])
        g, dgt = vjp(da_ref[...].astype(F32))
        g_ref[...] = g
        dgt_ref[...] = dgt.astype(dgt_ref.dtype)

    return pl.pallas_call(
        body, out_shape=(jax.ShapeDtypeStruct((N_DEV, S, f), F32), jax.ShapeDtypeStruct((N_DEV, S, f), BF16)),
        grid=(N_DEV, S // tr), in_specs=[row, prev, row, cws, cbs, row], out_specs=(row, row),
        scratch_shapes=[pltpu.VMEM((tr + HALO, f), F32)], name=name,
        compiler_params=_cparams("parallel", "parallel"))(u, u, gt, cw, cb, dact)


def ffn_act_bwd_conv(name, u, g, cw, l):
    _, S, f = u.shape
    tr = _pick(S, 512, 8)
    nt = S // tr
    row, prev, nxt, cws, _ = _ffn_specs(S, f, tr, l)

    def body(u_ref, uh_ref, g_ref, gn_ref, cw_ref, du_ref, dcw_ref, dcb_ref, ext_ref, gext_ref):
        i = pl.program_id(1)
        u = u_ref[...]
        ext_ref[0:HALO, :] = jnp.where(i > 0, uh_ref[...], 0.0)
        ext_ref[HALO:HALO + tr, :] = u
        u1 = ext_ref[HALO - 1:HALO - 1 + tr, :]
        u2 = ext_ref[HALO - 2:HALO - 2 + tr, :]
        g = g_ref[...]
        gext_ref[0:tr, :] = g
        gext_ref[tr:tr + HALO, :] = jnp.where(i < nt - 1, gn_ref[...], 0.0)
        g1 = gext_ref[1:1 + tr, :]
        g2 = gext_ref[2:2 + tr, :]
        cw = cw_ref[...]
        du_ref[...] = (cw[2:3, :] * g + cw[1:2, :] * g1 + cw[0:1, :] * g2).astype(du_ref.dtype)

        @pl.when(i == 0)
        def _():
            dcw_ref[...] = jnp.zeros_like(dcw_ref)
            dcb_ref[...] = jnp.zeros_like(dcb_ref)

        dcw_ref[0:1, :] += jnp.sum(g * u2, axis=0, keepdims=True)
        dcw_ref[1:2, :] += jnp.sum(g * u1, axis=0, keepdims=True)
        dcw_ref[2:3, :] += jnp.sum(g * u, axis=0, keepdims=True)
        dcb_ref[...] += jnp.sum(g, axis=0, keepdims=True)

    return pl.pallas_call(
        body, out_shape=(jax.ShapeDtypeStruct((N_DEV, S, f), BF16), jax.ShapeDtypeStruct((N_DEV, 3, f), F32),
                         jax.ShapeDtypeStruct((N_DEV, 1, f), F32)),
        grid=(N_DEV, nt), in_specs=[row, prev, row, nxt, cws],
        out_specs=(row, pl.BlockSpec((None, 3, f), lambda j, i: (j, 0, 0)),
                   pl.BlockSpec((None, 1, f), lambda j, i: (j, 0, 0))),
        scratch_shapes=[pltpu.VMEM((tr + HALO, f), F32), pltpu.VMEM((tr + HALO, f), F32)], name=name,
        compiler_params=_cparams("parallel", "arbitrary"))(u, u, g, g, cw)


def loss_head(name, x, g, tgt):
    S, D = x.shape
    tr = _pick(S, 256, 8)

    def body(x_ref, g_ref, t_ref, l_ref, dx_ref, dg_ref):
        tg = t_ref[...]

        def f(xv, gv):
            err = jnp.square(_rms(xv, gv) - tg)
            return 0.5 * jnp.sum(jnp.mean(err, axis=-1))

        val, vjp = jax.vjp(f, x_ref[...], g_ref[...])
        dx, dg = vjp(jnp.ones((), F32))
        dx_ref[...] = dx

        @pl.when(pl.program_id(0) == 0)
        def _():
            l_ref[...] = jnp.zeros_like(l_ref)
            dg_ref[...] = jnp.zeros_like(dg_ref)

        l_ref[...] += val
        dg_ref[...] += dg

    row = pl.BlockSpec((tr, D), lambda i: (i, 0))
    vec = pl.BlockSpec((1, D), lambda i: (0, 0))
    lt = pl.BlockSpec((8, LANES), lambda i: (0, 0))
    return pl.pallas_call(
        body, out_shape=(jax.ShapeDtypeStruct((8, LANES), F32), jax.ShapeDtypeStruct((S, D), F32),
                         jax.ShapeDtypeStruct((1, D), F32)),
        grid=(S // tr,), in_specs=[row, vec, row], out_specs=(lt, row, vec), name=name,
        compiler_params=_cparams("arbitrary"))(x, g, tgt)


def _tri(n, fn):
    r = lax.broadcasted_iota(jnp.int32, (n, n), 0)
    c = lax.broadcasted_iota(jnp.int32, (n, n), 1)
    return jnp.where(fn(r, c), 1.0, 0.0).astype(F32)


def _log_sigmoid(z):
    return jnp.minimum(z, 0.0) - jnp.log1p(jnp.exp(-jnp.abs(z)))


def fox_gate_fwd(name, ft, b):
    H, R, _ = ft.shape

    def body(f_ref, b_ref, o_ref):
        ls = _log_sigmoid(f_ref[...] + b_ref[...])
        cum = jnp.dot(ls, _tri(LANES, lambda r, c: r <= c), precision=HIGHEST, preferred_element_type=F32)
        tot = jnp.broadcast_to(cum[:, LANES - 1:LANES], (R, LANES))
        off = jnp.dot(_tri(R, lambda r, c: r > c), tot, precision=HIGHEST, preferred_element_type=F32)
        o_ref[...] = -(cum + off)

    blk = pl.BlockSpec((None, R, LANES), lambda h: (h, 0, 0))
    return pl.pallas_call(
        body, out_shape=jax.ShapeDtypeStruct((H, R, LANES), F32), grid=(H,),
        in_specs=[blk, pl.BlockSpec((None, 1, LANES), lambda h: (h, 0, 0))], out_specs=blk, name=name,
        compiler_params=_cparams("parallel"))(ft, b)


def fox_gate_bwd(name, ft, b, dkb):
    H, R, _ = ft.shape

    def body(f_ref, b_ref, d_ref, df_ref, db_ref):
        z = f_ref[...] + b_ref[...]
        d = d_ref[...]
        rev = jnp.dot(d, _tri(LANES, lambda r, c: r >= c), precision=HIGHEST, preferred_element_type=F32)
        tot = jnp.broadcast_to(rev[:, 0:1], (R, LANES))
        off = jnp.dot(_tri(R, lambda r, c: r < c), tot, precision=HIGHEST, preferred_element_type=F32)
        dls = -(rev + off)
        dz = dls * jax.nn.sigmoid(-z)
        df_ref[...] = dz
        s = jnp.sum(jnp.sum(dz, axis=1, keepdims=True), axis=0, keepdims=True)
        db_ref[...] = jnp.broadcast_to(s, (1, LANES))

    blk = pl.BlockSpec((None, R, LANES), lambda h: (h, 0, 0))
    vec = pl.BlockSpec((None, 1, LANES), lambda h: (h, 0, 0))
    return pl.pallas_call(
        body, out_shape=(jax.ShapeDtypeStruct((H, R, LANES), F32), jax.ShapeDtypeStruct((H, 1, LANES), F32)),
        grid=(H,), in_specs=[blk, vec, blk], out_specs=(blk, vec), name=name,
        compiler_params=_cparams("parallel"))(ft, b, dkb)


def _ret_log_gamma(h):
    lg = [float(np.log(np.float32(1.0) - np.float32(2.0) ** np.float32(-5.0 - i))) for i in range(RET_HEADS)]
    out = jnp.float32(lg[RET_HEADS - 1])
    for i in range(RET_HEADS - 2, -1, -1):
        out = jnp.where(h == i, jnp.float32(lg[i]), out)
    return out


def _visible(mode, B):
    r = lax.broadcasted_iota(jnp.int32, (B, B), 0)
    c = lax.broadcasted_iota(jnp.int32, (B, B), 1)
    if mode == "fox":
        return c <= r
    return (c // CHUNK) <= (r // CHUNK)


def _decay(lg, B, blocks_apart):
    r = lax.broadcasted_iota(jnp.int32, (B, B), 0)
    c = lax.broadcasted_iota(jnp.int32, (B, B), 1)
    dist = jnp.abs(r - c + blocks_apart * B).astype(F32)
    return jnp.exp(lg * dist)


def _attn_block(S):
    return 512 if S >= 2048 else 128


def attn_fwd(name, mode, q1, q1_cb, k1, k1_cb, v, v_cb, H, dv, scale, q2=None, q2_cb=0, k2=None, kbias=None):
    S = q1.shape[0]
    B = _attn_block(S)
    nq = S // B
    softmax = mode != "ret"
    two = mode == "mla"
    has_bias = mode == "fox"

    def body(*refs):
        it = iter(refs)
        q1_ref, k1_ref, v_ref = next(it), next(it), next(it)
        q2_ref = next(it) if two else None
        k2_ref = next(it) if two else None
        kb_ref = next(it) if has_bias else None
        o_ref = next(it)
        lse_ref = next(it) if softmax else None
        kbuf, vbuf = next(it), next(it)
        k2buf = next(it) if two else None
        acc = next(it)
        m_ref = next(it) if softmax else None
        l_ref = next(it) if softmax else None
        s_all = next(it) if softmax else None
        h = pl.program_id(0)
        i = pl.program_id(1)

        @pl.when(i == 0)
        def _():
            kbuf[...] = k1_ref[...].astype(BF16)
            vbuf[...] = v_ref[...].astype(BF16)
            if two:
                k2buf[...] = k2_ref[...].astype(BF16)

        qb = q1_ref[...].astype(BF16)
        q2b = q2_ref[...].astype(BF16) if two else None
        lg = _ret_log_gamma(h) if mode == "ret" else None
        acc[...] = jnp.zeros_like(acc)
        if softmax:
            m_ref[...] = jnp.full_like(m_ref, NEG_BIG)
            l_ref[...] = jnp.zeros_like(l_ref)

        def scores(g, diag):
            rows = slice(g * B, (g + 1) * B)
            s = lax.dot_general(qb, kbuf[rows, :], NT_DIMS, preferred_element_type=F32)
            if two:
                s = s + lax.dot_general(q2b, k2buf[rows, :], NT_DIMS, preferred_element_type=F32)
            if softmax:
                s = s * scale
                if has_bias:
                    s = s + kb_ref[g]
                if diag:
                    s = jnp.where(_visible(mode, B), s, NEG_BIG)
                s_all[:, rows] = s
                m_ref[...] = jnp.maximum(m_ref[...], jnp.max(s, axis=1, keepdims=True))
            else:
                if diag:
                    p = jnp.where(_visible(mode, B), s * _decay(lg, B, 0), 0.0)
                else:
                    p = s * _decay(lg, B, i - g)
                acc[...] += jnp.dot(p.astype(BF16), vbuf[rows, :], preferred_element_type=F32)

        def weighted(g):
            rows = slice(g * B, (g + 1) * B)
            p = jnp.exp(s_all[:, rows] - m_ref[...])
            l_ref[...] += jnp.sum(p, axis=1, keepdims=True)
            acc[...] += jnp.dot(p.astype(BF16), vbuf[rows, :], preferred_element_type=F32)

        for g in range(nq):
            pl.when(g < i)(functools.partial(scores, g, False))
            pl.when(g == i)(functools.partial(scores, g, True))
        if softmax:
            for g in range(nq):
                pl.when(g <= i)(functools.partial(weighted, g))
            o_ref[...] = acc[...] / l_ref[...]
            lse_ref[...] = jnp.broadcast_to(m_ref[...] + jnp.log(l_ref[...]), (B, LANES))
        else:
            o_ref[...] = acc[...]

    in_specs = [pl.BlockSpec((B, LANES), lambda h, i: (i, q1_cb + h)),
                pl.BlockSpec((S, LANES), lambda h, i: (0, k1_cb + h)),
                pl.BlockSpec((S, dv), lambda h, i: (0, v_cb + h))]
    args = [q1, k1, v]
    if two:
        in_specs += [pl.BlockSpec((B, LANES), lambda h, i: (i, q2_cb + h)),
                     pl.BlockSpec((S, LANES), lambda h, i: (0, 0))]
        args += [q2, k2]
    if has_bias:
        in_specs.append(pl.BlockSpec((None, nq, 1, B), lambda h, i: (h, 0, 0, 0)))
        args.append(kbias)
    out_shape = [jax.ShapeDtypeStruct((S, H * dv), F32)]
    out_specs = [pl.BlockSpec((B, dv), lambda h, i: (i, h))]
    if softmax:
        out_shape.append(jax.ShapeDtypeStruct((S, H * LANES), F32))
        out_specs.append(pl.BlockSpec((B, LANES), lambda h, i: (i, h)))
    scratch = [pltpu.VMEM((S, LANES), BF16), pltpu.VMEM((S, dv), BF16)]
    if two:
        scratch.append(pltpu.VMEM((S, LANES), BF16))
    scratch.append(pltpu.VMEM((B, dv), F32))
    if softmax:
        scratch += [pltpu.VMEM((B, 1), F32), pltpu.VMEM((B, 1), F32), pltpu.VMEM((B, S), F32)]
    res = pl.pallas_call(body, out_shape=tuple(out_shape), grid=(H, nq), in_specs=in_specs,
                         out_specs=tuple(out_specs), scratch_shapes=scratch, name=name,
                         compiler_params=_cparams("parallel", "arbitrary"))(*args)
    return res if softmax else (res[0], None)


def attn_bwd(name, mode, q1, q1_cb, k1, k1_cb, v, v_cb, H, dv, scale, do, o=None, lse=None,
             q2=None, q2_cb=0, k2=None, kbias=None):
    S = q1.shape[0]
    B = _attn_block(S)
    nb = S // B
    softmax = mode != "ret"
    two = mode == "mla"
    has_bias = mode == "fox"

    def body(*refs):
        it = iter(refs)
        q1_ref, k1_ref, v_ref, do_ref = next(it), next(it), next(it), next(it)
        o_ref = next(it) if softmax else None
        lse_ref = next(it) if softmax else None
        q2_ref = next(it) if two else None
        k2_ref = next(it) if two else None
        kb_ref = next(it) if has_bias else None
        dq1_ref, dk1_ref, dv_ref = next(it), next(it), next(it)
        dq2_ref = next(it) if two else None
        dk2_ref = next(it) if two else None
        dkb_ref = next(it) if has_bias else None
        drow_ref = next(it) if has_bias else None
        qbuf, dobuf = next(it), next(it)
        q2buf = next(it) if two else None
        delta = next(it) if softmax else None
        dk_acc, dv_acc = next(it), next(it)
        dk2_acc = next(it) if two else None
        dkb_acc = next(it) if has_bias else None
        h = pl.program_id(0)
        j = pl.program_id(1)

        @pl.when(j == 0)
        def _():
            qbuf[...] = q1_ref[...].astype(BF16)
            dobuf[...] = do_ref[...].astype(BF16)
            dq1_ref[...] = jnp.zeros_like(dq1_ref)
            if has_bias:
                drow_ref[...] = jnp.zeros_like(drow_ref)
            if two:
                q2buf[...] = q2_ref[...].astype(BF16)
                dq2_ref[...] = jnp.zeros_like(dq2_ref)
            if softmax:
                def drow(t, carry):
                    rows = pl.ds(pl.multiple_of(t * B, B), B)
                    delta[rows, :] = jnp.sum(do_ref[rows, :].astype(F32) * o_ref[rows, :], axis=1, keepdims=True)
                    return carry
                lax.fori_loop(0, nb, drow, 0)

        kj = k1_ref[...].astype(BF16)
        vj = v_ref[...].astype(BF16)
        k2j = k2_ref[...].astype(BF16) if two else None
        kbj = kb_ref[...] if has_bias else None
        lg = _ret_log_gamma(h) if mode == "ret" else None
        dk_acc[...] = jnp.zeros_like(dk_acc)
        dv_acc[...] = jnp.zeros_like(dv_acc)
        if two:
            dk2_acc[...] = jnp.zeros_like(dk2_acc)
        if has_bias:
            dkb_acc[...] = jnp.zeros_like(dkb_acc)

        def step(i, diag):
            rows = slice(i * B, (i + 1) * B)
            qi = qbuf[rows, :]
            doi = dobuf[rows, :]
            s = lax.dot_general(qi, kj, NT_DIMS, preferred_element_type=F32)
            if two:
                q2i = q2buf[rows, :]
                s = s + lax.dot_general(q2i, k2j, NT_DIMS, preferred_element_type=F32)
            dp = lax.dot_general(doi, vj, NT_DIMS, preferred_element_type=F32)
            if softmax:
                s = s * scale
                if has_bias:
                    s = s + kbj
                if diag:
                    s = jnp.where(_visible(mode, B), s, NEG_BIG)
                p = jnp.exp(s - lse_ref[rows, 0:1])
                ds = p * (dp - delta[rows, :])
                if has_bias:
                    dkb_acc[...] += jnp.sum(ds, axis=0, keepdims=True)
                    drow_ref[rows, :] += jnp.broadcast_to(jnp.sum(ds, axis=1, keepdims=True), (B, LANES))
                dsb = (ds * scale).astype(BF16)
            else:
                if diag:
                    dec = jnp.where(_visible(mode, B), _decay(lg, B, 0), 0.0)
                else:
                    dec = _decay(lg, B, i - j)
                p = s * dec
                dsb = (dp * dec).astype(BF16)
            dv_acc[...] += lax.dot_general(p.astype(BF16), doi, TN_DIMS, preferred_element_type=F32)
            dk_acc[...] += lax.dot_general(dsb, qi, TN_DIMS, preferred_element_type=F32)
            dq1_ref[rows, :] += jnp.dot(dsb, kj, preferred_element_type=F32)
            if two:
                dk2_acc[...] += lax.dot_general(dsb, q2i, TN_DIMS, preferred_element_type=F32)
                dq2_ref[rows, :] += jnp.dot(dsb, k2j, preferred_element_type=F32)

        for i in range(nb):
            pl.when(i == j)(functools.partial(step, i, True))
            pl.when(i > j)(functools.partial(step, i, False))
        dk1_ref[...] = dk_acc[...]
        dv_ref[...] = dv_acc[...]
        if two:
            dk2_ref[...] = dk2_acc[...]
        if has_bias:
            dkb_ref[...] = dkb_acc[...]

    full = lambda w, cb: pl.BlockSpec((S, w), lambda h, j: (0, cb + h))
    blk = lambda w, cb: pl.BlockSpec((B, w), lambda h, j: (j, cb + h))
    in_specs = [full(LANES, q1_cb), blk(LANES, k1_cb), blk(dv, v_cb), full(dv, 0)]
    args = [q1, k1, v, do]
    if softmax:
        in_specs += [full(dv, 0), full(LANES, 0)]
        args += [o, lse]
    if two:
        in_specs += [full(LANES, q2_cb), pl.BlockSpec((B, LANES), lambda h, j: (j, 0))]
        args += [q2, k2]
    if has_bias:
        in_specs.append(pl.BlockSpec((None, None, 1, B), lambda h, j: (h, j, 0, 0)))
        args.append(kbias)
    names = ["dq1", "dk1", "dv"]
    out_shape = [jax.ShapeDtypeStruct((S, H * LANES), F32), jax.ShapeDtypeStruct((S, H * LANES), F32),
                 jax.ShapeDtypeStruct((S, H * dv), F32)]
    out_specs = [full(LANES, 0), blk(LANES, 0), blk(dv, 0)]
    if two:
        names += ["dq2", "dk2h"]
        out_shape += [jax.ShapeDtypeStruct((S, H * LANES), F32)] * 2
        out_specs += [full(LANES, 0), blk(LANES, 0)]
    if has_bias:
        names.append("dkb")
        out_shape.append(jax.ShapeDtypeStruct((H, nb, 1, B), F32))
        out_specs.append(pl.BlockSpec((None, None, 1, B), lambda h, j: (h, j, 0, 0)))
        names.append("drow")
        out_shape.append(jax.ShapeDtypeStruct((S, H * LANES), F32))
        out_specs.append(full(LANES, 0))
    scratch = [pltpu.VMEM((S, LANES), BF16), pltpu.VMEM((S, dv), BF16)]
    if two:
        scratch.append(pltpu.VMEM((S, LANES), BF16))
    if softmax:
        scratch.append(pltpu.VMEM((S, 1), F32))
    scratch += [pltpu.VMEM((B, LANES), F32), pltpu.VMEM((B, dv), F32)]
    if two:
        scratch.append(pltpu.VMEM((B, LANES), F32))
    if has_bias:
        scratch.append(pltpu.VMEM((1, B), F32))
    res = pl.pallas_call(body, out_shape=tuple(out_shape), grid=(H, nb), in_specs=in_specs,
                         out_specs=tuple(out_specs), scratch_shapes=scratch, name=name,
                         compiler_params=_cparams("parallel", "arbitrary"))(*args)
    return dict(zip(names, res))


def head_sum(name, x, H, out_dtype):
    S = x.shape[0]
    tr = _pick(S, 512, 8)

    def body(x_ref, o_ref):
        acc = x_ref[:, 0:LANES]
        for h in range(1, H):
            acc = acc + x_ref[:, h * LANES:(h + 1) * LANES]
        o_ref[...] = acc.astype(o_ref.dtype)

    return pl.pallas_call(
        body, out_shape=jax.ShapeDtypeStruct((S, LANES), out_dtype), grid=(S // tr,),
        in_specs=[pl.BlockSpec((tr, H * LANES), lambda i: (i, 0))],
        out_specs=pl.BlockSpec((tr, LANES), lambda i: (i, 0)), name=name,
        compiler_params=_cparams("parallel"))(x)


def _mesh_pos():
    return lax.axis_index("x"), lax.axis_index("y"), lax.axis_index("c")


def _peer(pos, k):
    x, y, c = pos
    px = 1 - x if k & 4 else x
    py = 1 - y if k & 2 else y
    pc = 1 - c if k & 1 else c
    return (px, py, pc), 4 * px + 2 * py + pc


def exchange(name, tensors):
    nt = len(tensors)
    flat_in, counts = [], []
    out_shape = []
    for mode, srcs in tensors:
        counts.append(len(srcs))
        flat_in += list(srcs)
        rc = srcs[0].shape[-2:]
        out_shape.append(jax.ShapeDtypeStruct((len(srcs), N_DEV) + tuple(rc), srcs[0].dtype))
    n_in = len(flat_in)

    def body(*refs):
        ins = refs[:n_in]
        outs = refs[n_in:n_in + nt]
        send_sems, recv_sems, local_sems = refs[n_in + nt:]
        pos = _mesh_pos()
        me = 4 * pos[0] + 2 * pos[1] + pos[2]
        srcs_of, base = [], 0
        for t in range(nt):
            srcs_of.append(ins[base:base + counts[t]])
            base += counts[t]

        def src_view(t, l, slot):
            ref = srcs_of[t][l]
            return ref if tensors[t][0] == "gather" else ref.at[slot]

        def all_layers(t, slot):
            return outs[t].at[pl.ds(0, counts[t]), slot]

        for t in range(nt):
            for l in range(counts[t]):
                pltpu.make_async_copy(src_view(t, l, me), outs[t].at[l, me], local_sems.at[t]).start()
        for t in range(nt):
            for k in range(1, N_DEV):
                peer, pid = _peer(pos, k)
                for l in range(counts[t]):
                    pltpu.make_async_remote_copy(
                        src_ref=src_view(t, l, pid), dst_ref=outs[t].at[l, me],
                        send_sem=send_sems.at[t, k - 1], recv_sem=recv_sems.at[t, k - 1],
                        device_id=peer, device_id_type=pl.DeviceIdType.MESH).start()
        for t in range(nt):
            for k in range(1, N_DEV):
                peer, pid = _peer(pos, k)
                pltpu.make_async_remote_copy(
                    src_ref=all_layers(t, pid), dst_ref=all_layers(t, pid),
                    send_sem=send_sems.at[t, k - 1], recv_sem=recv_sems.at[t, k - 1],
                    device_id=peer, device_id_type=pl.DeviceIdType.MESH).wait()
        for t in range(nt):
            pltpu.make_async_copy(all_layers(t, me), all_layers(t, me), local_sems.at[t]).wait()

    any_spec = pl.BlockSpec(memory_space=pl.ANY)
    return pl.pallas_call(
        body, out_shape=tuple(out_shape), in_specs=[any_spec] * n_in, out_specs=tuple([any_spec] * nt),
        scratch_shapes=[pltpu.SemaphoreType.DMA((nt, N_DEV - 1)), pltpu.SemaphoreType.DMA((nt, N_DEV - 1)),
                        pltpu.SemaphoreType.DMA((nt,))],
        name=name)(*flat_in)


HBM_SPEC = pl.BlockSpec(memory_space=pltpu.HBM)
SEM_SPEC = pl.BlockSpec(memory_space=pltpu.SEMAPHORE)
DATAFLOW = pltpu.SideEffectType.DATAFLOW_SIDE_EFFECTING


def _hbm(a):
    return pltpu.with_memory_space_constraint(a, pltpu.HBM)


def place_own(name, mode, srcs, after=None):
    n = len(srcs)
    extra = [] if after is None else [after]

    def body(*refs):
        src_refs, land_refs, sems = refs[:n], refs[n + len(extra):2 * n + len(extra)], refs[-1]
        pos = _mesh_pos()
        me = 4 * pos[0] + 2 * pos[1] + pos[2]
        cps = []
        for t in range(n):
            src = src_refs[t] if mode == "gather" else src_refs[t].at[me]
            cps.append(pltpu.make_async_copy(src, land_refs[t].at[me], sems.at[t]))
            cps[-1].start()
        for cp in cps:
            cp.wait()

    out_shape = tuple(pltpu.HBM((N_DEV,) + tuple(s.shape[-2:]), s.dtype) for s in srcs)
    return pl.pallas_call(
        body, out_shape=out_shape, in_specs=[HBM_SPEC] * n + [pl.BlockSpec(memory_space=pl.ANY)] * len(extra),
        out_specs=tuple([HBM_SPEC] * n), scratch_shapes=[pltpu.SemaphoreType.DMA((n,))],
        name=name)(*[_hbm(s) for s in srcs], *extra)


def exchange_start(name, mode, srcs, lands):
    n = len(srcs)

    def body(*refs):
        src_refs, land_refs = refs[:n], refs[n:2 * n]
        send_sems, recv_sems = refs[2 * n], refs[2 * n + 1]
        token = refs[-1]
        pos = _mesh_pos()
        me = 4 * pos[0] + 2 * pos[1] + pos[2]
        for t in range(n):
            for k in range(1, N_DEV):
                peer, pid = _peer(pos, k)
                src = src_refs[t] if mode == "gather" else src_refs[t].at[pid]
                pltpu.make_async_remote_copy(
                    src_ref=src, dst_ref=land_refs[t].at[me], send_sem=send_sems.at[t], recv_sem=recv_sems.at[t],
                    device_id=peer, device_id_type=pl.DeviceIdType.MESH).start()
        token[...] = jnp.zeros_like(token)

    thru = [pltpu.HBM(a.shape, a.dtype) for a in list(srcs) + list(lands)]
    out_shape = (pltpu.SemaphoreType.DMA((n,)), pltpu.SemaphoreType.DMA((n,)), *thru,
                 jax.ShapeDtypeStruct((8, LANES), F32))
    res = pl.pallas_call(
        body, out_shape=out_shape, in_specs=[HBM_SPEC] * (2 * n),
        out_specs=(SEM_SPEC, SEM_SPEC, *([HBM_SPEC] * (2 * n)), pl.BlockSpec(memory_space=pltpu.VMEM)),
        input_output_aliases={i: 2 + i for i in range(2 * n)}, name=name,
        compiler_params=pltpu.CompilerParams(has_side_effects=DATAFLOW))(*[_hbm(a) for a in list(srcs) + list(lands)])
    return res[0], res[1], list(res[2:2 + n]), list(res[2 + n:2 + 2 * n]), res[-1]


def exchange_wait(name, send_sems, recv_sems, srcs, lands, after):
    n = len(srcs)

    def body(*refs):
        land_refs = refs[n:2 * n]
        s_sems, r_sems = refs[2 * n], refs[2 * n + 1]
        pos = _mesh_pos()
        for t in range(n):
            seven = land_refs[t].at[pl.ds(0, N_DEV - 1)]
            cp = pltpu.make_async_remote_copy(
                src_ref=seven, dst_ref=seven, send_sem=s_sems.at[t], recv_sem=r_sems.at[t],
                device_id=pos, device_id_type=pl.DeviceIdType.MESH)
            cp.wait_send()
            cp.wait_recv()

    arrs = list(srcs) + list(lands)
    res = pl.pallas_call(
        body, out_shape=tuple(pltpu.HBM(a.shape, a.dtype) for a in arrs),
        in_specs=[HBM_SPEC] * (2 * n) + [SEM_SPEC, SEM_SPEC, pl.BlockSpec(memory_space=pl.ANY)],
        out_specs=tuple([HBM_SPEC] * (2 * n)), input_output_aliases={i: i for i in range(2 * n)}, name=name,
        compiler_params=pltpu.CompilerParams(has_side_effects=DATAFLOW))(*arrs, send_sems, recv_sems, after)
    return list(res[n:])


def reduce_parts(name, parts):
    n, R, C = parts.shape
    tr = _pick(R, max(8, (1 << 20) // (C * 4) // 8 * 8), 8)

    def body(p_ref, o_ref):
        acc = p_ref[0].astype(F32)
        for s in range(1, n):
            acc = acc + p_ref[s].astype(F32)
        o_ref[...] = acc

    return pl.pallas_call(
        body, out_shape=jax.ShapeDtypeStruct((R, C), F32), grid=(R // tr,),
        in_specs=[pl.BlockSpec((n, tr, C), lambda i: (0, i, 0))],
        out_specs=pl.BlockSpec((tr, C), lambda i: (i, 0)), name=name,
        compiler_params=_cparams("parallel"))(parts)


def adamw(name, w, m, v, parts):
    L, R, C = w.shape
    n = parts[0].shape[0]
    tr = _pick(R, max(8, (1 << 19) // (C * 4) // 8 * 8), 8)

    def body(*refs):
        w_ref, m_ref, v_ref = refs[:3]
        p_refs = refs[3:3 + L]
        g_ref, d_ref, nm_ref, nv_ref = refs[3 + L:]

        def update(p_ref):
            g = p_ref[0].astype(F32)
            for s in range(1, n):
                g = g + p_ref[s].astype(F32)
            wv = w_ref[...]
            mn = ADAM_B1 * m_ref[...] + (1.0 - ADAM_B1) * g
            vn = ADAM_B2 * v_ref[...] + (1.0 - ADAM_B2) * jnp.square(g)
            m_hat = mn / (1.0 - ADAM_B1 ** ADAM_STEP)
            v_hat = vn / (1.0 - ADAM_B2 ** ADAM_STEP)
            g_ref[...] = g
            d_ref[...] = -ADAM_LR * (m_hat / (jnp.sqrt(v_hat) + ADAM_EPS) + ADAM_WD * wv)
            nm_ref[...] = mn
            nv_ref[...] = vn

        for k in range(L):
            pl.when(pl.program_id(0) == k)(functools.partial(update, p_refs[k]))

    blk = pl.BlockSpec((None, tr, C), lambda l, i: (l, i, 0))
    pspecs = [pl.BlockSpec((n, tr, C), lambda l, i, k=k: (0, jnp.where(l == k, i, 0), 0)) for k in range(L)]
    sh = jax.ShapeDtypeStruct((L, R, C), F32)
    return pl.pallas_call(
        body, out_shape=(sh, sh, sh, sh), grid=(L, R // tr), in_specs=[blk, blk, blk] + pspecs,
        out_specs=(blk, blk, blk, blk), name=name,
        compiler_params=_cparams("arbitrary", "arbitrary"))(w, m, v, *parts)


def _cols_from_blocks(g):
    n, R, c = g.shape
    return g.transpose(1, 0, 2).reshape(R, n * c)


def _cols_to_blocks(w):
    R, C = w.shape
    return w.reshape(R, N_DEV, C // N_DEV).transpose(1, 0, 2)


def _uq_permute(w):
    lead = w.shape[:-1]
    w4 = w.reshape(lead + (MLA_HEADS, MLA_NOPE + MLA_ROPE))
    nope = w4[..., :MLA_NOPE].reshape(lead + (MLA_HEADS * MLA_NOPE,))
    rope = jnp.pad(w4[..., MLA_NOPE:], [(0, 0)] * (w4.ndim - 1) + [(0, LANES - MLA_ROPE)])
    return jnp.concatenate([nope, rope.reshape(lead + (MLA_HEADS * LANES,))], axis=-1)


def _uq_unpermute(w):
    lead = w.shape[:-1]
    n = MLA_HEADS * MLA_NOPE
    nope = w[..., :n].reshape(lead + (MLA_HEADS, MLA_NOPE))
    rope = w[..., n:].reshape(lead + (MLA_HEADS, LANES))[..., :MLA_ROPE]
    return jnp.concatenate([nope, rope], axis=-1).reshape(lead + (MLA_HEADS * (MLA_NOPE + MLA_ROPE),))


def _ukv_permute(w):
    lead = w.shape[:-1]
    w4 = w.reshape(lead + (MLA_HEADS, 2, MLA_NOPE))
    return jnp.swapaxes(w4, -3, -2).reshape(lead + (2 * MLA_HEADS * MLA_NOPE,))


def _ukv_unpermute(w):
    lead = w.shape[:-1]
    w4 = w.reshape(lead + (2, MLA_HEADS, MLA_NOPE))
    return jnp.swapaxes(w4, -3, -2).reshape(lead + (2 * MLA_HEADS * MLA_NOPE,))


SMALL = ["norm1_g", "mla_q_norm_g", "mla_kv_norm_g", "fox_b_f", "norm2_g", "ffn_conv_b", "final_norm_g"]
SMALL_TILE = 8 * LANES


def _pack_small(d):
    flat = jnp.concatenate([d[n].reshape(-1).astype(F32) for n in SMALL])
    pad = -flat.shape[0] % SMALL_TILE
    return jnp.pad(flat, (0, pad)).reshape(-1, LANES)


def _unpack_small(packed, like):
    flat = packed.reshape(-1)
    out, o = {}, 0
    for n in SMALL:
        sz = int(np.prod(like[n].shape))
        out[n] = flat[o:o + sz].reshape(like[n].shape)
        o += sz
    return out


WEIGHTS = ["norm1_g", "w_in", "mla_q_norm_g", "mla_kv_norm_g", "mla_w_uq", "mla_w_ukv", "fox_b_f", "w_br_fox",
           "w_br_mla", "w_br_ret", "w_out", "norm2_g", "ffn_w_up", "ffn_w_gate", "ffn_conv_w", "ffn_conv_b",
           "ffn_w_down", "final_norm_g"]
BIG = ["w_in", "mla_w_uq", "mla_w_ukv", "w_br_fox", "w_br_mla", "w_br_ret", "w_out", "ffn_w_up", "ffn_w_gate",
       "ffn_conv_w", "ffn_w_down"]


def kernel(x, norm1_g, w_in, mla_q_norm_g, mla_kv_norm_g, mla_w_uq, mla_w_ukv, fox_b_f, w_br_fox, w_br_mla, w_br_ret, w_out, norm2_g, ffn_w_up, ffn_w_gate, ffn_conv_w, ffn_conv_b, ffn_w_down, final_norm_g, loss_target, m_norm1_g, m_w_in, m_mla_q_norm_g, m_mla_kv_norm_g, m_mla_w_uq, m_mla_w_ukv, m_fox_b_f, m_w_br_fox, m_w_br_mla, m_w_br_ret, m_w_out, m_norm2_g, m_ffn_w_up, m_ffn_w_gate, m_ffn_conv_w, m_ffn_conv_b, m_ffn_w_down, m_final_norm_g, v_norm1_g, v_w_in, v_mla_q_norm_g, v_mla_kv_norm_g, v_mla_w_uq, v_mla_w_ukv, v_fox_b_f, v_w_br_fox, v_w_br_mla, v_w_br_ret, v_w_out, v_norm2_g, v_ffn_w_up, v_ffn_w_gate, v_ffn_conv_w, v_ffn_conv_b, v_ffn_w_down, v_final_norm_g):
    env = dict(locals())
    W = {n: env[n] for n in WEIGHTS}
    Mo = {n: env["m_" + n] for n in WEIGHTS}
    Vo = {n: env["v_" + n] for n in WEIGHTS}
    S, D = x.shape[1], x.shape[2]
    L = w_in.shape[0]
    lay = InLayout(D)
    NP = lay.total
    f = ffn_w_up.shape[-1]
    xs = x.reshape(S, D)
    tgt = loss_target.reshape(S, D)

    local = {n: W[n].astype(BF16) for n in BIG}
    local["w_in"] = lay.permute(W["w_in"]).astype(BF16)
    pending = []
    token = None
    for l in range(L):
        srcs = [local[n][l] for n in BIG]
        lands = place_own(f"place_weights_{l}", "gather", srcs, token)
        *flight, token = exchange_start(f"gather_start_{l}", "gather", srcs, lands)
        pending.append(flight)
    gather_token = token
    cbias_all = ffn_conv_b.reshape(L, 1, N_DEV, 1, f)

    def layer_weights(l, after):
        lands = exchange_wait(f"gather_wait_{l}", *pending[l], after)
        g = dict(zip(BIG, lands))
        return dict(
            Win=g["w_in"].reshape(1, D, NP), Wout=g["w_out"].reshape(1, D, D),
            Wuq=_uq_permute(_cols_from_blocks(g["mla_w_uq"])), Wukv=_ukv_permute(_cols_from_blocks(g["mla_w_ukv"])),
            Wbf=_cols_from_blocks(g["w_br_fox"]), Wbm=_cols_from_blocks(g["w_br_mla"]),
            Wbr=_cols_from_blocks(g["w_br_ret"]), Wup=g["ffn_w_up"][None], Wgate=g["ffn_w_gate"][None],
            Wdown=g["ffn_w_down"][None], Wconv=g["ffn_conv_w"].astype(F32)[None], cbias=cbias_all[l])

    tab64 = rope_tables(S, MLA_ROPE)
    tab128 = rope_tables(S, RET_DK)
    fox_scale = FOX_DH ** -0.5
    mla_scale = (MLA_NOPE + MLA_ROPE) ** -0.5
    ret_kscale = RET_DK ** -0.5
    R = S // LANES
    AB = _attn_block(S)
    NOPE_W = MLA_HEADS * MLA_NOPE

    def vec(a):
        return a.reshape(1, -1)

    saved = []
    xc = xs
    for l in range(L):
        Wl = layer_weights(l, gather_token if l == 0 else xc)
        Win, Wout, Wuq, Wukv, Wbf, Wbm, Wbr = (Wl[k] for k in ("Win", "Wout", "Wuq", "Wukv", "Wbf", "Wbm", "Wbr"))
        Wup, Wgate, Wdown, Wconv, cbias = (Wl[k] for k in ("Wup", "Wgate", "Wdown", "Wconv", "cbias"))
        s = {"x": xc, "W": Wl}
        h1 = rms_fwd("norm1", xc, 0, D, vec(norm1_g[l]), BF16)
        P = mm_nn("in_proj", h1, Win, F32, b_lead=0)
        s.update(h1=h1, P=P)
        ff_off = lay.off["ff"]
        ft = P[:, ff_off:ff_off + FOX_HEADS].T.reshape(FOX_HEADS, R, LANES)
        bfl = jnp.broadcast_to(fox_b_f[l].reshape(FOX_HEADS, 1, 1), (FOX_HEADS, 1, LANES))
        kbias = fox_gate_fwd("fox_gate", ft, bfl).reshape(FOX_HEADS, S // AB, 1, AB)
        o_fox, lse_fox = attn_fwd("fox_attn", "fox", P, lay.cb("fq", LANES), P, lay.cb("fk", LANES),
                                  P, lay.cb("fv", LANES), FOX_HEADS, FOX_DH, fox_scale, kbias=kbias)
        s.update(ft=ft, bfl=bfl, kbias=kbias, o_fox=o_fox, lse_fox=lse_fox)
        cqn = rms_fwd("mla_q_norm", P, lay.cb("mq", MLA_Q_LORA), MLA_Q_LORA, vec(mla_q_norm_g[l]), BF16)
        qall = mm_nn("mla_uq", cqn, Wuq, F32)
        ckvn = rms_fwd("mla_kv_norm", P, lay.cb("mkv", MLA_KV_LORA), MLA_KV_LORA, vec(mla_kv_norm_g[l]), BF16)
        kvall = mm_nn("mla_ukv", ckvn, Wukv, F32)
        qrope = rope_apply("mla_q_rope", qall, NOPE_W // LANES, MLA_HEADS, tab64, 1.0, F32)
        krope = rope_apply("mla_k_rope", P, lay.cb("mkr", LANES), 1, tab64, 1.0, F32)
        o_mla, lse_mla = attn_fwd("mla_attn", "mla", qall, 0, kvall, 0, kvall, NOPE_W // MLA_V, MLA_HEADS, MLA_V,
                                  mla_scale, q2=qrope, q2_cb=0, k2=krope)
        s.update(cqn=cqn, qall=qall, ckvn=ckvn, kvall=kvall, qrope=qrope, krope=krope, o_mla=o_mla,
                 lse_mla=lse_mla)
        rq = rope_apply("ret_q_rope", P, lay.cb("rq", LANES), RET_HEADS, tab128, 1.0, F32)
        rk = rope_apply("ret_k_rope", P, lay.cb("rk", LANES), RET_HEADS, tab128, ret_kscale, F32)
        o_ret, _ = attn_fwd("ret_attn", "ret", rq, 0, rk, 0, P, lay.cb("rv", RET_DV), RET_HEADS, RET_DV, 1.0)
        c_ret = ret_out_fwd("ret_out", o_ret, P, lay.cb("rg", RET_DV), BF16)
        s.update(rq=rq, rk=rk, o_ret=o_ret, c_ret=c_ret)
        A = mm_nn("br_fox", o_fox, Wbf, F32)
        Bm = mm_nn("br_mla", o_mla, Wbm, F32)
        C = mm_nn("br_ret", c_ret, Wbr, F32)
        merged = merge_fwd("merge", P, lay.cb("gates", D), A, Bm, C, BF16)
        x2 = mm_nn("out_proj", merged, Wout, F32, b_lead=0, res=xc)
        s.update(A=A, Bm=Bm, C=C, merged=merged, x2=x2)
        h2 = rms_fwd("norm2", x2, 0, D, vec(norm2_g[l]), BF16)
        u = ffn_up("ffn_up", h2, Wup, 0, F32)
        gt = ffn_up("ffn_gate", h2, Wgate, 0, F32)
        act = ffn_act_fwd("ffn_act", u, gt, Wconv, cbias, 0, BF16)
        xc = ffn_down("ffn_down", act, Wdown, 0, x2, F32)
        s.update(h2=h2, u=u, gt=gt, act=act)
        saved.append(s)

    loss_tile, dx, dgf = loss_head("loss_head", xc, vec(final_norm_g), tgt)
    loss = lax.psum(loss_tile[0, 0], ("x", "y", "c"))

    gbig = {n: [None] * L for n in BIG}
    gsmall = {n: [None] * L for n in SMALL if n != "final_norm_g"}
    scattering = [None] * L
    scatter_token = None
    for l in reversed(range(L)):
        s = saved[l]
        P = s["P"]
        Wl = s["W"]
        Win, Wout, Wuq, Wukv, Wbf, Wbm, Wbr = (Wl[k] for k in ("Win", "Wout", "Wuq", "Wukv", "Wbf", "Wbm", "Wbr"))
        Wup, Wgate, Wdown, Wconv, cbias = (Wl[k] for k in ("Wup", "Wgate", "Wdown", "Wconv", "cbias"))
        dxb = (dx if scatter_token is None else dx + scatter_token[0, 0]).astype(BF16)
        dact = ffn_down_bwd_act("ffn_down_da", dxb, Wdown, 0, BF16)
        gbig["ffn_w_down"][l] = ffn_down_bwd_w("ffn_down_dw", s["act"], dxb, BF16)
        g, dgt = ffn_act_bwd_point("ffn_act_bwd", s["u"], s["gt"], Wconv, cbias, 0, dact)
        du, dcw, dcb = ffn_act_bwd_conv("ffn_conv_bwd", s["u"], g, Wconv, 0)
        gbig["ffn_conv_w"][l] = dcw.astype(BF16)
        gsmall["ffn_conv_b"][l] = dcb.reshape(-1)
        gbig["ffn_w_up"][l] = ffn_up_bwd_w("ffn_up_dw", s["h2"], du, BF16)
        gbig["ffn_w_gate"][l] = ffn_up_bwd_w("ffn_gate_dw", s["h2"], dgt, BF16)
        dh2 = ffn_up_bwd_h("ffn_up_dh", du, Wup, 0, None, F32)
        dh2 = ffn_up_bwd_h("ffn_gate_dh", dgt, Wgate, 0, dh2, BF16)
        dx2, dg2 = rms_bwd("norm2_bwd", s["x2"], 0, D, vec(norm2_g[l]), dh2, F32, res=dx)
        gsmall["norm2_g"][l] = dg2.reshape(-1)
        dx2b = dx2.astype(BF16)
        dmerged = mm_nt("out_proj_dm", dx2b, Wout, BF16, b_lead=0)
        gbig["w_out"][l] = mm_tn("out_proj_dw", s["merged"], dx2b, BF16).reshape(N_DEV, D // N_DEV, D)
        dgates, dA, dB, dC = merge_bwd("merge_bwd", P, lay.cb("gates", D), s["A"], s["Bm"], s["C"], dmerged)
        do_fox = mm_nt("br_fox_do", dA, Wbf, F32)
        do_mla = mm_nt("br_mla_do", dB, Wbm, F32)
        dc_ret = mm_nt("br_ret_do", dC, Wbr, BF16)
        gbig["w_br_fox"][l] = _cols_to_blocks(mm_tn("br_fox_dw", s["o_fox"], dA, BF16))
        gbig["w_br_mla"][l] = _cols_to_blocks(mm_tn("br_mla_dw", s["o_mla"], dB, BF16))
        gbig["w_br_ret"][l] = _cols_to_blocks(mm_tn("br_ret_dw", s["c_ret"], dC, BF16))
        do_ret, drg = ret_out_bwd("ret_out_bwd", s["o_ret"], P, lay.cb("rg", RET_DV), dc_ret)
        rb = attn_bwd("ret_attn_bwd", "ret", s["rq"], 0, s["rk"], 0, P, lay.cb("rv", RET_DV), RET_HEADS, RET_DV,
                      1.0, do_ret)
        drq = rope_apply("ret_q_rope_bwd", rb["dq1"], 0, RET_HEADS, tab128, 1.0, BF16, transpose=True)
        drk = rope_apply("ret_k_rope_bwd", rb["dk1"], 0, RET_HEADS, tab128, ret_kscale, BF16, transpose=True)
        drv = rb["dv"].astype(BF16)
        mb = attn_bwd("mla_attn_bwd", "mla", s["qall"], 0, s["kvall"], 0, s["kvall"], NOPE_W // MLA_V, MLA_HEADS,
                      MLA_V, mla_scale, do_mla, o=s["o_mla"], lse=s["lse_mla"], q2=s["qrope"], q2_cb=0,
                      k2=s["krope"])
        dqrope = rope_apply("mla_q_rope_bwd", mb["dq2"], 0, MLA_HEADS, tab64, 1.0, BF16, transpose=True)
        dkr_sum = head_sum("mla_k_rope_sum", mb["dk2h"], MLA_HEADS, F32)
        dmkr = rope_apply("mla_k_rope_bwd", dkr_sum, 0, 1, tab64, 1.0, BF16, transpose=True)
        dqall = jnp.concatenate([mb["dq1"].astype(BF16), dqrope], axis=1)
        dkvall = jnp.concatenate([mb["dk1"].astype(BF16), mb["dv"].astype(BF16)], axis=1)
        dcqn = mm_nt("mla_uq_dx", dqall, Wuq, F32)
        dckvn = mm_nt("mla_ukv_dx", dkvall, Wukv, F32)
        guq = _uq_unpermute(mm_tn("mla_uq_dw", s["cqn"], dqall, BF16))
        gukv = _ukv_unpermute(mm_tn("mla_ukv_dw", s["ckvn"], dkvall, BF16))
        gbig["mla_w_uq"][l] = _cols_to_blocks(guq)
        gbig["mla_w_ukv"][l] = _cols_to_blocks(gukv)
        dmq, dgq = rms_bwd("mla_q_norm_bwd", P, lay.cb("mq", MLA_Q_LORA), MLA_Q_LORA, vec(mla_q_norm_g[l]),
                           dcqn, BF16)
        dmkv, dgkv = rms_bwd("mla_kv_norm_bwd", P, lay.cb("mkv", MLA_KV_LORA), MLA_KV_LORA,
                             vec(mla_kv_norm_g[l]), dckvn, BF16)
        gsmall["mla_q_norm_g"][l] = dgq.reshape(-1)
        gsmall["mla_kv_norm_g"][l] = dgkv.reshape(-1)
        fb = attn_bwd("fox_attn_bwd", "fox", P, lay.cb("fq", LANES), P, lay.cb("fk", LANES), P,
                      lay.cb("fv", LANES), FOX_HEADS, FOX_DH, fox_scale, do_fox, o=s["o_fox"], lse=s["lse_fox"],
                      kbias=s["kbias"])
        drow = fb["drow"].reshape(S, FOX_HEADS, LANES)[:, :, 0].T.reshape(FOX_HEADS, R, LANES)
        dft, dbf = fox_gate_bwd("fox_gate_bwd", s["ft"], s["bfl"], fb["dkb"].reshape(FOX_HEADS, R, LANES) - drow)
        gsmall["fox_b_f"][l] = dbf[:, 0, 0]
        dff = jnp.pad(dft.reshape(FOX_HEADS, S).T, ((0, 0), (0, LANES - FOX_HEADS))).astype(BF16)
        segs = dict(gates=dgates, rv=drv, rg=drg, mq=dmq, rq=drq, rk=drk, mkv=dmkv, fq=fb["dq1"].astype(BF16),
                    fk=fb["dk1"].astype(BF16), fv=fb["dv"].astype(BF16), mkr=dmkr, ff=dff)
        dP = jnp.concatenate([segs[n] for n in lay.order], axis=1)
        dh1 = mm_nt("in_proj_dh", dP, Win, BF16, b_lead=0)
        gbig["w_in"][l] = mm_tn("in_proj_dw", s["h1"], dP, BF16).reshape(N_DEV, D // N_DEV, NP)
        dx, dg1 = rms_bwd("norm1_bwd", s["x"], 0, D, vec(norm1_g[l]), dh1, F32, res=dx2)
        gsmall["norm1_g"][l] = dg1.reshape(-1)
        srcs = [gbig[n][l] for n in BIG]
        lands = place_own(f"place_grads_{l}", "scatter", srcs)
        *scattering[l], scatter_token = exchange_start(f"scatter_start_{l}", "scatter", srcs, lands)

    recv = {n: [None] * L for n in BIG}
    for l in range(L):
        for n, land in zip(BIG, exchange_wait(f"scatter_wait_{l}", *scattering[l], dx)):
            recv[n][l] = land
    small_like = {n: W[n] for n in SMALL}
    small_part = {n: jnp.stack(gsmall[n]) for n in gsmall}
    small_part["final_norm_g"] = dgf.reshape(-1)
    small_recv = exchange("gather_small_grads", [("gather", [_pack_small(small_part)])])[0]

    out = {}
    for n in BIG:
        parts = recv[n]
        if n == "w_in":
            parts = [lay.unpermute(reduce_parts("w_in_grad_sum", p))[None] for p in parts]
        out[n] = adamw("adamw_" + n, W[n], Mo[n], Vo[n], parts)
    ps = adamw("adamw_small", _pack_small(small_like)[None], _pack_small({n: Mo[n] for n in SMALL})[None],
               _pack_small({n: Vo[n] for n in SMALL})[None], [small_recv[0]])
    small_out = [_unpack_small(a[0], small_like) for a in ps]
    for n in SMALL:
        out[n] = tuple(so[n] for so in small_out)

    grads = [out[n][0] for n in WEIGHTS]
    deltas = [out[n][1] for n in WEIGHTS]
    new_m = [out[n][2] for n in WEIGHTS]
    new_v = [out[n][3] for n in WEIGHTS]
    return (loss, dx.reshape(1, S, D), *grads, *deltas, *new_m, *new_v)
```

```python
import functools
import math

import numpy as np
import jax
import jax.numpy as jnp
from jax import lax
from jax.experimental import pallas as pl
from jax.experimental.pallas import tpu as pltpu

F32 = jnp.float32
BF16 = jnp.bfloat16

CHUNK = 64
NORM_EPS = 1e-6
ROPE_THETA = 10000.0
FOX_HEADS, FOX_DH = 6, 128
FOX_W = FOX_HEADS * FOX_DH
MLA_HEADS, MLA_NOPE, MLA_ROPE, MLA_V = 6, 128, 64, 128
MLA_Q_LORA, MLA_KV_LORA = 512, 256
MLA_W = MLA_HEADS * MLA_V
RET_HEADS, RET_DK, RET_DV = 4, 128, 256
RET_QK_W, RET_V_W = RET_HEADS * RET_DK, RET_HEADS * RET_DV
ADAM_LR, ADAM_B1, ADAM_B2, ADAM_EPS, ADAM_WD, ADAM_STEP = 0.001, 0.9, 0.999, 1e-08, 0.01, 10

N_DEV = 8
LANES = 128
V7X_VMEM_LIMIT_BYTES = 52 * 1024 * 1024
NEG_BIG = -1e30
HIGHEST = lax.Precision.HIGHEST

NT_DIMS = (((1,), (1,)), ((), ()))
TN_DIMS = (((0,), (0,)), ((), ()))
NN_DIMS = (((1,), (0,)), ((), ()))


def _pick(n, cap, mult=LANES):
    best = None
    for t in range(mult, min(n, cap) + 1, mult):
        if n % t == 0:
            best = t
    return n if best is None else best


def _cparams(*sem):
    return pltpu.CompilerParams(dimension_semantics=sem, vmem_limit_bytes=V7X_VMEM_LIMIT_BYTES)


class InLayout:
    def __init__(self, d_model):
        d = d_model
        self.d = d
        orig = dict(fq=(0, FOX_W), fk=(FOX_W, FOX_W), fv=(2 * FOX_W, FOX_W), ff=(3 * FOX_W, FOX_HEADS))
        o = 3 * FOX_W + FOX_HEADS
        for name, w in (("mq", MLA_Q_LORA), ("mkv", MLA_KV_LORA), ("mkr", MLA_ROPE), ("rq", RET_QK_W),
                        ("rk", RET_QK_W), ("rv", RET_V_W), ("rg", RET_V_W), ("gates", 3 * d)):
            orig[name] = (o, w)
            o += w
        self.orig = orig
        self.orig_width = o
        order = ["gates", "rv", "rg", "mq", "rq", "rk", "mkv", "fq", "fk", "fv", "mkr", "ff"]
        self.order = order
        self.off, self.width = {}, {}
        p = 0
        for name in order:
            w = orig[name][1]
            wp = -(-w // LANES) * LANES
            self.off[name], self.width[name] = p, wp
            p += wp
        self.total = p

    def cb(self, name, block):
        assert self.off[name] % block == 0, (name, block)
        return self.off[name] // block

    def permute(self, w):
        parts = []
        for name in self.order:
            o, n = self.orig[name]
            seg = w[..., o:o + n]
            pad = self.width[name] - n
            if pad:
                seg = jnp.pad(seg, [(0, 0)] * (w.ndim - 1) + [(0, pad)])
            parts.append(seg)
        return jnp.concatenate(parts, axis=-1)

    def unpermute(self, w):
        names = sorted(self.orig, key=lambda n: self.orig[n][0])
        return jnp.concatenate([w[..., self.off[n]:self.off[n] + self.orig[n][1]] for n in names], axis=-1)


def _mm(name, a, b, out_shape, grid, a_spec, b_spec, o_spec, dims, acc_shape, res=None):
    nk = grid[-1]
    has_res = res is not None

    def body(*refs):
        if has_res:
            a_ref, b_ref, r_ref, o_ref = refs[:4]
        else:
            a_ref, b_ref, o_ref = refs[:3]
            r_ref = None
        prod = lax.dot_general(a_ref[...].astype(BF16), b_ref[...].astype(BF16), dims,
                               preferred_element_type=F32)
        if nk == 1:
            if has_res:
                prod = prod + r_ref[...].astype(F32)
            o_ref[...] = prod.astype(o_ref.dtype)
        else:
            acc_ref = refs[-1]
            k = pl.program_id(len(grid) - 1)

            @pl.when(k == 0)
            def _():
                acc_ref[...] = prod

            @pl.when(k > 0)
            def _():
                acc_ref[...] += prod

            @pl.when(k == nk - 1)
            def _():
                r = acc_ref[...]
                if has_res:
                    r = r + r_ref[...].astype(F32)
                o_ref[...] = r.astype(o_ref.dtype)

    in_specs = [a_spec, b_spec] + ([o_spec] if has_res else [])
    args = (a, b) + ((res,) if has_res else ())
    scratch = [pltpu.VMEM(acc_shape, F32)] if nk > 1 else []
    sem = ("parallel",) * (len(grid) - 1) + ("arbitrary",)
    return pl.pallas_call(body, out_shape=out_shape, grid=grid, in_specs=in_specs, out_specs=o_spec,
                          scratch_shapes=scratch, name=name, compiler_params=_cparams(*sem))(*args)


def mm_nn(name, a, b, out_dtype, b_lead=None, res=None):
    M, K = a.shape
    N = b.shape[-1]
    tm, tn, tk = _pick(M, 1024, 8), _pick(N, 1024), _pick(K, 2048)
    grid = (M // tm, N // tn, K // tk)
    a_spec = pl.BlockSpec((tm, tk), lambda i, j, k: (i, k))
    if b_lead is None:
        b_spec = pl.BlockSpec((tk, tn), lambda i, j, k: (k, j))
    else:
        b_spec = pl.BlockSpec((None, tk, tn), lambda i, j, k: (b_lead, k, j))
    o_spec = pl.BlockSpec((tm, tn), lambda i, j, k: (i, j))
    return _mm(name, a, b, jax.ShapeDtypeStruct((M, N), out_dtype), grid, a_spec, b_spec, o_spec,
               NN_DIMS, (tm, tn), res)


def mm_nt(name, a, b, out_dtype, b_lead=None, res=None):
    M, N = a.shape
    K = b.shape[-2]
    tm, tko, tk = _pick(M, 1024, 8), _pick(K, 1024), _pick(N, 2048)
    grid = (M // tm, K // tko, N // tk)
    a_spec = pl.BlockSpec((tm, tk), lambda i, j, k: (i, k))
    if b_lead is None:
        b_spec = pl.BlockSpec((tko, tk), lambda i, j, k: (j, k))
    else:
        b_spec = pl.BlockSpec((None, tko, tk), lambda i, j, k: (b_lead, j, k))
    o_spec = pl.BlockSpec((tm, tko), lambda i, j, k: (i, j))
    return _mm(name, a, b, jax.ShapeDtypeStruct((M, K), out_dtype), grid, a_spec, b_spec, o_spec,
               NT_DIMS, (tm, tko), res)


def mm_tn(name, a, b, out_dtype):
    M, K = a.shape
    N = b.shape[-1]
    cap = 4096 if (a.dtype == BF16 and b.dtype == BF16) else 2048
    tko, tn, tk = _pick(K, 1024), _pick(N, 1024), _pick(M, cap, 8)
    grid = (K // tko, N // tn, M // tk)
    a_spec = pl.BlockSpec((tk, tko), lambda i, j, k: (k, i))
    b_spec = pl.BlockSpec((tk, tn), lambda i, j, k: (k, j))
    o_spec = pl.BlockSpec((tko, tn), lambda i, j, k: (i, j))
    return _mm(name, a, b, jax.ShapeDtypeStruct((K, N), out_dtype), grid, a_spec, b_spec, o_spec,
               TN_DIMS, (tko, tn))


def ffn_up(name, h, w, l, out_dtype):
    M, D = h.shape
    f = w.shape[-1]
    tm = _pick(M, 1024, 8)
    grid = (M // tm, N_DEV, 1)
    return _mm(name, h, w, jax.ShapeDtypeStruct((N_DEV, M, f), out_dtype), grid,
               pl.BlockSpec((tm, D), lambda i, j, k: (i, 0)),
               pl.BlockSpec((None, None, D, f), lambda i, j, k: (l, j, 0, 0)),
               pl.BlockSpec((None, tm, f), lambda i, j, k: (j, i, 0)), NN_DIMS, (tm, f))


def ffn_down(name, act, w, l, res, out_dtype):
    _, M, f = act.shape
    D = w.shape[-1]
    tm, tn = _pick(M, 1024, 8), _pick(D, 1024)
    grid = (M // tm, D // tn, N_DEV)
    return _mm(name, act, w, jax.ShapeDtypeStruct((M, D), out_dtype), grid,
               pl.BlockSpec((None, tm, f), lambda i, j, k: (k, i, 0)),
               pl.BlockSpec((None, None, f, tn), lambda i, j, k: (l, k, 0, j)),
               pl.BlockSpec((tm, tn), lambda i, j, k: (i, j)), NN_DIMS, (tm, tn), res)


def ffn_down_bwd_act(name, dy, w, l, out_dtype):
    M, D = dy.shape
    f = w.shape[-2]
    tm = _pick(M, 1024, 8)
    grid = (M // tm, N_DEV, 1)
    return _mm(name, dy, w, jax.ShapeDtypeStruct((N_DEV, M, f), out_dtype), grid,
               pl.BlockSpec((tm, D), lambda i, j, k: (i, 0)),
               pl.BlockSpec((None, None, f, D), lambda i, j, k: (l, j, 0, 0)),
               pl.BlockSpec((None, tm, f), lambda i, j, k: (j, i, 0)), NT_DIMS, (tm, f))


def ffn_down_bwd_w(name, act, dy, out_dtype):
    _, M, f = act.shape
    D = dy.shape[-1]
    tn, tk = _pick(D, 1024), _pick(M, 4096 if dy.dtype == BF16 else 2048, 8)
    grid = (N_DEV, D // tn, M // tk)
    return _mm(name, act, dy, jax.ShapeDtypeStruct((N_DEV, f, D), out_dtype), grid,
               pl.BlockSpec((None, tk, f), lambda j, n, k: (j, k, 0)),
               pl.BlockSpec((tk, tn), lambda j, n, k: (k, n)),
               pl.BlockSpec((None, f, tn), lambda j, n, k: (j, 0, n)), TN_DIMS, (f, tn))


def ffn_up_bwd_h(name, du, w, l, res, out_dtype):
    _, M, f = du.shape
    D = w.shape[-2]
    tm, tn = _pick(M, 1024, 8), _pick(D, 1024)
    grid = (M // tm, D // tn, N_DEV)
    return _mm(name, du, w, jax.ShapeDtypeStruct((M, D), out_dtype), grid,
               pl.BlockSpec((None, tm, f), lambda i, j, k: (k, i, 0)),
               pl.BlockSpec((None, None, tn, f), lambda i, j, k: (l, k, j, 0)),
               pl.BlockSpec((tm, tn), lambda i, j, k: (i, j)), NT_DIMS, (tm, tn), res)


def ffn_up_bwd_w(name, h, du, out_dtype):
    M, D = h.shape
    f = du.shape[-1]
    tko, tk = _pick(D, 1024), _pick(M, 4096, 8)
    grid = (N_DEV, D // tko, M // tk)
    return _mm(name, h, du, jax.ShapeDtypeStruct((N_DEV, D, f), out_dtype), grid,
               pl.BlockSpec((tk, tko), lambda j, n, k: (k, n)),
               pl.BlockSpec((None, tk, f), lambda j, n, k: (j, k, 0)),
               pl.BlockSpec((None, tko, f), lambda j, n, k: (j, n, 0)), TN_DIMS, (tko, f))


def _rms(xf, g):
    return xf * lax.rsqrt(jnp.mean(xf * xf, axis=-1, keepdims=True) + NORM_EPS) * g


def rms_fwd(name, x, cb, W, g, out_dtype):
    S = x.shape[0]
    tr = _pick(S, 256, 8)

    def body(x_ref, g_ref, o_ref):
        o_ref[...] = _rms(x_ref[...].astype(F32), g_ref[...]).astype(o_ref.dtype)

    return pl.pallas_call(
        body, out_shape=jax.ShapeDtypeStruct((S, W), out_dtype), grid=(S // tr,),
        in_specs=[pl.BlockSpec((tr, W), lambda i: (i, cb)), pl.BlockSpec((1, W), lambda i: (0, 0))],
        out_specs=pl.BlockSpec((tr, W), lambda i: (i, 0)), name=name, compiler_params=_cparams("parallel"))(x, g)


def rms_bwd(name, x, cb, W, g, dy, out_dtype, res=None):
    S = x.shape[0]
    tr = _pick(S, 256, 8)
    has_res = res is not None

    def body(*refs):
        if has_res:
            x_ref, g_ref, dy_ref, r_ref, dx_ref, dg_ref = refs
        else:
            x_ref, g_ref, dy_ref, dx_ref, dg_ref = refs
        _, vjp = jax.vjp(_rms, x_ref[...].astype(F32), g_ref[...])
        dx, dg = vjp(dy_ref[...].astype(F32))
        if has_res:
            dx = dx + r_ref[...]
        dx_ref[...] = dx.astype(dx_ref.dtype)

        @pl.when(pl.program_id(0) == 0)
        def _():
            dg_ref[...] = jnp.zeros_like(dg_ref)

        dg_ref[...] += dg

    row = pl.BlockSpec((tr, W), lambda i: (i, 0))
    vec = pl.BlockSpec((1, W), lambda i: (0, 0))
    in_specs = [pl.BlockSpec((tr, W), lambda i: (i, cb)), vec, row] + ([row] if has_res else [])
    args = (x, g, dy) + ((res,) if has_res else ())
    return pl.pallas_call(
        body, out_shape=(jax.ShapeDtypeStruct((S, W), out_dtype), jax.ShapeDtypeStruct((1, W), F32)),
        grid=(S // tr,), in_specs=in_specs, out_specs=(row, vec), name=name,
        compiler_params=_cparams("arbitrary"))(*args)


def rope_tables(S, d):
    pos = jnp.arange(S, dtype=F32)
    inv_freq = ROPE_THETA ** (-jnp.arange(0, d, 2, dtype=F32) / d)
    ang = pos[:, None] * inv_freq[None, :]
    cos, sin = jnp.cos(ang), jnp.sin(ang)
    half = d // 2
    z = jnp.zeros((S, LANES - d), F32)
    zh = jnp.zeros((S, half), F32)
    c = jnp.concatenate([cos, cos, z], axis=1)
    sa = jnp.concatenate([-sin, zh, z], axis=1)
    sb = jnp.concatenate([zh, sin, z], axis=1)
    return c, sa, sb, half


def rope_apply(name, x, cb, H, tabs, scale, out_dtype, transpose=False):
    c, sa, sb, half = tabs
    S = x.shape[0]
    tr = _pick(S, 512, 8)
    up, down = LANES - half, half

    def body(x_ref, c_ref, sa_ref, sb_ref, o_ref):
        xv = x_ref[...].astype(F32)
        if not transpose:
            y = xv * c_ref[...] + pltpu.roll(xv, up, 1) * sa_ref[...] + pltpu.roll(xv, down, 1) * sb_ref[...]
            y = y * scale
        else:
            xv = xv * scale
            y = (xv * c_ref[...] + pltpu.roll(xv * sa_ref[...], down, 1)
                 + pltpu.roll(xv * sb_ref[...], up, 1))
        o_ref[...] = y.astype(o_ref.dtype)

    tab = pl.BlockSpec((tr, LANES), lambda h, i: (i, 0))
    return pl.pallas_call(
        body, out_shape=jax.ShapeDtypeStruct((S, H * LANES), out_dtype), grid=(H, S // tr),
        in_specs=[pl.BlockSpec((tr, LANES), lambda h, i: (i, cb + h)), tab, tab, tab],
        out_specs=pl.BlockSpec((tr, LANES), lambda h, i: (i, h)), name=name,
        compiler_params=_cparams("parallel", "parallel"))(x, c, sa, sb)


def _ret_out(o, g):
    y = o * lax.rsqrt(jnp.mean(o * o, axis=-1, keepdims=True) + NORM_EPS)
    return y * jax.nn.silu(g)


def ret_out_fwd(name, o, gsrc, g_cb, out_dtype):
    S = o.shape[0]
    tr = _pick(S, 512, 8)
    W = RET_DV

    def body(o_ref, g_ref, y_ref):
        y_ref[...] = _ret_out(o_ref[...], g_ref[...].astype(F32)).astype(y_ref.dtype)

    blk = pl.BlockSpec((tr, W), lambda h, i: (i, h))
    return pl.pallas_call(
        body, out_shape=jax.ShapeDtypeStruct((S, RET_HEADS * W), out_dtype), grid=(RET_HEADS, S // tr),
        in_specs=[blk, pl.BlockSpec((tr, W), lambda h, i: (i, g_cb + h))], out_specs=blk, name=name,
        compiler_params=_cparams("parallel", "parallel"))(o, gsrc)


def ret_out_bwd(name, o, gsrc, g_cb, dy):
    S = o.shape[0]
    tr = _pick(S, 512, 8)
    W = RET_DV

    def body(o_ref, g_ref, dy_ref, do_ref, dg_ref):
        _, vjp = jax.vjp(_ret_out, o_ref[...], g_ref[...].astype(F32))
        do, dg = vjp(dy_ref[...].astype(F32))
        do_ref[...] = do.astype(do_ref.dtype)
        dg_ref[...] = dg.astype(dg_ref.dtype)

    blk = pl.BlockSpec((tr, W), lambda h, i: (i, h))
    return pl.pallas_call(
        body, out_shape=(jax.ShapeDtypeStruct((S, RET_HEADS * W), F32),
                         jax.ShapeDtypeStruct((S, RET_HEADS * W), BF16)),
        grid=(RET_HEADS, S // tr),
        in_specs=[blk, pl.BlockSpec((tr, W), lambda h, i: (i, g_cb + h)), blk], out_specs=(blk, blk),
        name=name, compiler_params=_cparams("parallel", "parallel"))(o, gsrc, dy)


def _merge(g0, g1, g2, a, b, c):
    return jax.nn.sigmoid(g0) * a + jax.nn.sigmoid(g1) * b + jax.nn.sigmoid(g2) * c


def merge_fwd(name, P, gates_cb, a, b, c, out_dtype):
    S, D = a.shape
    tr = _pick(S, 128, 8)

    def body(g0, g1, g2, a_ref, b_ref, c_ref, o_ref):
        o_ref[...] = _merge(g0[...], g1[...], g2[...], a_ref[...], b_ref[...], c_ref[...]).astype(o_ref.dtype)

    row = pl.BlockSpec((tr, D), lambda i: (i, 0))
    gs = [pl.BlockSpec((tr, D), lambda i, k=k: (i, gates_cb + k)) for k in range(3)]
    return pl.pallas_call(
        body, out_shape=jax.ShapeDtypeStruct((S, D), out_dtype), grid=(S // tr,),
        in_specs=gs + [row, row, row], out_specs=row, name=name,
        compiler_params=_cparams("parallel"))(P, P, P, a, b, c)


def merge_bwd(name, P, gates_cb, a, b, c, dm):
    S, D = a.shape
    tr = _pick(S, 128, 8)

    def body(g0, g1, g2, a_ref, b_ref, c_ref, dm_ref, dg_ref, da_ref, db_ref, dc_ref):
        _, vjp = jax.vjp(_merge, g0[...], g1[...], g2[...], a_ref[...], b_ref[...], c_ref[...])
        d0, d1, d2, da, db, dc = vjp(dm_ref[...].astype(F32))
        dg_ref[:, 0:D] = d0.astype(dg_ref.dtype)
        dg_ref[:, D:2 * D] = d1.astype(dg_ref.dtype)
        dg_ref[:, 2 * D:3 * D] = d2.astype(dg_ref.dtype)
        da_ref[...] = da.astype(da_ref.dtype)
        db_ref[...] = db.astype(db_ref.dtype)
        dc_ref[...] = dc.astype(dc_ref.dtype)

    row = pl.BlockSpec((tr, D), lambda i: (i, 0))
    gs = [pl.BlockSpec((tr, D), lambda i, k=k: (i, gates_cb + k)) for k in range(3)]
    bf = jax.ShapeDtypeStruct((S, D), BF16)
    return pl.pallas_call(
        body, out_shape=(jax.ShapeDtypeStruct((S, 3 * D), BF16), bf, bf, bf), grid=(S // tr,),
        in_specs=gs + [row, row, row, row],
        out_specs=(pl.BlockSpec((tr, 3 * D), lambda i: (i, 0)), row, row, row), name=name,
        compiler_params=_cparams("parallel"))(P, P, P, a, b, c, dm)


HALO = 8


def _conv_pre(u_ref, uh_ref, ext_ref, cw_ref, cb_ref, tr):
    i = pl.program_id(1)
    u = u_ref[...]
    ext_ref[0:HALO, :] = jnp.where(i > 0, uh_ref[...], 0.0)
    ext_ref[HALO:HALO + tr, :] = u
    u1 = ext_ref[HALO - 1:HALO - 1 + tr, :]
    u2 = ext_ref[HALO - 2:HALO - 2 + tr, :]
    cw = cw_ref[...]
    uc = cb_ref[...] + ((cw[0:1, :] * u2 + cw[1:2, :] * u1) + cw[2:3, :] * u)
    return u, u1, u2, uc


def _ffn_specs(S, f, tr, l):
    nb = tr // HALO
    row = pl.BlockSpec((None, tr, f), lambda j, i: (j, i, 0))
    prev = pl.BlockSpec((None, HALO, f), lambda j, i: (j, jnp.maximum(i * nb - 1, 0), 0))
    nxt = pl.BlockSpec((None, HALO, f), lambda j, i: (j, jnp.minimum((i + 1) * nb, S // HALO - 1), 0))
    cw = pl.BlockSpec((None, None, 3, f), lambda j, i: (l, j, 0, 0))
    cb = pl.BlockSpec((None, None, 1, f), lambda j, i: (l, j, 0, 0))
    return row, prev, nxt, cw, cb


def ffn_act_fwd(name, u, gt, cw, cb, l, out_dtype):
    _, S, f = u.shape
    tr = _pick(S, 512, 8)
    row, prev, _, cws, cbs = _ffn_specs(S, f, tr, l)

    def body(u_ref, uh_ref, gt_ref, cw_ref, cb_ref, o_ref, ext_ref):
        _, _, _, uc = _conv_pre(u_ref, uh_ref, ext_ref, cw_ref, cb_ref, tr)
        o_ref[...] = (jax.nn.gelu(uc) * gt_ref[...]).astype(o_ref.dtype)

    return pl.pallas_call(
        body, out_shape=jax.ShapeDtypeStruct((N_DEV, S, f), out_dtype), grid=(N_DEV, S // tr),
        in_specs=[row, prev, row, cws, cbs], out_specs=row,
        scratch_shapes=[pltpu.VMEM((tr + HALO, f), F32)], name=name,
        compiler_params=_cparams("parallel", "parallel"))(u, u, gt, cw, cb)


def ffn_act_bwd_point(name, u, gt, cw, cb, l, dact):
    _, S, f = u.shape
    tr = _pick(S, 512, 8)
    row, prev, _, cws, cbs = _ffn_specs(S, f, tr, l)

    def body(u_ref, uh_ref, gt_ref, cw_ref, cb_ref, da_ref, g_ref, dgt_ref, ext_ref):
        _, _, _, uc = _conv_pre(u_ref, uh_ref, ext_ref, cw_ref, cb_ref, tr)
        _, vjp = jax.vjp(lambda c, t: jax.nn.gelu(c) * t, uc, gt_ref[...])
        g, dgt = vjp(da_ref[...].astype(F32))
        g_ref[...] = g
        dgt_ref[...] = dgt.astype(dgt_ref.dtype)

    return pl.pallas_call(
        body, out_shape=(jax.ShapeDtypeStruct((N_DEV, S, f), F32), jax.ShapeDtypeStruct((N_DEV, S, f), BF16)),
        grid=(N_DEV, S // tr), in_specs=[row, prev, row, cws, cbs, row], out_specs=(row, row),
        scratch_shapes=[pltpu.VMEM((tr + HALO, f), F32)], name=name,
        compiler_params=_cparams("parallel", "parallel"))(u, u, gt, cw, cb, dact)


def ffn_act_bwd_conv(name, u, g, cw, l):
    _, S, f = u.shape
    tr = _pick(S, 512, 8)
    nt = S // tr
    row, prev, nxt, cws, _ = _ffn_specs(S, f, tr, l)

    def body(u_ref, uh_ref, g_ref, gn_ref, cw_ref, du_ref, dcw_ref, dcb_ref, ext_ref, gext_ref):
        i = pl.program_id(1)
        u = u_ref[...]
        ext_ref[0:HALO, :] = jnp.where(i > 0, uh_ref[...], 0.0)
        ext_ref[HALO:HALO + tr, :] = u
        u1 = ext_ref[HALO - 1:HALO - 1 + tr, :]
        u2 = ext_ref[HALO - 2:HALO - 2 + tr, :]
        g = g_ref[...]
        gext_ref[0:tr, :] = g
        gext_ref[tr:tr + HALO, :] = jnp.where(i < nt - 1, gn_ref[...], 0.0)
        g1 = gext_ref[1:1 + tr, :]
        g2 = gext_ref[2:2 + tr, :]
        cw = cw_ref[...]
        du_ref[...] = (cw[2:3, :] * g + cw[1:2, :] * g1 + cw[0:1, :] * g2).astype(du_ref.dtype)

        @pl.when(i == 0)
        def _():
            dcw_ref[...] = jnp.zeros_like(dcw_ref)
            dcb_ref[...] = jnp.zeros_like(dcb_ref)

        dcw_ref[0:1, :] += jnp.sum(g * u2, axis=0, keepdims=True)
        dcw_ref[1:2, :] += jnp.sum(g * u1, axis=0, keepdims=True)
        dcw_ref[2:3, :] += jnp.sum(g * u, axis=0, keepdims=True)
        dcb_ref[...] += jnp.sum(g, axis=0, keepdims=True)

    return pl.pallas_call(
        body, out_shape=(jax.ShapeDtypeStruct((N_DEV, S, f), BF16), jax.ShapeDtypeStruct((N_DEV, 3, f), F32),
                         jax.ShapeDtypeStruct((N_DEV, 1, f), F32)),
        grid=(N_DEV, nt), in_specs=[row, prev, row, nxt, cws],
        out_specs=(row, pl.BlockSpec((None, 3, f), lambda j, i: (j, 0, 0)),
                   pl.BlockSpec((None, 1, f), lambda j, i: (j, 0, 0))),
        scratch_shapes=[pltpu.VMEM((tr + HALO, f), F32), pltpu.VMEM((tr + HALO, f), F32)], name=name,
        compiler_params=_cparams("parallel", "arbitrary"))(u, u, g, g, cw)


def loss_head(name, x, g, tgt):
    S, D = x.shape
    tr = _pick(S, 256, 8)

    def body(x_ref, g_ref, t_ref, l_ref, dx_ref, dg_ref):
        tg = t_ref[...]

        def f(xv, gv):
            err = jnp.square(_rms(xv, gv) - tg)
            return 0.5 * jnp.sum(jnp.mean(err, axis=-1))

        val, vjp = jax.vjp(f, x_ref[...], g_ref[...])
        dx, dg = vjp(jnp.ones((), F32))
        dx_ref[...] = dx

        @pl.when(pl.program_id(0) == 0)
        def _():
            l_ref[...] = jnp.zeros_like(l_ref)
            dg_ref[...] = jnp.zeros_like(dg_ref)

        l_ref[...] += val
        dg_ref[...] += dg

    row = pl.BlockSpec((tr, D), lambda i: (i, 0))
    vec = pl.BlockSpec((1, D), lambda i: (0, 0))
    lt = pl.BlockSpec((8, LANES), lambda i: (0, 0))
    return pl.pallas_call(
        body, out_shape=(jax.ShapeDtypeStruct((8, LANES), F32), jax.ShapeDtypeStruct((S, D), F32),
                         jax.ShapeDtypeStruct((1, D), F32)),
        grid=(S // tr,), in_specs=[row, vec, row], out_specs=(lt, row, vec), name=name,
        compiler_params=_cparams("arbitrary"))(x, g, tgt)


def _tri(n, fn):
    r = lax.broadcasted_iota(jnp.int32, (n, n), 0)
    c = lax.broadcasted_iota(jnp.int32, (n, n), 1)
    return jnp.where(fn(r, c), 1.0, 0.0).astype(F32)


def _log_sigmoid(z):
    return jnp.minimum(z, 0.0) - jnp.log1p(jnp.exp(-jnp.abs(z)))


def fox_gate_fwd(name, ft, b):
    H, R, _ = ft.shape

    def body(f_ref, b_ref, o_ref):
        ls = _log_sigmoid(f_ref[...] + b_ref[...])
        cum = jnp.dot(ls, _tri(LANES, lambda r, c: r <= c), precision=HIGHEST, preferred_element_type=F32)
        tot = jnp.broadcast_to(cum[:, LANES - 1:LANES], (R, LANES))
        off = jnp.dot(_tri(R, lambda r, c: r > c), tot, precision=HIGHEST, preferred_element_type=F32)
        o_ref[...] = -(cum + off)

    blk = pl.BlockSpec((None, R, LANES), lambda h: (h, 0, 0))
    return pl.pallas_call(
        body, out_shape=jax.ShapeDtypeStruct((H, R, LANES), F32), grid=(H,),
        in_specs=[blk, pl.BlockSpec((None, 1, LANES), lambda h: (h, 0, 0))], out_specs=blk, name=name,
        compiler_params=_cparams("parallel"))(ft, b)


def fox_gate_bwd(name, ft, b, dkb):
    H, R, _ = ft.shape

    def body(f_ref, b_ref, d_ref, df_ref, db_ref):
        z = f_ref[...] + b_ref[...]
        d = d_ref[...]
        rev = jnp.dot(d, _tri(LANES, lambda r, c: r >= c), precision=HIGHEST, preferred_element_type=F32)
        tot = jnp.broadcast_to(rev[:, 0:1], (R, LANES))
        off = jnp.dot(_tri(R, lambda r, c: r < c), tot, precision=HIGHEST, preferred_element_type=F32)
        dls = -(rev + off)
        dz = dls * jax.nn.sigmoid(-z)
        df_ref[...] = dz
        s = jnp.sum(jnp.sum(dz, axis=1, keepdims=True), axis=0, keepdims=True)
        db_ref[...] = jnp.broadcast_to(s, (1, LANES))

    blk = pl.BlockSpec((None, R, LANES), lambda h: (h, 0, 0))
    vec = pl.BlockSpec((None, 1, LANES), lambda h: (h, 0, 0))
    return pl.pallas_call(
        body, out_shape=(jax.ShapeDtypeStruct((H, R, LANES), F32), jax.ShapeDtypeStruct((H, 1, LANES), F32)),
        grid=(H,), in_specs=[blk, vec, blk], out_specs=(blk, vec), name=name,
        compiler_params=_cparams("parallel"))(ft, b, dkb)


def _ret_log_gamma(h):
    lg = [float(np.log(np.float32(1.0) - np.float32(2.0) ** np.float32(-5.0 - i))) for i in range(RET_HEADS)]
    out = jnp.float32(lg[RET_HEADS - 1])
    for i in range(RET_HEADS - 2, -1, -1):
        out = jnp.where(h == i, jnp.float32(lg[i]), out)
    return out


def _visible(mode, B):
    r = lax.broadcasted_iota(jnp.int32, (B, B), 0)
    c = lax.broadcasted_iota(jnp.int32, (B, B), 1)
    if mode == "fox":
        return c <= r
    return (c // CHUNK) <= (r // CHUNK)


def _decay(lg, B, blocks_apart):
    r = lax.broadcasted_iota(jnp.int32, (B, B), 0)
    c = lax.broadcasted_iota(jnp.int32, (B, B), 1)
    dist = jnp.abs(r - c + blocks_apart * B).astype(F32)
    return jnp.exp(lg * dist)


def _attn_block(S):
    return 512 if S >= 2048 else 128


def attn_fwd(name, mode, q1, q1_cb, k1, k1_cb, v, v_cb, H, dv, scale, q2=None, q2_cb=0, k2=None, kbias=None):
    S = q1.shape[0]
    B = _attn_block(S)
    nq = S // B
    softmax = mode != "ret"
    two = mode == "mla"
    has_bias = mode == "fox"

    def body(*refs):
        it = iter(refs)
        q1_ref, k1_ref, v_ref = next(it), next(it), next(it)
        q2_ref = next(it) if two else None
        k2_ref = next(it) if two else None
        kb_ref = next(it) if has_bias else None
        o_ref = next(it)
        lse_ref = next(it) if softmax else None
        kbuf, vbuf = next(it), next(it)
        k2buf = next(it) if two else None
        acc = next(it)
        m_ref = next(it) if softmax else None
        l_ref = next(it) if softmax else None
        s_all = next(it) if softmax else None
        h = pl.program_id(0)
        i = pl.program_id(1)

        @pl.when(i == 0)
        def _():
            kbuf[...] = k1_ref[...].astype(BF16)
            vbuf[...] = v_ref[...].astype(BF16)
            if two:
                k2buf[...] = k2_ref[...].astype(BF16)

        qb = q1_ref[...].astype(BF16)
        q2b = q2_ref[...].astype(BF16) if two else None
        lg = _ret_log_gamma(h) if mode == "ret" else None
        acc[...] = jnp.zeros_like(acc)
        if softmax:
            m_ref[...] = jnp.full_like(m_ref, NEG_BIG)
            l_ref[...] = jnp.zeros_like(l_ref)

        def scores(g, diag):
            rows = slice(g * B, (g + 1) * B)
            s = lax.dot_general(qb, kbuf[rows, :], NT_DIMS, preferred_element_type=F32)
            if two:
                s = s + lax.dot_general(q2b, k2buf[rows, :], NT_DIMS, preferred_element_type=F32)
            if softmax:
                s = s * scale
                if has_bias:
                    s = s + kb_ref[g]
                if diag:
                    s = jnp.where(_visible(mode, B), s, NEG_BIG)
                s_all[:, rows] = s
                m_ref[...] = jnp.maximum(m_ref[...], jnp.max(s, axis=1, keepdims=True))
            else:
                if diag:
                    p = jnp.where(_visible(mode, B), s * _decay(lg, B, 0), 0.0)
                else:
                    p = s * _decay(lg, B, i - g)
                acc[...] += jnp.dot(p.astype(BF16), vbuf[rows, :], preferred_element_type=F32)

        def weighted(g):
            rows = slice(g * B, (g + 1) * B)
            p = jnp.exp(s_all[:, rows] - m_ref[...])
            l_ref[...] += jnp.sum(p, axis=1, keepdims=True)
            acc[...] += jnp.dot(p.astype(BF16), vbuf[rows, :], preferred_element_type=F32)

        for g in range(nq):
            pl.when(g < i)(functools.partial(scores, g, False))
            pl.when(g == i)(functools.partial(scores, g, True))
        if softmax:
            for g in range(nq):
                pl.when(g <= i)(functools.partial(weighted, g))
            o_ref[...] = acc[...] / l_ref[...]
            lse_ref[...] = jnp.broadcast_to(m_ref[...] + jnp.log(l_ref[...]), (B, LANES))
        else:
            o_ref[...] = acc[...]

    in_specs = [pl.BlockSpec((B, LANES), lambda h, i: (i, q1_cb + h)),
                pl.BlockSpec((S, LANES), lambda h, i: (0, k1_cb + h)),
                pl.BlockSpec((S, dv), lambda h, i: (0, v_cb + h))]
    args = [q1, k1, v]
    if two:
        in_specs += [pl.BlockSpec((B, LANES), lambda h, i: (i, q2_cb + h)),
                     pl.BlockSpec((S, LANES), lambda h, i: (0, 0))]
        args += [q2, k2]
    if has_bias:
        in_specs.append(pl.BlockSpec((None, nq, 1, B), lambda h, i: (h, 0, 0, 0)))
        args.append(kbias)
    out_shape = [jax.ShapeDtypeStruct((S, H * dv), F32)]
    out_specs = [pl.BlockSpec((B, dv), lambda h, i: (i, h))]
    if softmax:
        out_shape.append(jax.ShapeDtypeStruct((S, H * LANES), F32))
        out_specs.append(pl.BlockSpec((B, LANES), lambda h, i: (i, h)))
    scratch = [pltpu.VMEM((S, LANES), BF16), pltpu.VMEM((S, dv), BF16)]
    if two:
        scratch.append(pltpu.VMEM((S, LANES), BF16))
    scratch.append(pltpu.VMEM((B, dv), F32))
    if softmax:
        scratch += [pltpu.VMEM((B, 1), F32), pltpu.VMEM((B, 1), F32), pltpu.VMEM((B, S), F32)]
    res = pl.pallas_call(body, out_shape=tuple(out_shape), grid=(H, nq), in_specs=in_specs,
                         out_specs=tuple(out_specs), scratch_shapes=scratch, name=name,
                         compiler_params=_cparams("parallel", "arbitrary"))(*args)
    return res if softmax else (res[0], None)


def attn_bwd(name, mode, q1, q1_cb, k1, k1_cb, v, v_cb, H, dv, scale, do, o=None, lse=None,
             q2=None, q2_cb=0, k2=None, kbias=None):
    S = q1.shape[0]
    B = _attn_block(S)
    nb = S // B
    softmax = mode != "ret"
    two = mode == "mla"
    has_bias = mode == "fox"

    def body(*refs):
        it = iter(refs)
        q1_ref, k1_ref, v_ref, do_ref = next(it), next(it), next(it), next(it)
        o_ref = next(it) if softmax else None
        lse_ref = next(it) if softmax else None
        q2_ref = next(it) if two else None
        k2_ref = next(it) if two else None
        kb_ref = next(it) if has_bias else None
        dq1_ref, dk1_ref, dv_ref = next(it), next(it), next(it)
        dq2_ref = next(it) if two else None
        dk2_ref = next(it) if two else None
        dkb_ref = next(it) if has_bias else None
        drow_ref = next(it) if has_bias else None
        qbuf, dobuf = next(it), next(it)
        q2buf = next(it) if two else None
        delta = next(it) if softmax else None
        dk_acc, dv_acc = next(it), next(it)
        dk2_acc = next(it) if two else None
        dkb_acc = next(it) if has_bias else None
        h = pl.program_id(0)
        j = pl.program_id(1)

        @pl.when(j == 0)
        def _():
            qbuf[...] = q1_ref[...].astype(BF16)
            dobuf[...] = do_ref[...].astype(BF16)
            dq1_ref[...] = jnp.zeros_like(dq1_ref)
            if has_bias:
                drow_ref[...] = jnp.zeros_like(drow_ref)
            if two:
                q2buf[...] = q2_ref[...].astype(BF16)
                dq2_ref[...] = jnp.zeros_like(dq2_ref)
            if softmax:
                def drow(t, carry):
                    rows = pl.ds(pl.multiple_of(t * B, B), B)
                    delta[rows, :] = jnp.sum(do_ref[rows, :].astype(F32) * o_ref[rows, :], axis=1, keepdims=True)
                    return carry
                lax.fori_loop(0, nb, drow, 0)

        kj = k1_ref[...].astype(BF16)
        vj = v_ref[...].astype(BF16)
        k2j = k2_ref[...].astype(BF16) if two else None
        kbj = kb_ref[...] if has_bias else None
        lg = _ret_log_gamma(h) if mode == "ret" else None
        dk_acc[...] = jnp.zeros_like(dk_acc)
        dv_acc[...] = jnp.zeros_like(dv_acc)
        if two:
            dk2_acc[...] = jnp.zeros_like(dk2_acc)
        if has_bias:
            dkb_acc[...] = jnp.zeros_like(dkb_acc)

        def step(i, diag):
            rows = slice(i * B, (i + 1) * B)
            qi = qbuf[rows, :]
            doi = dobuf[rows, :]
            s = lax.dot_general(qi, kj, NT_DIMS, preferred_element_type=F32)
            if two:
                q2i = q2buf[rows, :]
                s = s + lax.dot_general(q2i, k2j, NT_DIMS, preferred_element_type=F32)
            dp = lax.dot_general(doi, vj, NT_DIMS, preferred_element_type=F32)
            if softmax:
                s = s * scale
                if has_bias:
                    s = s + kbj
                if diag:
                    s = jnp.where(_visible(mode, B), s, NEG_BIG)
                p = jnp.exp(s - lse_ref[rows, 0:1])
                ds = p * (dp - delta[rows, :])
                if has_bias:
                    dkb_acc[...] += jnp.sum(ds, axis=0, keepdims=True)
                    drow_ref[rows, :] += jnp.broadcast_to(jnp.sum(ds, axis=1, keepdims=True), (B, LANES))
                dsb = (ds * scale).astype(BF16)
            else:
                if diag:
                    dec = jnp.where(_visible(mode, B), _decay(lg, B, 0), 0.0)
                else:
                    dec = _decay(lg, B, i - j)
                p = s * dec
                dsb = (dp * dec).astype(BF16)
            dv_acc[...] += lax.dot_general(p.astype(BF16), doi, TN_DIMS, preferred_element_type=F32)
            dk_acc[...] += lax.dot_general(dsb, qi, TN_DIMS, preferred_element_type=F32)
            dq1_ref[rows, :] += jnp.dot(dsb, kj, preferred_element_type=F32)
            if two:
                dk2_acc[...] += lax.dot_general(dsb, q2i, TN_DIMS, preferred_element_type=F32)
                dq2_ref[rows, :] += jnp.dot(dsb, k2j, preferred_element_type=F32)

        for i in range(nb):
            pl.when(i == j)(functools.partial(step, i, True))
            pl.when(i > j)(functools.partial(step, i, False))
        dk1_ref[...] = dk_acc[...]
        dv_ref[...] = dv_acc[...]
        if two:
            dk2_ref[...] = dk2_acc[...]
        if has_bias:
            dkb_ref[...] = dkb_acc[...]

    full = lambda w, cb: pl.BlockSpec((S, w), lambda h, j: (0, cb + h))
    blk = lambda w, cb: pl.BlockSpec((B, w), lambda h, j: (j, cb + h))
    in_specs = [full(LANES, q1_cb), blk(LANES, k1_cb), blk(dv, v_cb), full(dv, 0)]
    args = [q1, k1, v, do]
    if softmax:
        in_specs += [full(dv, 0), full(LANES, 0)]
        args += [o, lse]
    if two:
        in_specs += [full(LANES, q2_cb), pl.BlockSpec((B, LANES), lambda h, j: (j, 0))]
        args += [q2, k2]
    if has_bias:
        in_specs.append(pl.BlockSpec((None, None, 1, B), lambda h, j: (h, j, 0, 0)))
        args.append(kbias)
    names = ["dq1", "dk1", "dv"]
    out_shape = [jax.ShapeDtypeStruct((S, H * LANES), F32), jax.ShapeDtypeStruct((S, H * LANES), F32),
                 jax.ShapeDtypeStruct((S, H * dv), F32)]
    out_specs = [full(LANES, 0), blk(LANES, 0), blk(dv, 0)]
    if two:
        names += ["dq2", "dk2h"]
        out_shape += [jax.ShapeDtypeStruct((S, H * LANES), F32)] * 2
        out_specs += [full(LANES, 0), blk(LANES, 0)]
    if has_bias:
        names.append("dkb")
        out_shape.append(jax.ShapeDtypeStruct((H, nb, 1, B), F32))
        out_specs.append(pl.BlockSpec((None, None, 1, B), lambda h, j: (h, j, 0, 0)))
        names.append("drow")
        out_shape.append(jax.ShapeDtypeStruct((S, H * LANES), F32))
        out_specs.append(full(LANES, 0))
    scratch = [pltpu.VMEM((S, LANES), BF16), pltpu.VMEM((S, dv), BF16)]
    if two:
        scratch.append(pltpu.VMEM((S, LANES), BF16))
    if softmax:
        scratch.append(pltpu.VMEM((S, 1), F32))
    scratch += [pltpu.VMEM((B, LANES), F32), pltpu.VMEM((B, dv), F32)]
    if two:
        scratch.append(pltpu.VMEM((B, LANES), F32))
    if has_bias:
        scratch.append(pltpu.VMEM((1, B), F32))
    res = pl.pallas_call(body, out_shape=tuple(out_shape), grid=(H, nb), in_specs=in_specs,
                         out_specs=tuple(out_specs), scratch_shapes=scratch, name=name,
                         compiler_params=_cparams("parallel", "arbitrary"))(*args)
    return dict(zip(names, res))


def head_sum(name, x, H, out_dtype):
    S = x.shape[0]
    tr = _pick(S, 512, 8)

    def body(x_ref, o_ref):
        acc = x_ref[:, 0:LANES]
        for h in range(1, H):
            acc = acc + x_ref[:, h * LANES:(h + 1) * LANES]
        o_ref[...] = acc.astype(o_ref.dtype)

    return pl.pallas_call(
        body, out_shape=jax.ShapeDtypeStruct((S, LANES), out_dtype), grid=(S // tr,),
        in_specs=[pl.BlockSpec((tr, H * LANES), lambda i: (i, 0))],
        out_specs=pl.BlockSpec((tr, LANES), lambda i: (i, 0)), name=name,
        compiler_params=_cparams("parallel"))(x)


def _mesh_pos():
    return lax.axis_index("x"), lax.axis_index("y"), lax.axis_index("c")


def _peer(pos, k):
    x, y, c = pos
    px = 1 - x if k & 4 else x
    py = 1 - y if k & 2 else y
    pc = 1 - c if k & 1 else c
    return (px, py, pc), 4 * px + 2 * py + pc


def exchange(name, tensors):
    nt = len(tensors)
    flat_in, counts = [], []
    out_shape = []
    for mode, srcs in tensors:
        counts.append(len(srcs))
        flat_in += list(srcs)
        rc = srcs[0].shape[-2:]
        out_shape.append(jax.ShapeDtypeStruct((len(srcs), N_DEV) + tuple(rc), srcs[0].dtype))
    n_in = len(flat_in)

    def body(*refs):
        ins = refs[:n_in]
        outs = refs[n_in:n_in + nt]
        send_sems, recv_sems, local_sems = refs[n_in + nt:]
        pos = _mesh_pos()
        me = 4 * pos[0] + 2 * pos[1] + pos[2]
        srcs_of, base = [], 0
        for t in range(nt):
            srcs_of.append(ins[base:base + counts[t]])
            base += counts[t]

        def src_view(t, l, slot):
            ref = srcs_of[t][l]
            return ref if tensors[t][0] == "gather" else ref.at[slot]

        def all_layers(t, slot):
            return outs[t].at[pl.ds(0, counts[t]), slot]

        for t in range(nt):
            for l in range(counts[t]):
                pltpu.make_async_copy(src_view(t, l, me), outs[t].at[l, me], local_sems.at[t]).start()
        for t in range(nt):
            for k in range(1, N_DEV):
                peer, pid = _peer(pos, k)
                for l in range(counts[t]):
                    pltpu.make_async_remote_copy(
                        src_ref=src_view(t, l, pid), dst_ref=outs[t].at[l, me],
                        send_sem=send_sems.at[t, k - 1], recv_sem=recv_sems.at[t, k - 1],
                        device_id=peer, device_id_type=pl.DeviceIdType.MESH).start()
        for t in range(nt):
            for k in range(1, N_DEV):
                peer, pid = _peer(pos, k)
                pltpu.make_async_remote_copy(
                    src_ref=all_layers(t, pid), dst_ref=all_layers(t, pid),
                    send_sem=send_sems.at[t, k - 1], recv_sem=recv_sems.at[t, k - 1],
                    device_id=peer, device_id_type=pl.DeviceIdType.MESH).wait()
        for t in range(nt):
            pltpu.make_async_copy(all_layers(t, me), all_layers(t, me), local_sems.at[t]).wait()

    any_spec = pl.BlockSpec(memory_space=pl.ANY)
    return pl.pallas_call(
        body, out_shape=tuple(out_shape), in_specs=[any_spec] * n_in, out_specs=tuple([any_spec] * nt),
        scratch_shapes=[pltpu.SemaphoreType.DMA((nt, N_DEV - 1)), pltpu.SemaphoreType.DMA((nt, N_DEV - 1)),
                        pltpu.SemaphoreType.DMA((nt,))],
        name=name)(*flat_in)


HBM_SPEC = pl.BlockSpec(memory_space=pltpu.HBM)
SEM_SPEC = pl.BlockSpec(memory_space=pltpu.SEMAPHORE)
DATAFLOW = pltpu.SideEffectType.DATAFLOW_SIDE_EFFECTING


def _hbm(a):
    return pltpu.with_memory_space_constraint(a, pltpu.HBM)


def landing_zones(mode, srcs):
    pos = _mesh_pos()
    me = 4 * pos[0] + 2 * pos[1] + pos[2]
    lands = []
    for s in srcs:
        R, C = s.shape[-2:]
        own = s[None] if mode == "gather" else lax.dynamic_slice(s, (me, 0, 0), (1, R, C))
        lands.append(lax.dynamic_update_slice(lax.empty((N_DEV, R, C), s.dtype), own, (me, 0, 0)))
    return lands


def exchange_start(name, mode, srcs, lands, after=None):
    n = len(srcs)
    extra = [] if after is None else [after]

    def body(*refs):
        src_refs, land_refs = refs[:n], refs[n:2 * n]
        send_sems, recv_sems = refs[2 * n + len(extra)], refs[2 * n + len(extra) + 1]
        token = refs[-1]
        pos = _mesh_pos()
        me = 4 * pos[0] + 2 * pos[1] + pos[2]
        for t in range(n):
            for k in range(1, N_DEV):
                peer, pid = _peer(pos, k)
                src = src_refs[t] if mode == "gather" else src_refs[t].at[pid]
                pltpu.make_async_remote_copy(
                    src_ref=src, dst_ref=land_refs[t].at[me], send_sem=send_sems.at[t], recv_sem=recv_sems.at[t],
                    device_id=peer, device_id_type=pl.DeviceIdType.MESH).start()
        token[...] = jnp.zeros_like(token)

    thru = [pltpu.HBM(a.shape, a.dtype) for a in list(srcs) + list(lands)]
    out_shape = (pltpu.SemaphoreType.DMA((n,)), pltpu.SemaphoreType.DMA((n,)), *thru,
                 jax.ShapeDtypeStruct((8, LANES), F32))
    res = pl.pallas_call(
        body, out_shape=out_shape, in_specs=[HBM_SPEC] * (2 * n) + [pl.BlockSpec(memory_space=pl.ANY)] * len(extra),
        out_specs=(SEM_SPEC, SEM_SPEC, *([HBM_SPEC] * (2 * n)), pl.BlockSpec(memory_space=pltpu.VMEM)),
        input_output_aliases={i: 2 + i for i in range(2 * n)}, name=name,
        compiler_params=pltpu.CompilerParams(has_side_effects=DATAFLOW))(
            *[_hbm(a) for a in list(srcs) + list(lands)], *extra)
    return res[0], res[1], list(res[2:2 + n]), list(res[2 + n:2 + 2 * n]), res[-1]


def exchange_wait(name, send_sems, recv_sems, srcs, lands, after):
    n = len(srcs)

    def body(*refs):
        land_refs = refs[n:2 * n]
        s_sems, r_sems = refs[2 * n], refs[2 * n + 1]
        pos = _mesh_pos()
        for t in range(n):
            seven = land_refs[t].at[pl.ds(0, N_DEV - 1)]
            cp = pltpu.make_async_remote_copy(
                src_ref=seven, dst_ref=seven, send_sem=s_sems.at[t], recv_sem=r_sems.at[t],
                device_id=pos, device_id_type=pl.DeviceIdType.MESH)
            cp.wait_send()
            cp.wait_recv()

    arrs = list(srcs) + list(lands)
    res = pl.pallas_call(
        body, out_shape=tuple(pltpu.HBM(a.shape, a.dtype) for a in arrs),
        in_specs=[HBM_SPEC] * (2 * n) + [SEM_SPEC, SEM_SPEC, pl.BlockSpec(memory_space=pl.ANY)],
        out_specs=tuple([HBM_SPEC] * (2 * n)), input_output_aliases={i: i for i in range(2 * n)}, name=name,
        compiler_params=pltpu.CompilerParams(has_side_effects=DATAFLOW))(*arrs, send_sems, recv_sems, after)
    return list(res[n:])


def reduce_parts(name, parts):
    n, R, C = parts.shape
    tr = _pick(R, max(8, (1 << 20) // (C * 4) // 8 * 8), 8)

    def body(p_ref, o_ref):
        acc = p_ref[0].astype(F32)
        for s in range(1, n):
            acc = acc + p_ref[s].astype(F32)
        o_ref[...] = acc

    return pl.pallas_call(
        body, out_shape=jax.ShapeDtypeStruct((R, C), F32), grid=(R // tr,),
        in_specs=[pl.BlockSpec((n, tr, C), lambda i: (0, i, 0))],
        out_specs=pl.BlockSpec((tr, C), lambda i: (i, 0)), name=name,
        compiler_params=_cparams("parallel"))(parts)


def adamw(name, w, m, v, parts, first=0, prev=None):
    L, R, C = w.shape
    nl = len(parts)
    n = parts[0].shape[0]
    tr = _pick(R, max(8, (1 << 19) // (C * 4) // 8 * 8), 8)
    n_prev = 0 if prev is None else 4

    def body(*refs):
        w_ref, m_ref, v_ref = refs[:3]
        p_refs = refs[3:3 + nl]
        g_ref, d_ref, nm_ref, nv_ref = refs[3 + nl + n_prev:]

        def update(p_ref):
            g = p_ref[0].astype(F32)
            for s in range(1, n):
                g = g + p_ref[s].astype(F32)
            wv = w_ref[...]
            mn = ADAM_B1 * m_ref[...] + (1.0 - ADAM_B1) * g
            vn = ADAM_B2 * v_ref[...] + (1.0 - ADAM_B2) * jnp.square(g)
            m_hat = mn / (1.0 - ADAM_B1 ** ADAM_STEP)
            v_hat = vn / (1.0 - ADAM_B2 ** ADAM_STEP)
            g_ref[...] = g
            d_ref[...] = -ADAM_LR * (m_hat / (jnp.sqrt(v_hat) + ADAM_EPS) + ADAM_WD * wv)
            nm_ref[...] = mn
            nv_ref[...] = vn

        for k in range(nl):
            pl.when(pl.program_id(0) == k)(functools.partial(update, p_refs[k]))

    blk = pl.BlockSpec((None, tr, C), lambda l, i: (first + l, i, 0))
    pspecs = [pl.BlockSpec((n, tr, C), lambda l, i, k=k: (0, jnp.where(l == k, i, 0), 0)) for k in range(nl)]
    sh = jax.ShapeDtypeStruct((L, R, C), F32)
    prev_args = [] if prev is None else list(prev)
    return pl.pallas_call(
        body, out_shape=(sh, sh, sh, sh), grid=(nl, R // tr),
        in_specs=[blk, blk, blk] + pspecs + [pl.BlockSpec(memory_space=pl.ANY)] * n_prev,
        out_specs=(blk, blk, blk, blk), input_output_aliases={3 + nl + q: q for q in range(n_prev)}, name=name,
        compiler_params=_cparams("arbitrary", "arbitrary"))(w, m, v, *parts, *prev_args)


def _cols_from_blocks(g):
    n, R, c = g.shape
    return g.transpose(1, 0, 2).reshape(R, n * c)


def _cols_to_blocks(w):
    R, C = w.shape
    return w.reshape(R, N_DEV, C // N_DEV).transpose(1, 0, 2)


def _uq_permute(w):
    lead = w.shape[:-1]
    w4 = w.reshape(lead + (MLA_HEADS, MLA_NOPE + MLA_ROPE))
    nope = w4[..., :MLA_NOPE].reshape(lead + (MLA_HEADS * MLA_NOPE,))
    rope = jnp.pad(w4[..., MLA_NOPE:], [(0, 0)] * (w4.ndim - 1) + [(0, LANES - MLA_ROPE)])
    return jnp.concatenate([nope, rope.reshape(lead + (MLA_HEADS * LANES,))], axis=-1)


def _uq_unpermute(w):
    lead = w.shape[:-1]
    n = MLA_HEADS * MLA_NOPE
    nope = w[..., :n].reshape(lead + (MLA_HEADS, MLA_NOPE))
    rope = w[..., n:].reshape(lead + (MLA_HEADS, LANES))[..., :MLA_ROPE]
    return jnp.concatenate([nope, rope], axis=-1).reshape(lead + (MLA_HEADS * (MLA_NOPE + MLA_ROPE),))


def _ukv_permute(w):
    lead = w.shape[:-1]
    w4 = w.reshape(lead + (MLA_HEADS, 2, MLA_NOPE))
    return jnp.swapaxes(w4, -3, -2).reshape(lead + (2 * MLA_HEADS * MLA_NOPE,))


def _ukv_unpermute(w):
    lead = w.shape[:-1]
    w4 = w.reshape(lead + (2, MLA_HEADS, MLA_NOPE))
    return jnp.swapaxes(w4, -3, -2).reshape(lead + (2 * MLA_HEADS * MLA_NOPE,))


SMALL = ["norm1_g", "mla_q_norm_g", "mla_kv_norm_g", "fox_b_f", "norm2_g", "ffn_conv_b", "final_norm_g"]
SMALL_TILE = 8 * LANES


def _pack_small(d):
    flat = jnp.concatenate([d[n].reshape(-1).astype(F32) for n in SMALL])
    pad = -flat.shape[0] % SMALL_TILE
    return jnp.pad(flat, (0, pad)).reshape(-1, LANES)


def _unpack_small(packed, like):
    flat = packed.reshape(-1)
    out, o = {}, 0
    for n in SMALL:
        sz = int(np.prod(like[n].shape))
        out[n] = flat[o:o + sz].reshape(like[n].shape)
        o += sz
    return out


WEIGHTS = ["norm1_g", "w_in", "mla_q_norm_g", "mla_kv_norm_g", "mla_w_uq", "mla_w_ukv", "fox_b_f", "w_br_fox",
           "w_br_mla", "w_br_ret", "w_out", "norm2_g", "ffn_w_up", "ffn_w_gate", "ffn_conv_w", "ffn_conv_b",
           "ffn_w_down", "final_norm_g"]
EARLY = ["w_in", "mla_w_uq", "mla_w_ukv"]
LATE = ["w_br_fox", "w_br_mla", "w_br_ret", "w_out", "ffn_w_up", "ffn_w_gate", "ffn_conv_w", "ffn_w_down"]
FFN = ["ffn_w_up", "ffn_w_gate", "ffn_conv_w", "ffn_w_down"]
REST = ["w_br_fox", "w_br_mla", "w_br_ret", "w_out", "mla_w_uq", "mla_w_ukv", "w_in"]
BIG = EARLY + LATE


def kernel(x, norm1_g, w_in, mla_q_norm_g, mla_kv_norm_g, mla_w_uq, mla_w_ukv, fox_b_f, w_br_fox, w_br_mla, w_br_ret, w_out, norm2_g, ffn_w_up, ffn_w_gate, ffn_conv_w, ffn_conv_b, ffn_w_down, final_norm_g, loss_target, m_norm1_g, m_w_in, m_mla_q_norm_g, m_mla_kv_norm_g, m_mla_w_uq, m_mla_w_ukv, m_fox_b_f, m_w_br_fox, m_w_br_mla, m_w_br_ret, m_w_out, m_norm2_g, m_ffn_w_up, m_ffn_w_gate, m_ffn_conv_w, m_ffn_conv_b, m_ffn_w_down, m_final_norm_g, v_norm1_g, v_w_in, v_mla_q_norm_g, v_mla_kv_norm_g, v_mla_w_uq, v_mla_w_ukv, v_fox_b_f, v_w_br_fox, v_w_br_mla, v_w_br_ret, v_w_out, v_norm2_g, v_ffn_w_up, v_ffn_w_gate, v_ffn_conv_w, v_ffn_conv_b, v_ffn_w_down, v_final_norm_g):
    env = dict(locals())
    W = {n: env[n] for n in WEIGHTS}
    Mo = {n: env["m_" + n] for n in WEIGHTS}
    Vo = {n: env["v_" + n] for n in WEIGHTS}
    S, D = x.shape[1], x.shape[2]
    L = w_in.shape[0]
    lay = InLayout(D)
    NP = lay.total
    f = ffn_w_up.shape[-1]
    xs = x.reshape(S, D)
    tgt = loss_target.reshape(S, D)

    local = {n: W[n].astype(BF16) for n in BIG}
    local["w_in"] = lay.permute(W["w_in"]).astype(BF16)
    pending = {}
    token = None
    for l in range(L):
        for grp, names in (("a", EARLY), ("b", LATE)):
            srcs = [local[n][l] for n in names]
            *flight, token = exchange_start(f"gather_start_{l}{grp}", "gather", srcs, landing_zones("gather", srcs),
                                            token)
            pending[l, grp] = flight
    gather_token = token
    cbias_all = ffn_conv_b.reshape(L, 1, N_DEV, 1, f)

    def early_weights(l, after):
        g = dict(zip(EARLY, exchange_wait(f"gather_wait_{l}a", *pending[l, "a"], after)))
        return dict(Win=g["w_in"].reshape(1, D, NP), Wuq=_uq_permute(_cols_from_blocks(g["mla_w_uq"])),
                    Wukv=_ukv_permute(_cols_from_blocks(g["mla_w_ukv"])))

    def late_weights(l, after):
        g = dict(zip(LATE, exchange_wait(f"gather_wait_{l}b", *pending[l, "b"], after)))
        return dict(
            Wout=g["w_out"].reshape(1, D, D), Wbf=_cols_from_blocks(g["w_br_fox"]),
            Wbm=_cols_from_blocks(g["w_br_mla"]), Wbr=_cols_from_blocks(g["w_br_ret"]), Wup=g["ffn_w_up"][None],
            Wgate=g["ffn_w_gate"][None], Wdown=g["ffn_w_down"][None], Wconv=g["ffn_conv_w"].astype(F32)[None],
            cbias=cbias_all[l])

    tab64 = rope_tables(S, MLA_ROPE)
    tab128 = rope_tables(S, RET_DK)
    fox_scale = FOX_DH ** -0.5
    mla_scale = (MLA_NOPE + MLA_ROPE) ** -0.5
    ret_kscale = RET_DK ** -0.5
    R = S // LANES
    AB = _attn_block(S)
    NOPE_W = MLA_HEADS * MLA_NOPE

    def vec(a):
        return a.reshape(1, -1)

    saved = []
    xc = xs
    for l in range(L):
        Wl = early_weights(l, gather_token if l == 0 else xc)
        Win, Wuq, Wukv = Wl["Win"], Wl["Wuq"], Wl["Wukv"]
        s = {"x": xc, "W": Wl}
        h1 = rms_fwd("norm1", xc, 0, D, vec(norm1_g[l]), BF16)
        P = mm_nn("in_proj", h1, Win, F32, b_lead=0)
        s.update(h1=h1, P=P)
        ff_off = lay.off["ff"]
        ft = P[:, ff_off:ff_off + FOX_HEADS].T.reshape(FOX_HEADS, R, LANES)
        bfl = jnp.broadcast_to(fox_b_f[l].reshape(FOX_HEADS, 1, 1), (FOX_HEADS, 1, LANES))
        kbias = fox_gate_fwd("fox_gate", ft, bfl).reshape(FOX_HEADS, S // AB, 1, AB)
        o_fox, lse_fox = attn_fwd("fox_attn", "fox", P, lay.cb("fq", LANES), P, lay.cb("fk", LANES),
                                  P, lay.cb("fv", LANES), FOX_HEADS, FOX_DH, fox_scale, kbias=kbias)
        s.update(ft=ft, bfl=bfl, kbias=kbias, o_fox=o_fox, lse_fox=lse_fox)
        cqn = rms_fwd("mla_q_norm", P, lay.cb("mq", MLA_Q_LORA), MLA_Q_LORA, vec(mla_q_norm_g[l]), BF16)
        qall = mm_nn("mla_uq", cqn, Wuq, F32)
        ckvn = rms_fwd("mla_kv_norm", P, lay.cb("mkv", MLA_KV_LORA), MLA_KV_LORA, vec(mla_kv_norm_g[l]), BF16)
        kvall = mm_nn("mla_ukv", ckvn, Wukv, F32)
        qrope = rope_apply("mla_q_rope", qall, NOPE_W // LANES, MLA_HEADS, tab64, 1.0, F32)
        krope = rope_apply("mla_k_rope", P, lay.cb("mkr", LANES), 1, tab64, 1.0, F32)
        o_mla, lse_mla = attn_fwd("mla_attn", "mla", qall, 0, kvall, 0, kvall, NOPE_W // MLA_V, MLA_HEADS, MLA_V,
                                  mla_scale, q2=qrope, q2_cb=0, k2=krope)
        s.update(cqn=cqn, qall=qall, ckvn=ckvn, kvall=kvall, qrope=qrope, krope=krope, o_mla=o_mla,
                 lse_mla=lse_mla)
        rq = rope_apply("ret_q_rope", P, lay.cb("rq", LANES), RET_HEADS, tab128, 1.0, F32)
        rk = rope_apply("ret_k_rope", P, lay.cb("rk", LANES), RET_HEADS, tab128, ret_kscale, F32)
        o_ret, _ = attn_fwd("ret_attn", "ret", rq, 0, rk, 0, P, lay.cb("rv", RET_DV), RET_HEADS, RET_DV, 1.0)
        c_ret = ret_out_fwd("ret_out", o_ret, P, lay.cb("rg", RET_DV), BF16)
        s.update(rq=rq, rk=rk, o_ret=o_ret, c_ret=c_ret)
        Wl.update(late_weights(l, c_ret))
        Wout, Wbf, Wbm, Wbr = Wl["Wout"], Wl["Wbf"], Wl["Wbm"], Wl["Wbr"]
        Wup, Wgate, Wdown, Wconv, cbias = (Wl[k] for k in ("Wup", "Wgate", "Wdown", "Wconv", "cbias"))
        A = mm_nn("br_fox", o_fox, Wbf, F32)
        Bm = mm_nn("br_mla", o_mla, Wbm, F32)
        C = mm_nn("br_ret", c_ret, Wbr, F32)
        merged = merge_fwd("merge", P, lay.cb("gates", D), A, Bm, C, BF16)
        x2 = mm_nn("out_proj", merged, Wout, F32, b_lead=0, res=xc)
        s.update(A=A, Bm=Bm, C=C, merged=merged, x2=x2)
        h2 = rms_fwd("norm2", x2, 0, D, vec(norm2_g[l]), BF16)
        u = ffn_up("ffn_up", h2, Wup, 0, F32)
        gt = ffn_up("ffn_gate", h2, Wgate, 0, F32)
        act = ffn_act_fwd("ffn_act", u, gt, Wconv, cbias, 0, BF16)
        xc = ffn_down("ffn_down", act, Wdown, 0, x2, F32)
        s.update(h2=h2, u=u, gt=gt, act=act)
        saved.append(s)

    loss_tile, dx, dgf = loss_head("loss_head", xc, vec(final_norm_g), tgt)
    loss = lax.psum(loss_tile[0, 0], ("x", "y", "c"))

    gbig = {n: [None] * L for n in BIG}
    gsmall = {n: [None] * L for n in SMALL if n != "final_norm_g"}
    scattering = {}
    scatter_token = None

    def start_scatter(name, names, l):
        srcs = [gbig[n][l] for n in names]
        *flight, tok = exchange_start(name, "scatter", srcs, landing_zones("scatter", srcs))
        return flight, tok

    for l in reversed(range(L)):
        s = saved[l]
        P = s["P"]
        Wl = s["W"]
        Win, Wout, Wuq, Wukv, Wbf, Wbm, Wbr = (Wl[k] for k in ("Win", "Wout", "Wuq", "Wukv", "Wbf", "Wbm", "Wbr"))
        Wup, Wgate, Wdown, Wconv, cbias = (Wl[k] for k in ("Wup", "Wgate", "Wdown", "Wconv", "cbias"))
        dxb = (dx if scatter_token is None else dx + scatter_token[0, 0]).astype(BF16)
        dact = ffn_down_bwd_act("ffn_down_da", dxb, Wdown, 0, BF16)
        gbig["ffn_w_down"][l] = ffn_down_bwd_w("ffn_down_dw", s["act"], dxb, BF16)
        g, dgt = ffn_act_bwd_point("ffn_act_bwd", s["u"], s["gt"], Wconv, cbias, 0, dact)
        du, dcw, dcb = ffn_act_bwd_conv("ffn_conv_bwd", s["u"], g, Wconv, 0)
        gbig["ffn_conv_w"][l] = dcw.astype(BF16)
        gsmall["ffn_conv_b"][l] = dcb.reshape(-1)
        gbig["ffn_w_up"][l] = ffn_up_bwd_w("ffn_up_dw", s["h2"], du, BF16)
        gbig["ffn_w_gate"][l] = ffn_up_bwd_w("ffn_gate_dw", s["h2"], dgt, BF16)
        dh2 = ffn_up_bwd_h("ffn_up_dh", du, Wup, 0, None, F32)
        dh2 = ffn_up_bwd_h("ffn_gate_dh", dgt, Wgate, 0, dh2, BF16)
        dx2, dg2 = rms_bwd("norm2_bwd", s["x2"], 0, D, vec(norm2_g[l]), dh2, F32, res=dx)
        gsmall["norm2_g"][l] = dg2.reshape(-1)
        scattering[l, "ffn"], scatter_token = start_scatter(f"scatter_start_{l}ffn", FFN, l)
        dx2b = (dx2 + scatter_token[0, 0]).astype(BF16)
        dmerged = mm_nt("out_proj_dm", dx2b, Wout, BF16, b_lead=0)
        gbig["w_out"][l] = mm_tn("out_proj_dw", s["merged"], dx2b, BF16).reshape(N_DEV, D // N_DEV, D)
        dgates, dA, dB, dC = merge_bwd("merge_bwd", P, lay.cb("gates", D), s["A"], s["Bm"], s["C"], dmerged)
        do_fox = mm_nt("br_fox_do", dA, Wbf, F32)
        do_mla = mm_nt("br_mla_do", dB, Wbm, F32)
        dc_ret = mm_nt("br_ret_do", dC, Wbr, BF16)
        gbig["w_br_fox"][l] = _cols_to_blocks(mm_tn("br_fox_dw", s["o_fox"], dA, BF16))
        gbig["w_br_mla"][l] = _cols_to_blocks(mm_tn("br_mla_dw", s["o_mla"], dB, BF16))
        gbig["w_br_ret"][l] = _cols_to_blocks(mm_tn("br_ret_dw", s["c_ret"], dC, BF16))
        do_ret, drg = ret_out_bwd("ret_out_bwd", s["o_ret"], P, lay.cb("rg", RET_DV), dc_ret)
        rb = attn_bwd("ret_attn_bwd", "ret", s["rq"], 0, s["rk"], 0, P, lay.cb("rv", RET_DV), RET_HEADS, RET_DV,
                      1.0, do_ret)
        drq = rope_apply("ret_q_rope_bwd", rb["dq1"], 0, RET_HEADS, tab128, 1.0, BF16, transpose=True)
        drk = rope_apply("ret_k_rope_bwd", rb["dk1"], 0, RET_HEADS, tab128, ret_kscale, BF16, transpose=True)
        drv = rb["dv"].astype(BF16)
        mb = attn_bwd("mla_attn_bwd", "mla", s["qall"], 0, s["kvall"], 0, s["kvall"], NOPE_W // MLA_V, MLA_HEADS,
                      MLA_V, mla_scale, do_mla, o=s["o_mla"], lse=s["lse_mla"], q2=s["qrope"], q2_cb=0,
                      k2=s["krope"])
        dqrope = rope_apply("mla_q_rope_bwd", mb["dq2"], 0, MLA_HEADS, tab64, 1.0, BF16, transpose=True)
        dkr_sum = head_sum("mla_k_rope_sum", mb["dk2h"], MLA_HEADS, F32)
        dmkr = rope_apply("mla_k_rope_bwd", dkr_sum, 0, 1, tab64, 1.0, BF16, transpose=True)
        dqall = jnp.concatenate([mb["dq1"].astype(BF16), dqrope], axis=1)
        dkvall = jnp.concatenate([mb["dk1"].astype(BF16), mb["dv"].astype(BF16)], axis=1)
        dcqn = mm_nt("mla_uq_dx", dqall, Wuq, F32)
        dckvn = mm_nt("mla_ukv_dx", dkvall, Wukv, F32)
        guq = _uq_unpermute(mm_tn("mla_uq_dw", s["cqn"], dqall, BF16))
        gukv = _ukv_unpermute(mm_tn("mla_ukv_dw", s["ckvn"], dkvall, BF16))
        gbig["mla_w_uq"][l] = _cols_to_blocks(guq)
        gbig["mla_w_ukv"][l] = _cols_to_blocks(gukv)
        dmq, dgq = rms_bwd("mla_q_norm_bwd", P, lay.cb("mq", MLA_Q_LORA), MLA_Q_LORA, vec(mla_q_norm_g[l]),
                           dcqn, BF16)
        dmkv, dgkv = rms_bwd("mla_kv_norm_bwd", P, lay.cb("mkv", MLA_KV_LORA), MLA_KV_LORA,
                             vec(mla_kv_norm_g[l]), dckvn, BF16)
        gsmall["mla_q_norm_g"][l] = dgq.reshape(-1)
        gsmall["mla_kv_norm_g"][l] = dgkv.reshape(-1)
        fb = attn_bwd("fox_attn_bwd", "fox", P, lay.cb("fq", LANES), P, lay.cb("fk", LANES), P,
                      lay.cb("fv", LANES), FOX_HEADS, FOX_DH, fox_scale, do_fox, o=s["o_fox"], lse=s["lse_fox"],
                      kbias=s["kbias"])
        drow = fb["drow"].reshape(S, FOX_HEADS, LANES)[:, :, 0].T.reshape(FOX_HEADS, R, LANES)
        dft, dbf = fox_gate_bwd("fox_gate_bwd", s["ft"], s["bfl"], fb["dkb"].reshape(FOX_HEADS, R, LANES) - drow)
        gsmall["fox_b_f"][l] = dbf[:, 0, 0]
        dff = jnp.pad(dft.reshape(FOX_HEADS, S).T, ((0, 0), (0, LANES - FOX_HEADS))).astype(BF16)
        segs = dict(gates=dgates, rv=drv, rg=drg, mq=dmq, rq=drq, rk=drk, mkv=dmkv, fq=fb["dq1"].astype(BF16),
                    fk=fb["dk1"].astype(BF16), fv=fb["dv"].astype(BF16), mkr=dmkr, ff=dff)
        dP = jnp.concatenate([segs[n] for n in lay.order], axis=1)
        gbig["w_in"][l] = mm_tn("in_proj_dw", s["h1"], dP, BF16).reshape(N_DEV, D // N_DEV, NP)
        scattering[l, "rest"], scatter_token = start_scatter(f"scatter_start_{l}rest", REST, l)
        dh1 = mm_nt("in_proj_dh", dP, Win, BF16, b_lead=0)
        dx, dg1 = rms_bwd("norm1_bwd", s["x"], 0, D, vec(norm1_g[l]) + scatter_token[0:1, 0:1], dh1, F32, res=dx2)
        gsmall["norm1_g"][l] = dg1.reshape(-1)

    small_like = {n: W[n] for n in SMALL}
    small_part = {n: jnp.stack(gsmall[n]) for n in gsmall}
    small_part["final_norm_g"] = dgf.reshape(-1)
    small_recv = exchange("gather_small_grads", [("gather", [_pack_small(small_part)])])[0]
    ps = adamw("adamw_small", _pack_small(small_like)[None], _pack_small({n: Mo[n] for n in SMALL})[None],
               _pack_small({n: Vo[n] for n in SMALL})[None], [small_recv[0]])

    def received(l, after):
        r = {}
        for grp, names in (("ffn", FFN), ("rest", REST)):
            r.update(zip(names, exchange_wait(f"scatter_wait_{l}{grp}", *scattering[l, grp], after)))
        r["w_in"] = lay.unpermute(reduce_parts("w_in_grad_sum", r["w_in"]))[None]
        return r

    out = {}
    if L > 1:
        recv = [received(l, dx) for l in range(1, L)]
        for n in BIG:
            out[n] = adamw("adamw_" + n, W[n], Mo[n], Vo[n], [r[n] for r in recv], first=1)
    recv0 = received(0, out[BIG[-1]][0] if L > 1 else dx)
    for n in BIG:
        out[n] = adamw("adamw0_" + n, W[n], Mo[n], Vo[n], [recv0[n]], first=0, prev=out.get(n))
    small_out = [_unpack_small(a[0], small_like) for a in ps]
    for n in SMALL:
        out[n] = tuple(so[n] for so in small_out)

    grads = [out[n][0] for n in WEIGHTS]
    deltas = [out[n][1] for n in WEIGHTS]
    new_m = [out[n][2] for n in WEIGHTS]
    new_v = [out[n][3] for n in WEIGHTS]
    return (loss, dx.reshape(1, S, D), *grads, *deltas, *new_m, *new_v)
```

```python
import functools
import math

import numpy as np
import jax
import jax.numpy as jnp
from jax import lax
from jax.experimental import pallas as pl
from jax.experimental.pallas import tpu as pltpu

F32 = jnp.float32
BF16 = jnp.bfloat16

CHUNK = 64
NORM_EPS = 1e-6
ROPE_THETA = 10000.0
FOX_HEADS, FOX_DH = 6, 128
FOX_W = FOX_HEADS * FOX_DH
MLA_HEADS, MLA_NOPE, MLA_ROPE, MLA_V = 6, 128, 64, 128
MLA_Q_LORA, MLA_KV_LORA = 512, 256
MLA_W = MLA_HEADS * MLA_V
RET_HEADS, RET_DK, RET_DV = 4, 128, 256
RET_QK_W, RET_V_W = RET_HEADS * RET_DK, RET_HEADS * RET_DV
ADAM_LR, ADAM_B1, ADAM_B2, ADAM_EPS, ADAM_WD, ADAM_STEP = 0.001, 0.9, 0.999, 1e-08, 0.01, 10

N_DEV = 8
LANES = 128
V7X_VMEM_LIMIT_BYTES = 52 * 1024 * 1024
NEG_BIG = -1e30
HIGHEST = lax.Precision.HIGHEST

NT_DIMS = (((1,), (1,)), ((), ()))
TN_DIMS = (((0,), (0,)), ((), ()))
NN_DIMS = (((1,), (0,)), ((), ()))


def _pick(n, cap, mult=LANES):
    best = None
    for t in range(mult, min(n, cap) + 1, mult):
        if n % t == 0:
            best = t
    return n if best is None else best


def _cparams(*sem):
    return pltpu.CompilerParams(dimension_semantics=sem, vmem_limit_bytes=V7X_VMEM_LIMIT_BYTES)


class InLayout:
    def __init__(self, d_model):
        d = d_model
        self.d = d
        orig = dict(fq=(0, FOX_W), fk=(FOX_W, FOX_W), fv=(2 * FOX_W, FOX_W), ff=(3 * FOX_W, FOX_HEADS))
        o = 3 * FOX_W + FOX_HEADS
        for name, w in (("mq", MLA_Q_LORA), ("mkv", MLA_KV_LORA), ("mkr", MLA_ROPE), ("rq", RET_QK_W),
                        ("rk", RET_QK_W), ("rv", RET_V_W), ("rg", RET_V_W), ("gates", 3 * d)):
            orig[name] = (o, w)
            o += w
        self.orig = orig
        self.orig_width = o
        order = ["gates", "rv", "rg", "mq", "rq", "rk", "mkv", "fq", "fk", "fv", "mkr", "ff"]
        self.order = order
        self.off, self.width = {}, {}
        p = 0
        for name in order:
            w = orig[name][1]
            wp = -(-w // LANES) * LANES
            self.off[name], self.width[name] = p, wp
            p += wp
        self.total = p

    def cb(self, name, block):
        assert self.off[name] % block == 0, (name, block)
        return self.off[name] // block

    def permute(self, w):
        parts = []
        for name in self.order:
            o, n = self.orig[name]
            seg = w[..., o:o + n]
            pad = self.width[name] - n
            if pad:
                seg = jnp.pad(seg, [(0, 0)] * (w.ndim - 1) + [(0, pad)])
            parts.append(seg)
        return jnp.concatenate(parts, axis=-1)

    def unpermute(self, w):
        names = sorted(self.orig, key=lambda n: self.orig[n][0])
        return jnp.concatenate([w[..., self.off[n]:self.off[n] + self.orig[n][1]] for n in names], axis=-1)


def _mm(name, a, b, out_shape, grid, a_spec, b_spec, o_spec, dims, acc_shape, res=None, nsub=0):
    nk = grid[-1]
    has_res = res is not None

    def body(*refs):
        if has_res:
            a_ref, b_ref, r_ref, o_ref = refs[:4]
        else:
            a_ref, b_ref, o_ref = refs[:3]
            r_ref = None
        if nsub:
            prod = lax.dot_general(a_ref[0].astype(BF16), b_ref[0].astype(BF16), dims, preferred_element_type=F32)
            for q in range(1, nsub):
                prod = prod + lax.dot_general(a_ref[q].astype(BF16), b_ref[q].astype(BF16), dims,
                                              preferred_element_type=F32)
        else:
            prod = lax.dot_general(a_ref[...].astype(BF16), b_ref[...].astype(BF16), dims,
                                   preferred_element_type=F32)
        if nk == 1:
            if has_res:
                prod = prod + r_ref[...].astype(F32)
            o_ref[...] = prod.astype(o_ref.dtype)
        else:
            acc_ref = refs[-1]
            k = pl.program_id(len(grid) - 1)

            @pl.when(k == 0)
            def _():
                acc_ref[...] = prod

            @pl.when(k > 0)
            def _():
                acc_ref[...] += prod

            @pl.when(k == nk - 1)
            def _():
                r = acc_ref[...]
                if has_res:
                    r = r + r_ref[...].astype(F32)
                o_ref[...] = r.astype(o_ref.dtype)

    in_specs = [a_spec, b_spec] + ([o_spec] if has_res else [])
    args = (a, b) + ((res,) if has_res else ())
    scratch = [pltpu.VMEM(acc_shape, F32)] if nk > 1 else []
    sem = ("parallel",) * (len(grid) - 1) + ("arbitrary",)
    return pl.pallas_call(body, out_shape=out_shape, grid=grid, in_specs=in_specs, out_specs=o_spec,
                          scratch_shapes=scratch, name=name, compiler_params=_cparams(*sem))(*args)


def mm_nn(name, a, b, out_dtype, b_lead=None, res=None):
    M, K = a.shape
    N = b.shape[-1]
    tm, tn, tk = _pick(M, 1024, 8), _pick(N, 1024), _pick(K, 2048)
    grid = (M // tm, N // tn, K // tk)
    a_spec = pl.BlockSpec((tm, tk), lambda i, j, k: (i, k))
    if b_lead is None:
        b_spec = pl.BlockSpec((tk, tn), lambda i, j, k: (k, j))
    else:
        b_spec = pl.BlockSpec((None, tk, tn), lambda i, j, k: (b_lead, k, j))
    o_spec = pl.BlockSpec((tm, tn), lambda i, j, k: (i, j))
    return _mm(name, a, b, jax.ShapeDtypeStruct((M, N), out_dtype), grid, a_spec, b_spec, o_spec,
               NN_DIMS, (tm, tn), res)


def mm_nt(name, a, b, out_dtype, b_lead=None, res=None):
    M, N = a.shape
    K = b.shape[-2]
    tm, tko, tk = _pick(M, 1024, 8), _pick(K, 1024), _pick(N, 2048)
    grid = (M // tm, K // tko, N // tk)
    a_spec = pl.BlockSpec((tm, tk), lambda i, j, k: (i, k))
    if b_lead is None:
        b_spec = pl.BlockSpec((tko, tk), lambda i, j, k: (j, k))
    else:
        b_spec = pl.BlockSpec((None, tko, tk), lambda i, j, k: (b_lead, j, k))
    o_spec = pl.BlockSpec((tm, tko), lambda i, j, k: (i, j))
    return _mm(name, a, b, jax.ShapeDtypeStruct((M, K), out_dtype), grid, a_spec, b_spec, o_spec,
               NT_DIMS, (tm, tko), res)


def mm_tn(name, a, b, out_dtype):
    M, K = a.shape
    N = b.shape[-1]
    cap = 4096 if (a.dtype == BF16 and b.dtype == BF16) else 2048
    tko, tn, tk = _pick(K, 1024), _pick(N, 1024), _pick(M, cap, 8)
    grid = (K // tko, N // tn, M // tk)
    a_spec = pl.BlockSpec((tk, tko), lambda i, j, k: (k, i))
    b_spec = pl.BlockSpec((tk, tn), lambda i, j, k: (k, j))
    o_spec = pl.BlockSpec((tko, tn), lambda i, j, k: (i, j))
    return _mm(name, a, b, jax.ShapeDtypeStruct((K, N), out_dtype), grid, a_spec, b_spec, o_spec,
               TN_DIMS, (tko, tn))


FFN_SUB = 4
def ffn_up(name, h, w, l, out_dtype):
    M, D = h.shape
    f = w.shape[-1]
    tm = _pick(M, 1024, 8)
    grid = (M // tm, N_DEV, 1)
    return _mm(name, h, w, jax.ShapeDtypeStruct((N_DEV, M, f), out_dtype), grid,
               pl.BlockSpec((tm, D), lambda i, j, k: (i, 0)),
               pl.BlockSpec((None, None, D, f), lambda i, j, k: (l, j, 0, 0)),
               pl.BlockSpec((None, tm, f), lambda i, j, k: (j, i, 0)), NN_DIMS, (tm, f))


def ffn_down(name, act, w, l, res, out_dtype):
    _, M, f = act.shape
    D = w.shape[-1]
    tm, tn = _pick(M, 1024, 8), _pick(D, 1024)
    grid = (M // tm, D // tn, N_DEV // FFN_SUB)
    return _mm(name, act, w, jax.ShapeDtypeStruct((M, D), out_dtype), grid,
               pl.BlockSpec((FFN_SUB, tm, f), lambda i, j, k: (k, i, 0)),
               pl.BlockSpec((None, FFN_SUB, f, tn), lambda i, j, k: (l, k, 0, j)),
               pl.BlockSpec((tm, tn), lambda i, j, k: (i, j)), NN_DIMS, (tm, tn), res, nsub=FFN_SUB)


def ffn_down_bwd_act(name, dy, w, l, out_dtype):
    M, D = dy.shape
    f = w.shape[-2]
    tm = _pick(M, 1024, 8)
    grid = (M // tm, N_DEV, 1)
    return _mm(name, dy, w, jax.ShapeDtypeStruct((N_DEV, M, f), out_dtype), grid,
               pl.BlockSpec((tm, D), lambda i, j, k: (i, 0)),
               pl.BlockSpec((None, None, f, D), lambda i, j, k: (l, j, 0, 0)),
               pl.BlockSpec((None, tm, f), lambda i, j, k: (j, i, 0)), NT_DIMS, (tm, f))


def ffn_down_bwd_w(name, act, dy, out_dtype):
    _, M, f = act.shape
    D = dy.shape[-1]
    tn, tk = _pick(D, 1024), _pick(M, 4096 if dy.dtype == BF16 else 2048, 8)
    grid = (N_DEV, D // tn, M // tk)
    return _mm(name, act, dy, jax.ShapeDtypeStruct((N_DEV, f, D), out_dtype), grid,
               pl.BlockSpec((None, tk, f), lambda j, n, k: (j, k, 0)),
               pl.BlockSpec((tk, tn), lambda j, n, k: (k, n)),
               pl.BlockSpec((None, f, tn), lambda j, n, k: (j, 0, n)), TN_DIMS, (f, tn))


def ffn_up_bwd_h(name, du, w, l, res, out_dtype):
    _, M, f = du.shape
    D = w.shape[-2]
    tm, tn = _pick(M, 1024, 8), _pick(D, 1024)
    grid = (M // tm, D // tn, N_DEV // FFN_SUB)
    return _mm(name, du, w, jax.ShapeDtypeStruct((M, D), out_dtype), grid,
               pl.BlockSpec((FFN_SUB, tm, f), lambda i, j, k: (k, i, 0)),
               pl.BlockSpec((None, FFN_SUB, tn, f), lambda i, j, k: (l, k, j, 0)),
               pl.BlockSpec((tm, tn), lambda i, j, k: (i, j)), NT_DIMS, (tm, tn), res, nsub=FFN_SUB)


def ffn_up_bwd_w(name, h, du, out_dtype):
    M, D = h.shape
    f = du.shape[-1]
    tko, tk = _pick(D, 1024), _pick(M, 4096, 8)
    grid = (N_DEV, D // tko, M // tk)
    return _mm(name, h, du, jax.ShapeDtypeStruct((N_DEV, D, f), out_dtype), grid,
               pl.BlockSpec((tk, tko), lambda j, n, k: (k, n)),
               pl.BlockSpec((None, tk, f), lambda j, n, k: (j, k, 0)),
               pl.BlockSpec((None, tko, f), lambda j, n, k: (j, n, 0)), TN_DIMS, (tko, f))


def _rms(xf, g):
    return xf * lax.rsqrt(jnp.mean(xf * xf, axis=-1, keepdims=True) + NORM_EPS) * g


def rms_fwd(name, x, cb, W, g, out_dtype):
    S = x.shape[0]
    tr = _pick(S, 256, 8)

    def body(x_ref, g_ref, o_ref):
        o_ref[...] = _rms(x_ref[...].astype(F32), g_ref[...]).astype(o_ref.dtype)

    return pl.pallas_call(
        body, out_shape=jax.ShapeDtypeStruct((S, W), out_dtype), grid=(S // tr,),
        in_specs=[pl.BlockSpec((tr, W), lambda i: (i, cb)), pl.BlockSpec((1, W), lambda i: (0, 0))],
        out_specs=pl.BlockSpec((tr, W), lambda i: (i, 0)), name=name, compiler_params=_cparams("parallel"))(x, g)


def rms_bwd(name, x, cb, W, g, dy, out_dtype, res=None):
    S = x.shape[0]
    tr = _pick(S, 256, 8)
    has_res = res is not None

    def body(*refs):
        if has_res:
            x_ref, g_ref, dy_ref, r_ref, dx_ref, dg_ref = refs
        else:
            x_ref, g_ref, dy_ref, dx_ref, dg_ref = refs
        _, vjp = jax.vjp(_rms, x_ref[...].astype(F32), g_ref[...])
        dx, dg = vjp(dy_ref[...].astype(F32))
        if has_res:
            dx = dx + r_ref[...]
        dx_ref[...] = dx.astype(dx_ref.dtype)

        @pl.when(pl.program_id(0) == 0)
        def _():
            dg_ref[...] = jnp.zeros_like(dg_ref)

        dg_ref[...] += dg

    row = pl.BlockSpec((tr, W), lambda i: (i, 0))
    vec = pl.BlockSpec((1, W), lambda i: (0, 0))
    in_specs = [pl.BlockSpec((tr, W), lambda i: (i, cb)), vec, row] + ([row] if has_res else [])
    args = (x, g, dy) + ((res,) if has_res else ())
    return pl.pallas_call(
        body, out_shape=(jax.ShapeDtypeStruct((S, W), out_dtype), jax.ShapeDtypeStruct((1, W), F32)),
        grid=(S // tr,), in_specs=in_specs, out_specs=(row, vec), name=name,
        compiler_params=_cparams("arbitrary"))(*args)


def rope_tables(S, d):
    pos = jnp.arange(S, dtype=F32)
    inv_freq = ROPE_THETA ** (-jnp.arange(0, d, 2, dtype=F32) / d)
    ang = pos[:, None] * inv_freq[None, :]
    cos, sin = jnp.cos(ang), jnp.sin(ang)
    half = d // 2
    z = jnp.zeros((S, LANES - d), F32)
    zh = jnp.zeros((S, half), F32)
    c = jnp.concatenate([cos, cos, z], axis=1)
    sa = jnp.concatenate([-sin, zh, z], axis=1)
    sb = jnp.concatenate([zh, sin, z], axis=1)
    return c, sa, sb, half


def rope_apply(name, x, cb, H, tabs, scale, out_dtype, transpose=False):
    c, sa, sb, half = tabs
    S = x.shape[0]
    tr = _pick(S, 512, 8)
    up, down = LANES - half, half

    def body(x_ref, c_ref, sa_ref, sb_ref, o_ref):
        xv = x_ref[...].astype(F32)
        if not transpose:
            y = xv * c_ref[...] + pltpu.roll(xv, up, 1) * sa_ref[...] + pltpu.roll(xv, down, 1) * sb_ref[...]
            y = y * scale
        else:
            xv = xv * scale
            y = (xv * c_ref[...] + pltpu.roll(xv * sa_ref[...], down, 1)
                 + pltpu.roll(xv * sb_ref[...], up, 1))
        o_ref[...] = y.astype(o_ref.dtype)

    tab = pl.BlockSpec((tr, LANES), lambda h, i: (i, 0))
    return pl.pallas_call(
        body, out_shape=jax.ShapeDtypeStruct((S, H * LANES), out_dtype), grid=(H, S // tr),
        in_specs=[pl.BlockSpec((tr, LANES), lambda h, i: (i, cb + h)), tab, tab, tab],
        out_specs=pl.BlockSpec((tr, LANES), lambda h, i: (i, h)), name=name,
        compiler_params=_cparams("parallel", "parallel"))(x, c, sa, sb)


def _ret_out(o, g):
    y = o * lax.rsqrt(jnp.mean(o * o, axis=-1, keepdims=True) + NORM_EPS)
    return y * jax.nn.silu(g)


def ret_out_fwd(name, o, gsrc, g_cb, out_dtype):
    S = o.shape[0]
    tr = _pick(S, 512, 8)
    W = RET_DV

    def body(o_ref, g_ref, y_ref):
        y_ref[...] = _ret_out(o_ref[...], g_ref[...].astype(F32)).astype(y_ref.dtype)

    blk = pl.BlockSpec((tr, W), lambda h, i: (i, h))
    return pl.pallas_call(
        body, out_shape=jax.ShapeDtypeStruct((S, RET_HEADS * W), out_dtype), grid=(RET_HEADS, S // tr),
        in_specs=[blk, pl.BlockSpec((tr, W), lambda h, i: (i, g_cb + h))], out_specs=blk, name=name,
        compiler_params=_cparams("parallel", "parallel"))(o, gsrc)


def ret_out_bwd(name, o, gsrc, g_cb, dy):
    S = o.shape[0]
    tr = _pick(S, 512, 8)
    W = RET_DV

    def body(o_ref, g_ref, dy_ref, do_ref, dg_ref):
        _, vjp = jax.vjp(_ret_out, o_ref[...], g_ref[...].astype(F32))
        do, dg = vjp(dy_ref[...].astype(F32))
        do_ref[...] = do.astype(do_ref.dtype)
        dg_ref[...] = dg.astype(dg_ref.dtype)

    blk = pl.BlockSpec((tr, W), lambda h, i: (i, h))
    return pl.pallas_call(
        body, out_shape=(jax.ShapeDtypeStruct((S, RET_HEADS * W), F32),
                         jax.ShapeDtypeStruct((S, RET_HEADS * W), BF16)),
        grid=(RET_HEADS, S // tr),
        in_specs=[blk, pl.BlockSpec((tr, W), lambda h, i: (i, g_cb + h)), blk], out_specs=(blk, blk),
        name=name, compiler_params=_cparams("parallel", "parallel"))(o, gsrc, dy)


def _merge(g0, g1, g2, a, b, c):
    return jax.nn.sigmoid(g0) * a + jax.nn.sigmoid(g1) * b + jax.nn.sigmoid(g2) * c


def merge_fwd(name, P, gates_cb, a, b, c, out_dtype):
    S, D = a.shape
    tr = _pick(S, 128, 8)

    def body(g0, g1, g2, a_ref, b_ref, c_ref, o_ref):
        o_ref[...] = _merge(g0[...], g1[...], g2[...], a_ref[...], b_ref[...], c_ref[...]).astype(o_ref.dtype)

    row = pl.BlockSpec((tr, D), lambda i: (i, 0))
    gs = [pl.BlockSpec((tr, D), lambda i, k=k: (i, gates_cb + k)) for k in range(3)]
    return pl.pallas_call(
        body, out_shape=jax.ShapeDtypeStruct((S, D), out_dtype), grid=(S // tr,),
        in_specs=gs + [row, row, row], out_specs=row, name=name,
        compiler_params=_cparams("parallel"))(P, P, P, a, b, c)


def merge_bwd(name, P, gates_cb, a, b, c, dm):
    S, D = a.shape
    tr = _pick(S, 128, 8)

    def body(g0, g1, g2, a_ref, b_ref, c_ref, dm_ref, dg_ref, da_ref, db_ref, dc_ref):
        _, vjp = jax.vjp(_merge, g0[...], g1[...], g2[...], a_ref[...], b_ref[...], c_ref[...])
        d0, d1, d2, da, db, dc = vjp(dm_ref[...].astype(F32))
        dg_ref[:, 0:D] = d0.astype(dg_ref.dtype)
        dg_ref[:, D:2 * D] = d1.astype(dg_ref.dtype)
        dg_ref[:, 2 * D:3 * D] = d2.astype(dg_ref.dtype)
        da_ref[...] = da.astype(da_ref.dtype)
        db_ref[...] = db.astype(db_ref.dtype)
        dc_ref[...] = dc.astype(dc_ref.dtype)

    row = pl.BlockSpec((tr, D), lambda i: (i, 0))
    gs = [pl.BlockSpec((tr, D), lambda i, k=k: (i, gates_cb + k)) for k in range(3)]
    bf = jax.ShapeDtypeStruct((S, D), BF16)
    return pl.pallas_call(
        body, out_shape=(jax.ShapeDtypeStruct((S, 3 * D), BF16), bf, bf, bf), grid=(S // tr,),
        in_specs=gs + [row, row, row, row],
        out_specs=(pl.BlockSpec((tr, 3 * D), lambda i: (i, 0)), row, row, row), name=name,
        compiler_params=_cparams("parallel"))(P, P, P, a, b, c, dm)


HALO = 8


CONV_CHUNK = 32


def _shifted_back(u_ref, uh_ref, ext_ref, s1_ref, s2_ref, tr):
    ext_ref[0:HALO, :] = jnp.where(pl.program_id(1) > 0, uh_ref[...], 0.0)
    ext_ref[HALO:HALO + tr, :] = u_ref[...]
    s1_ref[...] = ext_ref[HALO - 1:HALO - 1 + tr, :]
    s2_ref[...] = ext_ref[HALO - 2:HALO - 2 + tr, :]


def _conv3(cw, cb, u, u1, u2):
    return cb + ((cw[0:1, :] * u2 + cw[1:2, :] * u1) + cw[2:3, :] * u)


def _chunks(tr, fn):
    def step(c, carry):
        return fn(pl.ds(pl.multiple_of(c * CONV_CHUNK, CONV_CHUNK), CONV_CHUNK), carry)
    return step


def _ffn_specs(S, f, tr, l):
    nb = tr // HALO
    row = pl.BlockSpec((None, tr, f), lambda j, i: (j, i, 0))
    prev = pl.BlockSpec((None, HALO, f), lambda j, i: (j, jnp.maximum(i * nb - 1, 0), 0))
    nxt = pl.BlockSpec((None, HALO, f), lambda j, i: (j, jnp.minimum((i + 1) * nb, S // HALO - 1), 0))
    cw = pl.BlockSpec((None, None, 3, f), lambda j, i: (l, j, 0, 0))
    cb = pl.BlockSpec((None, None, 1, f), lambda j, i: (l, j, 0, 0))
    return row, prev, nxt, cw, cb


def ffn_act_fwd(name, u, gt, cw, cb, l, out_dtype):
    _, S, f = u.shape
    tr = _pick(S, 512, 8)
    row, prev, _, cws, cbs = _ffn_specs(S, f, tr, l)

    def body(u_ref, uh_ref, gt_ref, cw_ref, cb_ref, o_ref, ext_ref, s1_ref, s2_ref):
        _shifted_back(u_ref, uh_ref, ext_ref, s1_ref, s2_ref, tr)
        cwv, cbv = cw_ref[...], cb_ref[...]

        def chunk(rows, carry):
            uc = _conv3(cwv, cbv, u_ref[rows, :], s1_ref[rows, :], s2_ref[rows, :])
            o_ref[rows, :] = (jax.nn.gelu(uc) * gt_ref[rows, :]).astype(o_ref.dtype)
            return carry

        lax.fori_loop(0, tr // CONV_CHUNK, _chunks(tr, chunk), 0)

    return pl.pallas_call(
        body, out_shape=jax.ShapeDtypeStruct((N_DEV, S, f), out_dtype), grid=(N_DEV, S // tr),
        in_specs=[row, prev, row, cws, cbs], out_specs=row,
        scratch_shapes=[pltpu.VMEM((tr + HALO, f), F32), pltpu.VMEM((tr, f), F32), pltpu.VMEM((tr, f), F32)],
        name=name, compiler_params=_cparams("parallel", "parallel"))(u, u, gt, cw, cb)


def ffn_act_bwd_point(name, u, gt, cw, cb, l, dact):
    _, S, f = u.shape
    tr = _pick(S, 512, 8)
    row, prev, _, cws, cbs = _ffn_specs(S, f, tr, l)

    def body(u_ref, uh_ref, gt_ref, cw_ref, cb_ref, da_ref, g_ref, dgt_ref, ext_ref, s1_ref, s2_ref):
        _shifted_back(u_ref, uh_ref, ext_ref, s1_ref, s2_ref, tr)
        cwv, cbv = cw_ref[...], cb_ref[...]

        def chunk(rows, carry):
            uc = _conv3(cwv, cbv, u_ref[rows, :], s1_ref[rows, :], s2_ref[rows, :])
            _, vjp = jax.vjp(lambda c, t: jax.nn.gelu(c) * t, uc, gt_ref[rows, :])
            g, dgt = vjp(da_ref[rows, :].astype(F32))
            g_ref[rows, :] = g
            dgt_ref[rows, :] = dgt.astype(dgt_ref.dtype)
            return carry

        lax.fori_loop(0, tr // CONV_CHUNK, _chunks(tr, chunk), 0)

    return pl.pallas_call(
        body, out_shape=(jax.ShapeDtypeStruct((N_DEV, S, f), F32), jax.ShapeDtypeStruct((N_DEV, S, f), BF16)),
        grid=(N_DEV, S // tr), in_specs=[row, prev, row, cws, cbs, row], out_specs=(row, row),
        scratch_shapes=[pltpu.VMEM((tr + HALO, f), F32), pltpu.VMEM((tr, f), F32), pltpu.VMEM((tr, f), F32)],
        name=name, compiler_params=_cparams("parallel", "parallel"))(u, u, gt, cw, cb, dact)


def ffn_act_bwd_conv(name, u, g, cw, l):
    _, S, f = u.shape
    tr = _pick(S, 512, 8)
    nt = S // tr
    row, prev, nxt, cws, _ = _ffn_specs(S, f, tr, l)

    def body(u_ref, uh_ref, g_ref, gn_ref, cw_ref, du_ref, dcw_ref, dcb_ref, ext_ref, s1_ref, s2_ref, n1_ref, n2_ref):
        i = pl.program_id(1)
        _shifted_back(u_ref, uh_ref, ext_ref, s1_ref, s2_ref, tr)
        ext_ref[0:tr, :] = g_ref[...]
        ext_ref[tr:tr + HALO, :] = jnp.where(i < nt - 1, gn_ref[...], 0.0)
        n1_ref[...] = ext_ref[1:1 + tr, :]
        n2_ref[...] = ext_ref[2:2 + tr, :]
        cw = cw_ref[...]

        def chunk(rows, carry):
            d0, d1, d2, db = carry
            g = g_ref[rows, :]
            du_ref[rows, :] = (cw[2:3, :] * g + cw[1:2, :] * n1_ref[rows, :]
                               + cw[0:1, :] * n2_ref[rows, :]).astype(du_ref.dtype)
            return (d0 + jnp.sum(g * s2_ref[rows, :], axis=0, keepdims=True),
                    d1 + jnp.sum(g * s1_ref[rows, :], axis=0, keepdims=True),
                    d2 + jnp.sum(g * u_ref[rows, :], axis=0, keepdims=True),
                    db + jnp.sum(g, axis=0, keepdims=True))

        z = jnp.zeros((1, f), F32)
        d0, d1, d2, db = lax.fori_loop(0, tr // CONV_CHUNK, _chunks(tr, chunk), (z, z, z, z))

        @pl.when(i == 0)
        def _():
            dcw_ref[...] = jnp.zeros_like(dcw_ref)
            dcb_ref[...] = jnp.zeros_like(dcb_ref)

        dcw_ref[0:1, :] += d0
        dcw_ref[1:2, :] += d1
        dcw_ref[2:3, :] += d2
        dcb_ref[...] += db

    tile = pltpu.VMEM((tr, f), F32)
    return pl.pallas_call(
        body, out_shape=(jax.ShapeDtypeStruct((N_DEV, S, f), BF16), jax.ShapeDtypeStruct((N_DEV, 3, f), F32),
                         jax.ShapeDtypeStruct((N_DEV, 1, f), F32)),
        grid=(N_DEV, nt), in_specs=[row, prev, row, nxt, cws],
        out_specs=(row, pl.BlockSpec((None, 3, f), lambda j, i: (j, 0, 0)),
                   pl.BlockSpec((None, 1, f), lambda j, i: (j, 0, 0))),
        scratch_shapes=[pltpu.VMEM((tr + HALO, f), F32), tile, tile, tile, tile], name=name,
        compiler_params=_cparams("parallel", "arbitrary"))(u, u, g, g, cw)


def loss_head(name, x, g, tgt):
    S, D = x.shape
    tr = _pick(S, 256, 8)

    def body(x_ref, g_ref, t_ref, l_ref, dx_ref, dg_ref):
        tg = t_ref[...]

        def f(xv, gv):
            err = jnp.square(_rms(xv, gv) - tg)
            return 0.5 * jnp.sum(jnp.mean(err, axis=-1))

        val, vjp = jax.vjp(f, x_ref[...], g_ref[...])
        dx, dg = vjp(jnp.ones((), F32))
        dx_ref[...] = dx

        @pl.when(pl.program_id(0) == 0)
        def _():
            l_ref[...] = jnp.zeros_like(l_ref)
            dg_ref[...] = jnp.zeros_like(dg_ref)

        l_ref[...] += val
        dg_ref[...] += dg

    row = pl.BlockSpec((tr, D), lambda i: (i, 0))
    vec = pl.BlockSpec((1, D), lambda i: (0, 0))
    lt = pl.BlockSpec((8, LANES), lambda i: (0, 0))
    return pl.pallas_call(
        body, out_shape=(jax.ShapeDtypeStruct((8, LANES), F32), jax.ShapeDtypeStruct((S, D), F32),
                         jax.ShapeDtypeStruct((1, D), F32)),
        grid=(S // tr,), in_specs=[row, vec, row], out_specs=(lt, row, vec), name=name,
        compiler_params=_cparams("arbitrary"))(x, g, tgt)


def _tri(n, fn):
    r = lax.broadcasted_iota(jnp.int32, (n, n), 0)
    c = lax.broadcasted_iota(jnp.int32, (n, n), 1)
    return jnp.where(fn(r, c), 1.0, 0.0).astype(F32)


def _log_sigmoid(z):
    return jnp.minimum(z, 0.0) - jnp.log1p(jnp.exp(-jnp.abs(z)))


def fox_gate_fwd(name, ft, b):
    H, R, _ = ft.shape

    def body(f_ref, b_ref, o_ref):
        ls = _log_sigmoid(f_ref[...] + b_ref[...])
        cum = jnp.dot(ls, _tri(LANES, lambda r, c: r <= c), precision=HIGHEST, preferred_element_type=F32)
        tot = jnp.broadcast_to(cum[:, LANES - 1:LANES], (R, LANES))
        off = jnp.dot(_tri(R, lambda r, c: r > c), tot, precision=HIGHEST, preferred_element_type=F32)
        o_ref[...] = -(cum + off)

    blk = pl.BlockSpec((None, R, LANES), lambda h: (h, 0, 0))
    return pl.pallas_call(
        body, out_shape=jax.ShapeDtypeStruct((H, R, LANES), F32), grid=(H,),
        in_specs=[blk, pl.BlockSpec((None, 1, LANES), lambda h: (h, 0, 0))], out_specs=blk, name=name,
        compiler_params=_cparams("parallel"))(ft, b)


def fox_gate_bwd(name, ft, b, dkb):
    H, R, _ = ft.shape

    def body(f_ref, b_ref, d_ref, df_ref, db_ref):
        z = f_ref[...] + b_ref[...]
        d = d_ref[...]
        rev = jnp.dot(d, _tri(LANES, lambda r, c: r >= c), precision=HIGHEST, preferred_element_type=F32)
        tot = jnp.broadcast_to(rev[:, 0:1], (R, LANES))
        off = jnp.dot(_tri(R, lambda r, c: r < c), tot, precision=HIGHEST, preferred_element_type=F32)
        dls = -(rev + off)
        dz = dls * jax.nn.sigmoid(-z)
        df_ref[...] = dz
        s = jnp.sum(jnp.sum(dz, axis=1, keepdims=True), axis=0, keepdims=True)
        db_ref[...] = jnp.broadcast_to(s, (1, LANES))

    blk = pl.BlockSpec((None, R, LANES), lambda h: (h, 0, 0))
    vec = pl.BlockSpec((None, 1, LANES), lambda h: (h, 0, 0))
    return pl.pallas_call(
        body, out_shape=(jax.ShapeDtypeStruct((H, R, LANES), F32), jax.ShapeDtypeStruct((H, 1, LANES), F32)),
        grid=(H,), in_specs=[blk, vec, blk], out_specs=(blk, vec), name=name,
        compiler_params=_cparams("parallel"))(ft, b, dkb)


def _ret_log_gamma(h):
    lg = [float(np.log(np.float32(1.0) - np.float32(2.0) ** np.float32(-5.0 - i))) for i in range(RET_HEADS)]
    out = jnp.float32(lg[RET_HEADS - 1])
    for i in range(RET_HEADS - 2, -1, -1):
        out = jnp.where(h == i, jnp.float32(lg[i]), out)
    return out


def _visible(mode, B):
    r = lax.broadcasted_iota(jnp.int32, (B, B), 0)
    c = lax.broadcasted_iota(jnp.int32, (B, B), 1)
    if mode == "fox":
        return c <= r
    return (c // CHUNK) <= (r // CHUNK)


def _decay(lg, B, blocks_apart):
    r = lax.broadcasted_iota(jnp.int32, (B, B), 0)
    c = lax.broadcasted_iota(jnp.int32, (B, B), 1)
    dist = jnp.abs(r - c + blocks_apart * B).astype(F32)
    return jnp.exp(lg * dist)


def _attn_block(S):
    return 512 if S >= 2048 else 128


def attn_fwd(name, mode, q1, q1_cb, k1, k1_cb, v, v_cb, H, dv, scale, q2=None, q2_cb=0, k2=None, kbias=None):
    S = q1.shape[0]
    B = _attn_block(S)
    nq = S // B
    softmax = mode != "ret"
    two = mode == "mla"
    has_bias = mode == "fox"

    def body(*refs):
        it = iter(refs)
        q1_ref, k1_ref, v_ref = next(it), next(it), next(it)
        q2_ref = next(it) if two else None
        k2_ref = next(it) if two else None
        kb_ref = next(it) if has_bias else None
        o_ref = next(it)
        lse_ref = next(it) if softmax else None
        kbuf, vbuf = next(it), next(it)
        k2buf = next(it) if two else None
        acc = next(it)
        m_ref = next(it) if softmax else None
        l_ref = next(it) if softmax else None
        s_all = next(it) if softmax else None
        h = pl.program_id(0)
        i = pl.program_id(1)

        @pl.when(i == 0)
        def _():
            kbuf[...] = k1_ref[...].astype(BF16)
            vbuf[...] = v_ref[...].astype(BF16)
            if two:
                k2buf[...] = k2_ref[...].astype(BF16)

        qb = q1_ref[...].astype(BF16)
        q2b = q2_ref[...].astype(BF16) if two else None
        lg = _ret_log_gamma(h) if mode == "ret" else None
        acc[...] = jnp.zeros_like(acc)
        if softmax:
            m_ref[...] = jnp.full_like(m_ref, NEG_BIG)
            l_ref[...] = jnp.zeros_like(l_ref)

        def scores(g, diag):
            rows = slice(g * B, (g + 1) * B)
            s = lax.dot_general(qb, kbuf[rows, :], NT_DIMS, preferred_element_type=F32)
            if two:
                s = s + lax.dot_general(q2b, k2buf[rows, :], NT_DIMS, preferred_element_type=F32)
            if softmax:
                s = s * scale
                if has_bias:
                    s = s + kb_ref[g]
                if diag:
                    s = jnp.where(_visible(mode, B), s, NEG_BIG)
                s_all[:, rows] = s
                m_ref[...] = jnp.maximum(m_ref[...], jnp.max(s, axis=1, keepdims=True))
            else:
                if diag:
                    p = jnp.where(_visible(mode, B), s * _decay(lg, B, 0), 0.0)
                else:
                    p = s * _decay(lg, B, i - g)
                acc[...] += jnp.dot(p.astype(BF16), vbuf[rows, :], preferred_element_type=F32)

        def weighted(g):
            rows = slice(g * B, (g + 1) * B)
            p = jnp.exp(s_all[:, rows] - m_ref[...])
            l_ref[...] += jnp.sum(p, axis=1, keepdims=True)
            acc[...] += jnp.dot(p.astype(BF16), vbuf[rows, :], preferred_element_type=F32)

        for g in range(nq):
            pl.when(g < i)(functools.partial(scores, g, False))
            pl.when(g == i)(functools.partial(scores, g, True))
        if softmax:
            for g in range(nq):
                pl.when(g <= i)(functools.partial(weighted, g))
            o_ref[...] = acc[...] / l_ref[...]
            lse_ref[...] = jnp.broadcast_to(m_ref[...] + jnp.log(l_ref[...]), (B, LANES))
        else:
            o_ref[...] = acc[...]

    in_specs = [pl.BlockSpec((B, LANES), lambda h, i: (i, q1_cb + h)),
                pl.BlockSpec((S, LANES), lambda h, i: (0, k1_cb + h)),
                pl.BlockSpec((S, dv), lambda h, i: (0, v_cb + h))]
    args = [q1, k1, v]
    if two:
        in_specs += [pl.BlockSpec((B, LANES), lambda h, i: (i, q2_cb + h)),
                     pl.BlockSpec((S, LANES), lambda h, i: (0, 0))]
        args += [q2, k2]
    if has_bias:
        in_specs.append(pl.BlockSpec((None, nq, 1, B), lambda h, i: (h, 0, 0, 0)))
        args.append(kbias)
    out_shape = [jax.ShapeDtypeStruct((S, H * dv), F32)]
    out_specs = [pl.BlockSpec((B, dv), lambda h, i: (i, h))]
    if softmax:
        out_shape.append(jax.ShapeDtypeStruct((S, H * LANES), F32))
        out_specs.append(pl.BlockSpec((B, LANES), lambda h, i: (i, h)))
    scratch = [pltpu.VMEM((S, LANES), BF16), pltpu.VMEM((S, dv), BF16)]
    if two:
        scratch.append(pltpu.VMEM((S, LANES), BF16))
    scratch.append(pltpu.VMEM((B, dv), F32))
    if softmax:
        scratch += [pltpu.VMEM((B, 1), F32), pltpu.VMEM((B, 1), F32), pltpu.VMEM((B, S), F32)]
    res = pl.pallas_call(body, out_shape=tuple(out_shape), grid=(H, nq), in_specs=in_specs,
                         out_specs=tuple(out_specs), scratch_shapes=scratch, name=name,
                         compiler_params=_cparams("parallel", "arbitrary"))(*args)
    return res if softmax else (res[0], None)


def attn_bwd(name, mode, q1, q1_cb, k1, k1_cb, v, v_cb, H, dv, scale, do, o=None, lse=None,
             q2=None, q2_cb=0, k2=None, kbias=None):
    S = q1.shape[0]
    B = _attn_block(S)
    nb = S // B
    softmax = mode != "ret"
    two = mode == "mla"
    has_bias = mode == "fox"

    def body(*refs):
        it = iter(refs)
        q1_ref, k1_ref, v_ref, do_ref = next(it), next(it), next(it), next(it)
        o_ref = next(it) if softmax else None
        lse_ref = next(it) if softmax else None
        q2_ref = next(it) if two else None
        k2_ref = next(it) if two else None
        kb_ref = next(it) if has_bias else None
        dq1_ref, dk1_ref, dv_ref = next(it), next(it), next(it)
        dq2_ref = next(it) if two else None
        dk2_ref = next(it) if two else None
        dkb_ref = next(it) if has_bias else None
        drow_ref = next(it) if has_bias else None
        qbuf, dobuf = next(it), next(it)
        q2buf = next(it) if two else None
        qT, doT = next(it), next(it)
        q2T = next(it) if two else None
        delta = next(it) if softmax else None
        dk_acc, dv_acc = next(it), next(it)
        dk2_acc = next(it) if two else None
        dkb_acc = next(it) if has_bias else None
        h = pl.program_id(0)
        j = pl.program_id(1)

        @pl.when(j == 0)
        def _():
            qbuf[...] = q1_ref[...].astype(BF16)
            dobuf[...] = do_ref[...].astype(BF16)
            qT[...] = q1_ref[...].astype(F32).T.astype(BF16)
            doT[...] = do_ref[...].astype(F32).T.astype(BF16)
            dq1_ref[...] = jnp.zeros_like(dq1_ref)
            if has_bias:
                drow_ref[...] = jnp.zeros_like(drow_ref)
            if two:
                q2buf[...] = q2_ref[...].astype(BF16)
                q2T[...] = q2_ref[...].astype(F32).T.astype(BF16)
                dq2_ref[...] = jnp.zeros_like(dq2_ref)
            if softmax:
                def drow(t, carry):
                    rows = pl.ds(pl.multiple_of(t * B, B), B)
                    delta[rows, :] = jnp.sum(do_ref[rows, :].astype(F32) * o_ref[rows, :], axis=1, keepdims=True)
                    return carry
                lax.fori_loop(0, nb, drow, 0)

        kj = k1_ref[...].astype(BF16)
        vj = v_ref[...].astype(BF16)
        k2j = k2_ref[...].astype(BF16) if two else None
        kbj = kb_ref[...] if has_bias else None
        lg = _ret_log_gamma(h) if mode == "ret" else None
        dk_acc[...] = jnp.zeros_like(dk_acc)
        dv_acc[...] = jnp.zeros_like(dv_acc)
        if two:
            dk2_acc[...] = jnp.zeros_like(dk2_acc)
        if has_bias:
            dkb_acc[...] = jnp.zeros_like(dkb_acc)

        def step(i, diag):
            rows = slice(i * B, (i + 1) * B)
            qi = qbuf[rows, :]
            doi = dobuf[rows, :]
            s = lax.dot_general(qi, kj, NT_DIMS, preferred_element_type=F32)
            if two:
                q2i = q2buf[rows, :]
                s = s + lax.dot_general(q2i, k2j, NT_DIMS, preferred_element_type=F32)
            dp = lax.dot_general(doi, vj, NT_DIMS, preferred_element_type=F32)
            if softmax:
                s = s * scale
                if has_bias:
                    s = s + kbj
                if diag:
                    s = jnp.where(_visible(mode, B), s, NEG_BIG)
                p = jnp.exp(s - lse_ref[rows, 0:1])
                ds = p * (dp - delta[rows, :])
                if has_bias:
                    dkb_acc[...] += jnp.sum(ds, axis=0, keepdims=True)
                    drow_ref[rows, :] += jnp.broadcast_to(jnp.sum(ds, axis=1, keepdims=True), (B, LANES))
                dsb = (ds * scale).astype(BF16)
            else:
                if diag:
                    dec = jnp.where(_visible(mode, B), _decay(lg, B, 0), 0.0)
                else:
                    dec = _decay(lg, B, i - j)
                p = s * dec
                dsb = (dp * dec).astype(BF16)
            dv_acc[...] += jnp.dot(doT[:, rows], p.astype(BF16), preferred_element_type=F32)
            dk_acc[...] += jnp.dot(qT[:, rows], dsb, preferred_element_type=F32)
            dq1_ref[rows, :] += jnp.dot(dsb, kj, preferred_element_type=F32)
            if two:
                dk2_acc[...] += jnp.dot(q2T[:, rows], dsb, preferred_element_type=F32)
                dq2_ref[rows, :] += jnp.dot(dsb, k2j, preferred_element_type=F32)

        for i in range(nb):
            pl.when(i == j)(functools.partial(step, i, True))
            pl.when(i > j)(functools.partial(step, i, False))
        dk1_ref[...] = dk_acc[...].T
        dv_ref[...] = dv_acc[...].T
        if two:
            dk2_ref[...] = dk2_acc[...].T
        if has_bias:
            dkb_ref[...] = dkb_acc[...]

    full = lambda w, cb: pl.BlockSpec((S, w), lambda h, j: (0, cb + h))
    blk = lambda w, cb: pl.BlockSpec((B, w), lambda h, j: (j, cb + h))
    in_specs = [full(LANES, q1_cb), blk(LANES, k1_cb), blk(dv, v_cb), full(dv, 0)]
    args = [q1, k1, v, do]
    if softmax:
        in_specs += [full(dv, 0), full(LANES, 0)]
        args += [o, lse]
    if two:
        in_specs += [full(LANES, q2_cb), pl.BlockSpec((B, LANES), lambda h, j: (j, 0))]
        args += [q2, k2]
    if has_bias:
        in_specs.append(pl.BlockSpec((None, None, 1, B), lambda h, j: (h, j, 0, 0)))
        args.append(kbias)
    names = ["dq1", "dk1", "dv"]
    out_shape = [jax.ShapeDtypeStruct((S, H * LANES), F32), jax.ShapeDtypeStruct((S, H * LANES), F32),
                 jax.ShapeDtypeStruct((S, H * dv), F32)]
    out_specs = [full(LANES, 0), blk(LANES, 0), blk(dv, 0)]
    if two:
        names += ["dq2", "dk2h"]
        out_shape += [jax.ShapeDtypeStruct((S, H * LANES), F32)] * 2
        out_specs += [full(LANES, 0), blk(LANES, 0)]
    if has_bias:
        names.append("dkb")
        out_shape.append(jax.ShapeDtypeStruct((H, nb, 1, B), F32))
        out_specs.append(pl.BlockSpec((None, None, 1, B), lambda h, j: (h, j, 0, 0)))
        names.append("drow")
        out_shape.append(jax.ShapeDtypeStruct((S, H * LANES), F32))
        out_specs.append(full(LANES, 0))
    scratch = [pltpu.VMEM((S, LANES), BF16), pltpu.VMEM((S, dv), BF16)]
    if two:
        scratch.append(pltpu.VMEM((S, LANES), BF16))
    scratch += [pltpu.VMEM((LANES, S), BF16), pltpu.VMEM((dv, S), BF16)]
    if two:
        scratch.append(pltpu.VMEM((LANES, S), BF16))
    if softmax:
        scratch.append(pltpu.VMEM((S, 1), F32))
    scratch += [pltpu.VMEM((LANES, B), F32), pltpu.VMEM((dv, B), F32)]
    if two:
        scratch.append(pltpu.VMEM((LANES, B), F32))
    if has_bias:
        scratch.append(pltpu.VMEM((1, B), F32))
    res = pl.pallas_call(body, out_shape=tuple(out_shape), grid=(H, nb), in_specs=in_specs,
                         out_specs=tuple(out_specs), scratch_shapes=scratch, name=name,
                         compiler_params=_cparams("parallel", "arbitrary"))(*args)
    return dict(zip(names, res))


def head_sum(name, x, H, out_dtype):
    S = x.shape[0]
    tr = _pick(S, 512, 8)

    def body(x_ref, o_ref):
        acc = x_ref[:, 0:LANES]
        for h in range(1, H):
            acc = acc + x_ref[:, h * LANES:(h + 1) * LANES]
        o_ref[...] = acc.astype(o_ref.dtype)

    return pl.pallas_call(
        body, out_shape=jax.ShapeDtypeStruct((S, LANES), out_dtype), grid=(S // tr,),
        in_specs=[pl.BlockSpec((tr, H * LANES), lambda i: (i, 0))],
        out_specs=pl.BlockSpec((tr, LANES), lambda i: (i, 0)), name=name,
        compiler_params=_cparams("parallel"))(x)


def _mesh_pos():
    return lax.axis_index("x"), lax.axis_index("y"), lax.axis_index("c")


def _peer(pos, k):
    x, y, c = pos
    px = 1 - x if k & 4 else x
    py = 1 - y if k & 2 else y
    pc = 1 - c if k & 1 else c
    return (px, py, pc), 4 * px + 2 * py + pc


def exchange(name, tensors):
    nt = len(tensors)
    flat_in, counts = [], []
    out_shape = []
    for mode, srcs in tensors:
        counts.append(len(srcs))
        flat_in += list(srcs)
        rc = srcs[0].shape[-2:]
        out_shape.append(jax.ShapeDtypeStruct((len(srcs), N_DEV) + tuple(rc), srcs[0].dtype))
    n_in = len(flat_in)

    def body(*refs):
        ins = refs[:n_in]
        outs = refs[n_in:n_in + nt]
        send_sems, recv_sems, local_sems = refs[n_in + nt:]
        pos = _mesh_pos()
        me = 4 * pos[0] + 2 * pos[1] + pos[2]
        srcs_of, base = [], 0
        for t in range(nt):
            srcs_of.append(ins[base:base + counts[t]])
            base += counts[t]

        def src_view(t, l, slot):
            ref = srcs_of[t][l]
            return ref if tensors[t][0] == "gather" else ref.at[slot]

        def all_layers(t, slot):
            return outs[t].at[pl.ds(0, counts[t]), slot]

        for t in range(nt):
            for l in range(counts[t]):
                pltpu.make_async_copy(src_view(t, l, me), outs[t].at[l, me], local_sems.at[t]).start()
        for t in range(nt):
            for k in range(1, N_DEV):
                peer, pid = _peer(pos, k)
                for l in range(counts[t]):
                    pltpu.make_async_remote_copy(
                        src_ref=src_view(t, l, pid), dst_ref=outs[t].at[l, me],
                        send_sem=send_sems.at[t, k - 1], recv_sem=recv_sems.at[t, k - 1],
                        device_id=peer, device_id_type=pl.DeviceIdType.MESH).start()
        for t in range(nt):
            for k in range(1, N_DEV):
                peer, pid = _peer(pos, k)
                pltpu.make_async_remote_copy(
                    src_ref=all_layers(t, pid), dst_ref=all_layers(t, pid),
                    send_sem=send_sems.at[t, k - 1], recv_sem=recv_sems.at[t, k - 1],
                    device_id=peer, device_id_type=pl.DeviceIdType.MESH).wait()
        for t in range(nt):
            pltpu.make_async_copy(all_layers(t, me), all_layers(t, me), local_sems.at[t]).wait()

    any_spec = pl.BlockSpec(memory_space=pl.ANY)
    return pl.pallas_call(
        body, out_shape=tuple(out_shape), in_specs=[any_spec] * n_in, out_specs=tuple([any_spec] * nt),
        scratch_shapes=[pltpu.SemaphoreType.DMA((nt, N_DEV - 1)), pltpu.SemaphoreType.DMA((nt, N_DEV - 1)),
                        pltpu.SemaphoreType.DMA((nt,))],
        name=name)(*flat_in)


HBM_SPEC = pl.BlockSpec(memory_space=pltpu.HBM)
SEM_SPEC = pl.BlockSpec(memory_space=pltpu.SEMAPHORE)
DATAFLOW = pltpu.SideEffectType.DATAFLOW_SIDE_EFFECTING


def _hbm(a):
    return pltpu.with_memory_space_constraint(a, pltpu.HBM)


def landing_zones(mode, srcs):
    pos = _mesh_pos()
    me = 4 * pos[0] + 2 * pos[1] + pos[2]
    lands = []
    for s in srcs:
        R, C = s.shape[-2:]
        own = s[None] if mode == "gather" else lax.dynamic_slice(s, (me, 0, 0), (1, R, C))
        lands.append(lax.dynamic_update_slice(lax.empty((N_DEV, R, C), s.dtype), own, (me, 0, 0)))
    return lands


def exchange_start(name, mode, srcs, lands, after=None):
    n = len(srcs)
    extra = [] if after is None else [after]

    def body(*refs):
        src_refs, land_refs = refs[:n], refs[n:2 * n]
        send_sems, recv_sems = refs[2 * n + len(extra)], refs[2 * n + len(extra) + 1]
        token = refs[-1]
        pos = _mesh_pos()
        me = 4 * pos[0] + 2 * pos[1] + pos[2]
        for t in range(n):
            for k in range(1, N_DEV):
                peer, pid = _peer(pos, k)
                src = src_refs[t] if mode == "gather" else src_refs[t].at[pid]
                pltpu.make_async_remote_copy(
                    src_ref=src, dst_ref=land_refs[t].at[me], send_sem=send_sems.at[t], recv_sem=recv_sems.at[t],
                    device_id=peer, device_id_type=pl.DeviceIdType.MESH).start()
        token[...] = jnp.zeros_like(token)

    thru = [pltpu.HBM(a.shape, a.dtype) for a in list(srcs) + list(lands)]
    out_shape = (pltpu.SemaphoreType.DMA((n,)), pltpu.SemaphoreType.DMA((n,)), *thru,
                 jax.ShapeDtypeStruct((8, LANES), F32))
    res = pl.pallas_call(
        body, out_shape=out_shape, in_specs=[HBM_SPEC] * (2 * n) + [pl.BlockSpec(memory_space=pl.ANY)] * len(extra),
        out_specs=(SEM_SPEC, SEM_SPEC, *([HBM_SPEC] * (2 * n)), pl.BlockSpec(memory_space=pltpu.VMEM)),
        input_output_aliases={i: 2 + i for i in range(2 * n)}, name=name,
        compiler_params=pltpu.CompilerParams(has_side_effects=DATAFLOW))(
            *[_hbm(a) for a in list(srcs) + list(lands)], *extra)
    return res[0], res[1], list(res[2:2 + n]), list(res[2 + n:2 + 2 * n]), res[-1]


def exchange_wait(name, send_sems, recv_sems, srcs, lands, after):
    n = len(srcs)

    def body(*refs):
        land_refs = refs[n:2 * n]
        s_sems, r_sems = refs[2 * n], refs[2 * n + 1]
        pos = _mesh_pos()
        for t in range(n):
            seven = land_refs[t].at[pl.ds(0, N_DEV - 1)]
            cp = pltpu.make_async_remote_copy(
                src_ref=seven, dst_ref=seven, send_sem=s_sems.at[t], recv_sem=r_sems.at[t],
                device_id=pos, device_id_type=pl.DeviceIdType.MESH)
            cp.wait_send()
            cp.wait_recv()

    arrs = list(srcs) + list(lands)
    afters = list(after) if isinstance(after, (list, tuple)) else [after]
    res = pl.pallas_call(
        body, out_shape=tuple(pltpu.HBM(a.shape, a.dtype) for a in arrs),
        in_specs=[HBM_SPEC] * (2 * n) + [SEM_SPEC, SEM_SPEC] + [pl.BlockSpec(memory_space=pl.ANY)] * len(afters),
        out_specs=tuple([HBM_SPEC] * (2 * n)), input_output_aliases={i: i for i in range(2 * n)}, name=name,
        compiler_params=pltpu.CompilerParams(has_side_effects=DATAFLOW))(*arrs, send_sems, recv_sems, *afters)
    return list(res[n:])


def reduce_parts(name, parts):
    n, R, C = parts.shape
    tr = _pick(R, max(8, (1 << 20) // (C * 4) // 8 * 8), 8)

    def body(p_ref, o_ref):
        acc = p_ref[0].astype(F32)
        for s in range(1, n):
            acc = acc + p_ref[s].astype(F32)
        o_ref[...] = acc

    return pl.pallas_call(
        body, out_shape=jax.ShapeDtypeStruct((R, C), F32), grid=(R // tr,),
        in_specs=[pl.BlockSpec((n, tr, C), lambda i: (0, i, 0))],
        out_specs=pl.BlockSpec((tr, C), lambda i: (i, 0)), name=name,
        compiler_params=_cparams("parallel"))(parts)


def adamw(name, w, m, v, parts, first=0, prev=None):
    L, R, C = w.shape
    nl = len(parts)
    n = parts[0].shape[0]
    tr = _pick(R, max(8, (1 << 19) // (C * 4) // 8 * 8), 8)
    n_prev = 0 if prev is None else 4

    def body(*refs):
        w_ref, m_ref, v_ref = refs[:3]
        p_refs = refs[3:3 + nl]
        g_ref, d_ref, nm_ref, nv_ref = refs[3 + nl + n_prev:]

        def update(p_ref):
            g = p_ref[0].astype(F32)
            for s in range(1, n):
                g = g + p_ref[s].astype(F32)
            wv = w_ref[...]
            mn = ADAM_B1 * m_ref[...] + (1.0 - ADAM_B1) * g
            vn = ADAM_B2 * v_ref[...] + (1.0 - ADAM_B2) * jnp.square(g)
            m_hat = mn / (1.0 - ADAM_B1 ** ADAM_STEP)
            v_hat = vn / (1.0 - ADAM_B2 ** ADAM_STEP)
            g_ref[...] = g
            d_ref[...] = -ADAM_LR * (m_hat / (jnp.sqrt(v_hat) + ADAM_EPS) + ADAM_WD * wv)
            nm_ref[...] = mn
            nv_ref[...] = vn

        for k in range(nl):
            pl.when(pl.program_id(0) == k)(functools.partial(update, p_refs[k]))

    blk = pl.BlockSpec((None, tr, C), lambda l, i: (first + l, i, 0))
    pspecs = [pl.BlockSpec((n, tr, C), lambda l, i, k=k: (0, jnp.where(l == k, i, 0), 0)) for k in range(nl)]
    sh = jax.ShapeDtypeStruct((L, R, C), F32)
    prev_args = [] if prev is None else list(prev)
    return pl.pallas_call(
        body, out_shape=(sh, sh, sh, sh), grid=(nl, R // tr),
        in_specs=[blk, blk, blk] + pspecs + [pl.BlockSpec(memory_space=pl.ANY)] * n_prev,
        out_specs=(blk, blk, blk, blk), input_output_aliases={3 + nl + q: q for q in range(n_prev)}, name=name,
        compiler_params=_cparams("arbitrary", "arbitrary"))(w, m, v, *parts, *prev_args)


def _cols_from_blocks(g):
    n, R, c = g.shape
    return g.transpose(1, 0, 2).reshape(R, n * c)


def _cols_to_blocks(w):
    R, C = w.shape
    return w.reshape(R, N_DEV, C // N_DEV).transpose(1, 0, 2)


def _uq_permute(w):
    lead = w.shape[:-1]
    w4 = w.reshape(lead + (MLA_HEADS, MLA_NOPE + MLA_ROPE))
    nope = w4[..., :MLA_NOPE].reshape(lead + (MLA_HEADS * MLA_NOPE,))
    rope = jnp.pad(w4[..., MLA_NOPE:], [(0, 0)] * (w4.ndim - 1) + [(0, LANES - MLA_ROPE)])
    return jnp.concatenate([nope, rope.reshape(lead + (MLA_HEADS * LANES,))], axis=-1)


def _uq_unpermute(w):
    lead = w.shape[:-1]
    n = MLA_HEADS * MLA_NOPE
    nope = w[..., :n].reshape(lead + (MLA_HEADS, MLA_NOPE))
    rope = w[..., n:].reshape(lead + (MLA_HEADS, LANES))[..., :MLA_ROPE]
    return jnp.concatenate([nope, rope], axis=-1).reshape(lead + (MLA_HEADS * (MLA_NOPE + MLA_ROPE),))


def _ukv_permute(w):
    lead = w.shape[:-1]
    w4 = w.reshape(lead + (MLA_HEADS, 2, MLA_NOPE))
    return jnp.swapaxes(w4, -3, -2).reshape(lead + (2 * MLA_HEADS * MLA_NOPE,))


def _ukv_unpermute(w):
    lead = w.shape[:-1]
    w4 = w.reshape(lead + (2, MLA_HEADS, MLA_NOPE))
    return jnp.swapaxes(w4, -3, -2).reshape(lead + (2 * MLA_HEADS * MLA_NOPE,))


SMALL = ["norm1_g", "mla_q_norm_g", "mla_kv_norm_g", "fox_b_f", "norm2_g", "ffn_conv_b", "final_norm_g"]
SMALL_TILE = 8 * LANES


def _pack_small(d):
    flat = jnp.concatenate([d[n].reshape(-1).astype(F32) for n in SMALL])
    pad = -flat.shape[0] % SMALL_TILE
    return jnp.pad(flat, (0, pad)).reshape(-1, LANES)


def _unpack_small(packed, like):
    flat = packed.reshape(-1)
    out, o = {}, 0
    for n in SMALL:
        sz = int(np.prod(like[n].shape))
        out[n] = flat[o:o + sz].reshape(like[n].shape)
        o += sz
    return out


WEIGHTS = ["norm1_g", "w_in", "mla_q_norm_g", "mla_kv_norm_g", "mla_w_uq", "mla_w_ukv", "fox_b_f", "w_br_fox",
           "w_br_mla", "w_br_ret", "w_out", "norm2_g", "ffn_w_up", "ffn_w_gate", "ffn_conv_w", "ffn_conv_b",
           "ffn_w_down", "final_norm_g"]
EARLY = ["w_in", "mla_w_uq", "mla_w_ukv"]
LATE = ["w_br_fox", "w_br_mla", "w_br_ret", "w_out", "ffn_w_up", "ffn_w_gate", "ffn_conv_w", "ffn_w_down"]
FFN = ["ffn_w_up", "ffn_w_gate", "ffn_conv_w", "ffn_w_down"]
REST = ["w_br_fox", "w_br_mla", "w_br_ret", "w_out", "mla_w_uq", "mla_w_ukv", "w_in"]
BIG = EARLY + LATE


def kernel(x, norm1_g, w_in, mla_q_norm_g, mla_kv_norm_g, mla_w_uq, mla_w_ukv, fox_b_f, w_br_fox, w_br_mla, w_br_ret, w_out, norm2_g, ffn_w_up, ffn_w_gate, ffn_conv_w, ffn_conv_b, ffn_w_down, final_norm_g, loss_target, m_norm1_g, m_w_in, m_mla_q_norm_g, m_mla_kv_norm_g, m_mla_w_uq, m_mla_w_ukv, m_fox_b_f, m_w_br_fox, m_w_br_mla, m_w_br_ret, m_w_out, m_norm2_g, m_ffn_w_up, m_ffn_w_gate, m_ffn_conv_w, m_ffn_conv_b, m_ffn_w_down, m_final_norm_g, v_norm1_g, v_w_in, v_mla_q_norm_g, v_mla_kv_norm_g, v_mla_w_uq, v_mla_w_ukv, v_fox_b_f, v_w_br_fox, v_w_br_mla, v_w_br_ret, v_w_out, v_norm2_g, v_ffn_w_up, v_ffn_w_gate, v_ffn_conv_w, v_ffn_conv_b, v_ffn_w_down, v_final_norm_g):
    env = dict(locals())
    W = {n: env[n] for n in WEIGHTS}
    Mo = {n: env["m_" + n] for n in WEIGHTS}
    Vo = {n: env["v_" + n] for n in WEIGHTS}
    S, D = x.shape[1], x.shape[2]
    L = w_in.shape[0]
    lay = InLayout(D)
    NP = lay.total
    f = ffn_w_up.shape[-1]
    xs = x.reshape(S, D)
    tgt = loss_target.reshape(S, D)

    local = {n: W[n].astype(BF16) for n in BIG}
    local["w_in"] = lay.permute(W["w_in"]).astype(BF16)
    pending = {}
    token = None
    for l in range(L):
        for grp, names in (("a", EARLY), ("b", LATE)):
            srcs = [local[n][l] for n in names]
            *flight, token = exchange_start(f"gather_start_{l}{grp}", "gather", srcs, landing_zones("gather", srcs),
                                            token)
            pending[l, grp] = flight
    gather_token = token
    cbias_all = ffn_conv_b.reshape(L, 1, N_DEV, 1, f)

    def early_weights(l, after):
        g = dict(zip(EARLY, exchange_wait(f"gather_wait_{l}a", *pending[l, "a"], after)))
        return dict(Win=g["w_in"].reshape(1, D, NP), Wuq=_uq_permute(_cols_from_blocks(g["mla_w_uq"])),
                    Wukv=_ukv_permute(_cols_from_blocks(g["mla_w_ukv"])))

    def late_weights(l, after):
        g = dict(zip(LATE, exchange_wait(f"gather_wait_{l}b", *pending[l, "b"], after)))
        return dict(
            Wout=g["w_out"].reshape(1, D, D), Wbf=_cols_from_blocks(g["w_br_fox"]),
            Wbm=_cols_from_blocks(g["w_br_mla"]), Wbr=_cols_from_blocks(g["w_br_ret"]), Wup=g["ffn_w_up"][None],
            Wgate=g["ffn_w_gate"][None], Wdown=g["ffn_w_down"][None], Wconv=g["ffn_conv_w"].astype(F32)[None],
            cbias=cbias_all[l])

    tab64 = rope_tables(S, MLA_ROPE)
    tab128 = rope_tables(S, RET_DK)
    fox_scale = FOX_DH ** -0.5
    mla_scale = (MLA_NOPE + MLA_ROPE) ** -0.5
    ret_kscale = RET_DK ** -0.5
    R = S // LANES
    AB = _attn_block(S)
    NOPE_W = MLA_HEADS * MLA_NOPE

    def vec(a):
        return a.reshape(1, -1)

    saved = []
    xc = xs
    for l in range(L):
        Wl = early_weights(l, gather_token if l == 0 else xc)
        Win, Wuq, Wukv = Wl["Win"], Wl["Wuq"], Wl["Wukv"]
        s = {"x": xc, "W": Wl}
        h1 = rms_fwd("norm1", xc, 0, D, vec(norm1_g[l]), BF16)
        P = mm_nn("in_proj", h1, Win, F32, b_lead=0)
        s.update(h1=h1, P=P)
        ff_off = lay.off["ff"]
        ft = P[:, ff_off:ff_off + FOX_HEADS].T.reshape(FOX_HEADS, R, LANES)
        bfl = jnp.broadcast_to(fox_b_f[l].reshape(FOX_HEADS, 1, 1), (FOX_HEADS, 1, LANES))
        kbias = fox_gate_fwd("fox_gate", ft, bfl).reshape(FOX_HEADS, S // AB, 1, AB)
        o_fox, lse_fox = attn_fwd("fox_attn", "fox", P, lay.cb("fq", LANES), P, lay.cb("fk", LANES),
                                  P, lay.cb("fv", LANES), FOX_HEADS, FOX_DH, fox_scale, kbias=kbias)
        s.update(ft=ft, bfl=bfl, kbias=kbias, o_fox=o_fox, lse_fox=lse_fox)
        cqn = rms_fwd("mla_q_norm", P, lay.cb("mq", MLA_Q_LORA), MLA_Q_LORA, vec(mla_q_norm_g[l]), BF16)
        qall = mm_nn("mla_uq", cqn, Wuq, F32)
        ckvn = rms_fwd("mla_kv_norm", P, lay.cb("mkv", MLA_KV_LORA), MLA_KV_LORA, vec(mla_kv_norm_g[l]), BF16)
        kvall = mm_nn("mla_ukv", ckvn, Wukv, F32)
        qrope = rope_apply("mla_q_rope", qall, NOPE_W // LANES, MLA_HEADS, tab64, 1.0, F32)
        krope = rope_apply("mla_k_rope", P, lay.cb("mkr", LANES), 1, tab64, 1.0, F32)
        o_mla, lse_mla = attn_fwd("mla_attn", "mla", qall, 0, kvall, 0, kvall, NOPE_W // MLA_V, MLA_HEADS, MLA_V,
                                  mla_scale, q2=qrope, q2_cb=0, k2=krope)
        s.update(cqn=cqn, qall=qall, ckvn=ckvn, kvall=kvall, qrope=qrope, krope=krope, o_mla=o_mla,
                 lse_mla=lse_mla)
        rq = rope_apply("ret_q_rope", P, lay.cb("rq", LANES), RET_HEADS, tab128, 1.0, F32)
        rk = rope_apply("ret_k_rope", P, lay.cb("rk", LANES), RET_HEADS, tab128, ret_kscale, F32)
        o_ret, _ = attn_fwd("ret_attn", "ret", rq, 0, rk, 0, P, lay.cb("rv", RET_DV), RET_HEADS, RET_DV, 1.0)
        c_ret = ret_out_fwd("ret_out", o_ret, P, lay.cb("rg", RET_DV), BF16)
        s.update(rq=rq, rk=rk, o_ret=o_ret, c_ret=c_ret)
        Wl.update(late_weights(l, (o_fox, o_mla, c_ret)))
        Wout, Wbf, Wbm, Wbr = Wl["Wout"], Wl["Wbf"], Wl["Wbm"], Wl["Wbr"]
        Wup, Wgate, Wdown, Wconv, cbias = (Wl[k] for k in ("Wup", "Wgate", "Wdown", "Wconv", "cbias"))
        A = mm_nn("br_fox", o_fox, Wbf, F32)
        Bm = mm_nn("br_mla", o_mla, Wbm, F32)
        C = mm_nn("br_ret", c_ret, Wbr, F32)
        merged = merge_fwd("merge", P, lay.cb("gates", D), A, Bm, C, BF16)
        x2 = mm_nn("out_proj", merged, Wout, F32, b_lead=0, res=xc)
        s.update(A=A, Bm=Bm, C=C, merged=merged, x2=x2)
        h2 = rms_fwd("norm2", x2, 0, D, vec(norm2_g[l]), BF16)
        u = ffn_up("ffn_up", h2, Wup, 0, F32)
        gt = ffn_up("ffn_gate", h2, Wgate, 0, F32)
        act = ffn_act_fwd("ffn_act", u, gt, Wconv, cbias, 0, BF16)
        xc = ffn_down("ffn_down", act, Wdown, 0, x2, F32)
        s.update(h2=h2, u=u, gt=gt, act=act)
        saved.append(s)

    loss_tile, dx, dgf = loss_head("loss_head", xc, vec(final_norm_g), tgt)
    loss = lax.psum(loss_tile[0, 0], ("x", "y", "c"))

    gbig = {n: [None] * L for n in BIG}
    gsmall = {n: [None] * L for n in SMALL if n != "final_norm_g"}
    scattering = {}
    scatter_token = None

    def start_scatter(name, names, l):
        srcs = [gbig[n][l] for n in names]
        *flight, tok = exchange_start(name, "scatter", srcs, landing_zones("scatter", srcs))
        return flight, tok

    for l in reversed(range(L)):
        s = saved[l]
        P = s["P"]
        Wl = s["W"]
        Win, Wout, Wuq, Wukv, Wbf, Wbm, Wbr = (Wl[k] for k in ("Win", "Wout", "Wuq", "Wukv", "Wbf", "Wbm", "Wbr"))
        Wup, Wgate, Wdown, Wconv, cbias = (Wl[k] for k in ("Wup", "Wgate", "Wdown", "Wconv", "cbias"))
        dxb = (dx if scatter_token is None else dx + scatter_token[0, 0]).astype(BF16)
        dact = ffn_down_bwd_act("ffn_down_da", dxb, Wdown, 0, BF16)
        gbig["ffn_w_down"][l] = ffn_down_bwd_w("ffn_down_dw", s["act"], dxb, BF16)
        g, dgt = ffn_act_bwd_point("ffn_act_bwd", s["u"], s["gt"], Wconv, cbias, 0, dact)
        du, dcw, dcb = ffn_act_bwd_conv("ffn_conv_bwd", s["u"], g, Wconv, 0)
        gbig["ffn_conv_w"][l] = dcw.astype(BF16)
        gsmall["ffn_conv_b"][l] = dcb.reshape(-1)
        gbig["ffn_w_up"][l] = ffn_up_bwd_w("ffn_up_dw", s["h2"], du, BF16)
        gbig["ffn_w_gate"][l] = ffn_up_bwd_w("ffn_gate_dw", s["h2"], dgt, BF16)
        dh2 = ffn_up_bwd_h("ffn_up_dh", du, Wup, 0, None, F32)
        dh2 = ffn_up_bwd_h("ffn_gate_dh", dgt, Wgate, 0, dh2, BF16)
        dx2, dg2 = rms_bwd("norm2_bwd", s["x2"], 0, D, vec(norm2_g[l]), dh2, F32, res=dx)
        gsmall["norm2_g"][l] = dg2.reshape(-1)
        scattering[l, "ffn"], scatter_token = start_scatter(f"scatter_start_{l}ffn", FFN, l)
        dx2b = (dx2 + scatter_token[0, 0]).astype(BF16)
        dmerged = mm_nt("out_proj_dm", dx2b, Wout, BF16, b_lead=0)
        gbig["w_out"][l] = mm_tn("out_proj_dw", s["merged"], dx2b, BF16).reshape(N_DEV, D // N_DEV, D)
        dgates, dA, dB, dC = merge_bwd("merge_bwd", P, lay.cb("gates", D), s["A"], s["Bm"], s["C"], dmerged)
        do_fox = mm_nt("br_fox_do", dA, Wbf, F32)
        do_mla = mm_nt("br_mla_do", dB, Wbm, F32)
        dc_ret = mm_nt("br_ret_do", dC, Wbr, BF16)
        gbig["w_br_fox"][l] = _cols_to_blocks(mm_tn("br_fox_dw", s["o_fox"], dA, BF16))
        gbig["w_br_mla"][l] = _cols_to_blocks(mm_tn("br_mla_dw", s["o_mla"], dB, BF16))
        gbig["w_br_ret"][l] = _cols_to_blocks(mm_tn("br_ret_dw", s["c_ret"], dC, BF16))
        do_ret, drg = ret_out_bwd("ret_out_bwd", s["o_ret"], P, lay.cb("rg", RET_DV), dc_ret)
        rb = attn_bwd("ret_attn_bwd", "ret", s["rq"], 0, s["rk"], 0, P, lay.cb("rv", RET_DV), RET_HEADS, RET_DV,
                      1.0, do_ret)
        drq = rope_apply("ret_q_rope_bwd", rb["dq1"], 0, RET_HEADS, tab128, 1.0, BF16, transpose=True)
        drk = rope_apply("ret_k_rope_bwd", rb["dk1"], 0, RET_HEADS, tab128, ret_kscale, BF16, transpose=True)
        drv = rb["dv"].astype(BF16)
        mb = attn_bwd("mla_attn_bwd", "mla", s["qall"], 0, s["kvall"], 0, s["kvall"], NOPE_W // MLA_V, MLA_HEADS,
                      MLA_V, mla_scale, do_mla, o=s["o_mla"], lse=s["lse_mla"], q2=s["qrope"], q2_cb=0,
                      k2=s["krope"])
        dqrope = rope_apply("mla_q_rope_bwd", mb["dq2"], 0, MLA_HEADS, tab64, 1.0, BF16, transpose=True)
        dkr_sum = head_sum("mla_k_rope_sum", mb["dk2h"], MLA_HEADS, F32)
        dmkr = rope_apply("mla_k_rope_bwd", dkr_sum, 0, 1, tab64, 1.0, BF16, transpose=True)
        dqall = jnp.concatenate([mb["dq1"].astype(BF16), dqrope], axis=1)
        dkvall = jnp.concatenate([mb["dk1"].astype(BF16), mb["dv"].astype(BF16)], axis=1)
        dcqn = mm_nt("mla_uq_dx", dqall, Wuq, F32)
        dckvn = mm_nt("mla_ukv_dx", dkvall, Wukv, F32)
        guq = _uq_unpermute(mm_tn("mla_uq_dw", s["cqn"], dqall, BF16))
        gukv = _ukv_unpermute(mm_tn("mla_ukv_dw", s["ckvn"], dkvall, BF16))
        gbig["mla_w_uq"][l] = _cols_to_blocks(guq)
        gbig["mla_w_ukv"][l] = _cols_to_blocks(gukv)
        dmq, dgq = rms_bwd("mla_q_norm_bwd", P, lay.cb("mq", MLA_Q_LORA), MLA_Q_LORA, vec(mla_q_norm_g[l]),
                           dcqn, BF16)
        dmkv, dgkv = rms_bwd("mla_kv_norm_bwd", P, lay.cb("mkv", MLA_KV_LORA), MLA_KV_LORA,
                             vec(mla_kv_norm_g[l]), dckvn, BF16)
        gsmall["mla_q_norm_g"][l] = dgq.reshape(-1)
        gsmall["mla_kv_norm_g"][l] = dgkv.reshape(-1)
        fb = attn_bwd("fox_attn_bwd", "fox", P, lay.cb("fq", LANES), P, lay.cb("fk", LANES), P,
                      lay.cb("fv", LANES), FOX_HEADS, FOX_DH, fox_scale, do_fox, o=s["o_fox"], lse=s["lse_fox"],
                      kbias=s["kbias"])
        drow = fb["drow"].reshape(S, FOX_HEADS, LANES)[:, :, 0].T.reshape(FOX_HEADS, R, LANES)
        dft, dbf = fox_gate_bwd("fox_gate_bwd", s["ft"], s["bfl"], fb["dkb"].reshape(FOX_HEADS, R, LANES) - drow)
        gsmall["fox_b_f"][l] = dbf[:, 0, 0]
        dff = jnp.pad(dft.reshape(FOX_HEADS, S).T, ((0, 0), (0, LANES - FOX_HEADS))).astype(BF16)
        segs = dict(gates=dgates, rv=drv, rg=drg, mq=dmq, rq=drq, rk=drk, mkv=dmkv, fq=fb["dq1"].astype(BF16),
                    fk=fb["dk1"].astype(BF16), fv=fb["dv"].astype(BF16), mkr=dmkr, ff=dff)
        dP = jnp.concatenate([segs[n] for n in lay.order], axis=1)
        gbig["w_in"][l] = mm_tn("in_proj_dw", s["h1"], dP, BF16).reshape(N_DEV, D // N_DEV, NP)
        scattering[l, "rest"], scatter_token = start_scatter(f"scatter_start_{l}rest", REST, l)
        dh1 = mm_nt("in_proj_dh", dP, Win, BF16, b_lead=0)
        dx, dg1 = rms_bwd("norm1_bwd", s["x"], 0, D, vec(norm1_g[l]) + scatter_token[0:1, 0:1], dh1, F32, res=dx2)
        gsmall["norm1_g"][l] = dg1.reshape(-1)

    small_like = {n: W[n] for n in SMALL}
    small_part = {n: jnp.stack(gsmall[n]) for n in gsmall}
    small_part["final_norm_g"] = dgf.reshape(-1)
    small_recv = exchange("gather_small_grads", [("gather", [_pack_small(small_part)])])[0]
    ps = adamw("adamw_small", _pack_small(small_like)[None], _pack_small({n: Mo[n] for n in SMALL})[None],
               _pack_small({n: Vo[n] for n in SMALL})[None], [small_recv[0]])

    def received(l, after):
        r = {}
        for grp, names in (("ffn", FFN), ("rest", REST)):
            r.update(zip(names, exchange_wait(f"scatter_wait_{l}{grp}", *scattering[l, grp], after)))
        r["w_in"] = lay.unpermute(reduce_parts("w_in_grad_sum", r["w_in"]))[None]
        return r

    out = {}
    if L > 1:
        recv = [received(l, dx) for l in range(1, L)]
        for n in BIG:
            out[n] = adamw("adamw_" + n, W[n], Mo[n], Vo[n], [r[n] for r in recv], first=1)
    recv0 = received(0, out[BIG[-1]][0] if L > 1 else dx)
    for n in BIG:
        out[n] = adamw("adamw0_" + n, W[n], Mo[n], Vo[n], [recv0[n]], first=0, prev=out.get(n))
    small_out = [_unpack_small(a[0], small_like) for a in ps]
    for n in SMALL:
        out[n] = tuple(so[n] for so in small_out)

    grads = [out[n][0] for n in WEIGHTS]
    deltas = [out[n][1] for n in WEIGHTS]
    new_m = [out[n][2] for n in WEIGHTS]
    new_v = [out[n][3] for n in WEIGHTS]
    return (loss, dx.reshape(1, S, D), *grads, *deltas, *new_m, *new_v)
```

```python
import functools
import math

import numpy as np
import jax
import jax.numpy as jnp
from jax import lax
from jax.experimental import pallas as pl
from jax.experimental.pallas import tpu as pltpu

F32 = jnp.float32
BF16 = jnp.bfloat16

CHUNK = 64
NORM_EPS = 1e-6
ROPE_THETA = 10000.0
FOX_HEADS, FOX_DH = 6, 128
FOX_W = FOX_HEADS * FOX_DH
MLA_HEADS, MLA_NOPE, MLA_ROPE, MLA_V = 6, 128, 64, 128
MLA_Q_LORA, MLA_KV_LORA = 512, 256
MLA_W = MLA_HEADS * MLA_V
RET_HEADS, RET_DK, RET_DV = 4, 128, 256
RET_QK_W, RET_V_W = RET_HEADS * RET_DK, RET_HEADS * RET_DV
ADAM_LR, ADAM_B1, ADAM_B2, ADAM_EPS, ADAM_WD, ADAM_STEP = 0.001, 0.9, 0.999, 1e-08, 0.01, 10

N_DEV = 8
LANES = 128
V7X_VMEM_LIMIT_BYTES = 52 * 1024 * 1024
NEG_BIG = -1e30
HIGHEST = lax.Precision.HIGHEST

NT_DIMS = (((1,), (1,)), ((), ()))
TN_DIMS = (((0,), (0,)), ((), ()))
NN_DIMS = (((1,), (0,)), ((), ()))


def _pick(n, cap, mult=LANES):
    best = None
    for t in range(mult, min(n, cap) + 1, mult):
        if n % t == 0:
            best = t
    return n if best is None else best


def _cparams(*sem):
    return pltpu.CompilerParams(dimension_semantics=sem, vmem_limit_bytes=V7X_VMEM_LIMIT_BYTES)


class InLayout:
    def __init__(self, d_model):
        d = d_model
        self.d = d
        orig = dict(fq=(0, FOX_W), fk=(FOX_W, FOX_W), fv=(2 * FOX_W, FOX_W), ff=(3 * FOX_W, FOX_HEADS))
        o = 3 * FOX_W + FOX_HEADS
        for name, w in (("mq", MLA_Q_LORA), ("mkv", MLA_KV_LORA), ("mkr", MLA_ROPE), ("rq", RET_QK_W),
                        ("rk", RET_QK_W), ("rv", RET_V_W), ("rg", RET_V_W), ("gates", 3 * d)):
            orig[name] = (o, w)
            o += w
        self.orig = orig
        self.orig_width = o
        order = ["gates", "rv", "rg", "mq", "rq", "rk", "mkv", "fq", "fk", "fv", "mkr", "ff"]
        self.order = order
        self.off, self.width = {}, {}
        p = 0
        for name in order:
            w = orig[name][1]
            wp = -(-w // LANES) * LANES
            self.off[name], self.width[name] = p, wp
            p += wp
        self.total = p

    def cb(self, name, block):
        assert self.off[name] % block == 0, (name, block)
        return self.off[name] // block

    def permute(self, w):
        parts = []
        for name in self.order:
            o, n = self.orig[name]
            seg = w[..., o:o + n]
            pad = self.width[name] - n
            if pad:
                seg = jnp.pad(seg, [(0, 0)] * (w.ndim - 1) + [(0, pad)])
            parts.append(seg)
        return jnp.concatenate(parts, axis=-1)

    def unpermute(self, w):
        names = sorted(self.orig, key=lambda n: self.orig[n][0])
        return jnp.concatenate([w[..., self.off[n]:self.off[n] + self.orig[n][1]] for n in names], axis=-1)


def _mm(name, a, b, out_shape, grid, a_spec, b_spec, o_spec, dims, acc_shape, res=None, nsub=0):
    nk = grid[-1]
    has_res = res is not None

    def body(*refs):
        if has_res:
            a_ref, b_ref, r_ref, o_ref = refs[:4]
        else:
            a_ref, b_ref, o_ref = refs[:3]
            r_ref = None
        if nsub:
            prod = lax.dot_general(a_ref[0].astype(BF16), b_ref[0].astype(BF16), dims, preferred_element_type=F32)
            for q in range(1, nsub):
                prod = prod + lax.dot_general(a_ref[q].astype(BF16), b_ref[q].astype(BF16), dims,
                                              preferred_element_type=F32)
        else:
            prod = lax.dot_general(a_ref[...].astype(BF16), b_ref[...].astype(BF16), dims,
                                   preferred_element_type=F32)
        if nk == 1:
            if has_res:
                prod = prod + r_ref[...].astype(F32)
            o_ref[...] = prod.astype(o_ref.dtype)
        else:
            acc_ref = refs[-1]
            k = pl.program_id(len(grid) - 1)

            @pl.when(k == 0)
            def _():
                acc_ref[...] = prod

            @pl.when(k > 0)
            def _():
                acc_ref[...] += prod

            @pl.when(k == nk - 1)
            def _():
                r = acc_ref[...]
                if has_res:
                    r = r + r_ref[...].astype(F32)
                o_ref[...] = r.astype(o_ref.dtype)

    in_specs = [a_spec, b_spec] + ([o_spec] if has_res else [])
    args = (a, b) + ((res,) if has_res else ())
    scratch = [pltpu.VMEM(acc_shape, F32)] if nk > 1 else []
    sem = ("parallel",) * (len(grid) - 1) + ("arbitrary",)
    return pl.pallas_call(body, out_shape=out_shape, grid=grid, in_specs=in_specs, out_specs=o_spec,
                          scratch_shapes=scratch, name=name, compiler_params=_cparams(*sem))(*args)


def mm_nn(name, a, b, out_dtype, b_lead=None, res=None):
    M, K = a.shape
    N = b.shape[-1]
    tm, tn, tk = _pick(M, 1024, 8), _pick(N, 1024), _pick(K, 2048)
    grid = (M // tm, N // tn, K // tk)
    a_spec = pl.BlockSpec((tm, tk), lambda i, j, k: (i, k))
    if b_lead is None:
        b_spec = pl.BlockSpec((tk, tn), lambda i, j, k: (k, j))
    else:
        b_spec = pl.BlockSpec((None, tk, tn), lambda i, j, k: (b_lead, k, j))
    o_spec = pl.BlockSpec((tm, tn), lambda i, j, k: (i, j))
    return _mm(name, a, b, jax.ShapeDtypeStruct((M, N), out_dtype), grid, a_spec, b_spec, o_spec,
               NN_DIMS, (tm, tn), res)


def mm_nt(name, a, b, out_dtype, b_lead=None, res=None):
    M, N = a.shape
    K = b.shape[-2]
    tm, tko, tk = _pick(M, 1024, 8), _pick(K, 1024), _pick(N, 2048)
    grid = (M // tm, K // tko, N // tk)
    a_spec = pl.BlockSpec((tm, tk), lambda i, j, k: (i, k))
    if b_lead is None:
        b_spec = pl.BlockSpec((tko, tk), lambda i, j, k: (j, k))
    else:
        b_spec = pl.BlockSpec((None, tko, tk), lambda i, j, k: (b_lead, j, k))
    o_spec = pl.BlockSpec((tm, tko), lambda i, j, k: (i, j))
    return _mm(name, a, b, jax.ShapeDtypeStruct((M, K), out_dtype), grid, a_spec, b_spec, o_spec,
               NT_DIMS, (tm, tko), res)


def mm_tn(name, a, b, out_dtype):
    M, K = a.shape
    N = b.shape[-1]
    cap = 4096 if (a.dtype == BF16 and b.dtype == BF16) else 2048
    tko, tn, tk = _pick(K, 1024), _pick(N, 1024), _pick(M, cap, 8)
    grid = (K // tko, N // tn, M // tk)
    a_spec = pl.BlockSpec((tk, tko), lambda i, j, k: (k, i))
    b_spec = pl.BlockSpec((tk, tn), lambda i, j, k: (k, j))
    o_spec = pl.BlockSpec((tko, tn), lambda i, j, k: (i, j))
    return _mm(name, a, b, jax.ShapeDtypeStruct((K, N), out_dtype), grid, a_spec, b_spec, o_spec,
               TN_DIMS, (tko, tn))


FFN_SUB = 4
def ffn_up(name, h, w, l, out_dtype):
    M, D = h.shape
    f = w.shape[-1]
    tm = _pick(M, 1024, 8)
    grid = (M // tm, N_DEV, 1)
    return _mm(name, h, w, jax.ShapeDtypeStruct((N_DEV, M, f), out_dtype), grid,
               pl.BlockSpec((tm, D), lambda i, j, k: (i, 0)),
               pl.BlockSpec((None, None, D, f), lambda i, j, k: (l, j, 0, 0)),
               pl.BlockSpec((None, tm, f), lambda i, j, k: (j, i, 0)), NN_DIMS, (tm, f))


def ffn_down(name, act, w, l, res, out_dtype):
    _, M, f = act.shape
    D = w.shape[-1]
    tm, tn = _pick(M, 1024, 8), _pick(D, 1024)
    grid = (M // tm, D // tn, N_DEV // FFN_SUB)
    return _mm(name, act, w, jax.ShapeDtypeStruct((M, D), out_dtype), grid,
               pl.BlockSpec((FFN_SUB, tm, f), lambda i, j, k: (k, i, 0)),
               pl.BlockSpec((None, FFN_SUB, f, tn), lambda i, j, k: (l, k, 0, j)),
               pl.BlockSpec((tm, tn), lambda i, j, k: (i, j)), NN_DIMS, (tm, tn), res, nsub=FFN_SUB)


def ffn_down_bwd_act(name, dy, w, l, out_dtype):
    M, D = dy.shape
    f = w.shape[-2]
    tm = _pick(M, 1024, 8)
    grid = (M // tm, N_DEV, 1)
    return _mm(name, dy, w, jax.ShapeDtypeStruct((N_DEV, M, f), out_dtype), grid,
               pl.BlockSpec((tm, D), lambda i, j, k: (i, 0)),
               pl.BlockSpec((None, None, f, D), lambda i, j, k: (l, j, 0, 0)),
               pl.BlockSpec((None, tm, f), lambda i, j, k: (j, i, 0)), NT_DIMS, (tm, f))


def ffn_down_bwd_w(name, act, dy, out_dtype):
    _, M, f = act.shape
    D = dy.shape[-1]
    tn, tk = _pick(D, 1024), _pick(M, 4096 if dy.dtype == BF16 else 2048, 8)
    grid = (N_DEV, D // tn, M // tk)
    return _mm(name, act, dy, jax.ShapeDtypeStruct((N_DEV, f, D), out_dtype), grid,
               pl.BlockSpec((None, tk, f), lambda j, n, k: (j, k, 0)),
               pl.BlockSpec((tk, tn), lambda j, n, k: (k, n)),
               pl.BlockSpec((None, f, tn), lambda j, n, k: (j, 0, n)), TN_DIMS, (f, tn))


def ffn_up_bwd_h(name, du, w, l, res, out_dtype):
    _, M, f = du.shape
    D = w.shape[-2]
    tm, tn = _pick(M, 1024, 8), _pick(D, 1024)
    grid = (M // tm, D // tn, N_DEV // FFN_SUB)
    return _mm(name, du, w, jax.ShapeDtypeStruct((M, D), out_dtype), grid,
               pl.BlockSpec((FFN_SUB, tm, f), lambda i, j, k: (k, i, 0)),
               pl.BlockSpec((None, FFN_SUB, tn, f), lambda i, j, k: (l, k, j, 0)),
               pl.BlockSpec((tm, tn), lambda i, j, k: (i, j)), NT_DIMS, (tm, tn), res, nsub=FFN_SUB)


def ffn_up_bwd_w(name, h, du, out_dtype):
    M, D = h.shape
    f = du.shape[-1]
    tko, tk = _pick(D, 1024), _pick(M, 4096, 8)
    grid = (N_DEV, D // tko, M // tk)
    return _mm(name, h, du, jax.ShapeDtypeStruct((N_DEV, D, f), out_dtype), grid,
               pl.BlockSpec((tk, tko), lambda j, n, k: (k, n)),
               pl.BlockSpec((None, tk, f), lambda j, n, k: (j, k, 0)),
               pl.BlockSpec((None, tko, f), lambda j, n, k: (j, n, 0)), TN_DIMS, (tko, f))


ROW_CHUNK = 16


def _rms(xf, g):
    return xf * lax.rsqrt(jnp.mean(xf * xf, axis=-1, keepdims=True) + NORM_EPS) * g


def rms_fwd(name, x, cb, W, g, out_dtype):
    S = x.shape[0]
    tr = _pick(S, 256, 8)

    def body(x_ref, g_ref, o_ref):
        o_ref[...] = _rms(x_ref[...].astype(F32), g_ref[...]).astype(o_ref.dtype)

    return pl.pallas_call(
        body, out_shape=jax.ShapeDtypeStruct((S, W), out_dtype), grid=(S // tr,),
        in_specs=[pl.BlockSpec((tr, W), lambda i: (i, cb)), pl.BlockSpec((1, W), lambda i: (0, 0))],
        out_specs=pl.BlockSpec((tr, W), lambda i: (i, 0)), name=name, compiler_params=_cparams("parallel"))(x, g)


def rms_bwd(name, x, cb, W, g, dy, out_dtype, res=None):
    S = x.shape[0]
    tr = _pick(S, 256, 8)
    has_res = res is not None

    def body(*refs):
        if has_res:
            x_ref, g_ref, dy_ref, r_ref, dx_ref, dg_ref = refs
        else:
            x_ref, g_ref, dy_ref, dx_ref, dg_ref = refs
        gv = g_ref[...]

        def chunk(c, dg):
            rows = pl.ds(pl.multiple_of(c * ROW_CHUNK, ROW_CHUNK), ROW_CHUNK)
            _, vjp = jax.vjp(_rms, x_ref[rows, :].astype(F32), gv)
            dx, dgc = vjp(dy_ref[rows, :].astype(F32))
            if has_res:
                dx = dx + r_ref[rows, :]
            dx_ref[rows, :] = dx.astype(dx_ref.dtype)
            return dg + dgc

        dg = lax.fori_loop(0, tr // ROW_CHUNK, chunk, jnp.zeros((1, W), F32))

        @pl.when(pl.program_id(0) == 0)
        def _():
            dg_ref[...] = jnp.zeros_like(dg_ref)

        dg_ref[...] += dg

    row = pl.BlockSpec((tr, W), lambda i: (i, 0))
    vec = pl.BlockSpec((1, W), lambda i: (0, 0))
    in_specs = [pl.BlockSpec((tr, W), lambda i: (i, cb)), vec, row] + ([row] if has_res else [])
    args = (x, g, dy) + ((res,) if has_res else ())
    return pl.pallas_call(
        body, out_shape=(jax.ShapeDtypeStruct((S, W), out_dtype), jax.ShapeDtypeStruct((1, W), F32)),
        grid=(S // tr,), in_specs=in_specs, out_specs=(row, vec), name=name,
        compiler_params=_cparams("arbitrary"))(*args)


def rope_tables(S, d):
    pos = jnp.arange(S, dtype=F32)
    inv_freq = ROPE_THETA ** (-jnp.arange(0, d, 2, dtype=F32) / d)
    ang = pos[:, None] * inv_freq[None, :]
    cos, sin = jnp.cos(ang), jnp.sin(ang)
    half = d // 2
    z = jnp.zeros((S, LANES - d), F32)
    zh = jnp.zeros((S, half), F32)
    c = jnp.concatenate([cos, cos, z], axis=1)
    sa = jnp.concatenate([-sin, zh, z], axis=1)
    sb = jnp.concatenate([zh, sin, z], axis=1)
    return c, sa, sb, half


def rope_apply(name, x, cb, H, tabs, scale, out_dtype, transpose=False):
    c, sa, sb, half = tabs
    S = x.shape[0]
    tr = _pick(S, 512, 8)
    up, down = LANES - half, half

    def body(x_ref, c_ref, sa_ref, sb_ref, o_ref):
        xv = x_ref[...].astype(F32)
        if not transpose:
            y = xv * c_ref[...] + pltpu.roll(xv, up, 1) * sa_ref[...] + pltpu.roll(xv, down, 1) * sb_ref[...]
            y = y * scale
        else:
            xv = xv * scale
            y = (xv * c_ref[...] + pltpu.roll(xv * sa_ref[...], down, 1)
                 + pltpu.roll(xv * sb_ref[...], up, 1))
        o_ref[...] = y.astype(o_ref.dtype)

    tab = pl.BlockSpec((tr, LANES), lambda h, i: (i, 0))
    return pl.pallas_call(
        body, out_shape=jax.ShapeDtypeStruct((S, H * LANES), out_dtype), grid=(H, S // tr),
        in_specs=[pl.BlockSpec((tr, LANES), lambda h, i: (i, cb + h)), tab, tab, tab],
        out_specs=pl.BlockSpec((tr, LANES), lambda h, i: (i, h)), name=name,
        compiler_params=_cparams("parallel", "parallel"))(x, c, sa, sb)


def _ret_out(o, g):
    y = o * lax.rsqrt(jnp.mean(o * o, axis=-1, keepdims=True) + NORM_EPS)
    return y * jax.nn.silu(g)


def ret_out_fwd(name, o, gsrc, g_cb, out_dtype):
    S = o.shape[0]
    tr = _pick(S, 512, 8)
    W = RET_DV

    def body(o_ref, g_ref, y_ref):
        y_ref[...] = _ret_out(o_ref[...], g_ref[...].astype(F32)).astype(y_ref.dtype)

    blk = pl.BlockSpec((tr, W), lambda h, i: (i, h))
    return pl.pallas_call(
        body, out_shape=jax.ShapeDtypeStruct((S, RET_HEADS * W), out_dtype), grid=(RET_HEADS, S // tr),
        in_specs=[blk, pl.BlockSpec((tr, W), lambda h, i: (i, g_cb + h))], out_specs=blk, name=name,
        compiler_params=_cparams("parallel", "parallel"))(o, gsrc)


def ret_out_bwd(name, o, gsrc, g_cb, dy):
    S = o.shape[0]
    tr = _pick(S, 512, 8)
    W = RET_DV

    def body(o_ref, g_ref, dy_ref, do_ref, dg_ref):
        _, vjp = jax.vjp(_ret_out, o_ref[...], g_ref[...].astype(F32))
        do, dg = vjp(dy_ref[...].astype(F32))
        do_ref[...] = do.astype(do_ref.dtype)
        dg_ref[...] = dg.astype(dg_ref.dtype)

    blk = pl.BlockSpec((tr, W), lambda h, i: (i, h))
    return pl.pallas_call(
        body, out_shape=(jax.ShapeDtypeStruct((S, RET_HEADS * W), F32),
                         jax.ShapeDtypeStruct((S, RET_HEADS * W), BF16)),
        grid=(RET_HEADS, S // tr),
        in_specs=[blk, pl.BlockSpec((tr, W), lambda h, i: (i, g_cb + h)), blk], out_specs=(blk, blk),
        name=name, compiler_params=_cparams("parallel", "parallel"))(o, gsrc, dy)


def _sigmoid(x):
    return 0.5 * jnp.tanh(0.5 * x) + 0.5


def _merge(g0, g1, g2, a, b, c):
    return _sigmoid(g0) * a + _sigmoid(g1) * b + _sigmoid(g2) * c


def merge_fwd(name, P, gates_cb, a, b, c, out_dtype):
    S, D = a.shape
    tr = _pick(S, 128, 8)

    def body(g0, g1, g2, a_ref, b_ref, c_ref, o_ref):
        o_ref[...] = _merge(g0[...], g1[...], g2[...], a_ref[...], b_ref[...], c_ref[...]).astype(o_ref.dtype)

    row = pl.BlockSpec((tr, D), lambda i: (i, 0))
    gs = [pl.BlockSpec((tr, D), lambda i, k=k: (i, gates_cb + k)) for k in range(3)]
    return pl.pallas_call(
        body, out_shape=jax.ShapeDtypeStruct((S, D), out_dtype), grid=(S // tr,),
        in_specs=gs + [row, row, row], out_specs=row, name=name,
        compiler_params=_cparams("parallel"))(P, P, P, a, b, c)


def merge_bwd(name, P, gates_cb, a, b, c, dm):
    S, D = a.shape
    tr = _pick(S, 128, 8)

    def body(g0, g1, g2, a_ref, b_ref, c_ref, dm_ref, dg_ref, da_ref, db_ref, dc_ref):
        _, vjp = jax.vjp(_merge, g0[...], g1[...], g2[...], a_ref[...], b_ref[...], c_ref[...])
        d0, d1, d2, da, db, dc = vjp(dm_ref[...].astype(F32))
        dg_ref[:, 0:D] = d0.astype(dg_ref.dtype)
        dg_ref[:, D:2 * D] = d1.astype(dg_ref.dtype)
        dg_ref[:, 2 * D:3 * D] = d2.astype(dg_ref.dtype)
        da_ref[...] = da.astype(da_ref.dtype)
        db_ref[...] = db.astype(db_ref.dtype)
        dc_ref[...] = dc.astype(dc_ref.dtype)

    row = pl.BlockSpec((tr, D), lambda i: (i, 0))
    gs = [pl.BlockSpec((tr, D), lambda i, k=k: (i, gates_cb + k)) for k in range(3)]
    bf = jax.ShapeDtypeStruct((S, D), BF16)
    return pl.pallas_call(
        body, out_shape=(jax.ShapeDtypeStruct((S, 3 * D), BF16), bf, bf, bf), grid=(S // tr,),
        in_specs=gs + [row, row, row, row],
        out_specs=(pl.BlockSpec((tr, 3 * D), lambda i: (i, 0)), row, row, row), name=name,
        compiler_params=_cparams("parallel"))(P, P, P, a, b, c, dm)


HALO = 8


CONV_CHUNK = 32


def _shifted_back(u_ref, uh_ref, ext_ref, s1_ref, s2_ref, tr):
    ext_ref[0:HALO, :] = jnp.where(pl.program_id(1) > 0, uh_ref[...], 0.0)
    ext_ref[HALO:HALO + tr, :] = u_ref[...]
    s1_ref[...] = ext_ref[HALO - 1:HALO - 1 + tr, :]
    s2_ref[...] = ext_ref[HALO - 2:HALO - 2 + tr, :]


def _conv3(cw, cb, u, u1, u2):
    return cb + ((cw[0:1, :] * u2 + cw[1:2, :] * u1) + cw[2:3, :] * u)


def _chunks(tr, fn):
    def step(c, carry):
        return fn(pl.ds(pl.multiple_of(c * CONV_CHUNK, CONV_CHUNK), CONV_CHUNK), carry)
    return step


def _ffn_specs(S, f, tr, l):
    nb = tr // HALO
    row = pl.BlockSpec((None, tr, f), lambda j, i: (j, i, 0))
    prev = pl.BlockSpec((None, HALO, f), lambda j, i: (j, jnp.maximum(i * nb - 1, 0), 0))
    nxt = pl.BlockSpec((None, HALO, f), lambda j, i: (j, jnp.minimum((i + 1) * nb, S // HALO - 1), 0))
    cw = pl.BlockSpec((None, None, 3, f), lambda j, i: (l, j, 0, 0))
    cb = pl.BlockSpec((None, None, 1, f), lambda j, i: (l, j, 0, 0))
    return row, prev, nxt, cw, cb


def ffn_act_fwd(name, u, gt, cw, cb, l, out_dtype):
    _, S, f = u.shape
    tr = _pick(S, 512, 8)
    row, prev, _, cws, cbs = _ffn_specs(S, f, tr, l)

    def body(u_ref, uh_ref, gt_ref, cw_ref, cb_ref, o_ref, ext_ref, s1_ref, s2_ref):
        _shifted_back(u_ref, uh_ref, ext_ref, s1_ref, s2_ref, tr)
        cwv, cbv = cw_ref[...], cb_ref[...]

        def chunk(rows, carry):
            uc = _conv3(cwv, cbv, u_ref[rows, :], s1_ref[rows, :], s2_ref[rows, :])
            o_ref[rows, :] = (jax.nn.gelu(uc) * gt_ref[rows, :]).astype(o_ref.dtype)
            return carry

        lax.fori_loop(0, tr // CONV_CHUNK, _chunks(tr, chunk), 0)

    return pl.pallas_call(
        body, out_shape=jax.ShapeDtypeStruct((N_DEV, S, f), out_dtype), grid=(N_DEV, S // tr),
        in_specs=[row, prev, row, cws, cbs], out_specs=row,
        scratch_shapes=[pltpu.VMEM((tr + HALO, f), F32), pltpu.VMEM((tr, f), F32), pltpu.VMEM((tr, f), F32)],
        name=name, compiler_params=_cparams("parallel", "parallel"))(u, u, gt, cw, cb)


def ffn_act_bwd_point(name, u, gt, cw, cb, l, dact):
    _, S, f = u.shape
    tr = _pick(S, 512, 8)
    row, prev, _, cws, cbs = _ffn_specs(S, f, tr, l)

    def body(u_ref, uh_ref, gt_ref, cw_ref, cb_ref, da_ref, g_ref, dgt_ref, ext_ref, s1_ref, s2_ref):
        _shifted_back(u_ref, uh_ref, ext_ref, s1_ref, s2_ref, tr)
        cwv, cbv = cw_ref[...], cb_ref[...]

        def chunk(rows, carry):
            uc = _conv3(cwv, cbv, u_ref[rows, :], s1_ref[rows, :], s2_ref[rows, :])
            _, vjp = jax.vjp(lambda c, t: jax.nn.gelu(c) * t, uc, gt_ref[rows, :])
            g, dgt = vjp(da_ref[rows, :].astype(F32))
            g_ref[rows, :] = g
            dgt_ref[rows, :] = dgt.astype(dgt_ref.dtype)
            return carry

        lax.fori_loop(0, tr // CONV_CHUNK, _chunks(tr, chunk), 0)

    return pl.pallas_call(
        body, out_shape=(jax.ShapeDtypeStruct((N_DEV, S, f), F32), jax.ShapeDtypeStruct((N_DEV, S, f), BF16)),
        grid=(N_DEV, S // tr), in_specs=[row, prev, row, cws, cbs, row], out_specs=(row, row),
        scratch_shapes=[pltpu.VMEM((tr + HALO, f), F32), pltpu.VMEM((tr, f), F32), pltpu.VMEM((tr, f), F32)],
        name=name, compiler_params=_cparams("parallel", "parallel"))(u, u, gt, cw, cb, dact)


def ffn_act_bwd_conv(name, u, g, cw, l):
    _, S, f = u.shape
    tr = _pick(S, 512, 8)
    nt = S // tr
    row, prev, nxt, cws, _ = _ffn_specs(S, f, tr, l)

    def body(u_ref, uh_ref, g_ref, gn_ref, cw_ref, du_ref, dcw_ref, dcb_ref, ext_ref, s1_ref, s2_ref, n1_ref, n2_ref):
        i = pl.program_id(1)
        _shifted_back(u_ref, uh_ref, ext_ref, s1_ref, s2_ref, tr)
        ext_ref[0:tr, :] = g_ref[...]
        ext_ref[tr:tr + HALO, :] = jnp.where(i < nt - 1, gn_ref[...], 0.0)
        n1_ref[...] = ext_ref[1:1 + tr, :]
        n2_ref[...] = ext_ref[2:2 + tr, :]
        cw = cw_ref[...]

        def chunk(rows, carry):
            d0, d1, d2, db = carry
            g = g_ref[rows, :]
            du_ref[rows, :] = (cw[2:3, :] * g + cw[1:2, :] * n1_ref[rows, :]
                               + cw[0:1, :] * n2_ref[rows, :]).astype(du_ref.dtype)
            return (d0 + jnp.sum(g * s2_ref[rows, :], axis=0, keepdims=True),
                    d1 + jnp.sum(g * s1_ref[rows, :], axis=0, keepdims=True),
                    d2 + jnp.sum(g * u_ref[rows, :], axis=0, keepdims=True),
                    db + jnp.sum(g, axis=0, keepdims=True))

        z = jnp.zeros((1, f), F32)
        d0, d1, d2, db = lax.fori_loop(0, tr // CONV_CHUNK, _chunks(tr, chunk), (z, z, z, z))

        @pl.when(i == 0)
        def _():
            dcw_ref[...] = jnp.zeros_like(dcw_ref)
            dcb_ref[...] = jnp.zeros_like(dcb_ref)

        dcw_ref[0:1, :] += d0
        dcw_ref[1:2, :] += d1
        dcw_ref[2:3, :] += d2
        dcb_ref[...] += db

    tile = pltpu.VMEM((tr, f), F32)
    return pl.pallas_call(
        body, out_shape=(jax.ShapeDtypeStruct((N_DEV, S, f), BF16), jax.ShapeDtypeStruct((N_DEV, 3, f), F32),
                         jax.ShapeDtypeStruct((N_DEV, 1, f), F32)),
        grid=(N_DEV, nt), in_specs=[row, prev, row, nxt, cws],
        out_specs=(row, pl.BlockSpec((None, 3, f), lambda j, i: (j, 0, 0)),
                   pl.BlockSpec((None, 1, f), lambda j, i: (j, 0, 0))),
        scratch_shapes=[pltpu.VMEM((tr + HALO, f), F32), tile, tile, tile, tile], name=name,
        compiler_params=_cparams("parallel", "arbitrary"))(u, u, g, g, cw)


def loss_head(name, x, g, tgt):
    S, D = x.shape
    tr = _pick(S, 256, 8)

    def body(x_ref, g_ref, t_ref, l_ref, dx_ref, dg_ref):
        tg = t_ref[...]

        def f(xv, gv):
            err = jnp.square(_rms(xv, gv) - tg)
            return 0.5 * jnp.sum(jnp.mean(err, axis=-1))

        val, vjp = jax.vjp(f, x_ref[...], g_ref[...])
        dx, dg = vjp(jnp.ones((), F32))
        dx_ref[...] = dx

        @pl.when(pl.program_id(0) == 0)
        def _():
            l_ref[...] = jnp.zeros_like(l_ref)
            dg_ref[...] = jnp.zeros_like(dg_ref)

        l_ref[...] += val
        dg_ref[...] += dg

    row = pl.BlockSpec((tr, D), lambda i: (i, 0))
    vec = pl.BlockSpec((1, D), lambda i: (0, 0))
    lt = pl.BlockSpec((8, LANES), lambda i: (0, 0))
    return pl.pallas_call(
        body, out_shape=(jax.ShapeDtypeStruct((8, LANES), F32), jax.ShapeDtypeStruct((S, D), F32),
                         jax.ShapeDtypeStruct((1, D), F32)),
        grid=(S // tr,), in_specs=[row, vec, row], out_specs=(lt, row, vec), name=name,
        compiler_params=_cparams("arbitrary"))(x, g, tgt)


def _tri(n, fn):
    r = lax.broadcasted_iota(jnp.int32, (n, n), 0)
    c = lax.broadcasted_iota(jnp.int32, (n, n), 1)
    return jnp.where(fn(r, c), 1.0, 0.0).astype(F32)


def _log_sigmoid(z):
    return jnp.minimum(z, 0.0) - jnp.log1p(jnp.exp(-jnp.abs(z)))


def fox_gate_fwd(name, ft, b):
    H, R, _ = ft.shape

    def body(f_ref, b_ref, o_ref):
        ls = _log_sigmoid(f_ref[...] + b_ref[...])
        cum = jnp.dot(ls, _tri(LANES, lambda r, c: r <= c), precision=HIGHEST, preferred_element_type=F32)
        tot = jnp.broadcast_to(cum[:, LANES - 1:LANES], (R, LANES))
        off = jnp.dot(_tri(R, lambda r, c: r > c), tot, precision=HIGHEST, preferred_element_type=F32)
        o_ref[...] = -(cum + off)

    blk = pl.BlockSpec((None, R, LANES), lambda h: (h, 0, 0))
    return pl.pallas_call(
        body, out_shape=jax.ShapeDtypeStruct((H, R, LANES), F32), grid=(H,),
        in_specs=[blk, pl.BlockSpec((None, 1, LANES), lambda h: (h, 0, 0))], out_specs=blk, name=name,
        compiler_params=_cparams("parallel"))(ft, b)


def fox_gate_bwd(name, ft, b, dkb):
    H, R, _ = ft.shape

    def body(f_ref, b_ref, d_ref, df_ref, db_ref):
        z = f_ref[...] + b_ref[...]
        d = d_ref[...]
        rev = jnp.dot(d, _tri(LANES, lambda r, c: r >= c), precision=HIGHEST, preferred_element_type=F32)
        tot = jnp.broadcast_to(rev[:, 0:1], (R, LANES))
        off = jnp.dot(_tri(R, lambda r, c: r < c), tot, precision=HIGHEST, preferred_element_type=F32)
        dls = -(rev + off)
        dz = dls * jax.nn.sigmoid(-z)
        df_ref[...] = dz
        s = jnp.sum(jnp.sum(dz, axis=1, keepdims=True), axis=0, keepdims=True)
        db_ref[...] = jnp.broadcast_to(s, (1, LANES))

    blk = pl.BlockSpec((None, R, LANES), lambda h: (h, 0, 0))
    vec = pl.BlockSpec((None, 1, LANES), lambda h: (h, 0, 0))
    return pl.pallas_call(
        body, out_shape=(jax.ShapeDtypeStruct((H, R, LANES), F32), jax.ShapeDtypeStruct((H, 1, LANES), F32)),
        grid=(H,), in_specs=[blk, vec, blk], out_specs=(blk, vec), name=name,
        compiler_params=_cparams("parallel"))(ft, b, dkb)


def _ret_log_gamma(h):
    lg = [float(np.log(np.float32(1.0) - np.float32(2.0) ** np.float32(-5.0 - i))) for i in range(RET_HEADS)]
    out = jnp.float32(lg[RET_HEADS - 1])
    for i in range(RET_HEADS - 2, -1, -1):
        out = jnp.where(h == i, jnp.float32(lg[i]), out)
    return out


def _visible(mode, B):
    r = lax.broadcasted_iota(jnp.int32, (B, B), 0)
    c = lax.broadcasted_iota(jnp.int32, (B, B), 1)
    if mode == "fox":
        return c <= r
    return (c // CHUNK) <= (r // CHUNK)


def _decay(lg, B, blocks_apart):
    r = lax.broadcasted_iota(jnp.int32, (B, B), 0)
    c = lax.broadcasted_iota(jnp.int32, (B, B), 1)
    dist = jnp.abs(r - c + blocks_apart * B).astype(F32)
    return jnp.exp(lg * dist)


def _attn_block(S):
    return 512 if S >= 2048 else 128


def attn_fwd(name, mode, q1, q1_cb, k1, k1_cb, v, v_cb, H, dv, scale, q2=None, q2_cb=0, k2=None, kbias=None):
    S = q1.shape[0]
    B = _attn_block(S)
    nq = S // B
    softmax = mode != "ret"
    two = mode == "mla"
    has_bias = mode == "fox"

    def body(*refs):
        it = iter(refs)
        q1_ref, k1_ref, v_ref = next(it), next(it), next(it)
        q2_ref = next(it) if two else None
        k2_ref = next(it) if two else None
        kb_ref = next(it) if has_bias else None
        o_ref = next(it)
        lse_ref = next(it) if softmax else None
        kbuf, vbuf = next(it), next(it)
        acc = next(it)
        m_ref = next(it) if softmax else None
        l_ref = next(it) if softmax else None
        s_all = next(it) if softmax else None
        h = pl.program_id(0)
        i = pl.program_id(1)

        @pl.when(i == 0)
        def _():
            kbuf[:, 0:LANES] = k1_ref[...].astype(BF16)
            vbuf[...] = v_ref[...].astype(BF16)
            if two:
                kbuf[:, LANES:2 * LANES] = k2_ref[...].astype(BF16)

        qb = q1_ref[...].astype(BF16)
        if two:
            qb = jnp.concatenate([qb, q2_ref[...].astype(BF16)], axis=1)
        lg = _ret_log_gamma(h) if mode == "ret" else None
        acc[...] = jnp.zeros_like(acc)
        if softmax:
            m_ref[...] = jnp.full_like(m_ref, NEG_BIG)
            l_ref[...] = jnp.zeros_like(l_ref)

        def scores(g, diag):
            rows = slice(g * B, (g + 1) * B)
            s = lax.dot_general(qb, kbuf[rows, :], NT_DIMS, preferred_element_type=F32)
            if softmax:
                s = s * scale
                if has_bias:
                    s = s + kb_ref[g]
                if diag:
                    s = jnp.where(_visible(mode, B), s, NEG_BIG)
                s_all[:, rows] = s
                m_ref[...] = jnp.maximum(m_ref[...], jnp.max(s, axis=1, keepdims=True))
            else:
                if diag:
                    p = jnp.where(_visible(mode, B), s * _decay(lg, B, 0), 0.0)
                else:
                    p = s * _decay(lg, B, i - g)
                acc[...] += jnp.dot(p.astype(BF16), vbuf[rows, :], preferred_element_type=F32)

        def weighted(g):
            rows = slice(g * B, (g + 1) * B)
            p = jnp.exp(s_all[:, rows] - m_ref[...])
            l_ref[...] += jnp.sum(p, axis=1, keepdims=True)
            acc[...] += jnp.dot(p.astype(BF16), vbuf[rows, :], preferred_element_type=F32)

        for g in range(nq):
            pl.when(g < i)(functools.partial(scores, g, False))
            pl.when(g == i)(functools.partial(scores, g, True))
        if softmax:
            for g in range(nq):
                pl.when(g <= i)(functools.partial(weighted, g))
            o_ref[...] = acc[...] / l_ref[...]
            lse_ref[...] = jnp.broadcast_to(m_ref[...] + jnp.log(l_ref[...]), (B, LANES))
        else:
            o_ref[...] = acc[...]

    in_specs = [pl.BlockSpec((B, LANES), lambda h, i: (i, q1_cb + h)),
                pl.BlockSpec((S, LANES), lambda h, i: (0, k1_cb + h)),
                pl.BlockSpec((S, dv), lambda h, i: (0, v_cb + h))]
    args = [q1, k1, v]
    if two:
        in_specs += [pl.BlockSpec((B, LANES), lambda h, i: (i, q2_cb + h)),
                     pl.BlockSpec((S, LANES), lambda h, i: (0, 0))]
        args += [q2, k2]
    if has_bias:
        in_specs.append(pl.BlockSpec((None, nq, 1, B), lambda h, i: (h, 0, 0, 0)))
        args.append(kbias)
    out_shape = [jax.ShapeDtypeStruct((S, H * dv), F32)]
    out_specs = [pl.BlockSpec((B, dv), lambda h, i: (i, h))]
    if softmax:
        out_shape.append(jax.ShapeDtypeStruct((S, H * LANES), F32))
        out_specs.append(pl.BlockSpec((B, LANES), lambda h, i: (i, h)))
    kw = 2 * LANES if two else LANES
    scratch = [pltpu.VMEM((S, kw), BF16), pltpu.VMEM((S, dv), BF16), pltpu.VMEM((B, dv), F32)]
    if softmax:
        scratch += [pltpu.VMEM((B, 1), F32), pltpu.VMEM((B, 1), F32), pltpu.VMEM((B, S), F32)]
    res = pl.pallas_call(body, out_shape=tuple(out_shape), grid=(H, nq), in_specs=in_specs,
                         out_specs=tuple(out_specs), scratch_shapes=scratch, name=name,
                         compiler_params=_cparams("parallel", "arbitrary"))(*args)
    return res if softmax else (res[0], None)


def attn_bwd(name, mode, q1, q1_cb, k1, k1_cb, v, v_cb, H, dv, scale, do, o=None, lse=None,
             q2=None, q2_cb=0, k2=None, kbias=None):
    S = q1.shape[0]
    B = _attn_block(S)
    nb = S // B
    softmax = mode != "ret"
    two = mode == "mla"
    has_bias = mode == "fox"

    def body(*refs):
        it = iter(refs)
        q1_ref, k1_ref, v_ref, do_ref = next(it), next(it), next(it), next(it)
        o_ref = next(it) if softmax else None
        lse_ref = next(it) if softmax else None
        q2_ref = next(it) if two else None
        k2_ref = next(it) if two else None
        kb_ref = next(it) if has_bias else None
        dq1_ref, dk1_ref, dv_ref = next(it), next(it), next(it)
        dq2_ref = next(it) if two else None
        dk2_ref = next(it) if two else None
        dkb_ref = next(it) if has_bias else None
        drow_ref = next(it) if has_bias else None
        qbuf, dobuf = next(it), next(it)
        qT, doT = next(it), next(it)
        delta = next(it) if softmax else None
        dk_acc, dv_acc = next(it), next(it)
        dkb_acc = next(it) if has_bias else None
        h = pl.program_id(0)
        j = pl.program_id(1)

        @pl.when(j == 0)
        def _():
            qbuf[:, 0:LANES] = q1_ref[...].astype(BF16)
            dobuf[...] = do_ref[...].astype(BF16)
            qT[0:LANES, :] = q1_ref[...].astype(F32).T.astype(BF16)
            doT[...] = do_ref[...].astype(F32).T.astype(BF16)
            dq1_ref[...] = jnp.zeros_like(dq1_ref)
            if has_bias:
                drow_ref[...] = jnp.zeros_like(drow_ref)
            if two:
                qbuf[:, LANES:2 * LANES] = q2_ref[...].astype(BF16)
                qT[LANES:2 * LANES, :] = q2_ref[...].astype(F32).T.astype(BF16)
                dq2_ref[...] = jnp.zeros_like(dq2_ref)
            if softmax:
                def drow(t, carry):
                    rows = pl.ds(pl.multiple_of(t * B, B), B)
                    delta[rows, :] = jnp.sum(do_ref[rows, :].astype(F32) * o_ref[rows, :], axis=1, keepdims=True)
                    return carry
                lax.fori_loop(0, nb, drow, 0)

        kj = k1_ref[...].astype(BF16)
        if two:
            kj = jnp.concatenate([kj, k2_ref[...].astype(BF16)], axis=1)
        vj = v_ref[...].astype(BF16)
        kbj = kb_ref[...] if has_bias else None
        lg = _ret_log_gamma(h) if mode == "ret" else None
        dk_acc[...] = jnp.zeros_like(dk_acc)
        dv_acc[...] = jnp.zeros_like(dv_acc)
        if has_bias:
            dkb_acc[...] = jnp.zeros_like(dkb_acc)

        def step(i, diag):
            rows = slice(i * B, (i + 1) * B)
            qi = qbuf[rows, :]
            doi = dobuf[rows, :]
            s = lax.dot_general(qi, kj, NT_DIMS, preferred_element_type=F32)
            dp = lax.dot_general(doi, vj, NT_DIMS, preferred_element_type=F32)
            if softmax:
                s = s * scale
                if has_bias:
                    s = s + kbj
                if diag:
                    s = jnp.where(_visible(mode, B), s, NEG_BIG)
                p = jnp.exp(s - lse_ref[rows, 0:1])
                ds = p * (dp - delta[rows, :])
                if has_bias:
                    dkb_acc[...] += jnp.sum(ds, axis=0, keepdims=True)
                    drow_ref[rows, :] += jnp.broadcast_to(jnp.sum(ds, axis=1, keepdims=True), (B, LANES))
                dsb = (ds * scale).astype(BF16)
            else:
                if diag:
                    dec = jnp.where(_visible(mode, B), _decay(lg, B, 0), 0.0)
                else:
                    dec = _decay(lg, B, i - j)
                p = s * dec
                dsb = (dp * dec).astype(BF16)
            dv_acc[...] += jnp.dot(doT[:, rows], p.astype(BF16), preferred_element_type=F32)
            dk_acc[...] += jnp.dot(qT[:, rows], dsb, preferred_element_type=F32)
            dq = jnp.dot(dsb, kj, preferred_element_type=F32)
            dq1_ref[rows, :] += dq[:, 0:LANES]
            if two:
                dq2_ref[rows, :] += dq[:, LANES:2 * LANES]

        for i in range(nb):
            pl.when(i == j)(functools.partial(step, i, True))
            pl.when(i > j)(functools.partial(step, i, False))
        dk1_ref[...] = dk_acc[0:LANES, :].T
        dv_ref[...] = dv_acc[...].T
        if two:
            dk2_ref[...] = dk_acc[LANES:2 * LANES, :].T
        if has_bias:
            dkb_ref[...] = dkb_acc[...]

    full = lambda w, cb: pl.BlockSpec((S, w), lambda h, j: (0, cb + h))
    blk = lambda w, cb: pl.BlockSpec((B, w), lambda h, j: (j, cb + h))
    in_specs = [full(LANES, q1_cb), blk(LANES, k1_cb), blk(dv, v_cb), full(dv, 0)]
    args = [q1, k1, v, do]
    if softmax:
        in_specs += [full(dv, 0), full(LANES, 0)]
        args += [o, lse]
    if two:
        in_specs += [full(LANES, q2_cb), pl.BlockSpec((B, LANES), lambda h, j: (j, 0))]
        args += [q2, k2]
    if has_bias:
        in_specs.append(pl.BlockSpec((None, None, 1, B), lambda h, j: (h, j, 0, 0)))
        args.append(kbias)
    names = ["dq1", "dk1", "dv"]
    out_shape = [jax.ShapeDtypeStruct((S, H * LANES), F32), jax.ShapeDtypeStruct((S, H * LANES), F32),
                 jax.ShapeDtypeStruct((S, H * dv), F32)]
    out_specs = [full(LANES, 0), blk(LANES, 0), blk(dv, 0)]
    if two:
        names += ["dq2", "dk2h"]
        out_shape += [jax.ShapeDtypeStruct((S, H * LANES), F32)] * 2
        out_specs += [full(LANES, 0), blk(LANES, 0)]
    if has_bias:
        names.append("dkb")
        out_shape.append(jax.ShapeDtypeStruct((H, nb, 1, B), F32))
        out_specs.append(pl.BlockSpec((None, None, 1, B), lambda h, j: (h, j, 0, 0)))
        names.append("drow")
        out_shape.append(jax.ShapeDtypeStruct((S, H * LANES), F32))
        out_specs.append(full(LANES, 0))
    kw = 2 * LANES if two else LANES
    scratch = [pltpu.VMEM((S, kw), BF16), pltpu.VMEM((S, dv), BF16),
               pltpu.VMEM((kw, S), BF16), pltpu.VMEM((dv, S), BF16)]
    if softmax:
        scratch.append(pltpu.VMEM((S, 1), F32))
    scratch += [pltpu.VMEM((kw, B), F32), pltpu.VMEM((dv, B), F32)]
    if has_bias:
        scratch.append(pltpu.VMEM((1, B), F32))
    res = pl.pallas_call(body, out_shape=tuple(out_shape), grid=(H, nb), in_specs=in_specs,
                         out_specs=tuple(out_specs), scratch_shapes=scratch, name=name,
                         compiler_params=_cparams("parallel", "arbitrary"))(*args)
    return dict(zip(names, res))


def head_sum(name, x, H, out_dtype):
    S = x.shape[0]
    tr = _pick(S, 512, 8)

    def body(x_ref, o_ref):
        acc = x_ref[:, 0:LANES]
        for h in range(1, H):
            acc = acc + x_ref[:, h * LANES:(h + 1) * LANES]
        o_ref[...] = acc.astype(o_ref.dtype)

    return pl.pallas_call(
        body, out_shape=jax.ShapeDtypeStruct((S, LANES), out_dtype), grid=(S // tr,),
        in_specs=[pl.BlockSpec((tr, H * LANES), lambda i: (i, 0))],
        out_specs=pl.BlockSpec((tr, LANES), lambda i: (i, 0)), name=name,
        compiler_params=_cparams("parallel"))(x)


def _mesh_pos():
    return lax.axis_index("x"), lax.axis_index("y"), lax.axis_index("c")


def _peer(pos, k):
    x, y, c = pos
    px = 1 - x if k & 4 else x
    py = 1 - y if k & 2 else y
    pc = 1 - c if k & 1 else c
    return (px, py, pc), 4 * px + 2 * py + pc


def exchange(name, tensors):
    nt = len(tensors)
    flat_in, counts = [], []
    out_shape = []
    for mode, srcs in tensors:
        counts.append(len(srcs))
        flat_in += list(srcs)
        rc = srcs[0].shape[-2:]
        out_shape.append(jax.ShapeDtypeStruct((len(srcs), N_DEV) + tuple(rc), srcs[0].dtype))
    n_in = len(flat_in)

    def body(*refs):
        ins = refs[:n_in]
        outs = refs[n_in:n_in + nt]
        send_sems, recv_sems, local_sems = refs[n_in + nt:]
        pos = _mesh_pos()
        me = 4 * pos[0] + 2 * pos[1] + pos[2]
        srcs_of, base = [], 0
        for t in range(nt):
            srcs_of.append(ins[base:base + counts[t]])
            base += counts[t]

        def src_view(t, l, slot):
            ref = srcs_of[t][l]
            return ref if tensors[t][0] == "gather" else ref.at[slot]

        def all_layers(t, slot):
            return outs[t].at[pl.ds(0, counts[t]), slot]

        for t in range(nt):
            for l in range(counts[t]):
                pltpu.make_async_copy(src_view(t, l, me), outs[t].at[l, me], local_sems.at[t]).start()
        for t in range(nt):
            for k in range(1, N_DEV):
                peer, pid = _peer(pos, k)
                for l in range(counts[t]):
                    pltpu.make_async_remote_copy(
                        src_ref=src_view(t, l, pid), dst_ref=outs[t].at[l, me],
                        send_sem=send_sems.at[t, k - 1], recv_sem=recv_sems.at[t, k - 1],
                        device_id=peer, device_id_type=pl.DeviceIdType.MESH).start()
        for t in range(nt):
            for k in range(1, N_DEV):
                peer, pid = _peer(pos, k)
                pltpu.make_async_remote_copy(
                    src_ref=all_layers(t, pid), dst_ref=all_layers(t, pid),
                    send_sem=send_sems.at[t, k - 1], recv_sem=recv_sems.at[t, k - 1],
                    device_id=peer, device_id_type=pl.DeviceIdType.MESH).wait()
        for t in range(nt):
            pltpu.make_async_copy(all_layers(t, me), all_layers(t, me), local_sems.at[t]).wait()

    any_spec = pl.BlockSpec(memory_space=pl.ANY)
    return pl.pallas_call(
        body, out_shape=tuple(out_shape), in_specs=[any_spec] * n_in, out_specs=tuple([any_spec] * nt),
        scratch_shapes=[pltpu.SemaphoreType.DMA((nt, N_DEV - 1)), pltpu.SemaphoreType.DMA((nt, N_DEV - 1)),
                        pltpu.SemaphoreType.DMA((nt,))],
        name=name)(*flat_in)


HBM_SPEC = pl.BlockSpec(memory_space=pltpu.HBM)
SEM_SPEC = pl.BlockSpec(memory_space=pltpu.SEMAPHORE)
DATAFLOW = pltpu.SideEffectType.DATAFLOW_SIDE_EFFECTING


def _hbm(a):
    return pltpu.with_memory_space_constraint(a, pltpu.HBM)


def landing_zones(mode, srcs):
    pos = _mesh_pos()
    me = 4 * pos[0] + 2 * pos[1] + pos[2]
    lands = []
    for s in srcs:
        R, C = s.shape[-2:]
        own = s[None] if mode == "gather" else lax.dynamic_slice(s, (me, 0, 0), (1, R, C))
        lands.append(lax.dynamic_update_slice(lax.empty((N_DEV, R, C), s.dtype), own, (me, 0, 0)))
    return lands


def exchange_start(name, mode, srcs, lands, after=None):
    n = len(srcs)
    extra = [] if after is None else [after]

    def body(*refs):
        src_refs, land_refs = refs[:n], refs[n:2 * n]
        send_sems, recv_sems = refs[2 * n + len(extra)], refs[2 * n + len(extra) + 1]
        token = refs[-1]
        pos = _mesh_pos()
        me = 4 * pos[0] + 2 * pos[1] + pos[2]
        for t in range(n):
            for k in range(1, N_DEV):
                peer, pid = _peer(pos, k)
                src = src_refs[t] if mode == "gather" else src_refs[t].at[pid]
                pltpu.make_async_remote_copy(
                    src_ref=src, dst_ref=land_refs[t].at[me], send_sem=send_sems.at[t], recv_sem=recv_sems.at[t],
                    device_id=peer, device_id_type=pl.DeviceIdType.MESH).start()
        token[...] = jnp.zeros_like(token)

    thru = [pltpu.HBM(a.shape, a.dtype) for a in list(srcs) + list(lands)]
    out_shape = (pltpu.SemaphoreType.DMA((n,)), pltpu.SemaphoreType.DMA((n,)), *thru,
                 jax.ShapeDtypeStruct((8, LANES), F32))
    res = pl.pallas_call(
        body, out_shape=out_shape, in_specs=[HBM_SPEC] * (2 * n) + [pl.BlockSpec(memory_space=pl.ANY)] * len(extra),
        out_specs=(SEM_SPEC, SEM_SPEC, *([HBM_SPEC] * (2 * n)), pl.BlockSpec(memory_space=pltpu.VMEM)),
        input_output_aliases={i: 2 + i for i in range(2 * n)}, name=name,
        compiler_params=pltpu.CompilerParams(has_side_effects=DATAFLOW))(
            *[_hbm(a) for a in list(srcs) + list(lands)], *extra)
    return res[0], res[1], list(res[2:2 + n]), list(res[2 + n:2 + 2 * n]), res[-1]


def exchange_wait(name, send_sems, recv_sems, srcs, lands, after):
    n = len(srcs)

    def body(*refs):
        land_refs = refs[n:2 * n]
        s_sems, r_sems = refs[2 * n], refs[2 * n + 1]
        pos = _mesh_pos()
        for t in range(n):
            seven = land_refs[t].at[pl.ds(0, N_DEV - 1)]
            cp = pltpu.make_async_remote_copy(
                src_ref=seven, dst_ref=seven, send_sem=s_sems.at[t], recv_sem=r_sems.at[t],
                device_id=pos, device_id_type=pl.DeviceIdType.MESH)
            cp.wait_send()
            cp.wait_recv()

    arrs = list(srcs) + list(lands)
    afters = list(after) if isinstance(after, (list, tuple)) else [after]
    res = pl.pallas_call(
        body, out_shape=tuple(pltpu.HBM(a.shape, a.dtype) for a in arrs),
        in_specs=[HBM_SPEC] * (2 * n) + [SEM_SPEC, SEM_SPEC] + [pl.BlockSpec(memory_space=pl.ANY)] * len(afters),
        out_specs=tuple([HBM_SPEC] * (2 * n)), input_output_aliases={i: i for i in range(2 * n)}, name=name,
        compiler_params=pltpu.CompilerParams(has_side_effects=DATAFLOW))(*arrs, send_sems, recv_sems, *afters)
    return list(res[n:])


def reduce_parts(name, parts):
    n, R, C = parts.shape
    tr = _pick(R, max(8, (1 << 20) // (C * 4) // 8 * 8), 8)

    def body(p_ref, o_ref):
        acc = p_ref[0].astype(F32)
        for s in range(1, n):
            acc = acc + p_ref[s].astype(F32)
        o_ref[...] = acc

    return pl.pallas_call(
        body, out_shape=jax.ShapeDtypeStruct((R, C), F32), grid=(R // tr,),
        in_specs=[pl.BlockSpec((n, tr, C), lambda i: (0, i, 0))],
        out_specs=pl.BlockSpec((tr, C), lambda i: (i, 0)), name=name,
        compiler_params=_cparams("parallel"))(parts)


def adamw(name, w, m, v, parts, first=0, prev=None):
    L, R, C = w.shape
    nl = len(parts)
    n = parts[0].shape[0]
    tr = _pick(R, max(8, (1 << 19) // (C * 4) // 8 * 8), 8)
    n_prev = 0 if prev is None else 4

    def body(*refs):
        w_ref, m_ref, v_ref = refs[:3]
        p_refs = refs[3:3 + nl]
        g_ref, d_ref, nm_ref, nv_ref = refs[3 + nl + n_prev:]

        def update(p_ref):
            g = p_ref[0].astype(F32)
            for s in range(1, n):
                g = g + p_ref[s].astype(F32)
            wv = w_ref[...]
            mn = ADAM_B1 * m_ref[...] + (1.0 - ADAM_B1) * g
            vn = ADAM_B2 * v_ref[...] + (1.0 - ADAM_B2) * jnp.square(g)
            m_hat = mn / (1.0 - ADAM_B1 ** ADAM_STEP)
            v_hat = vn / (1.0 - ADAM_B2 ** ADAM_STEP)
            g_ref[...] = g
            d_ref[...] = -ADAM_LR * (m_hat / (jnp.sqrt(v_hat) + ADAM_EPS) + ADAM_WD * wv)
            nm_ref[...] = mn
            nv_ref[...] = vn

        for k in range(nl):
            pl.when(pl.program_id(0) == k)(functools.partial(update, p_refs[k]))

    blk = pl.BlockSpec((None, tr, C), lambda l, i: (first + l, i, 0))
    pspecs = [pl.BlockSpec((n, tr, C), lambda l, i, k=k: (0, jnp.where(l == k, i, 0), 0)) for k in range(nl)]
    sh = jax.ShapeDtypeStruct((L, R, C), F32)
    prev_args = [] if prev is None else list(prev)
    return pl.pallas_call(
        body, out_shape=(sh, sh, sh, sh), grid=(nl, R // tr),
        in_specs=[blk, blk, blk] + pspecs + [pl.BlockSpec(memory_space=pl.ANY)] * n_prev,
        out_specs=(blk, blk, blk, blk), input_output_aliases={3 + nl + q: q for q in range(n_prev)}, name=name,
        compiler_params=_cparams("arbitrary", "arbitrary"))(w, m, v, *parts, *prev_args)


def _cols_from_blocks(g):
    n, R, c = g.shape
    return g.transpose(1, 0, 2).reshape(R, n * c)


def _cols_to_blocks(w):
    R, C = w.shape
    return w.reshape(R, N_DEV, C // N_DEV).transpose(1, 0, 2)


def _uq_permute(w):
    lead = w.shape[:-1]
    w4 = w.reshape(lead + (MLA_HEADS, MLA_NOPE + MLA_ROPE))
    nope = w4[..., :MLA_NOPE].reshape(lead + (MLA_HEADS * MLA_NOPE,))
    rope = jnp.pad(w4[..., MLA_NOPE:], [(0, 0)] * (w4.ndim - 1) + [(0, LANES - MLA_ROPE)])
    return jnp.concatenate([nope, rope.reshape(lead + (MLA_HEADS * LANES,))], axis=-1)


def _uq_unpermute(w):
    lead = w.shape[:-1]
    n = MLA_HEADS * MLA_NOPE
    nope = w[..., :n].reshape(lead + (MLA_HEADS, MLA_NOPE))
    rope = w[..., n:].reshape(lead + (MLA_HEADS, LANES))[..., :MLA_ROPE]
    return jnp.concatenate([nope, rope], axis=-1).reshape(lead + (MLA_HEADS * (MLA_NOPE + MLA_ROPE),))


def _ukv_permute(w):
    lead = w.shape[:-1]
    w4 = w.reshape(lead + (MLA_HEADS, 2, MLA_NOPE))
    return jnp.swapaxes(w4, -3, -2).reshape(lead + (2 * MLA_HEADS * MLA_NOPE,))


def _ukv_unpermute(w):
    lead = w.shape[:-1]
    w4 = w.reshape(lead + (2, MLA_HEADS, MLA_NOPE))
    return jnp.swapaxes(w4, -3, -2).reshape(lead + (2 * MLA_HEADS * MLA_NOPE,))


SMALL = ["norm1_g", "mla_q_norm_g", "mla_kv_norm_g", "fox_b_f", "norm2_g", "ffn_conv_b", "final_norm_g"]
SMALL_TILE = 8 * LANES


def _pack_small(d):
    flat = jnp.concatenate([d[n].reshape(-1).astype(F32) for n in SMALL])
    pad = -flat.shape[0] % SMALL_TILE
    return jnp.pad(flat, (0, pad)).reshape(-1, LANES)


def _unpack_small(packed, like):
    flat = packed.reshape(-1)
    out, o = {}, 0
    for n in SMALL:
        sz = int(np.prod(like[n].shape))
        out[n] = flat[o:o + sz].reshape(like[n].shape)
        o += sz
    return out


WEIGHTS = ["norm1_g", "w_in", "mla_q_norm_g", "mla_kv_norm_g", "mla_w_uq", "mla_w_ukv", "fox_b_f", "w_br_fox",
           "w_br_mla", "w_br_ret", "w_out", "norm2_g", "ffn_w_up", "ffn_w_gate", "ffn_conv_w", "ffn_conv_b",
           "ffn_w_down", "final_norm_g"]
EARLY = ["w_in", "mla_w_uq", "mla_w_ukv"]
LATE = ["w_br_fox", "w_br_mla", "w_br_ret", "w_out", "ffn_w_up", "ffn_w_gate", "ffn_conv_w", "ffn_w_down"]
FFN = ["ffn_w_up", "ffn_w_gate", "ffn_conv_w", "ffn_w_down"]
TRANSPOSED = ("ffn_w_up", "ffn_w_gate")
REST = ["w_br_fox", "w_br_mla", "w_br_ret", "w_out", "mla_w_uq", "mla_w_ukv", "w_in"]
BIG = EARLY + LATE


def kernel(x, norm1_g, w_in, mla_q_norm_g, mla_kv_norm_g, mla_w_uq, mla_w_ukv, fox_b_f, w_br_fox, w_br_mla, w_br_ret, w_out, norm2_g, ffn_w_up, ffn_w_gate, ffn_conv_w, ffn_conv_b, ffn_w_down, final_norm_g, loss_target, m_norm1_g, m_w_in, m_mla_q_norm_g, m_mla_kv_norm_g, m_mla_w_uq, m_mla_w_ukv, m_fox_b_f, m_w_br_fox, m_w_br_mla, m_w_br_ret, m_w_out, m_norm2_g, m_ffn_w_up, m_ffn_w_gate, m_ffn_conv_w, m_ffn_conv_b, m_ffn_w_down, m_final_norm_g, v_norm1_g, v_w_in, v_mla_q_norm_g, v_mla_kv_norm_g, v_mla_w_uq, v_mla_w_ukv, v_fox_b_f, v_w_br_fox, v_w_br_mla, v_w_br_ret, v_w_out, v_norm2_g, v_ffn_w_up, v_ffn_w_gate, v_ffn_conv_w, v_ffn_conv_b, v_ffn_w_down, v_final_norm_g):
    env = dict(locals())
    W = {n: env[n] for n in WEIGHTS}
    Mo = {n: env["m_" + n] for n in WEIGHTS}
    Vo = {n: env["v_" + n] for n in WEIGHTS}
    S, D = x.shape[1], x.shape[2]
    L = w_in.shape[0]
    lay = InLayout(D)
    NP = lay.total
    f = ffn_w_up.shape[-1]
    xs = x.reshape(S, D)
    tgt = loss_target.reshape(S, D)

    local = {n: W[n].astype(BF16) for n in BIG}
    local["w_in"] = lay.permute(W["w_in"]).astype(BF16)
    pending = {}
    token = None
    for l in range(L):
        for grp, names in (("a", EARLY), ("b", LATE)):
            srcs = [local[n][l] for n in names]
            *flight, token = exchange_start(f"gather_start_{l}{grp}", "gather", srcs, landing_zones("gather", srcs),
                                            token)
            pending[l, grp] = flight
    gather_token = token
    cbias_all = ffn_conv_b.reshape(L, 1, N_DEV, 1, f)

    def early_weights(l, after):
        g = dict(zip(EARLY, exchange_wait(f"gather_wait_{l}a", *pending[l, "a"], after)))
        return dict(Win=g["w_in"].reshape(1, D, NP), Wuq=_uq_permute(_cols_from_blocks(g["mla_w_uq"])),
                    Wukv=_ukv_permute(_cols_from_blocks(g["mla_w_ukv"])))

    def late_weights(l, after):
        g = dict(zip(LATE, exchange_wait(f"gather_wait_{l}b", *pending[l, "b"], after)))
        return dict(
            Wout=g["w_out"].reshape(1, D, D), Wbf=_cols_from_blocks(g["w_br_fox"]),
            Wbm=_cols_from_blocks(g["w_br_mla"]), Wbr=_cols_from_blocks(g["w_br_ret"]), Wup=g["ffn_w_up"][None],
            Wgate=g["ffn_w_gate"][None], Wdown=g["ffn_w_down"][None], Wconv=g["ffn_conv_w"].astype(F32)[None],
            cbias=cbias_all[l])

    tab64 = rope_tables(S, MLA_ROPE)
    tab128 = rope_tables(S, RET_DK)
    fox_scale = FOX_DH ** -0.5
    mla_scale = (MLA_NOPE + MLA_ROPE) ** -0.5
    ret_kscale = RET_DK ** -0.5
    R = S // LANES
    AB = _attn_block(S)
    NOPE_W = MLA_HEADS * MLA_NOPE

    def vec(a):
        return a.reshape(1, -1)

    saved = []
    xc = xs
    for l in range(L):
        Wl = early_weights(l, gather_token if l == 0 else xc)
        Win, Wuq, Wukv = Wl["Win"], Wl["Wuq"], Wl["Wukv"]
        s = {"x": xc, "W": Wl}
        h1 = rms_fwd("norm1", xc, 0, D, vec(norm1_g[l]), BF16)
        P = mm_nn("in_proj", h1, Win, F32, b_lead=0)
        s.update(h1=h1, P=P)
        ff_off = lay.off["ff"]
        ft = P[:, ff_off:ff_off + FOX_HEADS].T.reshape(FOX_HEADS, R, LANES)
        bfl = jnp.broadcast_to(fox_b_f[l].reshape(FOX_HEADS, 1, 1), (FOX_HEADS, 1, LANES))
        kbias = fox_gate_fwd("fox_gate", ft, bfl).reshape(FOX_HEADS, S // AB, 1, AB)
        o_fox, lse_fox = attn_fwd("fox_attn", "fox", P, lay.cb("fq", LANES), P, lay.cb("fk", LANES),
                                  P, lay.cb("fv", LANES), FOX_HEADS, FOX_DH, fox_scale, kbias=kbias)
        s.update(ft=ft, bfl=bfl, kbias=kbias, o_fox=o_fox, lse_fox=lse_fox)
        cqn = rms_fwd("mla_q_norm", P, lay.cb("mq", MLA_Q_LORA), MLA_Q_LORA, vec(mla_q_norm_g[l]), BF16)
        qall = mm_nn("mla_uq", cqn, Wuq, F32)
        ckvn = rms_fwd("mla_kv_norm", P, lay.cb("mkv", MLA_KV_LORA), MLA_KV_LORA, vec(mla_kv_norm_g[l]), BF16)
        kvall = mm_nn("mla_ukv", ckvn, Wukv, F32)
        qrope = rope_apply("mla_q_rope", qall, NOPE_W // LANES, MLA_HEADS, tab64, 1.0, F32)
        krope = rope_apply("mla_k_rope", P, lay.cb("mkr", LANES), 1, tab64, 1.0, F32)
        o_mla, lse_mla = attn_fwd("mla_attn", "mla", qall, 0, kvall, 0, kvall, NOPE_W // MLA_V, MLA_HEADS, MLA_V,
                                  mla_scale, q2=qrope, q2_cb=0, k2=krope)
        s.update(cqn=cqn, qall=qall, ckvn=ckvn, kvall=kvall, qrope=qrope, krope=krope, o_mla=o_mla,
                 lse_mla=lse_mla)
        rq = rope_apply("ret_q_rope", P, lay.cb("rq", LANES), RET_HEADS, tab128, 1.0, F32)
        rk = rope_apply("ret_k_rope", P, lay.cb("rk", LANES), RET_HEADS, tab128, ret_kscale, F32)
        o_ret, _ = attn_fwd("ret_attn", "ret", rq, 0, rk, 0, P, lay.cb("rv", RET_DV), RET_HEADS, RET_DV, 1.0)
        c_ret = ret_out_fwd("ret_out", o_ret, P, lay.cb("rg", RET_DV), BF16)
        s.update(rq=rq, rk=rk, o_ret=o_ret, c_ret=c_ret)
        Wl.update(late_weights(l, (o_fox, o_mla, c_ret)))
        Wout, Wbf, Wbm, Wbr = Wl["Wout"], Wl["Wbf"], Wl["Wbm"], Wl["Wbr"]
        Wup, Wgate, Wdown, Wconv, cbias = (Wl[k] for k in ("Wup", "Wgate", "Wdown", "Wconv", "cbias"))
        A = mm_nn("br_fox", o_fox, Wbf, F32)
        Bm = mm_nn("br_mla", o_mla, Wbm, F32)
        C = mm_nn("br_ret", c_ret, Wbr, F32)
        merged = merge_fwd("merge", P, lay.cb("gates", D), A, Bm, C, BF16)
        x2 = mm_nn("out_proj", merged, Wout, F32, b_lead=0, res=xc)
        s.update(A=A, Bm=Bm, C=C, merged=merged, x2=x2)
        h2 = rms_fwd("norm2", x2, 0, D, vec(norm2_g[l]), BF16)
        u = ffn_up("ffn_up", h2, Wup, 0, F32)
        gt = ffn_up("ffn_gate", h2, Wgate, 0, F32)
        act = ffn_act_fwd("ffn_act", u, gt, Wconv, cbias, 0, BF16)
        xc = ffn_down("ffn_down", act, Wdown, 0, x2, F32)
        s.update(h2=h2, u=u, gt=gt, act=act)
        saved.append(s)

    loss_tile, dx, dgf = loss_head("loss_head", xc, vec(final_norm_g), tgt)
    loss = lax.psum(loss_tile[0, 0], ("x", "y", "c"))

    gbig = {n: [None] * L for n in BIG}
    gsmall = {n: [None] * L for n in SMALL if n != "final_norm_g"}
    scattering = {}
    scatter_token = None

    def start_scatter(name, names, l):
        srcs = [gbig[n][l] for n in names]
        *flight, tok = exchange_start(name, "scatter", srcs, landing_zones("scatter", srcs))
        return flight, tok

    for l in reversed(range(L)):
        s = saved[l]
        P = s["P"]
        Wl = s["W"]
        Win, Wout, Wuq, Wukv, Wbf, Wbm, Wbr = (Wl[k] for k in ("Win", "Wout", "Wuq", "Wukv", "Wbf", "Wbm", "Wbr"))
        Wup, Wgate, Wdown, Wconv, cbias = (Wl[k] for k in ("Wup", "Wgate", "Wdown", "Wconv", "cbias"))
        dxb = (dx if scatter_token is None else dx + scatter_token[0, 0]).astype(BF16)
        dact = ffn_down_bwd_act("ffn_down_da", dxb, Wdown, 0, BF16)
        gbig["ffn_w_down"][l] = ffn_down_bwd_w("ffn_down_dw", s["act"], dxb, BF16)
        g, dgt = ffn_act_bwd_point("ffn_act_bwd", s["u"], s["gt"], Wconv, cbias, 0, dact)
        du, dcw, dcb = ffn_act_bwd_conv("ffn_conv_bwd", s["u"], g, Wconv, 0)
        gbig["ffn_conv_w"][l] = dcw.astype(BF16)
        gsmall["ffn_conv_b"][l] = dcb.reshape(-1)
        gbig["ffn_w_up"][l] = ffn_down_bwd_w("ffn_up_dw", du, s["h2"], BF16)
        gbig["ffn_w_gate"][l] = ffn_down_bwd_w("ffn_gate_dw", dgt, s["h2"], BF16)
        dh2 = ffn_up_bwd_h("ffn_up_dh", du, Wup, 0, None, F32)
        dh2 = ffn_up_bwd_h("ffn_gate_dh", dgt, Wgate, 0, dh2, BF16)
        dx2, dg2 = rms_bwd("norm2_bwd", s["x2"], 0, D, vec(norm2_g[l]), dh2, F32, res=dx)
        gsmall["norm2_g"][l] = dg2.reshape(-1)
        scattering[l, "ffn"], scatter_token = start_scatter(f"scatter_start_{l}ffn", FFN, l)
        dx2b = (dx2 + scatter_token[0, 0]).astype(BF16)
        dmerged = mm_nt("out_proj_dm", dx2b, Wout, BF16, b_lead=0)
        gbig["w_out"][l] = mm_tn("out_proj_dw", s["merged"], dx2b, BF16).reshape(N_DEV, D // N_DEV, D)
        dgates, dA, dB, dC = merge_bwd("merge_bwd", P, lay.cb("gates", D), s["A"], s["Bm"], s["C"], dmerged)
        do_fox = mm_nt("br_fox_do", dA, Wbf, F32)
        do_mla = mm_nt("br_mla_do", dB, Wbm, F32)
        dc_ret = mm_nt("br_ret_do", dC, Wbr, BF16)
        gbig["w_br_fox"][l] = _cols_to_blocks(mm_tn("br_fox_dw", s["o_fox"], dA, BF16))
        gbig["w_br_mla"][l] = _cols_to_blocks(mm_tn("br_mla_dw", s["o_mla"], dB, BF16))
        gbig["w_br_ret"][l] = _cols_to_blocks(mm_tn("br_ret_dw", s["c_ret"], dC, BF16))
        do_ret, drg = ret_out_bwd("ret_out_bwd", s["o_ret"], P, lay.cb("rg", RET_DV), dc_ret)
        rb = attn_bwd("ret_attn_bwd", "ret", s["rq"], 0, s["rk"], 0, P, lay.cb("rv", RET_DV), RET_HEADS, RET_DV,
                      1.0, do_ret)
        drq = rope_apply("ret_q_rope_bwd", rb["dq1"], 0, RET_HEADS, tab128, 1.0, BF16, transpose=True)
        drk = rope_apply("ret_k_rope_bwd", rb["dk1"], 0, RET_HEADS, tab128, ret_kscale, BF16, transpose=True)
        drv = rb["dv"].astype(BF16)
        mb = attn_bwd("mla_attn_bwd", "mla", s["qall"], 0, s["kvall"], 0, s["kvall"], NOPE_W // MLA_V, MLA_HEADS,
                      MLA_V, mla_scale, do_mla, o=s["o_mla"], lse=s["lse_mla"], q2=s["qrope"], q2_cb=0,
                      k2=s["krope"])
        dqrope = rope_apply("mla_q_rope_bwd", mb["dq2"], 0, MLA_HEADS, tab64, 1.0, BF16, transpose=True)
        dkr_sum = head_sum("mla_k_rope_sum", mb["dk2h"], MLA_HEADS, F32)
        dmkr = rope_apply("mla_k_rope_bwd", dkr_sum, 0, 1, tab64, 1.0, BF16, transpose=True)
        dqall = jnp.concatenate([mb["dq1"].astype(BF16), dqrope], axis=1)
        dkvall = jnp.concatenate([mb["dk1"].astype(BF16), mb["dv"].astype(BF16)], axis=1)
        dcqn = mm_nt("mla_uq_dx", dqall, Wuq, F32)
        dckvn = mm_nt("mla_ukv_dx", dkvall, Wukv, F32)
        guq = _uq_unpermute(mm_tn("mla_uq_dw", s["cqn"], dqall, BF16))
        gukv = _ukv_unpermute(mm_tn("mla_ukv_dw", s["ckvn"], dkvall, BF16))
        gbig["mla_w_uq"][l] = _cols_to_blocks(guq)
        gbig["mla_w_ukv"][l] = _cols_to_blocks(gukv)
        dmq, dgq = rms_bwd("mla_q_norm_bwd", P, lay.cb("mq", MLA_Q_LORA), MLA_Q_LORA, vec(mla_q_norm_g[l]),
                           dcqn, BF16)
        dmkv, dgkv = rms_bwd("mla_kv_norm_bwd", P, lay.cb("mkv", MLA_KV_LORA), MLA_KV_LORA,
                             vec(mla_kv_norm_g[l]), dckvn, BF16)
        gsmall["mla_q_norm_g"][l] = dgq.reshape(-1)
        gsmall["mla_kv_norm_g"][l] = dgkv.reshape(-1)
        fb = attn_bwd("fox_attn_bwd", "fox", P, lay.cb("fq", LANES), P, lay.cb("fk", LANES), P,
                      lay.cb("fv", LANES), FOX_HEADS, FOX_DH, fox_scale, do_fox, o=s["o_fox"], lse=s["lse_fox"],
                      kbias=s["kbias"])
        drow = fb["drow"].reshape(S, FOX_HEADS, LANES)[:, :, 0].T.reshape(FOX_HEADS, R, LANES)
        dft, dbf = fox_gate_bwd("fox_gate_bwd", s["ft"], s["bfl"], fb["dkb"].reshape(FOX_HEADS, R, LANES) - drow)
        gsmall["fox_b_f"][l] = dbf[:, 0, 0]
        dff = jnp.pad(dft.reshape(FOX_HEADS, S).T, ((0, 0), (0, LANES - FOX_HEADS))).astype(BF16)
        segs = dict(gates=dgates, rv=drv, rg=drg, mq=dmq, rq=drq, rk=drk, mkv=dmkv, fq=fb["dq1"].astype(BF16),
                    fk=fb["dk1"].astype(BF16), fv=fb["dv"].astype(BF16), mkr=dmkr, ff=dff)
        dP = jnp.concatenate([segs[n] for n in lay.order], axis=1)
        gbig["w_in"][l] = mm_tn("in_proj_dw", s["h1"], dP, BF16).reshape(N_DEV, D // N_DEV, NP)
        scattering[l, "rest"], scatter_token = start_scatter(f"scatter_start_{l}rest", REST, l)
        dh1 = mm_nt("in_proj_dh", dP, Win, BF16, b_lead=0)
        dx, dg1 = rms_bwd("norm1_bwd", s["x"], 0, D, vec(norm1_g[l]) + scatter_token[0:1, 0:1], dh1, F32, res=dx2)
        gsmall["norm1_g"][l] = dg1.reshape(-1)

    small_like = {n: W[n] for n in SMALL}
    small_part = {n: jnp.stack(gsmall[n]) for n in gsmall}
    small_part["final_norm_g"] = dgf.reshape(-1)
    small_recv = exchange("gather_small_grads", [("gather", [_pack_small(small_part)])])[0]
    ps = adamw("adamw_small", _pack_small(small_like)[None], _pack_small({n: Mo[n] for n in SMALL})[None],
               _pack_small({n: Vo[n] for n in SMALL})[None], [small_recv[0]])

    def received(l, after):
        r = {}
        for grp, names in (("ffn", FFN), ("rest", REST)):
            r.update(zip(names, exchange_wait(f"scatter_wait_{l}{grp}", *scattering[l, grp], after)))
        r["w_in"] = lay.unpermute(reduce_parts("w_in_grad_sum", r["w_in"]))[None]
        return r

    def oriented(n, a):
        return jnp.swapaxes(a, 1, 2) if n in TRANSPOSED else a

    out = {}
    if L > 1:
        recv = [received(l, dx) for l in range(1, L)]
        for n in BIG:
            out[n] = adamw("adamw_" + n, oriented(n, W[n]), oriented(n, Mo[n]), oriented(n, Vo[n]),
                           [r[n] for r in recv], first=1)
    recv0 = received(0, out[BIG[-1]][0] if L > 1 else dx)
    for n in BIG:
        res = adamw("adamw0_" + n, oriented(n, W[n]), oriented(n, Mo[n]), oriented(n, Vo[n]), [recv0[n]],
                    first=0, prev=out.get(n))
        out[n] = tuple(oriented(n, a) for a in res)
    small_out = [_unpack_small(a[0], small_like) for a in ps]
    for n in SMALL:
        out[n] = tuple(so[n] for so in small_out)

    grads = [out[n][0] for n in WEIGHTS]
    deltas = [out[n][1] for n in WEIGHTS]
    new_m = [out[n][2] for n in WEIGHTS]
    new_v = [out[n][3] for n in WEIGHTS]
    return (loss, dx.reshape(1, S, D), *grads, *deltas, *new_m, *new_v)
```

```python
import functools
import math

import numpy as np
import jax
import jax.numpy as jnp
from jax import lax
from jax.experimental import pallas as pl
from jax.experimental.pallas import tpu as pltpu

F32 = jnp.float32
BF16 = jnp.bfloat16

CHUNK = 64
NORM_EPS = 1e-6
ROPE_THETA = 10000.0
FOX_HEADS, FOX_DH = 6, 128
FOX_W = FOX_HEADS * FOX_DH
MLA_HEADS, MLA_NOPE, MLA_ROPE, MLA_V = 6, 128, 64, 128
MLA_Q_LORA, MLA_KV_LORA = 512, 256
MLA_W = MLA_HEADS * MLA_V
RET_HEADS, RET_DK, RET_DV = 4, 128, 256
RET_QK_W, RET_V_W = RET_HEADS * RET_DK, RET_HEADS * RET_DV
ADAM_LR, ADAM_B1, ADAM_B2, ADAM_EPS, ADAM_WD, ADAM_STEP = 0.001, 0.9, 0.999, 1e-08, 0.01, 10

N_DEV = 8
LANES = 128
V7X_VMEM_LIMIT_BYTES = 52 * 1024 * 1024
NEG_BIG = -1e30
HIGHEST = lax.Precision.HIGHEST

NT_DIMS = (((1,), (1,)), ((), ()))
TN_DIMS = (((0,), (0,)), ((), ()))
NN_DIMS = (((1,), (0,)), ((), ()))


def _pick(n, cap, mult=LANES):
    best = None
    for t in range(mult, min(n, cap) + 1, mult):
        if n % t == 0:
            best = t
    return n if best is None else best


def _cparams(*sem):
    return pltpu.CompilerParams(dimension_semantics=sem, vmem_limit_bytes=V7X_VMEM_LIMIT_BYTES)


class InLayout:
    def __init__(self, d_model):
        d = d_model
        self.d = d
        orig = dict(fq=(0, FOX_W), fk=(FOX_W, FOX_W), fv=(2 * FOX_W, FOX_W), ff=(3 * FOX_W, FOX_HEADS))
        o = 3 * FOX_W + FOX_HEADS
        for name, w in (("mq", MLA_Q_LORA), ("mkv", MLA_KV_LORA), ("mkr", MLA_ROPE), ("rq", RET_QK_W),
                        ("rk", RET_QK_W), ("rv", RET_V_W), ("rg", RET_V_W), ("gates", 3 * d)):
            orig[name] = (o, w)
            o += w
        self.orig = orig
        self.orig_width = o
        order = ["rv", "rg", "mq", "rq", "rk", "mkv", "fq", "fk", "fv", "mkr", "ff"]
        self.order = order
        self.off, self.width = {}, {}
        p = 0
        for name in order:
            w = orig[name][1]
            wp = -(-w // LANES) * LANES
            self.off[name], self.width[name] = p, wp
            p += wp
        self.total = p
        self.gates = 3 * d

    def cb(self, name, block):
        assert self.off[name] % block == 0, (name, block)
        return self.off[name] // block

    def permute(self, w):
        parts = []
        for name in self.order:
            o, n = self.orig[name]
            seg = w[..., o:o + n]
            pad = self.width[name] - n
            if pad:
                seg = jnp.pad(seg, [(0, 0)] * (w.ndim - 1) + [(0, pad)])
            parts.append(seg)
        o, n = self.orig["gates"]
        return w[..., o:o + n], jnp.concatenate(parts, axis=-1)

    def unpermute(self, gates, mix):
        names = sorted(self.order, key=lambda n: self.orig[n][0])
        return jnp.concatenate([mix[..., self.off[n]:self.off[n] + self.orig[n][1]] for n in names] + [gates],
                               axis=-1)


def _mm(name, a, b, out_shape, grid, a_spec, b_spec, o_spec, dims, acc_shape, res=None, nsub=0):
    nk = grid[-1]
    has_res = res is not None

    def body(*refs):
        if has_res:
            a_ref, b_ref, r_ref, o_ref = refs[:4]
        else:
            a_ref, b_ref, o_ref = refs[:3]
            r_ref = None
        if nsub:
            prod = lax.dot_general(a_ref[0].astype(BF16), b_ref[0].astype(BF16), dims, preferred_element_type=F32)
            for q in range(1, nsub):
                prod = prod + lax.dot_general(a_ref[q].astype(BF16), b_ref[q].astype(BF16), dims,
                                              preferred_element_type=F32)
        else:
            prod = lax.dot_general(a_ref[...].astype(BF16), b_ref[...].astype(BF16), dims,
                                   preferred_element_type=F32)
        if nk == 1:
            if has_res:
                prod = prod + r_ref[...].astype(F32)
            o_ref[...] = prod.astype(o_ref.dtype)
        else:
            acc_ref = refs[-1]
            k = pl.program_id(len(grid) - 1)

            @pl.when(k == 0)
            def _():
                acc_ref[...] = prod

            @pl.when(k > 0)
            def _():
                acc_ref[...] += prod

            @pl.when(k == nk - 1)
            def _():
                r = acc_ref[...]
                if has_res:
                    r = r + r_ref[...].astype(F32)
                o_ref[...] = r.astype(o_ref.dtype)

    in_specs = [a_spec, b_spec] + ([o_spec] if has_res else [])
    args = (a, b) + ((res,) if has_res else ())
    scratch = [pltpu.VMEM(acc_shape, F32)] if nk > 1 else []
    sem = ("parallel",) * (len(grid) - 1) + ("arbitrary",)
    return pl.pallas_call(body, out_shape=out_shape, grid=grid, in_specs=in_specs, out_specs=o_spec,
                          scratch_shapes=scratch, name=name, compiler_params=_cparams(*sem))(*args)


def mm_nn(name, a, b, out_dtype, b_lead=None, res=None):
    M, K = a.shape
    N = b.shape[-1]
    tm, tn, tk = _pick(M, 1024, 8), _pick(N, 1024), _pick(K, 2048)
    grid = (M // tm, N // tn, K // tk)
    a_spec = pl.BlockSpec((tm, tk), lambda i, j, k: (i, k))
    if b_lead is None:
        b_spec = pl.BlockSpec((tk, tn), lambda i, j, k: (k, j))
    else:
        b_spec = pl.BlockSpec((None, tk, tn), lambda i, j, k: (b_lead, k, j))
    o_spec = pl.BlockSpec((tm, tn), lambda i, j, k: (i, j))
    return _mm(name, a, b, jax.ShapeDtypeStruct((M, N), out_dtype), grid, a_spec, b_spec, o_spec,
               NN_DIMS, (tm, tn), res)


def mm_nt(name, a, b, out_dtype, b_lead=None, res=None):
    M, N = a.shape
    K = b.shape[-2]
    tm, tko, tk = _pick(M, 1024, 8), _pick(K, 1024), _pick(N, 2048)
    grid = (M // tm, K // tko, N // tk)
    a_spec = pl.BlockSpec((tm, tk), lambda i, j, k: (i, k))
    if b_lead is None:
        b_spec = pl.BlockSpec((tko, tk), lambda i, j, k: (j, k))
    else:
        b_spec = pl.BlockSpec((None, tko, tk), lambda i, j, k: (b_lead, j, k))
    o_spec = pl.BlockSpec((tm, tko), lambda i, j, k: (i, j))
    return _mm(name, a, b, jax.ShapeDtypeStruct((M, K), out_dtype), grid, a_spec, b_spec, o_spec,
               NT_DIMS, (tm, tko), res)


def mm_tn(name, a, b, out_dtype):
    M, K = a.shape
    N = b.shape[-1]
    cap = 4096 if (a.dtype == BF16 and b.dtype == BF16) else 2048
    tko, tn, tk = _pick(K, 1024), _pick(N, 1024), _pick(M, cap, 8)
    grid = (K // tko, N // tn, M // tk)
    a_spec = pl.BlockSpec((tk, tko), lambda i, j, k: (k, i))
    b_spec = pl.BlockSpec((tk, tn), lambda i, j, k: (k, j))
    o_spec = pl.BlockSpec((tko, tn), lambda i, j, k: (i, j))
    return _mm(name, a, b, jax.ShapeDtypeStruct((K, N), out_dtype), grid, a_spec, b_spec, o_spec,
               TN_DIMS, (tko, tn))


FFN_SUB = 4
def ffn_up(name, h, w, l, out_dtype):
    M, D = h.shape
    f = w.shape[-1]
    tm = _pick(M, 1024, 8)
    grid = (M // tm, N_DEV, 1)
    return _mm(name, h, w, jax.ShapeDtypeStruct((N_DEV, M, f), out_dtype), grid,
               pl.BlockSpec((tm, D), lambda i, j, k: (i, 0)),
               pl.BlockSpec((None, None, D, f), lambda i, j, k: (l, j, 0, 0)),
               pl.BlockSpec((None, tm, f), lambda i, j, k: (j, i, 0)), NN_DIMS, (tm, f))


def ffn_down(name, act, w, l, res, out_dtype):
    _, M, f = act.shape
    D = w.shape[-1]
    tm, tn = _pick(M, 1024, 8), _pick(D, 1024)
    grid = (M // tm, D // tn, N_DEV // FFN_SUB)
    return _mm(name, act, w, jax.ShapeDtypeStruct((M, D), out_dtype), grid,
               pl.BlockSpec((FFN_SUB, tm, f), lambda i, j, k: (k, i, 0)),
               pl.BlockSpec((None, FFN_SUB, f, tn), lambda i, j, k: (l, k, 0, j)),
               pl.BlockSpec((tm, tn), lambda i, j, k: (i, j)), NN_DIMS, (tm, tn), res, nsub=FFN_SUB)


def ffn_down_bwd_act(name, dy, w, l, out_dtype):
    M, D = dy.shape
    f = w.shape[-2]
    tm = _pick(M, 1024, 8)
    grid = (M // tm, N_DEV, 1)
    return _mm(name, dy, w, jax.ShapeDtypeStruct((N_DEV, M, f), out_dtype), grid,
               pl.BlockSpec((tm, D), lambda i, j, k: (i, 0)),
               pl.BlockSpec((None, None, f, D), lambda i, j, k: (l, j, 0, 0)),
               pl.BlockSpec((None, tm, f), lambda i, j, k: (j, i, 0)), NT_DIMS, (tm, f))


def ffn_down_bwd_w(name, act, dy, out_dtype):
    _, M, f = act.shape
    D = dy.shape[-1]
    tn, tk = _pick(D, 1024), _pick(M, 4096 if dy.dtype == BF16 else 2048, 8)
    grid = (N_DEV, D // tn, M // tk)
    return _mm(name, act, dy, jax.ShapeDtypeStruct((N_DEV, f, D), out_dtype), grid,
               pl.BlockSpec((None, tk, f), lambda j, n, k: (j, k, 0)),
               pl.BlockSpec((tk, tn), lambda j, n, k: (k, n)),
               pl.BlockSpec((None, f, tn), lambda j, n, k: (j, 0, n)), TN_DIMS, (f, tn))


def ffn_up_bwd_h(name, du, w, l, res, out_dtype):
    _, M, f = du.shape
    D = w.shape[-2]
    tm, tn = _pick(M, 1024, 8), _pick(D, 1024)
    grid = (M // tm, D // tn, N_DEV // FFN_SUB)
    return _mm(name, du, w, jax.ShapeDtypeStruct((M, D), out_dtype), grid,
               pl.BlockSpec((FFN_SUB, tm, f), lambda i, j, k: (k, i, 0)),
               pl.BlockSpec((None, FFN_SUB, tn, f), lambda i, j, k: (l, k, j, 0)),
               pl.BlockSpec((tm, tn), lambda i, j, k: (i, j)), NT_DIMS, (tm, tn), res, nsub=FFN_SUB)


def _rms(xf, g):
    return xf * lax.rsqrt(jnp.mean(xf * xf, axis=-1, keepdims=True) + NORM_EPS) * g


def rms_fwd(name, x, cb, W, g, out_dtype):
    S = x.shape[0]
    tr = _pick(S, 256, 8)

    def body(x_ref, g_ref, o_ref):
        o_ref[...] = _rms(x_ref[...].astype(F32), g_ref[...]).astype(o_ref.dtype)

    return pl.pallas_call(
        body, out_shape=jax.ShapeDtypeStruct((S, W), out_dtype), grid=(S // tr,),
        in_specs=[pl.BlockSpec((tr, W), lambda i: (i, cb)), pl.BlockSpec((1, W), lambda i: (0, 0))],
        out_specs=pl.BlockSpec((tr, W), lambda i: (i, 0)), name=name, compiler_params=_cparams("parallel"))(x, g)


def rms_bwd(name, x, cb, W, g, dy, out_dtype, res=None):
    S = x.shape[0]
    tr = _pick(S, 256, 8)
    has_res = res is not None

    def body(*refs):
        if has_res:
            x_ref, g_ref, dy_ref, r_ref, dx_ref, dg_ref = refs
        else:
            x_ref, g_ref, dy_ref, dx_ref, dg_ref = refs
        _, vjp = jax.vjp(_rms, x_ref[...].astype(F32), g_ref[...])
        dx, dg = vjp(dy_ref[...].astype(F32))
        if has_res:
            dx = dx + r_ref[...]
        dx_ref[...] = dx.astype(dx_ref.dtype)

        @pl.when(pl.program_id(0) == 0)
        def _():
            dg_ref[...] = jnp.zeros_like(dg_ref)

        dg_ref[...] += dg

    row = pl.BlockSpec((tr, W), lambda i: (i, 0))
    vec = pl.BlockSpec((1, W), lambda i: (0, 0))
    in_specs = [pl.BlockSpec((tr, W), lambda i: (i, cb)), vec, row] + ([row] if has_res else [])
    args = (x, g, dy) + ((res,) if has_res else ())
    return pl.pallas_call(
        body, out_shape=(jax.ShapeDtypeStruct((S, W), out_dtype), jax.ShapeDtypeStruct((1, W), F32)),
        grid=(S // tr,), in_specs=in_specs, out_specs=(row, vec), name=name,
        compiler_params=_cparams("arbitrary"))(*args)


def rope_tables(S, d):
    pos = jnp.arange(S, dtype=F32)
    inv_freq = ROPE_THETA ** (-jnp.arange(0, d, 2, dtype=F32) / d)
    ang = pos[:, None] * inv_freq[None, :]
    cos, sin = jnp.cos(ang), jnp.sin(ang)
    half = d // 2
    z = jnp.zeros((S, LANES - d), F32)
    zh = jnp.zeros((S, half), F32)
    c = jnp.concatenate([cos, cos, z], axis=1)
    sa = jnp.concatenate([-sin, zh, z], axis=1)
    sb = jnp.concatenate([zh, sin, z], axis=1)
    return c, sa, sb, half


def rope_apply(name, x, cb, H, tabs, scale, out_dtype, transpose=False):
    c, sa, sb, half = tabs
    S = x.shape[0]
    tr = _pick(S, 512, 8)
    up, down = LANES - half, half

    def body(x_ref, c_ref, sa_ref, sb_ref, o_ref):
        xv = x_ref[...].astype(F32)
        if not transpose:
            y = xv * c_ref[...] + pltpu.roll(xv, up, 1) * sa_ref[...] + pltpu.roll(xv, down, 1) * sb_ref[...]
            y = y * scale
        else:
            xv = xv * scale
            y = (xv * c_ref[...] + pltpu.roll(xv * sa_ref[...], down, 1)
                 + pltpu.roll(xv * sb_ref[...], up, 1))
        o_ref[...] = y.astype(o_ref.dtype)

    tab = pl.BlockSpec((tr, LANES), lambda h, i: (i, 0))
    return pl.pallas_call(
        body, out_shape=jax.ShapeDtypeStruct((S, H * LANES), out_dtype), grid=(H, S // tr),
        in_specs=[pl.BlockSpec((tr, LANES), lambda h, i: (i, cb + h)), tab, tab, tab],
        out_specs=pl.BlockSpec((tr, LANES), lambda h, i: (i, h)), name=name,
        compiler_params=_cparams("parallel", "parallel"))(x, c, sa, sb)


def _ret_out(o, g):
    y = o * lax.rsqrt(jnp.mean(o * o, axis=-1, keepdims=True) + NORM_EPS)
    return y * jax.nn.silu(g)


def ret_out_fwd(name, o, gsrc, g_cb, out_dtype):
    S = o.shape[0]
    tr = _pick(S, 512, 8)
    W = RET_DV

    def body(o_ref, g_ref, y_ref):
        y_ref[...] = _ret_out(o_ref[...], g_ref[...].astype(F32)).astype(y_ref.dtype)

    blk = pl.BlockSpec((tr, W), lambda h, i: (i, h))
    return pl.pallas_call(
        body, out_shape=jax.ShapeDtypeStruct((S, RET_HEADS * W), out_dtype), grid=(RET_HEADS, S // tr),
        in_specs=[blk, pl.BlockSpec((tr, W), lambda h, i: (i, g_cb + h))], out_specs=blk, name=name,
        compiler_params=_cparams("parallel", "parallel"))(o, gsrc)


def ret_out_bwd(name, o, gsrc, g_cb, dy):
    S = o.shape[0]
    tr = _pick(S, 512, 8)
    W = RET_DV

    def body(o_ref, g_ref, dy_ref, do_ref, dg_ref):
        _, vjp = jax.vjp(_ret_out, o_ref[...], g_ref[...].astype(F32))
        do, dg = vjp(dy_ref[...].astype(F32))
        do_ref[...] = do.astype(do_ref.dtype)
        dg_ref[...] = dg.astype(dg_ref.dtype)

    blk = pl.BlockSpec((tr, W), lambda h, i: (i, h))
    return pl.pallas_call(
        body, out_shape=(jax.ShapeDtypeStruct((S, RET_HEADS * W), F32),
                         jax.ShapeDtypeStruct((S, RET_HEADS * W), BF16)),
        grid=(RET_HEADS, S // tr),
        in_specs=[blk, pl.BlockSpec((tr, W), lambda h, i: (i, g_cb + h)), blk], out_specs=(blk, blk),
        name=name, compiler_params=_cparams("parallel", "parallel"))(o, gsrc, dy)


def _merge(g0, g1, g2, a, b, c):
    return jax.nn.sigmoid(g0) * a + jax.nn.sigmoid(g1) * b + jax.nn.sigmoid(g2) * c


def merge_fwd(name, P, gates_cb, a, b, c, out_dtype):
    S, D = a.shape
    tr = _pick(S, 128, 8)

    def body(g0, g1, g2, a_ref, b_ref, c_ref, o_ref):
        o_ref[...] = _merge(g0[...], g1[...], g2[...], a_ref[...], b_ref[...], c_ref[...]).astype(o_ref.dtype)

    row = pl.BlockSpec((tr, D), lambda i: (i, 0))
    gs = [pl.BlockSpec((tr, D), lambda i, k=k: (i, gates_cb + k)) for k in range(3)]
    return pl.pallas_call(
        body, out_shape=jax.ShapeDtypeStruct((S, D), out_dtype), grid=(S // tr,),
        in_specs=gs + [row, row, row], out_specs=row, name=name,
        compiler_params=_cparams("parallel"))(P, P, P, a, b, c)


def merge_bwd(name, P, gates_cb, a, b, c, dm):
    S, D = a.shape
    tr = _pick(S, 128, 8)

    def body(g0, g1, g2, a_ref, b_ref, c_ref, dm_ref, dg_ref, da_ref, db_ref, dc_ref):
        _, vjp = jax.vjp(_merge, g0[...], g1[...], g2[...], a_ref[...], b_ref[...], c_ref[...])
        d0, d1, d2, da, db, dc = vjp(dm_ref[...].astype(F32))
        dg_ref[:, 0:D] = d0.astype(dg_ref.dtype)
        dg_ref[:, D:2 * D] = d1.astype(dg_ref.dtype)
        dg_ref[:, 2 * D:3 * D] = d2.astype(dg_ref.dtype)
        da_ref[...] = da.astype(da_ref.dtype)
        db_ref[...] = db.astype(db_ref.dtype)
        dc_ref[...] = dc.astype(dc_ref.dtype)

    row = pl.BlockSpec((tr, D), lambda i: (i, 0))
    gs = [pl.BlockSpec((tr, D), lambda i, k=k: (i, gates_cb + k)) for k in range(3)]
    bf = jax.ShapeDtypeStruct((S, D), BF16)
    return pl.pallas_call(
        body, out_shape=(jax.ShapeDtypeStruct((S, 3 * D), BF16), bf, bf, bf), grid=(S // tr,),
        in_specs=gs + [row, row, row, row],
        out_specs=(pl.BlockSpec((tr, 3 * D), lambda i: (i, 0)), row, row, row), name=name,
        compiler_params=_cparams("parallel"))(P, P, P, a, b, c, dm)


HALO = 8


CONV_CHUNK = 32


def _shifted_back(u_ref, uh_ref, ext_ref, s1_ref, s2_ref, tr):
    ext_ref[0:HALO, :] = jnp.where(pl.program_id(1) > 0, uh_ref[...], 0.0)
    ext_ref[HALO:HALO + tr, :] = u_ref[...]
    s1_ref[...] = ext_ref[HALO - 1:HALO - 1 + tr, :]
    s2_ref[...] = ext_ref[HALO - 2:HALO - 2 + tr, :]


def _conv3(cw, cb, u, u1, u2):
    return cb + ((cw[0:1, :] * u2 + cw[1:2, :] * u1) + cw[2:3, :] * u)


def _chunks(tr, fn):
    def step(c, carry):
        return fn(pl.ds(pl.multiple_of(c * CONV_CHUNK, CONV_CHUNK), CONV_CHUNK), carry)
    return step


def _ffn_specs(S, f, tr, l):
    nb = tr // HALO
    row = pl.BlockSpec((None, tr, f), lambda j, i: (j, i, 0))
    prev = pl.BlockSpec((None, HALO, f), lambda j, i: (j, jnp.maximum(i * nb - 1, 0), 0))
    nxt = pl.BlockSpec((None, HALO, f), lambda j, i: (j, jnp.minimum((i + 1) * nb, S // HALO - 1), 0))
    cw = pl.BlockSpec((None, None, 3, f), lambda j, i: (l, j, 0, 0))
    cb = pl.BlockSpec((None, None, 1, f), lambda j, i: (l, j, 0, 0))
    return row, prev, nxt, cw, cb


def ffn_act_fwd(name, u, gt, cw, cb, l, out_dtype):
    _, S, f = u.shape
    tr = _pick(S, 512, 8)
    row, prev, _, cws, cbs = _ffn_specs(S, f, tr, l)

    def body(u_ref, uh_ref, gt_ref, cw_ref, cb_ref, o_ref, ext_ref, s1_ref, s2_ref):
        _shifted_back(u_ref, uh_ref, ext_ref, s1_ref, s2_ref, tr)
        cwv, cbv = cw_ref[...], cb_ref[...]

        def chunk(rows, carry):
            uc = _conv3(cwv, cbv, u_ref[rows, :], s1_ref[rows, :], s2_ref[rows, :])
            o_ref[rows, :] = (jax.nn.gelu(uc) * gt_ref[rows, :]).astype(o_ref.dtype)
            return carry

        lax.fori_loop(0, tr // CONV_CHUNK, _chunks(tr, chunk), 0)

    return pl.pallas_call(
        body, out_shape=jax.ShapeDtypeStruct((N_DEV, S, f), out_dtype), grid=(N_DEV, S // tr),
        in_specs=[row, prev, row, cws, cbs], out_specs=row,
        scratch_shapes=[pltpu.VMEM((tr + HALO, f), F32), pltpu.VMEM((tr, f), F32), pltpu.VMEM((tr, f), F32)],
        name=name, compiler_params=_cparams("parallel", "parallel"))(u, u, gt, cw, cb)


def ffn_act_bwd_point(name, u, gt, cw, cb, l, dact):
    _, S, f = u.shape
    tr = _pick(S, 512, 8)
    row, prev, _, cws, cbs = _ffn_specs(S, f, tr, l)

    def body(u_ref, uh_ref, gt_ref, cw_ref, cb_ref, da_ref, g_ref, dgt_ref, ext_ref, s1_ref, s2_ref):
        _shifted_back(u_ref, uh_ref, ext_ref, s1_ref, s2_ref, tr)
        cwv, cbv = cw_ref[...], cb_ref[...]

        def chunk(rows, carry):
            uc = _conv3(cwv, cbv, u_ref[rows, :], s1_ref[rows, :], s2_ref[rows, :])
            _, vjp = jax.vjp(lambda c, t: jax.nn.gelu(c) * t, uc, gt_ref[rows, :])
            g, dgt = vjp(da_ref[rows, :].astype(F32))
            g_ref[rows, :] = g
            dgt_ref[rows, :] = dgt.astype(dgt_ref.dtype)
            return carry

        lax.fori_loop(0, tr // CONV_CHUNK, _chunks(tr, chunk), 0)

    return pl.pallas_call(
        body, out_shape=(jax.ShapeDtypeStruct((N_DEV, S, f), F32), jax.ShapeDtypeStruct((N_DEV, S, f), BF16)),
        grid=(N_DEV, S // tr), in_specs=[row, prev, row, cws, cbs, row], out_specs=(row, row),
        scratch_shapes=[pltpu.VMEM((tr + HALO, f), F32), pltpu.VMEM((tr, f), F32), pltpu.VMEM((tr, f), F32)],
        name=name, compiler_params=_cparams("parallel", "parallel"))(u, u, gt, cw, cb, dact)


def ffn_act_bwd_conv(name, u, g, cw, l):
    _, S, f = u.shape
    tr = _pick(S, 512, 8)
    nt = S // tr
    row, prev, nxt, cws, _ = _ffn_specs(S, f, tr, l)

    def body(u_ref, uh_ref, g_ref, gn_ref, cw_ref, du_ref, dcw_ref, dcb_ref, ext_ref, s1_ref, s2_ref, n1_ref, n2_ref):
        i = pl.program_id(1)
        _shifted_back(u_ref, uh_ref, ext_ref, s1_ref, s2_ref, tr)
        ext_ref[0:tr, :] = g_ref[...]
        ext_ref[tr:tr + HALO, :] = jnp.where(i < nt - 1, gn_ref[...], 0.0)
        n1_ref[...] = ext_ref[1:1 + tr, :]
        n2_ref[...] = ext_ref[2:2 + tr, :]
        cw = cw_ref[...]

        def chunk(rows, carry):
            d0, d1, d2, db = carry
            g = g_ref[rows, :]
            du_ref[rows, :] = (cw[2:3, :] * g + cw[1:2, :] * n1_ref[rows, :]
                               + cw[0:1, :] * n2_ref[rows, :]).astype(du_ref.dtype)
            return (d0 + jnp.sum(g * s2_ref[rows, :], axis=0, keepdims=True),
                    d1 + jnp.sum(g * s1_ref[rows, :], axis=0, keepdims=True),
                    d2 + jnp.sum(g * u_ref[rows, :], axis=0, keepdims=True),
                    db + jnp.sum(g, axis=0, keepdims=True))

        z = jnp.zeros((1, f), F32)
        d0, d1, d2, db = lax.fori_loop(0, tr // CONV_CHUNK, _chunks(tr, chunk), (z, z, z, z))

        @pl.when(i == 0)
        def _():
            dcw_ref[...] = jnp.zeros_like(dcw_ref)
            dcb_ref[...] = jnp.zeros_like(dcb_ref)

        dcw_ref[0:1, :] += d0
        dcw_ref[1:2, :] += d1
        dcw_ref[2:3, :] += d2
        dcb_ref[...] += db

    tile = pltpu.VMEM((tr, f), F32)
    return pl.pallas_call(
        body, out_shape=(jax.ShapeDtypeStruct((N_DEV, S, f), BF16), jax.ShapeDtypeStruct((N_DEV, 3, f), F32),
                         jax.ShapeDtypeStruct((N_DEV, 1, f), F32)),
        grid=(N_DEV, nt), in_specs=[row, prev, row, nxt, cws],
        out_specs=(row, pl.BlockSpec((None, 3, f), lambda j, i: (j, 0, 0)),
                   pl.BlockSpec((None, 1, f), lambda j, i: (j, 0, 0))),
        scratch_shapes=[pltpu.VMEM((tr + HALO, f), F32), tile, tile, tile, tile], name=name,
        compiler_params=_cparams("parallel", "arbitrary"))(u, u, g, g, cw)


def loss_head(name, x, g, tgt):
    S, D = x.shape
    tr = _pick(S, 256, 8)

    def body(x_ref, g_ref, t_ref, l_ref, dx_ref, dg_ref):
        tg = t_ref[...]

        def f(xv, gv):
            err = jnp.square(_rms(xv, gv) - tg)
            return 0.5 * jnp.sum(jnp.mean(err, axis=-1))

        val, vjp = jax.vjp(f, x_ref[...], g_ref[...])
        dx, dg = vjp(jnp.ones((), F32))
        dx_ref[...] = dx

        @pl.when(pl.program_id(0) == 0)
        def _():
            l_ref[...] = jnp.zeros_like(l_ref)
            dg_ref[...] = jnp.zeros_like(dg_ref)

        l_ref[...] += val
        dg_ref[...] += dg

    row = pl.BlockSpec((tr, D), lambda i: (i, 0))
    vec = pl.BlockSpec((1, D), lambda i: (0, 0))
    lt = pl.BlockSpec((8, LANES), lambda i: (0, 0))
    return pl.pallas_call(
        body, out_shape=(jax.ShapeDtypeStruct((8, LANES), F32), jax.ShapeDtypeStruct((S, D), F32),
                         jax.ShapeDtypeStruct((1, D), F32)),
        grid=(S // tr,), in_specs=[row, vec, row], out_specs=(lt, row, vec), name=name,
        compiler_params=_cparams("arbitrary"))(x, g, tgt)


def _tri(n, fn):
    r = lax.broadcasted_iota(jnp.int32, (n, n), 0)
    c = lax.broadcasted_iota(jnp.int32, (n, n), 1)
    return jnp.where(fn(r, c), 1.0, 0.0).astype(F32)


def _log_sigmoid(z):
    return jnp.minimum(z, 0.0) - jnp.log1p(jnp.exp(-jnp.abs(z)))


def fox_gate_fwd(name, ft, b):
    H, R, _ = ft.shape

    def body(f_ref, b_ref, o_ref):
        ls = _log_sigmoid(f_ref[...] + b_ref[...])
        cum = jnp.dot(ls, _tri(LANES, lambda r, c: r <= c), precision=HIGHEST, preferred_element_type=F32)
        tot = jnp.broadcast_to(cum[:, LANES - 1:LANES], (R, LANES))
        off = jnp.dot(_tri(R, lambda r, c: r > c), tot, precision=HIGHEST, preferred_element_type=F32)
        o_ref[...] = -(cum + off)

    blk = pl.BlockSpec((None, R, LANES), lambda h: (h, 0, 0))
    return pl.pallas_call(
        body, out_shape=jax.ShapeDtypeStruct((H, R, LANES), F32), grid=(H,),
        in_specs=[blk, pl.BlockSpec((None, 1, LANES), lambda h: (h, 0, 0))], out_specs=blk, name=name,
        compiler_params=_cparams("parallel"))(ft, b)


def fox_gate_bwd(name, ft, b, dkb):
    H, R, _ = ft.shape

    def body(f_ref, b_ref, d_ref, df_ref, db_ref):
        z = f_ref[...] + b_ref[...]
        d = d_ref[...]
        rev = jnp.dot(d, _tri(LANES, lambda r, c: r >= c), precision=HIGHEST, preferred_element_type=F32)
        tot = jnp.broadcast_to(rev[:, 0:1], (R, LANES))
        off = jnp.dot(_tri(R, lambda r, c: r < c), tot, precision=HIGHEST, preferred_element_type=F32)
        dls = -(rev + off)
        dz = dls * jax.nn.sigmoid(-z)
        df_ref[...] = dz
        s = jnp.sum(jnp.sum(dz, axis=1, keepdims=True), axis=0, keepdims=True)
        db_ref[...] = jnp.broadcast_to(s, (1, LANES))

    blk = pl.BlockSpec((None, R, LANES), lambda h: (h, 0, 0))
    vec = pl.BlockSpec((None, 1, LANES), lambda h: (h, 0, 0))
    return pl.pallas_call(
        body, out_shape=(jax.ShapeDtypeStruct((H, R, LANES), F32), jax.ShapeDtypeStruct((H, 1, LANES), F32)),
        grid=(H,), in_specs=[blk, vec, blk], out_specs=(blk, vec), name=name,
        compiler_params=_cparams("parallel"))(ft, b, dkb)


def _ret_log_gamma(h):
    lg = [float(np.log(np.float32(1.0) - np.float32(2.0) ** np.float32(-5.0 - i))) for i in range(RET_HEADS)]
    out = jnp.float32(lg[RET_HEADS - 1])
    for i in range(RET_HEADS - 2, -1, -1):
        out = jnp.where(h == i, jnp.float32(lg[i]), out)
    return out


def _visible(mode, B):
    r = lax.broadcasted_iota(jnp.int32, (B, B), 0)
    c = lax.broadcasted_iota(jnp.int32, (B, B), 1)
    if mode == "fox":
        return c <= r
    return (c // CHUNK) <= (r // CHUNK)


def _decay(lg, B, blocks_apart):
    r = lax.broadcasted_iota(jnp.int32, (B, B), 0)
    c = lax.broadcasted_iota(jnp.int32, (B, B), 1)
    dist = jnp.abs(r - c + blocks_apart * B).astype(F32)
    return jnp.exp(lg * dist)


def _attn_block(S):
    return 512 if S >= 2048 else 128


def attn_fwd(name, mode, q1, q1_cb, k1, k1_cb, v, v_cb, H, dv, scale, q2=None, q2_cb=0, k2=None, kbias=None):
    S = q1.shape[0]
    B = _attn_block(S)
    nq = S // B
    softmax = mode != "ret"
    two = mode == "mla"
    has_bias = mode == "fox"

    def body(*refs):
        it = iter(refs)
        q1_ref, k1_ref, v_ref = next(it), next(it), next(it)
        q2_ref = next(it) if two else None
        k2_ref = next(it) if two else None
        kb_ref = next(it) if has_bias else None
        o_ref = next(it)
        lse_ref = next(it) if softmax else None
        kbuf, vbuf = next(it), next(it)
        acc = next(it)
        m_ref = next(it) if softmax else None
        l_ref = next(it) if softmax else None
        s_all = next(it) if softmax else None
        h = pl.program_id(0)
        i = pl.program_id(1)

        @pl.when(i == 0)
        def _():
            kbuf[:, 0:LANES] = k1_ref[...].astype(BF16)
            vbuf[...] = v_ref[...].astype(BF16)
            if two:
                kbuf[:, LANES:2 * LANES] = k2_ref[...].astype(BF16)

        qb = q1_ref[...].astype(BF16)
        if two:
            qb = jnp.concatenate([qb, q2_ref[...].astype(BF16)], axis=1)
        lg = _ret_log_gamma(h) if mode == "ret" else None
        acc[...] = jnp.zeros_like(acc)
        if softmax:
            m_ref[...] = jnp.full_like(m_ref, NEG_BIG)
            l_ref[...] = jnp.zeros_like(l_ref)

        def scores(g, diag):
            rows = slice(g * B, (g + 1) * B)
            s = lax.dot_general(qb, kbuf[rows, :], NT_DIMS, preferred_element_type=F32)
            if softmax:
                s = s * scale
                if has_bias:
                    s = s + kb_ref[g]
                if diag:
                    s = jnp.where(_visible(mode, B), s, NEG_BIG)
                s_all[:, rows] = s
                m_ref[...] = jnp.maximum(m_ref[...], jnp.max(s, axis=1, keepdims=True))
            else:
                if diag:
                    p = jnp.where(_visible(mode, B), s * _decay(lg, B, 0), 0.0)
                else:
                    p = s * _decay(lg, B, i - g)
                acc[...] += jnp.dot(p.astype(BF16), vbuf[rows, :], preferred_element_type=F32)

        def weighted(g):
            rows = slice(g * B, (g + 1) * B)
            p = jnp.exp(s_all[:, rows] - m_ref[...])
            l_ref[...] += jnp.sum(p, axis=1, keepdims=True)
            acc[...] += jnp.dot(p.astype(BF16), vbuf[rows, :], preferred_element_type=F32)

        for g in range(nq):
            pl.when(g < i)(functools.partial(scores, g, False))
            pl.when(g == i)(functools.partial(scores, g, True))
        if softmax:
            for g in range(nq):
                pl.when(g <= i)(functools.partial(weighted, g))
            o_ref[...] = acc[...] / l_ref[...]
            lse_ref[...] = jnp.broadcast_to(m_ref[...] + jnp.log(l_ref[...]), (B, LANES))
        else:
            o_ref[...] = acc[...]

    in_specs = [pl.BlockSpec((B, LANES), lambda h, i: (i, q1_cb + h)),
                pl.BlockSpec((S, LANES), lambda h, i: (0, k1_cb + h)),
                pl.BlockSpec((S, dv), lambda h, i: (0, v_cb + h))]
    args = [q1, k1, v]
    if two:
        in_specs += [pl.BlockSpec((B, LANES), lambda h, i: (i, q2_cb + h)),
                     pl.BlockSpec((S, LANES), lambda h, i: (0, 0))]
        args += [q2, k2]
    if has_bias:
        in_specs.append(pl.BlockSpec((None, nq, 1, B), lambda h, i: (h, 0, 0, 0)))
        args.append(kbias)
    out_shape = [jax.ShapeDtypeStruct((S, H * dv), F32)]
    out_specs = [pl.BlockSpec((B, dv), lambda h, i: (i, h))]
    if softmax:
        out_shape.append(jax.ShapeDtypeStruct((S, H * LANES), F32))
        out_specs.append(pl.BlockSpec((B, LANES), lambda h, i: (i, h)))
    kw = 2 * LANES if two else LANES
    scratch = [pltpu.VMEM((S, kw), BF16), pltpu.VMEM((S, dv), BF16), pltpu.VMEM((B, dv), F32)]
    if softmax:
        scratch += [pltpu.VMEM((B, 1), F32), pltpu.VMEM((B, 1), F32), pltpu.VMEM((B, S), F32)]
    res = pl.pallas_call(body, out_shape=tuple(out_shape), grid=(H, nq), in_specs=in_specs,
                         out_specs=tuple(out_specs), scratch_shapes=scratch, name=name,
                         compiler_params=_cparams("parallel", "arbitrary"))(*args)
    return res if softmax else (res[0], None)


def attn_bwd(name, mode, q1, q1_cb, k1, k1_cb, v, v_cb, H, dv, scale, do, o=None, lse=None,
             q2=None, q2_cb=0, k2=None, kbias=None):
    S = q1.shape[0]
    B = _attn_block(S)
    nb = S // B
    softmax = mode != "ret"
    two = mode == "mla"
    has_bias = mode == "fox"

    def body(*refs):
        it = iter(refs)
        q1_ref, k1_ref, v_ref, do_ref = next(it), next(it), next(it), next(it)
        o_ref = next(it) if softmax else None
        lse_ref = next(it) if softmax else None
        q2_ref = next(it) if two else None
        k2_ref = next(it) if two else None
        kb_ref = next(it) if has_bias else None
        dq1_ref, dk1_ref, dv_ref = next(it), next(it), next(it)
        dq2_ref = next(it) if two else None
        dk2_ref = next(it) if two else None
        dkb_ref = next(it) if has_bias else None
        drow_ref = next(it) if has_bias else None
        qbuf, dobuf = next(it), next(it)
        qT, doT = next(it), next(it)
        delta = next(it) if softmax else None
        dk_acc, dv_acc = next(it), next(it)
        dkb_acc = next(it) if has_bias else None
        h = pl.program_id(0)
        j = pl.program_id(1)

        @pl.when(j == 0)
        def _():
            qbuf[:, 0:LANES] = q1_ref[...].astype(BF16)
            dobuf[...] = do_ref[...].astype(BF16)
            qT[0:LANES, :] = q1_ref[...].astype(F32).T.astype(BF16)
            doT[...] = do_ref[...].astype(F32).T.astype(BF16)
            dq1_ref[...] = jnp.zeros_like(dq1_ref)
            if has_bias:
                drow_ref[...] = jnp.zeros_like(drow_ref)
            if two:
                qbuf[:, LANES:2 * LANES] = q2_ref[...].astype(BF16)
                qT[LANES:2 * LANES, :] = q2_ref[...].astype(F32).T.astype(BF16)
                dq2_ref[...] = jnp.zeros_like(dq2_ref)
            if softmax:
                def drow(t, carry):
                    rows = pl.ds(pl.multiple_of(t * B, B), B)
                    delta[rows, :] = jnp.sum(do_ref[rows, :].astype(F32) * o_ref[rows, :], axis=1, keepdims=True)
                    return carry
                lax.fori_loop(0, nb, drow, 0)

        kj = k1_ref[...].astype(BF16)
        if two:
            kj = jnp.concatenate([kj, k2_ref[...].astype(BF16)], axis=1)
        vj = v_ref[...].astype(BF16)
        kbj = kb_ref[...] if has_bias else None
        lg = _ret_log_gamma(h) if mode == "ret" else None
        dk_acc[...] = jnp.zeros_like(dk_acc)
        dv_acc[...] = jnp.zeros_like(dv_acc)
        if has_bias:
            dkb_acc[...] = jnp.zeros_like(dkb_acc)

        def step(i, diag):
            rows = slice(i * B, (i + 1) * B)
            qi = qbuf[rows, :]
            doi = dobuf[rows, :]
            s = lax.dot_general(qi, kj, NT_DIMS, preferred_element_type=F32)
            dp = lax.dot_general(doi, vj, NT_DIMS, preferred_element_type=F32)
            if softmax:
                s = s * scale
                if has_bias:
                    s = s + kbj
                if diag:
                    s = jnp.where(_visible(mode, B), s, NEG_BIG)
                p = jnp.exp(s - lse_ref[rows, 0:1])
                ds = p * (dp - delta[rows, :])
                if has_bias:
                    dkb_acc[...] += jnp.sum(ds, axis=0, keepdims=True)
                    drow_ref[rows, :] += jnp.broadcast_to(jnp.sum(ds, axis=1, keepdims=True), (B, LANES))
                dsb = (ds * scale).astype(BF16)
            else:
                if diag:
                    dec = jnp.where(_visible(mode, B), _decay(lg, B, 0), 0.0)
                else:
                    dec = _decay(lg, B, i - j)
                p = s * dec
                dsb = (dp * dec).astype(BF16)
            dv_acc[...] += jnp.dot(doT[:, rows], p.astype(BF16), preferred_element_type=F32)
            dk_acc[...] += jnp.dot(qT[:, rows], dsb, preferred_element_type=F32)
            dq = jnp.dot(dsb, kj, preferred_element_type=F32)
            dq1_ref[rows, :] += dq[:, 0:LANES]
            if two:
                dq2_ref[rows, :] += dq[:, LANES:2 * LANES]

        for i in range(nb):
            pl.when(i == j)(functools.partial(step, i, True))
            pl.when(i > j)(functools.partial(step, i, False))
        dk1_ref[...] = dk_acc[0:LANES, :].T
        dv_ref[...] = dv_acc[...].T
        if two:
            dk2_ref[...] = dk_acc[LANES:2 * LANES, :].T
        if has_bias:
            dkb_ref[...] = dkb_acc[...]

    full = lambda w, cb: pl.BlockSpec((S, w), lambda h, j: (0, cb + h))
    blk = lambda w, cb: pl.BlockSpec((B, w), lambda h, j: (j, cb + h))
    in_specs = [full(LANES, q1_cb), blk(LANES, k1_cb), blk(dv, v_cb), full(dv, 0)]
    args = [q1, k1, v, do]
    if softmax:
        in_specs += [full(dv, 0), full(LANES, 0)]
        args += [o, lse]
    if two:
        in_specs += [full(LANES, q2_cb), pl.BlockSpec((B, LANES), lambda h, j: (j, 0))]
        args += [q2, k2]
    if has_bias:
        in_specs.append(pl.BlockSpec((None, None, 1, B), lambda h, j: (h, j, 0, 0)))
        args.append(kbias)
    names = ["dq1", "dk1", "dv"]
    out_shape = [jax.ShapeDtypeStruct((S, H * LANES), F32), jax.ShapeDtypeStruct((S, H * LANES), F32),
                 jax.ShapeDtypeStruct((S, H * dv), F32)]
    out_specs = [full(LANES, 0), blk(LANES, 0), blk(dv, 0)]
    if two:
        names += ["dq2", "dk2h"]
        out_shape += [jax.ShapeDtypeStruct((S, H * LANES), F32)] * 2
        out_specs += [full(LANES, 0), blk(LANES, 0)]
    if has_bias:
        names.append("dkb")
        out_shape.append(jax.ShapeDtypeStruct((H, nb, 1, B), F32))
        out_specs.append(pl.BlockSpec((None, None, 1, B), lambda h, j: (h, j, 0, 0)))
        names.append("drow")
        out_shape.append(jax.ShapeDtypeStruct((S, H * LANES), F32))
        out_specs.append(full(LANES, 0))
    kw = 2 * LANES if two else LANES
    scratch = [pltpu.VMEM((S, kw), BF16), pltpu.VMEM((S, dv), BF16),
               pltpu.VMEM((kw, S), BF16), pltpu.VMEM((dv, S), BF16)]
    if softmax:
        scratch.append(pltpu.VMEM((S, 1), F32))
    scratch += [pltpu.VMEM((kw, B), F32), pltpu.VMEM((dv, B), F32)]
    if has_bias:
        scratch.append(pltpu.VMEM((1, B), F32))
    res = pl.pallas_call(body, out_shape=tuple(out_shape), grid=(H, nb), in_specs=in_specs,
                         out_specs=tuple(out_specs), scratch_shapes=scratch, name=name,
                         compiler_params=_cparams("parallel", "arbitrary"))(*args)
    return dict(zip(names, res))


def head_sum(name, x, H, out_dtype):
    S = x.shape[0]
    tr = _pick(S, 512, 8)

    def body(x_ref, o_ref):
        acc = x_ref[:, 0:LANES]
        for h in range(1, H):
            acc = acc + x_ref[:, h * LANES:(h + 1) * LANES]
        o_ref[...] = acc.astype(o_ref.dtype)

    return pl.pallas_call(
        body, out_shape=jax.ShapeDtypeStruct((S, LANES), out_dtype), grid=(S // tr,),
        in_specs=[pl.BlockSpec((tr, H * LANES), lambda i: (i, 0))],
        out_specs=pl.BlockSpec((tr, LANES), lambda i: (i, 0)), name=name,
        compiler_params=_cparams("parallel"))(x)


def _mesh_pos():
    return lax.axis_index("x"), lax.axis_index("y"), lax.axis_index("c")


def _peer(pos, k):
    x, y, c = pos
    px = 1 - x if k & 4 else x
    py = 1 - y if k & 2 else y
    pc = 1 - c if k & 1 else c
    return (px, py, pc), 4 * px + 2 * py + pc


def exchange(name, tensors):
    nt = len(tensors)
    flat_in, counts = [], []
    out_shape = []
    for mode, srcs in tensors:
        counts.append(len(srcs))
        flat_in += list(srcs)
        rc = srcs[0].shape[-2:]
        out_shape.append(jax.ShapeDtypeStruct((len(srcs), N_DEV) + tuple(rc), srcs[0].dtype))
    n_in = len(flat_in)

    def body(*refs):
        ins = refs[:n_in]
        outs = refs[n_in:n_in + nt]
        send_sems, recv_sems, local_sems = refs[n_in + nt:]
        pos = _mesh_pos()
        me = 4 * pos[0] + 2 * pos[1] + pos[2]
        srcs_of, base = [], 0
        for t in range(nt):
            srcs_of.append(ins[base:base + counts[t]])
            base += counts[t]

        def src_view(t, l, slot):
            ref = srcs_of[t][l]
            return ref if tensors[t][0] == "gather" else ref.at[slot]

        def all_layers(t, slot):
            return outs[t].at[pl.ds(0, counts[t]), slot]

        for t in range(nt):
            for l in range(counts[t]):
                pltpu.make_async_copy(src_view(t, l, me), outs[t].at[l, me], local_sems.at[t]).start()
        for t in range(nt):
            for k in range(1, N_DEV):
                peer, pid = _peer(pos, k)
                for l in range(counts[t]):
                    pltpu.make_async_remote_copy(
                        src_ref=src_view(t, l, pid), dst_ref=outs[t].at[l, me],
                        send_sem=send_sems.at[t, k - 1], recv_sem=recv_sems.at[t, k - 1],
                        device_id=peer, device_id_type=pl.DeviceIdType.MESH).start()
        for t in range(nt):
            for k in range(1, N_DEV):
                peer, pid = _peer(pos, k)
                pltpu.make_async_remote_copy(
                    src_ref=all_layers(t, pid), dst_ref=all_layers(t, pid),
                    send_sem=send_sems.at[t, k - 1], recv_sem=recv_sems.at[t, k - 1],
                    device_id=peer, device_id_type=pl.DeviceIdType.MESH).wait()
        for t in range(nt):
            pltpu.make_async_copy(all_layers(t, me), all_layers(t, me), local_sems.at[t]).wait()

    any_spec = pl.BlockSpec(memory_space=pl.ANY)
    return pl.pallas_call(
        body, out_shape=tuple(out_shape), in_specs=[any_spec] * n_in, out_specs=tuple([any_spec] * nt),
        scratch_shapes=[pltpu.SemaphoreType.DMA((nt, N_DEV - 1)), pltpu.SemaphoreType.DMA((nt, N_DEV - 1)),
                        pltpu.SemaphoreType.DMA((nt,))],
        name=name)(*flat_in)


HBM_SPEC = pl.BlockSpec(memory_space=pltpu.HBM)
SEM_SPEC = pl.BlockSpec(memory_space=pltpu.SEMAPHORE)
DATAFLOW = pltpu.SideEffectType.DATAFLOW_SIDE_EFFECTING


def _hbm(a):
    return pltpu.with_memory_space_constraint(a, pltpu.HBM)


def landing_zones(mode, srcs):
    pos = _mesh_pos()
    me = 4 * pos[0] + 2 * pos[1] + pos[2]
    lands = []
    for s in srcs:
        R, C = s.shape[-2:]
        own = s[None] if mode == "gather" else lax.dynamic_slice(s, (me, 0, 0), (1, R, C))
        lands.append(lax.dynamic_update_slice(lax.empty((N_DEV, R, C), s.dtype), own, (me, 0, 0)))
    return lands


def exchange_start(name, mode, srcs, lands, after=None):
    n = len(srcs)
    extra = [] if after is None else [after]

    def body(*refs):
        src_refs, land_refs = refs[:n], refs[n:2 * n]
        send_sems, recv_sems = refs[2 * n + len(extra)], refs[2 * n + len(extra) + 1]
        token = refs[-1]
        pos = _mesh_pos()
        me = 4 * pos[0] + 2 * pos[1] + pos[2]
        for t in range(n):
            for k in range(1, N_DEV):
                peer, pid = _peer(pos, k)
                src = src_refs[t] if mode == "gather" else src_refs[t].at[pid]
                pltpu.make_async_remote_copy(
                    src_ref=src, dst_ref=land_refs[t].at[me], send_sem=send_sems.at[t], recv_sem=recv_sems.at[t],
                    device_id=peer, device_id_type=pl.DeviceIdType.MESH).start()
        token[...] = jnp.zeros_like(token)

    thru = [pltpu.HBM(a.shape, a.dtype) for a in list(srcs) + list(lands)]
    out_shape = (pltpu.SemaphoreType.DMA((n,)), pltpu.SemaphoreType.DMA((n,)), *thru,
                 jax.ShapeDtypeStruct((8, LANES), F32))
    res = pl.pallas_call(
        body, out_shape=out_shape, in_specs=[HBM_SPEC] * (2 * n) + [pl.BlockSpec(memory_space=pl.ANY)] * len(extra),
        out_specs=(SEM_SPEC, SEM_SPEC, *([HBM_SPEC] * (2 * n)), pl.BlockSpec(memory_space=pltpu.VMEM)),
        input_output_aliases={i: 2 + i for i in range(2 * n)}, name=name,
        compiler_params=pltpu.CompilerParams(has_side_effects=DATAFLOW))(
            *[_hbm(a) for a in list(srcs) + list(lands)], *extra)
    return res[0], res[1], list(res[2:2 + n]), list(res[2 + n:2 + 2 * n]), res[-1]


def exchange_wait(name, send_sems, recv_sems, srcs, lands, after):
    n = len(srcs)

    def body(*refs):
        land_refs = refs[n:2 * n]
        s_sems, r_sems = refs[2 * n], refs[2 * n + 1]
        pos = _mesh_pos()
        for t in range(n):
            seven = land_refs[t].at[pl.ds(0, N_DEV - 1)]
            cp = pltpu.make_async_remote_copy(
                src_ref=seven, dst_ref=seven, send_sem=s_sems.at[t], recv_sem=r_sems.at[t],
                device_id=pos, device_id_type=pl.DeviceIdType.MESH)
            cp.wait_send()
            cp.wait_recv()

    arrs = list(srcs) + list(lands)
    afters = list(after) if isinstance(after, (list, tuple)) else [after]
    res = pl.pallas_call(
        body, out_shape=tuple(pltpu.HBM(a.shape, a.dtype) for a in arrs),
        in_specs=[HBM_SPEC] * (2 * n) + [SEM_SPEC, SEM_SPEC] + [pl.BlockSpec(memory_space=pl.ANY)] * len(afters),
        out_specs=tuple([HBM_SPEC] * (2 * n)), input_output_aliases={i: i for i in range(2 * n)}, name=name,
        compiler_params=pltpu.CompilerParams(has_side_effects=DATAFLOW))(*arrs, send_sems, recv_sems, *afters)
    return list(res[n:])


def reduce_parts(name, parts):
    n, R, C = parts.shape
    tr = _pick(R, max(8, (1 << 20) // (C * 4) // 8 * 8), 8)

    def body(p_ref, o_ref):
        acc = p_ref[0].astype(F32)
        for s in range(1, n):
            acc = acc + p_ref[s].astype(F32)
        o_ref[...] = acc

    return pl.pallas_call(
        body, out_shape=jax.ShapeDtypeStruct((R, C), F32), grid=(R // tr,),
        in_specs=[pl.BlockSpec((n, tr, C), lambda i: (0, i, 0))],
        out_specs=pl.BlockSpec((tr, C), lambda i: (i, 0)), name=name,
        compiler_params=_cparams("parallel"))(parts)


def adamw(name, w, m, v, parts, first=0, prev=None):
    L, R, C = w.shape
    nl = len(parts)
    n = parts[0].shape[0]
    tr = _pick(R, max(8, (1 << 19) // (C * 4) // 8 * 8), 8)
    n_prev = 0 if prev is None else 4

    def body(*refs):
        w_ref, m_ref, v_ref = refs[:3]
        p_refs = refs[3:3 + nl]
        g_ref, d_ref, nm_ref, nv_ref = refs[3 + nl + n_prev:]

        def update(p_ref):
            g = p_ref[0].astype(F32)
            for s in range(1, n):
                g = g + p_ref[s].astype(F32)
            wv = w_ref[...]
            mn = ADAM_B1 * m_ref[...] + (1.0 - ADAM_B1) * g
            vn = ADAM_B2 * v_ref[...] + (1.0 - ADAM_B2) * jnp.square(g)
            m_hat = mn / (1.0 - ADAM_B1 ** ADAM_STEP)
            v_hat = vn / (1.0 - ADAM_B2 ** ADAM_STEP)
            g_ref[...] = g
            d_ref[...] = -ADAM_LR * (m_hat / (jnp.sqrt(v_hat) + ADAM_EPS) + ADAM_WD * wv)
            nm_ref[...] = mn
            nv_ref[...] = vn

        for k in range(nl):
            pl.when(pl.program_id(0) == k)(functools.partial(update, p_refs[k]))

    blk = pl.BlockSpec((None, tr, C), lambda l, i: (first + l, i, 0))
    pspecs = [pl.BlockSpec((n, tr, C), lambda l, i, k=k: (0, jnp.where(l == k, i, 0), 0)) for k in range(nl)]
    sh = jax.ShapeDtypeStruct((L, R, C), F32)
    prev_args = [] if prev is None else list(prev)
    return pl.pallas_call(
        body, out_shape=(sh, sh, sh, sh), grid=(nl, R // tr),
        in_specs=[blk, blk, blk] + pspecs + [pl.BlockSpec(memory_space=pl.ANY)] * n_prev,
        out_specs=(blk, blk, blk, blk), input_output_aliases={3 + nl + q: q for q in range(n_prev)}, name=name,
        compiler_params=_cparams("arbitrary", "arbitrary"))(w, m, v, *parts, *prev_args)


def _cols_from_blocks(g):
    n, R, c = g.shape
    return g.transpose(1, 0, 2).reshape(R, n * c)


def _cols_to_blocks(w):
    R, C = w.shape
    return w.reshape(R, N_DEV, C // N_DEV).transpose(1, 0, 2)


def _uq_permute(w):
    lead = w.shape[:-1]
    w4 = w.reshape(lead + (MLA_HEADS, MLA_NOPE + MLA_ROPE))
    nope = w4[..., :MLA_NOPE].reshape(lead + (MLA_HEADS * MLA_NOPE,))
    rope = jnp.pad(w4[..., MLA_NOPE:], [(0, 0)] * (w4.ndim - 1) + [(0, LANES - MLA_ROPE)])
    return jnp.concatenate([nope, rope.reshape(lead + (MLA_HEADS * LANES,))], axis=-1)


def _uq_unpermute(w):
    lead = w.shape[:-1]
    n = MLA_HEADS * MLA_NOPE
    nope = w[..., :n].reshape(lead + (MLA_HEADS, MLA_NOPE))
    rope = w[..., n:].reshape(lead + (MLA_HEADS, LANES))[..., :MLA_ROPE]
    return jnp.concatenate([nope, rope], axis=-1).reshape(lead + (MLA_HEADS * (MLA_NOPE + MLA_ROPE),))


def _ukv_permute(w):
    lead = w.shape[:-1]
    w4 = w.reshape(lead + (MLA_HEADS, 2, MLA_NOPE))
    return jnp.swapaxes(w4, -3, -2).reshape(lead + (2 * MLA_HEADS * MLA_NOPE,))


def _ukv_unpermute(w):
    lead = w.shape[:-1]
    w4 = w.reshape(lead + (2, MLA_HEADS, MLA_NOPE))
    return jnp.swapaxes(w4, -3, -2).reshape(lead + (2 * MLA_HEADS * MLA_NOPE,))


SMALL = ["norm1_g", "mla_q_norm_g", "mla_kv_norm_g", "fox_b_f", "norm2_g", "ffn_conv_b", "final_norm_g"]
SMALL_TILE = 8 * LANES


def _pack_small(d):
    flat = jnp.concatenate([d[n].reshape(-1).astype(F32) for n in SMALL])
    pad = -flat.shape[0] % SMALL_TILE
    return jnp.pad(flat, (0, pad)).reshape(-1, LANES)


def _unpack_small(packed, like):
    flat = packed.reshape(-1)
    out, o = {}, 0
    for n in SMALL:
        sz = int(np.prod(like[n].shape))
        out[n] = flat[o:o + sz].reshape(like[n].shape)
        o += sz
    return out


WEIGHTS = ["norm1_g", "w_in", "mla_q_norm_g", "mla_kv_norm_g", "mla_w_uq", "mla_w_ukv", "fox_b_f", "w_br_fox",
           "w_br_mla", "w_br_ret", "w_out", "norm2_g", "ffn_w_up", "ffn_w_gate", "ffn_conv_w", "ffn_conv_b",
           "ffn_w_down", "final_norm_g"]
EARLY = ["w_in", "mla_w_uq", "mla_w_ukv"]
LATE = ["w_br_fox", "w_br_mla", "w_br_ret", "w_out", "ffn_w_up", "ffn_w_gate", "ffn_conv_w", "ffn_w_down"]
FFN = ["ffn_w_up", "ffn_w_gate", "ffn_conv_w", "ffn_w_down"]
TRANSPOSED = ("ffn_w_up", "ffn_w_gate")
BIG = EARLY + LATE
X_EARLY = ["w_in_mix", "mla_w_uq", "mla_w_ukv"]
X_LATE = ["w_in_gates"] + LATE
X_MID = ["w_in_gates", "w_out", "w_br_fox", "w_br_mla", "w_br_ret"]
X_REST = ["mla_w_uq", "mla_w_ukv", "w_in_mix"]


def kernel(x, norm1_g, w_in, mla_q_norm_g, mla_kv_norm_g, mla_w_uq, mla_w_ukv, fox_b_f, w_br_fox, w_br_mla, w_br_ret, w_out, norm2_g, ffn_w_up, ffn_w_gate, ffn_conv_w, ffn_conv_b, ffn_w_down, final_norm_g, loss_target, m_norm1_g, m_w_in, m_mla_q_norm_g, m_mla_kv_norm_g, m_mla_w_uq, m_mla_w_ukv, m_fox_b_f, m_w_br_fox, m_w_br_mla, m_w_br_ret, m_w_out, m_norm2_g, m_ffn_w_up, m_ffn_w_gate, m_ffn_conv_w, m_ffn_conv_b, m_ffn_w_down, m_final_norm_g, v_norm1_g, v_w_in, v_mla_q_norm_g, v_mla_kv_norm_g, v_mla_w_uq, v_mla_w_ukv, v_fox_b_f, v_w_br_fox, v_w_br_mla, v_w_br_ret, v_w_out, v_norm2_g, v_ffn_w_up, v_ffn_w_gate, v_ffn_conv_w, v_ffn_conv_b, v_ffn_w_down, v_final_norm_g):
    env = dict(locals())
    W = {n: env[n] for n in WEIGHTS}
    Mo = {n: env["m_" + n] for n in WEIGHTS}
    Vo = {n: env["v_" + n] for n in WEIGHTS}
    S, D = x.shape[1], x.shape[2]
    L = w_in.shape[0]
    lay = InLayout(D)
    NP = lay.total
    f = ffn_w_up.shape[-1]
    xs = x.reshape(S, D)
    tgt = loss_target.reshape(S, D)

    local = {n: W[n].astype(BF16) for n in BIG if n != "w_in"}
    w_gates, w_mix = lay.permute(W["w_in"])
    local["w_in_gates"], local["w_in_mix"] = w_gates.astype(BF16), w_mix.astype(BF16)
    pending = {}
    token = None
    for l in range(L):
        for grp, names in (("a", X_EARLY), ("b", X_LATE)):
            srcs = [local[n][l] for n in names]
            *flight, token = exchange_start(f"gather_start_{l}{grp}", "gather", srcs, landing_zones("gather", srcs),
                                            token)
            pending[l, grp] = flight
    gather_token = token
    cbias_all = ffn_conv_b.reshape(L, 1, N_DEV, 1, f)

    def early_weights(l, after):
        g = dict(zip(X_EARLY, exchange_wait(f"gather_wait_{l}a", *pending[l, "a"], after)))
        return dict(Win=g["w_in_mix"].reshape(1, D, NP), Wuq=_uq_permute(_cols_from_blocks(g["mla_w_uq"])),
                    Wukv=_ukv_permute(_cols_from_blocks(g["mla_w_ukv"])))

    def late_weights(l, after):
        g = dict(zip(X_LATE, exchange_wait(f"gather_wait_{l}b", *pending[l, "b"], after)))
        return dict(
            Wgates=g["w_in_gates"].reshape(1, D, 3 * D),
            Wout=g["w_out"].reshape(1, D, D), Wbf=_cols_from_blocks(g["w_br_fox"]),
            Wbm=_cols_from_blocks(g["w_br_mla"]), Wbr=_cols_from_blocks(g["w_br_ret"]), Wup=g["ffn_w_up"][None],
            Wgate=g["ffn_w_gate"][None], Wdown=g["ffn_w_down"][None], Wconv=g["ffn_conv_w"].astype(F32)[None],
            cbias=cbias_all[l])

    tab64 = rope_tables(S, MLA_ROPE)
    tab128 = rope_tables(S, RET_DK)
    fox_scale = FOX_DH ** -0.5
    mla_scale = (MLA_NOPE + MLA_ROPE) ** -0.5
    ret_kscale = RET_DK ** -0.5
    R = S // LANES
    AB = _attn_block(S)
    NOPE_W = MLA_HEADS * MLA_NOPE

    def vec(a):
        return a.reshape(1, -1)

    saved = []
    xc = xs
    for l in range(L):
        Wl = early_weights(l, gather_token if l == 0 else xc)
        Win, Wuq, Wukv = Wl["Win"], Wl["Wuq"], Wl["Wukv"]
        s = {"x": xc, "W": Wl}
        h1 = rms_fwd("norm1", xc, 0, D, vec(norm1_g[l]), BF16)
        P = mm_nn("in_proj", h1, Win, F32, b_lead=0)
        s.update(h1=h1, P=P)
        ff_off = lay.off["ff"]
        ft = P[:, ff_off:ff_off + FOX_HEADS].T.reshape(FOX_HEADS, R, LANES)
        bfl = jnp.broadcast_to(fox_b_f[l].reshape(FOX_HEADS, 1, 1), (FOX_HEADS, 1, LANES))
        kbias = fox_gate_fwd("fox_gate", ft, bfl).reshape(FOX_HEADS, S // AB, 1, AB)
        o_fox, lse_fox = attn_fwd("fox_attn", "fox", P, lay.cb("fq", LANES), P, lay.cb("fk", LANES),
                                  P, lay.cb("fv", LANES), FOX_HEADS, FOX_DH, fox_scale, kbias=kbias)
        s.update(ft=ft, bfl=bfl, kbias=kbias, o_fox=o_fox, lse_fox=lse_fox)
        cqn = rms_fwd("mla_q_norm", P, lay.cb("mq", MLA_Q_LORA), MLA_Q_LORA, vec(mla_q_norm_g[l]), BF16)
        qall = mm_nn("mla_uq", cqn, Wuq, F32)
        ckvn = rms_fwd("mla_kv_norm", P, lay.cb("mkv", MLA_KV_LORA), MLA_KV_LORA, vec(mla_kv_norm_g[l]), BF16)
        kvall = mm_nn("mla_ukv", ckvn, Wukv, F32)
        qrope = rope_apply("mla_q_rope", qall, NOPE_W // LANES, MLA_HEADS, tab64, 1.0, F32)
        krope = rope_apply("mla_k_rope", P, lay.cb("mkr", LANES), 1, tab64, 1.0, F32)
        o_mla, lse_mla = attn_fwd("mla_attn", "mla", qall, 0, kvall, 0, kvall, NOPE_W // MLA_V, MLA_HEADS, MLA_V,
                                  mla_scale, q2=qrope, q2_cb=0, k2=krope)
        s.update(cqn=cqn, qall=qall, ckvn=ckvn, kvall=kvall, qrope=qrope, krope=krope, o_mla=o_mla,
                 lse_mla=lse_mla)
        rq = rope_apply("ret_q_rope", P, lay.cb("rq", LANES), RET_HEADS, tab128, 1.0, F32)
        rk = rope_apply("ret_k_rope", P, lay.cb("rk", LANES), RET_HEADS, tab128, ret_kscale, F32)
        o_ret, _ = attn_fwd("ret_attn", "ret", rq, 0, rk, 0, P, lay.cb("rv", RET_DV), RET_HEADS, RET_DV, 1.0)
        c_ret = ret_out_fwd("ret_out", o_ret, P, lay.cb("rg", RET_DV), BF16)
        s.update(rq=rq, rk=rk, o_ret=o_ret, c_ret=c_ret)
        Wl.update(late_weights(l, (o_fox, o_mla, c_ret)))
        Wout, Wbf, Wbm, Wbr = Wl["Wout"], Wl["Wbf"], Wl["Wbm"], Wl["Wbr"]
        Wup, Wgate, Wdown, Wconv, cbias = (Wl[k] for k in ("Wup", "Wgate", "Wdown", "Wconv", "cbias"))
        A = mm_nn("br_fox", o_fox, Wbf, F32)
        Bm = mm_nn("br_mla", o_mla, Wbm, F32)
        C = mm_nn("br_ret", c_ret, Wbr, F32)
        Pg = mm_nn("gate_proj", h1, Wl["Wgates"], F32, b_lead=0)
        s["Pg"] = Pg
        merged = merge_fwd("merge", Pg, 0, A, Bm, C, BF16)
        x2 = mm_nn("out_proj", merged, Wout, F32, b_lead=0, res=xc)
        s.update(A=A, Bm=Bm, C=C, merged=merged, x2=x2)
        h2 = rms_fwd("norm2", x2, 0, D, vec(norm2_g[l]), BF16)
        u = ffn_up("ffn_up", h2, Wup, 0, F32)
        gt = ffn_up("ffn_gate", h2, Wgate, 0, F32)
        act = ffn_act_fwd("ffn_act", u, gt, Wconv, cbias, 0, BF16)
        xc = ffn_down("ffn_down", act, Wdown, 0, x2, F32)
        s.update(h2=h2, u=u, gt=gt, act=act)
        saved.append(s)

    loss_tile, dx, dgf = loss_head("loss_head", xc, vec(final_norm_g), tgt)
    loss = lax.psum(loss_tile[0, 0], ("x", "y", "c"))

    gbig = {n: [None] * L for n in BIG + ["w_in_gates", "w_in_mix"]}
    gsmall = {n: [None] * L for n in SMALL if n != "final_norm_g"}
    scattering = {}
    scatter_token = None

    def start_scatter(name, names, l):
        srcs = [gbig[n][l] for n in names]
        *flight, tok = exchange_start(name, "scatter", srcs, landing_zones("scatter", srcs))
        return flight, tok

    for l in reversed(range(L)):
        s = saved[l]
        P = s["P"]
        Wl = s["W"]
        Win, Wout, Wuq, Wukv, Wbf, Wbm, Wbr = (Wl[k] for k in ("Win", "Wout", "Wuq", "Wukv", "Wbf", "Wbm", "Wbr"))
        Wup, Wgate, Wdown, Wconv, cbias = (Wl[k] for k in ("Wup", "Wgate", "Wdown", "Wconv", "cbias"))
        dxb = (dx if scatter_token is None else dx + scatter_token[0, 0]).astype(BF16)
        dact = ffn_down_bwd_act("ffn_down_da", dxb, Wdown, 0, BF16)
        gbig["ffn_w_down"][l] = ffn_down_bwd_w("ffn_down_dw", s["act"], dxb, BF16)
        g, dgt = ffn_act_bwd_point("ffn_act_bwd", s["u"], s["gt"], Wconv, cbias, 0, dact)
        du, dcw, dcb = ffn_act_bwd_conv("ffn_conv_bwd", s["u"], g, Wconv, 0)
        gbig["ffn_conv_w"][l] = dcw.astype(BF16)
        gsmall["ffn_conv_b"][l] = dcb.reshape(-1)
        gbig["ffn_w_up"][l] = ffn_down_bwd_w("ffn_up_dw", du, s["h2"], BF16)
        gbig["ffn_w_gate"][l] = ffn_down_bwd_w("ffn_gate_dw", dgt, s["h2"], BF16)
        dh2 = ffn_up_bwd_h("ffn_up_dh", du, Wup, 0, None, F32)
        dh2 = ffn_up_bwd_h("ffn_gate_dh", dgt, Wgate, 0, dh2, BF16)
        dx2, dg2 = rms_bwd("norm2_bwd", s["x2"], 0, D, vec(norm2_g[l]), dh2, F32, res=dx)
        gsmall["norm2_g"][l] = dg2.reshape(-1)
        scattering[l, "ffn"], scatter_token = start_scatter(f"scatter_start_{l}ffn", FFN, l)
        dx2b = (dx2 + scatter_token[0, 0]).astype(BF16)
        dmerged = mm_nt("out_proj_dm", dx2b, Wout, BF16, b_lead=0)
        gbig["w_out"][l] = mm_tn("out_proj_dw", s["merged"], dx2b, BF16).reshape(N_DEV, D // N_DEV, D)
        dgates, dA, dB, dC = merge_bwd("merge_bwd", s["Pg"], 0, s["A"], s["Bm"], s["C"], dmerged)
        gbig["w_br_fox"][l] = _cols_to_blocks(mm_tn("br_fox_dw", s["o_fox"], dA, BF16))
        gbig["w_br_mla"][l] = _cols_to_blocks(mm_tn("br_mla_dw", s["o_mla"], dB, BF16))
        gbig["w_br_ret"][l] = _cols_to_blocks(mm_tn("br_ret_dw", s["c_ret"], dC, BF16))
        gbig["w_in_gates"][l] = mm_tn("gate_proj_dw", s["h1"], dgates, BF16).reshape(N_DEV, D // N_DEV, 3 * D)
        scattering[l, "mid"], scatter_token = start_scatter(f"scatter_start_{l}mid", X_MID, l)
        dh1_gates = mm_nt("gate_proj_dh", dgates, Wl["Wgates"], F32, b_lead=0)
        do_fox = mm_nt("br_fox_do", dA + scatter_token[0, 0].astype(BF16), Wbf, F32)
        do_mla = mm_nt("br_mla_do", dB, Wbm, F32)
        dc_ret = mm_nt("br_ret_do", dC, Wbr, BF16)
        do_ret, drg = ret_out_bwd("ret_out_bwd", s["o_ret"], P, lay.cb("rg", RET_DV), dc_ret)
        rb = attn_bwd("ret_attn_bwd", "ret", s["rq"], 0, s["rk"], 0, P, lay.cb("rv", RET_DV), RET_HEADS, RET_DV,
                      1.0, do_ret)
        drq = rope_apply("ret_q_rope_bwd", rb["dq1"], 0, RET_HEADS, tab128, 1.0, BF16, transpose=True)
        drk = rope_apply("ret_k_rope_bwd", rb["dk1"], 0, RET_HEADS, tab128, ret_kscale, BF16, transpose=True)
        drv = rb["dv"].astype(BF16)
        mb = attn_bwd("mla_attn_bwd", "mla", s["qall"], 0, s["kvall"], 0, s["kvall"], NOPE_W // MLA_V, MLA_HEADS,
                      MLA_V, mla_scale, do_mla, o=s["o_mla"], lse=s["lse_mla"], q2=s["qrope"], q2_cb=0,
                      k2=s["krope"])
        dqrope = rope_apply("mla_q_rope_bwd", mb["dq2"], 0, MLA_HEADS, tab64, 1.0, BF16, transpose=True)
        dkr_sum = head_sum("mla_k_rope_sum", mb["dk2h"], MLA_HEADS, F32)
        dmkr = rope_apply("mla_k_rope_bwd", dkr_sum, 0, 1, tab64, 1.0, BF16, transpose=True)
        dqall = jnp.concatenate([mb["dq1"].astype(BF16), dqrope], axis=1)
        dkvall = jnp.concatenate([mb["dk1"].astype(BF16), mb["dv"].astype(BF16)], axis=1)
        dcqn = mm_nt("mla_uq_dx", dqall, Wuq, F32)
        dckvn = mm_nt("mla_ukv_dx", dkvall, Wukv, F32)
        guq = _uq_unpermute(mm_tn("mla_uq_dw", s["cqn"], dqall, BF16))
        gukv = _ukv_unpermute(mm_tn("mla_ukv_dw", s["ckvn"], dkvall, BF16))
        gbig["mla_w_uq"][l] = _cols_to_blocks(guq)
        gbig["mla_w_ukv"][l] = _cols_to_blocks(gukv)
        dmq, dgq = rms_bwd("mla_q_norm_bwd", P, lay.cb("mq", MLA_Q_LORA), MLA_Q_LORA, vec(mla_q_norm_g[l]),
                           dcqn, BF16)
        dmkv, dgkv = rms_bwd("mla_kv_norm_bwd", P, lay.cb("mkv", MLA_KV_LORA), MLA_KV_LORA,
                             vec(mla_kv_norm_g[l]), dckvn, BF16)
        gsmall["mla_q_norm_g"][l] = dgq.reshape(-1)
        gsmall["mla_kv_norm_g"][l] = dgkv.reshape(-1)
        fb = attn_bwd("fox_attn_bwd", "fox", P, lay.cb("fq", LANES), P, lay.cb("fk", LANES), P,
                      lay.cb("fv", LANES), FOX_HEADS, FOX_DH, fox_scale, do_fox, o=s["o_fox"], lse=s["lse_fox"],
                      kbias=s["kbias"])
        drow = fb["drow"].reshape(S, FOX_HEADS, LANES)[:, :, 0].T.reshape(FOX_HEADS, R, LANES)
        dft, dbf = fox_gate_bwd("fox_gate_bwd", s["ft"], s["bfl"], fb["dkb"].reshape(FOX_HEADS, R, LANES) - drow)
        gsmall["fox_b_f"][l] = dbf[:, 0, 0]
        dff = jnp.pad(dft.reshape(FOX_HEADS, S).T, ((0, 0), (0, LANES - FOX_HEADS))).astype(BF16)
        segs = dict(rv=drv, rg=drg, mq=dmq, rq=drq, rk=drk, mkv=dmkv, fq=fb["dq1"].astype(BF16),
                    fk=fb["dk1"].astype(BF16), fv=fb["dv"].astype(BF16), mkr=dmkr, ff=dff)
        dP = jnp.concatenate([segs[n] for n in lay.order], axis=1)
        gbig["w_in_mix"][l] = mm_tn("in_proj_dw", s["h1"], dP, BF16).reshape(N_DEV, D // N_DEV, NP)
        scattering[l, "rest"], scatter_token = start_scatter(f"scatter_start_{l}rest", X_REST, l)
        dh1 = mm_nt("in_proj_dh", dP, Win, BF16, b_lead=0, res=dh1_gates)
        dx, dg1 = rms_bwd("norm1_bwd", s["x"], 0, D, vec(norm1_g[l]) + scatter_token[0:1, 0:1], dh1, F32, res=dx2)
        gsmall["norm1_g"][l] = dg1.reshape(-1)

    small_like = {n: W[n] for n in SMALL}
    small_part = {n: jnp.stack(gsmall[n]) for n in gsmall}
    small_part["final_norm_g"] = dgf.reshape(-1)
    small_recv = exchange("gather_small_grads", [("gather", [_pack_small(small_part)])])[0]
    ps = adamw("adamw_small", _pack_small(small_like)[None], _pack_small({n: Mo[n] for n in SMALL})[None],
               _pack_small({n: Vo[n] for n in SMALL})[None], [small_recv[0]])

    def received(l, after):
        r = {}
        for grp, names in (("ffn", FFN), ("mid", X_MID), ("rest", X_REST)):
            r.update(zip(names, exchange_wait(f"scatter_wait_{l}{grp}", *scattering[l, grp], after)))
        r["w_in"] = lay.unpermute(reduce_parts("w_in_gates_grad_sum", r["w_in_gates"]),
                                  reduce_parts("w_in_mix_grad_sum", r["w_in_mix"]))[None]
        return r

    def oriented(n, a):
        return jnp.swapaxes(a, 1, 2) if n in TRANSPOSED else a

    out = {}
    if L > 1:
        recv = [received(l, dx) for l in range(1, L)]
        for n in BIG:
            out[n] = adamw("adamw_" + n, oriented(n, W[n]), oriented(n, Mo[n]), oriented(n, Vo[n]),
                           [r[n] for r in recv], first=1)
    recv0 = received(0, out[BIG[-1]][0] if L > 1 else dx)
    for n in BIG:
        res = adamw("adamw0_" + n, oriented(n, W[n]), oriented(n, Mo[n]), oriented(n, Vo[n]), [recv0[n]],
                    first=0, prev=out.get(n))
        out[n] = tuple(oriented(n, a) for a in res)
    small_out = [_unpack_small(a[0], small_like) for a in ps]
    for n in SMALL:
        out[n] = tuple(so[n] for so in small_out)

    grads = [out[n][0] for n in WEIGHTS]
    deltas = [out[n][1] for n in WEIGHTS]
    new_m = [out[n][2] for n in WEIGHTS]
    new_v = [out[n][3] for n in WEIGHTS]
    return (loss, dx.reshape(1, S, D), *grads, *deltas, *new_m, *new_v)
```

```python
import functools
import math

import numpy as np
import jax
import jax.numpy as jnp
from jax import lax
from jax.experimental import pallas as pl
from jax.experimental.pallas import tpu as pltpu

F32 = jnp.float32
BF16 = jnp.bfloat16

CHUNK = 64
NORM_EPS = 1e-6
ROPE_THETA = 10000.0
FOX_HEADS, FOX_DH = 6, 128
FOX_W = FOX_HEADS * FOX_DH
MLA_HEADS, MLA_NOPE, MLA_ROPE, MLA_V = 6, 128, 64, 128
MLA_Q_LORA, MLA_KV_LORA = 512, 256
MLA_W = MLA_HEADS * MLA_V
RET_HEADS, RET_DK, RET_DV = 4, 128, 256
RET_QK_W, RET_V_W = RET_HEADS * RET_DK, RET_HEADS * RET_DV
ADAM_LR, ADAM_B1, ADAM_B2, ADAM_EPS, ADAM_WD, ADAM_STEP = 0.001, 0.9, 0.999, 1e-08, 0.01, 10

N_DEV = 8
LANES = 128
V7X_VMEM_LIMIT_BYTES = 52 * 1024 * 1024
NEG_BIG = -1e30
HIGHEST = lax.Precision.HIGHEST

NT_DIMS = (((1,), (1,)), ((), ()))
TN_DIMS = (((0,), (0,)), ((), ()))
NN_DIMS = (((1,), (0,)), ((), ()))


def _pick(n, cap, mult=LANES):
    best = None
    for t in range(mult, min(n, cap) + 1, mult):
        if n % t == 0:
            best = t
    return n if best is None else best


def _cparams(*sem):
    return pltpu.CompilerParams(dimension_semantics=sem, vmem_limit_bytes=V7X_VMEM_LIMIT_BYTES)


class InLayout:
    def __init__(self, d_model):
        d = d_model
        self.d = d
        orig = dict(fq=(0, FOX_W), fk=(FOX_W, FOX_W), fv=(2 * FOX_W, FOX_W), ff=(3 * FOX_W, FOX_HEADS))
        o = 3 * FOX_W + FOX_HEADS
        for name, w in (("mq", MLA_Q_LORA), ("mkv", MLA_KV_LORA), ("mkr", MLA_ROPE), ("rq", RET_QK_W),
                        ("rk", RET_QK_W), ("rv", RET_V_W), ("rg", RET_V_W), ("gates", 3 * d)):
            orig[name] = (o, w)
            o += w
        self.orig = orig
        self.orig_width = o
        order = ["rv", "rg", "mq", "rq", "rk", "mkv", "fq", "fk", "fv", "mkr", "ff"]
        self.order = order
        self.off, self.width = {}, {}
        p = 0
        for name in order:
            w = orig[name][1]
            wp = -(-w // LANES) * LANES
            self.off[name], self.width[name] = p, wp
            p += wp
        self.total = p
        self.gates = 3 * d

    def cb(self, name, block):
        assert self.off[name] % block == 0, (name, block)
        return self.off[name] // block

    def permute(self, w):
        parts = []
        for name in self.order:
            o, n = self.orig[name]
            seg = w[..., o:o + n]
            pad = self.width[name] - n
            if pad:
                seg = jnp.pad(seg, [(0, 0)] * (w.ndim - 1) + [(0, pad)])
            parts.append(seg)
        o, n = self.orig["gates"]
        return w[..., o:o + n], jnp.concatenate(parts, axis=-1)

    def unpermute(self, gates, mix):
        names = sorted(self.order, key=lambda n: self.orig[n][0])
        return jnp.concatenate([mix[..., self.off[n]:self.off[n] + self.orig[n][1]] for n in names] + [gates],
                               axis=-1)


def _mm(name, a, b, out_shape, grid, a_spec, b_spec, o_spec, dims, acc_shape, res=None, nsub=0):
    nk = grid[-1]
    has_res = res is not None

    def body(*refs):
        if has_res:
            a_ref, b_ref, r_ref, o_ref = refs[:4]
        else:
            a_ref, b_ref, o_ref = refs[:3]
            r_ref = None
        if nsub:
            prod = lax.dot_general(a_ref[0].astype(BF16), b_ref[0].astype(BF16), dims, preferred_element_type=F32)
            for q in range(1, nsub):
                prod = prod + lax.dot_general(a_ref[q].astype(BF16), b_ref[q].astype(BF16), dims,
                                              preferred_element_type=F32)
        else:
            prod = lax.dot_general(a_ref[...].astype(BF16), b_ref[...].astype(BF16), dims,
                                   preferred_element_type=F32)
        if nk == 1:
            if has_res:
                prod = prod + r_ref[...].astype(F32)
            o_ref[...] = prod.astype(o_ref.dtype)
        else:
            acc_ref = refs[-1]
            k = pl.program_id(len(grid) - 1)

            @pl.when(k == 0)
            def _():
                acc_ref[...] = prod

            @pl.when(k > 0)
            def _():
                acc_ref[...] += prod

            @pl.when(k == nk - 1)
            def _():
                r = acc_ref[...]
                if has_res:
                    r = r + r_ref[...].astype(F32)
                o_ref[...] = r.astype(o_ref.dtype)

    in_specs = [a_spec, b_spec] + ([o_spec] if has_res else [])
    args = (a, b) + ((res,) if has_res else ())
    scratch = [pltpu.VMEM(acc_shape, F32)] if nk > 1 else []
    sem = ("parallel",) * (len(grid) - 1) + ("arbitrary",)
    return pl.pallas_call(body, out_shape=out_shape, grid=grid, in_specs=in_specs, out_specs=o_spec,
                          scratch_shapes=scratch, name=name, compiler_params=_cparams(*sem))(*args)


def mm_nn(name, a, b, out_dtype, b_lead=None, res=None):
    M, K = a.shape
    N = b.shape[-1]
    tm, tn, tk = _pick(M, 1024, 8), _pick(N, 1024), _pick(K, 2048)
    grid = (M // tm, N // tn, K // tk)
    a_spec = pl.BlockSpec((tm, tk), lambda i, j, k: (i, k))
    if b_lead is None:
        b_spec = pl.BlockSpec((tk, tn), lambda i, j, k: (k, j))
    else:
        b_spec = pl.BlockSpec((None, tk, tn), lambda i, j, k: (b_lead, k, j))
    o_spec = pl.BlockSpec((tm, tn), lambda i, j, k: (i, j))
    return _mm(name, a, b, jax.ShapeDtypeStruct((M, N), out_dtype), grid, a_spec, b_spec, o_spec,
               NN_DIMS, (tm, tn), res)


def mm_nt(name, a, b, out_dtype, b_lead=None, res=None):
    M, N = a.shape
    K = b.shape[-2]
    tm, tko, tk = _pick(M, 1024, 8), _pick(K, 1024), _pick(N, 2048)
    grid = (M // tm, K // tko, N // tk)
    a_spec = pl.BlockSpec((tm, tk), lambda i, j, k: (i, k))
    if b_lead is None:
        b_spec = pl.BlockSpec((tko, tk), lambda i, j, k: (j, k))
    else:
        b_spec = pl.BlockSpec((None, tko, tk), lambda i, j, k: (b_lead, j, k))
    o_spec = pl.BlockSpec((tm, tko), lambda i, j, k: (i, j))
    return _mm(name, a, b, jax.ShapeDtypeStruct((M, K), out_dtype), grid, a_spec, b_spec, o_spec,
               NT_DIMS, (tm, tko), res)


def mm_tn(name, a, b, out_dtype):
    M, K = a.shape
    N = b.shape[-1]
    cap = 4096 if (a.dtype == BF16 and b.dtype == BF16) else 2048
    tko, tn, tk = _pick(K, 1024), _pick(N, 1024), _pick(M, cap, 8)
    grid = (K // tko, N // tn, M // tk)
    a_spec = pl.BlockSpec((tk, tko), lambda i, j, k: (k, i))
    b_spec = pl.BlockSpec((tk, tn), lambda i, j, k: (k, j))
    o_spec = pl.BlockSpec((tko, tn), lambda i, j, k: (i, j))
    return _mm(name, a, b, jax.ShapeDtypeStruct((K, N), out_dtype), grid, a_spec, b_spec, o_spec,
               TN_DIMS, (tko, tn))


FFN_SUB = 4
def ffn_up(name, h, w, l, out_dtype):
    M, D = h.shape
    f = w.shape[-1]
    tm = _pick(M, 1024, 8)
    grid = (M // tm, N_DEV, 1)
    return _mm(name, h, w, jax.ShapeDtypeStruct((N_DEV, M, f), out_dtype), grid,
               pl.BlockSpec((tm, D), lambda i, j, k: (i, 0)),
               pl.BlockSpec((None, None, D, f), lambda i, j, k: (l, j, 0, 0)),
               pl.BlockSpec((None, tm, f), lambda i, j, k: (j, i, 0)), NN_DIMS, (tm, f))


def ffn_down(name, act, w, l, res, out_dtype):
    _, M, f = act.shape
    D = w.shape[-1]
    tm, tn = _pick(M, 1024, 8), _pick(D, 1024)
    grid = (M // tm, D // tn, N_DEV // FFN_SUB)
    return _mm(name, act, w, jax.ShapeDtypeStruct((M, D), out_dtype), grid,
               pl.BlockSpec((FFN_SUB, tm, f), lambda i, j, k: (k, i, 0)),
               pl.BlockSpec((None, FFN_SUB, f, tn), lambda i, j, k: (l, k, 0, j)),
               pl.BlockSpec((tm, tn), lambda i, j, k: (i, j)), NN_DIMS, (tm, tn), res, nsub=FFN_SUB)


def ffn_down_bwd_act(name, dy, w, l, out_dtype):
    M, D = dy.shape
    f = w.shape[-2]
    tm = _pick(M, 1024, 8)
    grid = (M // tm, N_DEV, 1)
    return _mm(name, dy, w, jax.ShapeDtypeStruct((N_DEV, M, f), out_dtype), grid,
               pl.BlockSpec((tm, D), lambda i, j, k: (i, 0)),
               pl.BlockSpec((None, None, f, D), lambda i, j, k: (l, j, 0, 0)),
               pl.BlockSpec((None, tm, f), lambda i, j, k: (j, i, 0)), NT_DIMS, (tm, f))


def ffn_down_bwd_w(name, act, dy, out_dtype):
    _, M, f = act.shape
    D = dy.shape[-1]
    tn, tk = _pick(D, 1024), _pick(M, 4096 if dy.dtype == BF16 else 2048, 8)
    grid = (N_DEV, D // tn, M // tk)
    return _mm(name, act, dy, jax.ShapeDtypeStruct((N_DEV, f, D), out_dtype), grid,
               pl.BlockSpec((None, tk, f), lambda j, n, k: (j, k, 0)),
               pl.BlockSpec((tk, tn), lambda j, n, k: (k, n)),
               pl.BlockSpec((None, f, tn), lambda j, n, k: (j, 0, n)), TN_DIMS, (f, tn))


def ffn_up_bwd_h(name, du, w, l, res, out_dtype):
    _, M, f = du.shape
    D = w.shape[-2]
    tm, tn = _pick(M, 1024, 8), _pick(D, 1024)
    grid = (M // tm, D // tn, N_DEV // FFN_SUB)
    return _mm(name, du, w, jax.ShapeDtypeStruct((M, D), out_dtype), grid,
               pl.BlockSpec((FFN_SUB, tm, f), lambda i, j, k: (k, i, 0)),
               pl.BlockSpec((None, FFN_SUB, tn, f), lambda i, j, k: (l, k, j, 0)),
               pl.BlockSpec((tm, tn), lambda i, j, k: (i, j)), NT_DIMS, (tm, tn), res, nsub=FFN_SUB)


def _rms(xf, g):
    return xf * lax.rsqrt(jnp.mean(xf * xf, axis=-1, keepdims=True) + NORM_EPS) * g


def rms_fwd(name, x, cb, W, g, out_dtype):
    S = x.shape[0]
    tr = _pick(S, 256, 8)

    def body(x_ref, g_ref, o_ref):
        o_ref[...] = _rms(x_ref[...].astype(F32), g_ref[...]).astype(o_ref.dtype)

    return pl.pallas_call(
        body, out_shape=jax.ShapeDtypeStruct((S, W), out_dtype), grid=(S // tr,),
        in_specs=[pl.BlockSpec((tr, W), lambda i: (i, cb)), pl.BlockSpec((1, W), lambda i: (0, 0))],
        out_specs=pl.BlockSpec((tr, W), lambda i: (i, 0)), name=name, compiler_params=_cparams("parallel"))(x, g)


def rms_bwd(name, x, cb, W, g, dy, out_dtype, res=None):
    S = x.shape[0]
    tr = _pick(S, 256, 8)
    has_res = res is not None

    def body(*refs):
        if has_res:
            x_ref, g_ref, dy_ref, r_ref, dx_ref, dg_ref = refs
        else:
            x_ref, g_ref, dy_ref, dx_ref, dg_ref = refs
        _, vjp = jax.vjp(_rms, x_ref[...].astype(F32), g_ref[...])
        dx, dg = vjp(dy_ref[...].astype(F32))
        if has_res:
            dx = dx + r_ref[...]
        dx_ref[...] = dx.astype(dx_ref.dtype)

        @pl.when(pl.program_id(0) == 0)
        def _():
            dg_ref[...] = jnp.zeros_like(dg_ref)

        dg_ref[...] += dg

    row = pl.BlockSpec((tr, W), lambda i: (i, 0))
    vec = pl.BlockSpec((1, W), lambda i: (0, 0))
    in_specs = [pl.BlockSpec((tr, W), lambda i: (i, cb)), vec, row] + ([row] if has_res else [])
    args = (x, g, dy) + ((res,) if has_res else ())
    return pl.pallas_call(
        body, out_shape=(jax.ShapeDtypeStruct((S, W), out_dtype), jax.ShapeDtypeStruct((1, W), F32)),
        grid=(S // tr,), in_specs=in_specs, out_specs=(row, vec), name=name,
        compiler_params=_cparams("arbitrary"))(*args)


def rope_tables(S, d):
    pos = jnp.arange(S, dtype=F32)
    inv_freq = ROPE_THETA ** (-jnp.arange(0, d, 2, dtype=F32) / d)
    ang = pos[:, None] * inv_freq[None, :]
    cos, sin = jnp.cos(ang), jnp.sin(ang)
    half = d // 2
    z = jnp.zeros((S, LANES - d), F32)
    zh = jnp.zeros((S, half), F32)
    c = jnp.concatenate([cos, cos, z], axis=1)
    sa = jnp.concatenate([-sin, zh, z], axis=1)
    sb = jnp.concatenate([zh, sin, z], axis=1)
    return c, sa, sb, half


def rope_apply(name, x, cb, H, tabs, scale, out_dtype, transpose=False):
    c, sa, sb, half = tabs
    S = x.shape[0]
    tr = _pick(S, 512, 8)
    up, down = LANES - half, half

    def body(x_ref, c_ref, sa_ref, sb_ref, o_ref):
        xv = x_ref[...].astype(F32)
        if not transpose:
            y = xv * c_ref[...] + pltpu.roll(xv, up, 1) * sa_ref[...] + pltpu.roll(xv, down, 1) * sb_ref[...]
            y = y * scale
        else:
            xv = xv * scale
            y = (xv * c_ref[...] + pltpu.roll(xv * sa_ref[...], down, 1)
                 + pltpu.roll(xv * sb_ref[...], up, 1))
        o_ref[...] = y.astype(o_ref.dtype)

    tab = pl.BlockSpec((tr, LANES), lambda h, i: (i, 0))
    return pl.pallas_call(
        body, out_shape=jax.ShapeDtypeStruct((S, H * LANES), out_dtype), grid=(H, S // tr),
        in_specs=[pl.BlockSpec((tr, LANES), lambda h, i: (i, cb + h)), tab, tab, tab],
        out_specs=pl.BlockSpec((tr, LANES), lambda h, i: (i, h)), name=name,
        compiler_params=_cparams("parallel", "parallel"))(x, c, sa, sb)


def _ret_out(o, g):
    y = o * lax.rsqrt(jnp.mean(o * o, axis=-1, keepdims=True) + NORM_EPS)
    return y * jax.nn.silu(g)


def ret_out_fwd(name, o, gsrc, g_cb, out_dtype):
    S = o.shape[0]
    tr = _pick(S, 512, 8)
    W = RET_DV

    def body(o_ref, g_ref, y_ref):
        y_ref[...] = _ret_out(o_ref[...], g_ref[...].astype(F32)).astype(y_ref.dtype)

    blk = pl.BlockSpec((tr, W), lambda h, i: (i, h))
    return pl.pallas_call(
        body, out_shape=jax.ShapeDtypeStruct((S, RET_HEADS * W), out_dtype), grid=(RET_HEADS, S // tr),
        in_specs=[blk, pl.BlockSpec((tr, W), lambda h, i: (i, g_cb + h))], out_specs=blk, name=name,
        compiler_params=_cparams("parallel", "parallel"))(o, gsrc)


def ret_out_bwd(name, o, gsrc, g_cb, dy):
    S = o.shape[0]
    tr = _pick(S, 512, 8)
    W = RET_DV

    def body(o_ref, g_ref, dy_ref, do_ref, dg_ref):
        _, vjp = jax.vjp(_ret_out, o_ref[...], g_ref[...].astype(F32))
        do, dg = vjp(dy_ref[...].astype(F32))
        do_ref[...] = do.astype(do_ref.dtype)
        dg_ref[...] = dg.astype(dg_ref.dtype)

    blk = pl.BlockSpec((tr, W), lambda h, i: (i, h))
    return pl.pallas_call(
        body, out_shape=(jax.ShapeDtypeStruct((S, RET_HEADS * W), F32),
                         jax.ShapeDtypeStruct((S, RET_HEADS * W), BF16)),
        grid=(RET_HEADS, S // tr),
        in_specs=[blk, pl.BlockSpec((tr, W), lambda h, i: (i, g_cb + h)), blk], out_specs=(blk, blk),
        name=name, compiler_params=_cparams("parallel", "parallel"))(o, gsrc, dy)


def _merge(g0, g1, g2, a, b, c):
    return jax.nn.sigmoid(g0) * a + jax.nn.sigmoid(g1) * b + jax.nn.sigmoid(g2) * c


def merge_fwd(name, P, gates_cb, a, b, c, out_dtype):
    S, D = a.shape
    tr = _pick(S, 128, 8)

    def body(g0, g1, g2, a_ref, b_ref, c_ref, o_ref):
        o_ref[...] = _merge(g0[...], g1[...], g2[...], a_ref[...], b_ref[...], c_ref[...]).astype(o_ref.dtype)

    row = pl.BlockSpec((tr, D), lambda i: (i, 0))
    gs = [pl.BlockSpec((tr, D), lambda i, k=k: (i, gates_cb + k)) for k in range(3)]
    return pl.pallas_call(
        body, out_shape=jax.ShapeDtypeStruct((S, D), out_dtype), grid=(S // tr,),
        in_specs=gs + [row, row, row], out_specs=row, name=name,
        compiler_params=_cparams("parallel"))(P, P, P, a, b, c)


def merge_bwd(name, P, gates_cb, a, b, c, dm):
    S, D = a.shape
    tr = _pick(S, 128, 8)

    def body(g0, g1, g2, a_ref, b_ref, c_ref, dm_ref, dg_ref, da_ref, db_ref, dc_ref):
        _, vjp = jax.vjp(_merge, g0[...], g1[...], g2[...], a_ref[...], b_ref[...], c_ref[...])
        d0, d1, d2, da, db, dc = vjp(dm_ref[...].astype(F32))
        dg_ref[:, 0:D] = d0.astype(dg_ref.dtype)
        dg_ref[:, D:2 * D] = d1.astype(dg_ref.dtype)
        dg_ref[:, 2 * D:3 * D] = d2.astype(dg_ref.dtype)
        da_ref[...] = da.astype(da_ref.dtype)
        db_ref[...] = db.astype(db_ref.dtype)
        dc_ref[...] = dc.astype(dc_ref.dtype)

    row = pl.BlockSpec((tr, D), lambda i: (i, 0))
    gs = [pl.BlockSpec((tr, D), lambda i, k=k: (i, gates_cb + k)) for k in range(3)]
    bf = jax.ShapeDtypeStruct((S, D), BF16)
    return pl.pallas_call(
        body, out_shape=(jax.ShapeDtypeStruct((S, 3 * D), BF16), bf, bf, bf), grid=(S // tr,),
        in_specs=gs + [row, row, row, row],
        out_specs=(pl.BlockSpec((tr, 3 * D), lambda i: (i, 0)), row, row, row), name=name,
        compiler_params=_cparams("parallel"))(P, P, P, a, b, c, dm)


HALO = 8


CONV_CHUNK = 32


def _shifted_back(u_ref, uh_ref, ext_ref, s1_ref, s2_ref, tr):
    ext_ref[0:HALO, :] = jnp.where(pl.program_id(1) > 0, uh_ref[...], 0.0)
    ext_ref[HALO:HALO + tr, :] = u_ref[...]
    s1_ref[...] = ext_ref[HALO - 1:HALO - 1 + tr, :]
    s2_ref[...] = ext_ref[HALO - 2:HALO - 2 + tr, :]


def _conv3(cw, cb, u, u1, u2):
    return cb + ((cw[0:1, :] * u2 + cw[1:2, :] * u1) + cw[2:3, :] * u)


def _chunks(tr, fn):
    def step(c, carry):
        return fn(pl.ds(pl.multiple_of(c * CONV_CHUNK, CONV_CHUNK), CONV_CHUNK), carry)
    return step


def _ffn_specs(S, f, tr, l):
    nb = tr // HALO
    row = pl.BlockSpec((None, tr, f), lambda j, i: (j, i, 0))
    prev = pl.BlockSpec((None, HALO, f), lambda j, i: (j, jnp.maximum(i * nb - 1, 0), 0))
    nxt = pl.BlockSpec((None, HALO, f), lambda j, i: (j, jnp.minimum((i + 1) * nb, S // HALO - 1), 0))
    cw = pl.BlockSpec((None, None, 3, f), lambda j, i: (l, j, 0, 0))
    cb = pl.BlockSpec((None, None, 1, f), lambda j, i: (l, j, 0, 0))
    return row, prev, nxt, cw, cb


def ffn_act_fwd(name, u, gt, cw, cb, l, out_dtype):
    _, S, f = u.shape
    tr = _pick(S, 512, 8)
    row, prev, _, cws, cbs = _ffn_specs(S, f, tr, l)

    def body(u_ref, uh_ref, gt_ref, cw_ref, cb_ref, o_ref, ext_ref, s1_ref, s2_ref):
        _shifted_back(u_ref, uh_ref, ext_ref, s1_ref, s2_ref, tr)
        cwv, cbv = cw_ref[...], cb_ref[...]

        def chunk(rows, carry):
            uc = _conv3(cwv, cbv, u_ref[rows, :], s1_ref[rows, :], s2_ref[rows, :])
            o_ref[rows, :] = (jax.nn.gelu(uc) * gt_ref[rows, :]).astype(o_ref.dtype)
            return carry

        lax.fori_loop(0, tr // CONV_CHUNK, _chunks(tr, chunk), 0)

    return pl.pallas_call(
        body, out_shape=jax.ShapeDtypeStruct((N_DEV, S, f), out_dtype), grid=(N_DEV, S // tr),
        in_specs=[row, prev, row, cws, cbs], out_specs=row,
        scratch_shapes=[pltpu.VMEM((tr + HALO, f), F32), pltpu.VMEM((tr, f), F32), pltpu.VMEM((tr, f), F32)],
        name=name, compiler_params=_cparams("parallel", "parallel"))(u, u, gt, cw, cb)


def ffn_act_bwd_point(name, u, gt, cw, cb, l, dact):
    _, S, f = u.shape
    tr = _pick(S, 512, 8)
    row, prev, _, cws, cbs = _ffn_specs(S, f, tr, l)

    def body(u_ref, uh_ref, gt_ref, cw_ref, cb_ref, da_ref, g_ref, dgt_ref, ext_ref, s1_ref, s2_ref):
        _shifted_back(u_ref, uh_ref, ext_ref, s1_ref, s2_ref, tr)
        cwv, cbv = cw_ref[...], cb_ref[...]

        def chunk(rows, carry):
            uc = _conv3(cwv, cbv, u_ref[rows, :], s1_ref[rows, :], s2_ref[rows, :])
            _, vjp = jax.vjp(lambda c, t: jax.nn.gelu(c) * t, uc, gt_ref[rows, :])
            g, dgt = vjp(da_ref[rows, :].astype(F32))
            g_ref[rows, :] = g
            dgt_ref[rows, :] = dgt.astype(dgt_ref.dtype)
            return carry

        lax.fori_loop(0, tr // CONV_CHUNK, _chunks(tr, chunk), 0)

    return pl.pallas_call(
        body, out_shape=(jax.ShapeDtypeStruct((N_DEV, S, f), F32), jax.ShapeDtypeStruct((N_DEV, S, f), BF16)),
        grid=(N_DEV, S // tr), in_specs=[row, prev, row, cws, cbs, row], out_specs=(row, row),
        scratch_shapes=[pltpu.VMEM((tr + HALO, f), F32), pltpu.VMEM((tr, f), F32), pltpu.VMEM((tr, f), F32)],
        name=name, compiler_params=_cparams("parallel", "parallel"))(u, u, gt, cw, cb, dact)


def ffn_act_bwd_conv(name, u, g, cw, l):
    _, S, f = u.shape
    tr = _pick(S, 512, 8)
    nt = S // tr
    row, prev, nxt, cws, _ = _ffn_specs(S, f, tr, l)

    def body(u_ref, uh_ref, g_ref, gn_ref, cw_ref, du_ref, dcw_ref, dcb_ref, ext_ref, s1_ref, s2_ref, n1_ref, n2_ref):
        i = pl.program_id(1)
        _shifted_back(u_ref, uh_ref, ext_ref, s1_ref, s2_ref, tr)
        ext_ref[0:tr, :] = g_ref[...]
        ext_ref[tr:tr + HALO, :] = jnp.where(i < nt - 1, gn_ref[...], 0.0)
        n1_ref[...] = ext_ref[1:1 + tr, :]
        n2_ref[...] = ext_ref[2:2 + tr, :]
        cw = cw_ref[...]

        def chunk(rows, carry):
            d0, d1, d2, db = carry
            g = g_ref[rows, :]
            du_ref[rows, :] = (cw[2:3, :] * g + cw[1:2, :] * n1_ref[rows, :]
                               + cw[0:1, :] * n2_ref[rows, :]).astype(du_ref.dtype)
            return (d0 + jnp.sum(g * s2_ref[rows, :], axis=0, keepdims=True),
                    d1 + jnp.sum(g * s1_ref[rows, :], axis=0, keepdims=True),
                    d2 + jnp.sum(g * u_ref[rows, :], axis=0, keepdims=True),
                    db + jnp.sum(g, axis=0, keepdims=True))

        z = jnp.zeros((1, f), F32)
        d0, d1, d2, db = lax.fori_loop(0, tr // CONV_CHUNK, _chunks(tr, chunk), (z, z, z, z))

        @pl.when(i == 0)
        def _():
            dcw_ref[...] = jnp.zeros_like(dcw_ref)
            dcb_ref[...] = jnp.zeros_like(dcb_ref)

        dcw_ref[0:1, :] += d0
        dcw_ref[1:2, :] += d1
        dcw_ref[2:3, :] += d2
        dcb_ref[...] += db

    tile = pltpu.VMEM((tr, f), F32)
    return pl.pallas_call(
        body, out_shape=(jax.ShapeDtypeStruct((N_DEV, S, f), BF16), jax.ShapeDtypeStruct((N_DEV, 3, f), F32),
                         jax.ShapeDtypeStruct((N_DEV, 1, f), F32)),
        grid=(N_DEV, nt), in_specs=[row, prev, row, nxt, cws],
        out_specs=(row, pl.BlockSpec((None, 3, f), lambda j, i: (j, 0, 0)),
                   pl.BlockSpec((None, 1, f), lambda j, i: (j, 0, 0))),
        scratch_shapes=[pltpu.VMEM((tr + HALO, f), F32), tile, tile, tile, tile], name=name,
        compiler_params=_cparams("parallel", "arbitrary"))(u, u, g, g, cw)


def loss_head(name, x, g, tgt):
    S, D = x.shape
    tr = _pick(S, 256, 8)

    def body(x_ref, g_ref, t_ref, l_ref, dx_ref, dg_ref):
        tg = t_ref[...]

        def f(xv, gv):
            err = jnp.square(_rms(xv, gv) - tg)
            return 0.5 * jnp.sum(jnp.mean(err, axis=-1))

        val, vjp = jax.vjp(f, x_ref[...], g_ref[...])
        dx, dg = vjp(jnp.ones((), F32))
        dx_ref[...] = dx

        @pl.when(pl.program_id(0) == 0)
        def _():
            l_ref[...] = jnp.zeros_like(l_ref)
            dg_ref[...] = jnp.zeros_like(dg_ref)

        l_ref[...] += val
        dg_ref[...] += dg

    row = pl.BlockSpec((tr, D), lambda i: (i, 0))
    vec = pl.BlockSpec((1, D), lambda i: (0, 0))
    lt = pl.BlockSpec((8, LANES), lambda i: (0, 0))
    return pl.pallas_call(
        body, out_shape=(jax.ShapeDtypeStruct((8, LANES), F32), jax.ShapeDtypeStruct((S, D), F32),
                         jax.ShapeDtypeStruct((1, D), F32)),
        grid=(S // tr,), in_specs=[row, vec, row], out_specs=(lt, row, vec), name=name,
        compiler_params=_cparams("arbitrary"))(x, g, tgt)


def _tri(n, fn):
    r = lax.broadcasted_iota(jnp.int32, (n, n), 0)
    c = lax.broadcasted_iota(jnp.int32, (n, n), 1)
    return jnp.where(fn(r, c), 1.0, 0.0).astype(F32)


def _log_sigmoid(z):
    return jnp.minimum(z, 0.0) - jnp.log1p(jnp.exp(-jnp.abs(z)))


def fox_gate_fwd(name, ft, b):
    H, R, _ = ft.shape

    def body(f_ref, b_ref, o_ref):
        ls = _log_sigmoid(f_ref[...] + b_ref[...])
        cum = jnp.dot(ls, _tri(LANES, lambda r, c: r <= c), precision=HIGHEST, preferred_element_type=F32)
        tot = jnp.broadcast_to(cum[:, LANES - 1:LANES], (R, LANES))
        off = jnp.dot(_tri(R, lambda r, c: r > c), tot, precision=HIGHEST, preferred_element_type=F32)
        o_ref[...] = -(cum + off)

    blk = pl.BlockSpec((None, R, LANES), lambda h: (h, 0, 0))
    return pl.pallas_call(
        body, out_shape=jax.ShapeDtypeStruct((H, R, LANES), F32), grid=(H,),
        in_specs=[blk, pl.BlockSpec((None, 1, LANES), lambda h: (h, 0, 0))], out_specs=blk, name=name,
        compiler_params=_cparams("parallel"))(ft, b)


def fox_gate_bwd(name, ft, b, dkb):
    H, R, _ = ft.shape

    def body(f_ref, b_ref, d_ref, df_ref, db_ref):
        z = f_ref[...] + b_ref[...]
        d = d_ref[...]
        rev = jnp.dot(d, _tri(LANES, lambda r, c: r >= c), precision=HIGHEST, preferred_element_type=F32)
        tot = jnp.broadcast_to(rev[:, 0:1], (R, LANES))
        off = jnp.dot(_tri(R, lambda r, c: r < c), tot, precision=HIGHEST, preferred_element_type=F32)
        dls = -(rev + off)
        dz = dls * jax.nn.sigmoid(-z)
        df_ref[...] = dz
        s = jnp.sum(jnp.sum(dz, axis=1, keepdims=True), axis=0, keepdims=True)
        db_ref[...] = jnp.broadcast_to(s, (1, LANES))

    blk = pl.BlockSpec((None, R, LANES), lambda h: (h, 0, 0))
    vec = pl.BlockSpec((None, 1, LANES), lambda h: (h, 0, 0))
    return pl.pallas_call(
        body, out_shape=(jax.ShapeDtypeStruct((H, R, LANES), F32), jax.ShapeDtypeStruct((H, 1, LANES), F32)),
        grid=(H,), in_specs=[blk, vec, blk], out_specs=(blk, vec), name=name,
        compiler_params=_cparams("parallel"))(ft, b, dkb)


def _ret_log_gamma(h):
    lg = [float(np.log(np.float32(1.0) - np.float32(2.0) ** np.float32(-5.0 - i))) for i in range(RET_HEADS)]
    out = jnp.float32(lg[RET_HEADS - 1])
    for i in range(RET_HEADS - 2, -1, -1):
        out = jnp.where(h == i, jnp.float32(lg[i]), out)
    return out


def _visible(mode, B):
    r = lax.broadcasted_iota(jnp.int32, (B, B), 0)
    c = lax.broadcasted_iota(jnp.int32, (B, B), 1)
    if mode == "fox":
        return c <= r
    return (c // CHUNK) <= (r // CHUNK)


def _visible_t(mode, B):
    k = lax.broadcasted_iota(jnp.int32, (B, B), 0)
    q = lax.broadcasted_iota(jnp.int32, (B, B), 1)
    if mode == "fox":
        return k <= q
    return (k // CHUNK) <= (q // CHUNK)


def _decay(lg, B, blocks_apart):
    r = lax.broadcasted_iota(jnp.int32, (B, B), 0)
    c = lax.broadcasted_iota(jnp.int32, (B, B), 1)
    dist = jnp.abs(r - c + blocks_apart * B).astype(F32)
    return jnp.exp(lg * dist)


def _attn_block(S):
    return 512 if S >= 2048 else 128


def attn_fwd(name, mode, q1, q1_cb, k1, k1_cb, v, v_cb, H, dv, scale, q2=None, q2_cb=0, k2=None, kbias=None):
    S = q1.shape[0]
    B = _attn_block(S)
    nq = S // B
    softmax = mode != "ret"
    two = mode == "mla"
    has_bias = mode == "fox"

    def body(*refs):
        it = iter(refs)
        q1_ref, k1_ref, v_ref = next(it), next(it), next(it)
        q2_ref = next(it) if two else None
        k2_ref = next(it) if two else None
        kb_ref = next(it) if has_bias else None
        o_ref = next(it)
        lse_ref = next(it) if softmax else None
        kbuf, vT = next(it), next(it)
        acc = next(it)
        m_ref = next(it) if softmax else None
        l_ref = next(it) if softmax else None
        s_all = next(it) if softmax else None
        kcol = next(it) if has_bias else None
        h = pl.program_id(0)
        i = pl.program_id(1)

        @pl.when(i == 0)
        def _():
            kbuf[:, 0:LANES] = k1_ref[...].astype(BF16)
            vT[...] = v_ref[...].astype(F32).T.astype(BF16)
            if two:
                kbuf[:, LANES:2 * LANES] = k2_ref[...].astype(BF16)
            if has_bias:
                for g in range(nq):
                    kcol[g * B:(g + 1) * B, :] = jnp.broadcast_to(kb_ref[g], (LANES, B)).T[:, 0:1]

        qb = q1_ref[...].astype(BF16)
        if two:
            qb = jnp.concatenate([qb, q2_ref[...].astype(BF16)], axis=1)
        lg = _ret_log_gamma(h) if mode == "ret" else None
        acc[...] = jnp.zeros_like(acc)
        if softmax:
            m_ref[...] = jnp.full_like(m_ref, NEG_BIG)
            l_ref[...] = jnp.zeros_like(l_ref)

        def scores(g, diag):
            rows = slice(g * B, (g + 1) * B)
            s = lax.dot_general(kbuf[rows, :], qb, NT_DIMS, preferred_element_type=F32)
            if softmax:
                s = s * scale
                if has_bias:
                    s = s + kcol[rows, :]
                if diag:
                    s = jnp.where(_visible_t(mode, B), s, NEG_BIG)
                s_all[rows, :] = s
                m_ref[...] = jnp.maximum(m_ref[...], jnp.max(s, axis=0, keepdims=True))
            else:
                if diag:
                    p = jnp.where(_visible_t(mode, B), s * _decay(lg, B, 0), 0.0)
                else:
                    p = s * _decay(lg, B, g - i)
                acc[...] += jnp.dot(vT[:, rows], p.astype(BF16), preferred_element_type=F32)

        def weighted(g):
            rows = slice(g * B, (g + 1) * B)
            p = jnp.exp(s_all[rows, :] - m_ref[...])
            l_ref[...] += jnp.sum(p, axis=0, keepdims=True)
            acc[...] += jnp.dot(vT[:, rows], p.astype(BF16), preferred_element_type=F32)

        for g in range(nq):
            pl.when(g < i)(functools.partial(scores, g, False))
            pl.when(g == i)(functools.partial(scores, g, True))
        if softmax:
            for g in range(nq):
                pl.when(g <= i)(functools.partial(weighted, g))
            o_ref[...] = (acc[...] / l_ref[...]).T
            lse_ref[...] = jnp.broadcast_to(m_ref[...] + jnp.log(l_ref[...]), (LANES, B)).T
        else:
            o_ref[...] = acc[...].T

    in_specs = [pl.BlockSpec((B, LANES), lambda h, i: (i, q1_cb + h)),
                pl.BlockSpec((S, LANES), lambda h, i: (0, k1_cb + h)),
                pl.BlockSpec((S, dv), lambda h, i: (0, v_cb + h))]
    args = [q1, k1, v]
    if two:
        in_specs += [pl.BlockSpec((B, LANES), lambda h, i: (i, q2_cb + h)),
                     pl.BlockSpec((S, LANES), lambda h, i: (0, 0))]
        args += [q2, k2]
    if has_bias:
        in_specs.append(pl.BlockSpec((None, nq, 1, B), lambda h, i: (h, 0, 0, 0)))
        args.append(kbias)
    out_shape = [jax.ShapeDtypeStruct((S, H * dv), F32)]
    out_specs = [pl.BlockSpec((B, dv), lambda h, i: (i, h))]
    if softmax:
        out_shape.append(jax.ShapeDtypeStruct((S, H * LANES), F32))
        out_specs.append(pl.BlockSpec((B, LANES), lambda h, i: (i, h)))
    kw = 2 * LANES if two else LANES
    scratch = [pltpu.VMEM((S, kw), BF16), pltpu.VMEM((dv, S), BF16), pltpu.VMEM((dv, B), F32)]
    if softmax:
        scratch += [pltpu.VMEM((1, B), F32), pltpu.VMEM((1, B), F32), pltpu.VMEM((S, B), F32)]
    if has_bias:
        scratch.append(pltpu.VMEM((S, 1), F32))
    res = pl.pallas_call(body, out_shape=tuple(out_shape), grid=(H, nq), in_specs=in_specs,
                         out_specs=tuple(out_specs), scratch_shapes=scratch, name=name,
                         compiler_params=_cparams("parallel", "arbitrary"))(*args)
    return res if softmax else (res[0], None)


def attn_bwd(name, mode, q1, q1_cb, k1, k1_cb, v, v_cb, H, dv, scale, do, o=None, lse=None,
             q2=None, q2_cb=0, k2=None, kbias=None):
    S = q1.shape[0]
    B = _attn_block(S)
    nb = S // B
    softmax = mode != "ret"
    two = mode == "mla"
    has_bias = mode == "fox"

    def body(*refs):
        it = iter(refs)
        q1_ref, k1_ref, v_ref, do_ref = next(it), next(it), next(it), next(it)
        o_ref = next(it) if softmax else None
        lse_ref = next(it) if softmax else None
        q2_ref = next(it) if two else None
        k2_ref = next(it) if two else None
        kb_ref = next(it) if has_bias else None
        dq1_ref, dk1_ref, dv_ref = next(it), next(it), next(it)
        dq2_ref = next(it) if two else None
        dk2_ref = next(it) if two else None
        dkb_ref = next(it) if has_bias else None
        drow_ref = next(it) if has_bias else None
        qbuf, dobuf = next(it), next(it)
        qT, doT = next(it), next(it)
        delta = next(it) if softmax else None
        dk_acc, dv_acc = next(it), next(it)
        dkb_acc = next(it) if has_bias else None
        h = pl.program_id(0)
        j = pl.program_id(1)

        @pl.when(j == 0)
        def _():
            qbuf[:, 0:LANES] = q1_ref[...].astype(BF16)
            dobuf[...] = do_ref[...].astype(BF16)
            qT[0:LANES, :] = q1_ref[...].astype(F32).T.astype(BF16)
            doT[...] = do_ref[...].astype(F32).T.astype(BF16)
            dq1_ref[...] = jnp.zeros_like(dq1_ref)
            if has_bias:
                drow_ref[...] = jnp.zeros_like(drow_ref)
            if two:
                qbuf[:, LANES:2 * LANES] = q2_ref[...].astype(BF16)
                qT[LANES:2 * LANES, :] = q2_ref[...].astype(F32).T.astype(BF16)
                dq2_ref[...] = jnp.zeros_like(dq2_ref)
            if softmax:
                def drow(t, carry):
                    rows = pl.ds(pl.multiple_of(t * B, B), B)
                    delta[rows, :] = jnp.sum(do_ref[rows, :].astype(F32) * o_ref[rows, :], axis=1, keepdims=True)
                    return carry
                lax.fori_loop(0, nb, drow, 0)

        kj = k1_ref[...].astype(BF16)
        if two:
            kj = jnp.concatenate([kj, k2_ref[...].astype(BF16)], axis=1)
        vj = v_ref[...].astype(BF16)
        kbj = kb_ref[...] if has_bias else None
        lg = _ret_log_gamma(h) if mode == "ret" else None
        dk_acc[...] = jnp.zeros_like(dk_acc)
        dv_acc[...] = jnp.zeros_like(dv_acc)
        if has_bias:
            dkb_acc[...] = jnp.zeros_like(dkb_acc)

        def step(i, diag):
            rows = slice(i * B, (i + 1) * B)
            qi = qbuf[rows, :]
            doi = dobuf[rows, :]
            s = lax.dot_general(qi, kj, NT_DIMS, preferred_element_type=F32)
            dp = lax.dot_general(doi, vj, NT_DIMS, preferred_element_type=F32)
            if softmax:
                s = s * scale
                if has_bias:
                    s = s + kbj
                if diag:
                    s = jnp.where(_visible(mode, B), s, NEG_BIG)
                p = jnp.exp(s - lse_ref[rows, 0:1])
                ds = p * (dp - delta[rows, :])
                if has_bias:
                    dkb_acc[...] += jnp.sum(ds, axis=0, keepdims=True)
                    drow_ref[rows, :] += jnp.broadcast_to(jnp.sum(ds, axis=1, keepdims=True), (B, LANES))
                dsb = (ds * scale).astype(BF16)
            else:
                if diag:
                    dec = jnp.where(_visible(mode, B), _decay(lg, B, 0), 0.0)
                else:
                    dec = _decay(lg, B, i - j)
                p = s * dec
                dsb = (dp * dec).astype(BF16)
            dv_acc[...] += jnp.dot(doT[:, rows], p.astype(BF16), preferred_element_type=F32)
            dk_acc[...] += jnp.dot(qT[:, rows], dsb, preferred_element_type=F32)
            dq = jnp.dot(dsb, kj, preferred_element_type=F32)
            dq1_ref[rows, :] += dq[:, 0:LANES]
            if two:
                dq2_ref[rows, :] += dq[:, LANES:2 * LANES]

        for i in range(nb):
            pl.when(i == j)(functools.partial(step, i, True))
            pl.when(i > j)(functools.partial(step, i, False))
        dk1_ref[...] = dk_acc[0:LANES, :].T
        dv_ref[...] = dv_acc[...].T
        if two:
            dk2_ref[...] = dk_acc[LANES:2 * LANES, :].T
        if has_bias:
            dkb_ref[...] = dkb_acc[...]

    full = lambda w, cb: pl.BlockSpec((S, w), lambda h, j: (0, cb + h))
    blk = lambda w, cb: pl.BlockSpec((B, w), lambda h, j: (j, cb + h))
    in_specs = [full(LANES, q1_cb), blk(LANES, k1_cb), blk(dv, v_cb), full(dv, 0)]
    args = [q1, k1, v, do]
    if softmax:
        in_specs += [full(dv, 0), full(LANES, 0)]
        args += [o, lse]
    if two:
        in_specs += [full(LANES, q2_cb), pl.BlockSpec((B, LANES), lambda h, j: (j, 0))]
        args += [q2, k2]
    if has_bias:
        in_specs.append(pl.BlockSpec((None, None, 1, B), lambda h, j: (h, j, 0, 0)))
        args.append(kbias)
    names = ["dq1", "dk1", "dv"]
    out_shape = [jax.ShapeDtypeStruct((S, H * LANES), F32), jax.ShapeDtypeStruct((S, H * LANES), F32),
                 jax.ShapeDtypeStruct((S, H * dv), F32)]
    out_specs = [full(LANES, 0), blk(LANES, 0), blk(dv, 0)]
    if two:
        names += ["dq2", "dk2h"]
        out_shape += [jax.ShapeDtypeStruct((S, H * LANES), F32)] * 2
        out_specs += [full(LANES, 0), blk(LANES, 0)]
    if has_bias:
        names.append("dkb")
        out_shape.append(jax.ShapeDtypeStruct((H, nb, 1, B), F32))
        out_specs.append(pl.BlockSpec((None, None, 1, B), lambda h, j: (h, j, 0, 0)))
        names.append("drow")
        out_shape.append(jax.ShapeDtypeStruct((S, H * LANES), F32))
        out_specs.append(full(LANES, 0))
    kw = 2 * LANES if two else LANES
    scratch = [pltpu.VMEM((S, kw), BF16), pltpu.VMEM((S, dv), BF16),
               pltpu.VMEM((kw, S), BF16), pltpu.VMEM((dv, S), BF16)]
    if softmax:
        scratch.append(pltpu.VMEM((S, 1), F32))
    scratch += [pltpu.VMEM((kw, B), F32), pltpu.VMEM((dv, B), F32)]
    if has_bias:
        scratch.append(pltpu.VMEM((1, B), F32))
    res = pl.pallas_call(body, out_shape=tuple(out_shape), grid=(H, nb), in_specs=in_specs,
                         out_specs=tuple(out_specs), scratch_shapes=scratch, name=name,
                         compiler_params=_cparams("parallel", "arbitrary"))(*args)
    return dict(zip(names, res))


def head_sum(name, x, H, out_dtype):
    S = x.shape[0]
    tr = _pick(S, 512, 8)

    def body(x_ref, o_ref):
        acc = x_ref[:, 0:LANES]
        for h in range(1, H):
            acc = acc + x_ref[:, h * LANES:(h + 1) * LANES]
        o_ref[...] = acc.astype(o_ref.dtype)

    return pl.pallas_call(
        body, out_shape=jax.ShapeDtypeStruct((S, LANES), out_dtype), grid=(S // tr,),
        in_specs=[pl.BlockSpec((tr, H * LANES), lambda i: (i, 0))],
        out_specs=pl.BlockSpec((tr, LANES), lambda i: (i, 0)), name=name,
        compiler_params=_cparams("parallel"))(x)


def _mesh_pos():
    return lax.axis_index("x"), lax.axis_index("y"), lax.axis_index("c")


def _peer(pos, k):
    x, y, c = pos
    px = 1 - x if k & 4 else x
    py = 1 - y if k & 2 else y
    pc = 1 - c if k & 1 else c
    return (px, py, pc), 4 * px + 2 * py + pc


def exchange(name, tensors):
    nt = len(tensors)
    flat_in, counts = [], []
    out_shape = []
    for mode, srcs in tensors:
        counts.append(len(srcs))
        flat_in += list(srcs)
        rc = srcs[0].shape[-2:]
        out_shape.append(jax.ShapeDtypeStruct((len(srcs), N_DEV) + tuple(rc), srcs[0].dtype))
    n_in = len(flat_in)

    def body(*refs):
        ins = refs[:n_in]
        outs = refs[n_in:n_in + nt]
        send_sems, recv_sems, local_sems = refs[n_in + nt:]
        pos = _mesh_pos()
        me = 4 * pos[0] + 2 * pos[1] + pos[2]
        srcs_of, base = [], 0
        for t in range(nt):
            srcs_of.append(ins[base:base + counts[t]])
            base += counts[t]

        def src_view(t, l, slot):
            ref = srcs_of[t][l]
            return ref if tensors[t][0] == "gather" else ref.at[slot]

        def all_layers(t, slot):
            return outs[t].at[pl.ds(0, counts[t]), slot]

        for t in range(nt):
            for l in range(counts[t]):
                pltpu.make_async_copy(src_view(t, l, me), outs[t].at[l, me], local_sems.at[t]).start()
        for t in range(nt):
            for k in range(1, N_DEV):
                peer, pid = _peer(pos, k)
                for l in range(counts[t]):
                    pltpu.make_async_remote_copy(
                        src_ref=src_view(t, l, pid), dst_ref=outs[t].at[l, me],
                        send_sem=send_sems.at[t, k - 1], recv_sem=recv_sems.at[t, k - 1],
                        device_id=peer, device_id_type=pl.DeviceIdType.MESH).start()
        for t in range(nt):
            for k in range(1, N_DEV):
                peer, pid = _peer(pos, k)
                pltpu.make_async_remote_copy(
                    src_ref=all_layers(t, pid), dst_ref=all_layers(t, pid),
                    send_sem=send_sems.at[t, k - 1], recv_sem=recv_sems.at[t, k - 1],
                    device_id=peer, device_id_type=pl.DeviceIdType.MESH).wait()
        for t in range(nt):
            pltpu.make_async_copy(all_layers(t, me), all_layers(t, me), local_sems.at[t]).wait()

    any_spec = pl.BlockSpec(memory_space=pl.ANY)
    return pl.pallas_call(
        body, out_shape=tuple(out_shape), in_specs=[any_spec] * n_in, out_specs=tuple([any_spec] * nt),
        scratch_shapes=[pltpu.SemaphoreType.DMA((nt, N_DEV - 1)), pltpu.SemaphoreType.DMA((nt, N_DEV - 1)),
                        pltpu.SemaphoreType.DMA((nt,))],
        name=name)(*flat_in)


HBM_SPEC = pl.BlockSpec(memory_space=pltpu.HBM)
SEM_SPEC = pl.BlockSpec(memory_space=pltpu.SEMAPHORE)
DATAFLOW = pltpu.SideEffectType.DATAFLOW_SIDE_EFFECTING


def _hbm(a):
    return pltpu.with_memory_space_constraint(a, pltpu.HBM)


def landing_zones(mode, srcs):
    pos = _mesh_pos()
    me = 4 * pos[0] + 2 * pos[1] + pos[2]
    lands = []
    for s in srcs:
        R, C = s.shape[-2:]
        own = s[None] if mode == "gather" else lax.dynamic_slice(s, (me, 0, 0), (1, R, C))
        lands.append(lax.dynamic_update_slice(lax.empty((N_DEV, R, C), s.dtype), own, (me, 0, 0)))
    return lands


def exchange_start(name, mode, srcs, lands, after=None):
    n = len(srcs)
    extra = [] if after is None else [after]

    def body(*refs):
        src_refs, land_refs = refs[:n], refs[n:2 * n]
        send_sems, recv_sems = refs[2 * n + len(extra)], refs[2 * n + len(extra) + 1]
        token = refs[-1]
        pos = _mesh_pos()
        me = 4 * pos[0] + 2 * pos[1] + pos[2]
        for t in range(n):
            for k in range(1, N_DEV):
                peer, pid = _peer(pos, k)
                src = src_refs[t] if mode == "gather" else src_refs[t].at[pid]
                pltpu.make_async_remote_copy(
                    src_ref=src, dst_ref=land_refs[t].at[me], send_sem=send_sems.at[t], recv_sem=recv_sems.at[t],
                    device_id=peer, device_id_type=pl.DeviceIdType.MESH).start()
        token[...] = jnp.zeros_like(token)

    thru = [pltpu.HBM(a.shape, a.dtype) for a in list(srcs) + list(lands)]
    out_shape = (pltpu.SemaphoreType.DMA((n,)), pltpu.SemaphoreType.DMA((n,)), *thru,
                 jax.ShapeDtypeStruct((8, LANES), F32))
    res = pl.pallas_call(
        body, out_shape=out_shape, in_specs=[HBM_SPEC] * (2 * n) + [pl.BlockSpec(memory_space=pl.ANY)] * len(extra),
        out_specs=(SEM_SPEC, SEM_SPEC, *([HBM_SPEC] * (2 * n)), pl.BlockSpec(memory_space=pltpu.VMEM)),
        input_output_aliases={i: 2 + i for i in range(2 * n)}, name=name,
        compiler_params=pltpu.CompilerParams(has_side_effects=DATAFLOW))(
            *[_hbm(a) for a in list(srcs) + list(lands)], *extra)
    return res[0], res[1], list(res[2:2 + n]), list(res[2 + n:2 + 2 * n]), res[-1]


def exchange_wait(name, send_sems, recv_sems, srcs, lands, after):
    n = len(srcs)

    def body(*refs):
        land_refs = refs[n:2 * n]
        s_sems, r_sems = refs[2 * n], refs[2 * n + 1]
        pos = _mesh_pos()
        for t in range(n):
            seven = land_refs[t].at[pl.ds(0, N_DEV - 1)]
            cp = pltpu.make_async_remote_copy(
                src_ref=seven, dst_ref=seven, send_sem=s_sems.at[t], recv_sem=r_sems.at[t],
                device_id=pos, device_id_type=pl.DeviceIdType.MESH)
            cp.wait_send()
            cp.wait_recv()

    arrs = list(srcs) + list(lands)
    afters = list(after) if isinstance(after, (list, tuple)) else [after]
    res = pl.pallas_call(
        body, out_shape=tuple(pltpu.HBM(a.shape, a.dtype) for a in arrs),
        in_specs=[HBM_SPEC] * (2 * n) + [SEM_SPEC, SEM_SPEC] + [pl.BlockSpec(memory_space=pl.ANY)] * len(afters),
        out_specs=tuple([HBM_SPEC] * (2 * n)), input_output_aliases={i: i for i in range(2 * n)}, name=name,
        compiler_params=pltpu.CompilerParams(has_side_effects=DATAFLOW))(*arrs, send_sems, recv_sems, *afters)
    return list(res[n:])


def reduce_parts(name, parts):
    n, R, C = parts.shape
    tr = _pick(R, max(8, (1 << 20) // (C * 4) // 8 * 8), 8)

    def body(p_ref, o_ref):
        acc = p_ref[0].astype(F32)
        for s in range(1, n):
            acc = acc + p_ref[s].astype(F32)
        o_ref[...] = acc

    return pl.pallas_call(
        body, out_shape=jax.ShapeDtypeStruct((R, C), F32), grid=(R // tr,),
        in_specs=[pl.BlockSpec((n, tr, C), lambda i: (0, i, 0))],
        out_specs=pl.BlockSpec((tr, C), lambda i: (i, 0)), name=name,
        compiler_params=_cparams("parallel"))(parts)


def adamw(name, w, m, v, parts, first=0, prev=None):
    L, R, C = w.shape
    nl = len(parts)
    n = parts[0].shape[0]
    tr = _pick(R, max(8, (1 << 19) // (C * 4) // 8 * 8), 8)
    n_prev = 0 if prev is None else 4

    def body(*refs):
        w_ref, m_ref, v_ref = refs[:3]
        p_refs = refs[3:3 + nl]
        g_ref, d_ref, nm_ref, nv_ref = refs[3 + nl + n_prev:]

        def update(p_ref):
            g = p_ref[0].astype(F32)
            for s in range(1, n):
                g = g + p_ref[s].astype(F32)
            wv = w_ref[...]
            mn = ADAM_B1 * m_ref[...] + (1.0 - ADAM_B1) * g
            vn = ADAM_B2 * v_ref[...] + (1.0 - ADAM_B2) * jnp.square(g)
            m_hat = mn / (1.0 - ADAM_B1 ** ADAM_STEP)
            v_hat = vn / (1.0 - ADAM_B2 ** ADAM_STEP)
            g_ref[...] = g
            d_ref[...] = -ADAM_LR * (m_hat / (jnp.sqrt(v_hat) + ADAM_EPS) + ADAM_WD * wv)
            nm_ref[...] = mn
            nv_ref[...] = vn

        for k in range(nl):
            pl.when(pl.program_id(0) == k)(functools.partial(update, p_refs[k]))

    blk = pl.BlockSpec((None, tr, C), lambda l, i: (first + l, i, 0))
    pspecs = [pl.BlockSpec((n, tr, C), lambda l, i, k=k: (0, jnp.where(l == k, i, 0), 0)) for k in range(nl)]
    sh = jax.ShapeDtypeStruct((L, R, C), F32)
    prev_args = [] if prev is None else list(prev)
    return pl.pallas_call(
        body, out_shape=(sh, sh, sh, sh), grid=(nl, R // tr),
        in_specs=[blk, blk, blk] + pspecs + [pl.BlockSpec(memory_space=pl.ANY)] * n_prev,
        out_specs=(blk, blk, blk, blk), input_output_aliases={3 + nl + q: q for q in range(n_prev)}, name=name,
        compiler_params=_cparams("arbitrary", "arbitrary"))(w, m, v, *parts, *prev_args)


def _cols_from_blocks(g):
    n, R, c = g.shape
    return g.transpose(1, 0, 2).reshape(R, n * c)


def _cols_to_blocks(w):
    R, C = w.shape
    return w.reshape(R, N_DEV, C // N_DEV).transpose(1, 0, 2)


def _uq_permute(w):
    lead = w.shape[:-1]
    w4 = w.reshape(lead + (MLA_HEADS, MLA_NOPE + MLA_ROPE))
    nope = w4[..., :MLA_NOPE].reshape(lead + (MLA_HEADS * MLA_NOPE,))
    rope = jnp.pad(w4[..., MLA_NOPE:], [(0, 0)] * (w4.ndim - 1) + [(0, LANES - MLA_ROPE)])
    return jnp.concatenate([nope, rope.reshape(lead + (MLA_HEADS * LANES,))], axis=-1)


def _uq_unpermute(w):
    lead = w.shape[:-1]
    n = MLA_HEADS * MLA_NOPE
    nope = w[..., :n].reshape(lead + (MLA_HEADS, MLA_NOPE))
    rope = w[..., n:].reshape(lead + (MLA_HEADS, LANES))[..., :MLA_ROPE]
    return jnp.concatenate([nope, rope], axis=-1).reshape(lead + (MLA_HEADS * (MLA_NOPE + MLA_ROPE),))


def _ukv_permute(w):
    lead = w.shape[:-1]
    w4 = w.reshape(lead + (MLA_HEADS, 2, MLA_NOPE))
    return jnp.swapaxes(w4, -3, -2).reshape(lead + (2 * MLA_HEADS * MLA_NOPE,))


def _ukv_unpermute(w):
    lead = w.shape[:-1]
    w4 = w.reshape(lead + (2, MLA_HEADS, MLA_NOPE))
    return jnp.swapaxes(w4, -3, -2).reshape(lead + (2 * MLA_HEADS * MLA_NOPE,))


SMALL = ["norm1_g", "mla_q_norm_g", "mla_kv_norm_g", "fox_b_f", "norm2_g", "ffn_conv_b", "final_norm_g"]
SMALL_TILE = 8 * LANES


def _pack_small(d):
    flat = jnp.concatenate([d[n].reshape(-1).astype(F32) for n in SMALL])
    pad = -flat.shape[0] % SMALL_TILE
    return jnp.pad(flat, (0, pad)).reshape(-1, LANES)


def _unpack_small(packed, like):
    flat = packed.reshape(-1)
    out, o = {}, 0
    for n in SMALL:
        sz = int(np.prod(like[n].shape))
        out[n] = flat[o:o + sz].reshape(like[n].shape)
        o += sz
    return out


WEIGHTS = ["norm1_g", "w_in", "mla_q_norm_g", "mla_kv_norm_g", "mla_w_uq", "mla_w_ukv", "fox_b_f", "w_br_fox",
           "w_br_mla", "w_br_ret", "w_out", "norm2_g", "ffn_w_up", "ffn_w_gate", "ffn_conv_w", "ffn_conv_b",
           "ffn_w_down", "final_norm_g"]
EARLY = ["w_in", "mla_w_uq", "mla_w_ukv"]
LATE = ["w_br_fox", "w_br_mla", "w_br_ret", "w_out", "ffn_w_up", "ffn_w_gate", "ffn_conv_w", "ffn_w_down"]
FFN = ["ffn_w_up", "ffn_w_gate", "ffn_conv_w", "ffn_w_down"]
TRANSPOSED = ("ffn_w_up", "ffn_w_gate")
BIG = EARLY + LATE
X_EARLY = ["w_in_mix", "mla_w_uq", "mla_w_ukv"]
X_LATE = ["w_in_gates"] + LATE
X_MID = ["w_in_gates", "w_out", "w_br_fox", "w_br_mla", "w_br_ret"]
X_REST = ["mla_w_uq", "mla_w_ukv", "w_in_mix"]


def kernel(x, norm1_g, w_in, mla_q_norm_g, mla_kv_norm_g, mla_w_uq, mla_w_ukv, fox_b_f, w_br_fox, w_br_mla, w_br_ret, w_out, norm2_g, ffn_w_up, ffn_w_gate, ffn_conv_w, ffn_conv_b, ffn_w_down, final_norm_g, loss_target, m_norm1_g, m_w_in, m_mla_q_norm_g, m_mla_kv_norm_g, m_mla_w_uq, m_mla_w_ukv, m_fox_b_f, m_w_br_fox, m_w_br_mla, m_w_br_ret, m_w_out, m_norm2_g, m_ffn_w_up, m_ffn_w_gate, m_ffn_conv_w, m_ffn_conv_b, m_ffn_w_down, m_final_norm_g, v_norm1_g, v_w_in, v_mla_q_norm_g, v_mla_kv_norm_g, v_mla_w_uq, v_mla_w_ukv, v_fox_b_f, v_w_br_fox, v_w_br_mla, v_w_br_ret, v_w_out, v_norm2_g, v_ffn_w_up, v_ffn_w_gate, v_ffn_conv_w, v_ffn_conv_b, v_ffn_w_down, v_final_norm_g):
    env = dict(locals())
    W = {n: env[n] for n in WEIGHTS}
    Mo = {n: env["m_" + n] for n in WEIGHTS}
    Vo = {n: env["v_" + n] for n in WEIGHTS}
    S, D = x.shape[1], x.shape[2]
    L = w_in.shape[0]
    lay = InLayout(D)
    NP = lay.total
    f = ffn_w_up.shape[-1]
    xs = x.reshape(S, D)
    tgt = loss_target.reshape(S, D)

    local = {n: W[n].astype(BF16) for n in BIG if n != "w_in"}
    w_gates, w_mix = lay.permute(W["w_in"])
    local["w_in_gates"], local["w_in_mix"] = w_gates.astype(BF16), w_mix.astype(BF16)
    pending = {}
    token = None
    for l in range(L):
        for grp, names in (("a", X_EARLY), ("b", X_LATE)):
            srcs = [local[n][l] for n in names]
            *flight, token = exchange_start(f"gather_start_{l}{grp}", "gather", srcs, landing_zones("gather", srcs),
                                            token)
            pending[l, grp] = flight
    gather_token = token
    cbias_all = ffn_conv_b.reshape(L, 1, N_DEV, 1, f)

    def early_weights(l, after):
        g = dict(zip(X_EARLY, exchange_wait(f"gather_wait_{l}a", *pending[l, "a"], after)))
        return dict(Win=g["w_in_mix"].reshape(1, D, NP), Wuq=_uq_permute(_cols_from_blocks(g["mla_w_uq"])),
                    Wukv=_ukv_permute(_cols_from_blocks(g["mla_w_ukv"])))

    def late_weights(l, after):
        g = dict(zip(X_LATE, exchange_wait(f"gather_wait_{l}b", *pending[l, "b"], after)))
        return dict(
            Wgates=g["w_in_gates"].reshape(1, D, 3 * D),
            Wout=g["w_out"].reshape(1, D, D), Wbf=_cols_from_blocks(g["w_br_fox"]),
            Wbm=_cols_from_blocks(g["w_br_mla"]), Wbr=_cols_from_blocks(g["w_br_ret"]), Wup=g["ffn_w_up"][None],
            Wgate=g["ffn_w_gate"][None], Wdown=g["ffn_w_down"][None], Wconv=g["ffn_conv_w"].astype(F32)[None],
            cbias=cbias_all[l])

    tab64 = rope_tables(S, MLA_ROPE)
    tab128 = rope_tables(S, RET_DK)
    fox_scale = FOX_DH ** -0.5
    mla_scale = (MLA_NOPE + MLA_ROPE) ** -0.5
    ret_kscale = RET_DK ** -0.5
    R = S // LANES
    AB = _attn_block(S)
    NOPE_W = MLA_HEADS * MLA_NOPE

    def vec(a):
        return a.reshape(1, -1)

    saved = []
    xc = xs
    for l in range(L):
        Wl = early_weights(l, gather_token if l == 0 else xc)
        Win, Wuq, Wukv = Wl["Win"], Wl["Wuq"], Wl["Wukv"]
        s = {"x": xc, "W": Wl}
        h1 = rms_fwd("norm1", xc, 0, D, vec(norm1_g[l]), BF16)
        P = mm_nn("in_proj", h1, Win, F32, b_lead=0)
        s.update(h1=h1, P=P)
        ff_off = lay.off["ff"]
        ft = P[:, ff_off:ff_off + FOX_HEADS].T.reshape(FOX_HEADS, R, LANES)
        bfl = jnp.broadcast_to(fox_b_f[l].reshape(FOX_HEADS, 1, 1), (FOX_HEADS, 1, LANES))
        kbias = fox_gate_fwd("fox_gate", ft, bfl).reshape(FOX_HEADS, S // AB, 1, AB)
        o_fox, lse_fox = attn_fwd("fox_attn", "fox", P, lay.cb("fq", LANES), P, lay.cb("fk", LANES),
                                  P, lay.cb("fv", LANES), FOX_HEADS, FOX_DH, fox_scale, kbias=kbias)
        s.update(ft=ft, bfl=bfl, kbias=kbias, o_fox=o_fox, lse_fox=lse_fox)
        cqn = rms_fwd("mla_q_norm", P, lay.cb("mq", MLA_Q_LORA), MLA_Q_LORA, vec(mla_q_norm_g[l]), BF16)
        qall = mm_nn("mla_uq", cqn, Wuq, F32)
        ckvn = rms_fwd("mla_kv_norm", P, lay.cb("mkv", MLA_KV_LORA), MLA_KV_LORA, vec(mla_kv_norm_g[l]), BF16)
        kvall = mm_nn("mla_ukv", ckvn, Wukv, F32)
        qrope = rope_apply("mla_q_rope", qall, NOPE_W // LANES, MLA_HEADS, tab64, 1.0, F32)
        krope = rope_apply("mla_k_rope", P, lay.cb("mkr", LANES), 1, tab64, 1.0, F32)
        o_mla, lse_mla = attn_fwd("mla_attn", "mla", qall, 0, kvall, 0, kvall, NOPE_W // MLA_V, MLA_HEADS, MLA_V,
                                  mla_scale, q2=qrope, q2_cb=0, k2=krope)
        s.update(cqn=cqn, qall=qall, ckvn=ckvn, kvall=kvall, qrope=qrope, krope=krope, o_mla=o_mla,
                 lse_mla=lse_mla)
        rq = rope_apply("ret_q_rope", P, lay.cb("rq", LANES), RET_HEADS, tab128, 1.0, F32)
        rk = rope_apply("ret_k_rope", P, lay.cb("rk", LANES), RET_HEADS, tab128, ret_kscale, F32)
        o_ret, _ = attn_fwd("ret_attn", "ret", rq, 0, rk, 0, P, lay.cb("rv", RET_DV), RET_HEADS, RET_DV, 1.0)
        c_ret = ret_out_fwd("ret_out", o_ret, P, lay.cb("rg", RET_DV), BF16)
        s.update(rq=rq, rk=rk, o_ret=o_ret, c_ret=c_ret)
        Wl.update(late_weights(l, (o_fox, o_mla, c_ret)))
        Wout, Wbf, Wbm, Wbr = Wl["Wout"], Wl["Wbf"], Wl["Wbm"], Wl["Wbr"]
        Wup, Wgate, Wdown, Wconv, cbias = (Wl[k] for k in ("Wup", "Wgate", "Wdown", "Wconv", "cbias"))
        A = mm_nn("br_fox", o_fox, Wbf, F32)
        Bm = mm_nn("br_mla", o_mla, Wbm, F32)
        C = mm_nn("br_ret", c_ret, Wbr, F32)
        Pg = mm_nn("gate_proj", h1, Wl["Wgates"], F32, b_lead=0)
        s["Pg"] = Pg
        merged = merge_fwd("merge", Pg, 0, A, Bm, C, BF16)
        x2 = mm_nn("out_proj", merged, Wout, F32, b_lead=0, res=xc)
        s.update(A=A, Bm=Bm, C=C, merged=merged, x2=x2)
        h2 = rms_fwd("norm2", x2, 0, D, vec(norm2_g[l]), BF16)
        u = ffn_up("ffn_up", h2, Wup, 0, F32)
        gt = ffn_up("ffn_gate", h2, Wgate, 0, F32)
        act = ffn_act_fwd("ffn_act", u, gt, Wconv, cbias, 0, BF16)
        xc = ffn_down("ffn_down", act, Wdown, 0, x2, F32)
        s.update(h2=h2, u=u, gt=gt, act=act)
        saved.append(s)

    loss_tile, dx, dgf = loss_head("loss_head", xc, vec(final_norm_g), tgt)
    loss = lax.psum(loss_tile[0, 0], ("x", "y", "c"))

    gbig = {n: [None] * L for n in BIG + ["w_in_gates", "w_in_mix"]}
    gsmall = {n: [None] * L for n in SMALL if n != "final_norm_g"}
    scattering = {}
    scatter_token = None

    def start_scatter(name, names, l):
        srcs = [gbig[n][l] for n in names]
        *flight, tok = exchange_start(name, "scatter", srcs, landing_zones("scatter", srcs))
        return flight, tok

    for l in reversed(range(L)):
        s = saved[l]
        P = s["P"]
        Wl = s["W"]
        Win, Wout, Wuq, Wukv, Wbf, Wbm, Wbr = (Wl[k] for k in ("Win", "Wout", "Wuq", "Wukv", "Wbf", "Wbm", "Wbr"))
        Wup, Wgate, Wdown, Wconv, cbias = (Wl[k] for k in ("Wup", "Wgate", "Wdown", "Wconv", "cbias"))
        dxb = (dx if scatter_token is None else dx + scatter_token[0, 0]).astype(BF16)
        dact = ffn_down_bwd_act("ffn_down_da", dxb, Wdown, 0, BF16)
        gbig["ffn_w_down"][l] = ffn_down_bwd_w("ffn_down_dw", s["act"], dxb, BF16)
        g, dgt = ffn_act_bwd_point("ffn_act_bwd", s["u"], s["gt"], Wconv, cbias, 0, dact)
        du, dcw, dcb = ffn_act_bwd_conv("ffn_conv_bwd", s["u"], g, Wconv, 0)
        gbig["ffn_conv_w"][l] = dcw.astype(BF16)
        gsmall["ffn_conv_b"][l] = dcb.reshape(-1)
        gbig["ffn_w_up"][l] = ffn_down_bwd_w("ffn_up_dw", du, s["h2"], BF16)
        gbig["ffn_w_gate"][l] = ffn_down_bwd_w("ffn_gate_dw", dgt, s["h2"], BF16)
        dh2 = ffn_up_bwd_h("ffn_up_dh", du, Wup, 0, None, F32)
        dh2 = ffn_up_bwd_h("ffn_gate_dh", dgt, Wgate, 0, dh2, BF16)
        dx2, dg2 = rms_bwd("norm2_bwd", s["x2"], 0, D, vec(norm2_g[l]), dh2, F32, res=dx)
        gsmall["norm2_g"][l] = dg2.reshape(-1)
        scattering[l, "ffn"], scatter_token = start_scatter(f"scatter_start_{l}ffn", FFN, l)
        dx2b = (dx2 + scatter_token[0, 0]).astype(BF16)
        dmerged = mm_nt("out_proj_dm", dx2b, Wout, BF16, b_lead=0)
        gbig["w_out"][l] = mm_tn("out_proj_dw", s["merged"], dx2b, BF16).reshape(N_DEV, D // N_DEV, D)
        dgates, dA, dB, dC = merge_bwd("merge_bwd", s["Pg"], 0, s["A"], s["Bm"], s["C"], dmerged)
        gbig["w_br_fox"][l] = _cols_to_blocks(mm_tn("br_fox_dw", s["o_fox"], dA, BF16))
        gbig["w_br_mla"][l] = _cols_to_blocks(mm_tn("br_mla_dw", s["o_mla"], dB, BF16))
        gbig["w_br_ret"][l] = _cols_to_blocks(mm_tn("br_ret_dw", s["c_ret"], dC, BF16))
        gbig["w_in_gates"][l] = mm_tn("gate_proj_dw", s["h1"], dgates, BF16).reshape(N_DEV, D // N_DEV, 3 * D)
        scattering[l, "mid"], scatter_token = start_scatter(f"scatter_start_{l}mid", X_MID, l)
        dh1_gates = mm_nt("gate_proj_dh", dgates, Wl["Wgates"], F32, b_lead=0)
        do_fox = mm_nt("br_fox_do", dA + scatter_token[0, 0].astype(BF16), Wbf, F32)
        do_mla = mm_nt("br_mla_do", dB, Wbm, F32)
        dc_ret = mm_nt("br_ret_do", dC, Wbr, BF16)
        do_ret, drg = ret_out_bwd("ret_out_bwd", s["o_ret"], P, lay.cb("rg", RET_DV), dc_ret)
        rb = attn_bwd("ret_attn_bwd", "ret", s["rq"], 0, s["rk"], 0, P, lay.cb("rv", RET_DV), RET_HEADS, RET_DV,
                      1.0, do_ret)
        drq = rope_apply("ret_q_rope_bwd", rb["dq1"], 0, RET_HEADS, tab128, 1.0, BF16, transpose=True)
        drk = rope_apply("ret_k_rope_bwd", rb["dk1"], 0, RET_HEADS, tab128, ret_kscale, BF16, transpose=True)
        drv = rb["dv"].astype(BF16)
        mb = attn_bwd("mla_attn_bwd", "mla", s["qall"], 0, s["kvall"], 0, s["kvall"], NOPE_W // MLA_V, MLA_HEADS,
                      MLA_V, mla_scale, do_mla, o=s["o_mla"], lse=s["lse_mla"], q2=s["qrope"], q2_cb=0,
                      k2=s["krope"])
        dqrope = rope_apply("mla_q_rope_bwd", mb["dq2"], 0, MLA_HEADS, tab64, 1.0, BF16, transpose=True)
        dkr_sum = head_sum("mla_k_rope_sum", mb["dk2h"], MLA_HEADS, F32)
        dmkr = rope_apply("mla_k_rope_bwd", dkr_sum, 0, 1, tab64, 1.0, BF16, transpose=True)
        dqall = jnp.concatenate([mb["dq1"].astype(BF16), dqrope], axis=1)
        dkvall = jnp.concatenate([mb["dk1"].astype(BF16), mb["dv"].astype(BF16)], axis=1)
        dcqn = mm_nt("mla_uq_dx", dqall, Wuq, F32)
        dckvn = mm_nt("mla_ukv_dx", dkvall, Wukv, F32)
        guq = _uq_unpermute(mm_tn("mla_uq_dw", s["cqn"], dqall, BF16))
        gukv = _ukv_unpermute(mm_tn("mla_ukv_dw", s["ckvn"], dkvall, BF16))
        gbig["mla_w_uq"][l] = _cols_to_blocks(guq)
        gbig["mla_w_ukv"][l] = _cols_to_blocks(gukv)
        dmq, dgq = rms_bwd("mla_q_norm_bwd", P, lay.cb("mq", MLA_Q_LORA), MLA_Q_LORA, vec(mla_q_norm_g[l]),
                           dcqn, BF16)
        dmkv, dgkv = rms_bwd("mla_kv_norm_bwd", P, lay.cb("mkv", MLA_KV_LORA), MLA_KV_LORA,
                             vec(mla_kv_norm_g[l]), dckvn, BF16)
        gsmall["mla_q_norm_g"][l] = dgq.reshape(-1)
        gsmall["mla_kv_norm_g"][l] = dgkv.reshape(-1)
        fb = attn_bwd("fox_attn_bwd", "fox", P, lay.cb("fq", LANES), P, lay.cb("fk", LANES), P,
                      lay.cb("fv", LANES), FOX_HEADS, FOX_DH, fox_scale, do_fox, o=s["o_fox"], lse=s["lse_fox"],
                      kbias=s["kbias"])
        drow = fb["drow"].reshape(S, FOX_HEADS, LANES)[:, :, 0].T.reshape(FOX_HEADS, R, LANES)
        dft, dbf = fox_gate_bwd("fox_gate_bwd", s["ft"], s["bfl"], fb["dkb"].reshape(FOX_HEADS, R, LANES) - drow)
        gsmall["fox_b_f"][l] = dbf[:, 0, 0]
        dff = jnp.pad(dft.reshape(FOX_HEADS, S).T, ((0, 0), (0, LANES - FOX_HEADS))).astype(BF16)
        segs = dict(rv=drv, rg=drg, mq=dmq, rq=drq, rk=drk, mkv=dmkv, fq=fb["dq1"].astype(BF16),
                    fk=fb["dk1"].astype(BF16), fv=fb["dv"].astype(BF16), mkr=dmkr, ff=dff)
        dP = jnp.concatenate([segs[n] for n in lay.order], axis=1)
        gbig["w_in_mix"][l] = mm_tn("in_proj_dw", s["h1"], dP, BF16).reshape(N_DEV, D // N_DEV, NP)
        scattering[l, "rest"], scatter_token = start_scatter(f"scatter_start_{l}rest", X_REST, l)
        dh1 = mm_nt("in_proj_dh", dP, Win, BF16, b_lead=0, res=dh1_gates)
        dx, dg1 = rms_bwd("norm1_bwd", s["x"], 0, D, vec(norm1_g[l]) + scatter_token[0:1, 0:1], dh1, F32, res=dx2)
        gsmall["norm1_g"][l] = dg1.reshape(-1)

    small_like = {n: W[n] for n in SMALL}
    small_part = {n: jnp.stack(gsmall[n]) for n in gsmall}
    small_part["final_norm_g"] = dgf.reshape(-1)
    small_recv = exchange("gather_small_grads", [("gather", [_pack_small(small_part)])])[0]
    ps = adamw("adamw_small", _pack_small(small_like)[None], _pack_small({n: Mo[n] for n in SMALL})[None],
               _pack_small({n: Vo[n] for n in SMALL})[None], [small_recv[0]])

    def received(l, after):
        r = {}
        for grp, names in (("ffn", FFN), ("mid", X_MID), ("rest", X_REST)):
            r.update(zip(names, exchange_wait(f"scatter_wait_{l}{grp}", *scattering[l, grp], after)))
        r["w_in"] = lay.unpermute(reduce_parts("w_in_gates_grad_sum", r["w_in_gates"]),
                                  reduce_parts("w_in_mix_grad_sum", r["w_in_mix"]))[None]
        return r

    def oriented(n, a):
        return jnp.swapaxes(a, 1, 2) if n in TRANSPOSED else a

    out = {}
    if L > 1:
        recv = [received(l, dx) for l in range(1, L)]
        for n in BIG:
            out[n] = adamw("adamw_" + n, oriented(n, W[n]), oriented(n, Mo[n]), oriented(n, Vo[n]),
                           [r[n] for r in recv], first=1)
    recv0 = received(0, out[BIG[-1]][0] if L > 1 else dx)
    for n in BIG:
        res = adamw("adamw0_" + n, oriented(n, W[n]), oriented(n, Mo[n]), oriented(n, Vo[n]), [recv0[n]],
                    first=0, prev=out.get(n))
        out[n] = tuple(oriented(n, a) for a in res)
    small_out = [_unpack_small(a[0], small_like) for a in ps]
    for n in SMALL:
        out[n] = tuple(so[n] for so in small_out)

    grads = [out[n][0] for n in WEIGHTS]
    deltas = [out[n][1] for n in WEIGHTS]
    new_m = [out[n][2] for n in WEIGHTS]
    new_v = [out[n][3] for n in WEIGHTS]
    return (loss, dx.reshape(1, S, D), *grads, *deltas, *new_m, *new_v)
```

```python
import functools
import math

import numpy as np
import jax
import jax.numpy as jnp
from jax import lax
from jax.experimental import pallas as pl
from jax.experimental.pallas import tpu as pltpu

F32 = jnp.float32
BF16 = jnp.bfloat16

CHUNK = 64
NORM_EPS = 1e-6
ROPE_THETA = 10000.0
FOX_HEADS, FOX_DH = 6, 128
FOX_W = FOX_HEADS * FOX_DH
MLA_HEADS, MLA_NOPE, MLA_ROPE, MLA_V = 6, 128, 64, 128
MLA_Q_LORA, MLA_KV_LORA = 512, 256
MLA_W = MLA_HEADS * MLA_V
RET_HEADS, RET_DK, RET_DV = 4, 128, 256
RET_QK_W, RET_V_W = RET_HEADS * RET_DK, RET_HEADS * RET_DV
ADAM_LR, ADAM_B1, ADAM_B2, ADAM_EPS, ADAM_WD, ADAM_STEP = 0.001, 0.9, 0.999, 1e-08, 0.01, 10

N_DEV = 8
LANES = 128
V7X_VMEM_LIMIT_BYTES = 52 * 1024 * 1024
NEG_BIG = -1e30
HIGHEST = lax.Precision.HIGHEST

NT_DIMS = (((1,), (1,)), ((), ()))
TN_DIMS = (((0,), (0,)), ((), ()))
NN_DIMS = (((1,), (0,)), ((), ()))


def _pick(n, cap, mult=LANES):
    best = None
    for t in range(mult, min(n, cap) + 1, mult):
        if n % t == 0:
            best = t
    return n if best is None else best


def _cparams(*sem):
    return pltpu.CompilerParams(dimension_semantics=sem, vmem_limit_bytes=V7X_VMEM_LIMIT_BYTES)


class InLayout:
    def __init__(self, d_model):
        d = d_model
        self.d = d
        orig = dict(fq=(0, FOX_W), fk=(FOX_W, FOX_W), fv=(2 * FOX_W, FOX_W), ff=(3 * FOX_W, FOX_HEADS))
        o = 3 * FOX_W + FOX_HEADS
        for name, w in (("mq", MLA_Q_LORA), ("mkv", MLA_KV_LORA), ("mkr", MLA_ROPE), ("rq", RET_QK_W),
                        ("rk", RET_QK_W), ("rv", RET_V_W), ("rg", RET_V_W), ("gates", 3 * d)):
            orig[name] = (o, w)
            o += w
        self.orig = orig
        self.orig_width = o
        order = ["rv", "rg", "mq", "rq", "rk", "mkv", "fq", "fk", "fv", "mkr", "ff"]
        self.order = order
        self.off, self.width = {}, {}
        p = 0
        for name in order:
            w = orig[name][1]
            wp = -(-w // LANES) * LANES
            self.off[name], self.width[name] = p, wp
            p += wp
        self.total = p
        self.gates = 3 * d

    def cb(self, name, block):
        assert self.off[name] % block == 0, (name, block)
        return self.off[name] // block

    def permute(self, w):
        parts = []
        for name in self.order:
            o, n = self.orig[name]
            seg = w[..., o:o + n]
            pad = self.width[name] - n
            if pad:
                seg = jnp.pad(seg, [(0, 0)] * (w.ndim - 1) + [(0, pad)])
            parts.append(seg)
        o, n = self.orig["gates"]
        return w[..., o:o + n], jnp.concatenate(parts, axis=-1)

    def unpermute(self, gates, mix):
        names = sorted(self.order, key=lambda n: self.orig[n][0])
        return jnp.concatenate([mix[..., self.off[n]:self.off[n] + self.orig[n][1]] for n in names] + [gates],
                               axis=-1)


def _mm(name, a, b, out_shape, grid, a_spec, b_spec, o_spec, dims, acc_shape, res=None, nsub=0):
    nk = grid[-1]
    has_res = res is not None

    def body(*refs):
        if has_res:
            a_ref, b_ref, r_ref, o_ref = refs[:4]
        else:
            a_ref, b_ref, o_ref = refs[:3]
            r_ref = None
        if nsub:
            prod = lax.dot_general(a_ref[0].astype(BF16), b_ref[0].astype(BF16), dims, preferred_element_type=F32)
            for q in range(1, nsub):
                prod = prod + lax.dot_general(a_ref[q].astype(BF16), b_ref[q].astype(BF16), dims,
                                              preferred_element_type=F32)
        else:
            prod = lax.dot_general(a_ref[...].astype(BF16), b_ref[...].astype(BF16), dims,
                                   preferred_element_type=F32)
        if nk == 1:
            if has_res:
                prod = prod + r_ref[...].astype(F32)
            o_ref[...] = prod.astype(o_ref.dtype)
        else:
            acc_ref = refs[-1]
            k = pl.program_id(len(grid) - 1)

            @pl.when(k == 0)
            def _():
                acc_ref[...] = prod

            @pl.when(k > 0)
            def _():
                acc_ref[...] += prod

            @pl.when(k == nk - 1)
            def _():
                r = acc_ref[...]
                if has_res:
                    r = r + r_ref[...].astype(F32)
                o_ref[...] = r.astype(o_ref.dtype)

    in_specs = [a_spec, b_spec] + ([o_spec] if has_res else [])
    args = (a, b) + ((res,) if has_res else ())
    scratch = [pltpu.VMEM(acc_shape, F32)] if nk > 1 else []
    sem = ("parallel",) * (len(grid) - 1) + ("arbitrary",)
    return pl.pallas_call(body, out_shape=out_shape, grid=grid, in_specs=in_specs, out_specs=o_spec,
                          scratch_shapes=scratch, name=name, compiler_params=_cparams(*sem))(*args)


def mm_nn(name, a, b, out_dtype, b_lead=None, res=None):
    M, K = a.shape
    N = b.shape[-1]
    tm, tn, tk = _pick(M, 1024, 8), _pick(N, 1024), _pick(K, 2048)
    grid = (M // tm, N // tn, K // tk)
    a_spec = pl.BlockSpec((tm, tk), lambda i, j, k: (i, k))
    if b_lead is None:
        b_spec = pl.BlockSpec((tk, tn), lambda i, j, k: (k, j))
    else:
        b_spec = pl.BlockSpec((None, tk, tn), lambda i, j, k: (b_lead, k, j))
    o_spec = pl.BlockSpec((tm, tn), lambda i, j, k: (i, j))
    return _mm(name, a, b, jax.ShapeDtypeStruct((M, N), out_dtype), grid, a_spec, b_spec, o_spec,
               NN_DIMS, (tm, tn), res)


def mm_nt(name, a, b, out_dtype, b_lead=None, res=None):
    M, N = a.shape
    K = b.shape[-2]
    tm, tko, tk = _pick(M, 1024, 8), _pick(K, 1024), _pick(N, 2048)
    grid = (M // tm, K // tko, N // tk)
    a_spec = pl.BlockSpec((tm, tk), lambda i, j, k: (i, k))
    if b_lead is None:
        b_spec = pl.BlockSpec((tko, tk), lambda i, j, k: (j, k))
    else:
        b_spec = pl.BlockSpec((None, tko, tk), lambda i, j, k: (b_lead, j, k))
    o_spec = pl.BlockSpec((tm, tko), lambda i, j, k: (i, j))
    return _mm(name, a, b, jax.ShapeDtypeStruct((M, K), out_dtype), grid, a_spec, b_spec, o_spec,
               NT_DIMS, (tm, tko), res)


def mm_tn(name, a, b, out_dtype):
    M, K = a.shape
    N = b.shape[-1]
    cap = 4096 if (a.dtype == BF16 and b.dtype == BF16) else 2048
    tko, tn, tk = _pick(K, 1024), _pick(N, 1024), _pick(M, cap, 8)
    grid = (K // tko, N // tn, M // tk)
    a_spec = pl.BlockSpec((tk, tko), lambda i, j, k: (k, i))
    b_spec = pl.BlockSpec((tk, tn), lambda i, j, k: (k, j))
    o_spec = pl.BlockSpec((tko, tn), lambda i, j, k: (i, j))
    return _mm(name, a, b, jax.ShapeDtypeStruct((K, N), out_dtype), grid, a_spec, b_spec, o_spec,
               TN_DIMS, (tko, tn))


FFN_SUB = 4
def ffn_up(name, h, w, l, out_dtype):
    M, D = h.shape
    f = w.shape[-1]
    tm = _pick(M, 1024, 8)
    grid = (M // tm, N_DEV, 1)
    return _mm(name, h, w, jax.ShapeDtypeStruct((N_DEV, M, f), out_dtype), grid,
               pl.BlockSpec((tm, D), lambda i, j, k: (i, 0)),
               pl.BlockSpec((None, None, D, f), lambda i, j, k: (l, j, 0, 0)),
               pl.BlockSpec((None, tm, f), lambda i, j, k: (j, i, 0)), NN_DIMS, (tm, f))


def ffn_down(name, act, w, l, res, out_dtype):
    _, M, f = act.shape
    D = w.shape[-1]
    tm, tn = _pick(M, 1024, 8), _pick(D, 1024)
    grid = (M // tm, D // tn, N_DEV // FFN_SUB)
    return _mm(name, act, w, jax.ShapeDtypeStruct((M, D), out_dtype), grid,
               pl.BlockSpec((FFN_SUB, tm, f), lambda i, j, k: (k, i, 0)),
               pl.BlockSpec((None, FFN_SUB, f, tn), lambda i, j, k: (l, k, 0, j)),
               pl.BlockSpec((tm, tn), lambda i, j, k: (i, j)), NN_DIMS, (tm, tn), res, nsub=FFN_SUB)


def ffn_down_bwd_act(name, dy, w, l, out_dtype):
    M, D = dy.shape
    f = w.shape[-2]
    tm = _pick(M, 1024, 8)
    grid = (M // tm, N_DEV, 1)
    return _mm(name, dy, w, jax.ShapeDtypeStruct((N_DEV, M, f), out_dtype), grid,
               pl.BlockSpec((tm, D), lambda i, j, k: (i, 0)),
               pl.BlockSpec((None, None, f, D), lambda i, j, k: (l, j, 0, 0)),
               pl.BlockSpec((None, tm, f), lambda i, j, k: (j, i, 0)), NT_DIMS, (tm, f))


def ffn_down_bwd_w(name, act, dy, out_dtype):
    _, M, f = act.shape
    D = dy.shape[-1]
    tn, tk = _pick(D, 1024), _pick(M, 4096 if dy.dtype == BF16 else 2048, 8)
    grid = (N_DEV, D // tn, M // tk)
    return _mm(name, act, dy, jax.ShapeDtypeStruct((N_DEV, f, D), out_dtype), grid,
               pl.BlockSpec((None, tk, f), lambda j, n, k: (j, k, 0)),
               pl.BlockSpec((tk, tn), lambda j, n, k: (k, n)),
               pl.BlockSpec((None, f, tn), lambda j, n, k: (j, 0, n)), TN_DIMS, (f, tn))


def ffn_up_bwd_h(name, du, w, l, res, out_dtype):
    _, M, f = du.shape
    D = w.shape[-2]
    tm, tn = _pick(M, 1024, 8), _pick(D, 1024)
    grid = (M // tm, D // tn, N_DEV // FFN_SUB)
    return _mm(name, du, w, jax.ShapeDtypeStruct((M, D), out_dtype), grid,
               pl.BlockSpec((FFN_SUB, tm, f), lambda i, j, k: (k, i, 0)),
               pl.BlockSpec((None, FFN_SUB, tn, f), lambda i, j, k: (l, k, j, 0)),
               pl.BlockSpec((tm, tn), lambda i, j, k: (i, j)), NT_DIMS, (tm, tn), res, nsub=FFN_SUB)


def _rms(xf, g):
    return xf * lax.rsqrt(jnp.mean(xf * xf, axis=-1, keepdims=True) + NORM_EPS) * g


def rms_fwd(name, x, cb, W, g, out_dtype):
    S = x.shape[0]
    tr = _pick(S, 256, 8)

    def body(x_ref, g_ref, o_ref):
        o_ref[...] = _rms(x_ref[...].astype(F32), g_ref[...]).astype(o_ref.dtype)

    return pl.pallas_call(
        body, out_shape=jax.ShapeDtypeStruct((S, W), out_dtype), grid=(S // tr,),
        in_specs=[pl.BlockSpec((tr, W), lambda i: (i, cb)), pl.BlockSpec((1, W), lambda i: (0, 0))],
        out_specs=pl.BlockSpec((tr, W), lambda i: (i, 0)), name=name, compiler_params=_cparams("parallel"))(x, g)


def rms_bwd(name, x, cb, W, g, dy, out_dtype, res=None):
    S = x.shape[0]
    tr = _pick(S, 256, 8)
    has_res = res is not None

    def body(*refs):
        if has_res:
            x_ref, g_ref, dy_ref, r_ref, dx_ref, dg_ref = refs
        else:
            x_ref, g_ref, dy_ref, dx_ref, dg_ref = refs
        _, vjp = jax.vjp(_rms, x_ref[...].astype(F32), g_ref[...])
        dx, dg = vjp(dy_ref[...].astype(F32))
        if has_res:
            dx = dx + r_ref[...]
        dx_ref[...] = dx.astype(dx_ref.dtype)

        @pl.when(pl.program_id(0) == 0)
        def _():
            dg_ref[...] = jnp.zeros_like(dg_ref)

        dg_ref[...] += dg

    row = pl.BlockSpec((tr, W), lambda i: (i, 0))
    vec = pl.BlockSpec((1, W), lambda i: (0, 0))
    in_specs = [pl.BlockSpec((tr, W), lambda i: (i, cb)), vec, row] + ([row] if has_res else [])
    args = (x, g, dy) + ((res,) if has_res else ())
    return pl.pallas_call(
        body, out_shape=(jax.ShapeDtypeStruct((S, W), out_dtype), jax.ShapeDtypeStruct((1, W), F32)),
        grid=(S // tr,), in_specs=in_specs, out_specs=(row, vec), name=name,
        compiler_params=_cparams("arbitrary"))(*args)


def rope_tables(S, d):
    pos = jnp.arange(S, dtype=F32)
    inv_freq = ROPE_THETA ** (-jnp.arange(0, d, 2, dtype=F32) / d)
    ang = pos[:, None] * inv_freq[None, :]
    cos, sin = jnp.cos(ang), jnp.sin(ang)
    half = d // 2
    z = jnp.zeros((S, LANES - d), F32)
    zh = jnp.zeros((S, half), F32)
    c = jnp.concatenate([cos, cos, z], axis=1)
    sa = jnp.concatenate([-sin, zh, z], axis=1)
    sb = jnp.concatenate([zh, sin, z], axis=1)
    return c, sa, sb, half


def rope_apply(name, x, cb, H, tabs, scale, out_dtype, transpose=False):
    c, sa, sb, half = tabs
    S = x.shape[0]
    tr = _pick(S, 512, 8)
    up, down = LANES - half, half

    def body(x_ref, c_ref, sa_ref, sb_ref, o_ref):
        xv = x_ref[...].astype(F32)
        if not transpose:
            y = xv * c_ref[...] + pltpu.roll(xv, up, 1) * sa_ref[...] + pltpu.roll(xv, down, 1) * sb_ref[...]
            y = y * scale
        else:
            xv = xv * scale
            y = (xv * c_ref[...] + pltpu.roll(xv * sa_ref[...], down, 1)
                 + pltpu.roll(xv * sb_ref[...], up, 1))
        o_ref[...] = y.astype(o_ref.dtype)

    tab = pl.BlockSpec((tr, LANES), lambda h, i: (i, 0))
    return pl.pallas_call(
        body, out_shape=jax.ShapeDtypeStruct((S, H * LANES), out_dtype), grid=(H, S // tr),
        in_specs=[pl.BlockSpec((tr, LANES), lambda h, i: (i, cb + h)), tab, tab, tab],
        out_specs=pl.BlockSpec((tr, LANES), lambda h, i: (i, h)), name=name,
        compiler_params=_cparams("parallel", "parallel"))(x, c, sa, sb)


def _ret_out(o, g):
    y = o * lax.rsqrt(jnp.mean(o * o, axis=-1, keepdims=True) + NORM_EPS)
    return y * jax.nn.silu(g)


def ret_out_fwd(name, o, gsrc, g_cb, out_dtype):
    S = o.shape[0]
    tr = _pick(S, 512, 8)
    W = RET_DV

    def body(o_ref, g_ref, y_ref):
        y_ref[...] = _ret_out(o_ref[...], g_ref[...].astype(F32)).astype(y_ref.dtype)

    blk = pl.BlockSpec((tr, W), lambda h, i: (i, h))
    return pl.pallas_call(
        body, out_shape=jax.ShapeDtypeStruct((S, RET_HEADS * W), out_dtype), grid=(RET_HEADS, S // tr),
        in_specs=[blk, pl.BlockSpec((tr, W), lambda h, i: (i, g_cb + h))], out_specs=blk, name=name,
        compiler_params=_cparams("parallel", "parallel"))(o, gsrc)


def ret_out_bwd(name, o, gsrc, g_cb, dy):
    S = o.shape[0]
    tr = _pick(S, 512, 8)
    W = RET_DV

    def body(o_ref, g_ref, dy_ref, do_ref, dg_ref):
        _, vjp = jax.vjp(_ret_out, o_ref[...], g_ref[...].astype(F32))
        do, dg = vjp(dy_ref[...].astype(F32))
        do_ref[...] = do.astype(do_ref.dtype)
        dg_ref[...] = dg.astype(dg_ref.dtype)

    blk = pl.BlockSpec((tr, W), lambda h, i: (i, h))
    return pl.pallas_call(
        body, out_shape=(jax.ShapeDtypeStruct((S, RET_HEADS * W), F32),
                         jax.ShapeDtypeStruct((S, RET_HEADS * W), BF16)),
        grid=(RET_HEADS, S // tr),
        in_specs=[blk, pl.BlockSpec((tr, W), lambda h, i: (i, g_cb + h)), blk], out_specs=(blk, blk),
        name=name, compiler_params=_cparams("parallel", "parallel"))(o, gsrc, dy)


def _merge(g0, g1, g2, a, b, c):
    return jax.nn.sigmoid(g0) * a + jax.nn.sigmoid(g1) * b + jax.nn.sigmoid(g2) * c


def merge_fwd(name, P, gates_cb, a, b, c, out_dtype):
    S, D = a.shape
    tr = _pick(S, 128, 8)

    def body(g0, g1, g2, a_ref, b_ref, c_ref, o_ref):
        o_ref[...] = _merge(g0[...], g1[...], g2[...], a_ref[...], b_ref[...], c_ref[...]).astype(o_ref.dtype)

    row = pl.BlockSpec((tr, D), lambda i: (i, 0))
    gs = [pl.BlockSpec((tr, D), lambda i, k=k: (i, gates_cb + k)) for k in range(3)]
    return pl.pallas_call(
        body, out_shape=jax.ShapeDtypeStruct((S, D), out_dtype), grid=(S // tr,),
        in_specs=gs + [row, row, row], out_specs=row, name=name,
        compiler_params=_cparams("parallel"))(P, P, P, a, b, c)


def merge_bwd(name, P, gates_cb, a, b, c, dm):
    S, D = a.shape
    tr = _pick(S, 128, 8)

    def body(g0, g1, g2, a_ref, b_ref, c_ref, dm_ref, dg_ref, da_ref, db_ref, dc_ref):
        _, vjp = jax.vjp(_merge, g0[...], g1[...], g2[...], a_ref[...], b_ref[...], c_ref[...])
        d0, d1, d2, da, db, dc = vjp(dm_ref[...].astype(F32))
        dg_ref[:, 0:D] = d0.astype(dg_ref.dtype)
        dg_ref[:, D:2 * D] = d1.astype(dg_ref.dtype)
        dg_ref[:, 2 * D:3 * D] = d2.astype(dg_ref.dtype)
        da_ref[...] = da.astype(da_ref.dtype)
        db_ref[...] = db.astype(db_ref.dtype)
        dc_ref[...] = dc.astype(dc_ref.dtype)

    row = pl.BlockSpec((tr, D), lambda i: (i, 0))
    gs = [pl.BlockSpec((tr, D), lambda i, k=k: (i, gates_cb + k)) for k in range(3)]
    bf = jax.ShapeDtypeStruct((S, D), BF16)
    return pl.pallas_call(
        body, out_shape=(jax.ShapeDtypeStruct((S, 3 * D), BF16), bf, bf, bf), grid=(S // tr,),
        in_specs=gs + [row, row, row, row],
        out_specs=(pl.BlockSpec((tr, 3 * D), lambda i: (i, 0)), row, row, row), name=name,
        compiler_params=_cparams("parallel"))(P, P, P, a, b, c, dm)


HALO = 8


CONV_CHUNK = 32


def _shifted_back(u_ref, uh_ref, ext_ref, s1_ref, s2_ref, tr):
    ext_ref[0:HALO, :] = jnp.where(pl.program_id(1) > 0, uh_ref[...], 0.0)
    ext_ref[HALO:HALO + tr, :] = u_ref[...]
    s1_ref[...] = ext_ref[HALO - 1:HALO - 1 + tr, :]
    s2_ref[...] = ext_ref[HALO - 2:HALO - 2 + tr, :]


def _conv3(cw, cb, u, u1, u2):
    return cb + ((cw[0:1, :] * u2 + cw[1:2, :] * u1) + cw[2:3, :] * u)


def _chunks(tr, fn):
    def step(c, carry):
        return fn(pl.ds(pl.multiple_of(c * CONV_CHUNK, CONV_CHUNK), CONV_CHUNK), carry)
    return step


def _ffn_specs(S, f, tr, l):
    nb = tr // HALO
    row = pl.BlockSpec((None, tr, f), lambda j, i: (j, i, 0))
    prev = pl.BlockSpec((None, HALO, f), lambda j, i: (j, jnp.maximum(i * nb - 1, 0), 0))
    nxt = pl.BlockSpec((None, HALO, f), lambda j, i: (j, jnp.minimum((i + 1) * nb, S // HALO - 1), 0))
    cw = pl.BlockSpec((None, None, 3, f), lambda j, i: (l, j, 0, 0))
    cb = pl.BlockSpec((None, None, 1, f), lambda j, i: (l, j, 0, 0))
    return row, prev, nxt, cw, cb


def ffn_act_fwd(name, u, gt, cw, cb, l, out_dtype):
    _, S, f = u.shape
    tr = _pick(S, 512, 8)
    row, prev, _, cws, cbs = _ffn_specs(S, f, tr, l)

    def body(u_ref, uh_ref, gt_ref, cw_ref, cb_ref, o_ref, ext_ref, s1_ref, s2_ref):
        _shifted_back(u_ref, uh_ref, ext_ref, s1_ref, s2_ref, tr)
        cwv, cbv = cw_ref[...], cb_ref[...]

        def chunk(rows, carry):
            uc = _conv3(cwv, cbv, u_ref[rows, :], s1_ref[rows, :], s2_ref[rows, :])
            o_ref[rows, :] = (jax.nn.gelu(uc) * gt_ref[rows, :]).astype(o_ref.dtype)
            return carry

        lax.fori_loop(0, tr // CONV_CHUNK, _chunks(tr, chunk), 0)

    return pl.pallas_call(
        body, out_shape=jax.ShapeDtypeStruct((N_DEV, S, f), out_dtype), grid=(N_DEV, S // tr),
        in_specs=[row, prev, row, cws, cbs], out_specs=row,
        scratch_shapes=[pltpu.VMEM((tr + HALO, f), F32), pltpu.VMEM((tr, f), F32), pltpu.VMEM((tr, f), F32)],
        name=name, compiler_params=_cparams("parallel", "parallel"))(u, u, gt, cw, cb)


def ffn_act_bwd_point(name, u, gt, cw, cb, l, dact):
    _, S, f = u.shape
    tr = _pick(S, 512, 8)
    row, prev, _, cws, cbs = _ffn_specs(S, f, tr, l)

    def body(u_ref, uh_ref, gt_ref, cw_ref, cb_ref, da_ref, g_ref, dgt_ref, ext_ref, s1_ref, s2_ref):
        _shifted_back(u_ref, uh_ref, ext_ref, s1_ref, s2_ref, tr)
        cwv, cbv = cw_ref[...], cb_ref[...]

        def chunk(rows, carry):
            uc = _conv3(cwv, cbv, u_ref[rows, :], s1_ref[rows, :], s2_ref[rows, :])
            _, vjp = jax.vjp(lambda c, t: jax.nn.gelu(c) * t, uc, gt_ref[rows, :])
            g, dgt = vjp(da_ref[rows, :].astype(F32))
            g_ref[rows, :] = g
            dgt_ref[rows, :] = dgt.astype(dgt_ref.dtype)
            return carry

        lax.fori_loop(0, tr // CONV_CHUNK, _chunks(tr, chunk), 0)

    return pl.pallas_call(
        body, out_shape=(jax.ShapeDtypeStruct((N_DEV, S, f), F32), jax.ShapeDtypeStruct((N_DEV, S, f), BF16)),
        grid=(N_DEV, S // tr), in_specs=[row, prev, row, cws, cbs, row], out_specs=(row, row),
        scratch_shapes=[pltpu.VMEM((tr + HALO, f), F32), pltpu.VMEM((tr, f), F32), pltpu.VMEM((tr, f), F32)],
        name=name, compiler_params=_cparams("parallel", "parallel"))(u, u, gt, cw, cb, dact)


def ffn_act_bwd_conv(name, u, g, cw, l):
    _, S, f = u.shape
    tr = _pick(S, 512, 8)
    nt = S // tr
    row, prev, nxt, cws, _ = _ffn_specs(S, f, tr, l)

    def body(u_ref, uh_ref, g_ref, gn_ref, cw_ref, du_ref, dcw_ref, dcb_ref, ext_ref, s1_ref, s2_ref, n1_ref, n2_ref):
        i = pl.program_id(1)
        _shifted_back(u_ref, uh_ref, ext_ref, s1_ref, s2_ref, tr)
        ext_ref[0:tr, :] = g_ref[...]
        ext_ref[tr:tr + HALO, :] = jnp.where(i < nt - 1, gn_ref[...], 0.0)
        n1_ref[...] = ext_ref[1:1 + tr, :]
        n2_ref[...] = ext_ref[2:2 + tr, :]
        cw = cw_ref[...]

        def chunk(rows, carry):
            d0, d1, d2, db = carry
            g = g_ref[rows, :]
            du_ref[rows, :] = (cw[2:3, :] * g + cw[1:2, :] * n1_ref[rows, :]
                               + cw[0:1, :] * n2_ref[rows, :]).astype(du_ref.dtype)
            return (d0 + jnp.sum(g * s2_ref[rows, :], axis=0, keepdims=True),
                    d1 + jnp.sum(g * s1_ref[rows, :], axis=0, keepdims=True),
                    d2 + jnp.sum(g * u_ref[rows, :], axis=0, keepdims=True),
                    db + jnp.sum(g, axis=0, keepdims=True))

        z = jnp.zeros((1, f), F32)
        d0, d1, d2, db = lax.fori_loop(0, tr // CONV_CHUNK, _chunks(tr, chunk), (z, z, z, z))

        @pl.when(i == 0)
        def _():
            dcw_ref[...] = jnp.zeros_like(dcw_ref)
            dcb_ref[...] = jnp.zeros_like(dcb_ref)

        dcw_ref[0:1, :] += d0
        dcw_ref[1:2, :] += d1
        dcw_ref[2:3, :] += d2
        dcb_ref[...] += db

    tile = pltpu.VMEM((tr, f), F32)
    return pl.pallas_call(
        body, out_shape=(jax.ShapeDtypeStruct((N_DEV, S, f), BF16), jax.ShapeDtypeStruct((N_DEV, 3, f), F32),
                         jax.ShapeDtypeStruct((N_DEV, 1, f), F32)),
        grid=(N_DEV, nt), in_specs=[row, prev, row, nxt, cws],
        out_specs=(row, pl.BlockSpec((None, 3, f), lambda j, i: (j, 0, 0)),
                   pl.BlockSpec((None, 1, f), lambda j, i: (j, 0, 0))),
        scratch_shapes=[pltpu.VMEM((tr + HALO, f), F32), tile, tile, tile, tile], name=name,
        compiler_params=_cparams("parallel", "arbitrary"))(u, u, g, g, cw)


def loss_head(name, x, g, tgt):
    S, D = x.shape
    tr = _pick(S, 256, 8)

    def body(x_ref, g_ref, t_ref, l_ref, dx_ref, dg_ref):
        tg = t_ref[...]

        def f(xv, gv):
            err = jnp.square(_rms(xv, gv) - tg)
            return 0.5 * jnp.sum(jnp.mean(err, axis=-1))

        val, vjp = jax.vjp(f, x_ref[...], g_ref[...])
        dx, dg = vjp(jnp.ones((), F32))
        dx_ref[...] = dx

        @pl.when(pl.program_id(0) == 0)
        def _():
            l_ref[...] = jnp.zeros_like(l_ref)
            dg_ref[...] = jnp.zeros_like(dg_ref)

        l_ref[...] += val
        dg_ref[...] += dg

    row = pl.BlockSpec((tr, D), lambda i: (i, 0))
    vec = pl.BlockSpec((1, D), lambda i: (0, 0))
    lt = pl.BlockSpec((8, LANES), lambda i: (0, 0))
    return pl.pallas_call(
        body, out_shape=(jax.ShapeDtypeStruct((8, LANES), F32), jax.ShapeDtypeStruct((S, D), F32),
                         jax.ShapeDtypeStruct((1, D), F32)),
        grid=(S // tr,), in_specs=[row, vec, row], out_specs=(lt, row, vec), name=name,
        compiler_params=_cparams("arbitrary"))(x, g, tgt)


def _tri(n, fn):
    r = lax.broadcasted_iota(jnp.int32, (n, n), 0)
    c = lax.broadcasted_iota(jnp.int32, (n, n), 1)
    return jnp.where(fn(r, c), 1.0, 0.0).astype(F32)


def _log_sigmoid(z):
    return jnp.minimum(z, 0.0) - jnp.log1p(jnp.exp(-jnp.abs(z)))


def fox_gate_fwd(name, ft, b):
    H, R, _ = ft.shape

    def body(f_ref, b_ref, o_ref):
        ls = _log_sigmoid(f_ref[...] + b_ref[...])
        cum = jnp.dot(ls, _tri(LANES, lambda r, c: r <= c), precision=HIGHEST, preferred_element_type=F32)
        tot = jnp.broadcast_to(cum[:, LANES - 1:LANES], (R, LANES))
        off = jnp.dot(_tri(R, lambda r, c: r > c), tot, precision=HIGHEST, preferred_element_type=F32)
        o_ref[...] = -(cum + off)

    blk = pl.BlockSpec((None, R, LANES), lambda h: (h, 0, 0))
    return pl.pallas_call(
        body, out_shape=jax.ShapeDtypeStruct((H, R, LANES), F32), grid=(H,),
        in_specs=[blk, pl.BlockSpec((None, 1, LANES), lambda h: (h, 0, 0))], out_specs=blk, name=name,
        compiler_params=_cparams("parallel"))(ft, b)


def fox_gate_bwd(name, ft, b, dkb):
    H, R, _ = ft.shape

    def body(f_ref, b_ref, d_ref, df_ref, db_ref):
        z = f_ref[...] + b_ref[...]
        d = d_ref[...]
        rev = jnp.dot(d, _tri(LANES, lambda r, c: r >= c), precision=HIGHEST, preferred_element_type=F32)
        tot = jnp.broadcast_to(rev[:, 0:1], (R, LANES))
        off = jnp.dot(_tri(R, lambda r, c: r < c), tot, precision=HIGHEST, preferred_element_type=F32)
        dls = -(rev + off)
        dz = dls * jax.nn.sigmoid(-z)
        df_ref[...] = dz
        s = jnp.sum(jnp.sum(dz, axis=1, keepdims=True), axis=0, keepdims=True)
        db_ref[...] = jnp.broadcast_to(s, (1, LANES))

    blk = pl.BlockSpec((None, R, LANES), lambda h: (h, 0, 0))
    vec = pl.BlockSpec((None, 1, LANES), lambda h: (h, 0, 0))
    return pl.pallas_call(
        body, out_shape=(jax.ShapeDtypeStruct((H, R, LANES), F32), jax.ShapeDtypeStruct((H, 1, LANES), F32)),
        grid=(H,), in_specs=[blk, vec, blk], out_specs=(blk, vec), name=name,
        compiler_params=_cparams("parallel"))(ft, b, dkb)


def _ret_log_gamma(h):
    lg = [float(np.log(np.float32(1.0) - np.float32(2.0) ** np.float32(-5.0 - i))) for i in range(RET_HEADS)]
    out = jnp.float32(lg[RET_HEADS - 1])
    for i in range(RET_HEADS - 2, -1, -1):
        out = jnp.where(h == i, jnp.float32(lg[i]), out)
    return out


def _visible(mode, B):
    r = lax.broadcasted_iota(jnp.int32, (B, B), 0)
    c = lax.broadcasted_iota(jnp.int32, (B, B), 1)
    if mode == "fox":
        return c <= r
    return (c // CHUNK) <= (r // CHUNK)


def _visible_t(mode, B):
    k = lax.broadcasted_iota(jnp.int32, (B, B), 0)
    q = lax.broadcasted_iota(jnp.int32, (B, B), 1)
    if mode == "fox":
        return k <= q
    return (k // CHUNK) <= (q // CHUNK)


def _decay(lg, B, blocks_apart):
    r = lax.broadcasted_iota(jnp.int32, (B, B), 0)
    c = lax.broadcasted_iota(jnp.int32, (B, B), 1)
    dist = jnp.abs(r - c + blocks_apart * B).astype(F32)
    return jnp.exp(lg * dist)


def _attn_block(S):
    return 512 if S >= 2048 else 128


def attn_fwd(name, mode, q1, q1_cb, k1, k1_cb, v, v_cb, H, dv, scale, q2=None, q2_cb=0, k2=None, kbias=None):
    S = q1.shape[0]
    B = _attn_block(S)
    nq = S // B
    softmax = mode != "ret"
    two = mode == "mla"
    has_bias = mode == "fox"

    def body(*refs):
        it = iter(refs)
        q1_ref, k1_ref, v_ref = next(it), next(it), next(it)
        q2_ref = next(it) if two else None
        k2_ref = next(it) if two else None
        kb_ref = next(it) if has_bias else None
        o_ref = next(it)
        lse_ref = next(it) if softmax else None
        kbuf, vT = next(it), next(it)
        acc = next(it)
        m_ref = next(it) if softmax else None
        l_ref = next(it) if softmax else None
        s_all = next(it) if softmax else None
        kcol = next(it) if has_bias else None
        h = pl.program_id(0)
        i = pl.program_id(1)

        @pl.when(i == 0)
        def _():
            kbuf[:, 0:LANES] = k1_ref[...].astype(BF16)
            vT[...] = v_ref[...].astype(F32).T.astype(BF16)
            if two:
                kbuf[:, LANES:2 * LANES] = k2_ref[...].astype(BF16)
            if has_bias:
                for g in range(nq):
                    kcol[g * B:(g + 1) * B, :] = jnp.broadcast_to(kb_ref[g], (LANES, B)).T[:, 0:1]

        qb = q1_ref[...].astype(BF16)
        if two:
            qb = jnp.concatenate([qb, q2_ref[...].astype(BF16)], axis=1)
        lg = _ret_log_gamma(h) if mode == "ret" else None
        acc[...] = jnp.zeros_like(acc)
        if softmax:
            m_ref[...] = jnp.full_like(m_ref, NEG_BIG)
            l_ref[...] = jnp.zeros_like(l_ref)

        def scores(g, diag):
            rows = slice(g * B, (g + 1) * B)
            s = lax.dot_general(kbuf[rows, :], qb, NT_DIMS, preferred_element_type=F32)
            if softmax:
                s = s * scale
                if has_bias:
                    s = s + kcol[rows, :]
                if diag:
                    s = jnp.where(_visible_t(mode, B), s, NEG_BIG)
                s_all[rows, :] = s
                m_ref[...] = jnp.maximum(m_ref[...], jnp.max(s, axis=0, keepdims=True))
            else:
                if diag:
                    p = jnp.where(_visible_t(mode, B), s * _decay(lg, B, 0), 0.0)
                else:
                    p = s * _decay(lg, B, g - i)
                acc[...] += jnp.dot(vT[:, rows], p.astype(BF16), preferred_element_type=F32)

        def weighted(g):
            rows = slice(g * B, (g + 1) * B)
            p = jnp.exp(s_all[rows, :] - m_ref[...])
            l_ref[...] += jnp.sum(p, axis=0, keepdims=True)
            acc[...] += jnp.dot(vT[:, rows], p.astype(BF16), preferred_element_type=F32)

        for g in range(nq):
            pl.when(g < i)(functools.partial(scores, g, False))
            pl.when(g == i)(functools.partial(scores, g, True))
        if softmax:
            for g in range(nq):
                pl.when(g <= i)(functools.partial(weighted, g))
            o_ref[...] = (acc[...] / l_ref[...]).T
            lse_ref[...] = jnp.broadcast_to(m_ref[...] + jnp.log(l_ref[...]), (LANES, B)).T
        else:
            o_ref[...] = acc[...].T

    in_specs = [pl.BlockSpec((B, LANES), lambda h, i: (i, q1_cb + h)),
                pl.BlockSpec((S, LANES), lambda h, i: (0, k1_cb + h)),
                pl.BlockSpec((S, dv), lambda h, i: (0, v_cb + h))]
    args = [q1, k1, v]
    if two:
        in_specs += [pl.BlockSpec((B, LANES), lambda h, i: (i, q2_cb + h)),
                     pl.BlockSpec((S, LANES), lambda h, i: (0, 0))]
        args += [q2, k2]
    if has_bias:
        in_specs.append(pl.BlockSpec((None, nq, 1, B), lambda h, i: (h, 0, 0, 0)))
        args.append(kbias)
    out_shape = [jax.ShapeDtypeStruct((S, H * dv), F32)]
    out_specs = [pl.BlockSpec((B, dv), lambda h, i: (i, h))]
    if softmax:
        out_shape.append(jax.ShapeDtypeStruct((S, H * LANES), F32))
        out_specs.append(pl.BlockSpec((B, LANES), lambda h, i: (i, h)))
    kw = 2 * LANES if two else LANES
    scratch = [pltpu.VMEM((S, kw), BF16), pltpu.VMEM((dv, S), BF16), pltpu.VMEM((dv, B), F32)]
    if softmax:
        scratch += [pltpu.VMEM((1, B), F32), pltpu.VMEM((1, B), F32), pltpu.VMEM((S, B), F32)]
    if has_bias:
        scratch.append(pltpu.VMEM((S, 1), F32))
    res = pl.pallas_call(body, out_shape=tuple(out_shape), grid=(H, nq), in_specs=in_specs,
                         out_specs=tuple(out_specs), scratch_shapes=scratch, name=name,
                         compiler_params=_cparams("parallel", "arbitrary"))(*args)
    return res if softmax else (res[0], None)


def attn_bwd(name, mode, q1, q1_cb, k1, k1_cb, v, v_cb, H, dv, scale, do, o=None, lse=None,
             q2=None, q2_cb=0, k2=None, kbias=None):
    S = q1.shape[0]
    B = _attn_block(S)
    nb = S // B
    softmax = mode != "ret"
    two = mode == "mla"
    has_bias = mode == "fox"

    def body(*refs):
        it = iter(refs)
        q1_ref, k1_ref, v_ref, do_ref = next(it), next(it), next(it), next(it)
        o_ref = next(it) if softmax else None
        lse_ref = next(it) if softmax else None
        q2_ref = next(it) if two else None
        k2_ref = next(it) if two else None
        kb_ref = next(it) if has_bias else None
        dq1_ref, dk1_ref, dv_ref = next(it), next(it), next(it)
        dq2_ref = next(it) if two else None
        dk2_ref = next(it) if two else None
        dkb_ref = next(it) if has_bias else None
        drow_ref = next(it) if has_bias else None
        qbuf, dobuf = next(it), next(it)
        qT, doT = next(it), next(it)
        delta = next(it) if softmax else None
        dk_acc, dv_acc = next(it), next(it)
        dkb_acc = next(it) if has_bias else None
        h = pl.program_id(0)
        j = pl.program_id(1)

        @pl.when(j == 0)
        def _():
            qbuf[:, 0:LANES] = q1_ref[...].astype(BF16)
            dobuf[...] = do_ref[...].astype(BF16)
            qT[0:LANES, :] = q1_ref[...].astype(F32).T.astype(BF16)
            doT[...] = do_ref[...].astype(F32).T.astype(BF16)
            dq1_ref[...] = jnp.zeros_like(dq1_ref)
            if has_bias:
                drow_ref[...] = jnp.zeros_like(drow_ref)
            if two:
                qbuf[:, LANES:2 * LANES] = q2_ref[...].astype(BF16)
                qT[LANES:2 * LANES, :] = q2_ref[...].astype(F32).T.astype(BF16)
                dq2_ref[...] = jnp.zeros_like(dq2_ref)
            if softmax:
                def drow(t, carry):
                    rows = pl.ds(pl.multiple_of(t * B, B), B)
                    d = jnp.sum(do_ref[rows, :].astype(F32) * o_ref[rows, :], axis=1, keepdims=True)
                    delta[rows, :] = jnp.broadcast_to(d, (B, LANES))
                    return carry
                lax.fori_loop(0, nb, drow, 0)

        kj = k1_ref[...].astype(BF16)
        if two:
            kj = jnp.concatenate([kj, k2_ref[...].astype(BF16)], axis=1)
        vj = v_ref[...].astype(BF16)
        kbj = kb_ref[...] if has_bias else None
        lg = _ret_log_gamma(h) if mode == "ret" else None
        dk_acc[...] = jnp.zeros_like(dk_acc)
        dv_acc[...] = jnp.zeros_like(dv_acc)
        if has_bias:
            dkb_acc[...] = jnp.zeros_like(dkb_acc)

        def step(i, diag):
            rows = slice(i * B, (i + 1) * B)
            qi = qbuf[rows, :]
            doi = dobuf[rows, :]
            s = lax.dot_general(qi, kj, NT_DIMS, preferred_element_type=F32)
            dp = lax.dot_general(doi, vj, NT_DIMS, preferred_element_type=F32)
            if softmax:
                s = s * scale
                if has_bias:
                    s = s + kbj
                if diag:
                    s = jnp.where(_visible(mode, B), s, NEG_BIG)
                p = jnp.exp(s - jnp.tile(lse_ref[rows, :], (1, B // LANES)))
                ds = p * (dp - jnp.tile(delta[rows, :], (1, B // LANES)))
                if has_bias:
                    dkb_acc[...] += jnp.sum(ds, axis=0, keepdims=True)
                    drow_ref[rows, :] += jnp.broadcast_to(jnp.sum(ds, axis=1, keepdims=True), (B, LANES))
                dsb = (ds * scale).astype(BF16)
            else:
                if diag:
                    dec = jnp.where(_visible(mode, B), _decay(lg, B, 0), 0.0)
                else:
                    dec = _decay(lg, B, i - j)
                p = s * dec
                dsb = (dp * dec).astype(BF16)
            dv_acc[...] += jnp.dot(doT[:, rows], p.astype(BF16), preferred_element_type=F32)
            dk_acc[...] += jnp.dot(qT[:, rows], dsb, preferred_element_type=F32)
            dq = jnp.dot(dsb, kj, preferred_element_type=F32)
            dq1_ref[rows, :] += dq[:, 0:LANES]
            if two:
                dq2_ref[rows, :] += dq[:, LANES:2 * LANES]

        for i in range(nb):
            pl.when(i == j)(functools.partial(step, i, True))
            pl.when(i > j)(functools.partial(step, i, False))
        dk1_ref[...] = dk_acc[0:LANES, :].T
        dv_ref[...] = dv_acc[...].T
        if two:
            dk2_ref[...] = dk_acc[LANES:2 * LANES, :].T
        if has_bias:
            dkb_ref[...] = dkb_acc[...]

    full = lambda w, cb: pl.BlockSpec((S, w), lambda h, j: (0, cb + h))
    blk = lambda w, cb: pl.BlockSpec((B, w), lambda h, j: (j, cb + h))
    in_specs = [full(LANES, q1_cb), blk(LANES, k1_cb), blk(dv, v_cb), full(dv, 0)]
    args = [q1, k1, v, do]
    if softmax:
        in_specs += [full(dv, 0), full(LANES, 0)]
        args += [o, lse]
    if two:
        in_specs += [full(LANES, q2_cb), pl.BlockSpec((B, LANES), lambda h, j: (j, 0))]
        args += [q2, k2]
    if has_bias:
        in_specs.append(pl.BlockSpec((None, None, 1, B), lambda h, j: (h, j, 0, 0)))
        args.append(kbias)
    names = ["dq1", "dk1", "dv"]
    out_shape = [jax.ShapeDtypeStruct((S, H * LANES), F32), jax.ShapeDtypeStruct((S, H * LANES), F32),
                 jax.ShapeDtypeStruct((S, H * dv), F32)]
    out_specs = [full(LANES, 0), blk(LANES, 0), blk(dv, 0)]
    if two:
        names += ["dq2", "dk2h"]
        out_shape += [jax.ShapeDtypeStruct((S, H * LANES), F32)] * 2
        out_specs += [full(LANES, 0), blk(LANES, 0)]
    if has_bias:
        names.append("dkb")
        out_shape.append(jax.ShapeDtypeStruct((H, nb, 1, B), F32))
        out_specs.append(pl.BlockSpec((None, None, 1, B), lambda h, j: (h, j, 0, 0)))
        names.append("drow")
        out_shape.append(jax.ShapeDtypeStruct((S, H * LANES), F32))
        out_specs.append(full(LANES, 0))
    kw = 2 * LANES if two else LANES
    scratch = [pltpu.VMEM((S, kw), BF16), pltpu.VMEM((S, dv), BF16),
               pltpu.VMEM((kw, S), BF16), pltpu.VMEM((dv, S), BF16)]
    if softmax:
        scratch.append(pltpu.VMEM((S, LANES), F32))
    scratch += [pltpu.VMEM((kw, B), F32), pltpu.VMEM((dv, B), F32)]
    if has_bias:
        scratch.append(pltpu.VMEM((1, B), F32))
    res = pl.pallas_call(body, out_shape=tuple(out_shape), grid=(H, nb), in_specs=in_specs,
                         out_specs=tuple(out_specs), scratch_shapes=scratch, name=name,
                         compiler_params=_cparams("parallel", "arbitrary"))(*args)
    return dict(zip(names, res))


def head_sum(name, x, H, out_dtype):
    S = x.shape[0]
    tr = _pick(S, 512, 8)

    def body(x_ref, o_ref):
        acc = x_ref[:, 0:LANES]
        for h in range(1, H):
            acc = acc + x_ref[:, h * LANES:(h + 1) * LANES]
        o_ref[...] = acc.astype(o_ref.dtype)

    return pl.pallas_call(
        body, out_shape=jax.ShapeDtypeStruct((S, LANES), out_dtype), grid=(S // tr,),
        in_specs=[pl.BlockSpec((tr, H * LANES), lambda i: (i, 0))],
        out_specs=pl.BlockSpec((tr, LANES), lambda i: (i, 0)), name=name,
        compiler_params=_cparams("parallel"))(x)


def _mesh_pos():
    return lax.axis_index("x"), lax.axis_index("y"), lax.axis_index("c")


def _peer(pos, k):
    x, y, c = pos
    px = 1 - x if k & 4 else x
    py = 1 - y if k & 2 else y
    pc = 1 - c if k & 1 else c
    return (px, py, pc), 4 * px + 2 * py + pc


def exchange(name, tensors):
    nt = len(tensors)
    flat_in, counts = [], []
    out_shape = []
    for mode, srcs in tensors:
        counts.append(len(srcs))
        flat_in += list(srcs)
        rc = srcs[0].shape[-2:]
        out_shape.append(jax.ShapeDtypeStruct((len(srcs), N_DEV) + tuple(rc), srcs[0].dtype))
    n_in = len(flat_in)

    def body(*refs):
        ins = refs[:n_in]
        outs = refs[n_in:n_in + nt]
        send_sems, recv_sems, local_sems = refs[n_in + nt:]
        pos = _mesh_pos()
        me = 4 * pos[0] + 2 * pos[1] + pos[2]
        srcs_of, base = [], 0
        for t in range(nt):
            srcs_of.append(ins[base:base + counts[t]])
            base += counts[t]

        def src_view(t, l, slot):
            ref = srcs_of[t][l]
            return ref if tensors[t][0] == "gather" else ref.at[slot]

        def all_layers(t, slot):
            return outs[t].at[pl.ds(0, counts[t]), slot]

        for t in range(nt):
            for l in range(counts[t]):
                pltpu.make_async_copy(src_view(t, l, me), outs[t].at[l, me], local_sems.at[t]).start()
        for t in range(nt):
            for k in range(1, N_DEV):
                peer, pid = _peer(pos, k)
                for l in range(counts[t]):
                    pltpu.make_async_remote_copy(
                        src_ref=src_view(t, l, pid), dst_ref=outs[t].at[l, me],
                        send_sem=send_sems.at[t, k - 1], recv_sem=recv_sems.at[t, k - 1],
                        device_id=peer, device_id_type=pl.DeviceIdType.MESH).start()
        for t in range(nt):
            for k in range(1, N_DEV):
                peer, pid = _peer(pos, k)
                pltpu.make_async_remote_copy(
                    src_ref=all_layers(t, pid), dst_ref=all_layers(t, pid),
                    send_sem=send_sems.at[t, k - 1], recv_sem=recv_sems.at[t, k - 1],
                    device_id=peer, device_id_type=pl.DeviceIdType.MESH).wait()
        for t in range(nt):
            pltpu.make_async_copy(all_layers(t, me), all_layers(t, me), local_sems.at[t]).wait()

    any_spec = pl.BlockSpec(memory_space=pl.ANY)
    return pl.pallas_call(
        body, out_shape=tuple(out_shape), in_specs=[any_spec] * n_in, out_specs=tuple([any_spec] * nt),
        scratch_shapes=[pltpu.SemaphoreType.DMA((nt, N_DEV - 1)), pltpu.SemaphoreType.DMA((nt, N_DEV - 1)),
                        pltpu.SemaphoreType.DMA((nt,))],
        name=name)(*flat_in)


HBM_SPEC = pl.BlockSpec(memory_space=pltpu.HBM)
SEM_SPEC = pl.BlockSpec(memory_space=pltpu.SEMAPHORE)
DATAFLOW = pltpu.SideEffectType.DATAFLOW_SIDE_EFFECTING


def _hbm(a):
    return pltpu.with_memory_space_constraint(a, pltpu.HBM)


def landing_zones(mode, srcs):
    pos = _mesh_pos()
    me = 4 * pos[0] + 2 * pos[1] + pos[2]
    lands = []
    for s in srcs:
        R, C = s.shape[-2:]
        own = s[None] if mode == "gather" else lax.dynamic_slice(s, (me, 0, 0), (1, R, C))
        lands.append(lax.dynamic_update_slice(lax.empty((N_DEV, R, C), s.dtype), own, (me, 0, 0)))
    return lands


def exchange_start(name, mode, srcs, lands, after=None):
    n = len(srcs)
    extra = [] if after is None else [after]

    def body(*refs):
        src_refs, land_refs = refs[:n], refs[n:2 * n]
        send_sems, recv_sems = refs[2 * n + len(extra)], refs[2 * n + len(extra) + 1]
        token = refs[-1]
        pos = _mesh_pos()
        me = 4 * pos[0] + 2 * pos[1] + pos[2]
        for t in range(n):
            for k in range(1, N_DEV):
                peer, pid = _peer(pos, k)
                src = src_refs[t] if mode == "gather" else src_refs[t].at[pid]
                pltpu.make_async_remote_copy(
                    src_ref=src, dst_ref=land_refs[t].at[me], send_sem=send_sems.at[t], recv_sem=recv_sems.at[t],
                    device_id=peer, device_id_type=pl.DeviceIdType.MESH).start()
        token[...] = jnp.zeros_like(token)

    thru = [pltpu.HBM(a.shape, a.dtype) for a in list(srcs) + list(lands)]
    out_shape = (pltpu.SemaphoreType.DMA((n,)), pltpu.SemaphoreType.DMA((n,)), *thru,
                 jax.ShapeDtypeStruct((8, LANES), F32))
    res = pl.pallas_call(
        body, out_shape=out_shape, in_specs=[HBM_SPEC] * (2 * n) + [pl.BlockSpec(memory_space=pl.ANY)] * len(extra),
        out_specs=(SEM_SPEC, SEM_SPEC, *([HBM_SPEC] * (2 * n)), pl.BlockSpec(memory_space=pltpu.VMEM)),
        input_output_aliases={i: 2 + i for i in range(2 * n)}, name=name,
        compiler_params=pltpu.CompilerParams(has_side_effects=DATAFLOW))(
            *[_hbm(a) for a in list(srcs) + list(lands)], *extra)
    return res[0], res[1], list(res[2:2 + n]), list(res[2 + n:2 + 2 * n]), res[-1]


def exchange_wait(name, send_sems, recv_sems, srcs, lands, after):
    n = len(srcs)

    def body(*refs):
        land_refs = refs[n:2 * n]
        s_sems, r_sems = refs[2 * n], refs[2 * n + 1]
        pos = _mesh_pos()
        for t in range(n):
            seven = land_refs[t].at[pl.ds(0, N_DEV - 1)]
            cp = pltpu.make_async_remote_copy(
                src_ref=seven, dst_ref=seven, send_sem=s_sems.at[t], recv_sem=r_sems.at[t],
                device_id=pos, device_id_type=pl.DeviceIdType.MESH)
            cp.wait_send()
            cp.wait_recv()

    arrs = list(srcs) + list(lands)
    afters = list(after) if isinstance(after, (list, tuple)) else [after]
    res = pl.pallas_call(
        body, out_shape=tuple(pltpu.HBM(a.shape, a.dtype) for a in arrs),
        in_specs=[HBM_SPEC] * (2 * n) + [SEM_SPEC, SEM_SPEC] + [pl.BlockSpec(memory_space=pl.ANY)] * len(afters),
        out_specs=tuple([HBM_SPEC] * (2 * n)), input_output_aliases={i: i for i in range(2 * n)}, name=name,
        compiler_params=pltpu.CompilerParams(has_side_effects=DATAFLOW))(*arrs, send_sems, recv_sems, *afters)
    return list(res[n:])


def reduce_parts(name, parts):
    n, R, C = parts.shape
    tr = _pick(R, max(8, (1 << 20) // (C * 4) // 8 * 8), 8)

    def body(p_ref, o_ref):
        acc = p_ref[0].astype(F32)
        for s in range(1, n):
            acc = acc + p_ref[s].astype(F32)
        o_ref[...] = acc

    return pl.pallas_call(
        body, out_shape=jax.ShapeDtypeStruct((R, C), F32), grid=(R // tr,),
        in_specs=[pl.BlockSpec((n, tr, C), lambda i: (0, i, 0))],
        out_specs=pl.BlockSpec((tr, C), lambda i: (i, 0)), name=name,
        compiler_params=_cparams("parallel"))(parts)


def adamw(name, w, m, v, parts, first=0, prev=None):
    L, R, C = w.shape
    nl = len(parts)
    n = parts[0].shape[0]
    tr = _pick(R, max(8, (1 << 19) // (C * 4) // 8 * 8), 8)
    n_prev = 0 if prev is None else 4

    def body(*refs):
        w_ref, m_ref, v_ref = refs[:3]
        p_refs = refs[3:3 + nl]
        g_ref, d_ref, nm_ref, nv_ref = refs[3 + nl + n_prev:]

        def update(p_ref):
            g = p_ref[0].astype(F32)
            for s in range(1, n):
                g = g + p_ref[s].astype(F32)
            wv = w_ref[...]
            mn = ADAM_B1 * m_ref[...] + (1.0 - ADAM_B1) * g
            vn = ADAM_B2 * v_ref[...] + (1.0 - ADAM_B2) * jnp.square(g)
            m_hat = mn / (1.0 - ADAM_B1 ** ADAM_STEP)
            v_hat = vn / (1.0 - ADAM_B2 ** ADAM_STEP)
            g_ref[...] = g
            d_ref[...] = -ADAM_LR * (m_hat / (jnp.sqrt(v_hat) + ADAM_EPS) + ADAM_WD * wv)
            nm_ref[...] = mn
            nv_ref[...] = vn

        for k in range(nl):
            pl.when(pl.program_id(0) == k)(functools.partial(update, p_refs[k]))

    blk = pl.BlockSpec((None, tr, C), lambda l, i: (first + l, i, 0))
    pspecs = [pl.BlockSpec((n, tr, C), lambda l, i, k=k: (0, jnp.where(l == k, i, 0), 0)) for k in range(nl)]
    sh = jax.ShapeDtypeStruct((L, R, C), F32)
    prev_args = [] if prev is None else list(prev)
    return pl.pallas_call(
        body, out_shape=(sh, sh, sh, sh), grid=(nl, R // tr),
        in_specs=[blk, blk, blk] + pspecs + [pl.BlockSpec(memory_space=pl.ANY)] * n_prev,
        out_specs=(blk, blk, blk, blk), input_output_aliases={3 + nl + q: q for q in range(n_prev)}, name=name,
        compiler_params=_cparams("arbitrary", "arbitrary"))(w, m, v, *parts, *prev_args)


def _cols_from_blocks(g):
    n, R, c = g.shape
    return g.transpose(1, 0, 2).reshape(R, n * c)


def _cols_to_blocks(w):
    R, C = w.shape
    return w.reshape(R, N_DEV, C // N_DEV).transpose(1, 0, 2)


def _uq_permute(w):
    lead = w.shape[:-1]
    w4 = w.reshape(lead + (MLA_HEADS, MLA_NOPE + MLA_ROPE))
    nope = w4[..., :MLA_NOPE].reshape(lead + (MLA_HEADS * MLA_NOPE,))
    rope = jnp.pad(w4[..., MLA_NOPE:], [(0, 0)] * (w4.ndim - 1) + [(0, LANES - MLA_ROPE)])
    return jnp.concatenate([nope, rope.reshape(lead + (MLA_HEADS * LANES,))], axis=-1)


def _uq_unpermute(w):
    lead = w.shape[:-1]
    n = MLA_HEADS * MLA_NOPE
    nope = w[..., :n].reshape(lead + (MLA_HEADS, MLA_NOPE))
    rope = w[..., n:].reshape(lead + (MLA_HEADS, LANES))[..., :MLA_ROPE]
    return jnp.concatenate([nope, rope], axis=-1).reshape(lead + (MLA_HEADS * (MLA_NOPE + MLA_ROPE),))


def _ukv_permute(w):
    lead = w.shape[:-1]
    w4 = w.reshape(lead + (MLA_HEADS, 2, MLA_NOPE))
    return jnp.swapaxes(w4, -3, -2).reshape(lead + (2 * MLA_HEADS * MLA_NOPE,))


def _ukv_unpermute(w):
    lead = w.shape[:-1]
    w4 = w.reshape(lead + (2, MLA_HEADS, MLA_NOPE))
    return jnp.swapaxes(w4, -3, -2).reshape(lead + (2 * MLA_HEADS * MLA_NOPE,))


SMALL = ["norm1_g", "mla_q_norm_g", "mla_kv_norm_g", "fox_b_f", "norm2_g", "ffn_conv_b", "final_norm_g"]
SMALL_TILE = 8 * LANES


def _pack_small(d):
    flat = jnp.concatenate([d[n].reshape(-1).astype(F32) for n in SMALL])
    pad = -flat.shape[0] % SMALL_TILE
    return jnp.pad(flat, (0, pad)).reshape(-1, LANES)


def _unpack_small(packed, like):
    flat = packed.reshape(-1)
    out, o = {}, 0
    for n in SMALL:
        sz = int(np.prod(like[n].shape))
        out[n] = flat[o:o + sz].reshape(like[n].shape)
        o += sz
    return out


WEIGHTS = ["norm1_g", "w_in", "mla_q_norm_g", "mla_kv_norm_g", "mla_w_uq", "mla_w_ukv", "fox_b_f", "w_br_fox",
           "w_br_mla", "w_br_ret", "w_out", "norm2_g", "ffn_w_up", "ffn_w_gate", "ffn_conv_w", "ffn_conv_b",
           "ffn_w_down", "final_norm_g"]
EARLY = ["w_in", "mla_w_uq", "mla_w_ukv"]
LATE = ["w_br_fox", "w_br_mla", "w_br_ret", "w_out", "ffn_w_up", "ffn_w_gate", "ffn_conv_w", "ffn_w_down"]
FFN = ["ffn_w_up", "ffn_w_gate", "ffn_conv_w", "ffn_w_down"]
TRANSPOSED = ("ffn_w_up", "ffn_w_gate")
BIG = EARLY + LATE
X_EARLY = ["w_in_mix", "mla_w_uq", "mla_w_ukv"]
X_LATE = ["w_in_gates"] + LATE
X_MID = ["w_in_gates", "w_out", "w_br_fox", "w_br_mla", "w_br_ret"]
X_REST = ["mla_w_uq", "mla_w_ukv", "w_in_mix"]


def kernel(x, norm1_g, w_in, mla_q_norm_g, mla_kv_norm_g, mla_w_uq, mla_w_ukv, fox_b_f, w_br_fox, w_br_mla, w_br_ret, w_out, norm2_g, ffn_w_up, ffn_w_gate, ffn_conv_w, ffn_conv_b, ffn_w_down, final_norm_g, loss_target, m_norm1_g, m_w_in, m_mla_q_norm_g, m_mla_kv_norm_g, m_mla_w_uq, m_mla_w_ukv, m_fox_b_f, m_w_br_fox, m_w_br_mla, m_w_br_ret, m_w_out, m_norm2_g, m_ffn_w_up, m_ffn_w_gate, m_ffn_conv_w, m_ffn_conv_b, m_ffn_w_down, m_final_norm_g, v_norm1_g, v_w_in, v_mla_q_norm_g, v_mla_kv_norm_g, v_mla_w_uq, v_mla_w_ukv, v_fox_b_f, v_w_br_fox, v_w_br_mla, v_w_br_ret, v_w_out, v_norm2_g, v_ffn_w_up, v_ffn_w_gate, v_ffn_conv_w, v_ffn_conv_b, v_ffn_w_down, v_final_norm_g):
    env = dict(locals())
    W = {n: env[n] for n in WEIGHTS}
    Mo = {n: env["m_" + n] for n in WEIGHTS}
    Vo = {n: env["v_" + n] for n in WEIGHTS}
    S, D = x.shape[1], x.shape[2]
    L = w_in.shape[0]
    lay = InLayout(D)
    NP = lay.total
    f = ffn_w_up.shape[-1]
    xs = x.reshape(S, D)
    tgt = loss_target.reshape(S, D)

    local = {n: W[n].astype(BF16) for n in BIG if n != "w_in"}
    w_gates, w_mix = lay.permute(W["w_in"])
    local["w_in_gates"], local["w_in_mix"] = w_gates.astype(BF16), w_mix.astype(BF16)
    pending = {}
    token = None
    for l in range(L):
        for grp, names in (("a", X_EARLY), ("b", X_LATE)):
            srcs = [local[n][l] for n in names]
            *flight, token = exchange_start(f"gather_start_{l}{grp}", "gather", srcs, landing_zones("gather", srcs),
                                            token)
            pending[l, grp] = flight
    gather_token = token
    cbias_all = ffn_conv_b.reshape(L, 1, N_DEV, 1, f)

    def early_weights(l, after):
        g = dict(zip(X_EARLY, exchange_wait(f"gather_wait_{l}a", *pending[l, "a"], after)))
        return dict(Win=g["w_in_mix"].reshape(1, D, NP), Wuq=_uq_permute(_cols_from_blocks(g["mla_w_uq"])),
                    Wukv=_ukv_permute(_cols_from_blocks(g["mla_w_ukv"])))

    def late_weights(l, after):
        g = dict(zip(X_LATE, exchange_wait(f"gather_wait_{l}b", *pending[l, "b"], after)))
        return dict(
            Wgates=g["w_in_gates"].reshape(1, D, 3 * D),
            Wout=g["w_out"].reshape(1, D, D), Wbf=_cols_from_blocks(g["w_br_fox"]),
            Wbm=_cols_from_blocks(g["w_br_mla"]), Wbr=_cols_from_blocks(g["w_br_ret"]), Wup=g["ffn_w_up"][None],
            Wgate=g["ffn_w_gate"][None], Wdown=g["ffn_w_down"][None], Wconv=g["ffn_conv_w"].astype(F32)[None],
            cbias=cbias_all[l])

    tab64 = rope_tables(S, MLA_ROPE)
    tab128 = rope_tables(S, RET_DK)
    fox_scale = FOX_DH ** -0.5
    mla_scale = (MLA_NOPE + MLA_ROPE) ** -0.5
    ret_kscale = RET_DK ** -0.5
    R = S // LANES
    AB = _attn_block(S)
    NOPE_W = MLA_HEADS * MLA_NOPE

    def vec(a):
        return a.reshape(1, -1)

    saved = []
    xc = xs
    for l in range(L):
        Wl = early_weights(l, gather_token if l == 0 else xc)
        Win, Wuq, Wukv = Wl["Win"], Wl["Wuq"], Wl["Wukv"]
        s = {"x": xc, "W": Wl}
        h1 = rms_fwd("norm1", xc, 0, D, vec(norm1_g[l]), BF16)
        P = mm_nn("in_proj", h1, Win, F32, b_lead=0)
        s.update(h1=h1, P=P)
        ff_off = lay.off["ff"]
        ft = P[:, ff_off:ff_off + FOX_HEADS].T.reshape(FOX_HEADS, R, LANES)
        bfl = jnp.broadcast_to(fox_b_f[l].reshape(FOX_HEADS, 1, 1), (FOX_HEADS, 1, LANES))
        kbias = fox_gate_fwd("fox_gate", ft, bfl).reshape(FOX_HEADS, S // AB, 1, AB)
        o_fox, lse_fox = attn_fwd("fox_attn", "fox", P, lay.cb("fq", LANES), P, lay.cb("fk", LANES),
                                  P, lay.cb("fv", LANES), FOX_HEADS, FOX_DH, fox_scale, kbias=kbias)
        s.update(ft=ft, bfl=bfl, kbias=kbias, o_fox=o_fox, lse_fox=lse_fox)
        cqn = rms_fwd("mla_q_norm", P, lay.cb("mq", MLA_Q_LORA), MLA_Q_LORA, vec(mla_q_norm_g[l]), BF16)
        qall = mm_nn("mla_uq", cqn, Wuq, F32)
        ckvn = rms_fwd("mla_kv_norm", P, lay.cb("mkv", MLA_KV_LORA), MLA_KV_LORA, vec(mla_kv_norm_g[l]), BF16)
        kvall = mm_nn("mla_ukv", ckvn, Wukv, F32)
        qrope = rope_apply("mla_q_rope", qall, NOPE_W // LANES, MLA_HEADS, tab64, 1.0, F32)
        krope = rope_apply("mla_k_rope", P, lay.cb("mkr", LANES), 1, tab64, 1.0, F32)
        o_mla, lse_mla = attn_fwd("mla_attn", "mla", qall, 0, kvall, 0, kvall, NOPE_W // MLA_V, MLA_HEADS, MLA_V,
                                  mla_scale, q2=qrope, q2_cb=0, k2=krope)
        s.update(cqn=cqn, qall=qall, ckvn=ckvn, kvall=kvall, qrope=qrope, krope=krope, o_mla=o_mla,
                 lse_mla=lse_mla)
        rq = rope_apply("ret_q_rope", P, lay.cb("rq", LANES), RET_HEADS, tab128, 1.0, F32)
        rk = rope_apply("ret_k_rope", P, lay.cb("rk", LANES), RET_HEADS, tab128, ret_kscale, F32)
        o_ret, _ = attn_fwd("ret_attn", "ret", rq, 0, rk, 0, P, lay.cb("rv", RET_DV), RET_HEADS, RET_DV, 1.0)
        c_ret = ret_out_fwd("ret_out", o_ret, P, lay.cb("rg", RET_DV), BF16)
        s.update(rq=rq, rk=rk, o_ret=o_ret, c_ret=c_ret)
        Wl.update(late_weights(l, (o_fox, o_mla, c_ret)))
        Wout, Wbf, Wbm, Wbr = Wl["Wout"], Wl["Wbf"], Wl["Wbm"], Wl["Wbr"]
        Wup, Wgate, Wdown, Wconv, cbias = (Wl[k] for k in ("Wup", "Wgate", "Wdown", "Wconv", "cbias"))
        A = mm_nn("br_fox", o_fox, Wbf, F32)
        Bm = mm_nn("br_mla", o_mla, Wbm, F32)
        C = mm_nn("br_ret", c_ret, Wbr, F32)
        Pg = mm_nn("gate_proj", h1, Wl["Wgates"], F32, b_lead=0)
        s["Pg"] = Pg
        merged = merge_fwd("merge", Pg, 0, A, Bm, C, BF16)
        x2 = mm_nn("out_proj", merged, Wout, F32, b_lead=0, res=xc)
        s.update(A=A, Bm=Bm, C=C, merged=merged, x2=x2)
        h2 = rms_fwd("norm2", x2, 0, D, vec(norm2_g[l]), BF16)
        u = ffn_up("ffn_up", h2, Wup, 0, F32)
        gt = ffn_up("ffn_gate", h2, Wgate, 0, F32)
        act = ffn_act_fwd("ffn_act", u, gt, Wconv, cbias, 0, BF16)
        xc = ffn_down("ffn_down", act, Wdown, 0, x2, F32)
        s.update(h2=h2, u=u, gt=gt, act=act)
        saved.append(s)

    loss_tile, dx, dgf = loss_head("loss_head", xc, vec(final_norm_g), tgt)
    loss = lax.psum(loss_tile[0, 0], ("x", "y", "c"))

    gbig = {n: [None] * L for n in BIG + ["w_in_gates", "w_in_mix"]}
    gsmall = {n: [None] * L for n in SMALL if n != "final_norm_g"}
    scattering = {}
    scatter_token = None

    def start_scatter(name, names, l):
        srcs = [gbig[n][l] for n in names]
        *flight, tok = exchange_start(name, "scatter", srcs, landing_zones("scatter", srcs))
        return flight, tok

    for l in reversed(range(L)):
        s = saved[l]
        P = s["P"]
        Wl = s["W"]
        Win, Wout, Wuq, Wukv, Wbf, Wbm, Wbr = (Wl[k] for k in ("Win", "Wout", "Wuq", "Wukv", "Wbf", "Wbm", "Wbr"))
        Wup, Wgate, Wdown, Wconv, cbias = (Wl[k] for k in ("Wup", "Wgate", "Wdown", "Wconv", "cbias"))
        dxb = (dx if scatter_token is None else dx + scatter_token[0, 0]).astype(BF16)
        dact = ffn_down_bwd_act("ffn_down_da", dxb, Wdown, 0, BF16)
        gbig["ffn_w_down"][l] = ffn_down_bwd_w("ffn_down_dw", s["act"], dxb, BF16)
        g, dgt = ffn_act_bwd_point("ffn_act_bwd", s["u"], s["gt"], Wconv, cbias, 0, dact)
        du, dcw, dcb = ffn_act_bwd_conv("ffn_conv_bwd", s["u"], g, Wconv, 0)
        gbig["ffn_conv_w"][l] = dcw.astype(BF16)
        gsmall["ffn_conv_b"][l] = dcb.reshape(-1)
        gbig["ffn_w_up"][l] = ffn_down_bwd_w("ffn_up_dw", du, s["h2"], BF16)
        gbig["ffn_w_gate"][l] = ffn_down_bwd_w("ffn_gate_dw", dgt, s["h2"], BF16)
        dh2 = ffn_up_bwd_h("ffn_up_dh", du, Wup, 0, None, F32)
        dh2 = ffn_up_bwd_h("ffn_gate_dh", dgt, Wgate, 0, dh2, BF16)
        dx2, dg2 = rms_bwd("norm2_bwd", s["x2"], 0, D, vec(norm2_g[l]), dh2, F32, res=dx)
        gsmall["norm2_g"][l] = dg2.reshape(-1)
        scattering[l, "ffn"], scatter_token = start_scatter(f"scatter_start_{l}ffn", FFN, l)
        dx2b = (dx2 + scatter_token[0, 0]).astype(BF16)
        dmerged = mm_nt("out_proj_dm", dx2b, Wout, BF16, b_lead=0)
        gbig["w_out"][l] = mm_tn("out_proj_dw", s["merged"], dx2b, BF16).reshape(N_DEV, D // N_DEV, D)
        dgates, dA, dB, dC = merge_bwd("merge_bwd", s["Pg"], 0, s["A"], s["Bm"], s["C"], dmerged)
        gbig["w_br_fox"][l] = _cols_to_blocks(mm_tn("br_fox_dw", s["o_fox"], dA, BF16))
        gbig["w_br_mla"][l] = _cols_to_blocks(mm_tn("br_mla_dw", s["o_mla"], dB, BF16))
        gbig["w_br_ret"][l] = _cols_to_blocks(mm_tn("br_ret_dw", s["c_ret"], dC, BF16))
        gbig["w_in_gates"][l] = mm_tn("gate_proj_dw", s["h1"], dgates, BF16).reshape(N_DEV, D // N_DEV, 3 * D)
        scattering[l, "mid"], scatter_token = start_scatter(f"scatter_start_{l}mid", X_MID, l)
        dh1_gates = mm_nt("gate_proj_dh", dgates, Wl["Wgates"], F32, b_lead=0)
        do_fox = mm_nt("br_fox_do", dA + scatter_token[0, 0].astype(BF16), Wbf, F32)
        do_mla = mm_nt("br_mla_do", dB, Wbm, F32)
        dc_ret = mm_nt("br_ret_do", dC, Wbr, BF16)
        do_ret, drg = ret_out_bwd("ret_out_bwd", s["o_ret"], P, lay.cb("rg", RET_DV), dc_ret)
        rb = attn_bwd("ret_attn_bwd", "ret", s["rq"], 0, s["rk"], 0, P, lay.cb("rv", RET_DV), RET_HEADS, RET_DV,
                      1.0, do_ret)
        drq = rope_apply("ret_q_rope_bwd", rb["dq1"], 0, RET_HEADS, tab128, 1.0, BF16, transpose=True)
        drk = rope_apply("ret_k_rope_bwd", rb["dk1"], 0, RET_HEADS, tab128, ret_kscale, BF16, transpose=True)
        drv = rb["dv"].astype(BF16)
        mb = attn_bwd("mla_attn_bwd", "mla", s["qall"], 0, s["kvall"], 0, s["kvall"], NOPE_W // MLA_V, MLA_HEADS,
                      MLA_V, mla_scale, do_mla, o=s["o_mla"], lse=s["lse_mla"], q2=s["qrope"], q2_cb=0,
                      k2=s["krope"])
        dqrope = rope_apply("mla_q_rope_bwd", mb["dq2"], 0, MLA_HEADS, tab64, 1.0, BF16, transpose=True)
        dkr_sum = head_sum("mla_k_rope_sum", mb["dk2h"], MLA_HEADS, F32)
        dmkr = rope_apply("mla_k_rope_bwd", dkr_sum, 0, 1, tab64, 1.0, BF16, transpose=True)
        dqall = jnp.concatenate([mb["dq1"].astype(BF16), dqrope], axis=1)
        dkvall = jnp.concatenate([mb["dk1"].astype(BF16), mb["dv"].astype(BF16)], axis=1)
        dcqn = mm_nt("mla_uq_dx", dqall, Wuq, F32)
        dckvn = mm_nt("mla_ukv_dx", dkvall, Wukv, F32)
        guq = _uq_unpermute(mm_tn("mla_uq_dw", s["cqn"], dqall, BF16))
        gukv = _ukv_unpermute(mm_tn("mla_ukv_dw", s["ckvn"], dkvall, BF16))
        gbig["mla_w_uq"][l] = _cols_to_blocks(guq)
        gbig["mla_w_ukv"][l] = _cols_to_blocks(gukv)
        dmq, dgq = rms_bwd("mla_q_norm_bwd", P, lay.cb("mq", MLA_Q_LORA), MLA_Q_LORA, vec(mla_q_norm_g[l]),
                           dcqn, BF16)
        dmkv, dgkv = rms_bwd("mla_kv_norm_bwd", P, lay.cb("mkv", MLA_KV_LORA), MLA_KV_LORA,
                             vec(mla_kv_norm_g[l]), dckvn, BF16)
        gsmall["mla_q_norm_g"][l] = dgq.reshape(-1)
        gsmall["mla_kv_norm_g"][l] = dgkv.reshape(-1)
        fb = attn_bwd("fox_attn_bwd", "fox", P, lay.cb("fq", LANES), P, lay.cb("fk", LANES), P,
                      lay.cb("fv", LANES), FOX_HEADS, FOX_DH, fox_scale, do_fox, o=s["o_fox"], lse=s["lse_fox"],
                      kbias=s["kbias"])
        drow = fb["drow"].reshape(S, FOX_HEADS, LANES)[:, :, 0].T.reshape(FOX_HEADS, R, LANES)
        dft, dbf = fox_gate_bwd("fox_gate_bwd", s["ft"], s["bfl"], fb["dkb"].reshape(FOX_HEADS, R, LANES) - drow)
        gsmall["fox_b_f"][l] = dbf[:, 0, 0]
        dff = jnp.pad(dft.reshape(FOX_HEADS, S).T, ((0, 0), (0, LANES - FOX_HEADS))).astype(BF16)
        segs = dict(rv=drv, rg=drg, mq=dmq, rq=drq, rk=drk, mkv=dmkv, fq=fb["dq1"].astype(BF16),
                    fk=fb["dk1"].astype(BF16), fv=fb["dv"].astype(BF16), mkr=dmkr, ff=dff)
        dP = jnp.concatenate([segs[n] for n in lay.order], axis=1)
        gbig["w_in_mix"][l] = mm_tn("in_proj_dw", s["h1"], dP, BF16).reshape(N_DEV, D // N_DEV, NP)
        scattering[l, "rest"], scatter_token = start_scatter(f"scatter_start_{l}rest", X_REST, l)
        dh1 = mm_nt("in_proj_dh", dP, Win, BF16, b_lead=0, res=dh1_gates)
        dx, dg1 = rms_bwd("norm1_bwd", s["x"], 0, D, vec(norm1_g[l]) + scatter_token[0:1, 0:1], dh1, F32, res=dx2)
        gsmall["norm1_g"][l] = dg1.reshape(-1)

    small_like = {n: W[n] for n in SMALL}
    small_part = {n: jnp.stack(gsmall[n]) for n in gsmall}
    small_part["final_norm_g"] = dgf.reshape(-1)
    small_recv = exchange("gather_small_grads", [("gather", [_pack_small(small_part)])])[0]
    ps = adamw("adamw_small", _pack_small(small_like)[None], _pack_small({n: Mo[n] for n in SMALL})[None],
               _pack_small({n: Vo[n] for n in SMALL})[None], [small_recv[0]])

    def received(l, after):
        r = {}
        for grp, names in (("ffn", FFN), ("mid", X_MID), ("rest", X_REST)):
            r.update(zip(names, exchange_wait(f"scatter_wait_{l}{grp}", *scattering[l, grp], after)))
        r["w_in"] = lay.unpermute(reduce_parts("w_in_gates_grad_sum", r["w_in_gates"]),
                                  reduce_parts("w_in_mix_grad_sum", r["w_in_mix"]))[None]
        return r

    def oriented(n, a):
        return jnp.swapaxes(a, 1, 2) if n in TRANSPOSED else a

    out = {}
    if L > 1:
        recv = [received(l, dx) for l in range(1, L)]
        for n in BIG:
            out[n] = adamw("adamw_" + n, oriented(n, W[n]), oriented(n, Mo[n]), oriented(n, Vo[n]),
                           [r[n] for r in recv], first=1)
    recv0 = received(0, out[BIG[-1]][0] if L > 1 else dx)
    for n in BIG:
        res = adamw("adamw0_" + n, oriented(n, W[n]), oriented(n, Mo[n]), oriented(n, Vo[n]), [recv0[n]],
                    first=0, prev=out.get(n))
        out[n] = tuple(oriented(n, a) for a in res)
    small_out = [_unpack_small(a[0], small_like) for a in ps]
    for n in SMALL:
        out[n] = tuple(so[n] for so in small_out)

    grads = [out[n][0] for n in WEIGHTS]
    deltas = [out[n][1] for n in WEIGHTS]
    new_m = [out[n][2] for n in WEIGHTS]
    new_v = [out[n][3] for n in WEIGHTS]
    return (loss, dx.reshape(1, S, D), *grads, *deltas, *new_m, *new_v)
```

```python
import functools
import math

import numpy as np
import jax
import jax.numpy as jnp
from jax import lax
from jax.experimental import pallas as pl
from jax.experimental.pallas import tpu as pltpu

F32 = jnp.float32
BF16 = jnp.bfloat16

CHUNK = 64
NORM_EPS = 1e-6
ROPE_THETA = 10000.0
FOX_HEADS, FOX_DH = 6, 128
FOX_W = FOX_HEADS * FOX_DH
MLA_HEADS, MLA_NOPE, MLA_ROPE, MLA_V = 6, 128, 64, 128
MLA_Q_LORA, MLA_KV_LORA = 512, 256
MLA_W = MLA_HEADS * MLA_V
RET_HEADS, RET_DK, RET_DV = 4, 128, 256
RET_QK_W, RET_V_W = RET_HEADS * RET_DK, RET_HEADS * RET_DV
ADAM_LR, ADAM_B1, ADAM_B2, ADAM_EPS, ADAM_WD, ADAM_STEP = 0.001, 0.9, 0.999, 1e-08, 0.01, 10

N_DEV = 8
LANES = 128
V7X_VMEM_LIMIT_BYTES = 52 * 1024 * 1024
NEG_BIG = -1e30
HIGHEST = lax.Precision.HIGHEST

NT_DIMS = (((1,), (1,)), ((), ()))
TN_DIMS = (((0,), (0,)), ((), ()))
NN_DIMS = (((1,), (0,)), ((), ()))


def _pick(n, cap, mult=LANES):
    best = None
    for t in range(mult, min(n, cap) + 1, mult):
        if n % t == 0:
            best = t
    return n if best is None else best


def _cparams(*sem):
    return pltpu.CompilerParams(dimension_semantics=sem, vmem_limit_bytes=V7X_VMEM_LIMIT_BYTES)


class InLayout:
    def __init__(self, d_model):
        d = d_model
        self.d = d
        orig = dict(fq=(0, FOX_W), fk=(FOX_W, FOX_W), fv=(2 * FOX_W, FOX_W), ff=(3 * FOX_W, FOX_HEADS))
        o = 3 * FOX_W + FOX_HEADS
        for name, w in (("mq", MLA_Q_LORA), ("mkv", MLA_KV_LORA), ("mkr", MLA_ROPE), ("rq", RET_QK_W),
                        ("rk", RET_QK_W), ("rv", RET_V_W), ("rg", RET_V_W), ("gates", 3 * d)):
            orig[name] = (o, w)
            o += w
        self.orig = orig
        self.orig_width = o
        order = ["rv", "rg", "mq", "rq", "rk", "mkv", "fq", "fk", "fv", "mkr", "ff"]
        self.order = order
        self.off, self.width = {}, {}
        p = 0
        for name in order:
            w = orig[name][1]
            wp = -(-w // LANES) * LANES
            self.off[name], self.width[name] = p, wp
            p += wp
        self.total = p
        self.gates = 3 * d

    def cb(self, name, block):
        assert self.off[name] % block == 0, (name, block)
        return self.off[name] // block

    def permute(self, w):
        parts = []
        for name in self.order:
            o, n = self.orig[name]
            seg = w[..., o:o + n]
            pad = self.width[name] - n
            if pad:
                seg = jnp.pad(seg, [(0, 0)] * (w.ndim - 1) + [(0, pad)])
            parts.append(seg)
        o, n = self.orig["gates"]
        return w[..., o:o + n], jnp.concatenate(parts, axis=-1)

    def unpermute(self, gates, mix):
        names = sorted(self.order, key=lambda n: self.orig[n][0])
        return jnp.concatenate([mix[..., self.off[n]:self.off[n] + self.orig[n][1]] for n in names] + [gates],
                               axis=-1)


def _mm(name, a, b, out_shape, grid, a_spec, b_spec, o_spec, dims, acc_shape, res=None, nsub=0):
    nk = grid[-1]
    has_res = res is not None

    def body(*refs):
        if has_res:
            a_ref, b_ref, r_ref, o_ref = refs[:4]
        else:
            a_ref, b_ref, o_ref = refs[:3]
            r_ref = None
        if nsub:
            prod = lax.dot_general(a_ref[0].astype(BF16), b_ref[0].astype(BF16), dims, preferred_element_type=F32)
            for q in range(1, nsub):
                prod = prod + lax.dot_general(a_ref[q].astype(BF16), b_ref[q].astype(BF16), dims,
                                              preferred_element_type=F32)
        else:
            prod = lax.dot_general(a_ref[...].astype(BF16), b_ref[...].astype(BF16), dims,
                                   preferred_element_type=F32)
        if nk == 1:
            if has_res:
                prod = prod + r_ref[...].astype(F32)
            o_ref[...] = prod.astype(o_ref.dtype)
        else:
            acc_ref = refs[-1]
            k = pl.program_id(len(grid) - 1)

            @pl.when(k == 0)
            def _():
                acc_ref[...] = prod

            @pl.when(k > 0)
            def _():
                acc_ref[...] += prod

            @pl.when(k == nk - 1)
            def _():
                r = acc_ref[...]
                if has_res:
                    r = r + r_ref[...].astype(F32)
                o_ref[...] = r.astype(o_ref.dtype)

    in_specs = [a_spec, b_spec] + ([o_spec] if has_res else [])
    args = (a, b) + ((res,) if has_res else ())
    scratch = [pltpu.VMEM(acc_shape, F32)] if nk > 1 else []
    sem = ("parallel",) * (len(grid) - 1) + ("arbitrary",)
    return pl.pallas_call(body, out_shape=out_shape, grid=grid, in_specs=in_specs, out_specs=o_spec,
                          scratch_shapes=scratch, name=name, compiler_params=_cparams(*sem))(*args)


def mm_nn(name, a, b, out_dtype, b_lead=None, res=None):
    M, K = a.shape
    N = b.shape[-1]
    tm, tn, tk = _pick(M, 1024, 8), _pick(N, 1536), _pick(K, 2048)
    grid = (M // tm, N // tn, K // tk)
    a_spec = pl.BlockSpec((tm, tk), lambda i, j, k: (i, k))
    if b_lead is None:
        b_spec = pl.BlockSpec((tk, tn), lambda i, j, k: (k, j))
    else:
        b_spec = pl.BlockSpec((None, tk, tn), lambda i, j, k: (b_lead, k, j))
    o_spec = pl.BlockSpec((tm, tn), lambda i, j, k: (i, j))
    return _mm(name, a, b, jax.ShapeDtypeStruct((M, N), out_dtype), grid, a_spec, b_spec, o_spec,
               NN_DIMS, (tm, tn), res)


def mm_nt(name, a, b, out_dtype, b_lead=None, res=None):
    M, N = a.shape
    K = b.shape[-2]
    tm, tko, tk = _pick(M, 1024, 8), _pick(K, 1024), _pick(N, 2048)
    grid = (M // tm, K // tko, N // tk)
    a_spec = pl.BlockSpec((tm, tk), lambda i, j, k: (i, k))
    if b_lead is None:
        b_spec = pl.BlockSpec((tko, tk), lambda i, j, k: (j, k))
    else:
        b_spec = pl.BlockSpec((None, tko, tk), lambda i, j, k: (b_lead, j, k))
    o_spec = pl.BlockSpec((tm, tko), lambda i, j, k: (i, j))
    return _mm(name, a, b, jax.ShapeDtypeStruct((M, K), out_dtype), grid, a_spec, b_spec, o_spec,
               NT_DIMS, (tm, tko), res)


def mm_tn(name, a, b, out_dtype):
    M, K = a.shape
    N = b.shape[-1]
    cap = 4096 if (a.dtype == BF16 and b.dtype == BF16) else 2048
    tko, tn, tk = _pick(K, 1024), _pick(N, 1280), _pick(M, cap, 8)
    grid = (K // tko, N // tn, M // tk)
    a_spec = pl.BlockSpec((tk, tko), lambda i, j, k: (k, i))
    b_spec = pl.BlockSpec((tk, tn), lambda i, j, k: (k, j))
    o_spec = pl.BlockSpec((tko, tn), lambda i, j, k: (i, j))
    return _mm(name, a, b, jax.ShapeDtypeStruct((K, N), out_dtype), grid, a_spec, b_spec, o_spec,
               TN_DIMS, (tko, tn))


FFN_SUB = 4
def ffn_up(name, h, w, l, out_dtype):
    M, D = h.shape
    f = w.shape[-1]
    tm = _pick(M, 1024, 8)
    grid = (M // tm, N_DEV, 1)
    return _mm(name, h, w, jax.ShapeDtypeStruct((N_DEV, M, f), out_dtype), grid,
               pl.BlockSpec((tm, D), lambda i, j, k: (i, 0)),
               pl.BlockSpec((None, None, D, f), lambda i, j, k: (l, j, 0, 0)),
               pl.BlockSpec((None, tm, f), lambda i, j, k: (j, i, 0)), NN_DIMS, (tm, f))


def ffn_down(name, act, w, l, res, out_dtype):
    _, M, f = act.shape
    D = w.shape[-1]
    tm, tn = _pick(M, 1024, 8), _pick(D, 1024)
    grid = (M // tm, D // tn, N_DEV // FFN_SUB)
    return _mm(name, act, w, jax.ShapeDtypeStruct((M, D), out_dtype), grid,
               pl.BlockSpec((FFN_SUB, tm, f), lambda i, j, k: (k, i, 0)),
               pl.BlockSpec((None, FFN_SUB, f, tn), lambda i, j, k: (l, k, 0, j)),
               pl.BlockSpec((tm, tn), lambda i, j, k: (i, j)), NN_DIMS, (tm, tn), res, nsub=FFN_SUB)


def ffn_down_bwd_act(name, dy, w, l, out_dtype):
    M, D = dy.shape
    f = w.shape[-2]
    tm = _pick(M, 1024, 8)
    grid = (M // tm, N_DEV, 1)
    return _mm(name, dy, w, jax.ShapeDtypeStruct((N_DEV, M, f), out_dtype), grid,
               pl.BlockSpec((tm, D), lambda i, j, k: (i, 0)),
               pl.BlockSpec((None, None, f, D), lambda i, j, k: (l, j, 0, 0)),
               pl.BlockSpec((None, tm, f), lambda i, j, k: (j, i, 0)), NT_DIMS, (tm, f))


def ffn_down_bwd_w(name, act, dy, out_dtype):
    _, M, f = act.shape
    D = dy.shape[-1]
    tn, tk = _pick(D, 1024), _pick(M, 4096 if dy.dtype == BF16 else 2048, 8)
    grid = (N_DEV, D // tn, M // tk)
    return _mm(name, act, dy, jax.ShapeDtypeStruct((N_DEV, f, D), out_dtype), grid,
               pl.BlockSpec((None, tk, f), lambda j, n, k: (j, k, 0)),
               pl.BlockSpec((tk, tn), lambda j, n, k: (k, n)),
               pl.BlockSpec((None, f, tn), lambda j, n, k: (j, 0, n)), TN_DIMS, (f, tn))


def ffn_up_bwd_h(name, du, w, l, res, out_dtype):
    _, M, f = du.shape
    D = w.shape[-2]
    tm, tn = _pick(M, 1024, 8), _pick(D, 1024)
    grid = (M // tm, D // tn, N_DEV // FFN_SUB)
    return _mm(name, du, w, jax.ShapeDtypeStruct((M, D), out_dtype), grid,
               pl.BlockSpec((FFN_SUB, tm, f), lambda i, j, k: (k, i, 0)),
               pl.BlockSpec((None, FFN_SUB, tn, f), lambda i, j, k: (l, k, j, 0)),
               pl.BlockSpec((tm, tn), lambda i, j, k: (i, j)), NT_DIMS, (tm, tn), res, nsub=FFN_SUB)


def _rms(xf, g):
    return xf * lax.rsqrt(jnp.mean(xf * xf, axis=-1, keepdims=True) + NORM_EPS) * g


def rms_fwd(name, x, cb, W, g, out_dtype):
    S = x.shape[0]
    tr = _pick(S, 256, 8)

    def body(x_ref, g_ref, o_ref):
        o_ref[...] = _rms(x_ref[...].astype(F32), g_ref[...]).astype(o_ref.dtype)

    return pl.pallas_call(
        body, out_shape=jax.ShapeDtypeStruct((S, W), out_dtype), grid=(S // tr,),
        in_specs=[pl.BlockSpec((tr, W), lambda i: (i, cb)), pl.BlockSpec((1, W), lambda i: (0, 0))],
        out_specs=pl.BlockSpec((tr, W), lambda i: (i, 0)), name=name, compiler_params=_cparams("parallel"))(x, g)


def rms_bwd(name, x, cb, W, g, dy, out_dtype, res=None):
    S = x.shape[0]
    tr = _pick(S, 256, 8)
    has_res = res is not None

    def body(*refs):
        if has_res:
            x_ref, g_ref, dy_ref, r_ref, dx_ref, dg_ref = refs
        else:
            x_ref, g_ref, dy_ref, dx_ref, dg_ref = refs
        _, vjp = jax.vjp(_rms, x_ref[...].astype(F32), g_ref[...])
        dx, dg = vjp(dy_ref[...].astype(F32))
        if has_res:
            dx = dx + r_ref[...]
        dx_ref[...] = dx.astype(dx_ref.dtype)

        @pl.when(pl.program_id(0) == 0)
        def _():
            dg_ref[...] = jnp.zeros_like(dg_ref)

        dg_ref[...] += dg

    row = pl.BlockSpec((tr, W), lambda i: (i, 0))
    vec = pl.BlockSpec((1, W), lambda i: (0, 0))
    in_specs = [pl.BlockSpec((tr, W), lambda i: (i, cb)), vec, row] + ([row] if has_res else [])
    args = (x, g, dy) + ((res,) if has_res else ())
    return pl.pallas_call(
        body, out_shape=(jax.ShapeDtypeStruct((S, W), out_dtype), jax.ShapeDtypeStruct((1, W), F32)),
        grid=(S // tr,), in_specs=in_specs, out_specs=(row, vec), name=name,
        compiler_params=_cparams("arbitrary"))(*args)


def rope_tables(S, d):
    pos = jnp.arange(S, dtype=F32)
    inv_freq = ROPE_THETA ** (-jnp.arange(0, d, 2, dtype=F32) / d)
    ang = pos[:, None] * inv_freq[None, :]
    cos, sin = jnp.cos(ang), jnp.sin(ang)
    half = d // 2
    z = jnp.zeros((S, LANES - d), F32)
    zh = jnp.zeros((S, half), F32)
    c = jnp.concatenate([cos, cos, z], axis=1)
    sa = jnp.concatenate([-sin, zh, z], axis=1)
    sb = jnp.concatenate([zh, sin, z], axis=1)
    return c, sa, sb, half


def rope_apply(name, x, cb, H, tabs, scale, out_dtype, transpose=False):
    c, sa, sb, half = tabs
    S = x.shape[0]
    tr = _pick(S, 512, 8)
    up, down = LANES - half, half

    def body(x_ref, c_ref, sa_ref, sb_ref, o_ref):
        xv = x_ref[...].astype(F32)
        if not transpose:
            y = xv * c_ref[...] + pltpu.roll(xv, up, 1) * sa_ref[...] + pltpu.roll(xv, down, 1) * sb_ref[...]
            y = y * scale
        else:
            xv = xv * scale
            y = (xv * c_ref[...] + pltpu.roll(xv * sa_ref[...], down, 1)
                 + pltpu.roll(xv * sb_ref[...], up, 1))
        o_ref[...] = y.astype(o_ref.dtype)

    tab = pl.BlockSpec((tr, LANES), lambda h, i: (i, 0))
    return pl.pallas_call(
        body, out_shape=jax.ShapeDtypeStruct((S, H * LANES), out_dtype), grid=(H, S // tr),
        in_specs=[pl.BlockSpec((tr, LANES), lambda h, i: (i, cb + h)), tab, tab, tab],
        out_specs=pl.BlockSpec((tr, LANES), lambda h, i: (i, h)), name=name,
        compiler_params=_cparams("parallel", "parallel"))(x, c, sa, sb)


def _ret_out(o, g):
    y = o * lax.rsqrt(jnp.mean(o * o, axis=-1, keepdims=True) + NORM_EPS)
    return y * jax.nn.silu(g)


def ret_out_fwd(name, o, gsrc, g_cb, out_dtype):
    S = o.shape[0]
    tr = _pick(S, 512, 8)
    W = RET_DV

    def body(o_ref, g_ref, y_ref):
        y_ref[...] = _ret_out(o_ref[...], g_ref[...].astype(F32)).astype(y_ref.dtype)

    blk = pl.BlockSpec((tr, W), lambda h, i: (i, h))
    return pl.pallas_call(
        body, out_shape=jax.ShapeDtypeStruct((S, RET_HEADS * W), out_dtype), grid=(RET_HEADS, S // tr),
        in_specs=[blk, pl.BlockSpec((tr, W), lambda h, i: (i, g_cb + h))], out_specs=blk, name=name,
        compiler_params=_cparams("parallel", "parallel"))(o, gsrc)


def ret_out_bwd(name, o, gsrc, g_cb, dy):
    S = o.shape[0]
    tr = _pick(S, 512, 8)
    W = RET_DV

    def body(o_ref, g_ref, dy_ref, do_ref, dg_ref):
        _, vjp = jax.vjp(_ret_out, o_ref[...], g_ref[...].astype(F32))
        do, dg = vjp(dy_ref[...].astype(F32))
        do_ref[...] = do.astype(do_ref.dtype)
        dg_ref[...] = dg.astype(dg_ref.dtype)

    blk = pl.BlockSpec((tr, W), lambda h, i: (i, h))
    return pl.pallas_call(
        body, out_shape=(jax.ShapeDtypeStruct((S, RET_HEADS * W), F32),
                         jax.ShapeDtypeStruct((S, RET_HEADS * W), BF16)),
        grid=(RET_HEADS, S // tr),
        in_specs=[blk, pl.BlockSpec((tr, W), lambda h, i: (i, g_cb + h)), blk], out_specs=(blk, blk),
        name=name, compiler_params=_cparams("parallel", "parallel"))(o, gsrc, dy)


def _merge(g0, g1, g2, a, b, c):
    return jax.nn.sigmoid(g0) * a + jax.nn.sigmoid(g1) * b + jax.nn.sigmoid(g2) * c


def merge_fwd(name, P, gates_cb, a, b, c, out_dtype):
    S, D = a.shape
    tr = _pick(S, 128, 8)

    def body(g0, g1, g2, a_ref, b_ref, c_ref, o_ref):
        o_ref[...] = _merge(g0[...], g1[...], g2[...], a_ref[...], b_ref[...], c_ref[...]).astype(o_ref.dtype)

    row = pl.BlockSpec((tr, D), lambda i: (i, 0))
    gs = [pl.BlockSpec((tr, D), lambda i, k=k: (i, gates_cb + k)) for k in range(3)]
    return pl.pallas_call(
        body, out_shape=jax.ShapeDtypeStruct((S, D), out_dtype), grid=(S // tr,),
        in_specs=gs + [row, row, row], out_specs=row, name=name,
        compiler_params=_cparams("parallel"))(P, P, P, a, b, c)


def merge_bwd(name, P, gates_cb, a, b, c, dm):
    S, D = a.shape
    tr = _pick(S, 128, 8)

    def body(g0, g1, g2, a_ref, b_ref, c_ref, dm_ref, dg_ref, da_ref, db_ref, dc_ref):
        _, vjp = jax.vjp(_merge, g0[...], g1[...], g2[...], a_ref[...], b_ref[...], c_ref[...])
        d0, d1, d2, da, db, dc = vjp(dm_ref[...].astype(F32))
        dg_ref[:, 0:D] = d0.astype(dg_ref.dtype)
        dg_ref[:, D:2 * D] = d1.astype(dg_ref.dtype)
        dg_ref[:, 2 * D:3 * D] = d2.astype(dg_ref.dtype)
        da_ref[...] = da.astype(da_ref.dtype)
        db_ref[...] = db.astype(db_ref.dtype)
        dc_ref[...] = dc.astype(dc_ref.dtype)

    row = pl.BlockSpec((tr, D), lambda i: (i, 0))
    gs = [pl.BlockSpec((tr, D), lambda i, k=k: (i, gates_cb + k)) for k in range(3)]
    bf = jax.ShapeDtypeStruct((S, D), BF16)
    return pl.pallas_call(
        body, out_shape=(jax.ShapeDtypeStruct((S, 3 * D), BF16), bf, bf, bf), grid=(S // tr,),
        in_specs=gs + [row, row, row, row],
        out_specs=(pl.BlockSpec((tr, 3 * D), lambda i: (i, 0)), row, row, row), name=name,
        compiler_params=_cparams("parallel"))(P, P, P, a, b, c, dm)


HALO = 8


CONV_CHUNK = 32


def _shifted_back(u_ref, uh_ref, ext_ref, s1_ref, s2_ref, tr):
    ext_ref[0:HALO, :] = jnp.where(pl.program_id(1) > 0, uh_ref[...], 0.0)
    ext_ref[HALO:HALO + tr, :] = u_ref[...]
    s1_ref[...] = ext_ref[HALO - 1:HALO - 1 + tr, :]
    s2_ref[...] = ext_ref[HALO - 2:HALO - 2 + tr, :]


def _conv3(cw, cb, u, u1, u2):
    return cb + ((cw[0:1, :] * u2 + cw[1:2, :] * u1) + cw[2:3, :] * u)


def _chunks(tr, fn):
    def step(c, carry):
        return fn(pl.ds(pl.multiple_of(c * CONV_CHUNK, CONV_CHUNK), CONV_CHUNK), carry)
    return step


def _ffn_specs(S, f, tr, l):
    nb = tr // HALO
    row = pl.BlockSpec((None, tr, f), lambda j, i: (j, i, 0))
    prev = pl.BlockSpec((None, HALO, f), lambda j, i: (j, jnp.maximum(i * nb - 1, 0), 0))
    nxt = pl.BlockSpec((None, HALO, f), lambda j, i: (j, jnp.minimum((i + 1) * nb, S // HALO - 1), 0))
    cw = pl.BlockSpec((None, None, 3, f), lambda j, i: (l, j, 0, 0))
    cb = pl.BlockSpec((None, None, 1, f), lambda j, i: (l, j, 0, 0))
    return row, prev, nxt, cw, cb


def ffn_act_fwd(name, u, gt, cw, cb, l, out_dtype):
    _, S, f = u.shape
    tr = _pick(S, 512, 8)
    row, prev, _, cws, cbs = _ffn_specs(S, f, tr, l)

    def body(u_ref, uh_ref, gt_ref, cw_ref, cb_ref, o_ref, ext_ref, s1_ref, s2_ref):
        _shifted_back(u_ref, uh_ref, ext_ref, s1_ref, s2_ref, tr)
        cwv, cbv = cw_ref[...], cb_ref[...]

        def chunk(rows, carry):
            uc = _conv3(cwv, cbv, u_ref[rows, :], s1_ref[rows, :], s2_ref[rows, :])
            o_ref[rows, :] = (jax.nn.gelu(uc) * gt_ref[rows, :]).astype(o_ref.dtype)
            return carry

        lax.fori_loop(0, tr // CONV_CHUNK, _chunks(tr, chunk), 0)

    return pl.pallas_call(
        body, out_shape=jax.ShapeDtypeStruct((N_DEV, S, f), out_dtype), grid=(N_DEV, S // tr),
        in_specs=[row, prev, row, cws, cbs], out_specs=row,
        scratch_shapes=[pltpu.VMEM((tr + HALO, f), F32), pltpu.VMEM((tr, f), F32), pltpu.VMEM((tr, f), F32)],
        name=name, compiler_params=_cparams("parallel", "parallel"))(u, u, gt, cw, cb)


def ffn_act_bwd_point(name, u, gt, cw, cb, l, dact):
    _, S, f = u.shape
    tr = _pick(S, 512, 8)
    row, prev, _, cws, cbs = _ffn_specs(S, f, tr, l)

    def body(u_ref, uh_ref, gt_ref, cw_ref, cb_ref, da_ref, g_ref, dgt_ref, ext_ref, s1_ref, s2_ref):
        _shifted_back(u_ref, uh_ref, ext_ref, s1_ref, s2_ref, tr)
        cwv, cbv = cw_ref[...], cb_ref[...]

        def chunk(rows, carry):
            uc = _conv3(cwv, cbv, u_ref[rows, :], s1_ref[rows, :], s2_ref[rows, :])
            _, vjp = jax.vjp(lambda c, t: jax.nn.gelu(c) * t, uc, gt_ref[rows, :])
            g, dgt = vjp(da_ref[rows, :].astype(F32))
            g_ref[rows, :] = g
            dgt_ref[rows, :] = dgt.astype(dgt_ref.dtype)
            return carry

        lax.fori_loop(0, tr // CONV_CHUNK, _chunks(tr, chunk), 0)

    return pl.pallas_call(
        body, out_shape=(jax.ShapeDtypeStruct((N_DEV, S, f), F32), jax.ShapeDtypeStruct((N_DEV, S, f), BF16)),
        grid=(N_DEV, S // tr), in_specs=[row, prev, row, cws, cbs, row], out_specs=(row, row),
        scratch_shapes=[pltpu.VMEM((tr + HALO, f), F32), pltpu.VMEM((tr, f), F32), pltpu.VMEM((tr, f), F32)],
        name=name, compiler_params=_cparams("parallel", "parallel"))(u, u, gt, cw, cb, dact)


def ffn_act_bwd_conv(name, u, g, cw, l):
    _, S, f = u.shape
    tr = _pick(S, 512, 8)
    nt = S // tr
    row, prev, nxt, cws, _ = _ffn_specs(S, f, tr, l)

    def body(u_ref, uh_ref, g_ref, gn_ref, cw_ref, du_ref, dcw_ref, dcb_ref, ext_ref, s1_ref, s2_ref, n1_ref, n2_ref):
        i = pl.program_id(1)
        _shifted_back(u_ref, uh_ref, ext_ref, s1_ref, s2_ref, tr)
        ext_ref[0:tr, :] = g_ref[...]
        ext_ref[tr:tr + HALO, :] = jnp.where(i < nt - 1, gn_ref[...], 0.0)
        n1_ref[...] = ext_ref[1:1 + tr, :]
        n2_ref[...] = ext_ref[2:2 + tr, :]
        cw = cw_ref[...]

        def chunk(rows, carry):
            d0, d1, d2, db = carry
            g = g_ref[rows, :]
            du_ref[rows, :] = (cw[2:3, :] * g + cw[1:2, :] * n1_ref[rows, :]
                               + cw[0:1, :] * n2_ref[rows, :]).astype(du_ref.dtype)
            return (d0 + jnp.sum(g * s2_ref[rows, :], axis=0, keepdims=True),
                    d1 + jnp.sum(g * s1_ref[rows, :], axis=0, keepdims=True),
                    d2 + jnp.sum(g * u_ref[rows, :], axis=0, keepdims=True),
                    db + jnp.sum(g, axis=0, keepdims=True))

        z = jnp.zeros((1, f), F32)
        d0, d1, d2, db = lax.fori_loop(0, tr // CONV_CHUNK, _chunks(tr, chunk), (z, z, z, z))

        @pl.when(i == 0)
        def _():
            dcw_ref[...] = jnp.zeros_like(dcw_ref)
            dcb_ref[...] = jnp.zeros_like(dcb_ref)

        dcw_ref[0:1, :] += d0
        dcw_ref[1:2, :] += d1
        dcw_ref[2:3, :] += d2
        dcb_ref[...] += db

    tile = pltpu.VMEM((tr, f), F32)
    return pl.pallas_call(
        body, out_shape=(jax.ShapeDtypeStruct((N_DEV, S, f), BF16), jax.ShapeDtypeStruct((N_DEV, 3, f), F32),
                         jax.ShapeDtypeStruct((N_DEV, 1, f), F32)),
        grid=(N_DEV, nt), in_specs=[row, prev, row, nxt, cws],
        out_specs=(row, pl.BlockSpec((None, 3, f), lambda j, i: (j, 0, 0)),
                   pl.BlockSpec((None, 1, f), lambda j, i: (j, 0, 0))),
        scratch_shapes=[pltpu.VMEM((tr + HALO, f), F32), tile, tile, tile, tile], name=name,
        compiler_params=_cparams("parallel", "arbitrary"))(u, u, g, g, cw)


def loss_head(name, x, g, tgt):
    S, D = x.shape
    tr = _pick(S, 256, 8)

    def body(x_ref, g_ref, t_ref, l_ref, dx_ref, dg_ref):
        tg = t_ref[...]

        def f(xv, gv):
            err = jnp.square(_rms(xv, gv) - tg)
            return 0.5 * jnp.sum(jnp.mean(err, axis=-1))

        val, vjp = jax.vjp(f, x_ref[...], g_ref[...])
        dx, dg = vjp(jnp.ones((), F32))
        dx_ref[...] = dx

        @pl.when(pl.program_id(0) == 0)
        def _():
            l_ref[...] = jnp.zeros_like(l_ref)
            dg_ref[...] = jnp.zeros_like(dg_ref)

        l_ref[...] += val
        dg_ref[...] += dg

    row = pl.BlockSpec((tr, D), lambda i: (i, 0))
    vec = pl.BlockSpec((1, D), lambda i: (0, 0))
    lt = pl.BlockSpec((8, LANES), lambda i: (0, 0))
    return pl.pallas_call(
        body, out_shape=(jax.ShapeDtypeStruct((8, LANES), F32), jax.ShapeDtypeStruct((S, D), F32),
                         jax.ShapeDtypeStruct((1, D), F32)),
        grid=(S // tr,), in_specs=[row, vec, row], out_specs=(lt, row, vec), name=name,
        compiler_params=_cparams("arbitrary"))(x, g, tgt)


def _tri(n, fn):
    r = lax.broadcasted_iota(jnp.int32, (n, n), 0)
    c = lax.broadcasted_iota(jnp.int32, (n, n), 1)
    return jnp.where(fn(r, c), 1.0, 0.0).astype(F32)


def _log_sigmoid(z):
    return jnp.minimum(z, 0.0) - jnp.log1p(jnp.exp(-jnp.abs(z)))


def fox_gate_fwd(name, ft, b):
    H, R, _ = ft.shape

    def body(f_ref, b_ref, o_ref):
        ls = _log_sigmoid(f_ref[...] + b_ref[...])
        cum = jnp.dot(ls, _tri(LANES, lambda r, c: r <= c), precision=HIGHEST, preferred_element_type=F32)
        tot = jnp.broadcast_to(cum[:, LANES - 1:LANES], (R, LANES))
        off = jnp.dot(_tri(R, lambda r, c: r > c), tot, precision=HIGHEST, preferred_element_type=F32)
        o_ref[...] = -(cum + off)

    blk = pl.BlockSpec((None, R, LANES), lambda h: (h, 0, 0))
    return pl.pallas_call(
        body, out_shape=jax.ShapeDtypeStruct((H, R, LANES), F32), grid=(H,),
        in_specs=[blk, pl.BlockSpec((None, 1, LANES), lambda h: (h, 0, 0))], out_specs=blk, name=name,
        compiler_params=_cparams("parallel"))(ft, b)


def fox_gate_bwd(name, ft, b, dkb):
    H, R, _ = ft.shape

    def body(f_ref, b_ref, d_ref, df_ref, db_ref):
        z = f_ref[...] + b_ref[...]
        d = d_ref[...]
        rev = jnp.dot(d, _tri(LANES, lambda r, c: r >= c), precision=HIGHEST, preferred_element_type=F32)
        tot = jnp.broadcast_to(rev[:, 0:1], (R, LANES))
        off = jnp.dot(_tri(R, lambda r, c: r < c), tot, precision=HIGHEST, preferred_element_type=F32)
        dls = -(rev + off)
        dz = dls * jax.nn.sigmoid(-z)
        df_ref[...] = dz
        s = jnp.sum(jnp.sum(dz, axis=1, keepdims=True), axis=0, keepdims=True)
        db_ref[...] = jnp.broadcast_to(s, (1, LANES))

    blk = pl.BlockSpec((None, R, LANES), lambda h: (h, 0, 0))
    vec = pl.BlockSpec((None, 1, LANES), lambda h: (h, 0, 0))
    return pl.pallas_call(
        body, out_shape=(jax.ShapeDtypeStruct((H, R, LANES), F32), jax.ShapeDtypeStruct((H, 1, LANES), F32)),
        grid=(H,), in_specs=[blk, vec, blk], out_specs=(blk, vec), name=name,
        compiler_params=_cparams("parallel"))(ft, b, dkb)


def _ret_log_gamma(h):
    lg = [float(np.log(np.float32(1.0) - np.float32(2.0) ** np.float32(-5.0 - i))) for i in range(RET_HEADS)]
    out = jnp.float32(lg[RET_HEADS - 1])
    for i in range(RET_HEADS - 2, -1, -1):
        out = jnp.where(h == i, jnp.float32(lg[i]), out)
    return out


def _visible(mode, B):
    r = lax.broadcasted_iota(jnp.int32, (B, B), 0)
    c = lax.broadcasted_iota(jnp.int32, (B, B), 1)
    if mode == "fox":
        return c <= r
    return (c // CHUNK) <= (r // CHUNK)


def _visible_t(mode, B):
    k = lax.broadcasted_iota(jnp.int32, (B, B), 0)
    q = lax.broadcasted_iota(jnp.int32, (B, B), 1)
    if mode == "fox":
        return k <= q
    return (k // CHUNK) <= (q // CHUNK)


def _decay(lg, B, blocks_apart):
    r = lax.broadcasted_iota(jnp.int32, (B, B), 0)
    c = lax.broadcasted_iota(jnp.int32, (B, B), 1)
    dist = jnp.abs(r - c + blocks_apart * B).astype(F32)
    return jnp.exp(lg * dist)


def _attn_block(S):
    return 512 if S >= 2048 else 128


def attn_fwd(name, mode, q1, q1_cb, k1, k1_cb, v, v_cb, H, dv, scale, q2=None, q2_cb=0, k2=None, kbias=None):
    S = q1.shape[0]
    B = _attn_block(S)
    nq = S // B
    softmax = mode != "ret"
    two = mode == "mla"
    has_bias = mode == "fox"

    def body(*refs):
        it = iter(refs)
        q1_ref, k1_ref, v_ref = next(it), next(it), next(it)
        q2_ref = next(it) if two else None
        k2_ref = next(it) if two else None
        kb_ref = next(it) if has_bias else None
        o_ref = next(it)
        lse_ref = next(it) if softmax else None
        kbuf, vT = next(it), next(it)
        acc = next(it)
        m_ref = next(it) if softmax else None
        l_ref = next(it) if softmax else None
        s_all = next(it) if softmax else None
        kcol = next(it) if has_bias else None
        h = pl.program_id(0)
        i = pl.program_id(1)

        @pl.when(i == 0)
        def _():
            kbuf[:, 0:LANES] = k1_ref[...].astype(BF16)
            vT[...] = v_ref[...].astype(F32).T.astype(BF16)
            if two:
                kbuf[:, LANES:2 * LANES] = k2_ref[...].astype(BF16)
            if has_bias:
                for g in range(nq):
                    kcol[g * B:(g + 1) * B, :] = jnp.broadcast_to(kb_ref[g], (LANES, B)).T[:, 0:1]

        qb = q1_ref[...].astype(BF16)
        if two:
            qb = jnp.concatenate([qb, q2_ref[...].astype(BF16)], axis=1)
        lg = _ret_log_gamma(h) if mode == "ret" else None
        acc[...] = jnp.zeros_like(acc)
        if softmax:
            m_ref[...] = jnp.full_like(m_ref, NEG_BIG)
            l_ref[...] = jnp.zeros_like(l_ref)

        def scores(g, diag):
            rows = slice(g * B, (g + 1) * B)
            s = lax.dot_general(kbuf[rows, :], qb, NT_DIMS, preferred_element_type=F32)
            if softmax:
                s = s * scale
                if has_bias:
                    s = s + kcol[rows, :]
                if diag:
                    s = jnp.where(_visible_t(mode, B), s, NEG_BIG)
                s_all[rows, :] = s
                m_ref[...] = jnp.maximum(m_ref[...], jnp.max(s, axis=0, keepdims=True))
            else:
                if diag:
                    p = jnp.where(_visible_t(mode, B), s * _decay(lg, B, 0), 0.0)
                else:
                    p = s * _decay(lg, B, g - i)
                acc[...] += jnp.dot(vT[:, rows], p.astype(BF16), preferred_element_type=F32)

        def weighted(g):
            rows = slice(g * B, (g + 1) * B)
            p = jnp.exp(s_all[rows, :] - m_ref[...])
            l_ref[...] += jnp.sum(p, axis=0, keepdims=True)
            acc[...] += jnp.dot(vT[:, rows], p.astype(BF16), preferred_element_type=F32)

        for g in range(nq):
            pl.when(g < i)(functools.partial(scores, g, False))
            pl.when(g == i)(functools.partial(scores, g, True))
        if softmax:
            for g in range(nq):
                pl.when(g <= i)(functools.partial(weighted, g))
            o_ref[...] = (acc[...] / l_ref[...]).T
            lse_ref[...] = jnp.broadcast_to(m_ref[...] + jnp.log(l_ref[...]), (LANES, B)).T
        else:
            o_ref[...] = acc[...].T

    in_specs = [pl.BlockSpec((B, LANES), lambda h, i: (i, q1_cb + h)),
                pl.BlockSpec((S, LANES), lambda h, i: (0, k1_cb + h)),
                pl.BlockSpec((S, dv), lambda h, i: (0, v_cb + h))]
    args = [q1, k1, v]
    if two:
        in_specs += [pl.BlockSpec((B, LANES), lambda h, i: (i, q2_cb + h)),
                     pl.BlockSpec((S, LANES), lambda h, i: (0, 0))]
        args += [q2, k2]
    if has_bias:
        in_specs.append(pl.BlockSpec((None, nq, 1, B), lambda h, i: (h, 0, 0, 0)))
        args.append(kbias)
    out_shape = [jax.ShapeDtypeStruct((S, H * dv), F32)]
    out_specs = [pl.BlockSpec((B, dv), lambda h, i: (i, h))]
    if softmax:
        out_shape.append(jax.ShapeDtypeStruct((S, H * LANES), F32))
        out_specs.append(pl.BlockSpec((B, LANES), lambda h, i: (i, h)))
    kw = 2 * LANES if two else LANES
    scratch = [pltpu.VMEM((S, kw), BF16), pltpu.VMEM((dv, S), BF16), pltpu.VMEM((dv, B), F32)]
    if softmax:
        scratch += [pltpu.VMEM((1, B), F32), pltpu.VMEM((1, B), F32), pltpu.VMEM((S, B), F32)]
    if has_bias:
        scratch.append(pltpu.VMEM((S, 1), F32))
    res = pl.pallas_call(body, out_shape=tuple(out_shape), grid=(H, nq), in_specs=in_specs,
                         out_specs=tuple(out_specs), scratch_shapes=scratch, name=name,
                         compiler_params=_cparams("parallel", "arbitrary"))(*args)
    return res if softmax else (res[0], None)


def attn_bwd(name, mode, q1, q1_cb, k1, k1_cb, v, v_cb, H, dv, scale, do, o=None, lse=None,
             q2=None, q2_cb=0, k2=None, kbias=None):
    S = q1.shape[0]
    B = _attn_block(S)
    nb = S // B
    softmax = mode != "ret"
    two = mode == "mla"
    has_bias = mode == "fox"

    def body(*refs):
        it = iter(refs)
        q1_ref, k1_ref, v_ref, do_ref = next(it), next(it), next(it), next(it)
        o_ref = next(it) if softmax else None
        lse_ref = next(it) if softmax else None
        q2_ref = next(it) if two else None
        k2_ref = next(it) if two else None
        kb_ref = next(it) if has_bias else None
        dq1_ref, dk1_ref, dv_ref = next(it), next(it), next(it)
        dq2_ref = next(it) if two else None
        dk2_ref = next(it) if two else None
        dkb_ref = next(it) if has_bias else None
        drow_ref = next(it) if has_bias else None
        qbuf, dobuf = next(it), next(it)
        qT, doT = next(it), next(it)
        delta = next(it) if softmax else None
        dk_acc, dv_acc = next(it), next(it)
        dkb_acc = next(it) if has_bias else None
        h = pl.program_id(0)
        j = pl.program_id(1)

        @pl.when(j == 0)
        def _():
            qbuf[:, 0:LANES] = q1_ref[...].astype(BF16)
            dobuf[...] = do_ref[...].astype(BF16)
            qT[0:LANES, :] = q1_ref[...].astype(F32).T.astype(BF16)
            doT[...] = do_ref[...].astype(F32).T.astype(BF16)
            dq1_ref[...] = jnp.zeros_like(dq1_ref)
            if has_bias:
                drow_ref[...] = jnp.zeros_like(drow_ref)
            if two:
                qbuf[:, LANES:2 * LANES] = q2_ref[...].astype(BF16)
                qT[LANES:2 * LANES, :] = q2_ref[...].astype(F32).T.astype(BF16)
                dq2_ref[...] = jnp.zeros_like(dq2_ref)
            if softmax:
                def drow(t, carry):
                    rows = pl.ds(pl.multiple_of(t * B, B), B)
                    d = jnp.sum(do_ref[rows, :].astype(F32) * o_ref[rows, :], axis=1, keepdims=True)
                    delta[rows, :] = jnp.broadcast_to(d, (B, LANES))
                    return carry
                lax.fori_loop(0, nb, drow, 0)

        kj = k1_ref[...].astype(BF16)
        if two:
            kj = jnp.concatenate([kj, k2_ref[...].astype(BF16)], axis=1)
        vj = v_ref[...].astype(BF16)
        kbj = kb_ref[...] if has_bias else None
        lg = _ret_log_gamma(h) if mode == "ret" else None
        dk_acc[...] = jnp.zeros_like(dk_acc)
        dv_acc[...] = jnp.zeros_like(dv_acc)
        if has_bias:
            dkb_acc[...] = jnp.zeros_like(dkb_acc)

        def step(i, diag):
            rows = slice(i * B, (i + 1) * B)
            qi = qbuf[rows, :]
            doi = dobuf[rows, :]
            s = lax.dot_general(qi, kj, NT_DIMS, preferred_element_type=F32)
            dp = lax.dot_general(doi, vj, NT_DIMS, preferred_element_type=F32)
            if softmax:
                s = s * scale
                if has_bias:
                    s = s + kbj
                if diag:
                    s = jnp.where(_visible(mode, B), s, NEG_BIG)
                p = jnp.exp(s - jnp.tile(lse_ref[rows, :], (1, B // LANES)))
                ds = p * (dp - jnp.tile(delta[rows, :], (1, B // LANES)))
                if has_bias:
                    dkb_acc[...] += jnp.sum(ds, axis=0, keepdims=True)
                    drow_ref[rows, :] += jnp.broadcast_to(jnp.sum(ds, axis=1, keepdims=True), (B, LANES))
                dsb = (ds * scale).astype(BF16)
            else:
                if diag:
                    dec = jnp.where(_visible(mode, B), _decay(lg, B, 0), 0.0)
                else:
                    dec = _decay(lg, B, i - j)
                p = s * dec
                dsb = (dp * dec).astype(BF16)
            dv_acc[...] += jnp.dot(doT[:, rows], p.astype(BF16), preferred_element_type=F32)
            dk_acc[...] += jnp.dot(qT[:, rows], dsb, preferred_element_type=F32)
            dq = jnp.dot(dsb, kj, preferred_element_type=F32)
            dq1_ref[rows, :] += dq[:, 0:LANES]
            if two:
                dq2_ref[rows, :] += dq[:, LANES:2 * LANES]

        for i in range(nb):
            pl.when(i == j)(functools.partial(step, i, True))
            pl.when(i > j)(functools.partial(step, i, False))
        dk1_ref[...] = dk_acc[0:LANES, :].T
        dv_ref[...] = dv_acc[...].T
        if two:
            dk2_ref[...] = dk_acc[LANES:2 * LANES, :].T
        if has_bias:
            dkb_ref[...] = dkb_acc[...]

    full = lambda w, cb: pl.BlockSpec((S, w), lambda h, j: (0, cb + h))
    blk = lambda w, cb: pl.BlockSpec((B, w), lambda h, j: (j, cb + h))
    in_specs = [full(LANES, q1_cb), blk(LANES, k1_cb), blk(dv, v_cb), full(dv, 0)]
    args = [q1, k1, v, do]
    if softmax:
        in_specs += [full(dv, 0), full(LANES, 0)]
        args += [o, lse]
    if two:
        in_specs += [full(LANES, q2_cb), pl.BlockSpec((B, LANES), lambda h, j: (j, 0))]
        args += [q2, k2]
    if has_bias:
        in_specs.append(pl.BlockSpec((None, None, 1, B), lambda h, j: (h, j, 0, 0)))
        args.append(kbias)
    names = ["dq1", "dk1", "dv"]
    out_shape = [jax.ShapeDtypeStruct((S, H * LANES), F32), jax.ShapeDtypeStruct((S, H * LANES), F32),
                 jax.ShapeDtypeStruct((S, H * dv), F32)]
    out_specs = [full(LANES, 0), blk(LANES, 0), blk(dv, 0)]
    if two:
        names += ["dq2", "dk2h"]
        out_shape += [jax.ShapeDtypeStruct((S, H * LANES), F32)] * 2
        out_specs += [full(LANES, 0), blk(LANES, 0)]
    if has_bias:
        names.append("dkb")
        out_shape.append(jax.ShapeDtypeStruct((H, nb, 1, B), F32))
        out_specs.append(pl.BlockSpec((None, None, 1, B), lambda h, j: (h, j, 0, 0)))
        names.append("drow")
        out_shape.append(jax.ShapeDtypeStruct((S, H * LANES), F32))
        out_specs.append(full(LANES, 0))
    kw = 2 * LANES if two else LANES
    scratch = [pltpu.VMEM((S, kw), BF16), pltpu.VMEM((S, dv), BF16),
               pltpu.VMEM((kw, S), BF16), pltpu.VMEM((dv, S), BF16)]
    if softmax:
        scratch.append(pltpu.VMEM((S, LANES), F32))
    scratch += [pltpu.VMEM((kw, B), F32), pltpu.VMEM((dv, B), F32)]
    if has_bias:
        scratch.append(pltpu.VMEM((1, B), F32))
    res = pl.pallas_call(body, out_shape=tuple(out_shape), grid=(H, nb), in_specs=in_specs,
                         out_specs=tuple(out_specs), scratch_shapes=scratch, name=name,
                         compiler_params=_cparams("parallel", "arbitrary"))(*args)
    return dict(zip(names, res))


def head_sum(name, x, H, out_dtype):
    S = x.shape[0]
    tr = _pick(S, 512, 8)

    def body(x_ref, o_ref):
        acc = x_ref[:, 0:LANES]
        for h in range(1, H):
            acc = acc + x_ref[:, h * LANES:(h + 1) * LANES]
        o_ref[...] = acc.astype(o_ref.dtype)

    return pl.pallas_call(
        body, out_shape=jax.ShapeDtypeStruct((S, LANES), out_dtype), grid=(S // tr,),
        in_specs=[pl.BlockSpec((tr, H * LANES), lambda i: (i, 0))],
        out_specs=pl.BlockSpec((tr, LANES), lambda i: (i, 0)), name=name,
        compiler_params=_cparams("parallel"))(x)


def _mesh_pos():
    return lax.axis_index("x"), lax.axis_index("y"), lax.axis_index("c")


def _peer(pos, k):
    x, y, c = pos
    px = 1 - x if k & 4 else x
    py = 1 - y if k & 2 else y
    pc = 1 - c if k & 1 else c
    return (px, py, pc), 4 * px + 2 * py + pc


def exchange(name, tensors):
    nt = len(tensors)
    flat_in, counts = [], []
    out_shape = []
    for mode, srcs in tensors:
        counts.append(len(srcs))
        flat_in += list(srcs)
        rc = srcs[0].shape[-2:]
        out_shape.append(jax.ShapeDtypeStruct((len(srcs), N_DEV) + tuple(rc), srcs[0].dtype))
    n_in = len(flat_in)

    def body(*refs):
        ins = refs[:n_in]
        outs = refs[n_in:n_in + nt]
        send_sems, recv_sems, local_sems = refs[n_in + nt:]
        pos = _mesh_pos()
        me = 4 * pos[0] + 2 * pos[1] + pos[2]
        srcs_of, base = [], 0
        for t in range(nt):
            srcs_of.append(ins[base:base + counts[t]])
            base += counts[t]

        def src_view(t, l, slot):
            ref = srcs_of[t][l]
            return ref if tensors[t][0] == "gather" else ref.at[slot]

        def all_layers(t, slot):
            return outs[t].at[pl.ds(0, counts[t]), slot]

        for t in range(nt):
            for l in range(counts[t]):
                pltpu.make_async_copy(src_view(t, l, me), outs[t].at[l, me], local_sems.at[t]).start()
        for t in range(nt):
            for k in range(1, N_DEV):
                peer, pid = _peer(pos, k)
                for l in range(counts[t]):
                    pltpu.make_async_remote_copy(
                        src_ref=src_view(t, l, pid), dst_ref=outs[t].at[l, me],
                        send_sem=send_sems.at[t, k - 1], recv_sem=recv_sems.at[t, k - 1],
                        device_id=peer, device_id_type=pl.DeviceIdType.MESH).start()
        for t in range(nt):
            for k in range(1, N_DEV):
                peer, pid = _peer(pos, k)
                pltpu.make_async_remote_copy(
                    src_ref=all_layers(t, pid), dst_ref=all_layers(t, pid),
                    send_sem=send_sems.at[t, k - 1], recv_sem=recv_sems.at[t, k - 1],
                    device_id=peer, device_id_type=pl.DeviceIdType.MESH).wait()
        for t in range(nt):
            pltpu.make_async_copy(all_layers(t, me), all_layers(t, me), local_sems.at[t]).wait()

    any_spec = pl.BlockSpec(memory_space=pl.ANY)
    return pl.pallas_call(
        body, out_shape=tuple(out_shape), in_specs=[any_spec] * n_in, out_specs=tuple([any_spec] * nt),
        scratch_shapes=[pltpu.SemaphoreType.DMA((nt, N_DEV - 1)), pltpu.SemaphoreType.DMA((nt, N_DEV - 1)),
                        pltpu.SemaphoreType.DMA((nt,))],
        name=name)(*flat_in)


HBM_SPEC = pl.BlockSpec(memory_space=pltpu.HBM)
SEM_SPEC = pl.BlockSpec(memory_space=pltpu.SEMAPHORE)
DATAFLOW = pltpu.SideEffectType.DATAFLOW_SIDE_EFFECTING


def _hbm(a):
    return pltpu.with_memory_space_constraint(a, pltpu.HBM)


def landing_zones(mode, srcs):
    pos = _mesh_pos()
    me = 4 * pos[0] + 2 * pos[1] + pos[2]
    lands = []
    for s in srcs:
        R, C = s.shape[-2:]
        own = s[None] if mode == "gather" else lax.dynamic_slice(s, (me, 0, 0), (1, R, C))
        lands.append(lax.dynamic_update_slice(lax.empty((N_DEV, R, C), s.dtype), own, (me, 0, 0)))
    return lands


def exchange_start(name, mode, srcs, lands, after=None):
    n = len(srcs)
    extra = [] if after is None else [after]

    def body(*refs):
        src_refs, land_refs = refs[:n], refs[n:2 * n]
        send_sems, recv_sems = refs[2 * n + len(extra)], refs[2 * n + len(extra) + 1]
        token = refs[-1]
        pos = _mesh_pos()
        me = 4 * pos[0] + 2 * pos[1] + pos[2]
        for t in range(n):
            for k in range(1, N_DEV):
                peer, pid = _peer(pos, k)
                src = src_refs[t] if mode == "gather" else src_refs[t].at[pid]
                pltpu.make_async_remote_copy(
                    src_ref=src, dst_ref=land_refs[t].at[me], send_sem=send_sems.at[t], recv_sem=recv_sems.at[t],
                    device_id=peer, device_id_type=pl.DeviceIdType.MESH).start()
        token[...] = jnp.zeros_like(token)

    thru = [pltpu.HBM(a.shape, a.dtype) for a in list(srcs) + list(lands)]
    out_shape = (pltpu.SemaphoreType.DMA((n,)), pltpu.SemaphoreType.DMA((n,)), *thru,
                 jax.ShapeDtypeStruct((8, LANES), F32))
    res = pl.pallas_call(
        body, out_shape=out_shape, in_specs=[HBM_SPEC] * (2 * n) + [pl.BlockSpec(memory_space=pl.ANY)] * len(extra),
        out_specs=(SEM_SPEC, SEM_SPEC, *([HBM_SPEC] * (2 * n)), pl.BlockSpec(memory_space=pltpu.VMEM)),
        input_output_aliases={i: 2 + i for i in range(2 * n)}, name=name,
        compiler_params=pltpu.CompilerParams(has_side_effects=DATAFLOW))(
            *[_hbm(a) for a in list(srcs) + list(lands)], *extra)
    return res[0], res[1], list(res[2:2 + n]), list(res[2 + n:2 + 2 * n]), res[-1]


def exchange_wait(name, send_sems, recv_sems, srcs, lands, after):
    n = len(srcs)

    def body(*refs):
        land_refs = refs[n:2 * n]
        s_sems, r_sems = refs[2 * n], refs[2 * n + 1]
        pos = _mesh_pos()
        for t in range(n):
            seven = land_refs[t].at[pl.ds(0, N_DEV - 1)]
            cp = pltpu.make_async_remote_copy(
                src_ref=seven, dst_ref=seven, send_sem=s_sems.at[t], recv_sem=r_sems.at[t],
                device_id=pos, device_id_type=pl.DeviceIdType.MESH)
            cp.wait_send()
            cp.wait_recv()

    arrs = list(srcs) + list(lands)
    afters = list(after) if isinstance(after, (list, tuple)) else [after]
    res = pl.pallas_call(
        body, out_shape=tuple(pltpu.HBM(a.shape, a.dtype) for a in arrs),
        in_specs=[HBM_SPEC] * (2 * n) + [SEM_SPEC, SEM_SPEC] + [pl.BlockSpec(memory_space=pl.ANY)] * len(afters),
        out_specs=tuple([HBM_SPEC] * (2 * n)), input_output_aliases={i: i for i in range(2 * n)}, name=name,
        compiler_params=pltpu.CompilerParams(has_side_effects=DATAFLOW))(*arrs, send_sems, recv_sems, *afters)
    return list(res[n:])


def reduce_parts(name, parts):
    n, R, C = parts.shape
    tr = _pick(R, max(8, (1 << 20) // (C * 4) // 8 * 8), 8)

    def body(p_ref, o_ref):
        acc = p_ref[0].astype(F32)
        for s in range(1, n):
            acc = acc + p_ref[s].astype(F32)
        o_ref[...] = acc

    return pl.pallas_call(
        body, out_shape=jax.ShapeDtypeStruct((R, C), F32), grid=(R // tr,),
        in_specs=[pl.BlockSpec((n, tr, C), lambda i: (0, i, 0))],
        out_specs=pl.BlockSpec((tr, C), lambda i: (i, 0)), name=name,
        compiler_params=_cparams("parallel"))(parts)


def adamw(name, w, m, v, parts, first=0, prev=None):
    L, R, C = w.shape
    nl = len(parts)
    n = parts[0].shape[0]
    tr = _pick(R, max(8, (1 << 19) // (C * 4) // 8 * 8), 8)
    n_prev = 0 if prev is None else 4

    def body(*refs):
        w_ref, m_ref, v_ref = refs[:3]
        p_refs = refs[3:3 + nl]
        g_ref, d_ref, nm_ref, nv_ref = refs[3 + nl + n_prev:]

        def update(p_ref):
            g = p_ref[0].astype(F32)
            for s in range(1, n):
                g = g + p_ref[s].astype(F32)
            wv = w_ref[...]
            mn = ADAM_B1 * m_ref[...] + (1.0 - ADAM_B1) * g
            vn = ADAM_B2 * v_ref[...] + (1.0 - ADAM_B2) * jnp.square(g)
            m_hat = mn / (1.0 - ADAM_B1 ** ADAM_STEP)
            v_hat = vn / (1.0 - ADAM_B2 ** ADAM_STEP)
            g_ref[...] = g
            d_ref[...] = -ADAM_LR * (m_hat / (jnp.sqrt(v_hat) + ADAM_EPS) + ADAM_WD * wv)
            nm_ref[...] = mn
            nv_ref[...] = vn

        for k in range(nl):
            pl.when(pl.program_id(0) == k)(functools.partial(update, p_refs[k]))

    blk = pl.BlockSpec((None, tr, C), lambda l, i: (first + l, i, 0))
    pspecs = [pl.BlockSpec((n, tr, C), lambda l, i, k=k: (0, jnp.where(l == k, i, 0), 0)) for k in range(nl)]
    sh = jax.ShapeDtypeStruct((L, R, C), F32)
    prev_args = [] if prev is None else list(prev)
    return pl.pallas_call(
        body, out_shape=(sh, sh, sh, sh), grid=(nl, R // tr),
        in_specs=[blk, blk, blk] + pspecs + [pl.BlockSpec(memory_space=pl.ANY)] * n_prev,
        out_specs=(blk, blk, blk, blk), input_output_aliases={3 + nl + q: q for q in range(n_prev)}, name=name,
        compiler_params=_cparams("arbitrary", "arbitrary"))(w, m, v, *parts, *prev_args)


def _cols_from_blocks(g):
    n, R, c = g.shape
    return g.transpose(1, 0, 2).reshape(R, n * c)


def _cols_to_blocks(w):
    R, C = w.shape
    return w.reshape(R, N_DEV, C // N_DEV).transpose(1, 0, 2)


def _uq_permute(w):
    lead = w.shape[:-1]
    w4 = w.reshape(lead + (MLA_HEADS, MLA_NOPE + MLA_ROPE))
    nope = w4[..., :MLA_NOPE].reshape(lead + (MLA_HEADS * MLA_NOPE,))
    rope = jnp.pad(w4[..., MLA_NOPE:], [(0, 0)] * (w4.ndim - 1) + [(0, LANES - MLA_ROPE)])
    return jnp.concatenate([nope, rope.reshape(lead + (MLA_HEADS * LANES,))], axis=-1)


def _uq_unpermute(w):
    lead = w.shape[:-1]
    n = MLA_HEADS * MLA_NOPE
    nope = w[..., :n].reshape(lead + (MLA_HEADS, MLA_NOPE))
    rope = w[..., n:].reshape(lead + (MLA_HEADS, LANES))[..., :MLA_ROPE]
    return jnp.concatenate([nope, rope], axis=-1).reshape(lead + (MLA_HEADS * (MLA_NOPE + MLA_ROPE),))


def _ukv_permute(w):
    lead = w.shape[:-1]
    w4 = w.reshape(lead + (MLA_HEADS, 2, MLA_NOPE))
    return jnp.swapaxes(w4, -3, -2).reshape(lead + (2 * MLA_HEADS * MLA_NOPE,))


def _ukv_unpermute(w):
    lead = w.shape[:-1]
    w4 = w.reshape(lead + (2, MLA_HEADS, MLA_NOPE))
    return jnp.swapaxes(w4, -3, -2).reshape(lead + (2 * MLA_HEADS * MLA_NOPE,))


SMALL = ["norm1_g", "mla_q_norm_g", "mla_kv_norm_g", "fox_b_f", "norm2_g", "ffn_conv_b", "final_norm_g"]
SMALL_TILE = 8 * LANES


def _pack_small(d):
    flat = jnp.concatenate([d[n].reshape(-1).astype(F32) for n in SMALL])
    pad = -flat.shape[0] % SMALL_TILE
    return jnp.pad(flat, (0, pad)).reshape(-1, LANES)


def _unpack_small(packed, like):
    flat = packed.reshape(-1)
    out, o = {}, 0
    for n in SMALL:
        sz = int(np.prod(like[n].shape))
        out[n] = flat[o:o + sz].reshape(like[n].shape)
        o += sz
    return out


WEIGHTS = ["norm1_g", "w_in", "mla_q_norm_g", "mla_kv_norm_g", "mla_w_uq", "mla_w_ukv", "fox_b_f", "w_br_fox",
           "w_br_mla", "w_br_ret", "w_out", "norm2_g", "ffn_w_up", "ffn_w_gate", "ffn_conv_w", "ffn_conv_b",
           "ffn_w_down", "final_norm_g"]
EARLY = ["w_in", "mla_w_uq", "mla_w_ukv"]
LATE = ["w_br_fox", "w_br_mla", "w_br_ret", "w_out", "ffn_w_up", "ffn_w_gate", "ffn_conv_w", "ffn_w_down"]
FFN = ["ffn_w_up", "ffn_w_gate", "ffn_conv_w", "ffn_w_down"]
TRANSPOSED = ("ffn_w_up", "ffn_w_gate")
BIG = EARLY + LATE
X_EARLY = ["w_in_mix", "mla_w_uq", "mla_w_ukv"]
X_LATE = ["w_in_gates"] + LATE
X_MID = ["w_in_gates", "w_out", "w_br_fox", "w_br_mla", "w_br_ret"]
X_REST = ["mla_w_uq", "mla_w_ukv", "w_in_mix"]


def kernel(x, norm1_g, w_in, mla_q_norm_g, mla_kv_norm_g, mla_w_uq, mla_w_ukv, fox_b_f, w_br_fox, w_br_mla, w_br_ret, w_out, norm2_g, ffn_w_up, ffn_w_gate, ffn_conv_w, ffn_conv_b, ffn_w_down, final_norm_g, loss_target, m_norm1_g, m_w_in, m_mla_q_norm_g, m_mla_kv_norm_g, m_mla_w_uq, m_mla_w_ukv, m_fox_b_f, m_w_br_fox, m_w_br_mla, m_w_br_ret, m_w_out, m_norm2_g, m_ffn_w_up, m_ffn_w_gate, m_ffn_conv_w, m_ffn_conv_b, m_ffn_w_down, m_final_norm_g, v_norm1_g, v_w_in, v_mla_q_norm_g, v_mla_kv_norm_g, v_mla_w_uq, v_mla_w_ukv, v_fox_b_f, v_w_br_fox, v_w_br_mla, v_w_br_ret, v_w_out, v_norm2_g, v_ffn_w_up, v_ffn_w_gate, v_ffn_conv_w, v_ffn_conv_b, v_ffn_w_down, v_final_norm_g):
    env = dict(locals())
    W = {n: env[n] for n in WEIGHTS}
    Mo = {n: env["m_" + n] for n in WEIGHTS}
    Vo = {n: env["v_" + n] for n in WEIGHTS}
    S, D = x.shape[1], x.shape[2]
    L = w_in.shape[0]
    lay = InLayout(D)
    NP = lay.total
    f = ffn_w_up.shape[-1]
    xs = x.reshape(S, D)
    tgt = loss_target.reshape(S, D)

    local = {n: W[n].astype(BF16) for n in BIG if n != "w_in"}
    local["w_in_gates"], local["w_in_mix"] = lay.permute(W["w_in"].astype(BF16))
    pending = {}
    token = None
    for l in range(L):
        for grp, names in (("a", X_EARLY), ("b", X_LATE)):
            srcs = [local[n][l] for n in names]
            *flight, token = exchange_start(f"gather_start_{l}{grp}", "gather", srcs, landing_zones("gather", srcs),
                                            token)
            pending[l, grp] = flight
    gather_token = token
    cbias_all = ffn_conv_b.reshape(L, 1, N_DEV, 1, f)

    def early_weights(l, after):
        g = dict(zip(X_EARLY, exchange_wait(f"gather_wait_{l}a", *pending[l, "a"], after)))
        return dict(Win=g["w_in_mix"].reshape(1, D, NP), Wuq=_uq_permute(_cols_from_blocks(g["mla_w_uq"])),
                    Wukv=_ukv_permute(_cols_from_blocks(g["mla_w_ukv"])))

    def late_weights(l, after):
        g = dict(zip(X_LATE, exchange_wait(f"gather_wait_{l}b", *pending[l, "b"], after)))
        return dict(
            Wgates=g["w_in_gates"].reshape(1, D, 3 * D),
            Wout=g["w_out"].reshape(1, D, D), Wbf=_cols_from_blocks(g["w_br_fox"]),
            Wbm=_cols_from_blocks(g["w_br_mla"]), Wbr=_cols_from_blocks(g["w_br_ret"]), Wup=g["ffn_w_up"][None],
            Wgate=g["ffn_w_gate"][None], Wdown=g["ffn_w_down"][None], Wconv=g["ffn_conv_w"].astype(F32)[None],
            cbias=cbias_all[l])

    tab64 = rope_tables(S, MLA_ROPE)
    tab128 = rope_tables(S, RET_DK)
    fox_scale = FOX_DH ** -0.5
    mla_scale = (MLA_NOPE + MLA_ROPE) ** -0.5
    ret_kscale = RET_DK ** -0.5
    R = S // LANES
    AB = _attn_block(S)
    NOPE_W = MLA_HEADS * MLA_NOPE

    def vec(a):
        return a.reshape(1, -1)

    saved = []
    xc = xs
    for l in range(L):
        Wl = early_weights(l, gather_token if l == 0 else xc)
        Win, Wuq, Wukv = Wl["Win"], Wl["Wuq"], Wl["Wukv"]
        s = {"x": xc, "W": Wl}
        h1 = rms_fwd("norm1", xc, 0, D, vec(norm1_g[l]), BF16)
        P = mm_nn("in_proj", h1, Win, F32, b_lead=0)
        s.update(h1=h1, P=P)
        ff_off = lay.off["ff"]
        ft = P[:, ff_off:ff_off + FOX_HEADS].T.reshape(FOX_HEADS, R, LANES)
        bfl = jnp.broadcast_to(fox_b_f[l].reshape(FOX_HEADS, 1, 1), (FOX_HEADS, 1, LANES))
        kbias = fox_gate_fwd("fox_gate", ft, bfl).reshape(FOX_HEADS, S // AB, 1, AB)
        o_fox, lse_fox = attn_fwd("fox_attn", "fox", P, lay.cb("fq", LANES), P, lay.cb("fk", LANES),
                                  P, lay.cb("fv", LANES), FOX_HEADS, FOX_DH, fox_scale, kbias=kbias)
        s.update(ft=ft, bfl=bfl, kbias=kbias, o_fox=o_fox, lse_fox=lse_fox)
        cqn = rms_fwd("mla_q_norm", P, lay.cb("mq", MLA_Q_LORA), MLA_Q_LORA, vec(mla_q_norm_g[l]), BF16)
        qall = mm_nn("mla_uq", cqn, Wuq, F32)
        ckvn = rms_fwd("mla_kv_norm", P, lay.cb("mkv", MLA_KV_LORA), MLA_KV_LORA, vec(mla_kv_norm_g[l]), BF16)
        kvall = mm_nn("mla_ukv", ckvn, Wukv, F32)
        qrope = rope_apply("mla_q_rope", qall, NOPE_W // LANES, MLA_HEADS, tab64, 1.0, F32)
        krope = rope_apply("mla_k_rope", P, lay.cb("mkr", LANES), 1, tab64, 1.0, F32)
        o_mla, lse_mla = attn_fwd("mla_attn", "mla", qall, 0, kvall, 0, kvall, NOPE_W // MLA_V, MLA_HEADS, MLA_V,
                                  mla_scale, q2=qrope, q2_cb=0, k2=krope)
        s.update(cqn=cqn, qall=qall, ckvn=ckvn, kvall=kvall, qrope=qrope, krope=krope, o_mla=o_mla,
                 lse_mla=lse_mla)
        rq = rope_apply("ret_q_rope", P, lay.cb("rq", LANES), RET_HEADS, tab128, 1.0, F32)
        rk = rope_apply("ret_k_rope", P, lay.cb("rk", LANES), RET_HEADS, tab128, ret_kscale, F32)
        o_ret, _ = attn_fwd("ret_attn", "ret", rq, 0, rk, 0, P, lay.cb("rv", RET_DV), RET_HEADS, RET_DV, 1.0)
        c_ret = ret_out_fwd("ret_out", o_ret, P, lay.cb("rg", RET_DV), BF16)
        s.update(rq=rq, rk=rk, o_ret=o_ret, c_ret=c_ret)
        Wl.update(late_weights(l, (o_fox, o_mla, c_ret)))
        Wout, Wbf, Wbm, Wbr = Wl["Wout"], Wl["Wbf"], Wl["Wbm"], Wl["Wbr"]
        Wup, Wgate, Wdown, Wconv, cbias = (Wl[k] for k in ("Wup", "Wgate", "Wdown", "Wconv", "cbias"))
        A = mm_nn("br_fox", o_fox, Wbf, F32)
        Bm = mm_nn("br_mla", o_mla, Wbm, F32)
        C = mm_nn("br_ret", c_ret, Wbr, F32)
        Pg = mm_nn("gate_proj", h1, Wl["Wgates"], F32, b_lead=0)
        s["Pg"] = Pg
        merged = merge_fwd("merge", Pg, 0, A, Bm, C, BF16)
        x2 = mm_nn("out_proj", merged, Wout, F32, b_lead=0, res=xc)
        s.update(A=A, Bm=Bm, C=C, merged=merged, x2=x2)
        h2 = rms_fwd("norm2", x2, 0, D, vec(norm2_g[l]), BF16)
        u = ffn_up("ffn_up", h2, Wup, 0, F32)
        gt = ffn_up("ffn_gate", h2, Wgate, 0, F32)
        act = ffn_act_fwd("ffn_act", u, gt, Wconv, cbias, 0, BF16)
        xc = ffn_down("ffn_down", act, Wdown, 0, x2, F32)
        s.update(h2=h2, u=u, gt=gt, act=act)
        saved.append(s)

    loss_tile, dx, dgf = loss_head("loss_head", xc, vec(final_norm_g), tgt)
    loss = lax.psum(loss_tile[0, 0], ("x", "y", "c"))

    gbig = {n: [None] * L for n in BIG + ["w_in_gates", "w_in_mix"]}
    gsmall = {n: [None] * L for n in SMALL if n != "final_norm_g"}
    scattering = {}
    scatter_token = None

    def start_scatter(name, names, l):
        srcs = [gbig[n][l] for n in names]
        *flight, tok = exchange_start(name, "scatter", srcs, landing_zones("scatter", srcs))
        return flight, tok

    for l in reversed(range(L)):
        s = saved[l]
        P = s["P"]
        Wl = s["W"]
        Win, Wout, Wuq, Wukv, Wbf, Wbm, Wbr = (Wl[k] for k in ("Win", "Wout", "Wuq", "Wukv", "Wbf", "Wbm", "Wbr"))
        Wup, Wgate, Wdown, Wconv, cbias = (Wl[k] for k in ("Wup", "Wgate", "Wdown", "Wconv", "cbias"))
        dxb = (dx if scatter_token is None else dx + scatter_token[0, 0]).astype(BF16)
        dact = ffn_down_bwd_act("ffn_down_da", dxb, Wdown, 0, BF16)
        gbig["ffn_w_down"][l] = ffn_down_bwd_w("ffn_down_dw", s["act"], dxb, BF16)
        g, dgt = ffn_act_bwd_point("ffn_act_bwd", s["u"], s["gt"], Wconv, cbias, 0, dact)
        du, dcw, dcb = ffn_act_bwd_conv("ffn_conv_bwd", s["u"], g, Wconv, 0)
        gbig["ffn_conv_w"][l] = dcw.astype(BF16)
        gsmall["ffn_conv_b"][l] = dcb.reshape(-1)
        gbig["ffn_w_up"][l] = ffn_down_bwd_w("ffn_up_dw", du, s["h2"], BF16)
        gbig["ffn_w_gate"][l] = ffn_down_bwd_w("ffn_gate_dw", dgt, s["h2"], BF16)
        dh2 = ffn_up_bwd_h("ffn_up_dh", du, Wup, 0, None, F32)
        dh2 = ffn_up_bwd_h("ffn_gate_dh", dgt, Wgate, 0, dh2, BF16)
        dx2, dg2 = rms_bwd("norm2_bwd", s["x2"], 0, D, vec(norm2_g[l]), dh2, F32, res=dx)
        gsmall["norm2_g"][l] = dg2.reshape(-1)
        scattering[l, "ffn"], scatter_token = start_scatter(f"scatter_start_{l}ffn", FFN, l)
        dx2b = (dx2 + scatter_token[0, 0]).astype(BF16)
        dmerged = mm_nt("out_proj_dm", dx2b, Wout, BF16, b_lead=0)
        gbig["w_out"][l] = mm_tn("out_proj_dw", s["merged"], dx2b, BF16).reshape(N_DEV, D // N_DEV, D)
        dgates, dA, dB, dC = merge_bwd("merge_bwd", s["Pg"], 0, s["A"], s["Bm"], s["C"], dmerged)
        gbig["w_br_fox"][l] = _cols_to_blocks(mm_tn("br_fox_dw", s["o_fox"], dA, BF16))
        gbig["w_br_mla"][l] = _cols_to_blocks(mm_tn("br_mla_dw", s["o_mla"], dB, BF16))
        gbig["w_br_ret"][l] = _cols_to_blocks(mm_tn("br_ret_dw", s["c_ret"], dC, BF16))
        gbig["w_in_gates"][l] = mm_tn("gate_proj_dw", s["h1"], dgates, BF16).reshape(N_DEV, D // N_DEV, 3 * D)
        scattering[l, "mid"], scatter_token = start_scatter(f"scatter_start_{l}mid", X_MID, l)
        dh1_gates = mm_nt("gate_proj_dh", dgates, Wl["Wgates"], F32, b_lead=0)
        do_fox = mm_nt("br_fox_do", dA + scatter_token[0, 0].astype(BF16), Wbf, F32)
        do_mla = mm_nt("br_mla_do", dB, Wbm, F32)
        dc_ret = mm_nt("br_ret_do", dC, Wbr, BF16)
        do_ret, drg = ret_out_bwd("ret_out_bwd", s["o_ret"], P, lay.cb("rg", RET_DV), dc_ret)
        rb = attn_bwd("ret_attn_bwd", "ret", s["rq"], 0, s["rk"], 0, P, lay.cb("rv", RET_DV), RET_HEADS, RET_DV,
                      1.0, do_ret)
        drq = rope_apply("ret_q_rope_bwd", rb["dq1"], 0, RET_HEADS, tab128, 1.0, BF16, transpose=True)
        drk = rope_apply("ret_k_rope_bwd", rb["dk1"], 0, RET_HEADS, tab128, ret_kscale, BF16, transpose=True)
        drv = rb["dv"].astype(BF16)
        mb = attn_bwd("mla_attn_bwd", "mla", s["qall"], 0, s["kvall"], 0, s["kvall"], NOPE_W // MLA_V, MLA_HEADS,
                      MLA_V, mla_scale, do_mla, o=s["o_mla"], lse=s["lse_mla"], q2=s["qrope"], q2_cb=0,
                      k2=s["krope"])
        dqrope = rope_apply("mla_q_rope_bwd", mb["dq2"], 0, MLA_HEADS, tab64, 1.0, BF16, transpose=True)
        dkr_sum = head_sum("mla_k_rope_sum", mb["dk2h"], MLA_HEADS, F32)
        dmkr = rope_apply("mla_k_rope_bwd", dkr_sum, 0, 1, tab64, 1.0, BF16, transpose=True)
        dqall = jnp.concatenate([mb["dq1"].astype(BF16), dqrope], axis=1)
        dkvall = jnp.concatenate([mb["dk1"].astype(BF16), mb["dv"].astype(BF16)], axis=1)
        dcqn = mm_nt("mla_uq_dx", dqall, Wuq, F32)
        dckvn = mm_nt("mla_ukv_dx", dkvall, Wukv, F32)
        guq = _uq_unpermute(mm_tn("mla_uq_dw", s["cqn"], dqall, BF16))
        gukv = _ukv_unpermute(mm_tn("mla_ukv_dw", s["ckvn"], dkvall, BF16))
        gbig["mla_w_uq"][l] = _cols_to_blocks(guq)
        gbig["mla_w_ukv"][l] = _cols_to_blocks(gukv)
        dmq, dgq = rms_bwd("mla_q_norm_bwd", P, lay.cb("mq", MLA_Q_LORA), MLA_Q_LORA, vec(mla_q_norm_g[l]),
                           dcqn, BF16)
        dmkv, dgkv = rms_bwd("mla_kv_norm_bwd", P, lay.cb("mkv", MLA_KV_LORA), MLA_KV_LORA,
                             vec(mla_kv_norm_g[l]), dckvn, BF16)
        gsmall["mla_q_norm_g"][l] = dgq.reshape(-1)
        gsmall["mla_kv_norm_g"][l] = dgkv.reshape(-1)
        fb = attn_bwd("fox_attn_bwd", "fox", P, lay.cb("fq", LANES), P, lay.cb("fk", LANES), P,
                      lay.cb("fv", LANES), FOX_HEADS, FOX_DH, fox_scale, do_fox, o=s["o_fox"], lse=s["lse_fox"],
                      kbias=s["kbias"])
        drow = fb["drow"].reshape(S, FOX_HEADS, LANES)[:, :, 0].T.reshape(FOX_HEADS, R, LANES)
        dft, dbf = fox_gate_bwd("fox_gate_bwd", s["ft"], s["bfl"], fb["dkb"].reshape(FOX_HEADS, R, LANES) - drow)
        gsmall["fox_b_f"][l] = dbf[:, 0, 0]
        dff = jnp.pad(dft.reshape(FOX_HEADS, S).T, ((0, 0), (0, LANES - FOX_HEADS))).astype(BF16)
        segs = dict(rv=drv, rg=drg, mq=dmq, rq=drq, rk=drk, mkv=dmkv, fq=fb["dq1"].astype(BF16),
                    fk=fb["dk1"].astype(BF16), fv=fb["dv"].astype(BF16), mkr=dmkr, ff=dff)
        dP = jnp.concatenate([segs[n] for n in lay.order], axis=1)
        gbig["w_in_mix"][l] = mm_tn("in_proj_dw", s["h1"], dP, BF16).reshape(N_DEV, D // N_DEV, NP)
        scattering[l, "rest"], scatter_token = start_scatter(f"scatter_start_{l}rest", X_REST, l)
        dh1 = mm_nt("in_proj_dh", dP, Win, BF16, b_lead=0, res=dh1_gates)
        dx, dg1 = rms_bwd("norm1_bwd", s["x"], 0, D, vec(norm1_g[l]) + scatter_token[0:1, 0:1], dh1, F32, res=dx2)
        gsmall["norm1_g"][l] = dg1.reshape(-1)

    small_like = {n: W[n] for n in SMALL}
    small_part = {n: jnp.stack(gsmall[n]) for n in gsmall}
    small_part["final_norm_g"] = dgf.reshape(-1)
    small_recv = exchange("gather_small_grads", [("gather", [_pack_small(small_part)])])[0]
    ps = adamw("adamw_small", _pack_small(small_like)[None], _pack_small({n: Mo[n] for n in SMALL})[None],
               _pack_small({n: Vo[n] for n in SMALL})[None], [small_recv[0]])

    def received(l, after):
        r = {}
        for grp, names in (("ffn", FFN), ("mid", X_MID), ("rest", X_REST)):
            r.update(zip(names, exchange_wait(f"scatter_wait_{l}{grp}", *scattering[l, grp], after)))
        r["w_in"] = lay.unpermute(reduce_parts("w_in_gates_grad_sum", r["w_in_gates"]),
                                  reduce_parts("w_in_mix_grad_sum", r["w_in_mix"]))[None]
        return r

    def oriented(n, a):
        return jnp.swapaxes(a, 1, 2) if n in TRANSPOSED else a

    out = {}
    if L > 1:
        recv = [received(l, dx) for l in range(1, L)]
        for n in BIG:
            out[n] = adamw("adamw_" + n, oriented(n, W[n]), oriented(n, Mo[n]), oriented(n, Vo[n]),
                           [r[n] for r in recv], first=1)
    recv0 = received(0, out[BIG[-1]][0] if L > 1 else dx)
    for n in BIG:
        res = adamw("adamw0_" + n, oriented(n, W[n]), oriented(n, Mo[n]), oriented(n, Vo[n]), [recv0[n]],
                    first=0, prev=out.get(n))
        out[n] = tuple(oriented(n, a) for a in res)
    small_out = [_unpack_small(a[0], small_like) for a in ps]
    for n in SMALL:
        out[n] = tuple(so[n] for so in small_out)

    grads = [out[n][0] for n in WEIGHTS]
    deltas = [out[n][1] for n in WEIGHTS]
    new_m = [out[n][2] for n in WEIGHTS]
    new_v = [out[n][3] for n in WEIGHTS]
    return (loss, dx.reshape(1, S, D), *grads, *deltas, *new_m, *new_v)
```

```python
import functools
import math

import numpy as np
import jax
import jax.numpy as jnp
from jax import lax
from jax.experimental import pallas as pl
from jax.experimental.pallas import tpu as pltpu

F32 = jnp.float32
BF16 = jnp.bfloat16

CHUNK = 64
NORM_EPS = 1e-6
ROPE_THETA = 10000.0
FOX_HEADS, FOX_DH = 6, 128
FOX_W = FOX_HEADS * FOX_DH
MLA_HEADS, MLA_NOPE, MLA_ROPE, MLA_V = 6, 128, 64, 128
MLA_Q_LORA, MLA_KV_LORA = 512, 256
MLA_W = MLA_HEADS * MLA_V
RET_HEADS, RET_DK, RET_DV = 4, 128, 256
RET_QK_W, RET_V_W = RET_HEADS * RET_DK, RET_HEADS * RET_DV
ADAM_LR, ADAM_B1, ADAM_B2, ADAM_EPS, ADAM_WD, ADAM_STEP = 0.001, 0.9, 0.999, 1e-08, 0.01, 10

N_DEV = 8
LANES = 128
V7X_VMEM_LIMIT_BYTES = 52 * 1024 * 1024
NEG_BIG = -1e30
HIGHEST = lax.Precision.HIGHEST

NT_DIMS = (((1,), (1,)), ((), ()))
TN_DIMS = (((0,), (0,)), ((), ()))
NN_DIMS = (((1,), (0,)), ((), ()))


def _pick(n, cap, mult=LANES):
    best = None
    for t in range(mult, min(n, cap) + 1, mult):
        if n % t == 0:
            best = t
    return n if best is None else best


def _cparams(*sem):
    return pltpu.CompilerParams(dimension_semantics=sem, vmem_limit_bytes=V7X_VMEM_LIMIT_BYTES)


class InLayout:
    def __init__(self, d_model):
        d = d_model
        self.d = d
        orig = dict(fq=(0, FOX_W), fk=(FOX_W, FOX_W), fv=(2 * FOX_W, FOX_W), ff=(3 * FOX_W, FOX_HEADS))
        o = 3 * FOX_W + FOX_HEADS
        for name, w in (("mq", MLA_Q_LORA), ("mkv", MLA_KV_LORA), ("mkr", MLA_ROPE), ("rq", RET_QK_W),
                        ("rk", RET_QK_W), ("rv", RET_V_W), ("rg", RET_V_W), ("gates", 3 * d)):
            orig[name] = (o, w)
            o += w
        self.orig = orig
        self.orig_width = o
        order = ["rv", "rg", "mq", "rq", "rk", "mkv", "fq", "fk", "fv", "mkr", "ff"]
        self.order = order
        self.off, self.width = {}, {}
        p = 0
        for name in order:
            w = orig[name][1]
            wp = -(-w // LANES) * LANES
            self.off[name], self.width[name] = p, wp
            p += wp
        self.total = p
        self.gates = 3 * d

    def cb(self, name, block):
        assert self.off[name] % block == 0, (name, block)
        return self.off[name] // block

    def permute(self, w):
        parts = []
        for name in self.order:
            o, n = self.orig[name]
            seg = w[..., o:o + n]
            pad = self.width[name] - n
            if pad:
                seg = jnp.pad(seg, [(0, 0)] * (w.ndim - 1) + [(0, pad)])
            parts.append(seg)
        o, n = self.orig["gates"]
        return w[..., o:o + n], jnp.concatenate(parts, axis=-1)

    def unpermute(self, gates, mix):
        names = sorted(self.order, key=lambda n: self.orig[n][0])
        return jnp.concatenate([mix[..., self.off[n]:self.off[n] + self.orig[n][1]] for n in names] + [gates],
                               axis=-1)


def _mm(name, a, b, out_shape, grid, a_spec, b_spec, o_spec, dims, acc_shape, res=None, nsub=0):
    nk = grid[-1]
    has_res = res is not None

    def body(*refs):
        if has_res:
            a_ref, b_ref, r_ref, o_ref = refs[:4]
        else:
            a_ref, b_ref, o_ref = refs[:3]
            r_ref = None
        if nsub:
            prod = lax.dot_general(a_ref[0].astype(BF16), b_ref[0].astype(BF16), dims, preferred_element_type=F32)
            for q in range(1, nsub):
                prod = prod + lax.dot_general(a_ref[q].astype(BF16), b_ref[q].astype(BF16), dims,
                                              preferred_element_type=F32)
        else:
            prod = lax.dot_general(a_ref[...].astype(BF16), b_ref[...].astype(BF16), dims,
                                   preferred_element_type=F32)
        if nk == 1:
            if has_res:
                prod = prod + r_ref[...].astype(F32)
            o_ref[...] = prod.astype(o_ref.dtype)
        else:
            acc_ref = refs[-1]
            k = pl.program_id(len(grid) - 1)

            @pl.when(k == 0)
            def _():
                acc_ref[...] = prod

            @pl.when(k > 0)
            def _():
                acc_ref[...] += prod

            @pl.when(k == nk - 1)
            def _():
                r = acc_ref[...]
                if has_res:
                    r = r + r_ref[...].astype(F32)
                o_ref[...] = r.astype(o_ref.dtype)

    in_specs = [a_spec, b_spec] + ([o_spec] if has_res else [])
    args = (a, b) + ((res,) if has_res else ())
    scratch = [pltpu.VMEM(acc_shape, F32)] if nk > 1 else []
    sem = ("parallel",) * (len(grid) - 1) + ("arbitrary",)
    return pl.pallas_call(body, out_shape=out_shape, grid=grid, in_specs=in_specs, out_specs=o_spec,
                          scratch_shapes=scratch, name=name, compiler_params=_cparams(*sem))(*args)


def mm_nn(name, a, b, out_dtype, b_lead=None, res=None):
    M, K = a.shape
    N = b.shape[-1]
    tm, tn, tk = _pick(M, 1024, 8), _pick(N, 1536), _pick(K, 2048)
    grid = (M // tm, N // tn, K // tk)
    a_spec = pl.BlockSpec((tm, tk), lambda i, j, k: (i, k))
    if b_lead is None:
        b_spec = pl.BlockSpec((tk, tn), lambda i, j, k: (k, j))
    else:
        b_spec = pl.BlockSpec((None, tk, tn), lambda i, j, k: (b_lead, k, j))
    o_spec = pl.BlockSpec((tm, tn), lambda i, j, k: (i, j))
    return _mm(name, a, b, jax.ShapeDtypeStruct((M, N), out_dtype), grid, a_spec, b_spec, o_spec,
               NN_DIMS, (tm, tn), res)


def mm_nt(name, a, b, out_dtype, b_lead=None, res=None):
    M, N = a.shape
    K = b.shape[-2]
    tm, tko, tk = _pick(M, 1024, 8), _pick(K, 1024), _pick(N, 3200)
    grid = (M // tm, K // tko, N // tk)
    a_spec = pl.BlockSpec((tm, tk), lambda i, j, k: (i, k))
    if b_lead is None:
        b_spec = pl.BlockSpec((tko, tk), lambda i, j, k: (j, k))
    else:
        b_spec = pl.BlockSpec((None, tko, tk), lambda i, j, k: (b_lead, j, k))
    o_spec = pl.BlockSpec((tm, tko), lambda i, j, k: (i, j))
    return _mm(name, a, b, jax.ShapeDtypeStruct((M, K), out_dtype), grid, a_spec, b_spec, o_spec,
               NT_DIMS, (tm, tko), res)


def mm_tn(name, a, b, out_dtype):
    M, K = a.shape
    N = b.shape[-1]
    cap = 4096 if (a.dtype == BF16 and b.dtype == BF16) else 2048
    tko, tn, tk = _pick(K, 1024), _pick(N, 1280), _pick(M, cap, 8)
    grid = (K // tko, N // tn, M // tk)
    a_spec = pl.BlockSpec((tk, tko), lambda i, j, k: (k, i))
    b_spec = pl.BlockSpec((tk, tn), lambda i, j, k: (k, j))
    o_spec = pl.BlockSpec((tko, tn), lambda i, j, k: (i, j))
    return _mm(name, a, b, jax.ShapeDtypeStruct((K, N), out_dtype), grid, a_spec, b_spec, o_spec,
               TN_DIMS, (tko, tn))


FFN_SUB = 4
def ffn_up(name, h, w, l, out_dtype):
    M, D = h.shape
    f = w.shape[-1]
    tm = _pick(M, 1024, 8)
    grid = (M // tm, N_DEV, 1)
    return _mm(name, h, w, jax.ShapeDtypeStruct((N_DEV, M, f), out_dtype), grid,
               pl.BlockSpec((tm, D), lambda i, j, k: (i, 0)),
               pl.BlockSpec((None, None, D, f), lambda i, j, k: (l, j, 0, 0)),
               pl.BlockSpec((None, tm, f), lambda i, j, k: (j, i, 0)), NN_DIMS, (tm, f))


def ffn_down(name, act, w, l, res, out_dtype):
    _, M, f = act.shape
    D = w.shape[-1]
    tm, tn = _pick(M, 1024, 8), _pick(D, 1024)
    grid = (M // tm, D // tn, N_DEV // FFN_SUB)
    return _mm(name, act, w, jax.ShapeDtypeStruct((M, D), out_dtype), grid,
               pl.BlockSpec((FFN_SUB, tm, f), lambda i, j, k: (k, i, 0)),
               pl.BlockSpec((None, FFN_SUB, f, tn), lambda i, j, k: (l, k, 0, j)),
               pl.BlockSpec((tm, tn), lambda i, j, k: (i, j)), NN_DIMS, (tm, tn), res, nsub=FFN_SUB)


def ffn_down_bwd_act(name, dy, w, l, out_dtype):
    M, D = dy.shape
    f = w.shape[-2]
    tm = _pick(M, 1024, 8)
    grid = (M // tm, N_DEV, 1)
    return _mm(name, dy, w, jax.ShapeDtypeStruct((N_DEV, M, f), out_dtype), grid,
               pl.BlockSpec((tm, D), lambda i, j, k: (i, 0)),
               pl.BlockSpec((None, None, f, D), lambda i, j, k: (l, j, 0, 0)),
               pl.BlockSpec((None, tm, f), lambda i, j, k: (j, i, 0)), NT_DIMS, (tm, f))


def ffn_down_bwd_w(name, act, dy, out_dtype):
    _, M, f = act.shape
    D = dy.shape[-1]
    tn, tk = _pick(D, 1024), _pick(M, 4096 if dy.dtype == BF16 else 2048, 8)
    grid = (N_DEV, D // tn, M // tk)
    return _mm(name, act, dy, jax.ShapeDtypeStruct((N_DEV, f, D), out_dtype), grid,
               pl.BlockSpec((None, tk, f), lambda j, n, k: (j, k, 0)),
               pl.BlockSpec((tk, tn), lambda j, n, k: (k, n)),
               pl.BlockSpec((None, f, tn), lambda j, n, k: (j, 0, n)), TN_DIMS, (f, tn))


def ffn_up_bwd_h(name, du, w, l, res, out_dtype):
    _, M, f = du.shape
    D = w.shape[-2]
    tm, tn = _pick(M, 1024, 8), _pick(D, 1024)
    grid = (M // tm, D // tn, N_DEV // FFN_SUB)
    return _mm(name, du, w, jax.ShapeDtypeStruct((M, D), out_dtype), grid,
               pl.BlockSpec((FFN_SUB, tm, f), lambda i, j, k: (k, i, 0)),
               pl.BlockSpec((None, FFN_SUB, tn, f), lambda i, j, k: (l, k, j, 0)),
               pl.BlockSpec((tm, tn), lambda i, j, k: (i, j)), NT_DIMS, (tm, tn), res, nsub=FFN_SUB)


def _rms(xf, g):
    return xf * lax.rsqrt(jnp.mean(xf * xf, axis=-1, keepdims=True) + NORM_EPS) * g


def rms_fwd(name, x, cb, W, g, out_dtype):
    S = x.shape[0]
    tr = _pick(S, 256, 8)

    def body(x_ref, g_ref, o_ref):
        o_ref[...] = _rms(x_ref[...].astype(F32), g_ref[...]).astype(o_ref.dtype)

    return pl.pallas_call(
        body, out_shape=jax.ShapeDtypeStruct((S, W), out_dtype), grid=(S // tr,),
        in_specs=[pl.BlockSpec((tr, W), lambda i: (i, cb)), pl.BlockSpec((1, W), lambda i: (0, 0))],
        out_specs=pl.BlockSpec((tr, W), lambda i: (i, 0)), name=name, compiler_params=_cparams("parallel"))(x, g)


def rms_bwd(name, x, cb, W, g, dy, out_dtype, res=None):
    S = x.shape[0]
    tr = _pick(S, 256, 8)
    has_res = res is not None

    def body(*refs):
        if has_res:
            x_ref, g_ref, dy_ref, r_ref, dx_ref, dg_ref = refs
        else:
            x_ref, g_ref, dy_ref, dx_ref, dg_ref = refs
        _, vjp = jax.vjp(_rms, x_ref[...].astype(F32), g_ref[...])
        dx, dg = vjp(dy_ref[...].astype(F32))
        if has_res:
            dx = dx + r_ref[...]
        dx_ref[...] = dx.astype(dx_ref.dtype)

        @pl.when(pl.program_id(0) == 0)
        def _():
            dg_ref[...] = jnp.zeros_like(dg_ref)

        dg_ref[...] += dg

    row = pl.BlockSpec((tr, W), lambda i: (i, 0))
    vec = pl.BlockSpec((1, W), lambda i: (0, 0))
    in_specs = [pl.BlockSpec((tr, W), lambda i: (i, cb)), vec, row] + ([row] if has_res else [])
    args = (x, g, dy) + ((res,) if has_res else ())
    return pl.pallas_call(
        body, out_shape=(jax.ShapeDtypeStruct((S, W), out_dtype), jax.ShapeDtypeStruct((1, W), F32)),
        grid=(S // tr,), in_specs=in_specs, out_specs=(row, vec), name=name,
        compiler_params=_cparams("arbitrary"))(*args)


def rope_tables(S, d):
    pos = jnp.arange(S, dtype=F32)
    inv_freq = ROPE_THETA ** (-jnp.arange(0, d, 2, dtype=F32) / d)
    ang = pos[:, None] * inv_freq[None, :]
    cos, sin = jnp.cos(ang), jnp.sin(ang)
    half = d // 2
    z = jnp.zeros((S, LANES - d), F32)
    zh = jnp.zeros((S, half), F32)
    c = jnp.concatenate([cos, cos, z], axis=1)
    sa = jnp.concatenate([-sin, zh, z], axis=1)
    sb = jnp.concatenate([zh, sin, z], axis=1)
    return c, sa, sb, half


def rope_apply(name, x, cb, H, tabs, scale, out_dtype, transpose=False):
    c, sa, sb, half = tabs
    S = x.shape[0]
    tr = _pick(S, 512, 8)
    up, down = LANES - half, half

    def body(x_ref, c_ref, sa_ref, sb_ref, o_ref):
        xv = x_ref[...].astype(F32)
        if not transpose:
            y = xv * c_ref[...] + pltpu.roll(xv, up, 1) * sa_ref[...] + pltpu.roll(xv, down, 1) * sb_ref[...]
            y = y * scale
        else:
            xv = xv * scale
            y = (xv * c_ref[...] + pltpu.roll(xv * sa_ref[...], down, 1)
                 + pltpu.roll(xv * sb_ref[...], up, 1))
        o_ref[...] = y.astype(o_ref.dtype)

    tab = pl.BlockSpec((tr, LANES), lambda h, i: (i, 0))
    return pl.pallas_call(
        body, out_shape=jax.ShapeDtypeStruct((S, H * LANES), out_dtype), grid=(H, S // tr),
        in_specs=[pl.BlockSpec((tr, LANES), lambda h, i: (i, cb + h)), tab, tab, tab],
        out_specs=pl.BlockSpec((tr, LANES), lambda h, i: (i, h)), name=name,
        compiler_params=_cparams("parallel", "parallel"))(x, c, sa, sb)


def _ret_out(o, g):
    y = o * lax.rsqrt(jnp.mean(o * o, axis=-1, keepdims=True) + NORM_EPS)
    return y * jax.nn.silu(g)


def ret_out_fwd(name, o, gsrc, g_cb, out_dtype):
    S = o.shape[0]
    tr = _pick(S, 512, 8)
    W = RET_DV

    def body(o_ref, g_ref, y_ref):
        y_ref[...] = _ret_out(o_ref[...], g_ref[...].astype(F32)).astype(y_ref.dtype)

    blk = pl.BlockSpec((tr, W), lambda h, i: (i, h))
    return pl.pallas_call(
        body, out_shape=jax.ShapeDtypeStruct((S, RET_HEADS * W), out_dtype), grid=(RET_HEADS, S // tr),
        in_specs=[blk, pl.BlockSpec((tr, W), lambda h, i: (i, g_cb + h))], out_specs=blk, name=name,
        compiler_params=_cparams("parallel", "parallel"))(o, gsrc)


def ret_out_bwd(name, o, gsrc, g_cb, dy):
    S = o.shape[0]
    tr = _pick(S, 512, 8)
    W = RET_DV

    def body(o_ref, g_ref, dy_ref, do_ref, dg_ref):
        _, vjp = jax.vjp(_ret_out, o_ref[...], g_ref[...].astype(F32))
        do, dg = vjp(dy_ref[...].astype(F32))
        do_ref[...] = do.astype(do_ref.dtype)
        dg_ref[...] = dg.astype(dg_ref.dtype)

    blk = pl.BlockSpec((tr, W), lambda h, i: (i, h))
    return pl.pallas_call(
        body, out_shape=(jax.ShapeDtypeStruct((S, RET_HEADS * W), F32),
                         jax.ShapeDtypeStruct((S, RET_HEADS * W), BF16)),
        grid=(RET_HEADS, S // tr),
        in_specs=[blk, pl.BlockSpec((tr, W), lambda h, i: (i, g_cb + h)), blk], out_specs=(blk, blk),
        name=name, compiler_params=_cparams("parallel", "parallel"))(o, gsrc, dy)


def _merge(g0, g1, g2, a, b, c):
    return jax.nn.sigmoid(g0) * a + jax.nn.sigmoid(g1) * b + jax.nn.sigmoid(g2) * c


def merge_fwd(name, P, gates_cb, a, b, c, out_dtype):
    S, D = a.shape
    tr = _pick(S, 128, 8)

    def body(g0, g1, g2, a_ref, b_ref, c_ref, o_ref):
        o_ref[...] = _merge(g0[...], g1[...], g2[...], a_ref[...], b_ref[...], c_ref[...]).astype(o_ref.dtype)

    row = pl.BlockSpec((tr, D), lambda i: (i, 0))
    gs = [pl.BlockSpec((tr, D), lambda i, k=k: (i, gates_cb + k)) for k in range(3)]
    return pl.pallas_call(
        body, out_shape=jax.ShapeDtypeStruct((S, D), out_dtype), grid=(S // tr,),
        in_specs=gs + [row, row, row], out_specs=row, name=name,
        compiler_params=_cparams("parallel"))(P, P, P, a, b, c)


def merge_bwd(name, P, gates_cb, a, b, c, dm):
    S, D = a.shape
    tr = _pick(S, 128, 8)

    def body(g0, g1, g2, a_ref, b_ref, c_ref, dm_ref, dg_ref, da_ref, db_ref, dc_ref):
        _, vjp = jax.vjp(_merge, g0[...], g1[...], g2[...], a_ref[...], b_ref[...], c_ref[...])
        d0, d1, d2, da, db, dc = vjp(dm_ref[...].astype(F32))
        dg_ref[:, 0:D] = d0.astype(dg_ref.dtype)
        dg_ref[:, D:2 * D] = d1.astype(dg_ref.dtype)
        dg_ref[:, 2 * D:3 * D] = d2.astype(dg_ref.dtype)
        da_ref[...] = da.astype(da_ref.dtype)
        db_ref[...] = db.astype(db_ref.dtype)
        dc_ref[...] = dc.astype(dc_ref.dtype)

    row = pl.BlockSpec((tr, D), lambda i: (i, 0))
    gs = [pl.BlockSpec((tr, D), lambda i, k=k: (i, gates_cb + k)) for k in range(3)]
    bf = jax.ShapeDtypeStruct((S, D), BF16)
    return pl.pallas_call(
        body, out_shape=(jax.ShapeDtypeStruct((S, 3 * D), BF16), bf, bf, bf), grid=(S // tr,),
        in_specs=gs + [row, row, row, row],
        out_specs=(pl.BlockSpec((tr, 3 * D), lambda i: (i, 0)), row, row, row), name=name,
        compiler_params=_cparams("parallel"))(P, P, P, a, b, c, dm)


HALO = 8


CONV_CHUNK = 32


def _shifted_back(u_ref, uh_ref, ext_ref, s1_ref, s2_ref, tr):
    ext_ref[0:HALO, :] = jnp.where(pl.program_id(1) > 0, uh_ref[...], 0.0)
    ext_ref[HALO:HALO + tr, :] = u_ref[...]
    s1_ref[...] = ext_ref[HALO - 1:HALO - 1 + tr, :]
    s2_ref[...] = ext_ref[HALO - 2:HALO - 2 + tr, :]


def _conv3(cw, cb, u, u1, u2):
    return cb + ((cw[0:1, :] * u2 + cw[1:2, :] * u1) + cw[2:3, :] * u)


def _chunks(tr, fn):
    def step(c, carry):
        return fn(pl.ds(pl.multiple_of(c * CONV_CHUNK, CONV_CHUNK), CONV_CHUNK), carry)
    return step


def _ffn_specs(S, f, tr, l):
    nb = tr // HALO
    row = pl.BlockSpec((None, tr, f), lambda j, i: (j, i, 0))
    prev = pl.BlockSpec((None, HALO, f), lambda j, i: (j, jnp.maximum(i * nb - 1, 0), 0))
    nxt = pl.BlockSpec((None, HALO, f), lambda j, i: (j, jnp.minimum((i + 1) * nb, S // HALO - 1), 0))
    cw = pl.BlockSpec((None, None, 3, f), lambda j, i: (l, j, 0, 0))
    cb = pl.BlockSpec((None, None, 1, f), lambda j, i: (l, j, 0, 0))
    return row, prev, nxt, cw, cb


def ffn_act_fwd(name, u, gt, cw, cb, l, out_dtype):
    _, S, f = u.shape
    tr = _pick(S, 512, 8)
    row, prev, _, cws, cbs = _ffn_specs(S, f, tr, l)

    def body(u_ref, uh_ref, gt_ref, cw_ref, cb_ref, o_ref, ext_ref, s1_ref, s2_ref):
        _shifted_back(u_ref, uh_ref, ext_ref, s1_ref, s2_ref, tr)
        cwv, cbv = cw_ref[...], cb_ref[...]

        def chunk(rows, carry):
            uc = _conv3(cwv, cbv, u_ref[rows, :], s1_ref[rows, :], s2_ref[rows, :])
            o_ref[rows, :] = (jax.nn.gelu(uc) * gt_ref[rows, :]).astype(o_ref.dtype)
            return carry

        lax.fori_loop(0, tr // CONV_CHUNK, _chunks(tr, chunk), 0)

    return pl.pallas_call(
        body, out_shape=jax.ShapeDtypeStruct((N_DEV, S, f), out_dtype), grid=(N_DEV, S // tr),
        in_specs=[row, prev, row, cws, cbs], out_specs=row,
        scratch_shapes=[pltpu.VMEM((tr + HALO, f), F32), pltpu.VMEM((tr, f), F32), pltpu.VMEM((tr, f), F32)],
        name=name, compiler_params=_cparams("parallel", "parallel"))(u, u, gt, cw, cb)


def ffn_act_bwd_point(name, u, gt, cw, cb, l, dact):
    _, S, f = u.shape
    tr = _pick(S, 512, 8)
    row, prev, _, cws, cbs = _ffn_specs(S, f, tr, l)

    def body(u_ref, uh_ref, gt_ref, cw_ref, cb_ref, da_ref, g_ref, dgt_ref, ext_ref, s1_ref, s2_ref):
        _shifted_back(u_ref, uh_ref, ext_ref, s1_ref, s2_ref, tr)
        cwv, cbv = cw_ref[...], cb_ref[...]

        def chunk(rows, carry):
            uc = _conv3(cwv, cbv, u_ref[rows, :], s1_ref[rows, :], s2_ref[rows, :])
            _, vjp = jax.vjp(lambda c, t: jax.nn.gelu(c) * t, uc, gt_ref[rows, :])
            g, dgt = vjp(da_ref[rows, :].astype(F32))
            g_ref[rows, :] = g
            dgt_ref[rows, :] = dgt.astype(dgt_ref.dtype)
            return carry

        lax.fori_loop(0, tr // CONV_CHUNK, _chunks(tr, chunk), 0)

    return pl.pallas_call(
        body, out_shape=(jax.ShapeDtypeStruct((N_DEV, S, f), F32), jax.ShapeDtypeStruct((N_DEV, S, f), BF16)),
        grid=(N_DEV, S // tr), in_specs=[row, prev, row, cws, cbs, row], out_specs=(row, row),
        scratch_shapes=[pltpu.VMEM((tr + HALO, f), F32), pltpu.VMEM((tr, f), F32), pltpu.VMEM((tr, f), F32)],
        name=name, compiler_params=_cparams("parallel", "parallel"))(u, u, gt, cw, cb, dact)


def ffn_act_bwd_conv(name, u, g, cw, l):
    _, S, f = u.shape
    tr = _pick(S, 512, 8)
    nt = S // tr
    row, prev, nxt, cws, _ = _ffn_specs(S, f, tr, l)

    def body(u_ref, uh_ref, g_ref, gn_ref, cw_ref, du_ref, dcw_ref, dcb_ref, ext_ref, s1_ref, s2_ref, n1_ref, n2_ref):
        i = pl.program_id(1)
        _shifted_back(u_ref, uh_ref, ext_ref, s1_ref, s2_ref, tr)
        ext_ref[0:tr, :] = g_ref[...]
        ext_ref[tr:tr + HALO, :] = jnp.where(i < nt - 1, gn_ref[...], 0.0)
        n1_ref[...] = ext_ref[1:1 + tr, :]
        n2_ref[...] = ext_ref[2:2 + tr, :]
        cw = cw_ref[...]

        def chunk(rows, carry):
            d0, d1, d2, db = carry
            g = g_ref[rows, :]
            du_ref[rows, :] = (cw[2:3, :] * g + cw[1:2, :] * n1_ref[rows, :]
                               + cw[0:1, :] * n2_ref[rows, :]).astype(du_ref.dtype)
            return (d0 + jnp.sum(g * s2_ref[rows, :], axis=0, keepdims=True),
                    d1 + jnp.sum(g * s1_ref[rows, :], axis=0, keepdims=True),
                    d2 + jnp.sum(g * u_ref[rows, :], axis=0, keepdims=True),
                    db + jnp.sum(g, axis=0, keepdims=True))

        z = jnp.zeros((1, f), F32)
        d0, d1, d2, db = lax.fori_loop(0, tr // CONV_CHUNK, _chunks(tr, chunk), (z, z, z, z))

        @pl.when(i == 0)
        def _():
            dcw_ref[...] = jnp.zeros_like(dcw_ref)
            dcb_ref[...] = jnp.zeros_like(dcb_ref)

        dcw_ref[0:1, :] += d0
        dcw_ref[1:2, :] += d1
        dcw_ref[2:3, :] += d2
        dcb_ref[...] += db

    tile = pltpu.VMEM((tr, f), F32)
    return pl.pallas_call(
        body, out_shape=(jax.ShapeDtypeStruct((N_DEV, S, f), BF16), jax.ShapeDtypeStruct((N_DEV, 3, f), F32),
                         jax.ShapeDtypeStruct((N_DEV, 1, f), F32)),
        grid=(N_DEV, nt), in_specs=[row, prev, row, nxt, cws],
        out_specs=(row, pl.BlockSpec((None, 3, f), lambda j, i: (j, 0, 0)),
                   pl.BlockSpec((None, 1, f), lambda j, i: (j, 0, 0))),
        scratch_shapes=[pltpu.VMEM((tr + HALO, f), F32), tile, tile, tile, tile], name=name,
        compiler_params=_cparams("parallel", "arbitrary"))(u, u, g, g, cw)


def loss_head(name, x, g, tgt):
    S, D = x.shape
    tr = _pick(S, 256, 8)

    def body(x_ref, g_ref, t_ref, l_ref, dx_ref, dg_ref):
        tg = t_ref[...]

        def f(xv, gv):
            err = jnp.square(_rms(xv, gv) - tg)
            return 0.5 * jnp.sum(jnp.mean(err, axis=-1))

        val, vjp = jax.vjp(f, x_ref[...], g_ref[...])
        dx, dg = vjp(jnp.ones((), F32))
        dx_ref[...] = dx

        @pl.when(pl.program_id(0) == 0)
        def _():
            l_ref[...] = jnp.zeros_like(l_ref)
            dg_ref[...] = jnp.zeros_like(dg_ref)

        l_ref[...] += val
        dg_ref[...] += dg

    row = pl.BlockSpec((tr, D), lambda i: (i, 0))
    vec = pl.BlockSpec((1, D), lambda i: (0, 0))
    lt = pl.BlockSpec((8, LANES), lambda i: (0, 0))
    return pl.pallas_call(
        body, out_shape=(jax.ShapeDtypeStruct((8, LANES), F32), jax.ShapeDtypeStruct((S, D), F32),
                         jax.ShapeDtypeStruct((1, D), F32)),
        grid=(S // tr,), in_specs=[row, vec, row], out_specs=(lt, row, vec), name=name,
        compiler_params=_cparams("arbitrary"))(x, g, tgt)


def _tri(n, fn):
    r = lax.broadcasted_iota(jnp.int32, (n, n), 0)
    c = lax.broadcasted_iota(jnp.int32, (n, n), 1)
    return jnp.where(fn(r, c), 1.0, 0.0).astype(F32)


def _log_sigmoid(z):
    return jnp.minimum(z, 0.0) - jnp.log1p(jnp.exp(-jnp.abs(z)))


def fox_gate_fwd(name, ft, b):
    H, R, _ = ft.shape

    def body(f_ref, b_ref, o_ref):
        ls = _log_sigmoid(f_ref[...] + b_ref[...])
        cum = jnp.dot(ls, _tri(LANES, lambda r, c: r <= c), precision=HIGHEST, preferred_element_type=F32)
        tot = jnp.broadcast_to(cum[:, LANES - 1:LANES], (R, LANES))
        off = jnp.dot(_tri(R, lambda r, c: r > c), tot, precision=HIGHEST, preferred_element_type=F32)
        o_ref[...] = -(cum + off)

    blk = pl.BlockSpec((None, R, LANES), lambda h: (h, 0, 0))
    return pl.pallas_call(
        body, out_shape=jax.ShapeDtypeStruct((H, R, LANES), F32), grid=(H,),
        in_specs=[blk, pl.BlockSpec((None, 1, LANES), lambda h: (h, 0, 0))], out_specs=blk, name=name,
        compiler_params=_cparams("parallel"))(ft, b)


def fox_gate_bwd(name, ft, b, dkb):
    H, R, _ = ft.shape

    def body(f_ref, b_ref, d_ref, df_ref, db_ref):
        z = f_ref[...] + b_ref[...]
        d = d_ref[...]
        rev = jnp.dot(d, _tri(LANES, lambda r, c: r >= c), precision=HIGHEST, preferred_element_type=F32)
        tot = jnp.broadcast_to(rev[:, 0:1], (R, LANES))
        off = jnp.dot(_tri(R, lambda r, c: r < c), tot, precision=HIGHEST, preferred_element_type=F32)
        dls = -(rev + off)
        dz = dls * jax.nn.sigmoid(-z)
        df_ref[...] = dz
        s = jnp.sum(jnp.sum(dz, axis=1, keepdims=True), axis=0, keepdims=True)
        db_ref[...] = jnp.broadcast_to(s, (1, LANES))

    blk = pl.BlockSpec((None, R, LANES), lambda h: (h, 0, 0))
    vec = pl.BlockSpec((None, 1, LANES), lambda h: (h, 0, 0))
    return pl.pallas_call(
        body, out_shape=(jax.ShapeDtypeStruct((H, R, LANES), F32), jax.ShapeDtypeStruct((H, 1, LANES), F32)),
        grid=(H,), in_specs=[blk, vec, blk], out_specs=(blk, vec), name=name,
        compiler_params=_cparams("parallel"))(ft, b, dkb)


def _ret_log_gamma(h):
    lg = [float(np.log(np.float32(1.0) - np.float32(2.0) ** np.float32(-5.0 - i))) for i in range(RET_HEADS)]
    out = jnp.float32(lg[RET_HEADS - 1])
    for i in range(RET_HEADS - 2, -1, -1):
        out = jnp.where(h == i, jnp.float32(lg[i]), out)
    return out


def _visible(mode, B):
    r = lax.broadcasted_iota(jnp.int32, (B, B), 0)
    c = lax.broadcasted_iota(jnp.int32, (B, B), 1)
    if mode == "fox":
        return c <= r
    return (c // CHUNK) <= (r // CHUNK)


def _visible_t(mode, B):
    k = lax.broadcasted_iota(jnp.int32, (B, B), 0)
    q = lax.broadcasted_iota(jnp.int32, (B, B), 1)
    if mode == "fox":
        return k <= q
    return (k // CHUNK) <= (q // CHUNK)


def _decay(lg, B, blocks_apart):
    r = lax.broadcasted_iota(jnp.int32, (B, B), 0)
    c = lax.broadcasted_iota(jnp.int32, (B, B), 1)
    dist = jnp.abs(r - c + blocks_apart * B).astype(F32)
    return jnp.exp(lg * dist)


def _attn_block(S):
    return 512 if S >= 2048 else 128


def attn_fwd(name, mode, q1, q1_cb, k1, k1_cb, v, v_cb, H, dv, scale, q2=None, q2_cb=0, k2=None, kbias=None):
    S = q1.shape[0]
    B = _attn_block(S)
    nq = S // B
    softmax = mode != "ret"
    two = mode == "mla"
    has_bias = mode == "fox"

    def body(*refs):
        it = iter(refs)
        q1_ref, k1_ref, v_ref = next(it), next(it), next(it)
        q2_ref = next(it) if two else None
        k2_ref = next(it) if two else None
        kb_ref = next(it) if has_bias else None
        o_ref = next(it)
        lse_ref = next(it) if softmax else None
        kbuf, vT = next(it), next(it)
        acc = next(it)
        m_ref = next(it) if softmax else None
        l_ref = next(it) if softmax else None
        s_all = next(it) if softmax else None
        kcol = next(it) if has_bias else None
        h = pl.program_id(0)
        i = pl.program_id(1)

        @pl.when(i == 0)
        def _():
            kbuf[:, 0:LANES] = k1_ref[...].astype(BF16)
            vT[...] = v_ref[...].astype(F32).T.astype(BF16)
            if two:
                kbuf[:, LANES:2 * LANES] = k2_ref[...].astype(BF16)
            if has_bias:
                for g in range(nq):
                    kcol[g * B:(g + 1) * B, :] = jnp.broadcast_to(kb_ref[g], (LANES, B)).T[:, 0:1]

        qb = q1_ref[...].astype(BF16)
        if two:
            qb = jnp.concatenate([qb, q2_ref[...].astype(BF16)], axis=1)
        lg = _ret_log_gamma(h) if mode == "ret" else None
        acc[...] = jnp.zeros_like(acc)
        if softmax:
            m_ref[...] = jnp.full_like(m_ref, NEG_BIG)
            l_ref[...] = jnp.zeros_like(l_ref)

        def scores(g, diag):
            rows = slice(g * B, (g + 1) * B)
            s = lax.dot_general(kbuf[rows, :], qb, NT_DIMS, preferred_element_type=F32)
            if softmax:
                s = s * scale
                if has_bias:
                    s = s + kcol[rows, :]
                if diag:
                    s = jnp.where(_visible_t(mode, B), s, NEG_BIG)
                s_all[rows, :] = s
                m_ref[...] = jnp.maximum(m_ref[...], jnp.max(s, axis=0, keepdims=True))
            else:
                if diag:
                    p = jnp.where(_visible_t(mode, B), s * _decay(lg, B, 0), 0.0)
                else:
                    p = s * _decay(lg, B, g - i)
                acc[...] += jnp.dot(vT[:, rows], p.astype(BF16), preferred_element_type=F32)

        def weighted(g):
            rows = slice(g * B, (g + 1) * B)
            p = jnp.exp(s_all[rows, :] - m_ref[...])
            l_ref[...] += jnp.sum(p, axis=0, keepdims=True)
            acc[...] += jnp.dot(vT[:, rows], p.astype(BF16), preferred_element_type=F32)

        for g in range(nq):
            pl.when(g < i)(functools.partial(scores, g, False))
            pl.when(g == i)(functools.partial(scores, g, True))
        if softmax:
            for g in range(nq):
                pl.when(g <= i)(functools.partial(weighted, g))
            o_ref[...] = (acc[...] / l_ref[...]).T
            lse_ref[...] = jnp.broadcast_to(m_ref[...] + jnp.log(l_ref[...]), (LANES, B)).T
        else:
            o_ref[...] = acc[...].T

    in_specs = [pl.BlockSpec((B, LANES), lambda h, i: (i, q1_cb + h)),
                pl.BlockSpec((S, LANES), lambda h, i: (0, k1_cb + h)),
                pl.BlockSpec((S, dv), lambda h, i: (0, v_cb + h))]
    args = [q1, k1, v]
    if two:
        in_specs += [pl.BlockSpec((B, LANES), lambda h, i: (i, q2_cb + h)),
                     pl.BlockSpec((S, LANES), lambda h, i: (0, 0))]
        args += [q2, k2]
    if has_bias:
        in_specs.append(pl.BlockSpec((None, nq, 1, B), lambda h, i: (h, 0, 0, 0)))
        args.append(kbias)
    out_shape = [jax.ShapeDtypeStruct((S, H * dv), F32)]
    out_specs = [pl.BlockSpec((B, dv), lambda h, i: (i, h))]
    if softmax:
        out_shape.append(jax.ShapeDtypeStruct((S, H * LANES), F32))
        out_specs.append(pl.BlockSpec((B, LANES), lambda h, i: (i, h)))
    kw = 2 * LANES if two else LANES
    scratch = [pltpu.VMEM((S, kw), BF16), pltpu.VMEM((dv, S), BF16), pltpu.VMEM((dv, B), F32)]
    if softmax:
        scratch += [pltpu.VMEM((1, B), F32), pltpu.VMEM((1, B), F32), pltpu.VMEM((S, B), F32)]
    if has_bias:
        scratch.append(pltpu.VMEM((S, 1), F32))
    res = pl.pallas_call(body, out_shape=tuple(out_shape), grid=(H, nq), in_specs=in_specs,
                         out_specs=tuple(out_specs), scratch_shapes=scratch, name=name,
                         compiler_params=_cparams("parallel", "arbitrary"))(*args)
    return res if softmax else (res[0], None)


def attn_bwd(name, mode, q1, q1_cb, k1, k1_cb, v, v_cb, H, dv, scale, do, o=None, lse=None,
             q2=None, q2_cb=0, k2=None, kbias=None):
    S = q1.shape[0]
    B = _attn_block(S)
    nb = S // B
    softmax = mode != "ret"
    two = mode == "mla"
    has_bias = mode == "fox"

    def body(*refs):
        it = iter(refs)
        q1_ref, k1_ref, v_ref, do_ref = next(it), next(it), next(it), next(it)
        o_ref = next(it) if softmax else None
        lse_ref = next(it) if softmax else None
        q2_ref = next(it) if two else None
        k2_ref = next(it) if two else None
        kb_ref = next(it) if has_bias else None
        dq1_ref, dk1_ref, dv_ref = next(it), next(it), next(it)
        dq2_ref = next(it) if two else None
        dk2_ref = next(it) if two else None
        dkb_ref = next(it) if has_bias else None
        drow_ref = next(it) if has_bias else None
        qbuf, dobuf = next(it), next(it)
        qT, doT = next(it), next(it)
        delta = next(it) if softmax else None
        dk_acc, dv_acc = next(it), next(it)
        dkb_acc = next(it) if has_bias else None
        h = pl.program_id(0)
        j = pl.program_id(1)

        @pl.when(j == 0)
        def _():
            qbuf[:, 0:LANES] = q1_ref[...].astype(BF16)
            dobuf[...] = do_ref[...].astype(BF16)
            qT[0:LANES, :] = q1_ref[...].astype(F32).T.astype(BF16)
            doT[...] = do_ref[...].astype(F32).T.astype(BF16)
            dq1_ref[...] = jnp.zeros_like(dq1_ref)
            if has_bias:
                drow_ref[...] = jnp.zeros_like(drow_ref)
            if two:
                qbuf[:, LANES:2 * LANES] = q2_ref[...].astype(BF16)
                qT[LANES:2 * LANES, :] = q2_ref[...].astype(F32).T.astype(BF16)
                dq2_ref[...] = jnp.zeros_like(dq2_ref)
            if softmax:
                def drow(t, carry):
                    rows = pl.ds(pl.multiple_of(t * B, B), B)
                    d = jnp.sum(do_ref[rows, :].astype(F32) * o_ref[rows, :], axis=1, keepdims=True)
                    delta[rows, :] = jnp.broadcast_to(d, (B, LANES))
                    return carry
                lax.fori_loop(0, nb, drow, 0)

        kj = k1_ref[...].astype(BF16)
        if two:
            kj = jnp.concatenate([kj, k2_ref[...].astype(BF16)], axis=1)
        vj = v_ref[...].astype(BF16)
        kbj = kb_ref[...] if has_bias else None
        lg = _ret_log_gamma(h) if mode == "ret" else None
        dk_acc[...] = jnp.zeros_like(dk_acc)
        dv_acc[...] = jnp.zeros_like(dv_acc)
        if has_bias:
            dkb_acc[...] = jnp.zeros_like(dkb_acc)

        def step(i, diag):
            rows = slice(i * B, (i + 1) * B)
            qi = qbuf[rows, :]
            doi = dobuf[rows, :]
            s = lax.dot_general(qi, kj, NT_DIMS, preferred_element_type=F32)
            dp = lax.dot_general(doi, vj, NT_DIMS, preferred_element_type=F32)
            if softmax:
                s = s * scale
                if has_bias:
                    s = s + kbj
                if diag:
                    s = jnp.where(_visible(mode, B), s, NEG_BIG)
                p = jnp.exp(s - jnp.tile(lse_ref[rows, :], (1, B // LANES)))
                ds = p * (dp - jnp.tile(delta[rows, :], (1, B // LANES)))
                if has_bias:
                    dkb_acc[...] += jnp.sum(ds, axis=0, keepdims=True)
                    drow_ref[rows, :] += jnp.broadcast_to(jnp.sum(ds, axis=1, keepdims=True), (B, LANES))
                dsb = (ds * scale).astype(BF16)
            else:
                if diag:
                    dec = jnp.where(_visible(mode, B), _decay(lg, B, 0), 0.0)
                else:
                    dec = _decay(lg, B, i - j)
                p = s * dec
                dsb = (dp * dec).astype(BF16)
            dv_acc[...] += jnp.dot(doT[:, rows], p.astype(BF16), preferred_element_type=F32)
            dk_acc[...] += jnp.dot(qT[:, rows], dsb, preferred_element_type=F32)
            dq = jnp.dot(dsb, kj, preferred_element_type=F32)
            dq1_ref[rows, :] += dq[:, 0:LANES]
            if two:
                dq2_ref[rows, :] += dq[:, LANES:2 * LANES]

        for i in range(nb):
            pl.when(i == j)(functools.partial(step, i, True))
            pl.when(i > j)(functools.partial(step, i, False))
        dk1_ref[...] = dk_acc[0:LANES, :].T
        dv_ref[...] = dv_acc[...].T
        if two:
            dk2_ref[...] = dk_acc[LANES:2 * LANES, :].T
        if has_bias:
            dkb_ref[...] = dkb_acc[...]

    full = lambda w, cb: pl.BlockSpec((S, w), lambda h, j: (0, cb + h))
    blk = lambda w, cb: pl.BlockSpec((B, w), lambda h, j: (j, cb + h))
    in_specs = [full(LANES, q1_cb), blk(LANES, k1_cb), blk(dv, v_cb), full(dv, 0)]
    args = [q1, k1, v, do]
    if softmax:
        in_specs += [full(dv, 0), full(LANES, 0)]
        args += [o, lse]
    if two:
        in_specs += [full(LANES, q2_cb), pl.BlockSpec((B, LANES), lambda h, j: (j, 0))]
        args += [q2, k2]
    if has_bias:
        in_specs.append(pl.BlockSpec((None, None, 1, B), lambda h, j: (h, j, 0, 0)))
        args.append(kbias)
    names = ["dq1", "dk1", "dv"]
    out_shape = [jax.ShapeDtypeStruct((S, H * LANES), F32), jax.ShapeDtypeStruct((S, H * LANES), F32),
                 jax.ShapeDtypeStruct((S, H * dv), F32)]
    out_specs = [full(LANES, 0), blk(LANES, 0), blk(dv, 0)]
    if two:
        names += ["dq2", "dk2h"]
        out_shape += [jax.ShapeDtypeStruct((S, H * LANES), F32)] * 2
        out_specs += [full(LANES, 0), blk(LANES, 0)]
    if has_bias:
        names.append("dkb")
        out_shape.append(jax.ShapeDtypeStruct((H, nb, 1, B), F32))
        out_specs.append(pl.BlockSpec((None, None, 1, B), lambda h, j: (h, j, 0, 0)))
        names.append("drow")
        out_shape.append(jax.ShapeDtypeStruct((S, H * LANES), F32))
        out_specs.append(full(LANES, 0))
    kw = 2 * LANES if two else LANES
    scratch = [pltpu.VMEM((S, kw), BF16), pltpu.VMEM((S, dv), BF16),
               pltpu.VMEM((kw, S), BF16), pltpu.VMEM((dv, S), BF16)]
    if softmax:
        scratch.append(pltpu.VMEM((S, LANES), F32))
    scratch += [pltpu.VMEM((kw, B), F32), pltpu.VMEM((dv, B), F32)]
    if has_bias:
        scratch.append(pltpu.VMEM((1, B), F32))
    res = pl.pallas_call(body, out_shape=tuple(out_shape), grid=(H, nb), in_specs=in_specs,
                         out_specs=tuple(out_specs), scratch_shapes=scratch, name=name,
                         compiler_params=_cparams("parallel", "arbitrary"))(*args)
    return dict(zip(names, res))


def head_sum(name, x, H, out_dtype):
    S = x.shape[0]
    tr = _pick(S, 512, 8)

    def body(x_ref, o_ref):
        acc = x_ref[:, 0:LANES]
        for h in range(1, H):
            acc = acc + x_ref[:, h * LANES:(h + 1) * LANES]
        o_ref[...] = acc.astype(o_ref.dtype)

    return pl.pallas_call(
        body, out_shape=jax.ShapeDtypeStruct((S, LANES), out_dtype), grid=(S // tr,),
        in_specs=[pl.BlockSpec((tr, H * LANES), lambda i: (i, 0))],
        out_specs=pl.BlockSpec((tr, LANES), lambda i: (i, 0)), name=name,
        compiler_params=_cparams("parallel"))(x)


def _mesh_pos():
    return lax.axis_index("x"), lax.axis_index("y"), lax.axis_index("c")


def _peer(pos, k):
    x, y, c = pos
    px = 1 - x if k & 4 else x
    py = 1 - y if k & 2 else y
    pc = 1 - c if k & 1 else c
    return (px, py, pc), 4 * px + 2 * py + pc


def exchange(name, tensors):
    nt = len(tensors)
    flat_in, counts = [], []
    out_shape = []
    for mode, srcs in tensors:
        counts.append(len(srcs))
        flat_in += list(srcs)
        rc = srcs[0].shape[-2:]
        out_shape.append(jax.ShapeDtypeStruct((len(srcs), N_DEV) + tuple(rc), srcs[0].dtype))
    n_in = len(flat_in)

    def body(*refs):
        ins = refs[:n_in]
        outs = refs[n_in:n_in + nt]
        send_sems, recv_sems, local_sems = refs[n_in + nt:]
        pos = _mesh_pos()
        me = 4 * pos[0] + 2 * pos[1] + pos[2]
        srcs_of, base = [], 0
        for t in range(nt):
            srcs_of.append(ins[base:base + counts[t]])
            base += counts[t]

        def src_view(t, l, slot):
            ref = srcs_of[t][l]
            return ref if tensors[t][0] == "gather" else ref.at[slot]

        def all_layers(t, slot):
            return outs[t].at[pl.ds(0, counts[t]), slot]

        for t in range(nt):
            for l in range(counts[t]):
                pltpu.make_async_copy(src_view(t, l, me), outs[t].at[l, me], local_sems.at[t]).start()
        for t in range(nt):
            for k in range(1, N_DEV):
                peer, pid = _peer(pos, k)
                for l in range(counts[t]):
                    pltpu.make_async_remote_copy(
                        src_ref=src_view(t, l, pid), dst_ref=outs[t].at[l, me],
                        send_sem=send_sems.at[t, k - 1], recv_sem=recv_sems.at[t, k - 1],
                        device_id=peer, device_id_type=pl.DeviceIdType.MESH).start()
        for t in range(nt):
            for k in range(1, N_DEV):
                peer, pid = _peer(pos, k)
                pltpu.make_async_remote_copy(
                    src_ref=all_layers(t, pid), dst_ref=all_layers(t, pid),
                    send_sem=send_sems.at[t, k - 1], recv_sem=recv_sems.at[t, k - 1],
                    device_id=peer, device_id_type=pl.DeviceIdType.MESH).wait()
        for t in range(nt):
            pltpu.make_async_copy(all_layers(t, me), all_layers(t, me), local_sems.at[t]).wait()

    any_spec = pl.BlockSpec(memory_space=pl.ANY)
    return pl.pallas_call(
        body, out_shape=tuple(out_shape), in_specs=[any_spec] * n_in, out_specs=tuple([any_spec] * nt),
        scratch_shapes=[pltpu.SemaphoreType.DMA((nt, N_DEV - 1)), pltpu.SemaphoreType.DMA((nt, N_DEV - 1)),
                        pltpu.SemaphoreType.DMA((nt,))],
        name=name)(*flat_in)


HBM_SPEC = pl.BlockSpec(memory_space=pltpu.HBM)
SEM_SPEC = pl.BlockSpec(memory_space=pltpu.SEMAPHORE)
DATAFLOW = pltpu.SideEffectType.DATAFLOW_SIDE_EFFECTING


def _hbm(a):
    return pltpu.with_memory_space_constraint(a, pltpu.HBM)


def landing_zones(mode, srcs):
    pos = _mesh_pos()
    me = 4 * pos[0] + 2 * pos[1] + pos[2]
    lands = []
    for s in srcs:
        R, C = s.shape[-2:]
        own = s[None] if mode == "gather" else lax.dynamic_slice(s, (me, 0, 0), (1, R, C))
        lands.append(lax.dynamic_update_slice(lax.empty((N_DEV, R, C), s.dtype), own, (me, 0, 0)))
    return lands


def exchange_start(name, mode, srcs, lands, after=None):
    n = len(srcs)
    extra = [] if after is None else [after]

    def body(*refs):
        src_refs, land_refs = refs[:n], refs[n:2 * n]
        send_sems, recv_sems = refs[2 * n + len(extra)], refs[2 * n + len(extra) + 1]
        token = refs[-1]
        pos = _mesh_pos()
        me = 4 * pos[0] + 2 * pos[1] + pos[2]
        for t in range(n):
            for k in range(1, N_DEV):
                peer, pid = _peer(pos, k)
                src = src_refs[t] if mode == "gather" else src_refs[t].at[pid]
                pltpu.make_async_remote_copy(
                    src_ref=src, dst_ref=land_refs[t].at[me], send_sem=send_sems.at[t], recv_sem=recv_sems.at[t],
                    device_id=peer, device_id_type=pl.DeviceIdType.MESH).start()
        token[...] = jnp.zeros_like(token)

    thru = [pltpu.HBM(a.shape, a.dtype) for a in list(srcs) + list(lands)]
    out_shape = (pltpu.SemaphoreType.DMA((n,)), pltpu.SemaphoreType.DMA((n,)), *thru,
                 jax.ShapeDtypeStruct((8, LANES), F32))
    res = pl.pallas_call(
        body, out_shape=out_shape, in_specs=[HBM_SPEC] * (2 * n) + [pl.BlockSpec(memory_space=pl.ANY)] * len(extra),
        out_specs=(SEM_SPEC, SEM_SPEC, *([HBM_SPEC] * (2 * n)), pl.BlockSpec(memory_space=pltpu.VMEM)),
        input_output_aliases={i: 2 + i for i in range(2 * n)}, name=name,
        compiler_params=pltpu.CompilerParams(has_side_effects=DATAFLOW))(
            *[_hbm(a) for a in list(srcs) + list(lands)], *extra)
    return res[0], res[1], list(res[2:2 + n]), list(res[2 + n:2 + 2 * n]), res[-1]


def exchange_wait(name, send_sems, recv_sems, srcs, lands, after):
    n = len(srcs)

    def body(*refs):
        land_refs = refs[n:2 * n]
        s_sems, r_sems = refs[2 * n], refs[2 * n + 1]
        pos = _mesh_pos()
        for t in range(n):
            seven = land_refs[t].at[pl.ds(0, N_DEV - 1)]
            cp = pltpu.make_async_remote_copy(
                src_ref=seven, dst_ref=seven, send_sem=s_sems.at[t], recv_sem=r_sems.at[t],
                device_id=pos, device_id_type=pl.DeviceIdType.MESH)
            cp.wait_send()
            cp.wait_recv()

    arrs = list(srcs) + list(lands)
    afters = list(after) if isinstance(after, (list, tuple)) else [after]
    res = pl.pallas_call(
        body, out_shape=tuple(pltpu.HBM(a.shape, a.dtype) for a in arrs),
        in_specs=[HBM_SPEC] * (2 * n) + [SEM_SPEC, SEM_SPEC] + [pl.BlockSpec(memory_space=pl.ANY)] * len(afters),
        out_specs=tuple([HBM_SPEC] * (2 * n)), input_output_aliases={i: i for i in range(2 * n)}, name=name,
        compiler_params=pltpu.CompilerParams(has_side_effects=DATAFLOW))(*arrs, send_sems, recv_sems, *afters)
    return list(res[n:])


def reduce_parts(name, parts):
    n, R, C = parts.shape
    tr = _pick(R, max(8, (1 << 20) // (C * 4) // 8 * 8), 8)

    def body(p_ref, o_ref):
        acc = p_ref[0].astype(F32)
        for s in range(1, n):
            acc = acc + p_ref[s].astype(F32)
        o_ref[...] = acc

    return pl.pallas_call(
        body, out_shape=jax.ShapeDtypeStruct((R, C), F32), grid=(R // tr,),
        in_specs=[pl.BlockSpec((n, tr, C), lambda i: (0, i, 0))],
        out_specs=pl.BlockSpec((tr, C), lambda i: (i, 0)), name=name,
        compiler_params=_cparams("parallel"))(parts)


def adamw(name, w, m, v, parts, first=0, prev=None):
    L, R, C = w.shape
    nl = len(parts)
    n = parts[0].shape[0]
    tr = _pick(R, max(8, (1 << 19) // (C * 4) // 8 * 8), 8)
    n_prev = 0 if prev is None else 4

    def body(*refs):
        w_ref, m_ref, v_ref = refs[:3]
        p_refs = refs[3:3 + nl]
        g_ref, d_ref, nm_ref, nv_ref = refs[3 + nl + n_prev:]

        def update(p_ref):
            g = p_ref[0].astype(F32)
            for s in range(1, n):
                g = g + p_ref[s].astype(F32)
            wv = w_ref[...]
            mn = ADAM_B1 * m_ref[...] + (1.0 - ADAM_B1) * g
            vn = ADAM_B2 * v_ref[...] + (1.0 - ADAM_B2) * jnp.square(g)
            m_hat = mn / (1.0 - ADAM_B1 ** ADAM_STEP)
            v_hat = vn / (1.0 - ADAM_B2 ** ADAM_STEP)
            g_ref[...] = g
            d_ref[...] = -ADAM_LR * (m_hat / (jnp.sqrt(v_hat) + ADAM_EPS) + ADAM_WD * wv)
            nm_ref[...] = mn
            nv_ref[...] = vn

        for k in range(nl):
            pl.when(pl.program_id(0) == k)(functools.partial(update, p_refs[k]))

    blk = pl.BlockSpec((None, tr, C), lambda l, i: (first + l, i, 0))
    pspecs = [pl.BlockSpec((n, tr, C), lambda l, i, k=k: (0, jnp.where(l == k, i, 0), 0)) for k in range(nl)]
    sh = jax.ShapeDtypeStruct((L, R, C), F32)
    prev_args = [] if prev is None else list(prev)
    return pl.pallas_call(
        body, out_shape=(sh, sh, sh, sh), grid=(nl, R // tr),
        in_specs=[blk, blk, blk] + pspecs + [pl.BlockSpec(memory_space=pl.ANY)] * n_prev,
        out_specs=(blk, blk, blk, blk), input_output_aliases={3 + nl + q: q for q in range(n_prev)}, name=name,
        compiler_params=_cparams("arbitrary", "arbitrary"))(w, m, v, *parts, *prev_args)


def _cols_from_blocks(g):
    n, R, c = g.shape
    return g.transpose(1, 0, 2).reshape(R, n * c)


def _cols_to_blocks(w):
    R, C = w.shape
    return w.reshape(R, N_DEV, C // N_DEV).transpose(1, 0, 2)


def _uq_permute(w):
    lead = w.shape[:-1]
    w4 = w.reshape(lead + (MLA_HEADS, MLA_NOPE + MLA_ROPE))
    nope = w4[..., :MLA_NOPE].reshape(lead + (MLA_HEADS * MLA_NOPE,))
    rope = jnp.pad(w4[..., MLA_NOPE:], [(0, 0)] * (w4.ndim - 1) + [(0, LANES - MLA_ROPE)])
    return jnp.concatenate([nope, rope.reshape(lead + (MLA_HEADS * LANES,))], axis=-1)


def _uq_unpermute(w):
    lead = w.shape[:-1]
    n = MLA_HEADS * MLA_NOPE
    nope = w[..., :n].reshape(lead + (MLA_HEADS, MLA_NOPE))
    rope = w[..., n:].reshape(lead + (MLA_HEADS, LANES))[..., :MLA_ROPE]
    return jnp.concatenate([nope, rope], axis=-1).reshape(lead + (MLA_HEADS * (MLA_NOPE + MLA_ROPE),))


def _ukv_permute(w):
    lead = w.shape[:-1]
    w4 = w.reshape(lead + (MLA_HEADS, 2, MLA_NOPE))
    return jnp.swapaxes(w4, -3, -2).reshape(lead + (2 * MLA_HEADS * MLA_NOPE,))


def _ukv_unpermute(w):
    lead = w.shape[:-1]
    w4 = w.reshape(lead + (2, MLA_HEADS, MLA_NOPE))
    return jnp.swapaxes(w4, -3, -2).reshape(lead + (2 * MLA_HEADS * MLA_NOPE,))


SMALL = ["norm1_g", "mla_q_norm_g", "mla_kv_norm_g", "fox_b_f", "norm2_g", "ffn_conv_b", "final_norm_g"]
SMALL_TILE = 8 * LANES


def _pack_small(d):
    flat = jnp.concatenate([d[n].reshape(-1).astype(F32) for n in SMALL])
    pad = -flat.shape[0] % SMALL_TILE
    return jnp.pad(flat, (0, pad)).reshape(-1, LANES)


def _unpack_small(packed, like):
    flat = packed.reshape(-1)
    out, o = {}, 0
    for n in SMALL:
        sz = int(np.prod(like[n].shape))
        out[n] = flat[o:o + sz].reshape(like[n].shape)
        o += sz
    return out


WEIGHTS = ["norm1_g", "w_in", "mla_q_norm_g", "mla_kv_norm_g", "mla_w_uq", "mla_w_ukv", "fox_b_f", "w_br_fox",
           "w_br_mla", "w_br_ret", "w_out", "norm2_g", "ffn_w_up", "ffn_w_gate", "ffn_conv_w", "ffn_conv_b",
           "ffn_w_down", "final_norm_g"]
EARLY = ["w_in", "mla_w_uq", "mla_w_ukv"]
LATE = ["w_br_fox", "w_br_mla", "w_br_ret", "w_out", "ffn_w_up", "ffn_w_gate", "ffn_conv_w", "ffn_w_down"]
FFN = ["ffn_w_up", "ffn_w_gate", "ffn_conv_w", "ffn_w_down"]
TRANSPOSED = ("ffn_w_up", "ffn_w_gate")
BIG = EARLY + LATE
X_EARLY = ["w_in_mix", "mla_w_uq", "mla_w_ukv"]
X_LATE = ["w_in_gates"] + LATE
X_MID = ["w_in_gates", "w_out", "w_br_fox", "w_br_mla", "w_br_ret"]
X_REST = ["mla_w_uq", "mla_w_ukv", "w_in_mix"]


def kernel(x, norm1_g, w_in, mla_q_norm_g, mla_kv_norm_g, mla_w_uq, mla_w_ukv, fox_b_f, w_br_fox, w_br_mla, w_br_ret, w_out, norm2_g, ffn_w_up, ffn_w_gate, ffn_conv_w, ffn_conv_b, ffn_w_down, final_norm_g, loss_target, m_norm1_g, m_w_in, m_mla_q_norm_g, m_mla_kv_norm_g, m_mla_w_uq, m_mla_w_ukv, m_fox_b_f, m_w_br_fox, m_w_br_mla, m_w_br_ret, m_w_out, m_norm2_g, m_ffn_w_up, m_ffn_w_gate, m_ffn_conv_w, m_ffn_conv_b, m_ffn_w_down, m_final_norm_g, v_norm1_g, v_w_in, v_mla_q_norm_g, v_mla_kv_norm_g, v_mla_w_uq, v_mla_w_ukv, v_fox_b_f, v_w_br_fox, v_w_br_mla, v_w_br_ret, v_w_out, v_norm2_g, v_ffn_w_up, v_ffn_w_gate, v_ffn_conv_w, v_ffn_conv_b, v_ffn_w_down, v_final_norm_g):
    env = dict(locals())
    W = {n: env[n] for n in WEIGHTS}
    Mo = {n: env["m_" + n] for n in WEIGHTS}
    Vo = {n: env["v_" + n] for n in WEIGHTS}
    S, D = x.shape[1], x.shape[2]
    L = w_in.shape[0]
    lay = InLayout(D)
    NP = lay.total
    f = ffn_w_up.shape[-1]
    xs = x.reshape(S, D)
    tgt = loss_target.reshape(S, D)

    local = {n: W[n].astype(BF16) for n in BIG if n != "w_in"}
    local["w_in_gates"], local["w_in_mix"] = lay.permute(W["w_in"].astype(BF16))
    pending = {}
    token = None
    for l in range(L):
        for grp, names in (("a", X_EARLY), ("b", X_LATE)):
            srcs = [local[n][l] for n in names]
            *flight, token = exchange_start(f"gather_start_{l}{grp}", "gather", srcs, landing_zones("gather", srcs),
                                            token)
            pending[l, grp] = flight
    gather_token = token
    cbias_all = ffn_conv_b.reshape(L, 1, N_DEV, 1, f)

    def early_weights(l, after):
        g = dict(zip(X_EARLY, exchange_wait(f"gather_wait_{l}a", *pending[l, "a"], after)))
        return dict(Win=g["w_in_mix"].reshape(1, D, NP), Wuq=_uq_permute(_cols_from_blocks(g["mla_w_uq"])),
                    Wukv=_ukv_permute(_cols_from_blocks(g["mla_w_ukv"])))

    def late_weights(l, after):
        g = dict(zip(X_LATE, exchange_wait(f"gather_wait_{l}b", *pending[l, "b"], after)))
        return dict(
            Wgates=g["w_in_gates"].reshape(1, D, 3 * D),
            Wout=g["w_out"].reshape(1, D, D), Wbf=_cols_from_blocks(g["w_br_fox"]),
            Wbm=_cols_from_blocks(g["w_br_mla"]), Wbr=_cols_from_blocks(g["w_br_ret"]), Wup=g["ffn_w_up"][None],
            Wgate=g["ffn_w_gate"][None], Wdown=g["ffn_w_down"][None], Wconv=g["ffn_conv_w"].astype(F32)[None],
            cbias=cbias_all[l])

    tab64 = rope_tables(S, MLA_ROPE)
    tab128 = rope_tables(S, RET_DK)
    fox_scale = FOX_DH ** -0.5
    mla_scale = (MLA_NOPE + MLA_ROPE) ** -0.5
    ret_kscale = RET_DK ** -0.5
    R = S // LANES
    AB = _attn_block(S)
    NOPE_W = MLA_HEADS * MLA_NOPE

    def vec(a):
        return a.reshape(1, -1)

    saved = []
    xc = xs
    for l in range(L):
        Wl = early_weights(l, gather_token if l == 0 else xc)
        Win, Wuq, Wukv = Wl["Win"], Wl["Wuq"], Wl["Wukv"]
        s = {"x": xc, "W": Wl}
        h1 = rms_fwd("norm1", xc, 0, D, vec(norm1_g[l]), BF16)
        P = mm_nn("in_proj", h1, Win, F32, b_lead=0)
        s.update(h1=h1, P=P)
        ff_off = lay.off["ff"]
        ft = P[:, ff_off:ff_off + FOX_HEADS].T.reshape(FOX_HEADS, R, LANES)
        bfl = jnp.broadcast_to(fox_b_f[l].reshape(FOX_HEADS, 1, 1), (FOX_HEADS, 1, LANES))
        kbias = fox_gate_fwd("fox_gate", ft, bfl).reshape(FOX_HEADS, S // AB, 1, AB)
        o_fox, lse_fox = attn_fwd("fox_attn", "fox", P, lay.cb("fq", LANES), P, lay.cb("fk", LANES),
                                  P, lay.cb("fv", LANES), FOX_HEADS, FOX_DH, fox_scale, kbias=kbias)
        s.update(ft=ft, bfl=bfl, kbias=kbias, o_fox=o_fox, lse_fox=lse_fox)
        cqn = rms_fwd("mla_q_norm", P, lay.cb("mq", MLA_Q_LORA), MLA_Q_LORA, vec(mla_q_norm_g[l]), BF16)
        qall = mm_nn("mla_uq", cqn, Wuq, F32)
        ckvn = rms_fwd("mla_kv_norm", P, lay.cb("mkv", MLA_KV_LORA), MLA_KV_LORA, vec(mla_kv_norm_g[l]), BF16)
        kvall = mm_nn("mla_ukv", ckvn, Wukv, F32)
        qrope = rope_apply("mla_q_rope", qall, NOPE_W // LANES, MLA_HEADS, tab64, 1.0, F32)
        krope = rope_apply("mla_k_rope", P, lay.cb("mkr", LANES), 1, tab64, 1.0, F32)
        o_mla, lse_mla = attn_fwd("mla_attn", "mla", qall, 0, kvall, 0, kvall, NOPE_W // MLA_V, MLA_HEADS, MLA_V,
                                  mla_scale, q2=qrope, q2_cb=0, k2=krope)
        s.update(cqn=cqn, qall=qall, ckvn=ckvn, kvall=kvall, qrope=qrope, krope=krope, o_mla=o_mla,
                 lse_mla=lse_mla)
        rq = rope_apply("ret_q_rope", P, lay.cb("rq", LANES), RET_HEADS, tab128, 1.0, F32)
        rk = rope_apply("ret_k_rope", P, lay.cb("rk", LANES), RET_HEADS, tab128, ret_kscale, F32)
        o_ret, _ = attn_fwd("ret_attn", "ret", rq, 0, rk, 0, P, lay.cb("rv", RET_DV), RET_HEADS, RET_DV, 1.0)
        c_ret = ret_out_fwd("ret_out", o_ret, P, lay.cb("rg", RET_DV), BF16)
        s.update(rq=rq, rk=rk, o_ret=o_ret, c_ret=c_ret)
        Wl.update(late_weights(l, (o_fox, o_mla, c_ret)))
        Wout, Wbf, Wbm, Wbr = Wl["Wout"], Wl["Wbf"], Wl["Wbm"], Wl["Wbr"]
        Wup, Wgate, Wdown, Wconv, cbias = (Wl[k] for k in ("Wup", "Wgate", "Wdown", "Wconv", "cbias"))
        A = mm_nn("br_fox", o_fox, Wbf, F32)
        Bm = mm_nn("br_mla", o_mla, Wbm, F32)
        C = mm_nn("br_ret", c_ret, Wbr, F32)
        Pg = mm_nn("gate_proj", h1, Wl["Wgates"], F32, b_lead=0)
        s["Pg"] = Pg
        merged = merge_fwd("merge", Pg, 0, A, Bm, C, BF16)
        x2 = mm_nn("out_proj", merged, Wout, F32, b_lead=0, res=xc)
        s.update(A=A, Bm=Bm, C=C, merged=merged, x2=x2)
        h2 = rms_fwd("norm2", x2, 0, D, vec(norm2_g[l]), BF16)
        u = ffn_up("ffn_up", h2, Wup, 0, F32)
        gt = ffn_up("ffn_gate", h2, Wgate, 0, F32)
        act = ffn_act_fwd("ffn_act", u, gt, Wconv, cbias, 0, BF16)
        xc = ffn_down("ffn_down", act, Wdown, 0, x2, F32)
        s.update(h2=h2, u=u, gt=gt, act=act)
        saved.append(s)

    loss_tile, dx, dgf = loss_head("loss_head", xc, vec(final_norm_g), tgt)
    loss = lax.psum(loss_tile[0, 0], ("x", "y", "c"))

    gbig = {n: [None] * L for n in BIG + ["w_in_gates", "w_in_mix"]}
    gsmall = {n: [None] * L for n in SMALL if n != "final_norm_g"}
    scattering = {}
    scatter_token = None

    def start_scatter(name, names, l):
        srcs = [gbig[n][l] for n in names]
        *flight, tok = exchange_start(name, "scatter", srcs, landing_zones("scatter", srcs))
        return flight, tok

    for l in reversed(range(L)):
        s = saved[l]
        P = s["P"]
        Wl = s["W"]
        Win, Wout, Wuq, Wukv, Wbf, Wbm, Wbr = (Wl[k] for k in ("Win", "Wout", "Wuq", "Wukv", "Wbf", "Wbm", "Wbr"))
        Wup, Wgate, Wdown, Wconv, cbias = (Wl[k] for k in ("Wup", "Wgate", "Wdown", "Wconv", "cbias"))
        dxb = (dx if scatter_token is None else dx + scatter_token[0, 0]).astype(BF16)
        dact = ffn_down_bwd_act("ffn_down_da", dxb, Wdown, 0, BF16)
        gbig["ffn_w_down"][l] = ffn_down_bwd_w("ffn_down_dw", s["act"], dxb, BF16)
        g, dgt = ffn_act_bwd_point("ffn_act_bwd", s["u"], s["gt"], Wconv, cbias, 0, dact)
        du, dcw, dcb = ffn_act_bwd_conv("ffn_conv_bwd", s["u"], g, Wconv, 0)
        gbig["ffn_conv_w"][l] = dcw.astype(BF16)
        gsmall["ffn_conv_b"][l] = dcb.reshape(-1)
        gbig["ffn_w_up"][l] = ffn_down_bwd_w("ffn_up_dw", du, s["h2"], BF16)
        gbig["ffn_w_gate"][l] = ffn_down_bwd_w("ffn_gate_dw", dgt, s["h2"], BF16)
        dh2 = ffn_up_bwd_h("ffn_up_dh", du, Wup, 0, None, F32)
        dh2 = ffn_up_bwd_h("ffn_gate_dh", dgt, Wgate, 0, dh2, BF16)
        dx2, dg2 = rms_bwd("norm2_bwd", s["x2"], 0, D, vec(norm2_g[l]), dh2, F32, res=dx)
        gsmall["norm2_g"][l] = dg2.reshape(-1)
        scattering[l, "ffn"], scatter_token = start_scatter(f"scatter_start_{l}ffn", FFN, l)
        dx2b = (dx2 + scatter_token[0, 0]).astype(BF16)
        dmerged = mm_nt("out_proj_dm", dx2b, Wout, BF16, b_lead=0)
        gbig["w_out"][l] = mm_tn("out_proj_dw", s["merged"], dx2b, BF16).reshape(N_DEV, D // N_DEV, D)
        dgates, dA, dB, dC = merge_bwd("merge_bwd", s["Pg"], 0, s["A"], s["Bm"], s["C"], dmerged)
        gbig["w_br_fox"][l] = _cols_to_blocks(mm_tn("br_fox_dw", s["o_fox"], dA, BF16))
        gbig["w_br_mla"][l] = _cols_to_blocks(mm_tn("br_mla_dw", s["o_mla"], dB, BF16))
        gbig["w_br_ret"][l] = _cols_to_blocks(mm_tn("br_ret_dw", s["c_ret"], dC, BF16))
        gbig["w_in_gates"][l] = mm_tn("gate_proj_dw", s["h1"], dgates, BF16).reshape(N_DEV, D // N_DEV, 3 * D)
        scattering[l, "mid"], scatter_token = start_scatter(f"scatter_start_{l}mid", X_MID, l)
        dh1_gates = mm_nt("gate_proj_dh", dgates, Wl["Wgates"], F32, b_lead=0)
        do_fox = mm_nt("br_fox_do", dA + scatter_token[0, 0].astype(BF16), Wbf, F32)
        do_mla = mm_nt("br_mla_do", dB, Wbm, F32)
        dc_ret = mm_nt("br_ret_do", dC, Wbr, BF16)
        do_ret, drg = ret_out_bwd("ret_out_bwd", s["o_ret"], P, lay.cb("rg", RET_DV), dc_ret)
        rb = attn_bwd("ret_attn_bwd", "ret", s["rq"], 0, s["rk"], 0, P, lay.cb("rv", RET_DV), RET_HEADS, RET_DV,
                      1.0, do_ret)
        drq = rope_apply("ret_q_rope_bwd", rb["dq1"], 0, RET_HEADS, tab128, 1.0, BF16, transpose=True)
        drk = rope_apply("ret_k_rope_bwd", rb["dk1"], 0, RET_HEADS, tab128, ret_kscale, BF16, transpose=True)
        drv = rb["dv"].astype(BF16)
        mb = attn_bwd("mla_attn_bwd", "mla", s["qall"], 0, s["kvall"], 0, s["kvall"], NOPE_W // MLA_V, MLA_HEADS,
                      MLA_V, mla_scale, do_mla, o=s["o_mla"], lse=s["lse_mla"], q2=s["qrope"], q2_cb=0,
                      k2=s["krope"])
        dqrope = rope_apply("mla_q_rope_bwd", mb["dq2"], 0, MLA_HEADS, tab64, 1.0, BF16, transpose=True)
        dkr_sum = head_sum("mla_k_rope_sum", mb["dk2h"], MLA_HEADS, F32)
        dmkr = rope_apply("mla_k_rope_bwd", dkr_sum, 0, 1, tab64, 1.0, BF16, transpose=True)
        dqall = jnp.concatenate([mb["dq1"].astype(BF16), dqrope], axis=1)
        dkvall = jnp.concatenate([mb["dk1"].astype(BF16), mb["dv"].astype(BF16)], axis=1)
        dcqn = mm_nt("mla_uq_dx", dqall, Wuq, F32)
        dckvn = mm_nt("mla_ukv_dx", dkvall, Wukv, F32)
        guq = _uq_unpermute(mm_tn("mla_uq_dw", s["cqn"], dqall, BF16))
        gukv = _ukv_unpermute(mm_tn("mla_ukv_dw", s["ckvn"], dkvall, BF16))
        gbig["mla_w_uq"][l] = _cols_to_blocks(guq)
        gbig["mla_w_ukv"][l] = _cols_to_blocks(gukv)
        dmq, dgq = rms_bwd("mla_q_norm_bwd", P, lay.cb("mq", MLA_Q_LORA), MLA_Q_LORA, vec(mla_q_norm_g[l]),
                           dcqn, BF16)
        dmkv, dgkv = rms_bwd("mla_kv_norm_bwd", P, lay.cb("mkv", MLA_KV_LORA), MLA_KV_LORA,
                             vec(mla_kv_norm_g[l]), dckvn, BF16)
        gsmall["mla_q_norm_g"][l] = dgq.reshape(-1)
        gsmall["mla_kv_norm_g"][l] = dgkv.reshape(-1)
        fb = attn_bwd("fox_attn_bwd", "fox", P, lay.cb("fq", LANES), P, lay.cb("fk", LANES), P,
                      lay.cb("fv", LANES), FOX_HEADS, FOX_DH, fox_scale, do_fox, o=s["o_fox"], lse=s["lse_fox"],
                      kbias=s["kbias"])
        drow = fb["drow"].reshape(S, FOX_HEADS, LANES)[:, :, 0].T.reshape(FOX_HEADS, R, LANES)
        dft, dbf = fox_gate_bwd("fox_gate_bwd", s["ft"], s["bfl"], fb["dkb"].reshape(FOX_HEADS, R, LANES) - drow)
        gsmall["fox_b_f"][l] = dbf[:, 0, 0]
        dff = jnp.pad(dft.reshape(FOX_HEADS, S).T, ((0, 0), (0, LANES - FOX_HEADS))).astype(BF16)
        segs = dict(rv=drv, rg=drg, mq=dmq, rq=drq, rk=drk, mkv=dmkv, fq=fb["dq1"].astype(BF16),
                    fk=fb["dk1"].astype(BF16), fv=fb["dv"].astype(BF16), mkr=dmkr, ff=dff)
        dP = jnp.concatenate([segs[n] for n in lay.order], axis=1)
        gbig["w_in_mix"][l] = mm_tn("in_proj_dw", s["h1"], dP, BF16).reshape(N_DEV, D // N_DEV, NP)
        scattering[l, "rest"], scatter_token = start_scatter(f"scatter_start_{l}rest", X_REST, l)
        dh1 = mm_nt("in_proj_dh", dP, Win, BF16, b_lead=0, res=dh1_gates)
        dx, dg1 = rms_bwd("norm1_bwd", s["x"], 0, D, vec(norm1_g[l]) + scatter_token[0:1, 0:1], dh1, F32, res=dx2)
        gsmall["norm1_g"][l] = dg1.reshape(-1)

    small_like = {n: W[n] for n in SMALL}
    small_part = {n: jnp.stack(gsmall[n]) for n in gsmall}
    small_part["final_norm_g"] = dgf.reshape(-1)
    small_recv = exchange("gather_small_grads", [("gather", [_pack_small(small_part)])])[0]
    ps = adamw("adamw_small", _pack_small(small_like)[None], _pack_small({n: Mo[n] for n in SMALL})[None],
               _pack_small({n: Vo[n] for n in SMALL})[None], [small_recv[0]])

    def received(l, after):
        r = {}
        for grp, names in (("ffn", FFN), ("mid", X_MID), ("rest", X_REST)):
            r.update(zip(names, exchange_wait(f"scatter_wait_{l}{grp}", *scattering[l, grp], after)))
        r["w_in"] = lay.unpermute(reduce_parts("w_in_gates_grad_sum", r["w_in_gates"]),
                                  reduce_parts("w_in_mix_grad_sum", r["w_in_mix"]))[None]
        return r

    def oriented(n, a):
        return jnp.swapaxes(a, 1, 2) if n in TRANSPOSED else a

    out = {}
    if L > 1:
        recv = [received(l, dx) for l in range(1, L)]
        for n in BIG:
            out[n] = adamw("adamw_" + n, oriented(n, W[n]), oriented(n, Mo[n]), oriented(n, Vo[n]),
                           [r[n] for r in recv], first=1)
    recv0 = received(0, out[BIG[-1]][0] if L > 1 else dx)
    for n in BIG:
        res = adamw("adamw0_" + n, oriented(n, W[n]), oriented(n, Mo[n]), oriented(n, Vo[n]), [recv0[n]],
                    first=0, prev=out.get(n))
        out[n] = tuple(oriented(n, a) for a in res)
    small_out = [_unpack_small(a[0], small_like) for a in ps]
    for n in SMALL:
        out[n] = tuple(so[n] for so in small_out)

    grads = [out[n][0] for n in WEIGHTS]
    deltas = [out[n][1] for n in WEIGHTS]
    new_m = [out[n][2] for n in WEIGHTS]
    new_v = [out[n][3] for n in WEIGHTS]
    return (loss, dx.reshape(1, S, D), *grads, *deltas, *new_m, *new_v)
```

```python
import functools
import math

import numpy as np
import jax
import jax.numpy as jnp
from jax import lax
from jax.experimental import pallas as pl
from jax.experimental.pallas import tpu as pltpu

F32 = jnp.float32
BF16 = jnp.bfloat16

CHUNK = 64
NORM_EPS = 1e-6
ROPE_THETA = 10000.0
FOX_HEADS, FOX_DH = 6, 128
FOX_W = FOX_HEADS * FOX_DH
MLA_HEADS, MLA_NOPE, MLA_ROPE, MLA_V = 6, 128, 64, 128
MLA_Q_LORA, MLA_KV_LORA = 512, 256
MLA_W = MLA_HEADS * MLA_V
RET_HEADS, RET_DK, RET_DV = 4, 128, 256
RET_QK_W, RET_V_W = RET_HEADS * RET_DK, RET_HEADS * RET_DV
ADAM_LR, ADAM_B1, ADAM_B2, ADAM_EPS, ADAM_WD, ADAM_STEP = 0.001, 0.9, 0.999, 1e-08, 0.01, 10

N_DEV = 8
LANES = 128
V7X_VMEM_LIMIT_BYTES = 52 * 1024 * 1024
NEG_BIG = -1e30
HIGHEST = lax.Precision.HIGHEST

NT_DIMS = (((1,), (1,)), ((), ()))
TN_DIMS = (((0,), (0,)), ((), ()))
NN_DIMS = (((1,), (0,)), ((), ()))


def _pick(n, cap, mult=LANES):
    best = None
    for t in range(mult, min(n, cap) + 1, mult):
        if n % t == 0:
            best = t
    return n if best is None else best


def _cparams(*sem):
    return pltpu.CompilerParams(dimension_semantics=sem, vmem_limit_bytes=V7X_VMEM_LIMIT_BYTES)


class InLayout:
    def __init__(self, d_model):
        d = d_model
        self.d = d
        orig = dict(fq=(0, FOX_W), fk=(FOX_W, FOX_W), fv=(2 * FOX_W, FOX_W), ff=(3 * FOX_W, FOX_HEADS))
        o = 3 * FOX_W + FOX_HEADS
        for name, w in (("mq", MLA_Q_LORA), ("mkv", MLA_KV_LORA), ("mkr", MLA_ROPE), ("rq", RET_QK_W),
                        ("rk", RET_QK_W), ("rv", RET_V_W), ("rg", RET_V_W), ("gates", 3 * d)):
            orig[name] = (o, w)
            o += w
        self.orig = orig
        self.orig_width = o
        order = ["rv", "rg", "mq", "rq", "rk", "mkv", "fq", "fk", "fv", "mkr", "ff"]
        self.order = order
        self.off, self.width = {}, {}
        p = 0
        for name in order:
            w = orig[name][1]
            wp = -(-w // LANES) * LANES
            self.off[name], self.width[name] = p, wp
            p += wp
        self.total = p
        self.gates = 3 * d

    def cb(self, name, block):
        assert self.off[name] % block == 0, (name, block)
        return self.off[name] // block

    def permute(self, w):
        parts = []
        for name in self.order:
            o, n = self.orig[name]
            seg = w[..., o:o + n]
            pad = self.width[name] - n
            if pad:
                seg = jnp.pad(seg, [(0, 0)] * (w.ndim - 1) + [(0, pad)])
            parts.append(seg)
        o, n = self.orig["gates"]
        return w[..., o:o + n], jnp.concatenate(parts, axis=-1)

    def unpermute(self, gates, mix):
        names = sorted(self.order, key=lambda n: self.orig[n][0])
        return jnp.concatenate([mix[..., self.off[n]:self.off[n] + self.orig[n][1]] for n in names] + [gates],
                               axis=-1)


def _mm(name, a, b, out_shape, grid, a_spec, b_spec, o_spec, dims, acc_shape, res=None, nsub=0):
    nk = grid[-1]
    has_res = res is not None

    def body(*refs):
        if has_res:
            a_ref, b_ref, r_ref, o_ref = refs[:4]
        else:
            a_ref, b_ref, o_ref = refs[:3]
            r_ref = None
        if nsub:
            prod = lax.dot_general(a_ref[0].astype(BF16), b_ref[0].astype(BF16), dims, preferred_element_type=F32)
            for q in range(1, nsub):
                prod = prod + lax.dot_general(a_ref[q].astype(BF16), b_ref[q].astype(BF16), dims,
                                              preferred_element_type=F32)
        else:
            prod = lax.dot_general(a_ref[...].astype(BF16), b_ref[...].astype(BF16), dims,
                                   preferred_element_type=F32)
        if nk == 1:
            if has_res:
                prod = prod + r_ref[...].astype(F32)
            o_ref[...] = prod.astype(o_ref.dtype)
        else:
            acc_ref = refs[-1]
            k = pl.program_id(len(grid) - 1)

            @pl.when(k == 0)
            def _():
                acc_ref[...] = prod

            @pl.when(k > 0)
            def _():
                acc_ref[...] += prod

            @pl.when(k == nk - 1)
            def _():
                r = acc_ref[...]
                if has_res:
                    r = r + r_ref[...].astype(F32)
                o_ref[...] = r.astype(o_ref.dtype)

    in_specs = [a_spec, b_spec] + ([o_spec] if has_res else [])
    args = (a, b) + ((res,) if has_res else ())
    scratch = [pltpu.VMEM(acc_shape, F32)] if nk > 1 else []
    sem = ("parallel",) * (len(grid) - 1) + ("arbitrary",)
    return pl.pallas_call(body, out_shape=out_shape, grid=grid, in_specs=in_specs, out_specs=o_spec,
                          scratch_shapes=scratch, name=name, compiler_params=_cparams(*sem))(*args)


def mm_nn(name, a, b, out_dtype, b_lead=None, res=None):
    M, K = a.shape
    N = b.shape[-1]
    tm, tn, tk = _pick(M, 1024, 8), _pick(N, 1536), _pick(K, 2048)
    grid = (M // tm, N // tn, K // tk)
    a_spec = pl.BlockSpec((tm, tk), lambda i, j, k: (i, k))
    if b_lead is None:
        b_spec = pl.BlockSpec((tk, tn), lambda i, j, k: (k, j))
    else:
        b_spec = pl.BlockSpec((None, tk, tn), lambda i, j, k: (b_lead, k, j))
    o_spec = pl.BlockSpec((tm, tn), lambda i, j, k: (i, j))
    return _mm(name, a, b, jax.ShapeDtypeStruct((M, N), out_dtype), grid, a_spec, b_spec, o_spec,
               NN_DIMS, (tm, tn), res)


def mm_nt(name, a, b, out_dtype, b_lead=None, res=None):
    M, N = a.shape
    K = b.shape[-2]
    tm, tko, tk = _pick(M, 1024, 8), _pick(K, 1024), _pick(N, 3200)
    grid = (M // tm, K // tko, N // tk)
    a_spec = pl.BlockSpec((tm, tk), lambda i, j, k: (i, k))
    if b_lead is None:
        b_spec = pl.BlockSpec((tko, tk), lambda i, j, k: (j, k))
    else:
        b_spec = pl.BlockSpec((None, tko, tk), lambda i, j, k: (b_lead, j, k))
    o_spec = pl.BlockSpec((tm, tko), lambda i, j, k: (i, j))
    return _mm(name, a, b, jax.ShapeDtypeStruct((M, K), out_dtype), grid, a_spec, b_spec, o_spec,
               NT_DIMS, (tm, tko), res)


def mm_tn(name, a, b, out_dtype):
    M, K = a.shape
    N = b.shape[-1]
    cap = 4096 if (a.dtype == BF16 and b.dtype == BF16) else 2048
    tko, tn, tk = _pick(K, 1024), _pick(N, 1280), _pick(M, cap, 8)
    grid = (K // tko, N // tn, M // tk)
    a_spec = pl.BlockSpec((tk, tko), lambda i, j, k: (k, i))
    b_spec = pl.BlockSpec((tk, tn), lambda i, j, k: (k, j))
    o_spec = pl.BlockSpec((tko, tn), lambda i, j, k: (i, j))
    return _mm(name, a, b, jax.ShapeDtypeStruct((K, N), out_dtype), grid, a_spec, b_spec, o_spec,
               TN_DIMS, (tko, tn))


FFN_SUB = 4
def ffn_up(name, h, w, l, out_dtype):
    M, D = h.shape
    f = w.shape[-1]
    tm = _pick(M, 2048, 8)
    grid = (M // tm, N_DEV, 1)
    return _mm(name, h, w, jax.ShapeDtypeStruct((N_DEV, M, f), out_dtype), grid,
               pl.BlockSpec((tm, D), lambda i, j, k: (i, 0)),
               pl.BlockSpec((None, None, D, f), lambda i, j, k: (l, j, 0, 0)),
               pl.BlockSpec((None, tm, f), lambda i, j, k: (j, i, 0)), NN_DIMS, (tm, f))


def ffn_down(name, act, w, l, res, out_dtype):
    _, M, f = act.shape
    D = w.shape[-1]
    tm, tn = _pick(M, 1024, 8), _pick(D, 1024)
    grid = (M // tm, D // tn, N_DEV // FFN_SUB)
    return _mm(name, act, w, jax.ShapeDtypeStruct((M, D), out_dtype), grid,
               pl.BlockSpec((FFN_SUB, tm, f), lambda i, j, k: (k, i, 0)),
               pl.BlockSpec((None, FFN_SUB, f, tn), lambda i, j, k: (l, k, 0, j)),
               pl.BlockSpec((tm, tn), lambda i, j, k: (i, j)), NN_DIMS, (tm, tn), res, nsub=FFN_SUB)


def ffn_down_bwd_act(name, dy, w, l, out_dtype):
    M, D = dy.shape
    f = w.shape[-2]
    tm = _pick(M, 2048, 8)
    grid = (M // tm, N_DEV, 1)
    return _mm(name, dy, w, jax.ShapeDtypeStruct((N_DEV, M, f), out_dtype), grid,
               pl.BlockSpec((tm, D), lambda i, j, k: (i, 0)),
               pl.BlockSpec((None, None, f, D), lambda i, j, k: (l, j, 0, 0)),
               pl.BlockSpec((None, tm, f), lambda i, j, k: (j, i, 0)), NT_DIMS, (tm, f))


def ffn_down_bwd_w(name, act, dy, out_dtype):
    _, M, f = act.shape
    D = dy.shape[-1]
    tn, tk = _pick(D, 1024), _pick(M, 4096 if dy.dtype == BF16 else 2048, 8)
    grid = (N_DEV, D // tn, M // tk)
    return _mm(name, act, dy, jax.ShapeDtypeStruct((N_DEV, f, D), out_dtype), grid,
               pl.BlockSpec((None, tk, f), lambda j, n, k: (j, k, 0)),
               pl.BlockSpec((tk, tn), lambda j, n, k: (k, n)),
               pl.BlockSpec((None, f, tn), lambda j, n, k: (j, 0, n)), TN_DIMS, (f, tn))


def ffn_up_bwd_h(name, du, w, l, res, out_dtype):
    _, M, f = du.shape
    D = w.shape[-2]
    tm, tn = _pick(M, 1024, 8), _pick(D, 1024)
    grid = (M // tm, D // tn, N_DEV // FFN_SUB)
    return _mm(name, du, w, jax.ShapeDtypeStruct((M, D), out_dtype), grid,
               pl.BlockSpec((FFN_SUB, tm, f), lambda i, j, k: (k, i, 0)),
               pl.BlockSpec((None, FFN_SUB, tn, f), lambda i, j, k: (l, k, j, 0)),
               pl.BlockSpec((tm, tn), lambda i, j, k: (i, j)), NT_DIMS, (tm, tn), res, nsub=FFN_SUB)


def _rms(xf, g):
    return xf * lax.rsqrt(jnp.mean(xf * xf, axis=-1, keepdims=True) + NORM_EPS) * g


def rms_fwd(name, x, cb, W, g, out_dtype):
    S = x.shape[0]
    tr = _pick(S, 256, 8)

    def body(x_ref, g_ref, o_ref):
        o_ref[...] = _rms(x_ref[...].astype(F32), g_ref[...]).astype(o_ref.dtype)

    return pl.pallas_call(
        body, out_shape=jax.ShapeDtypeStruct((S, W), out_dtype), grid=(S // tr,),
        in_specs=[pl.BlockSpec((tr, W), lambda i: (i, cb)), pl.BlockSpec((1, W), lambda i: (0, 0))],
        out_specs=pl.BlockSpec((tr, W), lambda i: (i, 0)), name=name, compiler_params=_cparams("parallel"))(x, g)


def rms_bwd(name, x, cb, W, g, dy, out_dtype, res=None):
    S = x.shape[0]
    tr = _pick(S, 256, 8)
    has_res = res is not None

    def body(*refs):
        if has_res:
            x_ref, g_ref, dy_ref, r_ref, dx_ref, dg_ref = refs
        else:
            x_ref, g_ref, dy_ref, dx_ref, dg_ref = refs
        _, vjp = jax.vjp(_rms, x_ref[...].astype(F32), g_ref[...])
        dx, dg = vjp(dy_ref[...].astype(F32))
        if has_res:
            dx = dx + r_ref[...]
        dx_ref[...] = dx.astype(dx_ref.dtype)

        @pl.when(pl.program_id(0) == 0)
        def _():
            dg_ref[...] = jnp.zeros_like(dg_ref)

        dg_ref[...] += dg

    row = pl.BlockSpec((tr, W), lambda i: (i, 0))
    vec = pl.BlockSpec((1, W), lambda i: (0, 0))
    in_specs = [pl.BlockSpec((tr, W), lambda i: (i, cb)), vec, row] + ([row] if has_res else [])
    args = (x, g, dy) + ((res,) if has_res else ())
    return pl.pallas_call(
        body, out_shape=(jax.ShapeDtypeStruct((S, W), out_dtype), jax.ShapeDtypeStruct((1, W), F32)),
        grid=(S // tr,), in_specs=in_specs, out_specs=(row, vec), name=name,
        compiler_params=_cparams("arbitrary"))(*args)


def rope_tables(S, d):
    pos = jnp.arange(S, dtype=F32)
    inv_freq = ROPE_THETA ** (-jnp.arange(0, d, 2, dtype=F32) / d)
    ang = pos[:, None] * inv_freq[None, :]
    cos, sin = jnp.cos(ang), jnp.sin(ang)
    half = d // 2
    z = jnp.zeros((S, LANES - d), F32)
    zh = jnp.zeros((S, half), F32)
    c = jnp.concatenate([cos, cos, z], axis=1)
    sa = jnp.concatenate([-sin, zh, z], axis=1)
    sb = jnp.concatenate([zh, sin, z], axis=1)
    return c, sa, sb, half


def rope_apply(name, x, cb, H, tabs, scale, out_dtype, transpose=False):
    c, sa, sb, half = tabs
    S = x.shape[0]
    tr = _pick(S, 512, 8)
    up, down = LANES - half, half

    def body(x_ref, c_ref, sa_ref, sb_ref, o_ref):
        xv = x_ref[...].astype(F32)
        if not transpose:
            y = xv * c_ref[...] + pltpu.roll(xv, up, 1) * sa_ref[...] + pltpu.roll(xv, down, 1) * sb_ref[...]
            y = y * scale
        else:
            xv = xv * scale
            y = (xv * c_ref[...] + pltpu.roll(xv * sa_ref[...], down, 1)
                 + pltpu.roll(xv * sb_ref[...], up, 1))
        o_ref[...] = y.astype(o_ref.dtype)

    tab = pl.BlockSpec((tr, LANES), lambda h, i: (i, 0))
    return pl.pallas_call(
        body, out_shape=jax.ShapeDtypeStruct((S, H * LANES), out_dtype), grid=(H, S // tr),
        in_specs=[pl.BlockSpec((tr, LANES), lambda h, i: (i, cb + h)), tab, tab, tab],
        out_specs=pl.BlockSpec((tr, LANES), lambda h, i: (i, h)), name=name,
        compiler_params=_cparams("parallel", "parallel"))(x, c, sa, sb)


def _ret_out(o, g):
    y = o * lax.rsqrt(jnp.mean(o * o, axis=-1, keepdims=True) + NORM_EPS)
    return y * jax.nn.silu(g)


def ret_out_fwd(name, o, gsrc, g_cb, out_dtype):
    S = o.shape[0]
    tr = _pick(S, 512, 8)
    W = RET_DV

    def body(o_ref, g_ref, y_ref):
        y_ref[...] = _ret_out(o_ref[...], g_ref[...].astype(F32)).astype(y_ref.dtype)

    blk = pl.BlockSpec((tr, W), lambda h, i: (i, h))
    return pl.pallas_call(
        body, out_shape=jax.ShapeDtypeStruct((S, RET_HEADS * W), out_dtype), grid=(RET_HEADS, S // tr),
        in_specs=[blk, pl.BlockSpec((tr, W), lambda h, i: (i, g_cb + h))], out_specs=blk, name=name,
        compiler_params=_cparams("parallel", "parallel"))(o, gsrc)


def ret_out_bwd(name, o, gsrc, g_cb, dy):
    S = o.shape[0]
    tr = _pick(S, 512, 8)
    W = RET_DV

    def body(o_ref, g_ref, dy_ref, do_ref, dg_ref):
        _, vjp = jax.vjp(_ret_out, o_ref[...], g_ref[...].astype(F32))
        do, dg = vjp(dy_ref[...].astype(F32))
        do_ref[...] = do.astype(do_ref.dtype)
        dg_ref[...] = dg.astype(dg_ref.dtype)

    blk = pl.BlockSpec((tr, W), lambda h, i: (i, h))
    return pl.pallas_call(
        body, out_shape=(jax.ShapeDtypeStruct((S, RET_HEADS * W), F32),
                         jax.ShapeDtypeStruct((S, RET_HEADS * W), BF16)),
        grid=(RET_HEADS, S // tr),
        in_specs=[blk, pl.BlockSpec((tr, W), lambda h, i: (i, g_cb + h)), blk], out_specs=(blk, blk),
        name=name, compiler_params=_cparams("parallel", "parallel"))(o, gsrc, dy)


def _merge(g0, g1, g2, a, b, c):
    return jax.nn.sigmoid(g0) * a + jax.nn.sigmoid(g1) * b + jax.nn.sigmoid(g2) * c


def merge_fwd(name, P, gates_cb, a, b, c, out_dtype):
    S, D = a.shape
    tr = _pick(S, 128, 8)

    def body(g0, g1, g2, a_ref, b_ref, c_ref, o_ref):
        o_ref[...] = _merge(g0[...], g1[...], g2[...], a_ref[...], b_ref[...], c_ref[...]).astype(o_ref.dtype)

    row = pl.BlockSpec((tr, D), lambda i: (i, 0))
    gs = [pl.BlockSpec((tr, D), lambda i, k=k: (i, gates_cb + k)) for k in range(3)]
    return pl.pallas_call(
        body, out_shape=jax.ShapeDtypeStruct((S, D), out_dtype), grid=(S // tr,),
        in_specs=gs + [row, row, row], out_specs=row, name=name,
        compiler_params=_cparams("parallel"))(P, P, P, a, b, c)


def merge_bwd(name, P, gates_cb, a, b, c, dm):
    S, D = a.shape
    tr = _pick(S, 128, 8)

    def body(g0, g1, g2, a_ref, b_ref, c_ref, dm_ref, dg_ref, da_ref, db_ref, dc_ref):
        _, vjp = jax.vjp(_merge, g0[...], g1[...], g2[...], a_ref[...], b_ref[...], c_ref[...])
        d0, d1, d2, da, db, dc = vjp(dm_ref[...].astype(F32))
        dg_ref[:, 0:D] = d0.astype(dg_ref.dtype)
        dg_ref[:, D:2 * D] = d1.astype(dg_ref.dtype)
        dg_ref[:, 2 * D:3 * D] = d2.astype(dg_ref.dtype)
        da_ref[...] = da.astype(da_ref.dtype)
        db_ref[...] = db.astype(db_ref.dtype)
        dc_ref[...] = dc.astype(dc_ref.dtype)

    row = pl.BlockSpec((tr, D), lambda i: (i, 0))
    gs = [pl.BlockSpec((tr, D), lambda i, k=k: (i, gates_cb + k)) for k in range(3)]
    bf = jax.ShapeDtypeStruct((S, D), BF16)
    return pl.pallas_call(
        body, out_shape=(jax.ShapeDtypeStruct((S, 3 * D), BF16), bf, bf, bf), grid=(S // tr,),
        in_specs=gs + [row, row, row, row],
        out_specs=(pl.BlockSpec((tr, 3 * D), lambda i: (i, 0)), row, row, row), name=name,
        compiler_params=_cparams("parallel"))(P, P, P, a, b, c, dm)


HALO = 8


CONV_CHUNK = 32


def _shifted_back(u_ref, uh_ref, ext_ref, s1_ref, s2_ref, tr):
    ext_ref[0:HALO, :] = jnp.where(pl.program_id(1) > 0, uh_ref[...], 0.0)
    ext_ref[HALO:HALO + tr, :] = u_ref[...]
    s1_ref[...] = ext_ref[HALO - 1:HALO - 1 + tr, :]
    s2_ref[...] = ext_ref[HALO - 2:HALO - 2 + tr, :]


def _conv3(cw, cb, u, u1, u2):
    return cb + ((cw[0:1, :] * u2 + cw[1:2, :] * u1) + cw[2:3, :] * u)


def _chunks(tr, fn):
    def step(c, carry):
        return fn(pl.ds(pl.multiple_of(c * CONV_CHUNK, CONV_CHUNK), CONV_CHUNK), carry)
    return step


def _ffn_specs(S, f, tr, l):
    nb = tr // HALO
    row = pl.BlockSpec((None, tr, f), lambda j, i: (j, i, 0))
    prev = pl.BlockSpec((None, HALO, f), lambda j, i: (j, jnp.maximum(i * nb - 1, 0), 0))
    nxt = pl.BlockSpec((None, HALO, f), lambda j, i: (j, jnp.minimum((i + 1) * nb, S // HALO - 1), 0))
    cw = pl.BlockSpec((None, None, 3, f), lambda j, i: (l, j, 0, 0))
    cb = pl.BlockSpec((None, None, 1, f), lambda j, i: (l, j, 0, 0))
    return row, prev, nxt, cw, cb


def ffn_act_fwd(name, u, gt, cw, cb, l, out_dtype):
    _, S, f = u.shape
    tr = _pick(S, 512, 8)
    row, prev, _, cws, cbs = _ffn_specs(S, f, tr, l)

    def body(u_ref, uh_ref, gt_ref, cw_ref, cb_ref, o_ref, ext_ref, s1_ref, s2_ref):
        _shifted_back(u_ref, uh_ref, ext_ref, s1_ref, s2_ref, tr)
        cwv, cbv = cw_ref[...], cb_ref[...]

        def chunk(rows, carry):
            uc = _conv3(cwv, cbv, u_ref[rows, :], s1_ref[rows, :], s2_ref[rows, :])
            o_ref[rows, :] = (jax.nn.gelu(uc) * gt_ref[rows, :]).astype(o_ref.dtype)
            return carry

        lax.fori_loop(0, tr // CONV_CHUNK, _chunks(tr, chunk), 0)

    return pl.pallas_call(
        body, out_shape=jax.ShapeDtypeStruct((N_DEV, S, f), out_dtype), grid=(N_DEV, S // tr),
        in_specs=[row, prev, row, cws, cbs], out_specs=row,
        scratch_shapes=[pltpu.VMEM((tr + HALO, f), F32), pltpu.VMEM((tr, f), F32), pltpu.VMEM((tr, f), F32)],
        name=name, compiler_params=_cparams("parallel", "parallel"))(u, u, gt, cw, cb)


def ffn_act_bwd_point(name, u, gt, cw, cb, l, dact):
    _, S, f = u.shape
    tr = _pick(S, 512, 8)
    row, prev, _, cws, cbs = _ffn_specs(S, f, tr, l)

    def body(u_ref, uh_ref, gt_ref, cw_ref, cb_ref, da_ref, g_ref, dgt_ref, ext_ref, s1_ref, s2_ref):
        _shifted_back(u_ref, uh_ref, ext_ref, s1_ref, s2_ref, tr)
        cwv, cbv = cw_ref[...], cb_ref[...]

        def chunk(rows, carry):
            uc = _conv3(cwv, cbv, u_ref[rows, :], s1_ref[rows, :], s2_ref[rows, :])
            _, vjp = jax.vjp(lambda c, t: jax.nn.gelu(c) * t, uc, gt_ref[rows, :])
            g, dgt = vjp(da_ref[rows, :].astype(F32))
            g_ref[rows, :] = g
            dgt_ref[rows, :] = dgt.astype(dgt_ref.dtype)
            return carry

        lax.fori_loop(0, tr // CONV_CHUNK, _chunks(tr, chunk), 0)

    return pl.pallas_call(
        body, out_shape=(jax.ShapeDtypeStruct((N_DEV, S, f), F32), jax.ShapeDtypeStruct((N_DEV, S, f), BF16)),
        grid=(N_DEV, S // tr), in_specs=[row, prev, row, cws, cbs, row], out_specs=(row, row),
        scratch_shapes=[pltpu.VMEM((tr + HALO, f), F32), pltpu.VMEM((tr, f), F32), pltpu.VMEM((tr, f), F32)],
        name=name, compiler_params=_cparams("parallel", "parallel"))(u, u, gt, cw, cb, dact)


def ffn_act_bwd_conv(name, u, g, cw, l):
    _, S, f = u.shape
    tr = _pick(S, 512, 8)
    nt = S // tr
    row, prev, nxt, cws, _ = _ffn_specs(S, f, tr, l)

    def body(u_ref, uh_ref, g_ref, gn_ref, cw_ref, du_ref, dcw_ref, dcb_ref, ext_ref, s1_ref, s2_ref, n1_ref, n2_ref):
        i = pl.program_id(1)
        _shifted_back(u_ref, uh_ref, ext_ref, s1_ref, s2_ref, tr)
        ext_ref[0:tr, :] = g_ref[...]
        ext_ref[tr:tr + HALO, :] = jnp.where(i < nt - 1, gn_ref[...], 0.0)
        n1_ref[...] = ext_ref[1:1 + tr, :]
        n2_ref[...] = ext_ref[2:2 + tr, :]
        cw = cw_ref[...]

        def chunk(rows, carry):
            d0, d1, d2, db = carry
            g = g_ref[rows, :]
            du_ref[rows, :] = (cw[2:3, :] * g + cw[1:2, :] * n1_ref[rows, :]
                               + cw[0:1, :] * n2_ref[rows, :]).astype(du_ref.dtype)
            return (d0 + jnp.sum(g * s2_ref[rows, :], axis=0, keepdims=True),
                    d1 + jnp.sum(g * s1_ref[rows, :], axis=0, keepdims=True),
                    d2 + jnp.sum(g * u_ref[rows, :], axis=0, keepdims=True),
                    db + jnp.sum(g, axis=0, keepdims=True))

        z = jnp.zeros((1, f), F32)
        d0, d1, d2, db = lax.fori_loop(0, tr // CONV_CHUNK, _chunks(tr, chunk), (z, z, z, z))

        @pl.when(i == 0)
        def _():
            dcw_ref[...] = jnp.zeros_like(dcw_ref)
            dcb_ref[...] = jnp.zeros_like(dcb_ref)

        dcw_ref[0:1, :] += d0
        dcw_ref[1:2, :] += d1
        dcw_ref[2:3, :] += d2
        dcb_ref[...] += db

    tile = pltpu.VMEM((tr, f), F32)
    return pl.pallas_call(
        body, out_shape=(jax.ShapeDtypeStruct((N_DEV, S, f), BF16), jax.ShapeDtypeStruct((N_DEV, 3, f), F32),
                         jax.ShapeDtypeStruct((N_DEV, 1, f), F32)),
        grid=(N_DEV, nt), in_specs=[row, prev, row, nxt, cws],
        out_specs=(row, pl.BlockSpec((None, 3, f), lambda j, i: (j, 0, 0)),
                   pl.BlockSpec((None, 1, f), lambda j, i: (j, 0, 0))),
        scratch_shapes=[pltpu.VMEM((tr + HALO, f), F32), tile, tile, tile, tile], name=name,
        compiler_params=_cparams("parallel", "arbitrary"))(u, u, g, g, cw)


def loss_head(name, x, g, tgt):
    S, D = x.shape
    tr = _pick(S, 256, 8)

    def body(x_ref, g_ref, t_ref, l_ref, dx_ref, dg_ref):
        tg = t_ref[...]

        def f(xv, gv):
            err = jnp.square(_rms(xv, gv) - tg)
            return 0.5 * jnp.sum(jnp.mean(err, axis=-1))

        val, vjp = jax.vjp(f, x_ref[...], g_ref[...])
        dx, dg = vjp(jnp.ones((), F32))
        dx_ref[...] = dx

        @pl.when(pl.program_id(0) == 0)
        def _():
            l_ref[...] = jnp.zeros_like(l_ref)
            dg_ref[...] = jnp.zeros_like(dg_ref)

        l_ref[...] += val
        dg_ref[...] += dg

    row = pl.BlockSpec((tr, D), lambda i: (i, 0))
    vec = pl.BlockSpec((1, D), lambda i: (0, 0))
    lt = pl.BlockSpec((8, LANES), lambda i: (0, 0))
    return pl.pallas_call(
        body, out_shape=(jax.ShapeDtypeStruct((8, LANES), F32), jax.ShapeDtypeStruct((S, D), F32),
                         jax.ShapeDtypeStruct((1, D), F32)),
        grid=(S // tr,), in_specs=[row, vec, row], out_specs=(lt, row, vec), name=name,
        compiler_params=_cparams("arbitrary"))(x, g, tgt)


def _tri(n, fn):
    r = lax.broadcasted_iota(jnp.int32, (n, n), 0)
    c = lax.broadcasted_iota(jnp.int32, (n, n), 1)
    return jnp.where(fn(r, c), 1.0, 0.0).astype(F32)


def _log_sigmoid(z):
    return jnp.minimum(z, 0.0) - jnp.log1p(jnp.exp(-jnp.abs(z)))


def fox_gate_fwd(name, ft, b):
    H, R, _ = ft.shape

    def body(f_ref, b_ref, o_ref):
        ls = _log_sigmoid(f_ref[...] + b_ref[...])
        cum = jnp.dot(ls, _tri(LANES, lambda r, c: r <= c), precision=HIGHEST, preferred_element_type=F32)
        tot = jnp.broadcast_to(cum[:, LANES - 1:LANES], (R, LANES))
        off = jnp.dot(_tri(R, lambda r, c: r > c), tot, precision=HIGHEST, preferred_element_type=F32)
        o_ref[...] = -(cum + off)

    blk = pl.BlockSpec((None, R, LANES), lambda h: (h, 0, 0))
    return pl.pallas_call(
        body, out_shape=jax.ShapeDtypeStruct((H, R, LANES), F32), grid=(H,),
        in_specs=[blk, pl.BlockSpec((None, 1, LANES), lambda h: (h, 0, 0))], out_specs=blk, name=name,
        compiler_params=_cparams("parallel"))(ft, b)


def fox_gate_bwd(name, ft, b, dkb):
    H, R, _ = ft.shape

    def body(f_ref, b_ref, d_ref, df_ref, db_ref):
        z = f_ref[...] + b_ref[...]
        d = d_ref[...]
        rev = jnp.dot(d, _tri(LANES, lambda r, c: r >= c), precision=HIGHEST, preferred_element_type=F32)
        tot = jnp.broadcast_to(rev[:, 0:1], (R, LANES))
        off = jnp.dot(_tri(R, lambda r, c: r < c), tot, precision=HIGHEST, preferred_element_type=F32)
        dls = -(rev + off)
        dz = dls * jax.nn.sigmoid(-z)
        df_ref[...] = dz
        s = jnp.sum(jnp.sum(dz, axis=1, keepdims=True), axis=0, keepdims=True)
        db_ref[...] = jnp.broadcast_to(s, (1, LANES))

    blk = pl.BlockSpec((None, R, LANES), lambda h: (h, 0, 0))
    vec = pl.BlockSpec((None, 1, LANES), lambda h: (h, 0, 0))
    return pl.pallas_call(
        body, out_shape=(jax.ShapeDtypeStruct((H, R, LANES), F32), jax.ShapeDtypeStruct((H, 1, LANES), F32)),
        grid=(H,), in_specs=[blk, vec, blk], out_specs=(blk, vec), name=name,
        compiler_params=_cparams("parallel"))(ft, b, dkb)


def _ret_log_gamma(h):
    lg = [float(np.log(np.float32(1.0) - np.float32(2.0) ** np.float32(-5.0 - i))) for i in range(RET_HEADS)]
    out = jnp.float32(lg[RET_HEADS - 1])
    for i in range(RET_HEADS - 2, -1, -1):
        out = jnp.where(h == i, jnp.float32(lg[i]), out)
    return out


def _visible(mode, B):
    r = lax.broadcasted_iota(jnp.int32, (B, B), 0)
    c = lax.broadcasted_iota(jnp.int32, (B, B), 1)
    if mode == "fox":
        return c <= r
    return (c // CHUNK) <= (r // CHUNK)


def _visible_t(mode, B):
    k = lax.broadcasted_iota(jnp.int32, (B, B), 0)
    q = lax.broadcasted_iota(jnp.int32, (B, B), 1)
    if mode == "fox":
        return k <= q
    return (k // CHUNK) <= (q // CHUNK)


def _decay(lg, B, blocks_apart):
    r = lax.broadcasted_iota(jnp.int32, (B, B), 0)
    c = lax.broadcasted_iota(jnp.int32, (B, B), 1)
    dist = jnp.abs(r - c + blocks_apart * B).astype(F32)
    return jnp.exp(lg * dist)


def _attn_block(S):
    return 512 if S >= 2048 else 128


def attn_fwd(name, mode, q1, q1_cb, k1, k1_cb, v, v_cb, H, dv, scale, q2=None, q2_cb=0, k2=None, kbias=None):
    S = q1.shape[0]
    B = _attn_block(S)
    nq = S // B
    softmax = mode != "ret"
    two = mode == "mla"
    has_bias = mode == "fox"

    def body(*refs):
        it = iter(refs)
        q1_ref, k1_ref, v_ref = next(it), next(it), next(it)
        q2_ref = next(it) if two else None
        k2_ref = next(it) if two else None
        kb_ref = next(it) if has_bias else None
        o_ref = next(it)
        lse_ref = next(it) if softmax else None
        kbuf, vT = next(it), next(it)
        acc = next(it)
        m_ref = next(it) if softmax else None
        l_ref = next(it) if softmax else None
        s_all = next(it) if softmax else None
        kcol = next(it) if has_bias else None
        h = pl.program_id(0)
        i = pl.program_id(1)

        @pl.when(i == 0)
        def _():
            kbuf[:, 0:LANES] = k1_ref[...].astype(BF16)
            vT[...] = v_ref[...].astype(F32).T.astype(BF16)
            if two:
                kbuf[:, LANES:2 * LANES] = k2_ref[...].astype(BF16)
            if has_bias:
                for g in range(nq):
                    kcol[g * B:(g + 1) * B, :] = jnp.broadcast_to(kb_ref[g], (LANES, B)).T[:, 0:1]

        qb = q1_ref[...].astype(BF16)
        if two:
            qb = jnp.concatenate([qb, q2_ref[...].astype(BF16)], axis=1)
        lg = _ret_log_gamma(h) if mode == "ret" else None
        acc[...] = jnp.zeros_like(acc)
        if softmax:
            m_ref[...] = jnp.full_like(m_ref, NEG_BIG)
            l_ref[...] = jnp.zeros_like(l_ref)

        def scores(g, diag):
            rows = slice(g * B, (g + 1) * B)
            s = lax.dot_general(kbuf[rows, :], qb, NT_DIMS, preferred_element_type=F32)
            if softmax:
                s = s * scale
                if has_bias:
                    s = s + kcol[rows, :]
                if diag:
                    s = jnp.where(_visible_t(mode, B), s, NEG_BIG)
                s_all[rows, :] = s
                m_ref[...] = jnp.maximum(m_ref[...], jnp.max(s, axis=0, keepdims=True))
            else:
                if diag:
                    p = jnp.where(_visible_t(mode, B), s * _decay(lg, B, 0), 0.0)
                else:
                    p = s * _decay(lg, B, g - i)
                acc[...] += jnp.dot(vT[:, rows], p.astype(BF16), preferred_element_type=F32)

        def weighted(g):
            rows = slice(g * B, (g + 1) * B)
            p = jnp.exp(s_all[rows, :] - m_ref[...])
            l_ref[...] += jnp.sum(p, axis=0, keepdims=True)
            acc[...] += jnp.dot(vT[:, rows], p.astype(BF16), preferred_element_type=F32)

        for g in range(nq):
            pl.when(g < i)(functools.partial(scores, g, False))
            pl.when(g == i)(functools.partial(scores, g, True))
        if softmax:
            for g in range(nq):
                pl.when(g <= i)(functools.partial(weighted, g))
            o_ref[...] = (acc[...] / l_ref[...]).T
            lse_ref[...] = jnp.broadcast_to(m_ref[...] + jnp.log(l_ref[...]), (LANES, B)).T
        else:
            o_ref[...] = acc[...].T

    in_specs = [pl.BlockSpec((B, LANES), lambda h, i: (i, q1_cb + h)),
                pl.BlockSpec((S, LANES), lambda h, i: (0, k1_cb + h)),
                pl.BlockSpec((S, dv), lambda h, i: (0, v_cb + h))]
    args = [q1, k1, v]
    if two:
        in_specs += [pl.BlockSpec((B, LANES), lambda h, i: (i, q2_cb + h)),
                     pl.BlockSpec((S, LANES), lambda h, i: (0, 0))]
        args += [q2, k2]
    if has_bias:
        in_specs.append(pl.BlockSpec((None, nq, 1, B), lambda h, i: (h, 0, 0, 0)))
        args.append(kbias)
    out_shape = [jax.ShapeDtypeStruct((S, H * dv), F32)]
    out_specs = [pl.BlockSpec((B, dv), lambda h, i: (i, h))]
    if softmax:
        out_shape.append(jax.ShapeDtypeStruct((S, H * LANES), F32))
        out_specs.append(pl.BlockSpec((B, LANES), lambda h, i: (i, h)))
    kw = 2 * LANES if two else LANES
    scratch = [pltpu.VMEM((S, kw), BF16), pltpu.VMEM((dv, S), BF16), pltpu.VMEM((dv, B), F32)]
    if softmax:
        scratch += [pltpu.VMEM((1, B), F32), pltpu.VMEM((1, B), F32), pltpu.VMEM((S, B), F32)]
    if has_bias:
        scratch.append(pltpu.VMEM((S, 1), F32))
    res = pl.pallas_call(body, out_shape=tuple(out_shape), grid=(H, nq), in_specs=in_specs,
                         out_specs=tuple(out_specs), scratch_shapes=scratch, name=name,
                         compiler_params=_cparams("parallel", "arbitrary"))(*args)
    return res if softmax else (res[0], None)


def attn_bwd(name, mode, q1, q1_cb, k1, k1_cb, v, v_cb, H, dv, scale, do, o=None, lse=None,
             q2=None, q2_cb=0, k2=None, kbias=None):
    S = q1.shape[0]
    B = _attn_block(S)
    nb = S // B
    softmax = mode != "ret"
    two = mode == "mla"
    has_bias = mode == "fox"

    def body(*refs):
        it = iter(refs)
        q1_ref, k1_ref, v_ref, do_ref = next(it), next(it), next(it), next(it)
        o_ref = next(it) if softmax else None
        lse_ref = next(it) if softmax else None
        q2_ref = next(it) if two else None
        k2_ref = next(it) if two else None
        kb_ref = next(it) if has_bias else None
        dq1_ref, dk1_ref, dv_ref = next(it), next(it), next(it)
        dq2_ref = next(it) if two else None
        dk2_ref = next(it) if two else None
        dkb_ref = next(it) if has_bias else None
        drow_ref = next(it) if has_bias else None
        qbuf, dobuf = next(it), next(it)
        qT, doT = next(it), next(it)
        delta = next(it) if softmax else None
        dk_acc, dv_acc = next(it), next(it)
        dkb_acc = next(it) if has_bias else None
        h = pl.program_id(0)
        j = pl.program_id(1)

        @pl.when(j == 0)
        def _():
            qbuf[:, 0:LANES] = q1_ref[...].astype(BF16)
            dobuf[...] = do_ref[...].astype(BF16)
            qT[0:LANES, :] = q1_ref[...].astype(F32).T.astype(BF16)
            doT[...] = do_ref[...].astype(F32).T.astype(BF16)
            dq1_ref[...] = jnp.zeros_like(dq1_ref)
            if has_bias:
                drow_ref[...] = jnp.zeros_like(drow_ref)
            if two:
                qbuf[:, LANES:2 * LANES] = q2_ref[...].astype(BF16)
                qT[LANES:2 * LANES, :] = q2_ref[...].astype(F32).T.astype(BF16)
                dq2_ref[...] = jnp.zeros_like(dq2_ref)
            if softmax:
                def drow(t, carry):
                    rows = pl.ds(pl.multiple_of(t * B, B), B)
                    d = jnp.sum(do_ref[rows, :].astype(F32) * o_ref[rows, :], axis=1, keepdims=True)
                    delta[rows, :] = jnp.broadcast_to(d, (B, LANES))
                    return carry
                lax.fori_loop(0, nb, drow, 0)

        kj = k1_ref[...].astype(BF16)
        if two:
            kj = jnp.concatenate([kj, k2_ref[...].astype(BF16)], axis=1)
        vj = v_ref[...].astype(BF16)
        kbj = kb_ref[...] if has_bias else None
        lg = _ret_log_gamma(h) if mode == "ret" else None
        dk_acc[...] = jnp.zeros_like(dk_acc)
        dv_acc[...] = jnp.zeros_like(dv_acc)
        if has_bias:
            dkb_acc[...] = jnp.zeros_like(dkb_acc)

        def step(i, diag):
            rows = slice(i * B, (i + 1) * B)
            qi = qbuf[rows, :]
            doi = dobuf[rows, :]
            s = lax.dot_general(qi, kj, NT_DIMS, preferred_element_type=F32)
            dp = lax.dot_general(doi, vj, NT_DIMS, preferred_element_type=F32)
            if softmax:
                s = s * scale
                if has_bias:
                    s = s + kbj
                if diag:
                    s = jnp.where(_visible(mode, B), s, NEG_BIG)
                p = jnp.exp(s - jnp.tile(lse_ref[rows, :], (1, B // LANES)))
                ds = p * (dp - jnp.tile(delta[rows, :], (1, B // LANES)))
                if has_bias:
                    dkb_acc[...] += jnp.sum(ds, axis=0, keepdims=True)
                    drow_ref[rows, :] += jnp.broadcast_to(jnp.sum(ds, axis=1, keepdims=True), (B, LANES))
                dsb = (ds * scale).astype(BF16)
            else:
                if diag:
                    dec = jnp.where(_visible(mode, B), _decay(lg, B, 0), 0.0)
                else:
                    dec = _decay(lg, B, i - j)
                p = s * dec
                dsb = (dp * dec).astype(BF16)
            dv_acc[...] += jnp.dot(doT[:, rows], p.astype(BF16), preferred_element_type=F32)
            dk_acc[...] += jnp.dot(qT[:, rows], dsb, preferred_element_type=F32)
            dq = jnp.dot(dsb, kj, preferred_element_type=F32)
            dq1_ref[rows, :] += dq[:, 0:LANES]
            if two:
                dq2_ref[rows, :] += dq[:, LANES:2 * LANES]

        for i in range(nb):
            pl.when(i == j)(functools.partial(step, i, True))
            pl.when(i > j)(functools.partial(step, i, False))
        dk1_ref[...] = dk_acc[0:LANES, :].T
        dv_ref[...] = dv_acc[...].T
        if two:
            dk2_ref[...] = dk_acc[LANES:2 * LANES, :].T
        if has_bias:
            dkb_ref[...] = dkb_acc[...]

    full = lambda w, cb: pl.BlockSpec((S, w), lambda h, j: (0, cb + h))
    blk = lambda w, cb: pl.BlockSpec((B, w), lambda h, j: (j, cb + h))
    in_specs = [full(LANES, q1_cb), blk(LANES, k1_cb), blk(dv, v_cb), full(dv, 0)]
    args = [q1, k1, v, do]
    if softmax:
        in_specs += [full(dv, 0), full(LANES, 0)]
        args += [o, lse]
    if two:
        in_specs += [full(LANES, q2_cb), pl.BlockSpec((B, LANES), lambda h, j: (j, 0))]
        args += [q2, k2]
    if has_bias:
        in_specs.append(pl.BlockSpec((None, None, 1, B), lambda h, j: (h, j, 0, 0)))
        args.append(kbias)
    names = ["dq1", "dk1", "dv"]
    out_shape = [jax.ShapeDtypeStruct((S, H * LANES), F32), jax.ShapeDtypeStruct((S, H * LANES), F32),
                 jax.ShapeDtypeStruct((S, H * dv), F32)]
    out_specs = [full(LANES, 0), blk(LANES, 0), blk(dv, 0)]
    if two:
        names += ["dq2", "dk2h"]
        out_shape += [jax.ShapeDtypeStruct((S, H * LANES), F32)] * 2
        out_specs += [full(LANES, 0), blk(LANES, 0)]
    if has_bias:
        names.append("dkb")
        out_shape.append(jax.ShapeDtypeStruct((H, nb, 1, B), F32))
        out_specs.append(pl.BlockSpec((None, None, 1, B), lambda h, j: (h, j, 0, 0)))
        names.append("drow")
        out_shape.append(jax.ShapeDtypeStruct((S, H * LANES), F32))
        out_specs.append(full(LANES, 0))
    kw = 2 * LANES if two else LANES
    scratch = [pltpu.VMEM((S, kw), BF16), pltpu.VMEM((S, dv), BF16),
               pltpu.VMEM((kw, S), BF16), pltpu.VMEM((dv, S), BF16)]
    if softmax:
        scratch.append(pltpu.VMEM((S, LANES), F32))
    scratch += [pltpu.VMEM((kw, B), F32), pltpu.VMEM((dv, B), F32)]
    if has_bias:
        scratch.append(pltpu.VMEM((1, B), F32))
    res = pl.pallas_call(body, out_shape=tuple(out_shape), grid=(H, nb), in_specs=in_specs,
                         out_specs=tuple(out_specs), scratch_shapes=scratch, name=name,
                         compiler_params=_cparams("parallel", "arbitrary"))(*args)
    return dict(zip(names, res))


def head_sum(name, x, H, out_dtype):
    S = x.shape[0]
    tr = _pick(S, 512, 8)

    def body(x_ref, o_ref):
        acc = x_ref[:, 0:LANES]
        for h in range(1, H):
            acc = acc + x_ref[:, h * LANES:(h + 1) * LANES]
        o_ref[...] = acc.astype(o_ref.dtype)

    return pl.pallas_call(
        body, out_shape=jax.ShapeDtypeStruct((S, LANES), out_dtype), grid=(S // tr,),
        in_specs=[pl.BlockSpec((tr, H * LANES), lambda i: (i, 0))],
        out_specs=pl.BlockSpec((tr, LANES), lambda i: (i, 0)), name=name,
        compiler_params=_cparams("parallel"))(x)


def _mesh_pos():
    return lax.axis_index("x"), lax.axis_index("y"), lax.axis_index("c")


def _peer(pos, k):
    x, y, c = pos
    px = 1 - x if k & 4 else x
    py = 1 - y if k & 2 else y
    pc = 1 - c if k & 1 else c
    return (px, py, pc), 4 * px + 2 * py + pc


def exchange(name, tensors):
    nt = len(tensors)
    flat_in, counts = [], []
    out_shape = []
    for mode, srcs in tensors:
        counts.append(len(srcs))
        flat_in += list(srcs)
        rc = srcs[0].shape[-2:]
        out_shape.append(jax.ShapeDtypeStruct((len(srcs), N_DEV) + tuple(rc), srcs[0].dtype))
    n_in = len(flat_in)

    def body(*refs):
        ins = refs[:n_in]
        outs = refs[n_in:n_in + nt]
        send_sems, recv_sems, local_sems = refs[n_in + nt:]
        pos = _mesh_pos()
        me = 4 * pos[0] + 2 * pos[1] + pos[2]
        srcs_of, base = [], 0
        for t in range(nt):
            srcs_of.append(ins[base:base + counts[t]])
            base += counts[t]

        def src_view(t, l, slot):
            ref = srcs_of[t][l]
            return ref if tensors[t][0] == "gather" else ref.at[slot]

        def all_layers(t, slot):
            return outs[t].at[pl.ds(0, counts[t]), slot]

        for t in range(nt):
            for l in range(counts[t]):
                pltpu.make_async_copy(src_view(t, l, me), outs[t].at[l, me], local_sems.at[t]).start()
        for t in range(nt):
            for k in range(1, N_DEV):
                peer, pid = _peer(pos, k)
                for l in range(counts[t]):
                    pltpu.make_async_remote_copy(
                        src_ref=src_view(t, l, pid), dst_ref=outs[t].at[l, me],
                        send_sem=send_sems.at[t, k - 1], recv_sem=recv_sems.at[t, k - 1],
                        device_id=peer, device_id_type=pl.DeviceIdType.MESH).start()
        for t in range(nt):
            for k in range(1, N_DEV):
                peer, pid = _peer(pos, k)
                pltpu.make_async_remote_copy(
                    src_ref=all_layers(t, pid), dst_ref=all_layers(t, pid),
                    send_sem=send_sems.at[t, k - 1], recv_sem=recv_sems.at[t, k - 1],
                    device_id=peer, device_id_type=pl.DeviceIdType.MESH).wait()
        for t in range(nt):
            pltpu.make_async_copy(all_layers(t, me), all_layers(t, me), local_sems.at[t]).wait()

    any_spec = pl.BlockSpec(memory_space=pl.ANY)
    return pl.pallas_call(
        body, out_shape=tuple(out_shape), in_specs=[any_spec] * n_in, out_specs=tuple([any_spec] * nt),
        scratch_shapes=[pltpu.SemaphoreType.DMA((nt, N_DEV - 1)), pltpu.SemaphoreType.DMA((nt, N_DEV - 1)),
                        pltpu.SemaphoreType.DMA((nt,))],
        name=name)(*flat_in)


HBM_SPEC = pl.BlockSpec(memory_space=pltpu.HBM)
SEM_SPEC = pl.BlockSpec(memory_space=pltpu.SEMAPHORE)
DATAFLOW = pltpu.SideEffectType.DATAFLOW_SIDE_EFFECTING


def _hbm(a):
    return pltpu.with_memory_space_constraint(a, pltpu.HBM)


def landing_zones(mode, srcs):
    pos = _mesh_pos()
    me = 4 * pos[0] + 2 * pos[1] + pos[2]
    lands = []
    for s in srcs:
        R, C = s.shape[-2:]
        own = s[None] if mode == "gather" else lax.dynamic_slice(s, (me, 0, 0), (1, R, C))
        lands.append(lax.dynamic_update_slice(lax.empty((N_DEV, R, C), s.dtype), own, (me, 0, 0)))
    return lands


def exchange_start(name, mode, srcs, lands, after=None):
    n = len(srcs)
    extra = [] if after is None else [after]

    def body(*refs):
        src_refs, land_refs = refs[:n], refs[n:2 * n]
        send_sems, recv_sems = refs[2 * n + len(extra)], refs[2 * n + len(extra) + 1]
        token = refs[-1]
        pos = _mesh_pos()
        me = 4 * pos[0] + 2 * pos[1] + pos[2]
        for t in range(n):
            for k in range(1, N_DEV):
                peer, pid = _peer(pos, k)
                src = src_refs[t] if mode == "gather" else src_refs[t].at[pid]
                pltpu.make_async_remote_copy(
                    src_ref=src, dst_ref=land_refs[t].at[me], send_sem=send_sems.at[t], recv_sem=recv_sems.at[t],
                    device_id=peer, device_id_type=pl.DeviceIdType.MESH).start()
        token[...] = jnp.zeros_like(token)

    thru = [pltpu.HBM(a.shape, a.dtype) for a in list(srcs) + list(lands)]
    out_shape = (pltpu.SemaphoreType.DMA((n,)), pltpu.SemaphoreType.DMA((n,)), *thru,
                 jax.ShapeDtypeStruct((8, LANES), F32))
    res = pl.pallas_call(
        body, out_shape=out_shape, in_specs=[HBM_SPEC] * (2 * n) + [pl.BlockSpec(memory_space=pl.ANY)] * len(extra),
        out_specs=(SEM_SPEC, SEM_SPEC, *([HBM_SPEC] * (2 * n)), pl.BlockSpec(memory_space=pltpu.VMEM)),
        input_output_aliases={i: 2 + i for i in range(2 * n)}, name=name,
        compiler_params=pltpu.CompilerParams(has_side_effects=DATAFLOW))(
            *[_hbm(a) for a in list(srcs) + list(lands)], *extra)
    return res[0], res[1], list(res[2:2 + n]), list(res[2 + n:2 + 2 * n]), res[-1]


def exchange_wait(name, send_sems, recv_sems, srcs, lands, after):
    n = len(srcs)

    def body(*refs):
        land_refs = refs[n:2 * n]
        s_sems, r_sems = refs[2 * n], refs[2 * n + 1]
        pos = _mesh_pos()
        for t in range(n):
            seven = land_refs[t].at[pl.ds(0, N_DEV - 1)]
            cp = pltpu.make_async_remote_copy(
                src_ref=seven, dst_ref=seven, send_sem=s_sems.at[t], recv_sem=r_sems.at[t],
                device_id=pos, device_id_type=pl.DeviceIdType.MESH)
            cp.wait_send()
            cp.wait_recv()

    arrs = list(srcs) + list(lands)
    afters = list(after) if isinstance(after, (list, tuple)) else [after]
    res = pl.pallas_call(
        body, out_shape=tuple(pltpu.HBM(a.shape, a.dtype) for a in arrs),
        in_specs=[HBM_SPEC] * (2 * n) + [SEM_SPEC, SEM_SPEC] + [pl.BlockSpec(memory_space=pl.ANY)] * len(afters),
        out_specs=tuple([HBM_SPEC] * (2 * n)), input_output_aliases={i: i for i in range(2 * n)}, name=name,
        compiler_params=pltpu.CompilerParams(has_side_effects=DATAFLOW))(*arrs, send_sems, recv_sems, *afters)
    return list(res[n:])


def reduce_parts(name, parts):
    n, R, C = parts.shape
    tr = _pick(R, max(8, (1 << 20) // (C * 4) // 8 * 8), 8)

    def body(p_ref, o_ref):
        acc = p_ref[0].astype(F32)
        for s in range(1, n):
            acc = acc + p_ref[s].astype(F32)
        o_ref[...] = acc

    return pl.pallas_call(
        body, out_shape=jax.ShapeDtypeStruct((R, C), F32), grid=(R // tr,),
        in_specs=[pl.BlockSpec((n, tr, C), lambda i: (0, i, 0))],
        out_specs=pl.BlockSpec((tr, C), lambda i: (i, 0)), name=name,
        compiler_params=_cparams("parallel"))(parts)


def adamw(name, w, m, v, parts, first=0, prev=None):
    L, R, C = w.shape
    nl = len(parts)
    n = parts[0].shape[0]
    tr = _pick(R, max(8, (1 << 19) // (C * 4) // 8 * 8), 8)
    n_prev = 0 if prev is None else 4

    def body(*refs):
        w_ref, m_ref, v_ref = refs[:3]
        p_refs = refs[3:3 + nl]
        g_ref, d_ref, nm_ref, nv_ref = refs[3 + nl + n_prev:]

        def update(p_ref):
            g = p_ref[0].astype(F32)
            for s in range(1, n):
                g = g + p_ref[s].astype(F32)
            wv = w_ref[...]
            mn = ADAM_B1 * m_ref[...] + (1.0 - ADAM_B1) * g
            vn = ADAM_B2 * v_ref[...] + (1.0 - ADAM_B2) * jnp.square(g)
            m_hat = mn / (1.0 - ADAM_B1 ** ADAM_STEP)
            v_hat = vn / (1.0 - ADAM_B2 ** ADAM_STEP)
            g_ref[...] = g
            d_ref[...] = -ADAM_LR * (m_hat / (jnp.sqrt(v_hat) + ADAM_EPS) + ADAM_WD * wv)
            nm_ref[...] = mn
            nv_ref[...] = vn

        for k in range(nl):
            pl.when(pl.program_id(0) == k)(functools.partial(update, p_refs[k]))

    blk = pl.BlockSpec((None, tr, C), lambda l, i: (first + l, i, 0))
    pspecs = [pl.BlockSpec((n, tr, C), lambda l, i, k=k: (0, jnp.where(l == k, i, 0), 0)) for k in range(nl)]
    sh = jax.ShapeDtypeStruct((L, R, C), F32)
    prev_args = [] if prev is None else list(prev)
    return pl.pallas_call(
        body, out_shape=(sh, sh, sh, sh), grid=(nl, R // tr),
        in_specs=[blk, blk, blk] + pspecs + [pl.BlockSpec(memory_space=pl.ANY)] * n_prev,
        out_specs=(blk, blk, blk, blk), input_output_aliases={3 + nl + q: q for q in range(n_prev)}, name=name,
        compiler_params=_cparams("arbitrary", "arbitrary"))(w, m, v, *parts, *prev_args)


def _cols_from_blocks(g):
    n, R, c = g.shape
    return g.transpose(1, 0, 2).reshape(R, n * c)


def _cols_to_blocks(w):
    R, C = w.shape
    return w.reshape(R, N_DEV, C // N_DEV).transpose(1, 0, 2)


def _uq_permute(w):
    lead = w.shape[:-1]
    w4 = w.reshape(lead + (MLA_HEADS, MLA_NOPE + MLA_ROPE))
    nope = w4[..., :MLA_NOPE].reshape(lead + (MLA_HEADS * MLA_NOPE,))
    rope = jnp.pad(w4[..., MLA_NOPE:], [(0, 0)] * (w4.ndim - 1) + [(0, LANES - MLA_ROPE)])
    return jnp.concatenate([nope, rope.reshape(lead + (MLA_HEADS * LANES,))], axis=-1)


def _uq_unpermute(w):
    lead = w.shape[:-1]
    n = MLA_HEADS * MLA_NOPE
    nope = w[..., :n].reshape(lead + (MLA_HEADS, MLA_NOPE))
    rope = w[..., n:].reshape(lead + (MLA_HEADS, LANES))[..., :MLA_ROPE]
    return jnp.concatenate([nope, rope], axis=-1).reshape(lead + (MLA_HEADS * (MLA_NOPE + MLA_ROPE),))


def _ukv_permute(w):
    lead = w.shape[:-1]
    w4 = w.reshape(lead + (MLA_HEADS, 2, MLA_NOPE))
    return jnp.swapaxes(w4, -3, -2).reshape(lead + (2 * MLA_HEADS * MLA_NOPE,))


def _ukv_unpermute(w):
    lead = w.shape[:-1]
    w4 = w.reshape(lead + (2, MLA_HEADS, MLA_NOPE))
    return jnp.swapaxes(w4, -3, -2).reshape(lead + (2 * MLA_HEADS * MLA_NOPE,))


SMALL = ["norm1_g", "mla_q_norm_g", "mla_kv_norm_g", "fox_b_f", "norm2_g", "ffn_conv_b", "final_norm_g"]
SMALL_TILE = 8 * LANES


def _pack_small(d):
    flat = jnp.concatenate([d[n].reshape(-1).astype(F32) for n in SMALL])
    pad = -flat.shape[0] % SMALL_TILE
    return jnp.pad(flat, (0, pad)).reshape(-1, LANES)


def _unpack_small(packed, like):
    flat = packed.reshape(-1)
    out, o = {}, 0
    for n in SMALL:
        sz = int(np.prod(like[n].shape))
        out[n] = flat[o:o + sz].reshape(like[n].shape)
        o += sz
    return out


WEIGHTS = ["norm1_g", "w_in", "mla_q_norm_g", "mla_kv_norm_g", "mla_w_uq", "mla_w_ukv", "fox_b_f", "w_br_fox",
           "w_br_mla", "w_br_ret", "w_out", "norm2_g", "ffn_w_up", "ffn_w_gate", "ffn_conv_w", "ffn_conv_b",
           "ffn_w_down", "final_norm_g"]
EARLY = ["w_in", "mla_w_uq", "mla_w_ukv"]
LATE = ["w_br_fox", "w_br_mla", "w_br_ret", "w_out", "ffn_w_up", "ffn_w_gate", "ffn_conv_w", "ffn_w_down"]
FFN = ["ffn_w_up", "ffn_w_gate", "ffn_conv_w", "ffn_w_down"]
TRANSPOSED = ("ffn_w_up", "ffn_w_gate")
BIG = EARLY + LATE
X_EARLY = ["w_in_mix", "mla_w_uq", "mla_w_ukv"]
X_LATE = ["w_in_gates"] + LATE
X_MID = ["w_in_gates", "w_out", "w_br_fox", "w_br_mla", "w_br_ret"]
X_REST = ["mla_w_uq", "mla_w_ukv", "w_in_mix"]


def kernel(x, norm1_g, w_in, mla_q_norm_g, mla_kv_norm_g, mla_w_uq, mla_w_ukv, fox_b_f, w_br_fox, w_br_mla, w_br_ret, w_out, norm2_g, ffn_w_up, ffn_w_gate, ffn_conv_w, ffn_conv_b, ffn_w_down, final_norm_g, loss_target, m_norm1_g, m_w_in, m_mla_q_norm_g, m_mla_kv_norm_g, m_mla_w_uq, m_mla_w_ukv, m_fox_b_f, m_w_br_fox, m_w_br_mla, m_w_br_ret, m_w_out, m_norm2_g, m_ffn_w_up, m_ffn_w_gate, m_ffn_conv_w, m_ffn_conv_b, m_ffn_w_down, m_final_norm_g, v_norm1_g, v_w_in, v_mla_q_norm_g, v_mla_kv_norm_g, v_mla_w_uq, v_mla_w_ukv, v_fox_b_f, v_w_br_fox, v_w_br_mla, v_w_br_ret, v_w_out, v_norm2_g, v_ffn_w_up, v_ffn_w_gate, v_ffn_conv_w, v_ffn_conv_b, v_ffn_w_down, v_final_norm_g):
    env = dict(locals())
    W = {n: env[n] for n in WEIGHTS}
    Mo = {n: env["m_" + n] for n in WEIGHTS}
    Vo = {n: env["v_" + n] for n in WEIGHTS}
    S, D = x.shape[1], x.shape[2]
    L = w_in.shape[0]
    lay = InLayout(D)
    NP = lay.total
    f = ffn_w_up.shape[-1]
    xs = x.reshape(S, D)
    tgt = loss_target.reshape(S, D)

    local = {n: W[n].astype(BF16) for n in BIG if n != "w_in"}
    local["w_in_gates"], local["w_in_mix"] = lay.permute(W["w_in"].astype(BF16))
    pending = {}
    token = None
    for l in range(L):
        for grp, names in (("a", X_EARLY), ("b", X_LATE)):
            srcs = [local[n][l] for n in names]
            *flight, token = exchange_start(f"gather_start_{l}{grp}", "gather", srcs, landing_zones("gather", srcs),
                                            token)
            pending[l, grp] = flight
    gather_token = token
    cbias_all = ffn_conv_b.reshape(L, 1, N_DEV, 1, f)

    def early_weights(l, after):
        g = dict(zip(X_EARLY, exchange_wait(f"gather_wait_{l}a", *pending[l, "a"], after)))
        return dict(Win=g["w_in_mix"].reshape(1, D, NP), Wuq=_uq_permute(_cols_from_blocks(g["mla_w_uq"])),
                    Wukv=_ukv_permute(_cols_from_blocks(g["mla_w_ukv"])))

    def late_weights(l, after):
        g = dict(zip(X_LATE, exchange_wait(f"gather_wait_{l}b", *pending[l, "b"], after)))
        return dict(
            Wgates=g["w_in_gates"].reshape(1, D, 3 * D),
            Wout=g["w_out"].reshape(1, D, D), Wbf=_cols_from_blocks(g["w_br_fox"]),
            Wbm=_cols_from_blocks(g["w_br_mla"]), Wbr=_cols_from_blocks(g["w_br_ret"]), Wup=g["ffn_w_up"][None],
            Wgate=g["ffn_w_gate"][None], Wdown=g["ffn_w_down"][None], Wconv=g["ffn_conv_w"].astype(F32)[None],
            cbias=cbias_all[l])

    tab64 = rope_tables(S, MLA_ROPE)
    tab128 = rope_tables(S, RET_DK)
    fox_scale = FOX_DH ** -0.5
    mla_scale = (MLA_NOPE + MLA_ROPE) ** -0.5
    ret_kscale = RET_DK ** -0.5
    R = S // LANES
    AB = _attn_block(S)
    NOPE_W = MLA_HEADS * MLA_NOPE

    def vec(a):
        return a.reshape(1, -1)

    saved = []
    xc = xs
    for l in range(L):
        Wl = early_weights(l, gather_token if l == 0 else xc)
        Win, Wuq, Wukv = Wl["Win"], Wl["Wuq"], Wl["Wukv"]
        s = {"x": xc, "W": Wl}
        h1 = rms_fwd("norm1", xc, 0, D, vec(norm1_g[l]), BF16)
        P = mm_nn("in_proj", h1, Win, F32, b_lead=0)
        s.update(h1=h1, P=P)
        ff_off = lay.off["ff"]
        ft = P[:, ff_off:ff_off + FOX_HEADS].T.reshape(FOX_HEADS, R, LANES)
        bfl = jnp.broadcast_to(fox_b_f[l].reshape(FOX_HEADS, 1, 1), (FOX_HEADS, 1, LANES))
        kbias = fox_gate_fwd("fox_gate", ft, bfl).reshape(FOX_HEADS, S // AB, 1, AB)
        o_fox, lse_fox = attn_fwd("fox_attn", "fox", P, lay.cb("fq", LANES), P, lay.cb("fk", LANES),
                                  P, lay.cb("fv", LANES), FOX_HEADS, FOX_DH, fox_scale, kbias=kbias)
        s.update(ft=ft, bfl=bfl, kbias=kbias, o_fox=o_fox, lse_fox=lse_fox)
        cqn = rms_fwd("mla_q_norm", P, lay.cb("mq", MLA_Q_LORA), MLA_Q_LORA, vec(mla_q_norm_g[l]), BF16)
        qall = mm_nn("mla_uq", cqn, Wuq, F32)
        ckvn = rms_fwd("mla_kv_norm", P, lay.cb("mkv", MLA_KV_LORA), MLA_KV_LORA, vec(mla_kv_norm_g[l]), BF16)
        kvall = mm_nn("mla_ukv", ckvn, Wukv, F32)
        qrope = rope_apply("mla_q_rope", qall, NOPE_W // LANES, MLA_HEADS, tab64, 1.0, F32)
        krope = rope_apply("mla_k_rope", P, lay.cb("mkr", LANES), 1, tab64, 1.0, F32)
        o_mla, lse_mla = attn_fwd("mla_attn", "mla", qall, 0, kvall, 0, kvall, NOPE_W // MLA_V, MLA_HEADS, MLA_V,
                                  mla_scale, q2=qrope, q2_cb=0, k2=krope)
        s.update(cqn=cqn, qall=qall, ckvn=ckvn, kvall=kvall, qrope=qrope, krope=krope, o_mla=o_mla,
                 lse_mla=lse_mla)
        rq = rope_apply("ret_q_rope", P, lay.cb("rq", LANES), RET_HEADS, tab128, 1.0, F32)
        rk = rope_apply("ret_k_rope", P, lay.cb("rk", LANES), RET_HEADS, tab128, ret_kscale, F32)
        o_ret, _ = attn_fwd("ret_attn", "ret", rq, 0, rk, 0, P, lay.cb("rv", RET_DV), RET_HEADS, RET_DV, 1.0)
        c_ret = ret_out_fwd("ret_out", o_ret, P, lay.cb("rg", RET_DV), BF16)
        s.update(rq=rq, rk=rk, o_ret=o_ret, c_ret=c_ret)
        Wl.update(late_weights(l, (o_fox, o_mla, c_ret)))
        Wout, Wbf, Wbm, Wbr = Wl["Wout"], Wl["Wbf"], Wl["Wbm"], Wl["Wbr"]
        Wup, Wgate, Wdown, Wconv, cbias = (Wl[k] for k in ("Wup", "Wgate", "Wdown", "Wconv", "cbias"))
        A = mm_nn("br_fox", o_fox, Wbf, F32)
        Bm = mm_nn("br_mla", o_mla, Wbm, F32)
        C = mm_nn("br_ret", c_ret, Wbr, F32)
        Pg = mm_nn("gate_proj", h1, Wl["Wgates"], F32, b_lead=0)
        s["Pg"] = Pg
        merged = merge_fwd("merge", Pg, 0, A, Bm, C, BF16)
        x2 = mm_nn("out_proj", merged, Wout, F32, b_lead=0, res=xc)
        s.update(A=A, Bm=Bm, C=C, merged=merged, x2=x2)
        h2 = rms_fwd("norm2", x2, 0, D, vec(norm2_g[l]), BF16)
        u = ffn_up("ffn_up", h2, Wup, 0, F32)
        gt = ffn_up("ffn_gate", h2, Wgate, 0, F32)
        act = ffn_act_fwd("ffn_act", u, gt, Wconv, cbias, 0, BF16)
        xc = ffn_down("ffn_down", act, Wdown, 0, x2, F32)
        s.update(h2=h2, u=u, gt=gt, act=act)
        saved.append(s)

    loss_tile, dx, dgf = loss_head("loss_head", xc, vec(final_norm_g), tgt)
    loss = lax.psum(loss_tile[0, 0], ("x", "y", "c"))

    gbig = {n: [None] * L for n in BIG + ["w_in_gates", "w_in_mix"]}
    gsmall = {n: [None] * L for n in SMALL if n != "final_norm_g"}
    scattering = {}
    scatter_token = None

    def start_scatter(name, names, l):
        srcs = [gbig[n][l] for n in names]
        *flight, tok = exchange_start(name, "scatter", srcs, landing_zones("scatter", srcs))
        return flight, tok

    for l in reversed(range(L)):
        s = saved[l]
        P = s["P"]
        Wl = s["W"]
        Win, Wout, Wuq, Wukv, Wbf, Wbm, Wbr = (Wl[k] for k in ("Win", "Wout", "Wuq", "Wukv", "Wbf", "Wbm", "Wbr"))
        Wup, Wgate, Wdown, Wconv, cbias = (Wl[k] for k in ("Wup", "Wgate", "Wdown", "Wconv", "cbias"))
        dxb = (dx if scatter_token is None else dx + scatter_token[0, 0]).astype(BF16)
        dact = ffn_down_bwd_act("ffn_down_da", dxb, Wdown, 0, BF16)
        gbig["ffn_w_down"][l] = ffn_down_bwd_w("ffn_down_dw", s["act"], dxb, BF16)
        g, dgt = ffn_act_bwd_point("ffn_act_bwd", s["u"], s["gt"], Wconv, cbias, 0, dact)
        du, dcw, dcb = ffn_act_bwd_conv("ffn_conv_bwd", s["u"], g, Wconv, 0)
        gbig["ffn_conv_w"][l] = dcw.astype(BF16)
        gsmall["ffn_conv_b"][l] = dcb.reshape(-1)
        gbig["ffn_w_up"][l] = ffn_down_bwd_w("ffn_up_dw", du, s["h2"], BF16)
        gbig["ffn_w_gate"][l] = ffn_down_bwd_w("ffn_gate_dw", dgt, s["h2"], BF16)
        dh2 = ffn_up_bwd_h("ffn_up_dh", du, Wup, 0, None, F32)
        dh2 = ffn_up_bwd_h("ffn_gate_dh", dgt, Wgate, 0, dh2, BF16)
        dx2, dg2 = rms_bwd("norm2_bwd", s["x2"], 0, D, vec(norm2_g[l]), dh2, F32, res=dx)
        gsmall["norm2_g"][l] = dg2.reshape(-1)
        scattering[l, "ffn"], scatter_token = start_scatter(f"scatter_start_{l}ffn", FFN, l)
        dx2b = (dx2 + scatter_token[0, 0]).astype(BF16)
        dmerged = mm_nt("out_proj_dm", dx2b, Wout, BF16, b_lead=0)
        gbig["w_out"][l] = mm_tn("out_proj_dw", s["merged"], dx2b, BF16).reshape(N_DEV, D // N_DEV, D)
        dgates, dA, dB, dC = merge_bwd("merge_bwd", s["Pg"], 0, s["A"], s["Bm"], s["C"], dmerged)
        gbig["w_br_fox"][l] = _cols_to_blocks(mm_tn("br_fox_dw", s["o_fox"], dA, BF16))
        gbig["w_br_mla"][l] = _cols_to_blocks(mm_tn("br_mla_dw", s["o_mla"], dB, BF16))
        gbig["w_br_ret"][l] = _cols_to_blocks(mm_tn("br_ret_dw", s["c_ret"], dC, BF16))
        gbig["w_in_gates"][l] = mm_tn("gate_proj_dw", s["h1"], dgates, BF16).reshape(N_DEV, D // N_DEV, 3 * D)
        scattering[l, "mid"], scatter_token = start_scatter(f"scatter_start_{l}mid", X_MID, l)
        dh1_gates = mm_nt("gate_proj_dh", dgates, Wl["Wgates"], F32, b_lead=0)
        do_fox = mm_nt("br_fox_do", dA + scatter_token[0, 0].astype(BF16), Wbf, F32)
        do_mla = mm_nt("br_mla_do", dB, Wbm, F32)
        dc_ret = mm_nt("br_ret_do", dC, Wbr, BF16)
        do_ret, drg = ret_out_bwd("ret_out_bwd", s["o_ret"], P, lay.cb("rg", RET_DV), dc_ret)
        rb = attn_bwd("ret_attn_bwd", "ret", s["rq"], 0, s["rk"], 0, P, lay.cb("rv", RET_DV), RET_HEADS, RET_DV,
                      1.0, do_ret)
        drq = rope_apply("ret_q_rope_bwd", rb["dq1"], 0, RET_HEADS, tab128, 1.0, BF16, transpose=True)
        drk = rope_apply("ret_k_rope_bwd", rb["dk1"], 0, RET_HEADS, tab128, ret_kscale, BF16, transpose=True)
        drv = rb["dv"].astype(BF16)
        mb = attn_bwd("mla_attn_bwd", "mla", s["qall"], 0, s["kvall"], 0, s["kvall"], NOPE_W // MLA_V, MLA_HEADS,
                      MLA_V, mla_scale, do_mla, o=s["o_mla"], lse=s["lse_mla"], q2=s["qrope"], q2_cb=0,
                      k2=s["krope"])
        dqrope = rope_apply("mla_q_rope_bwd", mb["dq2"], 0, MLA_HEADS, tab64, 1.0, BF16, transpose=True)
        dkr_sum = head_sum("mla_k_rope_sum", mb["dk2h"], MLA_HEADS, F32)
        dmkr = rope_apply("mla_k_rope_bwd", dkr_sum, 0, 1, tab64, 1.0, BF16, transpose=True)
        dqall = jnp.concatenate([mb["dq1"].astype(BF16), dqrope], axis=1)
        dkvall = jnp.concatenate([mb["dk1"].astype(BF16), mb["dv"].astype(BF16)], axis=1)
        dcqn = mm_nt("mla_uq_dx", dqall, Wuq, F32)
        dckvn = mm_nt("mla_ukv_dx", dkvall, Wukv, F32)
        guq = _uq_unpermute(mm_tn("mla_uq_dw", s["cqn"], dqall, BF16))
        gukv = _ukv_unpermute(mm_tn("mla_ukv_dw", s["ckvn"], dkvall, BF16))
        gbig["mla_w_uq"][l] = _cols_to_blocks(guq)
        gbig["mla_w_ukv"][l] = _cols_to_blocks(gukv)
        dmq, dgq = rms_bwd("mla_q_norm_bwd", P, lay.cb("mq", MLA_Q_LORA), MLA_Q_LORA, vec(mla_q_norm_g[l]),
                           dcqn, BF16)
        dmkv, dgkv = rms_bwd("mla_kv_norm_bwd", P, lay.cb("mkv", MLA_KV_LORA), MLA_KV_LORA,
                             vec(mla_kv_norm_g[l]), dckvn, BF16)
        gsmall["mla_q_norm_g"][l] = dgq.reshape(-1)
        gsmall["mla_kv_norm_g"][l] = dgkv.reshape(-1)
        fb = attn_bwd("fox_attn_bwd", "fox", P, lay.cb("fq", LANES), P, lay.cb("fk", LANES), P,
                      lay.cb("fv", LANES), FOX_HEADS, FOX_DH, fox_scale, do_fox, o=s["o_fox"], lse=s["lse_fox"],
                      kbias=s["kbias"])
        drow = fb["drow"].reshape(S, FOX_HEADS, LANES)[:, :, 0].T.reshape(FOX_HEADS, R, LANES)
        dft, dbf = fox_gate_bwd("fox_gate_bwd", s["ft"], s["bfl"], fb["dkb"].reshape(FOX_HEADS, R, LANES) - drow)
        gsmall["fox_b_f"][l] = dbf[:, 0, 0]
        dff = jnp.pad(dft.reshape(FOX_HEADS, S).T, ((0, 0), (0, LANES - FOX_HEADS))).astype(BF16)
        segs = dict(rv=drv, rg=drg, mq=dmq, rq=drq, rk=drk, mkv=dmkv, fq=fb["dq1"].astype(BF16),
                    fk=fb["dk1"].astype(BF16), fv=fb["dv"].astype(BF16), mkr=dmkr, ff=dff)
        dP = jnp.concatenate([segs[n] for n in lay.order], axis=1)
        gbig["w_in_mix"][l] = mm_tn("in_proj_dw", s["h1"], dP, BF16).reshape(N_DEV, D // N_DEV, NP)
        scattering[l, "rest"], scatter_token = start_scatter(f"scatter_start_{l}rest", X_REST, l)
        dh1 = mm_nt("in_proj_dh", dP, Win, BF16, b_lead=0, res=dh1_gates)
        dx, dg1 = rms_bwd("norm1_bwd", s["x"], 0, D, vec(norm1_g[l]) + scatter_token[0:1, 0:1], dh1, F32, res=dx2)
        gsmall["norm1_g"][l] = dg1.reshape(-1)

    small_like = {n: W[n] for n in SMALL}
    small_part = {n: jnp.stack(gsmall[n]) for n in gsmall}
    small_part["final_norm_g"] = dgf.reshape(-1)
    small_recv = exchange("gather_small_grads", [("gather", [_pack_small(small_part)])])[0]
    ps = adamw("adamw_small", _pack_small(small_like)[None], _pack_small({n: Mo[n] for n in SMALL})[None],
               _pack_small({n: Vo[n] for n in SMALL})[None], [small_recv[0]])

    def received(l, after):
        r = {}
        for grp, names in (("ffn", FFN), ("mid", X_MID), ("rest", X_REST)):
            r.update(zip(names, exchange_wait(f"scatter_wait_{l}{grp}", *scattering[l, grp], after)))
        r["w_in"] = lay.unpermute(reduce_parts("w_in_gates_grad_sum", r["w_in_gates"]),
                                  reduce_parts("w_in_mix_grad_sum", r["w_in_mix"]))[None]
        return r

    def oriented(n, a):
        return jnp.swapaxes(a, 1, 2) if n in TRANSPOSED else a

    out = {}
    if L > 1:
        recv = [received(l, dx) for l in range(1, L)]
        for n in BIG:
            out[n] = adamw("adamw_" + n, oriented(n, W[n]), oriented(n, Mo[n]), oriented(n, Vo[n]),
                           [r[n] for r in recv], first=1)
    recv0 = received(0, out[BIG[-1]][0] if L > 1 else dx)
    for n in BIG:
        res = adamw("adamw0_" + n, oriented(n, W[n]), oriented(n, Mo[n]), oriented(n, Vo[n]), [recv0[n]],
                    first=0, prev=out.get(n))
        out[n] = tuple(oriented(n, a) for a in res)
    small_out = [_unpack_small(a[0], small_like) for a in ps]
    for n in SMALL:
        out[n] = tuple(so[n] for so in small_out)

    grads = [out[n][0] for n in WEIGHTS]
    deltas = [out[n][1] for n in WEIGHTS]
    new_m = [out[n][2] for n in WEIGHTS]
    new_v = [out[n][3] for n in WEIGHTS]
    return (loss, dx.reshape(1, S, D), *grads, *deltas, *new_m, *new_v)
```
